```python
import functools
import jax
import jax.numpy as jnp
from jax import lax
import numpy as np

D_MODEL = 1024
BATCH = 4
SEQ = 4096
DEPTH = 2
DEC_BATCH = 8
DEC_SEQ = 32
PAST_LEN = 1024

CHUNK = 64
N_MEM = 256
EPS = 1e-6
ROPE_THETA = 500000.0

A_HEAD_DIM = 64
A_HEADS = (D_MODEL // 2) // A_HEAD_DIM
A_KV_HEADS = 2
A_GROUP = A_HEADS // A_KV_HEADS
A_WIDTH = A_HEADS * A_HEAD_DIM
ROT_A = A_HEAD_DIM // 4
IDX_HEADS = 4
IDX_DIM = 64
ROT_IDX = IDX_DIM // 4
TOPK_MAX = 256
Q_BLOCK = 128

B_KEY_DIM = 64
B_VAL_DIM = 64
B_HEADS = (D_MODEL // 2) // B_VAL_DIM
B_WIDTH = B_HEADS * B_VAL_DIM
B_CHUNK = 16

MIX_WIDTH = A_WIDTH + B_WIDTH

MEM_HEADS = 4
MEM_HEAD_DIM = D_MODEL // MEM_HEADS

D_FF = ((8 * D_MODEL // 3 + 127) // 128) * 128
N_EXPERTS = 8
TOP_K_EXPERTS = 2
N_DENSE = (DEPTH + 1) // 2
N_MOE = DEPTH // 2

IN_SPLITS = (A_WIDTH, A_KV_HEADS * A_HEAD_DIM, A_KV_HEADS * A_HEAD_DIM, IDX_HEADS * IDX_DIM, IDX_DIM, IDX_HEADS,
             B_HEADS * B_KEY_DIM, B_HEADS * B_KEY_DIM, B_WIDTH, B_WIDTH)
IN_WIDTH = sum(IN_SPLITS)

kernel_name = 'hybrid_dsa_hgrn2_stream_step'


def rms_norm(x, g):
    xf = x.astype(jnp.float32)
    y = xf * lax.rsqrt(jnp.mean(xf * xf, axis=-1, keepdims=True) + EPS)
    return (y * g.astype(jnp.float32)).astype(x.dtype)


def rope_partial(x, pos, rot_dim):
    half = rot_dim // 2
    inv_freq = 1.0 / (ROPE_THETA ** (jnp.arange(half, dtype=jnp.float32) * (2.0 / rot_dim)))
    ang = pos.astype(jnp.float32)[:, None] * inv_freq[None, :]
    cos = jnp.cos(ang)[:, None, :]
    sin = jnp.sin(ang)[:, None, :]
    xr = x[..., :rot_dim].astype(jnp.float32)
    x1, x2 = xr[..., :half], xr[..., half:]
    rot = jnp.concatenate([x1 * cos - x2 * sin, x2 * cos + x1 * sin], axis=-1).astype(x.dtype)
    return jnp.concatenate([rot, x[..., rot_dim:]], axis=-1)


def dsa_block(q, iq, iw, qpos, k_all, v_all, ik_all, kpos, topk):
    f32 = jnp.float32
    logits = jnp.einsum('bthd,bsd->bths', iq.astype(f32), ik_all.astype(f32)) * (IDX_DIM ** -0.5)
    score = jnp.einsum('bth,bths->bts', iw.astype(f32) * (IDX_HEADS ** -0.5), jax.nn.relu(logits))
    admissible = (kpos[None, :] // CHUNK) <= (qpos[:, None] // CHUNK)
    score = jnp.where(admissible[None], score, -jnp.inf)
    _, sel = lax.top_k(score, topk)
    valid = (kpos[sel] // CHUNK) <= (qpos[None, :, None] // CHUNK)
    gather = jax.vmap(lambda rows, idx: rows[idx])
    k_sel = gather(k_all, sel)
    v_sel = gather(v_all, sel)
    s = jnp.einsum('bthgd,btkhd->bthgk', q.astype(f32), k_sel.astype(f32)) * (A_HEAD_DIM ** -0.5)
    s = jnp.where(valid[:, :, None, None, :], s, -jnp.inf)
    p = jax.nn.softmax(s, axis=-1).astype(v_sel.dtype)
    o = jnp.einsum('bthgk,btkhd->bthgd', p, v_sel)
    return o.reshape(o.shape[0], o.shape[1], A_WIDTH)


def dsa_attention(q, iq, iw, qpos, k_all, v_all, ik_all, kpos, topk):
    bsz, t = q.shape[0], q.shape[1]
    if t > Q_BLOCK and t % Q_BLOCK == 0:
        nb = t // Q_BLOCK

        def blocks(a):
            return a.reshape((bsz, nb, Q_BLOCK) + a.shape[2:]).swapaxes(0, 1)

        def one(args):
            qb, iqb, iwb, pb = args
            return dsa_block(qb, iqb, iwb, pb, k_all, v_all, ik_all, kpos, topk)

        out = lax.map(one, (blocks(q), blocks(iq), blocks(iw), qpos.reshape(nb, Q_BLOCK)))
        return out.swapaxes(0, 1).reshape(bsz, t, A_WIDTH)
    return dsa_block(q, iq, iw, qpos, k_all, v_all, ik_all, kpos, topk)


def hgrn2_chunked(q, k, v, logf, s0):
    bsz, t, nh, dk = q.shape
    pad = (-t) % B_CHUNK
    if pad:
        pw = ((0, 0), (0, pad), (0, 0), (0, 0))
        q, k, v, logf = jnp.pad(q, pw), jnp.pad(k, pw), jnp.pad(v, pw), jnp.pad(logf, pw)
    n = (t + pad) // B_CHUNK

    def to_chunks(a):
        return a.reshape(bsz, n, B_CHUNK, nh, a.shape[-1]).transpose(1, 0, 3, 2, 4)

    causal = jnp.tril(jnp.ones((B_CHUNK, B_CHUNK), dtype=bool))

    def step(S, inp):
        qc, kc, vc, gc = inp
        b = jnp.cumsum(gc, axis=2)
        o_inter = jnp.einsum('bhtk,bhkv->bhtv', qc * jnp.exp(b), S)
        dec = jnp.where(causal[None, None, :, :, None], b[:, :, :, None, :] - b[:, :, None, :, :], -jnp.inf)
        A = jnp.einsum('bhtk,bhsk,bhtsk->bhts', qc, kc, jnp.exp(dec))
        o = o_inter + jnp.einsum('bhts,bhsv->bhtv', A, vc)
        b_last = b[:, :, -1:, :]
        S_new = jnp.exp(b_last[:, :, 0, :])[..., None] * S + jnp.einsum('bhsk,bhsv->bhkv', kc * jnp.exp(b_last - b), vc)
        return S_new, o

    S, o = lax.scan(step, s0, (to_chunks(q), to_chunks(k), to_chunks(v), to_chunks(logf)))
    o = o.transpose(1, 0, 3, 2, 4).reshape(bsz, n * B_CHUNK, nh, v.shape[-1])[:, :t]
    return o, S


def mixer_layer(x, pos, k_past, v_past, ik_past, s0, lb, norm_g, w_in, gnorm, w_out, topk):
    f32 = jnp.float32
    bsz, t, _ = x.shape
    h = rms_norm(x, norm_g) @ w_in
    aq, ak, av, iq, ik, iw, bq, bf, bi, bg = jnp.split(h, np.cumsum(IN_SPLITS)[:-1].tolist(), axis=-1)
    aq = rope_partial(aq.reshape(bsz, t, A_HEADS, A_HEAD_DIM), pos, ROT_A)
    ak = rope_partial(ak.reshape(bsz, t, A_KV_HEADS, A_HEAD_DIM), pos, ROT_A)
    av = av.reshape(bsz, t, A_KV_HEADS, A_HEAD_DIM)
    iq = rope_partial(iq.reshape(bsz, t, IDX_HEADS, IDX_DIM), pos, ROT_IDX)
    ik = rope_partial(ik.reshape(bsz, t, 1, IDX_DIM), pos, ROT_IDX)[:, :, 0]
    if k_past is None:
        k_all, v_all, ik_all, kpos = ak, av, ik, pos
    else:
        k_all = jnp.concatenate([k_past.astype(ak.dtype), ak], axis=1)
        v_all = jnp.concatenate([v_past.astype(av.dtype), av], axis=1)
        ik_all = jnp.concatenate([ik_past.astype(ik.dtype), ik], axis=1)
        kpos = jnp.arange(k_all.shape[1], dtype=jnp.int32)
    oa = dsa_attention(aq.reshape(bsz, t, A_KV_HEADS, A_GROUP, A_HEAD_DIM), iq, iw, pos,
                       k_all, v_all, ik_all, kpos, topk)
    lbh = lb.reshape(B_HEADS, B_KEY_DIM)
    f = lbh + (1.0 - lbh) * jax.nn.sigmoid(bf.reshape(bsz, t, B_HEADS, B_KEY_DIM).astype(f32))
    qb = jax.nn.silu(bq.reshape(bsz, t, B_HEADS, B_KEY_DIM).astype(f32)) * (B_KEY_DIM ** -0.5)
    vb = bi.reshape(bsz, t, B_HEADS, B_VAL_DIM).astype(f32)
    ob, s_new = hgrn2_chunked(qb, 1.0 - f, vb, jnp.log(f), s0.astype(f32))
    ob = rms_norm(ob, gnorm) * jax.nn.silu(bg.reshape(bsz, t, B_HEADS, B_VAL_DIM).astype(f32))
    mixed = jnp.concatenate([oa, ob.astype(x.dtype).reshape(bsz, t, B_WIDTH)], axis=-1)
    return x + mixed @ w_out, ak, av, ik, s_new


def memory_kv(mem, g, w_k, w_v):
    m = rms_norm(mem, g)
    bsz = mem.shape[0]
    k = (m @ w_k).reshape(bsz, N_MEM, MEM_HEADS, MEM_HEAD_DIM)
    v = (m @ w_v).reshape(bsz, N_MEM, MEM_HEADS, MEM_HEAD_DIM)
    return k, v


def memory_attend(xn, w_q, w_o, mk, mv):
    bsz, t, _ = xn.shape
    q = (xn @ w_q).reshape(bsz, t, MEM_HEADS, MEM_HEAD_DIM)
    s = jnp.einsum('bthd,bmhd->bhtm', q.astype(jnp.float32), mk.astype(jnp.float32)) * (MEM_HEAD_DIM ** -0.5)
    p = jax.nn.softmax(s, axis=-1).astype(mv.dtype)
    o = jnp.einsum('bhtm,bmhd->bthd', p, mv).astype(xn.dtype)
    return o.reshape(bsz, t, D_MODEL) @ w_o


def swiglu(x, w1, w3, w2):
    return (jax.nn.silu(x @ w1) * (x @ w3)) @ w2


def moe_swiglu(x, router, w1, w3, w2):
    logits = jnp.einsum('btd,de->bte', x, router).astype(jnp.float32)
    top_val, top_idx = lax.top_k(logits, TOP_K_EXPERTS)
    gates = jax.nn.softmax(top_val, axis=-1)
    combine = jnp.sum(jax.nn.one_hot(top_idx, N_EXPERTS, dtype=jnp.float32) * gates[..., None], axis=-2)
    out = jnp.zeros_like(x)
    for e in range(N_EXPERTS):
        out = out + combine[..., e:e + 1].astype(x.dtype) * swiglu(x, w1[e], w3[e], w2[e])
    return out


def setup_inputs(seed: int = 0) -> dict:
    key = jax.random.key(seed)
    ks = iter(jax.random.split(key, 32))
    f32 = jnp.float32

    def nrm(shape, scale=1.0):
        return jax.random.normal(next(ks), shape, f32) * scale

    def gain(shape):
        return 1.0 + 0.05 * jax.random.normal(next(ks), shape, f32)

    return {
        'x_prompt': nrm((BATCH, SEQ, D_MODEL)),
        'x_sample': nrm((DEC_BATCH, DEC_SEQ, D_MODEL)),
        'cache_k': nrm((DEPTH, DEC_BATCH, PAST_LEN, A_KV_HEADS, A_HEAD_DIM)),
        'cache_v': nrm((DEPTH, DEC_BATCH, PAST_LEN, A_KV_HEADS, A_HEAD_DIM)),
        'cache_idx_k': nrm((DEPTH, DEC_BATCH, PAST_LEN, IDX_DIM)),
        'state_hgrn': nrm((DEPTH, DEC_BATCH, B_HEADS, B_KEY_DIM, B_VAL_DIM), 0.5),
        'cache_mem_k': nrm((DEPTH, DEC_BATCH, N_MEM, MEM_HEADS, MEM_HEAD_DIM)),
        'cache_mem_v': nrm((DEPTH, DEC_BATCH, N_MEM, MEM_HEADS, MEM_HEAD_DIM)),
        'mem_prompt': nrm((BATCH, N_MEM, D_MODEL)),
        'norm_mix': gain((DEPTH, D_MODEL)),
        'w_in': nrm((DEPTH, D_MODEL, IN_WIDTH), D_MODEL ** -0.5),
        'hgrn_lb': nrm((DEPTH, B_HEADS * B_KEY_DIM)),
        'hgrn_gnorm': gain((DEPTH, B_VAL_DIM)),
        'w_out': nrm((DEPTH, MIX_WIDTH, D_MODEL), MIX_WIDTH ** -0.5),
        'norm_mem': gain((DEPTH, D_MODEL)),
        'norm_memkv': gain((DEPTH, D_MODEL)),
        'w_mq': nrm((DEPTH, D_MODEL, D_MODEL), D_MODEL ** -0.5),
        'w_mk': nrm((DEPTH, D_MODEL, D_MODEL), D_MODEL ** -0.5),
        'w_mv': nrm((DEPTH, D_MODEL, D_MODEL), D_MODEL ** -0.5),
        'w_mo': nrm((DEPTH, D_MODEL, D_MODEL), D_MODEL ** -0.5),
        'norm_ffn': gain((DEPTH, D_MODEL)),
        'ffn_w1': nrm((N_DENSE, D_MODEL, D_FF), D_MODEL ** -0.5),
        'ffn_w3': nrm((N_DENSE, D_MODEL, D_FF), D_MODEL ** -0.5),
        'ffn_w2': nrm((N_DENSE, D_FF, D_MODEL), D_FF ** -0.5),
        'moe_router': nrm((N_MOE, D_MODEL, N_EXPERTS), D_MODEL ** -0.5),
        'moe_w1': nrm((N_MOE, N_EXPERTS, D_MODEL, D_FF), D_MODEL ** -0.5),
        'moe_w3': nrm((N_MOE, N_EXPERTS, D_MODEL, D_FF), D_MODEL ** -0.5),
        'moe_w2': nrm((N_MOE, N_EXPERTS, D_FF, D_MODEL), D_FF ** -0.5),
        'norm_final': gain((D_MODEL,)),
    }


def reference(x_prompt, x_sample, cache_k, cache_v, cache_idx_k, state_hgrn, cache_mem_k, cache_mem_v, mem_prompt,
              norm_mix, w_in, hgrn_lb, hgrn_gnorm, w_out, norm_mem, norm_memkv, w_mq, w_mk, w_mv, w_mo,
              norm_ffn, ffn_w1, ffn_w3, ffn_w2, moe_router, moe_w1, moe_w3, moe_w2, norm_final):
    lb_all = jnp.cumsum(jax.nn.softmax(hgrn_lb.astype(jnp.float32), axis=0), axis=0)
    lb_all = lb_all - lb_all[:1]
    bp, tp, _ = x_prompt.shape
    bs, ts, _ = x_sample.shape
    past = cache_k.shape[2]
    pos_p = jnp.arange(tp, dtype=jnp.int32)
    pos_s = past + jnp.arange(ts, dtype=jnp.int32)
    topk_p = min(TOPK_MAX, tp // 4)
    topk_s = min(TOPK_MAX, (past + ts) // 4)
    s0_p = jnp.zeros((bp, B_HEADS, B_KEY_DIM, B_VAL_DIM), jnp.float32)
    xp, xs = x_prompt, x_sample
    kp_l, vp_l, ikp_l, sp_l, mkp_l, mvp_l = [], [], [], [], [], []
    ks_l, vs_l, iks_l, ss_l = [], [], [], []
    for l in range(DEPTH):
        mix = functools.partial(mixer_layer, lb=lb_all[l], norm_g=norm_mix[l], w_in=w_in[l],
                                gnorm=hgrn_gnorm[l], w_out=w_out[l])
        xp, kp, vp, ikp, sp = mix(xp, pos_p, None, None, None, s0_p, topk=topk_p)
        xs, ks_, vs_, iks, ss = mix(xs, pos_s, cache_k[l], cache_v[l], cache_idx_k[l], state_hgrn[l], topk=topk_s)
        mk_p, mv_p = memory_kv(mem_prompt, norm_memkv[l], w_mk[l], w_mv[l])
        xp = xp + memory_attend(rms_norm(xp, norm_mem[l]), w_mq[l], w_mo[l], mk_p, mv_p)
        xs = xs + memory_attend(rms_norm(xs, norm_mem[l]), w_mq[l], w_mo[l], cache_mem_k[l], cache_mem_v[l])
        if l % 2 == 0:
            ffn = functools.partial(swiglu, w1=ffn_w1[l // 2], w3=ffn_w3[l // 2], w2=ffn_w2[l // 2])
        else:
            ffn = functools.partial(moe_swiglu, router=moe_router[l // 2], w1=moe_w1[l // 2],
                                    w3=moe_w3[l // 2], w2=moe_w2[l // 2])
        xp = xp + ffn(rms_norm(xp, norm_ffn[l]))
        xs = xs + ffn(rms_norm(xs, norm_ffn[l]))
        kp_l.append(kp); vp_l.append(vp); ikp_l.append(ikp); sp_l.append(sp.astype(state_hgrn.dtype))
        mkp_l.append(mk_p); mvp_l.append(mv_p)
        ks_l.append(ks_); vs_l.append(vs_); iks_l.append(iks); ss_l.append(ss.astype(state_hgrn.dtype))
    y_prompt = rms_norm(xp, norm_final)
    y_sample = rms_norm(xs, norm_final)
    return (y_prompt, y_sample,
            jnp.stack(kp_l), jnp.stack(vp_l), jnp.stack(ikp_l), jnp.stack(sp_l), jnp.stack(mkp_l), jnp.stack(mvp_l),
            jnp.stack(ks_l), jnp.stack(vs_l), jnp.stack(iks_l), jnp.stack(ss_l))
```

```python
import functools

import numpy as np
import jax
import jax.numpy as jnp
from jax import lax
from jax.experimental import pallas as pl
from jax.experimental.pallas import tpu as pltpu

F32 = jnp.float32
BF16 = jnp.bfloat16
I32 = jnp.int32

D_MODEL = 1024
CHUNK = 64
EPS = 1e-6
ROPE_THETA = 500000.0
HEAD_DIM = 64
ROT_DIM = HEAD_DIM // 4
A_HEADS = 8
A_KV_HEADS = 2
A_GROUP = A_HEADS // A_KV_HEADS
A_WIDTH = A_HEADS * HEAD_DIM
KV_WIDTH = A_KV_HEADS * HEAD_DIM
IDX_HEADS = 4
IDX_WIDTH = IDX_HEADS * HEAD_DIM
TOPK_MAX = 256
B_HEADS = 8
B_WIDTH = B_HEADS * HEAD_DIM
B_CHUNK = 16
MEM_HEADS = 4
MEM_HEAD_DIM = D_MODEL // MEM_HEADS
N_EXPERTS = 8
LANES = 128
HALF = 256
IN_WIDTH_PAD = 3200
VMEM_LIMIT = 56 * 1024 * 1024
INT_MIN = -2147483648
NEG_BIG = -1e30

_NT = (((1,), (1,)), ((), ()))
_TN = (((0,), (0,)), ((), ()))


def _cparams(*sem):
    return pltpu.CompilerParams(dimension_semantics=sem, vmem_limit_bytes=VMEM_LIMIT)


def _rms(x, g):
    return x * lax.rsqrt(jnp.mean(x * x, axis=-1, keepdims=True) + EPS) * g


def _sigmoid(x):
    return 1.0 / (1.0 + jnp.exp(-x))


def _dot(a, b):
    return jnp.dot(a, b, preferred_element_type=F32)


def _split2(x):
    hi = x.astype(BF16)
    lo = (x - hi.astype(F32)).astype(BF16)
    return hi, lo


def _split3(x):
    hi = x.astype(BF16)
    r = x - hi.astype(F32)
    mid = r.astype(BF16)
    lo = (r - mid.astype(F32)).astype(BF16)
    return hi, mid, lo


def _rope128(xb, c, s1, s2):
    return xb * c + pltpu.roll(xb, LANES - ROT_DIM // 2, 1) * s1 + pltpu.roll(xb, ROT_DIM // 2, 1) * s2


def _inproj_kernel(x_ref, g_ref, w_ref, c_ref, s1_ref, s2_ref, qa_ref, hb_ref, kv_ref):
    xn = _rms(x_ref[...], g_ref[...]).astype(BF16)
    c, s1, s2 = c_ref[...], s1_ref[...], s2_ref[...]
    for j in range(A_WIDTH // LANES):
        h = _dot(xn, w_ref[:, j * LANES:(j + 1) * LANES])
        qa_ref[:, j * LANES:(j + 1) * LANES] = _rope128(h, c, s1, s2)
    for j in range(4):
        lo = A_WIDTH + j * B_WIDTH
        hb_ref[:, j * B_WIDTH:(j + 1) * B_WIDTH] = _dot(xn, w_ref[:, lo:lo + B_WIDTH])
    base = A_WIDTH + 4 * B_WIDTH
    kv_ref[:, 0:128] = _rope128(_dot(xn, w_ref[:, base:base + 128]), c, s1, s2)
    kv_ref[:, 128:256] = _dot(xn, w_ref[:, base + 128:base + 256])
    kv_ref[:, 256:384] = _rope128(_dot(xn, w_ref[:, base + 256:base + 384]), c, s1, s2)
    kv_ref[:, 384:512] = _rope128(_dot(xn, w_ref[:, base + 384:base + 512]), c, s1, s2)
    first = lax.broadcasted_iota(I32, c.shape, 1) < HEAD_DIM
    kv_ref[:, 512:640] = _rope128(_dot(xn, w_ref[:, base + 512:base + 640]),
                                  jnp.where(first, c, 1.0), jnp.where(first, s1, 0.0), jnp.where(first, s2, 0.0))


def _in_proj(x, g, w, tabs, t_len, tm):
    n = x.shape[0]
    nt = t_len // tm
    row = lambda i: (i, 0)
    tab = lambda i: (i % nt, 0)
    whole = lambda i: (0, 0)
    return pl.pallas_call(
        _inproj_kernel,
        grid=(n // tm,),
        in_specs=[pl.BlockSpec((tm, D_MODEL), row), pl.BlockSpec((1, D_MODEL), whole),
                  pl.BlockSpec((D_MODEL, IN_WIDTH_PAD), whole),
                  pl.BlockSpec((tm, LANES), tab), pl.BlockSpec((tm, LANES), tab), pl.BlockSpec((tm, LANES), tab)],
        out_specs=[pl.BlockSpec((tm, A_WIDTH), row), pl.BlockSpec((tm, 4 * B_WIDTH), row),
                   pl.BlockSpec((tm, 640), row)],
        out_shape=[jax.ShapeDtypeStruct((n, A_WIDTH), F32), jax.ShapeDtypeStruct((n, 4 * B_WIDTH), F32),
                   jax.ShapeDtypeStruct((n, 640), F32)],
        compiler_params=_cparams("parallel"),
        name="in_proj",
    )(x, g, w, *tabs)


def _rope_tables(pos):
    half = ROT_DIM // 2
    inv_freq = 1.0 / (ROPE_THETA ** (jnp.arange(half, dtype=F32) * (2.0 / ROT_DIM)))
    ang = pos.astype(F32)[:, None] * inv_freq[None, :]
    cos, sin = jnp.cos(ang), jnp.sin(ang)
    t = pos.shape[0]
    pad = HEAD_DIM - ROT_DIM
    c = jnp.concatenate([cos, cos, jnp.ones((t, pad), F32)], axis=1)
    s1 = jnp.concatenate([-sin, jnp.zeros((t, half + pad), F32)], axis=1)
    s2 = jnp.concatenate([jnp.zeros((t, half), F32), sin, jnp.zeros((t, pad), F32)], axis=1)
    rep = LANES // HEAD_DIM
    return tuple(jnp.tile(a, (1, rep)) for a in (c, s1, s2))


def _regroup_w_in(w):
    o = np.cumsum([0, A_WIDTH, KV_WIDTH, KV_WIDTH, IDX_WIDTH, HEAD_DIM, IDX_HEADS, B_WIDTH, B_WIDTH, B_WIDTH, B_WIDTH])
    seg = lambda i: w[:, o[i]:o[i + 1]]
    used = A_WIDTH + 4 * B_WIDTH + 2 * KV_WIDTH + IDX_WIDTH + HEAD_DIM + IDX_HEADS
    pad = jnp.zeros((w.shape[0], IN_WIDTH_PAD - used), w.dtype)
    return jnp.concatenate([seg(0), seg(6), seg(7), seg(8), seg(9), seg(1), seg(2), seg(3), seg(4), seg(5), pad],
                           axis=1).astype(BF16)


def _dsa_kernel(q_ref, iq_ref, iw_ref, k_ref, vt_ref, ik_ref, o_ref, key_ref, m_ref, l_ref, acc_ref,
                *, l_real, pos0, kb_rows, topk, tq, idx_bits):
    i = pl.program_id(1)
    qpos0 = pos0 + i * tq
    kend = jnp.minimum(((qpos0 + tq - 1) // CHUNK + 1) * CHUNK, l_real)
    nkb = (kend + kb_rows - 1) // kb_rows
    qchunk = (qpos0 + lax.broadcasted_iota(I32, (1, tq), 1)) // CHUNK
    row_iota = lax.broadcasted_iota(I32, (kb_rows, tq), 0)

    iq = iq_ref[0].astype(BF16)
    iqh = [iq[:, h * HEAD_DIM:(h + 1) * HEAD_DIM] for h in range(IDX_HEADS)]
    iw = iw_ref[0] * (IDX_HEADS ** -0.5 * HEAD_DIM ** -0.5)

    def score_body(kb, carry):
        r0 = pl.multiple_of(kb * kb_rows, kb_rows)
        ikb = ik_ref[0, pl.ds(r0, kb_rows), :]
        acc = jnp.zeros((kb_rows, tq), F32)
        for h in range(IDX_HEADS):
            lg = lax.dot_general(ikb, iqh[h], _NT, preferred_element_type=F32)
            acc = acc + iw[h:h + 1, :] * jnp.maximum(lg, 0.0)
        bits = pltpu.bitcast(acc + 0.0, I32)
        key = bits ^ ((bits >> 31) & 0x7FFFFFFF)
        kpos = r0 + row_iota
        adm = (kpos // CHUNK <= qchunk) & (kpos < l_real)
        key_ref[pl.ds(r0, kb_rows), :] = jnp.where(adm, key, INT_MIN)
        return carry

    lax.fori_loop(0, nkb, score_body, 0)

    def count(pred):
        def body(kb, c):
            r0 = pl.multiple_of(kb * kb_rows, kb_rows)
            hit = pred(key_ref[pl.ds(r0, kb_rows), :], r0)
            return c + jnp.where(hit, 1, 0).reshape(kb_rows // 8, 8, tq).sum(axis=0)
        c = lax.fori_loop(0, nkb, body, jnp.zeros((8, tq), I32))
        return c.sum(axis=0, keepdims=True)

    c0 = count(lambda blk, r0: blk >= 0)
    tau = jnp.where(c0 >= topk, 0, INT_MIN).astype(I32)

    def bit_body(it, prefix):
        cand = prefix | lax.shift_left(jnp.int32(1), 30 - it)
        c = count(lambda blk, r0: blk >= cand)
        return jnp.where(c >= topk, cand, prefix)

    tau = lax.fori_loop(0, 31, bit_body, tau)

    need = topk - count(lambda blk, r0: blk > tau)

    def idx_body(it, prefix):
        cand = prefix | lax.shift_left(jnp.int32(1), idx_bits - 1 - it)
        c = count(lambda blk, r0: (blk == tau) & (r0 + row_iota < cand))
        return jnp.where(c < need, cand, prefix)

    jstar = lax.fori_loop(0, idx_bits, idx_body, jnp.zeros((1, tq), I32))

    m_ref[...] = jnp.full(m_ref.shape, NEG_BIG, F32)
    l_ref[...] = jnp.zeros(l_ref.shape, F32)
    acc_ref[...] = jnp.zeros(acc_ref.shape, F32)
    qs = (q_ref[0] * (HEAD_DIM ** -0.5)).astype(BF16)
    qh = [qs[:, h * HEAD_DIM:(h + 1) * HEAD_DIM] for h in range(A_HEADS)]

    def att_body(kb, carry):
        r0 = pl.multiple_of(kb * kb_rows, kb_rows)
        key = key_ref[pl.ds(r0, kb_rows), :]
        sel = ((key > tau) | ((key == tau) & (r0 + row_iota <= jstar))) & (key != INT_MIN)
        for j in range(A_KV_HEADS):
            kblk = k_ref[0, j, pl.ds(r0, kb_rows), :]
            vtb = vt_ref[0, j * HEAD_DIM:(j + 1) * HEAD_DIM, pl.ds(r0, kb_rows)]
            for g in range(A_GROUP):
                h = j * A_GROUP + g
                s = lax.dot_general(kblk, qh[h], _NT, preferred_element_type=F32)
                m_old = m_ref[h:h + 1, :]
                m_new = jnp.maximum(m_old, jnp.max(jnp.where(sel, s, NEG_BIG), axis=0, keepdims=True))
                p = jnp.where(sel, jnp.exp(s - m_new), 0.0)
                alpha = jnp.exp(m_old - m_new)
                l_ref[h:h + 1, :] = alpha * l_ref[h:h + 1, :] + jnp.sum(p, axis=0, keepdims=True)
                rows = slice(h * HEAD_DIM, (h + 1) * HEAD_DIM)
                acc_ref[rows, :] = alpha * acc_ref[rows, :] + _dot(vtb, p.astype(BF16))
                m_ref[h:h + 1, :] = m_new
        return carry

    lax.fori_loop(0, nkb, att_body, 0)

    for h in range(A_HEADS):
        rows = slice(h * HEAD_DIM, (h + 1) * HEAD_DIM)
        acc_ref[rows, :] = acc_ref[rows, :] * (1.0 / l_ref[h:h + 1, :])
    o_ref[0] = acc_ref[...].T


def _dsa(q, iq, iw_t, k_hm, v_t, ik, *, l_real, pos0, kb_rows, topk, tq):
    bsz, t_len, _ = q.shape
    l_pad = ik.shape[1]
    idx_bits = max(1, int(l_pad - 1).bit_length())
    kern = functools.partial(_dsa_kernel, l_real=l_real, pos0=pos0, kb_rows=kb_rows, topk=topk, tq=tq,
                             idx_bits=idx_bits)
    return pl.pallas_call(
        kern,
        grid=(bsz, t_len // tq),
        in_specs=[pl.BlockSpec((1, tq, A_WIDTH), lambda b, i: (b, i, 0)),
                  pl.BlockSpec((1, tq, IDX_WIDTH), lambda b, i: (b, i, 0)),
                  pl.BlockSpec((1, 8, tq), lambda b, i: (b, 0, i)),
                  pl.BlockSpec((1, A_KV_HEADS, l_pad, HEAD_DIM), lambda b, i: (b, 0, 0, 0)),
                  pl.BlockSpec((1, KV_WIDTH, l_pad), lambda b, i: (b, 0, 0)),
                  pl.BlockSpec((1, l_pad, HEAD_DIM), lambda b, i: (b, 0, 0))],
        out_specs=pl.BlockSpec((1, tq, A_WIDTH), lambda b, i: (b, i, 0)),
        out_shape=jax.ShapeDtypeStruct((bsz, t_len, A_WIDTH), F32),
        scratch_shapes=[pltpu.VMEM((l_pad, tq), I32), pltpu.VMEM((8, tq), F32), pltpu.VMEM((8, tq), F32),
                        pltpu.VMEM((A_WIDTH, tq), F32)],
        compiler_params=_cparams("parallel", "arbitrary"),
        name="dsa",
    )(q, iq, iw_t, k_hm, v_t, ik)


def _hgrn_kernel(bq_ref, bf_ref, bi_ref, bg_ref, lb_ref, gn_ref, s0_ref, o_ref, sfin_ref,
                 st_ref, q_s, kk_s, b_s, qe_s, kd_s, dec_s, o_s, e_s, *, layer, tb):
    tstep = pl.program_id(1)
    nchunk = tb // B_CHUNK

    @pl.when(tstep == 0)
    def _():
        st_ref[...] = s0_ref[0]

    lbr = lb_ref[...]
    lbe = jnp.exp(lbr - jnp.max(lbr, axis=0, keepdims=True))
    lbs = lbe / jnp.sum(lbe, axis=0, keepdims=True)
    cum = lbs[0:1, :]
    first = cum
    for r in range(1, layer + 1):
        cum = cum + lbs[r:r + 1, :]
    lb = cum - first

    f = lb + (1.0 - lb) * _sigmoid(bf_ref[...])
    bq = bq_ref[...]
    q = bq * _sigmoid(bq) * (HEAD_DIM ** -0.5)
    g3 = _split3(jnp.log(f))
    ri = lax.broadcasted_iota(I32, (tb, tb), 0)
    ci = lax.broadcasted_iota(I32, (tb, tb), 1)
    same = (ri // B_CHUNK) == (ci // B_CHUNK)
    tri = jnp.where(same & (ci <= ri), 1.0, 0.0).astype(BF16)
    ones = jnp.where(same, 1.0, 0.0).astype(BF16)
    b = _dot(tri, g3[0]) + _dot(tri, g3[1]) + _dot(tri, g3[2])
    blast = _dot(ones, g3[0]) + _dot(ones, g3[1]) + _dot(ones, g3[2])
    q_s[...] = q
    kk_s[...] = 1.0 - f
    b_s[...] = b
    qe_s[...] = q * jnp.exp(b)
    kd_s[...] = (1.0 - f) * jnp.exp(blast - b)
    dec_s[...] = jnp.exp(blast)

    hi = lax.broadcasted_iota(I32, (HALF, HALF), 0) // HEAD_DIM
    hj = lax.broadcasted_iota(I32, (HALF, HALF), 1) // HEAD_DIM
    bd_mask = hi == hj
    bd = jnp.where(bd_mask, 1.0, 0.0).astype(BF16)
    t_iota = lax.broadcasted_iota(I32, (B_CHUNK, B_WIDTH), 0)

    def seg_sum(x):
        out = []
        for half in range(2):
            xh, xl = _split2(x[:, half * HALF:(half + 1) * HALF])
            out.append(_dot(xh, bd) + _dot(xl, bd))
        return jnp.concatenate(out, axis=1)

    def chunk_body(c, carry):
        r0 = pl.multiple_of(c * B_CHUNK, B_CHUNK)
        rows = pl.ds(r0, B_CHUNK)
        qc, kc, bc, vc = q_s[rows, :], kk_s[rows, :], b_s[rows, :], bi_ref[rows, :]
        qe, kd = qe_s[rows, :].astype(BF16), kd_s[rows, :].astype(BF16)
        dec = dec_s[pl.ds(r0, 1), :]
        for s in range(B_CHUNK):
            ex = jnp.exp(jnp.where(t_iota >= s, bc - bc[s:s + 1, :], NEG_BIG))
            e_s[s * B_CHUNK:(s + 1) * B_CHUNK, :] = qc * kc[s:s + 1, :] * ex
        aexp = seg_sum(e_s[...])
        o = jnp.zeros((B_CHUNK, B_WIDTH), F32)
        for s in range(B_CHUNK):
            o = o + aexp[s * B_CHUNK:(s + 1) * B_CHUNK, :] * vc[s:s + 1, :]
        inter = []
        vb = vc.astype(BF16)
        for half in range(2):
            cols = slice(half * HALF, (half + 1) * HALF)
            st = st_ref[half]
            inter.append(lax.dot_general(qe[:, cols], st.astype(BF16), _NT, preferred_element_type=F32))
            upd = lax.dot_general(vb[:, cols], kd[:, cols], _TN, preferred_element_type=F32)
            st_ref[half] = st * dec[:, cols] + jnp.where(bd_mask, upd, 0.0)
        o_s[rows, :] = o + jnp.concatenate(inter, axis=1)
        return carry

    lax.fori_loop(0, nchunk, chunk_body, 0)

    o = o_s[...]
    ms = seg_sum(o * o) * (1.0 / HEAD_DIM)
    bg = bg_ref[...]
    o_ref[...] = o * lax.rsqrt(ms + EPS) * gn_ref[...] * (bg * _sigmoid(bg))

    @pl.when(tstep == pl.num_programs(1) - 1)
    def _():
        sfin_ref[0] = st_ref[...]


def _hgrn(hb, lb_raw, gn_tiled, s0_bd, *, layer, bsz, t_len, tb):
    nt = t_len // tb
    col = lambda j: (lambda b, t: (b * nt + t, j))
    scr = lambda r: pltpu.VMEM((r, B_WIDTH), F32)
    kern = functools.partial(_hgrn_kernel, layer=layer, tb=tb)
    return pl.pallas_call(
        kern,
        grid=(bsz, nt),
        in_specs=[pl.BlockSpec((tb, B_WIDTH), col(0)), pl.BlockSpec((tb, B_WIDTH), col(1)),
                  pl.BlockSpec((tb, B_WIDTH), col(2)), pl.BlockSpec((tb, B_WIDTH), col(3)),
                  pl.BlockSpec(lb_raw.shape, lambda b, t: (0, 0)),
                  pl.BlockSpec((1, B_WIDTH), lambda b, t: (0, 0)),
                  pl.BlockSpec((1, 2, HALF, HALF), lambda b, t: (b, 0, 0, 0))],
        out_specs=[pl.BlockSpec((tb, B_WIDTH), lambda b, t: (b * nt + t, 0)),
                   pl.BlockSpec((1, 2, HALF, HALF), lambda b, t: (b, 0, 0, 0))],
        out_shape=[jax.ShapeDtypeStruct((bsz * t_len, B_WIDTH), F32),
                   jax.ShapeDtypeStruct((bsz, 2, HALF, HALF), F32)],
        scratch_shapes=[pltpu.VMEM((2, HALF, HALF), F32), scr(tb), scr(tb), scr(tb), scr(tb), scr(tb), scr(tb),
                        scr(tb), scr(B_CHUNK * B_CHUNK)],
        compiler_params=_cparams("parallel", "arbitrary"),
        name="hgrn2",
    )(hb, hb, hb, hb, lb_raw, gn_tiled, s0_bd)


def _state_to_tiles(s):
    bsz = s.shape[0]
    st = jnp.swapaxes(s, -1, -2).reshape(bsz, 2, B_HEADS // 2, HEAD_DIM, HEAD_DIM)
    eye = jnp.eye(B_HEADS // 2, dtype=s.dtype)
    t = st[:, :, :, :, None, :] * eye[None, None, :, None, :, None]
    return t.reshape(bsz, 2, HALF, HALF)


def _tiles_to_state(t):
    bsz = t.shape[0]
    t6 = t.reshape(bsz, 2, B_HEADS // 2, HEAD_DIM, B_HEADS // 2, HEAD_DIM)
    diag = jnp.stack([t6[:, :, h, :, h, :] for h in range(B_HEADS // 2)], axis=2)
    return jnp.swapaxes(diag.reshape(bsz, B_HEADS, HEAD_DIM, HEAD_DIM), -1, -2)


def _outproj_kernel(a_ref, b_ref, wa_ref, wb_ref, x_ref, o_ref):
    o_ref[...] = (x_ref[...] + _dot(a_ref[...].astype(BF16), wa_ref[...])
                  + _dot(b_ref[...].astype(BF16), wb_ref[...]))


def _out_proj(a, b, wa, wb, x, tm):
    n = x.shape[0]
    row = lambda i: (i, 0)
    whole = lambda i: (0, 0)
    return pl.pallas_call(
        _outproj_kernel,
        grid=(n // tm,),
        in_specs=[pl.BlockSpec((tm, A_WIDTH), row), pl.BlockSpec((tm, B_WIDTH), row),
                  pl.BlockSpec((A_WIDTH, D_MODEL), whole), pl.BlockSpec((B_WIDTH, D_MODEL), whole),
                  pl.BlockSpec((tm, D_MODEL), row)],
        out_specs=pl.BlockSpec((tm, D_MODEL), row),
        out_shape=jax.ShapeDtypeStruct((n, D_MODEL), F32),
        compiler_params=_cparams("parallel"),
        name="out_proj",
    )(a, b, wa, wb, x)


def _normmm_kernel(x_ref, g_ref, w_ref, o_ref):
    o_ref[...] = _dot(_rms(x_ref[...], g_ref[...]).astype(BF16), w_ref[...])


def _norm_matmul(x, g, w, tm):
    n, m = x.shape[0], w.shape[1]
    return pl.pallas_call(
        _normmm_kernel,
        grid=(n // tm,),
        in_specs=[pl.BlockSpec((tm, D_MODEL), lambda i: (i, 0)), pl.BlockSpec((1, D_MODEL), lambda i: (0, 0)),
                  pl.BlockSpec((D_MODEL, m), lambda i: (0, 0))],
        out_specs=pl.BlockSpec((tm, m), lambda i: (i, 0)),
        out_shape=jax.ShapeDtypeStruct((n, m), F32),
        compiler_params=_cparams("parallel"),
        name="norm_matmul",
    )(x, g, w)


def _memattn_kernel(x_ref, g_ref, wq_ref, mk_ref, mv_ref, wo_ref, o_ref):
    x = x_ref[...]
    q = _dot(_rms(x, g_ref[...]).astype(BF16), wq_ref[...])
    heads = []
    for h in range(MEM_HEADS):
        cols = slice(h * MEM_HEAD_DIM, (h + 1) * MEM_HEAD_DIM)
        s = lax.dot_general(q[:, cols].astype(BF16), mk_ref[0, :, cols], _NT,
                            preferred_element_type=F32) * (MEM_HEAD_DIM ** -0.5)
        p = jnp.exp(s - jnp.max(s, axis=-1, keepdims=True))
        p = p * (1.0 / jnp.sum(p, axis=-1, keepdims=True))
        heads.append(_dot(p.astype(BF16), mv_ref[0, :, cols]).astype(BF16))
    o_ref[...] = x + _dot(jnp.concatenate(heads, axis=1), wo_ref[...])


def _mem_attn(x, g, wq, mk, mv, wo, rows_per_batch, tm):
    n = x.shape[0]
    n_mem = mk.shape[1]
    per = rows_per_batch // tm
    row = lambda i: (i, 0)
    whole = lambda i: (0, 0)
    mem = lambda i: (i // per, 0, 0)
    return pl.pallas_call(
        _memattn_kernel,
        grid=(n // tm,),
        in_specs=[pl.BlockSpec((tm, D_MODEL), row), pl.BlockSpec((1, D_MODEL), whole),
                  pl.BlockSpec((D_MODEL, D_MODEL), whole),
                  pl.BlockSpec((1, n_mem, D_MODEL), mem), pl.BlockSpec((1, n_mem, D_MODEL), mem),
                  pl.BlockSpec((D_MODEL, D_MODEL), whole)],
        out_specs=pl.BlockSpec((tm, D_MODEL), row),
        out_shape=jax.ShapeDtypeStruct((n, D_MODEL), F32),
        compiler_params=_cparams("parallel"),
        name="mem_attn",
    )(x, g, wq, mk, mv, wo)


def _swiglu_mid(xn, w1, w3):
    a = _dot(xn, w1)
    return (a * _sigmoid(a) * _dot(xn, w3)).astype(BF16)


def _ffn_kernel(x_ref, g_ref, w1_ref, w3_ref, w2_ref, gf_ref, o_ref, xn_ref, *, final):
    j = pl.program_id(1)

    @pl.when(j == 0)
    def _():
        x = x_ref[...]
        xn_ref[...] = _rms(x, g_ref[...]).astype(BF16)
        o_ref[...] = x

    o_ref[...] += _dot(_swiglu_mid(xn_ref[...], w1_ref[...], w3_ref[...]), w2_ref[...])

    if final:
        @pl.when(j == pl.num_programs(1) - 1)
        def _():
            o_ref[...] = _rms(o_ref[...], gf_ref[...])


def _ffn(x, g, w1, w3, w2, gf, *, final, tm, tf):
    n = x.shape[0]
    d_ff = w1.shape[1]
    row = lambda i, j: (i, 0)
    whole = lambda i, j: (0, 0)
    return pl.pallas_call(
        functools.partial(_ffn_kernel, final=final),
        grid=(n // tm, d_ff // tf),
        in_specs=[pl.BlockSpec((tm, D_MODEL), row), pl.BlockSpec((1, D_MODEL), whole),
                  pl.BlockSpec((D_MODEL, tf), lambda i, j: (0, j)), pl.BlockSpec((D_MODEL, tf), lambda i, j: (0, j)),
                  pl.BlockSpec((tf, D_MODEL), lambda i, j: (j, 0)), pl.BlockSpec((1, D_MODEL), whole)],
        out_specs=pl.BlockSpec((tm, D_MODEL), row),
        out_shape=jax.ShapeDtypeStruct((n, D_MODEL), F32),
        scratch_shapes=[pltpu.VMEM((tm, D_MODEL), BF16)],
        compiler_params=_cparams("parallel", "arbitrary"),
        name="ffn",
    )(x, g, w1, w3, w2, gf)


def _router_kernel(x_ref, g_ref, rt_ref, comb_ref):
    xn = _rms(x_ref[...], g_ref[...])
    logits = lax.dot_general(rt_ref[...], xn, _NT, preferred_element_type=F32,
                             precision=lax.Precision.HIGHEST)
    e_idx = lax.broadcasted_iota(I32, logits.shape, 0)
    m1 = jnp.max(logits, axis=0, keepdims=True)
    i1 = jnp.min(jnp.where(logits == m1, e_idx, N_EXPERTS), axis=0, keepdims=True)
    rest = jnp.where(e_idx == i1, -jnp.inf, logits)
    m2 = jnp.max(rest, axis=0, keepdims=True)
    i2 = jnp.min(jnp.where(rest == m2, e_idx, N_EXPERTS), axis=0, keepdims=True)
    e2 = jnp.exp(m2 - m1)
    g1 = 1.0 / (1.0 + e2)
    comb_ref[...] = jnp.where(e_idx == i1, g1, 0.0) + jnp.where(e_idx == i2, e2 * g1, 0.0)


def _router(x, g, rt, tm):
    n = x.shape[0]
    return pl.pallas_call(
        _router_kernel,
        grid=(n // tm,),
        in_specs=[pl.BlockSpec((tm, D_MODEL), lambda i: (i, 0)), pl.BlockSpec((1, D_MODEL), lambda i: (0, 0)),
                  pl.BlockSpec((N_EXPERTS, D_MODEL), lambda i: (0, 0))],
        out_specs=pl.BlockSpec((N_EXPERTS, tm), lambda i: (0, i)),
        out_shape=jax.ShapeDtypeStruct((N_EXPERTS, n), F32),
        compiler_params=_cparams("parallel"),
        name="moe_router",
    )(x, g, rt)


def _moe_kernel(x_ref, g_ref, comb_ref, w1_ref, w3_ref, w2_ref, gf_ref, o_ref, xn_ref, *, final):
    e = pl.program_id(1)
    j = pl.program_id(2)

    @pl.when((e == 0) & (j == 0))
    def _():
        x = x_ref[...]
        xn_ref[...] = _rms(x, g_ref[...]).astype(BF16)
        o_ref[...] = x

    comb = comb_ref[...]
    lane = lax.broadcasted_iota(I32, comb.shape, 1)
    ce = jnp.sum(jnp.where(lane == e, comb, 0.0), axis=1, keepdims=True)
    o_ref[...] += ce * _dot(_swiglu_mid(xn_ref[...], w1_ref[0], w3_ref[0]), w2_ref[0])

    if final:
        @pl.when((e == pl.num_programs(1) - 1) & (j == pl.num_programs(2) - 1))
        def _():
            o_ref[...] = _rms(o_ref[...], gf_ref[...])


def _moe(x, g, comb, w1, w3, w2, gf, *, final, tm, tf):
    n = x.shape[0]
    d_ff = w1.shape[2]
    row = lambda i, e, j: (i, 0)
    whole = lambda i, e, j: (0, 0)
    return pl.pallas_call(
        functools.partial(_moe_kernel, final=final),
        grid=(n // tm, N_EXPERTS, d_ff // tf),
        in_specs=[pl.BlockSpec((tm, D_MODEL), row), pl.BlockSpec((1, D_MODEL), whole),
                  pl.BlockSpec((tm, N_EXPERTS), row),
                  pl.BlockSpec((1, D_MODEL, tf), lambda i, e, j: (e, 0, j)),
                  pl.BlockSpec((1, D_MODEL, tf), lambda i, e, j: (e, 0, j)),
                  pl.BlockSpec((1, tf, D_MODEL), lambda i, e, j: (e, j, 0)), pl.BlockSpec((1, D_MODEL), whole)],
        out_specs=pl.BlockSpec((tm, D_MODEL), row),
        out_shape=jax.ShapeDtypeStruct((n, D_MODEL), F32),
        scratch_shapes=[pltpu.VMEM((tm, D_MODEL), BF16)],
        compiler_params=_cparams("parallel", "arbitrary", "arbitrary"),
        name="moe",
    )(x, g, comb, w1, w3, w2, gf)


def _rmsnorm_kernel(x_ref, g_ref, o_ref):
    o_ref[...] = _rms(x_ref[...], g_ref[...])


def _rmsnorm(x, g, tm):
    n = x.shape[0]
    return pl.pallas_call(
        _rmsnorm_kernel,
        grid=(n // tm,),
        in_specs=[pl.BlockSpec((tm, D_MODEL), lambda i: (i, 0)), pl.BlockSpec((1, D_MODEL), lambda i: (0, 0))],
        out_specs=pl.BlockSpec((tm, D_MODEL), lambda i: (i, 0)),
        out_shape=jax.ShapeDtypeStruct((n, D_MODEL), F32),
        compiler_params=_cparams("parallel"),
        name="final_norm",
    )(x, g)


def _row_tile(n, cap):
    t = min(n, cap)
    assert n % t == 0, (n, cap)
    return t


def _ff_tile(d_ff):
    for parts in (2, 4, 11, 22):
        if d_ff % parts == 0 and (d_ff // parts) % LANES == 0:
            return d_ff // parts
    return d_ff


def _mixer(x, tabs, w, *, bsz, t_len, pos0, past, s0, layer):
    n = bsz * t_len
    tm = _row_tile(t_len, 256)
    qa, hb, kv = _in_proj(x, w["norm_mix"], w["w_in"], tabs, t_len, tm)
    ak = kv[:, 0:KV_WIDTH].reshape(bsz, t_len, A_KV_HEADS, HEAD_DIM)
    av = kv[:, KV_WIDTH:2 * KV_WIDTH].reshape(bsz, t_len, A_KV_HEADS, HEAD_DIM)
    iq = kv[:, 256:512].reshape(bsz, t_len, IDX_WIDTH)
    ik = kv[:, 512:512 + HEAD_DIM].reshape(bsz, t_len, HEAD_DIM)
    iw = kv[:, 512 + HEAD_DIM:512 + HEAD_DIM + IDX_HEADS].reshape(bsz, t_len, IDX_HEADS)
    if past is None:
        k_all, v_all, ik_all = ak, av, ik
    else:
        k_all = jnp.concatenate([past[0], ak], axis=1)
        v_all = jnp.concatenate([past[1], av], axis=1)
        ik_all = jnp.concatenate([past[2], ik], axis=1)
    l_real = k_all.shape[1]
    topk = min(TOPK_MAX, l_real // 4)
    tq = LANES
    kb_rows = 256 if l_real % 256 == 0 else LANES
    l_pad = -(-l_real // kb_rows) * kb_rows
    t_pad = -(-t_len // tq) * tq
    padk = lambda a: jnp.pad(a, ((0, 0), (0, l_pad - l_real)) + ((0, 0),) * (a.ndim - 2))
    padq = lambda a: jnp.pad(a, ((0, 0), (0, t_pad - t_len), (0, 0)))
    k_hm = jnp.swapaxes(padk(k_all), 1, 2).astype(BF16)
    v_t = jnp.swapaxes(padk(v_all).reshape(bsz, l_pad, KV_WIDTH), 1, 2).astype(BF16)
    iw_t = jnp.pad(jnp.swapaxes(padq(iw), 1, 2), ((0, 0), (0, 8 - IDX_HEADS), (0, 0)))
    oa = _dsa(padq(qa.reshape(bsz, t_len, A_WIDTH)), padq(iq), iw_t, k_hm, v_t, padk(ik_all).astype(BF16),
              l_real=l_real, pos0=pos0, kb_rows=kb_rows, topk=topk, tq=tq)
    oa = oa[:, :t_len].reshape(n, A_WIDTH)
    ob, s_tiles = _hgrn(hb, w["hgrn_lb"], w["gnorm"], _state_to_tiles(s0), layer=layer, bsz=bsz, t_len=t_len,
                        tb=_row_tile(t_len, 256))
    x1 = _out_proj(oa, ob, w["w_out_a"], w["w_out_b"], x, _row_tile(n, 512))
    return x1, ak, av, ik, _tiles_to_state(s_tiles)


def kernel(x_prompt, x_sample, cache_k, cache_v, cache_idx_k, state_hgrn, cache_mem_k, cache_mem_v, mem_prompt,
           norm_mix, w_in, hgrn_lb, hgrn_gnorm, w_out, norm_mem, norm_memkv, w_mq, w_mk, w_mv, w_mo,
           norm_ffn, ffn_w1, ffn_w3, ffn_w2, moe_router, moe_w1, moe_w3, moe_w2, norm_final):
    depth = w_in.shape[0]
    bp, tp, _ = x_prompt.shape
    bs, ts, _ = x_sample.shape
    past = cache_k.shape[2]
    n_mem = mem_prompt.shape[1]
    tabs_p = _rope_tables(jnp.arange(tp, dtype=I32))
    tabs_s = _rope_tables(past + jnp.arange(ts, dtype=I32))
    xp = x_prompt.reshape(bp * tp, D_MODEL)
    xs = x_sample.reshape(bs * ts, D_MODEL)
    mem = mem_prompt.reshape(bp * n_mem, D_MODEL)
    s0_p = jnp.zeros((bp, B_HEADS, HEAD_DIM, HEAD_DIM), F32)
    gfin = norm_final.reshape(1, D_MODEL)
    row2 = lambda a: a.reshape(1, -1)
    outs = {k: [] for k in ("kp", "vp", "ikp", "sp", "mkp", "mvp", "ks", "vs", "iks", "ss")}
    for l in range(depth):
        w = {"norm_mix": row2(norm_mix[l]), "w_in": _regroup_w_in(w_in[l]), "hgrn_lb": hgrn_lb.astype(F32),
             "gnorm": row2(jnp.tile(hgrn_gnorm[l], B_HEADS)),
             "w_out_a": w_out[l, :A_WIDTH].astype(BF16), "w_out_b": w_out[l, A_WIDTH:].astype(BF16)}
        xp, kp, vp, ikp, sp = _mixer(xp, tabs_p, w, bsz=bp, t_len=tp, pos0=0, past=None, s0=s0_p, layer=l)
        xs, ks, vs, iks, ss = _mixer(xs, tabs_s, w, bsz=bs, t_len=ts, pos0=past,
                                     past=(cache_k[l], cache_v[l], cache_idx_k[l]), s0=state_hgrn[l], layer=l)
        wkv = jnp.concatenate([w_mk[l], w_mv[l]], axis=1).astype(BF16)
        mkv = _norm_matmul(mem, row2(norm_memkv[l]), wkv, _row_tile(bp * n_mem, 256))
        mk_p = mkv[:, :D_MODEL].reshape(bp, n_mem, D_MODEL)
        mv_p = mkv[:, D_MODEL:].reshape(bp, n_mem, D_MODEL)
        wq, wo, gm = w_mq[l].astype(BF16), w_mo[l].astype(BF16), row2(norm_mem[l])
        xp = _mem_attn(xp, gm, wq, mk_p.astype(BF16), mv_p.astype(BF16), wo, tp, _row_tile(tp, 256))
        xs = _mem_attn(xs, gm, wq, cache_mem_k[l].reshape(bs, n_mem, D_MODEL).astype(BF16),
                       cache_mem_v[l].reshape(bs, n_mem, D_MODEL).astype(BF16), wo, ts, _row_tile(ts, 256))
        final = l == depth - 1
        gf = row2(norm_ffn[l])
        if l % 2 == 0:
            w1, w3, w2 = ffn_w1[l // 2].astype(BF16), ffn_w3[l // 2].astype(BF16), ffn_w2[l // 2].astype(BF16)
            tf = _ff_tile(w1.shape[1])
            xp = _ffn(xp, gf, w1, w3, w2, gfin, final=final, tm=_row_tile(bp * tp, 512), tf=tf)
            xs = _ffn(xs, gf, w1, w3, w2, gfin, final=final, tm=_row_tile(bs * ts, 512), tf=tf)
        else:
            w1, w3, w2 = moe_w1[l // 2].astype(BF16), moe_w3[l // 2].astype(BF16), moe_w2[l // 2].astype(BF16)
            rt = moe_router[l // 2].T
            tf = _ff_tile(w1.shape[2])
            comb_p = _router(xp, gf, rt, _row_tile(bp * tp, 512)).T
            comb_s = _router(xs, gf, rt, _row_tile(bs * ts, 512)).T
            xp = _moe(xp, gf, comb_p, w1, w3, w2, gfin, final=final, tm=_row_tile(bp * tp, 512), tf=tf)
            xs = _moe(xs, gf, comb_s, w1, w3, w2, gfin, final=final, tm=_row_tile(bs * ts, 512), tf=tf)
        outs["kp"].append(kp); outs["vp"].append(vp); outs["ikp"].append(ikp); outs["sp"].append(sp)
        outs["mkp"].append(mk_p.reshape(bp, n_mem, MEM_HEADS, MEM_HEAD_DIM))
        outs["mvp"].append(mv_p.reshape(bp, n_mem, MEM_HEADS, MEM_HEAD_DIM))
        outs["ks"].append(ks); outs["vs"].append(vs); outs["iks"].append(iks); outs["ss"].append(ss)
    y_prompt = xp.reshape(bp, tp, D_MODEL)
    y_sample = xs.reshape(bs, ts, D_MODEL)
    st = lambda k: jnp.stack(outs[k])
    return (y_prompt, y_sample, st("kp"), st("vp"), st("ikp"), st("sp"), st("mkp"), st("mvp"),
            st("ks"), st("vs"), st("iks"), st("ss"))
```

```python
import functools

import numpy as np
import jax
import jax.numpy as jnp
from jax import lax
from jax.experimental import pallas as pl
from jax.experimental.pallas import tpu as pltpu

F32 = jnp.float32
BF16 = jnp.bfloat16
I32 = jnp.int32
I16 = jnp.int16

D_MODEL = 1024
CHUNK = 64
EPS = 1e-6
ROPE_THETA = 500000.0
HEAD_DIM = 64
ROT_DIM = HEAD_DIM // 4
A_HEADS = 8
A_KV_HEADS = 2
A_GROUP = A_HEADS // A_KV_HEADS
A_WIDTH = A_HEADS * HEAD_DIM
KV_WIDTH = A_KV_HEADS * HEAD_DIM
IDX_HEADS = 4
IDX_WIDTH = IDX_HEADS * HEAD_DIM
TOPK_MAX = 256
B_HEADS = 8
B_WIDTH = B_HEADS * HEAD_DIM
B_CHUNK = 16
MEM_HEADS = 4
MEM_HEAD_DIM = D_MODEL // MEM_HEADS
N_EXPERTS = 8
LANES = 128
HALF = 256
IN_WIDTH_PAD = 3200
VMEM_LIMIT = 56 * 1024 * 1024
INT_MIN = -2147483648
NEG_BIG = -1e30
LOG2E = 1.4426950408889634

_NT = (((1,), (1,)), ((), ()))
_TN = (((0,), (0,)), ((), ()))


def _cparams(*sem):
    return pltpu.CompilerParams(dimension_semantics=sem, vmem_limit_bytes=VMEM_LIMIT)


def _rms(x, g):
    return x * lax.rsqrt(jnp.mean(x * x, axis=-1, keepdims=True) + EPS) * g


def _sigmoid(x):
    return 1.0 / (1.0 + jnp.exp(-x))


def _dot(a, b):
    return jnp.dot(a, b, preferred_element_type=F32)


def _split2(x):
    hi = x.astype(BF16)
    lo = (x - hi.astype(F32)).astype(BF16)
    return hi, lo


def _split3(x):
    hi = x.astype(BF16)
    r = x - hi.astype(F32)
    mid = r.astype(BF16)
    lo = (r - mid.astype(F32)).astype(BF16)
    return hi, mid, lo


def _rope128(xb, c, s1, s2):
    return xb * c + pltpu.roll(xb, LANES - ROT_DIM // 2, 1) * s1 + pltpu.roll(xb, ROT_DIM // 2, 1) * s2


def _inproj_kernel(x_ref, g_ref, w_ref, c_ref, s1_ref, s2_ref, qa_ref, hb_ref, kv_ref):
    xn = _rms(x_ref[...], g_ref[...]).astype(BF16)
    c, s1, s2 = c_ref[...], s1_ref[...], s2_ref[...]
    for j in range(A_WIDTH // LANES):
        h = _dot(xn, w_ref[:, j * LANES:(j + 1) * LANES])
        qa_ref[:, j * LANES:(j + 1) * LANES] = _rope128(h, c, s1, s2)
    for j in range(4):
        lo = A_WIDTH + j * B_WIDTH
        hb_ref[:, j * B_WIDTH:(j + 1) * B_WIDTH] = _dot(xn, w_ref[:, lo:lo + B_WIDTH])
    base = A_WIDTH + 4 * B_WIDTH
    kv_ref[:, 0:128] = _rope128(_dot(xn, w_ref[:, base:base + 128]), c, s1, s2)
    kv_ref[:, 128:256] = _dot(xn, w_ref[:, base + 128:base + 256])
    kv_ref[:, 256:384] = _rope128(_dot(xn, w_ref[:, base + 256:base + 384]), c, s1, s2)
    kv_ref[:, 384:512] = _rope128(_dot(xn, w_ref[:, base + 384:base + 512]), c, s1, s2)
    first = lax.broadcasted_iota(I32, c.shape, 1) < HEAD_DIM
    kv_ref[:, 512:640] = _rope128(_dot(xn, w_ref[:, base + 512:base + 640]),
                                  jnp.where(first, c, 1.0), jnp.where(first, s1, 0.0), jnp.where(first, s2, 0.0))


def _in_proj(x, g, w, tabs, t_len, tm):
    n = x.shape[0]
    nt = t_len // tm
    row = lambda i: (i, 0)
    tab = lambda i: (i % nt, 0)
    whole = lambda i: (0, 0)
    return pl.pallas_call(
        _inproj_kernel,
        grid=(n // tm,),
        in_specs=[pl.BlockSpec((tm, D_MODEL), row), pl.BlockSpec((1, D_MODEL), whole),
                  pl.BlockSpec((D_MODEL, IN_WIDTH_PAD), whole),
                  pl.BlockSpec((tm, LANES), tab), pl.BlockSpec((tm, LANES), tab), pl.BlockSpec((tm, LANES), tab)],
        out_specs=[pl.BlockSpec((tm, A_WIDTH), row), pl.BlockSpec((tm, 4 * B_WIDTH), row),
                   pl.BlockSpec((tm, 640), row)],
        out_shape=[jax.ShapeDtypeStruct((n, A_WIDTH), F32), jax.ShapeDtypeStruct((n, 4 * B_WIDTH), F32),
                   jax.ShapeDtypeStruct((n, 640), F32)],
        compiler_params=_cparams("parallel"),
        name="in_proj",
    )(x, g, w, *tabs)


def _rope_tables(pos):
    half = ROT_DIM // 2
    inv_freq = 1.0 / (ROPE_THETA ** (jnp.arange(half, dtype=F32) * (2.0 / ROT_DIM)))
    ang = pos.astype(F32)[:, None] * inv_freq[None, :]
    cos, sin = jnp.cos(ang), jnp.sin(ang)
    t = pos.shape[0]
    pad = HEAD_DIM - ROT_DIM
    c = jnp.concatenate([cos, cos, jnp.ones((t, pad), F32)], axis=1)
    s1 = jnp.concatenate([-sin, jnp.zeros((t, half + pad), F32)], axis=1)
    s2 = jnp.concatenate([jnp.zeros((t, half), F32), sin, jnp.zeros((t, pad), F32)], axis=1)
    rep = LANES // HEAD_DIM
    return tuple(jnp.tile(a, (1, rep)) for a in (c, s1, s2))


def _regroup_w_in(w):
    o = np.cumsum([0, A_WIDTH, KV_WIDTH, KV_WIDTH, IDX_WIDTH, HEAD_DIM, IDX_HEADS, B_WIDTH, B_WIDTH, B_WIDTH, B_WIDTH])
    seg = lambda i: w[:, o[i]:o[i + 1]]
    used = A_WIDTH + 4 * B_WIDTH + 2 * KV_WIDTH + IDX_WIDTH + HEAD_DIM + IDX_HEADS
    pad = jnp.zeros((w.shape[0], IN_WIDTH_PAD - used), w.dtype)
    return jnp.concatenate([seg(0), seg(6), seg(7), seg(8), seg(9), seg(1), seg(2), seg(3), seg(4), seg(5), pad],
                           axis=1).astype(BF16)


def _dsa_kernel(q_ref, iq_ref, iw_ref, k_ref, v_ref, ik_ref, o_ref, key_ref, hi_ref, lo_ref, acc_ref, m_ref, s0_ref, s1_ref, alpha_ref, p_ref, bias_ref,
                *, l_real, pos0, kb_rows, topk, tq, idx_bits):
    i = pl.program_id(1)
    qpos0 = pos0 + i * tq
    kend = jnp.minimum(((qpos0 + tq - 1) // CHUNK + 1) * CHUNK, l_real)
    nkb = (kend + kb_rows - 1) // kb_rows
    qchunk = (qpos0 + lax.broadcasted_iota(I32, (1, tq), 1)) // CHUNK
    row_iota = lax.broadcasted_iota(I32, (kb_rows, tq), 0)

    iq = iq_ref[0].astype(BF16)
    iq4 = jnp.concatenate([iq[:, h * HEAD_DIM:(h + 1) * HEAD_DIM] for h in range(IDX_HEADS)], axis=0)
    iw = iw_ref[0] * (IDX_HEADS ** -0.5 * HEAD_DIM ** -0.5)

    def score_body(kb, carry):
        r0 = pl.multiple_of(kb * kb_rows, kb_rows)
        ikb = ik_ref[0, pl.ds(r0, kb_rows), :]
        lg4 = lax.dot_general(ikb, iq4, _NT, preferred_element_type=F32)
        acc = jnp.zeros((kb_rows, tq), F32)
        for h in range(IDX_HEADS):
            acc = acc + iw[h:h + 1, :] * jnp.maximum(lg4[:, h * tq:(h + 1) * tq], 0.0)
        bits = pltpu.bitcast(acc + 0.0, I32)
        key = bits ^ ((bits >> 31) & 0x7FFFFFFF)
        kpos = r0 + row_iota
        adm = (kpos // CHUNK <= qchunk) & (kpos < l_real)
        key = jnp.where(adm, key, INT_MIN)
        key_ref[pl.ds(r0, kb_rows), :] = key
        hi_ref[pl.ds(r0, kb_rows), :] = (key >> 16).astype(I16)
        return carry

    lax.fori_loop(0, nkb, score_body, 0)

    def count(pred):
        def body(kb, c):
            r0 = pl.multiple_of(kb * kb_rows, kb_rows)
            hit = pred(key_ref[pl.ds(r0, kb_rows), :], r0)
            return c + jnp.where(hit, 1, 0).reshape(kb_rows // 8, 8, tq).sum(axis=0)
        c = lax.fori_loop(0, nkb, body, jnp.zeros((8, tq), I32))
        return c.sum(axis=0, keepdims=True)

    def count16(ref, pred):
        one, zero = jnp.ones((), BF16), jnp.zeros((), BF16)

        def body(kb, c):
            r0 = pl.multiple_of(kb * kb_rows, kb_rows)
            w = jnp.where(pred(ref[pl.ds(r0, kb_rows), :]), one, zero).reshape(kb_rows // 16, 16, tq)
            parts = [w[r] for r in range(kb_rows // 16)]
            while len(parts) > 1:
                parts = [a + b for a, b in zip(parts[::2], parts[1::2])] + parts[len(parts) & ~1:]
            return c + parts[0]
        c = lax.fori_loop(0, nkb, body, jnp.zeros((16, tq), BF16))
        return c.astype(F32).sum(axis=0, keepdims=True).astype(I32)

    def bisect16(ref, kth):
        c0 = count16(ref, lambda blk: blk >= jnp.zeros((), I16))
        start = jnp.where(c0 >= kth, 0, -32768).astype(I32)

        def bit_body(it, prefix):
            cand = prefix | lax.shift_left(jnp.int32(1), 14 - it)
            c = count16(ref, lambda blk: blk >= cand.astype(I16))
            return jnp.where(c >= kth, cand, prefix)

        return lax.fori_loop(0, 15, bit_body, start)

    tau_hi = bisect16(hi_ref, topk)
    tau_hi16 = tau_hi.astype(I16)
    kth_lo = topk - count16(hi_ref, lambda blk: blk > tau_hi16)

    def lo_body(kb, carry):
        r0 = pl.multiple_of(kb * kb_rows, kb_rows)
        lo = ((key_ref[pl.ds(r0, kb_rows), :] & 0xFFFF) - 32768).astype(I16)
        lo_ref[pl.ds(r0, kb_rows), :] = jnp.where(hi_ref[pl.ds(r0, kb_rows), :] == tau_hi16, lo,
                                                 jnp.full((), -32768, I16))
        return carry

    lax.fori_loop(0, nkb, lo_body, 0)
    tau_lo = bisect16(lo_ref, kth_lo)
    tau = lax.shift_left(tau_hi, 16) | (tau_lo + 32768)
    n_ge = count(lambda blk, r0: blk >= tau)

    def tie_index():
        need = topk - count(lambda blk, r0: blk > tau)

        def idx_body(it, prefix):
            cand = prefix | lax.shift_left(jnp.int32(1), idx_bits - 1 - it)
            c = count(lambda blk, r0: (blk == tau) & (r0 + row_iota < cand))
            return jnp.where(c < need, cand, prefix)

        return lax.fori_loop(0, idx_bits, idx_body, jnp.zeros((1, tq), I32))

    surplus = jnp.max(jnp.where((n_ge > topk) & (tau != INT_MIN), 1, 0))
    jstar = lax.cond(surplus > 0, tie_index, lambda: jnp.full((1, tq), 2 ** idx_bits, I32))

    lane_rep = kb_rows // LANES
    tau_c = jnp.tile(jnp.broadcast_to(tau, (LANES, tq)).T, (1, lane_rep))
    jst_c = jnp.tile(jnp.broadcast_to(jstar, (LANES, tq)).T, (1, lane_rep))
    col_iota = lax.broadcasted_iota(I32, (tq, kb_rows), 1)
    acc_ref[...] = jnp.zeros(acc_ref.shape, F32)
    m_ref[...] = jnp.full(m_ref.shape, NEG_BIG, F32)
    qs = (q_ref[0] * (HEAD_DIM ** -0.5 * LOG2E)).astype(BF16)
    q4 = [jnp.concatenate([qs[:, (j * A_GROUP + g) * HEAD_DIM:(j * A_GROUP + g + 1) * HEAD_DIM]
                           for g in range(A_GROUP)], axis=0) for j in range(A_KV_HEADS)]

    def qk_scores(kb, s_slot):
        r0 = pl.multiple_of(jnp.minimum(kb, nkb - 1) * kb_rows, kb_rows)
        for j in range(A_KV_HEADS):
            s_slot[j] = lax.dot_general(q4[j], k_ref[0, j, pl.ds(r0, kb_rows), :], _NT,
                                        preferred_element_type=F32)

    def softmax_pv(kb, s_slot):
        r0 = pl.multiple_of(jnp.minimum(kb, nkb - 1) * kb_rows, kb_rows)
        key_t = jnp.where(kb < nkb, key_ref[pl.ds(r0, kb_rows), :], INT_MIN).T
        sel = ((key_t > tau_c) | ((key_t == tau_c) & (r0 + col_iota <= jst_c))) & (key_t != INT_MIN)
        bias_ref[...] = jnp.where(sel, 0.0, NEG_BIG)
        for j in range(A_KV_HEADS):
            for g in range(A_GROUP):
                rows = slice(g * tq, (g + 1) * tq)
                m_old = m_ref[j, g]
                m_new = jnp.maximum(m_old, jnp.max(s_slot[j, rows, :] + bias_ref[...], axis=1, keepdims=True))
                alpha_ref[j, g] = jnp.exp2(m_old - m_new)
                m_ref[j, g] = m_new
            for g in range(A_GROUP):
                rows = slice(g * tq, (g + 1) * tq)
                sm = s_slot[j, rows, :] + bias_ref[...]
                p_ref[j, rows, :] = jnp.exp2(sm - jnp.tile(m_ref[j, g], (1, lane_rep))).astype(BF16)
            pv = _dot(p_ref[j], v_ref[0, j, pl.ds(r0, kb_rows), :])
            acc_ref[j] = alpha_ref[j] * acc_ref[j] + pv.reshape(A_GROUP, tq, LANES)

    qk_scores(0, s0_ref)

    def pair_body(i, carry):
        kb = 2 * i
        qk_scores(kb + 1, s1_ref)
        softmax_pv(kb, s0_ref)
        qk_scores(kb + 2, s0_ref)
        softmax_pv(kb + 1, s1_ref)
        return carry

    lax.fori_loop(0, (nkb + 1) // 2, pair_body, 0)

    outs = []
    for j in range(A_KV_HEADS):
        acc = acc_ref[j]
        for g in range(A_GROUP):
            outs.append(acc[g, :, :HEAD_DIM] * (1.0 / acc[g, :, HEAD_DIM:HEAD_DIM + 1]))
    o_ref[0] = jnp.concatenate(outs, axis=1)


def _dsa(q, iq, iw_t, k_hm, v_aug, ik, *, l_real, pos0, kb_rows, topk, tq):
    bsz, t_len, _ = q.shape
    l_pad = ik.shape[1]
    idx_bits = max(1, int(l_pad - 1).bit_length())
    assert l_pad // 16 <= 256, "packed hit counts are summed in bf16, exact only up to 256 per slot"
    kern = functools.partial(_dsa_kernel, l_real=l_real, pos0=pos0, kb_rows=kb_rows, topk=topk, tq=tq,
                             idx_bits=idx_bits)
    return pl.pallas_call(
        kern,
        grid=(bsz, t_len // tq),
        in_specs=[pl.BlockSpec((1, tq, A_WIDTH), lambda b, i: (b, i, 0)),
                  pl.BlockSpec((1, tq, IDX_WIDTH), lambda b, i: (b, i, 0)),
                  pl.BlockSpec((1, 8, tq), lambda b, i: (b, 0, i)),
                  pl.BlockSpec((1, A_KV_HEADS, l_pad, HEAD_DIM), lambda b, i: (b, 0, 0, 0)),
                  pl.BlockSpec((1, A_KV_HEADS, l_pad, LANES), lambda b, i: (b, 0, 0, 0)),
                  pl.BlockSpec((1, l_pad, HEAD_DIM), lambda b, i: (b, 0, 0))],
        out_specs=pl.BlockSpec((1, tq, A_WIDTH), lambda b, i: (b, i, 0)),
        out_shape=jax.ShapeDtypeStruct((bsz, t_len, A_WIDTH), F32),
        scratch_shapes=[pltpu.VMEM((l_pad, tq), I32), pltpu.VMEM((l_pad, tq), I16), pltpu.VMEM((l_pad, tq), I16),
                        pltpu.VMEM((A_KV_HEADS, A_GROUP, tq, LANES), F32),
                        pltpu.VMEM((A_KV_HEADS, A_GROUP, tq, LANES), F32),
                        pltpu.VMEM((A_KV_HEADS, A_GROUP * tq, kb_rows), F32),
                        pltpu.VMEM((A_KV_HEADS, A_GROUP * tq, kb_rows), F32),
                        pltpu.VMEM((A_KV_HEADS, A_GROUP, tq, LANES), F32),
                        pltpu.VMEM((A_KV_HEADS, A_GROUP * tq, kb_rows), BF16),
                        pltpu.VMEM((tq, kb_rows), F32)],
        compiler_params=_cparams("parallel", "arbitrary"),
        name="dsa",
    )(q, iq, iw_t, k_hm, v_aug, ik)


def _hgrn_kernel(bq_ref, bf_ref, bi_ref, bg_ref, lb_ref, gn_ref, s0_ref, o_ref, sfin_ref,
                 st_ref, q_s, kk_s, b_s, qe_s, kd_s, dec_s, o_s, e_s, *, layer, tb):
    tstep = pl.program_id(1)
    nchunk = tb // B_CHUNK

    @pl.when(tstep == 0)
    def _():
        st_ref[...] = s0_ref[0]

    lbr = lb_ref[...]
    lbe = jnp.exp(lbr - jnp.max(lbr, axis=0, keepdims=True))
    lbs = lbe / jnp.sum(lbe, axis=0, keepdims=True)
    cum = lbs[0:1, :]
    first = cum
    for r in range(1, layer + 1):
        cum = cum + lbs[r:r + 1, :]
    lb = cum - first

    f = lb + (1.0 - lb) * _sigmoid(bf_ref[...])
    bq = bq_ref[...]
    q = bq * _sigmoid(bq) * (HEAD_DIM ** -0.5)
    g3 = _split3(jnp.log(f))
    ri = lax.broadcasted_iota(I32, (tb, tb), 0)
    ci = lax.broadcasted_iota(I32, (tb, tb), 1)
    same = (ri // B_CHUNK) == (ci // B_CHUNK)
    tri = jnp.where(same & (ci <= ri), 1.0, 0.0).astype(BF16)
    ones = jnp.where(same, 1.0, 0.0).astype(BF16)
    b = _dot(tri, g3[0]) + _dot(tri, g3[1]) + _dot(tri, g3[2])
    blast = _dot(ones, g3[0]) + _dot(ones, g3[1]) + _dot(ones, g3[2])
    q_s[...] = q
    kk_s[...] = 1.0 - f
    b_s[...] = b
    qe_s[...] = q * jnp.exp(b)
    kd_s[...] = (1.0 - f) * jnp.exp(blast - b)
    dec_s[...] = jnp.exp(blast)

    hi = lax.broadcasted_iota(I32, (HALF, HALF), 0) // HEAD_DIM
    hj = lax.broadcasted_iota(I32, (HALF, HALF), 1) // HEAD_DIM
    bd_mask = hi == hj
    bd = jnp.where(bd_mask, 1.0, 0.0).astype(BF16)
    t_iota = lax.broadcasted_iota(I32, (B_CHUNK, B_WIDTH), 0)

    def seg_sum(x):
        out = []
        for half in range(2):
            xh, xl = _split2(x[:, half * HALF:(half + 1) * HALF])
            out.append(_dot(xh, bd) + _dot(xl, bd))
        return jnp.concatenate(out, axis=1)

    def chunk_body(c, carry):
        r0 = pl.multiple_of(c * B_CHUNK, B_CHUNK)
        rows = pl.ds(r0, B_CHUNK)
        qc, kc, bc, vc = q_s[rows, :], kk_s[rows, :], b_s[rows, :], bi_ref[rows, :]
        qe, kd = qe_s[rows, :].astype(BF16), kd_s[rows, :].astype(BF16)
        dec = dec_s[pl.ds(r0, 1), :]
        for s in range(B_CHUNK):
            ex = jnp.exp(jnp.where(t_iota >= s, bc - bc[s:s + 1, :], NEG_BIG))
            e_s[s * B_CHUNK:(s + 1) * B_CHUNK, :] = qc * kc[s:s + 1, :] * ex
        aexp = seg_sum(e_s[...])
        o = jnp.zeros((B_CHUNK, B_WIDTH), F32)
        for s in range(B_CHUNK):
            o = o + aexp[s * B_CHUNK:(s + 1) * B_CHUNK, :] * vc[s:s + 1, :]
        inter = []
        vb = vc.astype(BF16)
        for half in range(2):
            cols = slice(half * HALF, (half + 1) * HALF)
            st = st_ref[half]
            inter.append(lax.dot_general(qe[:, cols], st.astype(BF16), _NT, preferred_element_type=F32))
            upd = lax.dot_general(vb[:, cols], kd[:, cols], _TN, preferred_element_type=F32)
            st_ref[half] = st * dec[:, cols] + jnp.where(bd_mask, upd, 0.0)
        o_s[rows, :] = o + jnp.concatenate(inter, axis=1)
        return carry

    lax.fori_loop(0, nchunk, chunk_body, 0)

    o = o_s[...]
    ms = seg_sum(o * o) * (1.0 / HEAD_DIM)
    bg = bg_ref[...]
    o_ref[...] = o * lax.rsqrt(ms + EPS) * gn_ref[...] * (bg * _sigmoid(bg))

    @pl.when(tstep == pl.num_programs(1) - 1)
    def _():
        sfin_ref[0] = st_ref[...]


def _hgrn(hb, lb_raw, gn_tiled, s0_bd, *, layer, bsz, t_len, tb):
    nt = t_len // tb
    col = lambda j: (lambda b, t: (b * nt + t, j))
    scr = lambda r: pltpu.VMEM((r, B_WIDTH), F32)
    kern = functools.partial(_hgrn_kernel, layer=layer, tb=tb)
    return pl.pallas_call(
        kern,
        grid=(bsz, nt),
        in_specs=[pl.BlockSpec((tb, B_WIDTH), col(0)), pl.BlockSpec((tb, B_WIDTH), col(1)),
                  pl.BlockSpec((tb, B_WIDTH), col(2)), pl.BlockSpec((tb, B_WIDTH), col(3)),
                  pl.BlockSpec(lb_raw.shape, lambda b, t: (0, 0)),
                  pl.BlockSpec((1, B_WIDTH), lambda b, t: (0, 0)),
                  pl.BlockSpec((1, 2, HALF, HALF), lambda b, t: (b, 0, 0, 0))],
        out_specs=[pl.BlockSpec((tb, B_WIDTH), lambda b, t: (b * nt + t, 0)),
                   pl.BlockSpec((1, 2, HALF, HALF), lambda b, t: (b, 0, 0, 0))],
        out_shape=[jax.ShapeDtypeStruct((bsz * t_len, B_WIDTH), F32),
                   jax.ShapeDtypeStruct((bsz, 2, HALF, HALF), F32)],
        scratch_shapes=[pltpu.VMEM((2, HALF, HALF), F32), scr(tb), scr(tb), scr(tb), scr(tb), scr(tb), scr(tb),
                        scr(tb), scr(B_CHUNK * B_CHUNK)],
        compiler_params=_cparams("parallel", "arbitrary"),
        name="hgrn2",
    )(hb, hb, hb, hb, lb_raw, gn_tiled, s0_bd)


def _state_to_tiles(s):
    bsz = s.shape[0]
    st = jnp.swapaxes(s, -1, -2).reshape(bsz, 2, B_HEADS // 2, HEAD_DIM, HEAD_DIM)
    eye = jnp.eye(B_HEADS // 2, dtype=s.dtype)
    t = st[:, :, :, :, None, :] * eye[None, None, :, None, :, None]
    return t.reshape(bsz, 2, HALF, HALF)


def _tiles_to_state(t):
    bsz = t.shape[0]
    t6 = t.reshape(bsz, 2, B_HEADS // 2, HEAD_DIM, B_HEADS // 2, HEAD_DIM)
    diag = jnp.stack([t6[:, :, h, :, h, :] for h in range(B_HEADS // 2)], axis=2)
    return jnp.swapaxes(diag.reshape(bsz, B_HEADS, HEAD_DIM, HEAD_DIM), -1, -2)


def _outproj_kernel(a_ref, b_ref, wa_ref, wb_ref, x_ref, o_ref):
    o_ref[...] = (x_ref[...] + _dot(a_ref[...].astype(BF16), wa_ref[...])
                  + _dot(b_ref[...].astype(BF16), wb_ref[...]))


def _out_proj(a, b, wa, wb, x, tm):
    n = x.shape[0]
    row = lambda i: (i, 0)
    whole = lambda i: (0, 0)
    return pl.pallas_call(
        _outproj_kernel,
        grid=(n // tm,),
        in_specs=[pl.BlockSpec((tm, A_WIDTH), row), pl.BlockSpec((tm, B_WIDTH), row),
                  pl.BlockSpec((A_WIDTH, D_MODEL), whole), pl.BlockSpec((B_WIDTH, D_MODEL), whole),
                  pl.BlockSpec((tm, D_MODEL), row)],
        out_specs=pl.BlockSpec((tm, D_MODEL), row),
        out_shape=jax.ShapeDtypeStruct((n, D_MODEL), F32),
        compiler_params=_cparams("parallel"),
        name="out_proj",
    )(a, b, wa, wb, x)


def _normmm_kernel(x_ref, g_ref, w_ref, o_ref):
    o_ref[...] = _dot(_rms(x_ref[...], g_ref[...]).astype(BF16), w_ref[...])


def _norm_matmul(x, g, w, tm):
    n, m = x.shape[0], w.shape[1]
    return pl.pallas_call(
        _normmm_kernel,
        grid=(n // tm,),
        in_specs=[pl.BlockSpec((tm, D_MODEL), lambda i: (i, 0)), pl.BlockSpec((1, D_MODEL), lambda i: (0, 0)),
                  pl.BlockSpec((D_MODEL, m), lambda i: (0, 0))],
        out_specs=pl.BlockSpec((tm, m), lambda i: (i, 0)),
        out_shape=jax.ShapeDtypeStruct((n, m), F32),
        compiler_params=_cparams("parallel"),
        name="norm_matmul",
    )(x, g, w)


def _memattn_kernel(x_ref, g_ref, wq_ref, mk_ref, mv_ref, wo_ref, o_ref):
    x = x_ref[...]
    q = _dot(_rms(x, g_ref[...]).astype(BF16), wq_ref[...])
    heads = []
    for h in range(MEM_HEADS):
        cols = slice(h * MEM_HEAD_DIM, (h + 1) * MEM_HEAD_DIM)
        s = lax.dot_general(q[:, cols].astype(BF16), mk_ref[0, :, cols], _NT,
                            preferred_element_type=F32) * (MEM_HEAD_DIM ** -0.5)
        p = jnp.exp(s - jnp.max(s, axis=-1, keepdims=True))
        p = p * (1.0 / jnp.sum(p, axis=-1, keepdims=True))
        heads.append(_dot(p.astype(BF16), mv_ref[0, :, cols]).astype(BF16))
    o_ref[...] = x + _dot(jnp.concatenate(heads, axis=1), wo_ref[...])


def _mem_attn(x, g, wq, mk, mv, wo, rows_per_batch, tm):
    n = x.shape[0]
    n_mem = mk.shape[1]
    per = rows_per_batch // tm
    row = lambda i: (i, 0)
    whole = lambda i: (0, 0)
    mem = lambda i: (i // per, 0, 0)
    return pl.pallas_call(
        _memattn_kernel,
        grid=(n // tm,),
        in_specs=[pl.BlockSpec((tm, D_MODEL), row), pl.BlockSpec((1, D_MODEL), whole),
                  pl.BlockSpec((D_MODEL, D_MODEL), whole),
                  pl.BlockSpec((1, n_mem, D_MODEL), mem), pl.BlockSpec((1, n_mem, D_MODEL), mem),
                  pl.BlockSpec((D_MODEL, D_MODEL), whole)],
        out_specs=pl.BlockSpec((tm, D_MODEL), row),
        out_shape=jax.ShapeDtypeStruct((n, D_MODEL), F32),
        compiler_params=_cparams("parallel"),
        name="mem_attn",
    )(x, g, wq, mk, mv, wo)


def _swiglu_mid(xn, w1, w3):
    a = _dot(xn, w1)
    return (a * _sigmoid(a) * _dot(xn, w3)).astype(BF16)


def _ffn_kernel(x_ref, g_ref, w1_ref, w3_ref, w2_ref, gf_ref, o_ref, xn_ref, *, final):
    j = pl.program_id(1)

    @pl.when(j == 0)
    def _():
        x = x_ref[...]
        xn_ref[...] = _rms(x, g_ref[...]).astype(BF16)
        o_ref[...] = x

    o_ref[...] += _dot(_swiglu_mid(xn_ref[...], w1_ref[...], w3_ref[...]), w2_ref[...])

    if final:
        @pl.when(j == pl.num_programs(1) - 1)
        def _():
            o_ref[...] = _rms(o_ref[...], gf_ref[...])


def _ffn(x, g, w1, w3, w2, gf, *, final, tm, tf):
    n = x.shape[0]
    d_ff = w1.shape[1]
    row = lambda i, j: (i, 0)
    whole = lambda i, j: (0, 0)
    return pl.pallas_call(
        functools.partial(_ffn_kernel, final=final),
        grid=(n // tm, d_ff // tf),
        in_specs=[pl.BlockSpec((tm, D_MODEL), row), pl.BlockSpec((1, D_MODEL), whole),
                  pl.BlockSpec((D_MODEL, tf), lambda i, j: (0, j)), pl.BlockSpec((D_MODEL, tf), lambda i, j: (0, j)),
                  pl.BlockSpec((tf, D_MODEL), lambda i, j: (j, 0)), pl.BlockSpec((1, D_MODEL), whole)],
        out_specs=pl.BlockSpec((tm, D_MODEL), row),
        out_shape=jax.ShapeDtypeStruct((n, D_MODEL), F32),
        scratch_shapes=[pltpu.VMEM((tm, D_MODEL), BF16)],
        compiler_params=_cparams("parallel", "arbitrary"),
        name="ffn",
    )(x, g, w1, w3, w2, gf)


def _router_kernel(x_ref, g_ref, rt_ref, comb_ref):
    xn = _rms(x_ref[...], g_ref[...])
    logits = lax.dot_general(rt_ref[...], xn, _NT, preferred_element_type=F32,
                             precision=lax.Precision.HIGHEST)
    e_idx = lax.broadcasted_iota(I32, logits.shape, 0)
    m1 = jnp.max(logits, axis=0, keepdims=True)
    i1 = jnp.min(jnp.where(logits == m1, e_idx, N_EXPERTS), axis=0, keepdims=True)
    rest = jnp.where(e_idx == i1, -jnp.inf, logits)
    m2 = jnp.max(rest, axis=0, keepdims=True)
    i2 = jnp.min(jnp.where(rest == m2, e_idx, N_EXPERTS), axis=0, keepdims=True)
    e2 = jnp.exp(m2 - m1)
    g1 = 1.0 / (1.0 + e2)
    comb_ref[...] = jnp.where(e_idx == i1, g1, 0.0) + jnp.where(e_idx == i2, e2 * g1, 0.0)


def _router(x, g, rt, tm):
    n = x.shape[0]
    return pl.pallas_call(
        _router_kernel,
        grid=(n // tm,),
        in_specs=[pl.BlockSpec((tm, D_MODEL), lambda i: (i, 0)), pl.BlockSpec((1, D_MODEL), lambda i: (0, 0)),
                  pl.BlockSpec((N_EXPERTS, D_MODEL), lambda i: (0, 0))],
        out_specs=pl.BlockSpec((N_EXPERTS, tm), lambda i: (0, i)),
        out_shape=jax.ShapeDtypeStruct((N_EXPERTS, n), F32),
        compiler_params=_cparams("parallel"),
        name="moe_router",
    )(x, g, rt)


def _moe_kernel(x_ref, g_ref, comb_ref, w1_ref, w3_ref, w2_ref, gf_ref, o_ref, xn_ref, *, final):
    e = pl.program_id(1)
    j = pl.program_id(2)

    @pl.when((e == 0) & (j == 0))
    def _():
        x = x_ref[...]
        xn_ref[...] = _rms(x, g_ref[...]).astype(BF16)
        o_ref[...] = x

    comb = comb_ref[...]
    lane = lax.broadcasted_iota(I32, comb.shape, 1)
    ce = jnp.sum(jnp.where(lane == e, comb, 0.0), axis=1, keepdims=True)
    o_ref[...] += ce * _dot(_swiglu_mid(xn_ref[...], w1_ref[0], w3_ref[0]), w2_ref[0])

    if final:
        @pl.when((e == pl.num_programs(1) - 1) & (j == pl.num_programs(2) - 1))
        def _():
            o_ref[...] = _rms(o_ref[...], gf_ref[...])


def _moe(x, g, comb, w1, w3, w2, gf, *, final, tm, tf):
    n = x.shape[0]
    d_ff = w1.shape[2]
    row = lambda i, e, j: (i, 0)
    whole = lambda i, e, j: (0, 0)
    return pl.pallas_call(
        functools.partial(_moe_kernel, final=final),
        grid=(n // tm, N_EXPERTS, d_ff // tf),
        in_specs=[pl.BlockSpec((tm, D_MODEL), row), pl.BlockSpec((1, D_MODEL), whole),
                  pl.BlockSpec((tm, N_EXPERTS), row),
                  pl.BlockSpec((1, D_MODEL, tf), lambda i, e, j: (e, 0, j)),
                  pl.BlockSpec((1, D_MODEL, tf), lambda i, e, j: (e, 0, j)),
                  pl.BlockSpec((1, tf, D_MODEL), lambda i, e, j: (e, j, 0)), pl.BlockSpec((1, D_MODEL), whole)],
        out_specs=pl.BlockSpec((tm, D_MODEL), row),
        out_shape=jax.ShapeDtypeStruct((n, D_MODEL), F32),
        scratch_shapes=[pltpu.VMEM((tm, D_MODEL), BF16)],
        compiler_params=_cparams("parallel", "arbitrary", "arbitrary"),
        name="moe",
    )(x, g, comb, w1, w3, w2, gf)


def _rmsnorm_kernel(x_ref, g_ref, o_ref):
    o_ref[...] = _rms(x_ref[...], g_ref[...])


def _rmsnorm(x, g, tm):
    n = x.shape[0]
    return pl.pallas_call(
        _rmsnorm_kernel,
        grid=(n // tm,),
        in_specs=[pl.BlockSpec((tm, D_MODEL), lambda i: (i, 0)), pl.BlockSpec((1, D_MODEL), lambda i: (0, 0))],
        out_specs=pl.BlockSpec((tm, D_MODEL), lambda i: (i, 0)),
        out_shape=jax.ShapeDtypeStruct((n, D_MODEL), F32),
        compiler_params=_cparams("parallel"),
        name="final_norm",
    )(x, g)


def _row_tile(n, cap):
    t = min(n, cap)
    assert n % t == 0, (n, cap)
    return t


def _ff_tile(d_ff):
    for parts in (2, 4, 11, 22):
        if d_ff % parts == 0 and (d_ff // parts) % LANES == 0:
            return d_ff // parts
    return d_ff


def _mixer(x, tabs, w, *, bsz, t_len, pos0, past, s0, layer):
    n = bsz * t_len
    tm = _row_tile(t_len, 256)
    qa, hb, kv = _in_proj(x, w["norm_mix"], w["w_in"], tabs, t_len, tm)
    ak = kv[:, 0:KV_WIDTH].reshape(bsz, t_len, A_KV_HEADS, HEAD_DIM)
    av = kv[:, KV_WIDTH:2 * KV_WIDTH].reshape(bsz, t_len, A_KV_HEADS, HEAD_DIM)
    iq = kv[:, 256:512].reshape(bsz, t_len, IDX_WIDTH)
    ik = kv[:, 512:512 + HEAD_DIM].reshape(bsz, t_len, HEAD_DIM)
    iw = kv[:, 512 + HEAD_DIM:512 + HEAD_DIM + IDX_HEADS].reshape(bsz, t_len, IDX_HEADS)
    if past is None:
        k_all, v_all, ik_all = ak, av, ik
    else:
        k_all = jnp.concatenate([past[0], ak], axis=1)
        v_all = jnp.concatenate([past[1], av], axis=1)
        ik_all = jnp.concatenate([past[2], ik], axis=1)
    l_real = k_all.shape[1]
    topk = min(TOPK_MAX, l_real // 4)
    tq = LANES
    kb_rows = 256 if l_real % 256 == 0 else LANES
    l_pad = -(-l_real // kb_rows) * kb_rows
    t_pad = -(-t_len // tq) * tq
    padk = lambda a: jnp.pad(a, ((0, 0), (0, l_pad - l_real)) + ((0, 0),) * (a.ndim - 2))
    padq = lambda a: jnp.pad(a, ((0, 0), (0, t_pad - t_len), (0, 0)))
    k_hm = jnp.swapaxes(padk(k_all), 1, 2).astype(BF16)
    v_hm = jnp.swapaxes(padk(v_all), 1, 2).astype(BF16)
    v_aug = jnp.concatenate([v_hm, jnp.ones(v_hm.shape[:3] + (1,), BF16),
                             jnp.zeros(v_hm.shape[:3] + (LANES - HEAD_DIM - 1,), BF16)], axis=-1)
    iw_t = jnp.pad(jnp.swapaxes(padq(iw), 1, 2), ((0, 0), (0, 8 - IDX_HEADS), (0, 0)))
    oa = _dsa(padq(qa.reshape(bsz, t_len, A_WIDTH)), padq(iq), iw_t, k_hm, v_aug, padk(ik_all).astype(BF16),
              l_real=l_real, pos0=pos0, kb_rows=kb_rows, topk=topk, tq=tq)
    oa = oa[:, :t_len].reshape(n, A_WIDTH)
    ob, s_tiles = _hgrn(hb, w["hgrn_lb"], w["gnorm"], _state_to_tiles(s0), layer=layer, bsz=bsz, t_len=t_len,
                        tb=_row_tile(t_len, 256))
    x1 = _out_proj(oa, ob, w["w_out_a"], w["w_out_b"], x, _row_tile(n, 512))
    return x1, ak, av, ik, _tiles_to_state(s_tiles)


def kernel(x_prompt, x_sample, cache_k, cache_v, cache_idx_k, state_hgrn, cache_mem_k, cache_mem_v, mem_prompt,
           norm_mix, w_in, hgrn_lb, hgrn_gnorm, w_out, norm_mem, norm_memkv, w_mq, w_mk, w_mv, w_mo,
           norm_ffn, ffn_w1, ffn_w3, ffn_w2, moe_router, moe_w1, moe_w3, moe_w2, norm_final):
    depth = w_in.shape[0]
    bp, tp, _ = x_prompt.shape
    bs, ts, _ = x_sample.shape
    past = cache_k.shape[2]
    n_mem = mem_prompt.shape[1]
    tabs_p = _rope_tables(jnp.arange(tp, dtype=I32))
    tabs_s = _rope_tables(past + jnp.arange(ts, dtype=I32))
    xp = x_prompt.reshape(bp * tp, D_MODEL)
    xs = x_sample.reshape(bs * ts, D_MODEL)
    mem = mem_prompt.reshape(bp * n_mem, D_MODEL)
    s0_p = jnp.zeros((bp, B_HEADS, HEAD_DIM, HEAD_DIM), F32)
    gfin = norm_final.reshape(1, D_MODEL)
    row2 = lambda a: a.reshape(1, -1)
    outs = {k: [] for k in ("kp", "vp", "ikp", "sp", "mkp", "mvp", "ks", "vs", "iks", "ss")}
    for l in range(depth):
        w = {"norm_mix": row2(norm_mix[l]), "w_in": _regroup_w_in(w_in[l]), "hgrn_lb": hgrn_lb.astype(F32),
             "gnorm": row2(jnp.tile(hgrn_gnorm[l], B_HEADS)),
             "w_out_a": w_out[l, :A_WIDTH].astype(BF16), "w_out_b": w_out[l, A_WIDTH:].astype(BF16)}
        xp, kp, vp, ikp, sp = _mixer(xp, tabs_p, w, bsz=bp, t_len=tp, pos0=0, past=None, s0=s0_p, layer=l)
        xs, ks, vs, iks, ss = _mixer(xs, tabs_s, w, bsz=bs, t_len=ts, pos0=past,
                                     past=(cache_k[l], cache_v[l], cache_idx_k[l]), s0=state_hgrn[l], layer=l)
        wkv = jnp.concatenate([w_mk[l], w_mv[l]], axis=1).astype(BF16)
        mkv = _norm_matmul(mem, row2(norm_memkv[l]), wkv, _row_tile(bp * n_mem, 256))
        mk_p = mkv[:, :D_MODEL].reshape(bp, n_mem, D_MODEL)
        mv_p = mkv[:, D_MODEL:].reshape(bp, n_mem, D_MODEL)
        wq, wo, gm = w_mq[l].astype(BF16), w_mo[l].astype(BF16), row2(norm_mem[l])
        xp = _mem_attn(xp, gm, wq, mk_p.astype(BF16), mv_p.astype(BF16), wo, tp, _row_tile(tp, 256))
        xs = _mem_attn(xs, gm, wq, cache_mem_k[l].reshape(bs, n_mem, D_MODEL).astype(BF16),
                       cache_mem_v[l].reshape(bs, n_mem, D_MODEL).astype(BF16), wo, ts, _row_tile(ts, 256))
        final = l == depth - 1
        gf = row2(norm_ffn[l])
        if l % 2 == 0:
            w1, w3, w2 = ffn_w1[l // 2].astype(BF16), ffn_w3[l // 2].astype(BF16), ffn_w2[l // 2].astype(BF16)
            tf = _ff_tile(w1.shape[1])
            xp = _ffn(xp, gf, w1, w3, w2, gfin, final=final, tm=_row_tile(bp * tp, 512), tf=tf)
            xs = _ffn(xs, gf, w1, w3, w2, gfin, final=final, tm=_row_tile(bs * ts, 512), tf=tf)
        else:
            w1, w3, w2 = moe_w1[l // 2].astype(BF16), moe_w3[l // 2].astype(BF16), moe_w2[l // 2].astype(BF16)
            rt = moe_router[l // 2].T
            tf = _ff_tile(w1.shape[2])
            comb_p = _router(xp, gf, rt, _row_tile(bp * tp, 512)).T
            comb_s = _router(xs, gf, rt, _row_tile(bs * ts, 512)).T
            xp = _moe(xp, gf, comb_p, w1, w3, w2, gfin, final=final, tm=_row_tile(bp * tp, 512), tf=tf)
            xs = _moe(xs, gf, comb_s, w1, w3, w2, gfin, final=final, tm=_row_tile(bs * ts, 512), tf=tf)
        outs["kp"].append(kp); outs["vp"].append(vp); outs["ikp"].append(ikp); outs["sp"].append(sp)
        outs["mkp"].append(mk_p.reshape(bp, n_mem, MEM_HEADS, MEM_HEAD_DIM))
        outs["mvp"].append(mv_p.reshape(bp, n_mem, MEM_HEADS, MEM_HEAD_DIM))
        outs["ks"].append(ks); outs["vs"].append(vs); outs["iks"].append(iks); outs["ss"].append(ss)
    y_prompt = xp.reshape(bp, tp, D_MODEL)
    y_sample = xs.reshape(bs, ts, D_MODEL)
    st = lambda k: jnp.stack(outs[k])
    return (y_prompt, y_sample, st("kp"), st("vp"), st("ikp"), st("sp"), st("mkp"), st("mvp"),
            st("ks"), st("vs"), st("iks"), st("ss"))
```

```python
import functools

import numpy as np
import jax
import jax.numpy as jnp
from jax import lax
from jax.experimental import pallas as pl
from jax.experimental.pallas import tpu as pltpu

F32 = jnp.float32
BF16 = jnp.bfloat16
I32 = jnp.int32
I16 = jnp.int16

D_MODEL = 1024
CHUNK = 64
EPS = 1e-6
ROPE_THETA = 500000.0
HEAD_DIM = 64
ROT_DIM = HEAD_DIM // 4
A_HEADS = 8
A_KV_HEADS = 2
A_GROUP = A_HEADS // A_KV_HEADS
A_WIDTH = A_HEADS * HEAD_DIM
KV_WIDTH = A_KV_HEADS * HEAD_DIM
IDX_HEADS = 4
IDX_WIDTH = IDX_HEADS * HEAD_DIM
TOPK_MAX = 256
B_HEADS = 8
B_WIDTH = B_HEADS * HEAD_DIM
B_CHUNK = 16
MEM_HEADS = 4
MEM_HEAD_DIM = D_MODEL // MEM_HEADS
N_EXPERTS = 8
LANES = 128
HALF = 256
IN_WIDTH_PAD = 3200
VMEM_LIMIT = 56 * 1024 * 1024
INT_MIN = -2147483648
NEG_BIG = -1e30
LOG2E = 1.4426950408889634

_NT = (((1,), (1,)), ((), ()))
_TN = (((0,), (0,)), ((), ()))


def _cparams(*sem):
    return pltpu.CompilerParams(dimension_semantics=sem, vmem_limit_bytes=VMEM_LIMIT)


def _rms(x, g):
    return x * lax.rsqrt(jnp.mean(x * x, axis=-1, keepdims=True) + EPS) * g


def _sigmoid(x):
    return 1.0 / (1.0 + jnp.exp(-x))


def _dot(a, b):
    return jnp.dot(a, b, preferred_element_type=F32)


def _split2(x):
    hi = x.astype(BF16)
    lo = (x - hi.astype(F32)).astype(BF16)
    return hi, lo


def _split3(x):
    hi = x.astype(BF16)
    r = x - hi.astype(F32)
    mid = r.astype(BF16)
    lo = (r - mid.astype(F32)).astype(BF16)
    return hi, mid, lo


def _rope128(xb, c, s1, s2):
    return xb * c + pltpu.roll(xb, LANES - ROT_DIM // 2, 1) * s1 + pltpu.roll(xb, ROT_DIM // 2, 1) * s2


def _inproj_kernel(x_ref, g_ref, w_ref, c_ref, s1_ref, s2_ref, qa_ref, hb_ref, kv_ref):
    xn = _rms(x_ref[...], g_ref[...]).astype(BF16)
    c, s1, s2 = c_ref[...], s1_ref[...], s2_ref[...]
    for j in range(A_WIDTH // LANES):
        h = _dot(xn, w_ref[:, j * LANES:(j + 1) * LANES])
        qa_ref[:, j * LANES:(j + 1) * LANES] = _rope128(h, c, s1, s2)
    for j in range(4):
        lo = A_WIDTH + j * B_WIDTH
        hb_ref[:, j * B_WIDTH:(j + 1) * B_WIDTH] = _dot(xn, w_ref[:, lo:lo + B_WIDTH])
    base = A_WIDTH + 4 * B_WIDTH
    kv_ref[:, 0:128] = _rope128(_dot(xn, w_ref[:, base:base + 128]), c, s1, s2)
    kv_ref[:, 128:256] = _dot(xn, w_ref[:, base + 128:base + 256])
    kv_ref[:, 256:384] = _rope128(_dot(xn, w_ref[:, base + 256:base + 384]), c, s1, s2)
    kv_ref[:, 384:512] = _rope128(_dot(xn, w_ref[:, base + 384:base + 512]), c, s1, s2)
    first = lax.broadcasted_iota(I32, c.shape, 1) < HEAD_DIM
    kv_ref[:, 512:640] = _rope128(_dot(xn, w_ref[:, base + 512:base + 640]),
                                  jnp.where(first, c, 1.0), jnp.where(first, s1, 0.0), jnp.where(first, s2, 0.0))


def _in_proj(x, g, w, tabs, t_len, tm):
    n = x.shape[0]
    nt = t_len // tm
    row = lambda i: (i, 0)
    tab = lambda i: (i % nt, 0)
    whole = lambda i: (0, 0)
    return pl.pallas_call(
        _inproj_kernel,
        grid=(n // tm,),
        in_specs=[pl.BlockSpec((tm, D_MODEL), row), pl.BlockSpec((1, D_MODEL), whole),
                  pl.BlockSpec((D_MODEL, IN_WIDTH_PAD), whole),
                  pl.BlockSpec((tm, LANES), tab), pl.BlockSpec((tm, LANES), tab), pl.BlockSpec((tm, LANES), tab)],
        out_specs=[pl.BlockSpec((tm, A_WIDTH), row), pl.BlockSpec((tm, 4 * B_WIDTH), row),
                   pl.BlockSpec((tm, 640), row)],
        out_shape=[jax.ShapeDtypeStruct((n, A_WIDTH), F32), jax.ShapeDtypeStruct((n, 4 * B_WIDTH), F32),
                   jax.ShapeDtypeStruct((n, 640), F32)],
        compiler_params=_cparams("parallel"),
        name="in_proj",
    )(x, g, w, *tabs)


def _rope_tables(pos):
    half = ROT_DIM // 2
    inv_freq = 1.0 / (ROPE_THETA ** (jnp.arange(half, dtype=F32) * (2.0 / ROT_DIM)))
    ang = pos.astype(F32)[:, None] * inv_freq[None, :]
    cos, sin = jnp.cos(ang), jnp.sin(ang)
    t = pos.shape[0]
    pad = HEAD_DIM - ROT_DIM
    c = jnp.concatenate([cos, cos, jnp.ones((t, pad), F32)], axis=1)
    s1 = jnp.concatenate([-sin, jnp.zeros((t, half + pad), F32)], axis=1)
    s2 = jnp.concatenate([jnp.zeros((t, half), F32), sin, jnp.zeros((t, pad), F32)], axis=1)
    rep = LANES // HEAD_DIM
    return tuple(jnp.tile(a, (1, rep)) for a in (c, s1, s2))


def _regroup_w_in(w):
    o = np.cumsum([0, A_WIDTH, KV_WIDTH, KV_WIDTH, IDX_WIDTH, HEAD_DIM, IDX_HEADS, B_WIDTH, B_WIDTH, B_WIDTH, B_WIDTH])
    seg = lambda i: w[:, o[i]:o[i + 1]]
    used = A_WIDTH + 4 * B_WIDTH + 2 * KV_WIDTH + IDX_WIDTH + HEAD_DIM + IDX_HEADS
    pad = jnp.zeros((w.shape[0], IN_WIDTH_PAD - used), w.dtype)
    return jnp.concatenate([seg(0), seg(6), seg(7), seg(8), seg(9), seg(1), seg(2), seg(3), seg(4), seg(5), pad],
                           axis=1).astype(BF16)


def _dsa_kernel(q_ref, iq_ref, iw_ref, k_ref, v_ref, ik_ref, o_ref, key_ref, hi_ref, lo_ref, acc_ref, m_ref,
                s0_ref, s1_ref, alpha_ref, p_ref, bias_ref, *, l_real, pos0, kb_rows, topk, tq, idx_bits):
    i = pl.program_id(1)
    qpos0 = pos0 + i * tq
    kend = jnp.minimum(((qpos0 + tq - 1) // CHUNK + 1) * CHUNK, l_real)
    cb_rows = 2 * kb_rows
    nkc = (kend + cb_rows - 1) // cb_rows
    nkb = 2 * nkc
    qchunk = (qpos0 + lax.broadcasted_iota(I32, (1, tq), 1)) // CHUNK
    row_iota = lax.broadcasted_iota(I32, (kb_rows, tq), 0)
    crow_iota = lax.broadcasted_iota(I32, (cb_rows, tq), 0)

    iq = iq_ref[0].astype(BF16)
    iq4 = jnp.concatenate([iq[:, h * HEAD_DIM:(h + 1) * HEAD_DIM] for h in range(IDX_HEADS)], axis=0)
    iw = iw_ref[0] * (IDX_HEADS ** -0.5 * HEAD_DIM ** -0.5)

    def score_body(kb, carry):
        r0 = pl.multiple_of(kb * kb_rows, kb_rows)
        ikb = ik_ref[0, pl.ds(r0, kb_rows), :]
        lg4 = lax.dot_general(ikb, iq4, _NT, preferred_element_type=F32)
        acc = jnp.zeros((kb_rows, tq), F32)
        for h in range(IDX_HEADS):
            acc = acc + iw[h:h + 1, :] * jnp.maximum(lg4[:, h * tq:(h + 1) * tq], 0.0)
        bits = pltpu.bitcast(acc + 0.0, I32)
        key = bits ^ ((bits >> 31) & 0x7FFFFFFF)
        kpos = r0 + row_iota
        adm = (kpos // CHUNK <= qchunk) & (kpos < l_real)
        key = jnp.where(adm, key, INT_MIN)
        key_ref[pl.ds(r0, kb_rows), :] = key
        hi_ref[pl.ds(r0, kb_rows), :] = (key >> 16).astype(I16)
        return carry

    lax.fori_loop(0, nkb, score_body, 0)

    def count(pred):
        def body(kc, c):
            r0 = pl.multiple_of(kc * cb_rows, cb_rows)
            hit = pred(key_ref[pl.ds(r0, cb_rows), :], r0)
            return c + jnp.where(hit, 1, 0).reshape(cb_rows // 8, 8, tq).sum(axis=0)
        c = lax.fori_loop(0, nkc, body, jnp.zeros((8, tq), I32))
        return c.sum(axis=0, keepdims=True)

    def count16(ref, pred):
        one, zero = jnp.ones((), BF16), jnp.zeros((), BF16)

        def body(kc, c):
            r0 = pl.multiple_of(kc * cb_rows, cb_rows)
            w = jnp.where(pred(ref[pl.ds(r0, cb_rows), :]), one, zero).reshape(cb_rows // 16, 16, tq)
            parts = [w[r] for r in range(cb_rows // 16)]
            while len(parts) > 1:
                parts = [a + b for a, b in zip(parts[::2], parts[1::2])] + parts[len(parts) & ~1:]
            return c + parts[0]
        c = lax.fori_loop(0, nkc, body, jnp.zeros((16, tq), BF16))
        return c.astype(F32).sum(axis=0, keepdims=True).astype(I32)

    def bisect16(ref, kth):
        c0 = count16(ref, lambda blk: blk >= jnp.zeros((), I16))
        start = jnp.where(c0 >= kth, 0, -32768).astype(I32)

        def bit_body(it, prefix):
            cand = prefix | lax.shift_left(jnp.int32(1), 14 - it)
            c = count16(ref, lambda blk: blk >= cand.astype(I16))
            return jnp.where(c >= kth, cand, prefix)

        return lax.fori_loop(0, 15, bit_body, start)

    tau_hi = bisect16(hi_ref, topk)
    tau_hi16 = tau_hi.astype(I16)
    kth_lo = topk - count16(hi_ref, lambda blk: blk > tau_hi16)

    def lo_body(kc, carry):
        r0 = pl.multiple_of(kc * cb_rows, cb_rows)
        lo = ((key_ref[pl.ds(r0, cb_rows), :] & 0xFFFF) - 32768).astype(I16)
        lo_ref[pl.ds(r0, cb_rows), :] = jnp.where(hi_ref[pl.ds(r0, cb_rows), :] == tau_hi16, lo,
                                                 jnp.full((), -32768, I16))
        return carry

    lax.fori_loop(0, nkc, lo_body, 0)
    tau_lo = bisect16(lo_ref, kth_lo)
    tau = lax.shift_left(tau_hi, 16) | (tau_lo + 32768)
    n_ge = count(lambda blk, r0: blk >= tau)

    def tie_index():
        need = topk - count(lambda blk, r0: blk > tau)

        def idx_body(it, prefix):
            cand = prefix | lax.shift_left(jnp.int32(1), idx_bits - 1 - it)
            c = count(lambda blk, r0: (blk == tau) & (r0 + crow_iota < cand))
            return jnp.where(c < need, cand, prefix)

        return lax.fori_loop(0, idx_bits, idx_body, jnp.zeros((1, tq), I32))

    surplus = jnp.max(jnp.where((n_ge > topk) & (tau != INT_MIN), 1, 0))
    jstar = lax.cond(surplus > 0, tie_index, lambda: jnp.full((1, tq), 2 ** idx_bits, I32))

    lane_rep = kb_rows // LANES
    tau_c = jnp.tile(jnp.broadcast_to(tau, (LANES, tq)).T, (1, lane_rep))
    jst_c = jnp.tile(jnp.broadcast_to(jstar, (LANES, tq)).T, (1, lane_rep))
    col_iota = lax.broadcasted_iota(I32, (tq, kb_rows), 1)
    acc_ref[...] = jnp.zeros(acc_ref.shape, F32)
    m_ref[...] = jnp.full(m_ref.shape, NEG_BIG, F32)
    qs = (q_ref[0] * (HEAD_DIM ** -0.5 * LOG2E)).astype(BF16)
    q4 = [jnp.concatenate([qs[:, (j * A_GROUP + g) * HEAD_DIM:(j * A_GROUP + g + 1) * HEAD_DIM]
                           for g in range(A_GROUP)], axis=0) for j in range(A_KV_HEADS)]

    def qk_scores(kb, s_slot):
        r0 = pl.multiple_of(jnp.minimum(kb, nkb - 1) * kb_rows, kb_rows)
        for j in range(A_KV_HEADS):
            s_slot[j] = lax.dot_general(q4[j], k_ref[0, j, pl.ds(r0, kb_rows), :], _NT,
                                        preferred_element_type=F32)

    def softmax_pv(kb, s_slot):
        r0 = pl.multiple_of(kb * kb_rows, kb_rows)
        key_t = key_ref[pl.ds(r0, kb_rows), :].T
        sel = ((key_t > tau_c) | ((key_t == tau_c) & (r0 + col_iota <= jst_c))) & (key_t != INT_MIN)
        bias_ref[...] = jnp.where(sel, 0.0, NEG_BIG)
        for j in range(A_KV_HEADS):
            for g in range(A_GROUP):
                rows = slice(g * tq, (g + 1) * tq)
                m_old = m_ref[j, g]
                m_new = jnp.maximum(m_old, jnp.max(s_slot[j, rows, :] + bias_ref[...], axis=1, keepdims=True))
                alpha_ref[j, g] = jnp.exp2(m_old - m_new)
                m_ref[j, g] = m_new
            for g in range(A_GROUP):
                rows = slice(g * tq, (g + 1) * tq)
                sm = s_slot[j, rows, :] + bias_ref[...]
                p_ref[j, rows, :] = jnp.exp2(sm - jnp.tile(m_ref[j, g], (1, lane_rep))).astype(BF16)
            pv = _dot(p_ref[j], v_ref[0, j, pl.ds(r0, kb_rows), :])
            acc_ref[j] = alpha_ref[j] * acc_ref[j] + pv.reshape(A_GROUP, tq, LANES)

    qk_scores(0, s0_ref)

    def pair_body(i, carry):
        kb = 2 * i
        qk_scores(kb + 1, s1_ref)
        softmax_pv(kb, s0_ref)
        qk_scores(kb + 2, s0_ref)
        softmax_pv(kb + 1, s1_ref)
        return carry

    lax.fori_loop(0, nkc, pair_body, 0)

    outs = []
    for j in range(A_KV_HEADS):
        acc = acc_ref[j]
        for g in range(A_GROUP):
            outs.append(acc[g, :, :HEAD_DIM] * (1.0 / acc[g, :, HEAD_DIM:HEAD_DIM + 1]))
    o_ref[0] = jnp.concatenate(outs, axis=1)


def _dsa(q, iq, iw_t, k_hm, v_aug, ik, *, l_real, pos0, kb_rows, topk, tq):
    bsz, t_len, _ = q.shape
    l_pad = ik.shape[1]
    idx_bits = max(1, int(l_pad - 1).bit_length())
    assert l_pad // 16 <= 256, "packed hit counts are summed in bf16, exact only up to 256 per slot"
    kern = functools.partial(_dsa_kernel, l_real=l_real, pos0=pos0, kb_rows=kb_rows, topk=topk, tq=tq,
                             idx_bits=idx_bits)
    return pl.pallas_call(
        kern,
        grid=(bsz, t_len // tq),
        in_specs=[pl.BlockSpec((1, tq, A_WIDTH), lambda b, i: (b, i, 0)),
                  pl.BlockSpec((1, tq, IDX_WIDTH), lambda b, i: (b, i, 0)),
                  pl.BlockSpec((1, 8, tq), lambda b, i: (b, 0, i)),
                  pl.BlockSpec((1, A_KV_HEADS, l_pad, HEAD_DIM), lambda b, i: (b, 0, 0, 0)),
                  pl.BlockSpec((1, A_KV_HEADS, l_pad, LANES), lambda b, i: (b, 0, 0, 0)),
                  pl.BlockSpec((1, l_pad, HEAD_DIM), lambda b, i: (b, 0, 0))],
        out_specs=pl.BlockSpec((1, tq, A_WIDTH), lambda b, i: (b, i, 0)),
        out_shape=jax.ShapeDtypeStruct((bsz, t_len, A_WIDTH), F32),
        scratch_shapes=[pltpu.VMEM((l_pad, tq), I32), pltpu.VMEM((l_pad, tq), I16), pltpu.VMEM((l_pad, tq), I16),
                        pltpu.VMEM((A_KV_HEADS, A_GROUP, tq, LANES), F32),
                        pltpu.VMEM((A_KV_HEADS, A_GROUP, tq, LANES), F32),
                        pltpu.VMEM((A_KV_HEADS, A_GROUP * tq, kb_rows), F32),
                        pltpu.VMEM((A_KV_HEADS, A_GROUP * tq, kb_rows), F32),
                        pltpu.VMEM((A_KV_HEADS, A_GROUP, tq, LANES), F32),
                        pltpu.VMEM((A_KV_HEADS, A_GROUP * tq, kb_rows), BF16),
                        pltpu.VMEM((tq, kb_rows), F32)],
        compiler_params=_cparams("parallel", "arbitrary"),
        name="dsa",
    )(q, iq, iw_t, k_hm, v_aug, ik)


def _hgrn_kernel(bq_ref, bf_ref, bi_ref, bg_ref, lb_ref, gn_ref, s0_ref, o_ref, sfin_ref,
                 st_ref, q_s, kk_s, b_s, qe_s, kd_s, dec_s, o_s, e_s, *, layer, tb):
    tstep = pl.program_id(1)
    nchunk = tb // B_CHUNK

    @pl.when(tstep == 0)
    def _():
        st_ref[...] = s0_ref[0]

    lbr = lb_ref[...]
    lbe = jnp.exp(lbr - jnp.max(lbr, axis=0, keepdims=True))
    lbs = lbe / jnp.sum(lbe, axis=0, keepdims=True)
    cum = lbs[0:1, :]
    first = cum
    for r in range(1, layer + 1):
        cum = cum + lbs[r:r + 1, :]
    lb = cum - first

    f = lb + (1.0 - lb) * _sigmoid(bf_ref[...])
    bq = bq_ref[...]
    q = bq * _sigmoid(bq) * (HEAD_DIM ** -0.5)
    g3 = _split3(jnp.log(f))
    ri = lax.broadcasted_iota(I32, (tb, tb), 0)
    ci = lax.broadcasted_iota(I32, (tb, tb), 1)
    same = (ri // B_CHUNK) == (ci // B_CHUNK)
    tri = jnp.where(same & (ci <= ri), 1.0, 0.0).astype(BF16)
    ones = jnp.where(same, 1.0, 0.0).astype(BF16)
    b = _dot(tri, g3[0]) + _dot(tri, g3[1]) + _dot(tri, g3[2])
    blast = _dot(ones, g3[0]) + _dot(ones, g3[1]) + _dot(ones, g3[2])
    q_s[...] = q
    kk_s[...] = 1.0 - f
    b_s[...] = b
    qe_s[...] = q * jnp.exp(b)
    kd_s[...] = (1.0 - f) * jnp.exp(blast - b)
    dec_s[...] = jnp.exp(blast)

    hi = lax.broadcasted_iota(I32, (HALF, HALF), 0) // HEAD_DIM
    hj = lax.broadcasted_iota(I32, (HALF, HALF), 1) // HEAD_DIM
    bd_mask = hi == hj
    bd = jnp.where(bd_mask, 1.0, 0.0).astype(BF16)
    t_iota = lax.broadcasted_iota(I32, (B_CHUNK, B_WIDTH), 0)

    def seg_sum(x):
        out = []
        for half in range(2):
            xh, xl = _split2(x[:, half * HALF:(half + 1) * HALF])
            out.append(_dot(xh, bd) + _dot(xl, bd))
        return jnp.concatenate(out, axis=1)

    def chunk_body(c, carry):
        r0 = pl.multiple_of(c * B_CHUNK, B_CHUNK)
        rows = pl.ds(r0, B_CHUNK)
        qc, kc, bc, vc = q_s[rows, :], kk_s[rows, :], b_s[rows, :], bi_ref[rows, :]
        qe, kd = qe_s[rows, :].astype(BF16), kd_s[rows, :].astype(BF16)
        dec = dec_s[pl.ds(r0, 1), :]
        for s in range(B_CHUNK):
            ex = jnp.exp(jnp.where(t_iota >= s, bc - bc[s:s + 1, :], NEG_BIG))
            e_s[s * B_CHUNK:(s + 1) * B_CHUNK, :] = qc * kc[s:s + 1, :] * ex
        aexp = seg_sum(e_s[...])
        o = jnp.zeros((B_CHUNK, B_WIDTH), F32)
        for s in range(B_CHUNK):
            o = o + aexp[s * B_CHUNK:(s + 1) * B_CHUNK, :] * vc[s:s + 1, :]
        inter = []
        vb = vc.astype(BF16)
        for half in range(2):
            cols = slice(half * HALF, (half + 1) * HALF)
            st = st_ref[half]
            inter.append(lax.dot_general(qe[:, cols], st.astype(BF16), _NT, preferred_element_type=F32))
            upd = lax.dot_general(vb[:, cols], kd[:, cols], _TN, preferred_element_type=F32)
            st_ref[half] = st * dec[:, cols] + jnp.where(bd_mask, upd, 0.0)
        o_s[rows, :] = o + jnp.concatenate(inter, axis=1)
        return carry

    lax.fori_loop(0, nchunk, chunk_body, 0)

    o = o_s[...]
    ms = seg_sum(o * o) * (1.0 / HEAD_DIM)
    bg = bg_ref[...]
    o_ref[...] = o * lax.rsqrt(ms + EPS) * gn_ref[...] * (bg * _sigmoid(bg))

    @pl.when(tstep == pl.num_programs(1) - 1)
    def _():
        sfin_ref[0] = st_ref[...]


def _hgrn(hb, lb_raw, gn_tiled, s0_bd, *, layer, bsz, t_len, tb):
    nt = t_len // tb
    col = lambda j: (lambda b, t: (b * nt + t, j))
    scr = lambda r: pltpu.VMEM((r, B_WIDTH), F32)
    kern = functools.partial(_hgrn_kernel, layer=layer, tb=tb)
    return pl.pallas_call(
        kern,
        grid=(bsz, nt),
        in_specs=[pl.BlockSpec((tb, B_WIDTH), col(0)), pl.BlockSpec((tb, B_WIDTH), col(1)),
                  pl.BlockSpec((tb, B_WIDTH), col(2)), pl.BlockSpec((tb, B_WIDTH), col(3)),
                  pl.BlockSpec(lb_raw.shape, lambda b, t: (0, 0)),
                  pl.BlockSpec((1, B_WIDTH), lambda b, t: (0, 0)),
                  pl.BlockSpec((1, 2, HALF, HALF), lambda b, t: (b, 0, 0, 0))],
        out_specs=[pl.BlockSpec((tb, B_WIDTH), lambda b, t: (b * nt + t, 0)),
                   pl.BlockSpec((1, 2, HALF, HALF), lambda b, t: (b, 0, 0, 0))],
        out_shape=[jax.ShapeDtypeStruct((bsz * t_len, B_WIDTH), F32),
                   jax.ShapeDtypeStruct((bsz, 2, HALF, HALF), F32)],
        scratch_shapes=[pltpu.VMEM((2, HALF, HALF), F32), scr(tb), scr(tb), scr(tb), scr(tb), scr(tb), scr(tb),
                        scr(tb), scr(B_CHUNK * B_CHUNK)],
        compiler_params=_cparams("parallel", "arbitrary"),
        name="hgrn2",
    )(hb, hb, hb, hb, lb_raw, gn_tiled, s0_bd)


def _state_to_tiles(s):
    bsz = s.shape[0]
    st = jnp.swapaxes(s, -1, -2).reshape(bsz, 2, B_HEADS // 2, HEAD_DIM, HEAD_DIM)
    eye = jnp.eye(B_HEADS // 2, dtype=s.dtype)
    t = st[:, :, :, :, None, :] * eye[None, None, :, None, :, None]
    return t.reshape(bsz, 2, HALF, HALF)


def _tiles_to_state(t):
    bsz = t.shape[0]
    t6 = t.reshape(bsz, 2, B_HEADS // 2, HEAD_DIM, B_HEADS // 2, HEAD_DIM)
    diag = jnp.stack([t6[:, :, h, :, h, :] for h in range(B_HEADS // 2)], axis=2)
    return jnp.swapaxes(diag.reshape(bsz, B_HEADS, HEAD_DIM, HEAD_DIM), -1, -2)


def _outproj_kernel(a_ref, b_ref, wa_ref, wb_ref, x_ref, o_ref):
    o_ref[...] = (x_ref[...] + _dot(a_ref[...].astype(BF16), wa_ref[...])
                  + _dot(b_ref[...].astype(BF16), wb_ref[...]))


def _out_proj(a, b, wa, wb, x, tm):
    n = x.shape[0]
    row = lambda i: (i, 0)
    whole = lambda i: (0, 0)
    return pl.pallas_call(
        _outproj_kernel,
        grid=(n // tm,),
        in_specs=[pl.BlockSpec((tm, A_WIDTH), row), pl.BlockSpec((tm, B_WIDTH), row),
                  pl.BlockSpec((A_WIDTH, D_MODEL), whole), pl.BlockSpec((B_WIDTH, D_MODEL), whole),
                  pl.BlockSpec((tm, D_MODEL), row)],
        out_specs=pl.BlockSpec((tm, D_MODEL), row),
        out_shape=jax.ShapeDtypeStruct((n, D_MODEL), F32),
        compiler_params=_cparams("parallel"),
        name="out_proj",
    )(a, b, wa, wb, x)


def _normmm_kernel(x_ref, g_ref, w_ref, o_ref):
    o_ref[...] = _dot(_rms(x_ref[...], g_ref[...]).astype(BF16), w_ref[...])


def _norm_matmul(x, g, w, tm):
    n, m = x.shape[0], w.shape[1]
    return pl.pallas_call(
        _normmm_kernel,
        grid=(n // tm,),
        in_specs=[pl.BlockSpec((tm, D_MODEL), lambda i: (i, 0)), pl.BlockSpec((1, D_MODEL), lambda i: (0, 0)),
                  pl.BlockSpec((D_MODEL, m), lambda i: (0, 0))],
        out_specs=pl.BlockSpec((tm, m), lambda i: (i, 0)),
        out_shape=jax.ShapeDtypeStruct((n, m), F32),
        compiler_params=_cparams("parallel"),
        name="norm_matmul",
    )(x, g, w)


def _memattn_kernel(x_ref, g_ref, wq_ref, mk_ref, mv_ref, wo_ref, o_ref):
    x = x_ref[...]
    q = _dot(_rms(x, g_ref[...]).astype(BF16), wq_ref[...])
    heads = []
    for h in range(MEM_HEADS):
        cols = slice(h * MEM_HEAD_DIM, (h + 1) * MEM_HEAD_DIM)
        s = lax.dot_general(q[:, cols].astype(BF16), mk_ref[0, :, cols], _NT,
                            preferred_element_type=F32) * (MEM_HEAD_DIM ** -0.5)
        p = jnp.exp(s - jnp.max(s, axis=-1, keepdims=True))
        p = p * (1.0 / jnp.sum(p, axis=-1, keepdims=True))
        heads.append(_dot(p.astype(BF16), mv_ref[0, :, cols]).astype(BF16))
    o_ref[...] = x + _dot(jnp.concatenate(heads, axis=1), wo_ref[...])


def _mem_attn(x, g, wq, mk, mv, wo, rows_per_batch, tm):
    n = x.shape[0]
    n_mem = mk.shape[1]
    per = rows_per_batch // tm
    row = lambda i: (i, 0)
    whole = lambda i: (0, 0)
    mem = lambda i: (i // per, 0, 0)
    return pl.pallas_call(
        _memattn_kernel,
        grid=(n // tm,),
        in_specs=[pl.BlockSpec((tm, D_MODEL), row), pl.BlockSpec((1, D_MODEL), whole),
                  pl.BlockSpec((D_MODEL, D_MODEL), whole),
                  pl.BlockSpec((1, n_mem, D_MODEL), mem), pl.BlockSpec((1, n_mem, D_MODEL), mem),
                  pl.BlockSpec((D_MODEL, D_MODEL), whole)],
        out_specs=pl.BlockSpec((tm, D_MODEL), row),
        out_shape=jax.ShapeDtypeStruct((n, D_MODEL), F32),
        compiler_params=_cparams("parallel"),
        name="mem_attn",
    )(x, g, wq, mk, mv, wo)


def _swiglu_mid(xn, w1, w3):
    a = _dot(xn, w1)
    return (a * _sigmoid(a) * _dot(xn, w3)).astype(BF16)


def _ffn_kernel(x_ref, g_ref, w1_ref, w3_ref, w2_ref, gf_ref, o_ref, xn_ref, *, final):
    j = pl.program_id(1)

    @pl.when(j == 0)
    def _():
        x = x_ref[...]
        xn_ref[...] = _rms(x, g_ref[...]).astype(BF16)
        o_ref[...] = x

    o_ref[...] += _dot(_swiglu_mid(xn_ref[...], w1_ref[...], w3_ref[...]), w2_ref[...])

    if final:
        @pl.when(j == pl.num_programs(1) - 1)
        def _():
            o_ref[...] = _rms(o_ref[...], gf_ref[...])


def _ffn(x, g, w1, w3, w2, gf, *, final, tm, tf):
    n = x.shape[0]
    d_ff = w1.shape[1]
    row = lambda i, j: (i, 0)
    whole = lambda i, j: (0, 0)
    return pl.pallas_call(
        functools.partial(_ffn_kernel, final=final),
        grid=(n // tm, d_ff // tf),
        in_specs=[pl.BlockSpec((tm, D_MODEL), row), pl.BlockSpec((1, D_MODEL), whole),
                  pl.BlockSpec((D_MODEL, tf), lambda i, j: (0, j)), pl.BlockSpec((D_MODEL, tf), lambda i, j: (0, j)),
                  pl.BlockSpec((tf, D_MODEL), lambda i, j: (j, 0)), pl.BlockSpec((1, D_MODEL), whole)],
        out_specs=pl.BlockSpec((tm, D_MODEL), row),
        out_shape=jax.ShapeDtypeStruct((n, D_MODEL), F32),
        scratch_shapes=[pltpu.VMEM((tm, D_MODEL), BF16)],
        compiler_params=_cparams("parallel", "arbitrary"),
        name="ffn",
    )(x, g, w1, w3, w2, gf)


def _router_kernel(x_ref, g_ref, rt_ref, comb_ref, sel_ref):
    xn = _rms(x_ref[...], g_ref[...])
    logits = lax.dot_general(rt_ref[...], xn, _NT, preferred_element_type=F32,
                             precision=lax.Precision.HIGHEST)
    e_idx = lax.broadcasted_iota(I32, logits.shape, 0)
    m1 = jnp.max(logits, axis=0, keepdims=True)
    i1 = jnp.min(jnp.where(logits == m1, e_idx, N_EXPERTS), axis=0, keepdims=True)
    rest = jnp.where(e_idx == i1, -jnp.inf, logits)
    m2 = jnp.max(rest, axis=0, keepdims=True)
    i2 = jnp.min(jnp.where(rest == m2, e_idx, N_EXPERTS), axis=0, keepdims=True)
    e2 = jnp.exp(m2 - m1)
    g1 = 1.0 / (1.0 + e2)
    comb_ref[...] = jnp.where(e_idx == i1, g1, 0.0) + jnp.where(e_idx == i2, e2 * g1, 0.0)
    sel_ref[...] = jnp.where((e_idx == i1) | (e_idx == i2), 1.0, 0.0)


def _router(x, g, rt, tm):
    n = x.shape[0]
    return pl.pallas_call(
        _router_kernel,
        grid=(n // tm,),
        in_specs=[pl.BlockSpec((tm, D_MODEL), lambda i: (i, 0)), pl.BlockSpec((1, D_MODEL), lambda i: (0, 0)),
                  pl.BlockSpec((N_EXPERTS, D_MODEL), lambda i: (0, 0))],
        out_specs=[pl.BlockSpec((N_EXPERTS, tm), lambda i: (0, i)), pl.BlockSpec((N_EXPERTS, tm), lambda i: (0, i))],
        out_shape=[jax.ShapeDtypeStruct((N_EXPERTS, n), F32), jax.ShapeDtypeStruct((N_EXPERTS, n), F32)],
        compiler_params=_cparams("parallel"),
        name="moe_router",
    )(x, g, rt)


def _moe_kernel(x_ref, g_ref, selt_ref, sel_ref, comb_ref, w1_ref, w3_ref, w2_ref, gf_ref, o_ref,
                xn_ref, posr_ref, posc_ref, cnt_ref, xg_ref, yg_ref, *, final, nb, sub):
    e = pl.program_id(1)
    j = pl.program_id(2)
    last_j = pl.num_programs(2) - 1

    @pl.when((e == 0) & (j == 0))
    def _():
        x = x_ref[...]
        xn_ref[...] = _rms(x, g_ref[...]).astype(BF16)
        o_ref[...] = x
        ri = lax.broadcasted_iota(I32, (sub, sub), 0)
        ci = lax.broadcasted_iota(I32, (sub, sub), 1)
        before_r = jnp.where(ri < ci, 1.0, 0.0).astype(BF16)
        before_c = jnp.where(ci < ri, 1.0, 0.0).astype(BF16)
        run_r = jnp.zeros((N_EXPERTS, 1), F32)
        run_c = jnp.zeros((1, N_EXPERTS), F32)
        for c in range(nb // sub):
            cols = slice(c * sub, (c + 1) * sub)
            mr = selt_ref[:, cols]
            posr_ref[:, cols] = jnp.where(mr > 0.0, _dot(mr.astype(BF16), before_r) + run_r, -1.0)
            run_r = run_r + jnp.sum(mr, axis=1, keepdims=True)
            mc = sel_ref[cols, :]
            posc_ref[cols, :] = jnp.where(mc > 0.0, _dot(before_c, mc.astype(BF16)) + run_c, -1.0)
            run_c = run_c + jnp.sum(mc, axis=0, keepdims=True)
        for k in range(N_EXPERTS):
            cnt_ref[k] = run_c[0, k].astype(I32)

    nsub = (cnt_ref[e] + sub - 1) // sub
    def gather_matrix(r0):
        slot_r = posr_ref[pl.ds(e, 1), :]
        want = (r0 + lax.broadcasted_iota(I32, (sub, nb), 0)).astype(F32)
        return jnp.where(slot_r == want, 1.0, 0.0).astype(BF16)

    @pl.when(j == 0)
    def _():
        def body(s, carry):
            r0 = pl.multiple_of(s * sub, sub)
            xg_ref[pl.ds(r0, sub), :] = _dot(gather_matrix(r0), xn_ref[...]).astype(BF16)
            return carry
        lax.fori_loop(0, nsub, body, 0)

    def ffn_body(s, carry):
        rows = pl.ds(pl.multiple_of(s * sub, sub), sub)
        part = _dot(_swiglu_mid(xg_ref[rows, :], w1_ref[0], w3_ref[0]), w2_ref[0])

        @pl.when(j == 0)
        def _():
            yg_ref[rows, :] = part

        @pl.when(j > 0)
        def _():
            yg_ref[rows, :] += part
        return carry

    lax.fori_loop(0, nsub, ffn_body, 0)

    @pl.when(j == last_j)
    def _():
        c3 = _split3(comb_ref[...])
        lane_e = lax.broadcasted_iota(I32, (sub, N_EXPERTS), 1) == e
        lane8 = lax.broadcasted_iota(I32, (nb, N_EXPERTS), 1)
        slot_c = jnp.sum(jnp.where(lane8 == e, posc_ref[...], 0.0), axis=1, keepdims=True)
        sub_cols = lax.broadcasted_iota(I32, (nb, sub), 1)

        def body(s, carry):
            r0 = pl.multiple_of(s * sub, sub)
            pm = gather_matrix(r0)
            gates = _dot(pm, c3[0]) + _dot(pm, c3[1]) + _dot(pm, c3[2])
            gate = jnp.sum(jnp.where(lane_e, gates, 0.0), axis=1, keepdims=True)
            y = (yg_ref[pl.ds(r0, sub), :] * gate).astype(BF16)
            scatter = jnp.where(slot_c == (r0 + sub_cols).astype(F32), 1.0, 0.0).astype(BF16)
            o_ref[...] += _dot(scatter, y)
            return carry
        lax.fori_loop(0, nsub, body, 0)

        if final:
            @pl.when(e == pl.num_programs(1) - 1)
            def _():
                o_ref[...] = _rms(o_ref[...], gf_ref[...])


def _moe(x, g, sel_t, sel, comb, w1, w3, w2, gf, *, final, nb, tf):
    n = x.shape[0]
    d_ff = w1.shape[2]
    sub = min(nb, 256)
    row = lambda i, e, j: (i, 0)
    whole = lambda i, e, j: (0, 0)
    return pl.pallas_call(
        functools.partial(_moe_kernel, final=final, nb=nb, sub=sub),
        grid=(n // nb, N_EXPERTS, d_ff // tf),
        in_specs=[pl.BlockSpec((nb, D_MODEL), row), pl.BlockSpec((1, D_MODEL), whole),
                  pl.BlockSpec((N_EXPERTS, nb), lambda i, e, j: (0, i)),
                  pl.BlockSpec((nb, N_EXPERTS), row), pl.BlockSpec((nb, N_EXPERTS), row),
                  pl.BlockSpec((1, D_MODEL, tf), lambda i, e, j: (e, 0, j)),
                  pl.BlockSpec((1, D_MODEL, tf), lambda i, e, j: (e, 0, j)),
                  pl.BlockSpec((1, tf, D_MODEL), lambda i, e, j: (e, j, 0)), pl.BlockSpec((1, D_MODEL), whole)],
        out_specs=pl.BlockSpec((nb, D_MODEL), row),
        out_shape=jax.ShapeDtypeStruct((n, D_MODEL), F32),
        scratch_shapes=[pltpu.VMEM((nb, D_MODEL), BF16), pltpu.VMEM((N_EXPERTS, nb), F32),
                        pltpu.VMEM((nb, N_EXPERTS), F32), pltpu.SMEM((N_EXPERTS,), I32),
                        pltpu.VMEM((nb, D_MODEL), BF16), pltpu.VMEM((nb, D_MODEL), F32)],
        compiler_params=_cparams("parallel", "arbitrary", "arbitrary"),
        name="moe",
    )(x, g, sel_t, sel, comb, w1, w3, w2, gf)


def _rmsnorm_kernel(x_ref, g_ref, o_ref):
    o_ref[...] = _rms(x_ref[...], g_ref[...])


def _rmsnorm(x, g, tm):
    n = x.shape[0]
    return pl.pallas_call(
        _rmsnorm_kernel,
        grid=(n // tm,),
        in_specs=[pl.BlockSpec((tm, D_MODEL), lambda i: (i, 0)), pl.BlockSpec((1, D_MODEL), lambda i: (0, 0))],
        out_specs=pl.BlockSpec((tm, D_MODEL), lambda i: (i, 0)),
        out_shape=jax.ShapeDtypeStruct((n, D_MODEL), F32),
        compiler_params=_cparams("parallel"),
        name="final_norm",
    )(x, g)


def _row_tile(n, cap):
    t = min(n, cap)
    assert n % t == 0, (n, cap)
    return t


def _ff_tile(d_ff):
    for parts in (2, 4, 11, 22):
        if d_ff % parts == 0 and (d_ff // parts) % LANES == 0:
            return d_ff // parts
    return d_ff


def _mixer(x, tabs, w, *, bsz, t_len, pos0, past, s0, layer):
    n = bsz * t_len
    tm = _row_tile(t_len, 256)
    qa, hb, kv = _in_proj(x, w["norm_mix"], w["w_in"], tabs, t_len, tm)
    ak = kv[:, 0:KV_WIDTH].reshape(bsz, t_len, A_KV_HEADS, HEAD_DIM)
    av = kv[:, KV_WIDTH:2 * KV_WIDTH].reshape(bsz, t_len, A_KV_HEADS, HEAD_DIM)
    iq = kv[:, 256:512].reshape(bsz, t_len, IDX_WIDTH)
    ik = kv[:, 512:512 + HEAD_DIM].reshape(bsz, t_len, HEAD_DIM)
    iw = kv[:, 512 + HEAD_DIM:512 + HEAD_DIM + IDX_HEADS].reshape(bsz, t_len, IDX_HEADS)
    if past is None:
        k_all, v_all, ik_all = ak, av, ik
    else:
        k_all = jnp.concatenate([past[0], ak], axis=1)
        v_all = jnp.concatenate([past[1], av], axis=1)
        ik_all = jnp.concatenate([past[2], ik], axis=1)
    l_real = k_all.shape[1]
    topk = min(TOPK_MAX, l_real // 4)
    tq = LANES
    kb_rows = 256 if l_real % 256 == 0 else LANES
    l_pad = -(-l_real // (2 * kb_rows)) * (2 * kb_rows)
    t_pad = -(-t_len // tq) * tq
    padk = lambda a: jnp.pad(a, ((0, 0), (0, l_pad - l_real)) + ((0, 0),) * (a.ndim - 2))
    padq = lambda a: jnp.pad(a, ((0, 0), (0, t_pad - t_len), (0, 0)))
    k_hm = jnp.swapaxes(padk(k_all), 1, 2).astype(BF16)
    v_hm = jnp.swapaxes(padk(v_all), 1, 2).astype(BF16)
    v_aug = jnp.concatenate([v_hm, jnp.ones(v_hm.shape[:3] + (1,), BF16),
                             jnp.zeros(v_hm.shape[:3] + (LANES - HEAD_DIM - 1,), BF16)], axis=-1)
    iw_t = jnp.pad(jnp.swapaxes(padq(iw), 1, 2), ((0, 0), (0, 8 - IDX_HEADS), (0, 0)))
    oa = _dsa(padq(qa.reshape(bsz, t_len, A_WIDTH)), padq(iq), iw_t, k_hm, v_aug, padk(ik_all).astype(BF16),
              l_real=l_real, pos0=pos0, kb_rows=kb_rows, topk=topk, tq=tq)
    oa = oa[:, :t_len].reshape(n, A_WIDTH)
    ob, s_tiles = _hgrn(hb, w["hgrn_lb"], w["gnorm"], _state_to_tiles(s0), layer=layer, bsz=bsz, t_len=t_len,
                        tb=_row_tile(t_len, 256))
    x1 = _out_proj(oa, ob, w["w_out_a"], w["w_out_b"], x, _row_tile(n, 512))
    return x1, ak, av, ik, _tiles_to_state(s_tiles)


def kernel(x_prompt, x_sample, cache_k, cache_v, cache_idx_k, state_hgrn, cache_mem_k, cache_mem_v, mem_prompt,
           norm_mix, w_in, hgrn_lb, hgrn_gnorm, w_out, norm_mem, norm_memkv, w_mq, w_mk, w_mv, w_mo,
           norm_ffn, ffn_w1, ffn_w3, ffn_w2, moe_router, moe_w1, moe_w3, moe_w2, norm_final):
    depth = w_in.shape[0]
    bp, tp, _ = x_prompt.shape
    bs, ts, _ = x_sample.shape
    past = cache_k.shape[2]
    n_mem = mem_prompt.shape[1]
    tabs_p = _rope_tables(jnp.arange(tp, dtype=I32))
    tabs_s = _rope_tables(past + jnp.arange(ts, dtype=I32))
    xp = x_prompt.reshape(bp * tp, D_MODEL)
    xs = x_sample.reshape(bs * ts, D_MODEL)
    mem = mem_prompt.reshape(bp * n_mem, D_MODEL)
    s0_p = jnp.zeros((bp, B_HEADS, HEAD_DIM, HEAD_DIM), F32)
    gfin = norm_final.reshape(1, D_MODEL)
    row2 = lambda a: a.reshape(1, -1)
    outs = {k: [] for k in ("kp", "vp", "ikp", "sp", "mkp", "mvp", "ks", "vs", "iks", "ss")}
    for l in range(depth):
        w = {"norm_mix": row2(norm_mix[l]), "w_in": _regroup_w_in(w_in[l]), "hgrn_lb": hgrn_lb.astype(F32),
             "gnorm": row2(jnp.tile(hgrn_gnorm[l], B_HEADS)),
             "w_out_a": w_out[l, :A_WIDTH].astype(BF16), "w_out_b": w_out[l, A_WIDTH:].astype(BF16)}
        xp, kp, vp, ikp, sp = _mixer(xp, tabs_p, w, bsz=bp, t_len=tp, pos0=0, past=None, s0=s0_p, layer=l)
        xs, ks, vs, iks, ss = _mixer(xs, tabs_s, w, bsz=bs, t_len=ts, pos0=past,
                                     past=(cache_k[l], cache_v[l], cache_idx_k[l]), s0=state_hgrn[l], layer=l)
        wkv = jnp.concatenate([w_mk[l], w_mv[l]], axis=1).astype(BF16)
        mkv = _norm_matmul(mem, row2(norm_memkv[l]), wkv, _row_tile(bp * n_mem, 256))
        mk_p = mkv[:, :D_MODEL].reshape(bp, n_mem, D_MODEL)
        mv_p = mkv[:, D_MODEL:].reshape(bp, n_mem, D_MODEL)
        wq, wo, gm = w_mq[l].astype(BF16), w_mo[l].astype(BF16), row2(norm_mem[l])
        xp = _mem_attn(xp, gm, wq, mk_p.astype(BF16), mv_p.astype(BF16), wo, tp, _row_tile(tp, 256))
        xs = _mem_attn(xs, gm, wq, cache_mem_k[l].reshape(bs, n_mem, D_MODEL).astype(BF16),
                       cache_mem_v[l].reshape(bs, n_mem, D_MODEL).astype(BF16), wo, ts, _row_tile(ts, 256))
        final = l == depth - 1
        gf = row2(norm_ffn[l])
        if l % 2 == 0:
            w1, w3, w2 = ffn_w1[l // 2].astype(BF16), ffn_w3[l // 2].astype(BF16), ffn_w2[l // 2].astype(BF16)
            tf = _ff_tile(w1.shape[1])
            xp = _ffn(xp, gf, w1, w3, w2, gfin, final=final, tm=_row_tile(bp * tp, 512), tf=tf)
            xs = _ffn(xs, gf, w1, w3, w2, gfin, final=final, tm=_row_tile(bs * ts, 512), tf=tf)
        else:
            w1, w3, w2 = moe_w1[l // 2].astype(BF16), moe_w3[l // 2].astype(BF16), moe_w2[l // 2].astype(BF16)
            rt = moe_router[l // 2].T
            tf = _ff_tile(w1.shape[2])
            comb_p, sel_p = _router(xp, gf, rt, _row_tile(bp * tp, 512))
            comb_s, sel_s = _router(xs, gf, rt, _row_tile(bs * ts, 512))
            xp = _moe(xp, gf, sel_p, sel_p.T, comb_p.T, w1, w3, w2, gfin, final=final,
                      nb=_row_tile(bp * tp, 1024), tf=tf)
            xs = _moe(xs, gf, sel_s, sel_s.T, comb_s.T, w1, w3, w2, gfin, final=final,
                      nb=_row_tile(bs * ts, 1024), tf=tf)
        outs["kp"].append(kp); outs["vp"].append(vp); outs["ikp"].append(ikp); outs["sp"].append(sp)
        outs["mkp"].append(mk_p.reshape(bp, n_mem, MEM_HEADS, MEM_HEAD_DIM))
        outs["mvp"].append(mv_p.reshape(bp, n_mem, MEM_HEADS, MEM_HEAD_DIM))
        outs["ks"].append(ks); outs["vs"].append(vs); outs["iks"].append(iks); outs["ss"].append(ss)
    y_prompt = xp.reshape(bp, tp, D_MODEL)
    y_sample = xs.reshape(bs, ts, D_MODEL)
    st = lambda k: jnp.stack(outs[k])
    return (y_prompt, y_sample, st("kp"), st("vp"), st("ikp"), st("sp"), st("mkp"), st("mvp"),
            st("ks"), st("vs"), st("iks"), st("ss"))
```

```python
import functools

import numpy as np
import jax
import jax.numpy as jnp
from jax import lax
from jax.experimental import pallas as pl
from jax.experimental.pallas import tpu as pltpu

F32 = jnp.float32
BF16 = jnp.bfloat16
I32 = jnp.int32
I16 = jnp.int16

D_MODEL = 1024
CHUNK = 64
EPS = 1e-6
ROPE_THETA = 500000.0
HEAD_DIM = 64
ROT_DIM = HEAD_DIM // 4
A_HEADS = 8
A_KV_HEADS = 2
A_GROUP = A_HEADS // A_KV_HEADS
A_WIDTH = A_HEADS * HEAD_DIM
KV_WIDTH = A_KV_HEADS * HEAD_DIM
IDX_HEADS = 4
IDX_WIDTH = IDX_HEADS * HEAD_DIM
TOPK_MAX = 256
B_HEADS = 8
B_WIDTH = B_HEADS * HEAD_DIM
B_CHUNK = 16
MEM_HEADS = 4
MEM_HEAD_DIM = D_MODEL // MEM_HEADS
N_EXPERTS = 8
LANES = 128
HALF = 256
IN_WIDTH_PAD = 3200
VMEM_LIMIT = 56 * 1024 * 1024
INT_MIN = -2147483648
NEG_BIG = -1e30
LOG2E = 1.4426950408889634

_NT = (((1,), (1,)), ((), ()))
_TN = (((0,), (0,)), ((), ()))


def _cparams(*sem):
    return pltpu.CompilerParams(dimension_semantics=sem, vmem_limit_bytes=VMEM_LIMIT)


def _rms(x, g):
    return x * lax.rsqrt(jnp.mean(x * x, axis=-1, keepdims=True) + EPS) * g


def _sigmoid(x):
    return 1.0 / (1.0 + jnp.exp(-x))


def _dot(a, b):
    return jnp.dot(a, b, preferred_element_type=F32)


def _split2(x):
    hi = x.astype(BF16)
    lo = (x - hi.astype(F32)).astype(BF16)
    return hi, lo


def _split3(x):
    hi = x.astype(BF16)
    r = x - hi.astype(F32)
    mid = r.astype(BF16)
    lo = (r - mid.astype(F32)).astype(BF16)
    return hi, mid, lo


def _rope128(xb, c, s1, s2):
    return xb * c + pltpu.roll(xb, LANES - ROT_DIM // 2, 1) * s1 + pltpu.roll(xb, ROT_DIM // 2, 1) * s2


def _inproj_kernel(x_ref, g_ref, w_ref, c_ref, s1_ref, s2_ref,
                   qa_ref, hb_ref, iq_ref, kt_ref, vt_ref, ikt_ref, khm_ref, vaug_ref, ikb_ref, iwt_ref):
    xn = _rms(x_ref[...], g_ref[...]).astype(BF16)
    c, s1, s2 = c_ref[...], s1_ref[...], s2_ref[...]
    for j in range(A_WIDTH // LANES):
        h = _dot(xn, w_ref[:, j * LANES:(j + 1) * LANES])
        qa_ref[:, j * LANES:(j + 1) * LANES] = _rope128(h, c, s1, s2)
    for j in range(4):
        lo = A_WIDTH + j * B_WIDTH
        hb_ref[:, j * B_WIDTH:(j + 1) * B_WIDTH] = _dot(xn, w_ref[:, lo:lo + B_WIDTH])
    base = A_WIDTH + 4 * B_WIDTH
    k = _rope128(_dot(xn, w_ref[:, base:base + 128]), c, s1, s2)
    v = _dot(xn, w_ref[:, base + 128:base + 256])
    iq_ref[:, 0:128] = _rope128(_dot(xn, w_ref[:, base + 256:base + 384]), c, s1, s2)
    iq_ref[:, 128:256] = _rope128(_dot(xn, w_ref[:, base + 384:base + 512]), c, s1, s2)
    lane = lax.broadcasted_iota(I32, c.shape, 1)
    first = lane < HEAD_DIM
    ikw = _rope128(_dot(xn, w_ref[:, base + 512:base + 640]),
                   jnp.where(first, c, 1.0), jnp.where(first, s1, 0.0), jnp.where(first, s2, 0.0))
    kt_ref[...] = k.T
    vt_ref[...] = v.T
    ikw_t = ikw.T
    ikt_ref[...] = ikw_t[:HEAD_DIM]
    iwt_ref[...] = ikw_t[HEAD_DIM:HEAD_DIM + 8]
    ikb_ref[...] = ikw[:, :HEAD_DIM].astype(BF16)
    ones_col = jnp.where(lane == HEAD_DIM, 1.0, 0.0)
    for j in range(A_KV_HEADS):
        khm_ref[j] = k[:, j * HEAD_DIM:(j + 1) * HEAD_DIM].astype(BF16)
        vj = v if j == 0 else pltpu.roll(v, LANES - j * HEAD_DIM, 1)
        vaug_ref[j] = jnp.where(first, vj, ones_col).astype(BF16)


def _in_proj(x, g, w, tabs, tm):
    n = x.shape[0]
    nt = tabs[0].shape[0] // tm
    row = lambda i: (i, 0)
    tab = lambda i: (i % nt, 0)
    whole = lambda i: (0, 0)
    tmin = lambda i: (0, i)
    hmaj = lambda i: (0, i, 0)
    return pl.pallas_call(
        _inproj_kernel,
        grid=(n // tm,),
        in_specs=[pl.BlockSpec((tm, D_MODEL), row), pl.BlockSpec((1, D_MODEL), whole),
                  pl.BlockSpec((D_MODEL, IN_WIDTH_PAD), whole),
                  pl.BlockSpec((tm, LANES), tab), pl.BlockSpec((tm, LANES), tab), pl.BlockSpec((tm, LANES), tab)],
        out_specs=[pl.BlockSpec((tm, A_WIDTH), row), pl.BlockSpec((tm, 4 * B_WIDTH), row),
                   pl.BlockSpec((tm, IDX_WIDTH), row),
                   pl.BlockSpec((KV_WIDTH, tm), tmin), pl.BlockSpec((KV_WIDTH, tm), tmin),
                   pl.BlockSpec((HEAD_DIM, tm), tmin),
                   pl.BlockSpec((A_KV_HEADS, tm, HEAD_DIM), hmaj), pl.BlockSpec((A_KV_HEADS, tm, LANES), hmaj),
                   pl.BlockSpec((tm, HEAD_DIM), row), pl.BlockSpec((8, tm), tmin)],
        out_shape=[jax.ShapeDtypeStruct((n, A_WIDTH), F32), jax.ShapeDtypeStruct((n, 4 * B_WIDTH), F32),
                   jax.ShapeDtypeStruct((n, IDX_WIDTH), F32),
                   jax.ShapeDtypeStruct((KV_WIDTH, n), F32), jax.ShapeDtypeStruct((KV_WIDTH, n), F32),
                   jax.ShapeDtypeStruct((HEAD_DIM, n), F32),
                   jax.ShapeDtypeStruct((A_KV_HEADS, n, HEAD_DIM), BF16),
                   jax.ShapeDtypeStruct((A_KV_HEADS, n, LANES), BF16),
                   jax.ShapeDtypeStruct((n, HEAD_DIM), BF16), jax.ShapeDtypeStruct((8, n), F32)],
        compiler_params=_cparams("parallel"),
        name="in_proj",
    )(x, g, w, *tabs)


def _rope_tables(pos):
    half = ROT_DIM // 2
    inv_freq = 1.0 / (ROPE_THETA ** (jnp.arange(half, dtype=F32) * (2.0 / ROT_DIM)))
    ang = pos.astype(F32)[:, None] * inv_freq[None, :]
    cos, sin = jnp.cos(ang), jnp.sin(ang)
    t = pos.shape[0]
    pad = HEAD_DIM - ROT_DIM
    c = jnp.concatenate([cos, cos, jnp.ones((t, pad), F32)], axis=1)
    s1 = jnp.concatenate([-sin, jnp.zeros((t, half + pad), F32)], axis=1)
    s2 = jnp.concatenate([jnp.zeros((t, half), F32), sin, jnp.zeros((t, pad), F32)], axis=1)
    rep = LANES // HEAD_DIM
    return tuple(jnp.tile(a, (1, rep)) for a in (c, s1, s2))


def _regroup_w_in(w):
    o = np.cumsum([0, A_WIDTH, KV_WIDTH, KV_WIDTH, IDX_WIDTH, HEAD_DIM, IDX_HEADS, B_WIDTH, B_WIDTH, B_WIDTH, B_WIDTH])
    seg = lambda i: w[:, o[i]:o[i + 1]]
    used = A_WIDTH + 4 * B_WIDTH + 2 * KV_WIDTH + IDX_WIDTH + HEAD_DIM + IDX_HEADS
    pad = jnp.zeros((w.shape[0], IN_WIDTH_PAD - used), w.dtype)
    return jnp.concatenate([seg(0), seg(6), seg(7), seg(8), seg(9), seg(1), seg(2), seg(3), seg(4), seg(5), pad],
                           axis=1).astype(BF16)


def _dsa_kernel(q_ref, iq_ref, iw_ref, k_ref, v_ref, ik_ref, o_ref, key_ref, hi_ref, lo_ref, acc_ref, m_ref,
                s0_ref, s1_ref, alpha_ref, p_ref, bias_ref, *, l_real, pos0, kb_rows, topk, tq):
    i = pl.program_id(1)
    qpos0 = pos0 + i * tq
    kend = jnp.minimum(((qpos0 + tq - 1) // CHUNK + 1) * CHUNK, l_real)
    cb_rows = 2 * kb_rows
    nkc = (kend + cb_rows - 1) // cb_rows
    nkb = 2 * nkc
    qchunk = (qpos0 + lax.broadcasted_iota(I32, (1, tq), 1)) // CHUNK
    row_iota = lax.broadcasted_iota(I32, (kb_rows, tq), 0)

    iq = iq_ref[...].astype(BF16)
    iq4 = jnp.concatenate([iq[:, h * HEAD_DIM:(h + 1) * HEAD_DIM] for h in range(IDX_HEADS)], axis=0)
    iw = iw_ref[...] * (IDX_HEADS ** -0.5 * HEAD_DIM ** -0.5)

    def score_body(kb, carry):
        r0 = pl.multiple_of(kb * kb_rows, kb_rows)
        ikb = ik_ref[pl.ds(r0, kb_rows), :]
        lg4 = lax.dot_general(ikb, iq4, _NT, preferred_element_type=F32)
        acc = jnp.zeros((kb_rows, tq), F32)
        for h in range(IDX_HEADS):
            acc = acc + iw[h:h + 1, :] * jnp.maximum(lg4[:, h * tq:(h + 1) * tq], 0.0)
        bits = pltpu.bitcast(acc + 0.0, I32)
        key = bits ^ ((bits >> 31) & 0x7FFFFFFF)
        kpos = r0 + row_iota
        adm = (kpos // CHUNK <= qchunk) & (kpos < l_real)
        key = jnp.where(adm, key, INT_MIN)
        key_ref[pl.ds(r0, kb_rows), :] = key
        hi_ref[pl.ds(r0, kb_rows), :] = (key >> 16).astype(I16)
        return carry

    lax.fori_loop(0, nkb, score_body, 0)

    def count(pred):
        def body(kc, c):
            r0 = pl.multiple_of(kc * cb_rows, cb_rows)
            hit = pred(key_ref[pl.ds(r0, cb_rows), :], r0)
            return c + jnp.where(hit, 1, 0).reshape(cb_rows // 8, 8, tq).sum(axis=0)
        c = lax.fori_loop(0, nkc, body, jnp.zeros((8, tq), I32))
        return c.sum(axis=0, keepdims=True)

    def count16(ref, pred):
        one, zero = jnp.ones((), BF16), jnp.zeros((), BF16)

        def body(kc, c):
            r0 = pl.multiple_of(kc * cb_rows, cb_rows)
            w = jnp.where(pred(ref[pl.ds(r0, cb_rows), :]), one, zero).reshape(cb_rows // 16, 16, tq)
            parts = [w[r] for r in range(cb_rows // 16)]
            while len(parts) > 1:
                parts = [a + b for a, b in zip(parts[::2], parts[1::2])] + parts[len(parts) & ~1:]
            return c + parts[0]
        c = lax.fori_loop(0, nkc, body, jnp.zeros((16, tq), BF16))
        return c.astype(F32).sum(axis=0, keepdims=True).astype(I32)

    def bisect16(ref, kth):
        c0 = count16(ref, lambda blk: blk >= jnp.zeros((), I16))
        start = jnp.where(c0 >= kth, 0, -32768).astype(I32)

        def bit_body(it, prefix):
            cand = prefix | lax.shift_left(jnp.int32(1), 14 - it)
            c = count16(ref, lambda blk: blk >= cand.astype(I16))
            return jnp.where(c >= kth, cand, prefix)

        return lax.fori_loop(0, 15, bit_body, start)

    tau_hi = bisect16(hi_ref, topk)
    tau_hi16 = tau_hi.astype(I16)
    kth_lo = topk - count16(hi_ref, lambda blk: blk > tau_hi16)

    def lo_body(kc, carry):
        r0 = pl.multiple_of(kc * cb_rows, cb_rows)
        lo = ((key_ref[pl.ds(r0, cb_rows), :] & 0xFFFF) - 32768).astype(I16)
        lo_ref[pl.ds(r0, cb_rows), :] = jnp.where(hi_ref[pl.ds(r0, cb_rows), :] == tau_hi16, lo,
                                                 jnp.full((), -32768, I16))
        return carry

    lax.fori_loop(0, nkc, lo_body, 0)
    tau_lo = bisect16(lo_ref, kth_lo)
    tau = lax.shift_left(tau_hi, 16) | (tau_lo + 32768)
    n_ge = count(lambda blk, r0: blk >= tau)

    @pl.when(jnp.max(jnp.where((n_ge > topk) & (tau != INT_MIN), 1, 0)) > 0)
    def _():
        need = (topk - count(lambda blk, r0: blk > tau)).astype(F32)
        ri = lax.broadcasted_iota(I32, (kb_rows, kb_rows), 0)
        ci = lax.broadcasted_iota(I32, (kb_rows, kb_rows), 1)
        upto = jnp.where(ci <= ri, 1.0, 0.0).astype(BF16)

        def strike(kc, seen):
            for half in range(2):
                rows = pl.ds(pl.multiple_of(kc * cb_rows + half * kb_rows, kb_rows), kb_rows)
                key = key_ref[rows, :]
                tie = key == tau
                rank = _dot(upto, jnp.where(tie, 1.0, 0.0).astype(BF16)) + seen
                key_ref[rows, :] = jnp.where(tie & (rank > need), INT_MIN, key)
                seen = rank[kb_rows - 1:kb_rows, :]
            return seen

        lax.fori_loop(0, nkc, strike, jnp.zeros((1, tq), F32))

    lane_rep = kb_rows // LANES
    tau_c = jnp.tile(jnp.broadcast_to(tau, (LANES, tq)).T, (1, lane_rep))
    acc_ref[...] = jnp.zeros(acc_ref.shape, F32)
    m_ref[...] = jnp.full(m_ref.shape, NEG_BIG, F32)
    qs = (q_ref[...] * (HEAD_DIM ** -0.5 * LOG2E)).astype(BF16)
    q4 = [jnp.concatenate([qs[:, (j * A_GROUP + g) * HEAD_DIM:(j * A_GROUP + g + 1) * HEAD_DIM]
                           for g in range(A_GROUP)], axis=0) for j in range(A_KV_HEADS)]

    def qk_scores(kb, s_slot):
        r0 = pl.multiple_of(jnp.minimum(kb, nkb - 1) * kb_rows, kb_rows)
        for j in range(A_KV_HEADS):
            s_slot[j] = lax.dot_general(q4[j], k_ref[j, pl.ds(r0, kb_rows), :], _NT,
                                        preferred_element_type=F32)

    def softmax_pv(kb, s_slot):
        r0 = pl.multiple_of(kb * kb_rows, kb_rows)
        key_t = key_ref[pl.ds(r0, kb_rows), :].T
        bias_ref[...] = jnp.where((key_t >= tau_c) & (key_t != INT_MIN), 0.0, NEG_BIG)
        for j in range(A_KV_HEADS):
            for g in range(A_GROUP):
                rows = slice(g * tq, (g + 1) * tq)
                m_old = m_ref[j, g]
                m_new = jnp.maximum(m_old, jnp.max(s_slot[j, rows, :] + bias_ref[...], axis=1, keepdims=True))
                alpha_ref[j, g] = jnp.exp2(m_old - m_new)
                m_ref[j, g] = m_new
            for g in range(A_GROUP):
                rows = slice(g * tq, (g + 1) * tq)
                sm = s_slot[j, rows, :] + bias_ref[...]
                p_ref[j, rows, :] = jnp.exp2(sm - jnp.tile(m_ref[j, g], (1, lane_rep))).astype(BF16)
            pv = _dot(p_ref[j], v_ref[j, pl.ds(r0, kb_rows), :])
            acc_ref[j] = alpha_ref[j] * acc_ref[j] + pv.reshape(A_GROUP, tq, LANES)

    qk_scores(0, s0_ref)

    def pair_body(i, carry):
        kb = 2 * i
        qk_scores(kb + 1, s1_ref)
        softmax_pv(kb, s0_ref)
        qk_scores(kb + 2, s0_ref)
        softmax_pv(kb + 1, s1_ref)
        return carry

    lax.fori_loop(0, nkc, pair_body, 0)

    outs = []
    for j in range(A_KV_HEADS):
        acc = acc_ref[j]
        for g in range(A_GROUP):
            outs.append(acc[g, :, :HEAD_DIM] * (1.0 / acc[g, :, HEAD_DIM:HEAD_DIM + 1]))
    o_ref[...] = jnp.concatenate(outs, axis=1)


def _dsa(q, iq, iw_t, k_hm, v_aug, ik, *, bsz, l_real, pos0, kb_rows, topk, tq):
    t_len = q.shape[0] // bsz
    l_pad = ik.shape[0] // bsz
    nq = t_len // tq
    assert l_pad // 16 <= 256, "packed hit counts are summed in bf16, exact only up to 256 per slot"
    kern = functools.partial(_dsa_kernel, l_real=l_real, pos0=pos0, kb_rows=kb_rows, topk=topk, tq=tq)
    qrow = lambda b, i: (b * nq + i, 0)
    return pl.pallas_call(
        kern,
        grid=(bsz, nq),
        in_specs=[pl.BlockSpec((tq, A_WIDTH), qrow), pl.BlockSpec((tq, IDX_WIDTH), qrow),
                  pl.BlockSpec((8, tq), lambda b, i: (0, b * nq + i)),
                  pl.BlockSpec((A_KV_HEADS, l_pad, HEAD_DIM), lambda b, i: (0, b, 0)),
                  pl.BlockSpec((A_KV_HEADS, l_pad, LANES), lambda b, i: (0, b, 0)),
                  pl.BlockSpec((l_pad, HEAD_DIM), lambda b, i: (b, 0))],
        out_specs=pl.BlockSpec((tq, A_WIDTH), qrow),
        out_shape=jax.ShapeDtypeStruct((bsz * t_len, A_WIDTH), F32),
        scratch_shapes=[pltpu.VMEM((l_pad, tq), I32), pltpu.VMEM((l_pad, tq), I16), pltpu.VMEM((l_pad, tq), I16),
                        pltpu.VMEM((A_KV_HEADS, A_GROUP, tq, LANES), F32),
                        pltpu.VMEM((A_KV_HEADS, A_GROUP, tq, LANES), F32),
                        pltpu.VMEM((A_KV_HEADS, A_GROUP * tq, kb_rows), F32),
                        pltpu.VMEM((A_KV_HEADS, A_GROUP * tq, kb_rows), F32),
                        pltpu.VMEM((A_KV_HEADS, A_GROUP, tq, LANES), F32),
                        pltpu.VMEM((A_KV_HEADS, A_GROUP * tq, kb_rows), BF16),
                        pltpu.VMEM((tq, kb_rows), F32)],
        compiler_params=_cparams("parallel", "arbitrary"),
        name="dsa",
    )(q, iq, iw_t, k_hm, v_aug, ik)


def _hgrn_kernel(bq_ref, bf_ref, bi_ref, bg_ref, lb_ref, gn_ref, s0_ref, o_ref, sfin_ref,
                 st_ref, q_s, kk_s, b_s, qe_s, kd_s, dec_s, o_s, e_s, *, layer, tb):
    tstep = pl.program_id(1)
    nchunk = tb // B_CHUNK

    @pl.when(tstep == 0)
    def _():
        st_ref[...] = s0_ref[0]

    lbr = lb_ref[...]
    lbe = jnp.exp(lbr - jnp.max(lbr, axis=0, keepdims=True))
    lbs = lbe / jnp.sum(lbe, axis=0, keepdims=True)
    cum = lbs[0:1, :]
    first = cum
    for r in range(1, layer + 1):
        cum = cum + lbs[r:r + 1, :]
    lb = cum - first

    f = lb + (1.0 - lb) * _sigmoid(bf_ref[...])
    bq = bq_ref[...]
    q = bq * _sigmoid(bq) * (HEAD_DIM ** -0.5)
    g3 = _split3(jnp.log(f))
    ri = lax.broadcasted_iota(I32, (tb, tb), 0)
    ci = lax.broadcasted_iota(I32, (tb, tb), 1)
    same = (ri // B_CHUNK) == (ci // B_CHUNK)
    tri = jnp.where(same & (ci <= ri), 1.0, 0.0).astype(BF16)
    ones = jnp.where(same, 1.0, 0.0).astype(BF16)
    b = _dot(tri, g3[0]) + _dot(tri, g3[1]) + _dot(tri, g3[2])
    blast = _dot(ones, g3[0]) + _dot(ones, g3[1]) + _dot(ones, g3[2])
    q_s[...] = q
    kk_s[...] = 1.0 - f
    b_s[...] = b
    qe_s[...] = q * jnp.exp(b)
    kd_s[...] = (1.0 - f) * jnp.exp(blast - b)
    dec_s[...] = jnp.exp(blast)

    hi = lax.broadcasted_iota(I32, (HALF, HALF), 0) // HEAD_DIM
    hj = lax.broadcasted_iota(I32, (HALF, HALF), 1) // HEAD_DIM
    bd_mask = hi == hj
    bd = jnp.where(bd_mask, 1.0, 0.0).astype(BF16)
    t_iota = lax.broadcasted_iota(I32, (B_CHUNK, B_WIDTH), 0)

    def seg_sum(x):
        out = []
        for half in range(2):
            xh, xl = _split2(x[:, half * HALF:(half + 1) * HALF])
            out.append(_dot(xh, bd) + _dot(xl, bd))
        return jnp.concatenate(out, axis=1)

    def chunk_body(c, carry):
        r0 = pl.multiple_of(c * B_CHUNK, B_CHUNK)
        rows = pl.ds(r0, B_CHUNK)
        qc, kc, bc, vc = q_s[rows, :], kk_s[rows, :], b_s[rows, :], bi_ref[rows, :]
        qe, kd = qe_s[rows, :].astype(BF16), kd_s[rows, :].astype(BF16)
        dec = dec_s[pl.ds(r0, 1), :]
        for s in range(B_CHUNK):
            ex = jnp.exp(jnp.where(t_iota >= s, bc - bc[s:s + 1, :], NEG_BIG))
            e_s[s * B_CHUNK:(s + 1) * B_CHUNK, :] = qc * kc[s:s + 1, :] * ex
        aexp = seg_sum(e_s[...])
        o = jnp.zeros((B_CHUNK, B_WIDTH), F32)
        for s in range(B_CHUNK):
            o = o + aexp[s * B_CHUNK:(s + 1) * B_CHUNK, :] * vc[s:s + 1, :]
        inter = []
        vb = vc.astype(BF16)
        for half in range(2):
            cols = slice(half * HALF, (half + 1) * HALF)
            st = st_ref[half]
            inter.append(lax.dot_general(qe[:, cols], st.astype(BF16), _NT, preferred_element_type=F32))
            upd = lax.dot_general(vb[:, cols], kd[:, cols], _TN, preferred_element_type=F32)
            st_ref[half] = st * dec[:, cols] + jnp.where(bd_mask, upd, 0.0)
        o_s[rows, :] = o + jnp.concatenate(inter, axis=1)
        return carry

    lax.fori_loop(0, nchunk, chunk_body, 0)

    o = o_s[...]
    ms = seg_sum(o * o) * (1.0 / HEAD_DIM)
    bg = bg_ref[...]
    o_ref[...] = o * lax.rsqrt(ms + EPS) * gn_ref[...] * (bg * _sigmoid(bg))

    @pl.when(tstep == pl.num_programs(1) - 1)
    def _():
        sfin_ref[0] = st_ref[...]


def _hgrn(hb, lb_raw, gn_tiled, s0_bd, *, layer, bsz, t_len, tb):
    nt = t_len // tb
    col = lambda j: (lambda b, t: (b * nt + t, j))
    scr = lambda r: pltpu.VMEM((r, B_WIDTH), F32)
    kern = functools.partial(_hgrn_kernel, layer=layer, tb=tb)
    return pl.pallas_call(
        kern,
        grid=(bsz, nt),
        in_specs=[pl.BlockSpec((tb, B_WIDTH), col(0)), pl.BlockSpec((tb, B_WIDTH), col(1)),
                  pl.BlockSpec((tb, B_WIDTH), col(2)), pl.BlockSpec((tb, B_WIDTH), col(3)),
                  pl.BlockSpec(lb_raw.shape, lambda b, t: (0, 0)),
                  pl.BlockSpec((1, B_WIDTH), lambda b, t: (0, 0)),
                  pl.BlockSpec((1, 2, HALF, HALF), lambda b, t: (b, 0, 0, 0))],
        out_specs=[pl.BlockSpec((tb, B_WIDTH), lambda b, t: (b * nt + t, 0)),
                   pl.BlockSpec((1, 2, HALF, HALF), lambda b, t: (b, 0, 0, 0))],
        out_shape=[jax.ShapeDtypeStruct((bsz * t_len, B_WIDTH), F32),
                   jax.ShapeDtypeStruct((bsz, 2, HALF, HALF), F32)],
        scratch_shapes=[pltpu.VMEM((2, HALF, HALF), F32), scr(tb), scr(tb), scr(tb), scr(tb), scr(tb), scr(tb),
                        scr(tb), scr(B_CHUNK * B_CHUNK)],
        compiler_params=_cparams("parallel", "arbitrary"),
        name="hgrn2",
    )(hb, hb, hb, hb, lb_raw, gn_tiled, s0_bd)


def _state_to_tiles(s):
    bsz = s.shape[0]
    st = jnp.swapaxes(s, -1, -2).reshape(bsz, 2, B_HEADS // 2, HEAD_DIM, HEAD_DIM)
    eye = jnp.eye(B_HEADS // 2, dtype=s.dtype)
    t = st[:, :, :, :, None, :] * eye[None, None, :, None, :, None]
    return t.reshape(bsz, 2, HALF, HALF)


def _tiles_to_state(t):
    bsz = t.shape[0]
    t6 = t.reshape(bsz, 2, B_HEADS // 2, HEAD_DIM, B_HEADS // 2, HEAD_DIM)
    diag = jnp.stack([t6[:, :, h, :, h, :] for h in range(B_HEADS // 2)], axis=2)
    return jnp.swapaxes(diag.reshape(bsz, B_HEADS, HEAD_DIM, HEAD_DIM), -1, -2)


def _outproj_kernel(a_ref, b_ref, wa_ref, wb_ref, x_ref, o_ref):
    o_ref[...] = (x_ref[...] + _dot(a_ref[...].astype(BF16), wa_ref[...])
                  + _dot(b_ref[...].astype(BF16), wb_ref[...]))


def _out_proj(a, b, wa, wb, x, tm):
    n = x.shape[0]
    row = lambda i: (i, 0)
    whole = lambda i: (0, 0)
    return pl.pallas_call(
        _outproj_kernel,
        grid=(n // tm,),
        in_specs=[pl.BlockSpec((tm, A_WIDTH), row), pl.BlockSpec((tm, B_WIDTH), row),
                  pl.BlockSpec((A_WIDTH, D_MODEL), whole), pl.BlockSpec((B_WIDTH, D_MODEL), whole),
                  pl.BlockSpec((tm, D_MODEL), row)],
        out_specs=pl.BlockSpec((tm, D_MODEL), row),
        out_shape=jax.ShapeDtypeStruct((n, D_MODEL), F32),
        compiler_params=_cparams("parallel"),
        name="out_proj",
    )(a, b, wa, wb, x)


def _normmm_kernel(x_ref, g_ref, w_ref, o_ref):
    o_ref[...] = _dot(_rms(x_ref[...], g_ref[...]).astype(BF16), w_ref[...])


def _norm_matmul(x, g, w, tm):
    n, m = x.shape[0], w.shape[1]
    return pl.pallas_call(
        _normmm_kernel,
        grid=(n // tm,),
        in_specs=[pl.BlockSpec((tm, D_MODEL), lambda i: (i, 0)), pl.BlockSpec((1, D_MODEL), lambda i: (0, 0)),
                  pl.BlockSpec((D_MODEL, m), lambda i: (0, 0))],
        out_specs=pl.BlockSpec((tm, m), lambda i: (i, 0)),
        out_shape=jax.ShapeDtypeStruct((n, m), F32),
        compiler_params=_cparams("parallel"),
        name="norm_matmul",
    )(x, g, w)


def _memattn_kernel(x_ref, g_ref, wq_ref, mk_ref, mv_ref, wo_ref, o_ref):
    x = x_ref[...]
    q = _dot(_rms(x, g_ref[...]).astype(BF16), wq_ref[...])
    heads = []
    for h in range(MEM_HEADS):
        cols = slice(h * MEM_HEAD_DIM, (h + 1) * MEM_HEAD_DIM)
        s = lax.dot_general(q[:, cols].astype(BF16), mk_ref[0, :, cols], _NT,
                            preferred_element_type=F32) * (MEM_HEAD_DIM ** -0.5)
        p = jnp.exp(s - jnp.max(s, axis=-1, keepdims=True))
        p = p * (1.0 / jnp.sum(p, axis=-1, keepdims=True))
        heads.append(_dot(p.astype(BF16), mv_ref[0, :, cols]).astype(BF16))
    o_ref[...] = x + _dot(jnp.concatenate(heads, axis=1), wo_ref[...])


def _mem_attn(x, g, wq, mk, mv, wo, rows_per_batch, tm):
    n = x.shape[0]
    n_mem = mk.shape[1]
    per = rows_per_batch // tm
    row = lambda i: (i, 0)
    whole = lambda i: (0, 0)
    mem = lambda i: (i // per, 0, 0)
    return pl.pallas_call(
        _memattn_kernel,
        grid=(n // tm,),
        in_specs=[pl.BlockSpec((tm, D_MODEL), row), pl.BlockSpec((1, D_MODEL), whole),
                  pl.BlockSpec((D_MODEL, D_MODEL), whole),
                  pl.BlockSpec((1, n_mem, D_MODEL), mem), pl.BlockSpec((1, n_mem, D_MODEL), mem),
                  pl.BlockSpec((D_MODEL, D_MODEL), whole)],
        out_specs=pl.BlockSpec((tm, D_MODEL), row),
        out_shape=jax.ShapeDtypeStruct((n, D_MODEL), F32),
        compiler_params=_cparams("parallel"),
        name="mem_attn",
    )(x, g, wq, mk, mv, wo)


def _swiglu_mid(xn, w1, w3):
    a = _dot(xn, w1)
    return (a * _sigmoid(a) * _dot(xn, w3)).astype(BF16)


def _ffn_kernel(x_ref, g_ref, w1_ref, w3_ref, w2_ref, gf_ref, o_ref, xn_ref, *, final):
    j = pl.program_id(1)

    @pl.when(j == 0)
    def _():
        x = x_ref[...]
        xn_ref[...] = _rms(x, g_ref[...]).astype(BF16)
        o_ref[...] = x

    o_ref[...] += _dot(_swiglu_mid(xn_ref[...], w1_ref[...], w3_ref[...]), w2_ref[...])

    if final:
        @pl.when(j == pl.num_programs(1) - 1)
        def _():
            o_ref[...] = _rms(o_ref[...], gf_ref[...])


def _ffn(x, g, w1, w3, w2, gf, *, final, tm, tf):
    n = x.shape[0]
    d_ff = w1.shape[1]
    row = lambda i, j: (i, 0)
    whole = lambda i, j: (0, 0)
    return pl.pallas_call(
        functools.partial(_ffn_kernel, final=final),
        grid=(n // tm, d_ff // tf),
        in_specs=[pl.BlockSpec((tm, D_MODEL), row), pl.BlockSpec((1, D_MODEL), whole),
                  pl.BlockSpec((D_MODEL, tf), lambda i, j: (0, j)), pl.BlockSpec((D_MODEL, tf), lambda i, j: (0, j)),
                  pl.BlockSpec((tf, D_MODEL), lambda i, j: (j, 0)), pl.BlockSpec((1, D_MODEL), whole)],
        out_specs=pl.BlockSpec((tm, D_MODEL), row),
        out_shape=jax.ShapeDtypeStruct((n, D_MODEL), F32),
        scratch_shapes=[pltpu.VMEM((tm, D_MODEL), BF16)],
        compiler_params=_cparams("parallel", "arbitrary"),
        name="ffn",
    )(x, g, w1, w3, w2, gf)


def _router_kernel(x_ref, g_ref, rt_ref, comb_ref, sel_ref):
    xn = _rms(x_ref[...], g_ref[...])
    logits = lax.dot_general(rt_ref[...], xn, _NT, preferred_element_type=F32,
                             precision=lax.Precision.HIGHEST)
    e_idx = lax.broadcasted_iota(I32, logits.shape, 0)
    m1 = jnp.max(logits, axis=0, keepdims=True)
    i1 = jnp.min(jnp.where(logits == m1, e_idx, N_EXPERTS), axis=0, keepdims=True)
    rest = jnp.where(e_idx == i1, -jnp.inf, logits)
    m2 = jnp.max(rest, axis=0, keepdims=True)
    i2 = jnp.min(jnp.where(rest == m2, e_idx, N_EXPERTS), axis=0, keepdims=True)
    e2 = jnp.exp(m2 - m1)
    g1 = 1.0 / (1.0 + e2)
    comb_ref[...] = jnp.where(e_idx == i1, g1, 0.0) + jnp.where(e_idx == i2, e2 * g1, 0.0)
    sel_ref[...] = jnp.where((e_idx == i1) | (e_idx == i2), 1.0, 0.0)


def _router(x, g, rt, tm):
    n = x.shape[0]
    return pl.pallas_call(
        _router_kernel,
        grid=(n // tm,),
        in_specs=[pl.BlockSpec((tm, D_MODEL), lambda i: (i, 0)), pl.BlockSpec((1, D_MODEL), lambda i: (0, 0)),
                  pl.BlockSpec((N_EXPERTS, D_MODEL), lambda i: (0, 0))],
        out_specs=[pl.BlockSpec((N_EXPERTS, tm), lambda i: (0, i)), pl.BlockSpec((N_EXPERTS, tm), lambda i: (0, i))],
        out_shape=[jax.ShapeDtypeStruct((N_EXPERTS, n), F32), jax.ShapeDtypeStruct((N_EXPERTS, n), F32)],
        compiler_params=_cparams("parallel"),
        name="moe_router",
    )(x, g, rt)


def _moe_kernel(x_ref, g_ref, selt_ref, sel_ref, comb_ref, w1_ref, w3_ref, w2_ref, gf_ref, o_ref,
                xn_ref, posr_ref, posc_ref, cnt_ref, xg_ref, yg_ref, *, final, nb, sub):
    e = pl.program_id(1)
    j = pl.program_id(2)
    last_j = pl.num_programs(2) - 1

    @pl.when((e == 0) & (j == 0))
    def _():
        x = x_ref[...]
        xn_ref[...] = _rms(x, g_ref[...]).astype(BF16)
        o_ref[...] = x
        ri = lax.broadcasted_iota(I32, (sub, sub), 0)
        ci = lax.broadcasted_iota(I32, (sub, sub), 1)
        before_r = jnp.where(ri < ci, 1.0, 0.0).astype(BF16)
        before_c = jnp.where(ci < ri, 1.0, 0.0).astype(BF16)
        run_r = jnp.zeros((N_EXPERTS, 1), F32)
        run_c = jnp.zeros((1, N_EXPERTS), F32)
        for c in range(nb // sub):
            cols = slice(c * sub, (c + 1) * sub)
            mr = selt_ref[:, cols]
            posr_ref[:, cols] = jnp.where(mr > 0.0, _dot(mr.astype(BF16), before_r) + run_r, -1.0)
            run_r = run_r + jnp.sum(mr, axis=1, keepdims=True)
            mc = sel_ref[cols, :]
            posc_ref[cols, :] = jnp.where(mc > 0.0, _dot(before_c, mc.astype(BF16)) + run_c, -1.0)
            run_c = run_c + jnp.sum(mc, axis=0, keepdims=True)
        for k in range(N_EXPERTS):
            cnt_ref[k] = run_c[0, k].astype(I32)

    nsub = (cnt_ref[e] + sub - 1) // sub
    def gather_matrix(r0):
        slot_r = posr_ref[pl.ds(e, 1), :]
        want = (r0 + lax.broadcasted_iota(I32, (sub, nb), 0)).astype(F32)
        return jnp.where(slot_r == want, 1.0, 0.0).astype(BF16)

    @pl.when(j == 0)
    def _():
        def body(s, carry):
            r0 = pl.multiple_of(s * sub, sub)
            xg_ref[pl.ds(r0, sub), :] = _dot(gather_matrix(r0), xn_ref[...]).astype(BF16)
            return carry
        lax.fori_loop(0, nsub, body, 0)

    def ffn_body(s, carry):
        rows = pl.ds(pl.multiple_of(s * sub, sub), sub)
        part = _dot(_swiglu_mid(xg_ref[rows, :], w1_ref[0], w3_ref[0]), w2_ref[0])

        @pl.when(j == 0)
        def _():
            yg_ref[rows, :] = part

        @pl.when(j > 0)
        def _():
            yg_ref[rows, :] += part
        return carry

    lax.fori_loop(0, nsub, ffn_body, 0)

    @pl.when(j == last_j)
    def _():
        c3 = _split3(comb_ref[...])
        lane_e = lax.broadcasted_iota(I32, (sub, N_EXPERTS), 1) == e
        lane8 = lax.broadcasted_iota(I32, (nb, N_EXPERTS), 1)
        slot_c = jnp.sum(jnp.where(lane8 == e, posc_ref[...], 0.0), axis=1, keepdims=True)
        sub_cols = lax.broadcasted_iota(I32, (nb, sub), 1)

        def body(s, carry):
            r0 = pl.multiple_of(s * sub, sub)
            pm = gather_matrix(r0)
            gates = _dot(pm, c3[0]) + _dot(pm, c3[1]) + _dot(pm, c3[2])
            gate = jnp.sum(jnp.where(lane_e, gates, 0.0), axis=1, keepdims=True)
            y = (yg_ref[pl.ds(r0, sub), :] * gate).astype(BF16)
            scatter = jnp.where(slot_c == (r0 + sub_cols).astype(F32), 1.0, 0.0).astype(BF16)
            o_ref[...] += _dot(scatter, y)
            return carry
        lax.fori_loop(0, nsub, body, 0)

        if final:
            @pl.when(e == pl.num_programs(1) - 1)
            def _():
                o_ref[...] = _rms(o_ref[...], gf_ref[...])


def _moe(x, g, sel_t, sel, comb, w1, w3, w2, gf, *, final, nb, tf):
    n = x.shape[0]
    d_ff = w1.shape[2]
    sub = min(nb, 256)
    row = lambda i, e, j: (i, 0)
    whole = lambda i, e, j: (0, 0)
    return pl.pallas_call(
        functools.partial(_moe_kernel, final=final, nb=nb, sub=sub),
        grid=(n // nb, N_EXPERTS, d_ff // tf),
        in_specs=[pl.BlockSpec((nb, D_MODEL), row), pl.BlockSpec((1, D_MODEL), whole),
                  pl.BlockSpec((N_EXPERTS, nb), lambda i, e, j: (0, i)),
                  pl.BlockSpec((nb, N_EXPERTS), row), pl.BlockSpec((nb, N_EXPERTS), row),
                  pl.BlockSpec((1, D_MODEL, tf), lambda i, e, j: (e, 0, j)),
                  pl.BlockSpec((1, D_MODEL, tf), lambda i, e, j: (e, 0, j)),
                  pl.BlockSpec((1, tf, D_MODEL), lambda i, e, j: (e, j, 0)), pl.BlockSpec((1, D_MODEL), whole)],
        out_specs=pl.BlockSpec((nb, D_MODEL), row),
        out_shape=jax.ShapeDtypeStruct((n, D_MODEL), F32),
        scratch_shapes=[pltpu.VMEM((nb, D_MODEL), BF16), pltpu.VMEM((N_EXPERTS, nb), F32),
                        pltpu.VMEM((nb, N_EXPERTS), F32), pltpu.SMEM((N_EXPERTS,), I32),
                        pltpu.VMEM((nb, D_MODEL), BF16), pltpu.VMEM((nb, D_MODEL), F32)],
        compiler_params=_cparams("parallel", "arbitrary", "arbitrary"),
        name="moe",
    )(x, g, sel_t, sel, comb, w1, w3, w2, gf)


def _rmsnorm_kernel(x_ref, g_ref, o_ref):
    o_ref[...] = _rms(x_ref[...], g_ref[...])


def _rmsnorm(x, g, tm):
    n = x.shape[0]
    return pl.pallas_call(
        _rmsnorm_kernel,
        grid=(n // tm,),
        in_specs=[pl.BlockSpec((tm, D_MODEL), lambda i: (i, 0)), pl.BlockSpec((1, D_MODEL), lambda i: (0, 0))],
        out_specs=pl.BlockSpec((tm, D_MODEL), lambda i: (i, 0)),
        out_shape=jax.ShapeDtypeStruct((n, D_MODEL), F32),
        compiler_params=_cparams("parallel"),
        name="final_norm",
    )(x, g)


def _row_tile(n, cap):
    t = min(n, cap)
    assert n % t == 0, (n, cap)
    return t


def _ff_tile(d_ff):
    for parts in (2, 4, 11, 22):
        if d_ff % parts == 0 and (d_ff // parts) % LANES == 0:
            return d_ff // parts
    return d_ff


def _mixer(x, tabs, w, *, bsz, t_len, pos0, past, s0, layer):
    n = bsz * t_len
    qa, hb, iq, k_t, v_t, ik_t, k_hm, v_aug, ik_b, iw_t = _in_proj(x, w["norm_mix"], w["w_in"], tabs,
                                                                   _row_tile(n, 256))
    l_real = t_len if past is None else past[0].shape[1] + t_len
    topk = min(TOPK_MAX, l_real // 4)
    tq = 2 * LANES if t_len % (2 * LANES) == 0 else LANES
    kb_rows = 256 if l_real % 256 == 0 else LANES
    l_pad = -(-l_real // (2 * kb_rows)) * (2 * kb_rows)
    t_pad = -(-t_len // tq) * tq
    if past is not None:
        pk, pv, pik = past
        one = jnp.ones(pv.shape[:2] + (A_KV_HEADS, 1), BF16)
        zero = jnp.zeros(pv.shape[:2] + (A_KV_HEADS, LANES - HEAD_DIM - 1), BF16)
        pv_aug = jnp.concatenate([pv.astype(BF16), one, zero], axis=-1)
        hm = lambda a: jnp.transpose(a, (2, 0, 1, 3))
        k_hm = jnp.concatenate([hm(pk.astype(BF16)), k_hm.reshape(A_KV_HEADS, bsz, t_len, HEAD_DIM)], axis=2)
        v_aug = jnp.concatenate([hm(pv_aug), v_aug.reshape(A_KV_HEADS, bsz, t_len, LANES)], axis=2)
        ik_b = jnp.concatenate([pik.astype(BF16), ik_b.reshape(bsz, t_len, HEAD_DIM)], axis=1)
    if l_pad != l_real or past is not None:
        padl = l_pad - l_real
        k_hm = jnp.pad(k_hm.reshape(A_KV_HEADS, bsz, l_real, HEAD_DIM), ((0, 0), (0, 0), (0, padl), (0, 0)))
        v_aug = jnp.pad(v_aug.reshape(A_KV_HEADS, bsz, l_real, LANES), ((0, 0), (0, 0), (0, padl), (0, 0)))
        ik_b = jnp.pad(ik_b.reshape(bsz, l_real, HEAD_DIM), ((0, 0), (0, padl), (0, 0)))
        k_hm = k_hm.reshape(A_KV_HEADS, bsz * l_pad, HEAD_DIM)
        v_aug = v_aug.reshape(A_KV_HEADS, bsz * l_pad, LANES)
        ik_b = ik_b.reshape(bsz * l_pad, HEAD_DIM)
    q_in, iq_in, iw_in = qa, iq, iw_t
    if t_pad != t_len:
        padq = lambda a: jnp.pad(a.reshape(bsz, t_len, -1),
                                 ((0, 0), (0, t_pad - t_len), (0, 0))).reshape(bsz * t_pad, -1)
        q_in, iq_in = padq(qa), padq(iq)
        iw_in = jnp.pad(iw_t.reshape(8, bsz, t_len), ((0, 0), (0, 0), (0, t_pad - t_len))).reshape(8, bsz * t_pad)
    oa = _dsa(q_in, iq_in, iw_in, k_hm, v_aug, ik_b, bsz=bsz, l_real=l_real, pos0=pos0, kb_rows=kb_rows,
              topk=topk, tq=tq)
    if t_pad != t_len:
        oa = oa.reshape(bsz, t_pad, A_WIDTH)[:, :t_len].reshape(n, A_WIDTH)
    ob, s_tiles = _hgrn(hb, w["hgrn_lb"], w["gnorm"], _state_to_tiles(s0), layer=layer, bsz=bsz, t_len=t_len,
                        tb=_row_tile(t_len, 256))
    x1 = _out_proj(oa, ob, w["w_out_a"], w["w_out_b"], x, _row_tile(n, 512))
    return x1, k_t, v_t, ik_t, _tiles_to_state(s_tiles)


def _from_feature_major(a, bsz, t_len, heads):
    if heads is None:
        return jnp.transpose(a.reshape(HEAD_DIM, bsz, t_len), (1, 2, 0))
    return jnp.transpose(a.reshape(heads, HEAD_DIM, bsz, t_len), (2, 3, 0, 1))


def kernel(x_prompt, x_sample, cache_k, cache_v, cache_idx_k, state_hgrn, cache_mem_k, cache_mem_v, mem_prompt,
           norm_mix, w_in, hgrn_lb, hgrn_gnorm, w_out, norm_mem, norm_memkv, w_mq, w_mk, w_mv, w_mo,
           norm_ffn, ffn_w1, ffn_w3, ffn_w2, moe_router, moe_w1, moe_w3, moe_w2, norm_final):
    depth = w_in.shape[0]
    bp, tp, _ = x_prompt.shape
    bs, ts, _ = x_sample.shape
    past = cache_k.shape[2]
    n_mem = mem_prompt.shape[1]
    tabs_p = _rope_tables(jnp.arange(tp, dtype=I32))
    tabs_s = tuple(jnp.tile(a, (bs, 1)) for a in _rope_tables(past + jnp.arange(ts, dtype=I32)))
    xp = x_prompt.reshape(bp * tp, D_MODEL)
    xs = x_sample.reshape(bs * ts, D_MODEL)
    mem = mem_prompt.reshape(bp * n_mem, D_MODEL)
    s0_p = jnp.zeros((bp, B_HEADS, HEAD_DIM, HEAD_DIM), F32)
    gfin = norm_final.reshape(1, D_MODEL)
    row2 = lambda a: a.reshape(1, -1)
    outs = {k: [] for k in ("kp", "vp", "ikp", "sp", "mkp", "mvp", "ks", "vs", "iks", "ss")}
    for l in range(depth):
        w = {"norm_mix": row2(norm_mix[l]), "w_in": _regroup_w_in(w_in[l]), "hgrn_lb": hgrn_lb.astype(F32),
             "gnorm": row2(jnp.tile(hgrn_gnorm[l], B_HEADS)),
             "w_out_a": w_out[l, :A_WIDTH].astype(BF16), "w_out_b": w_out[l, A_WIDTH:].astype(BF16)}
        xp, kp, vp, ikp, sp = _mixer(xp, tabs_p, w, bsz=bp, t_len=tp, pos0=0, past=None, s0=s0_p, layer=l)
        xs, ks, vs, iks, ss = _mixer(xs, tabs_s, w, bsz=bs, t_len=ts, pos0=past,
                                     past=(cache_k[l], cache_v[l], cache_idx_k[l]), s0=state_hgrn[l], layer=l)
        wkv = jnp.concatenate([w_mk[l], w_mv[l]], axis=1).astype(BF16)
        mkv = _norm_matmul(mem, row2(norm_memkv[l]), wkv, _row_tile(bp * n_mem, 256))
        mk_p = mkv[:, :D_MODEL].reshape(bp, n_mem, D_MODEL)
        mv_p = mkv[:, D_MODEL:].reshape(bp, n_mem, D_MODEL)
        wq, wo, gm = w_mq[l].astype(BF16), w_mo[l].astype(BF16), row2(norm_mem[l])
        xp = _mem_attn(xp, gm, wq, mk_p.astype(BF16), mv_p.astype(BF16), wo, tp, _row_tile(tp, 256))
        xs = _mem_attn(xs, gm, wq, cache_mem_k[l].reshape(bs, n_mem, D_MODEL).astype(BF16),
                       cache_mem_v[l].reshape(bs, n_mem, D_MODEL).astype(BF16), wo, ts, _row_tile(ts, 256))
        final = l == depth - 1
        gf = row2(norm_ffn[l])
        if l % 2 == 0:
            w1, w3, w2 = ffn_w1[l // 2].astype(BF16), ffn_w3[l // 2].astype(BF16), ffn_w2[l // 2].astype(BF16)
            tf = _ff_tile(w1.shape[1])
            xp = _ffn(xp, gf, w1, w3, w2, gfin, final=final, tm=_row_tile(bp * tp, 512), tf=tf)
            xs = _ffn(xs, gf, w1, w3, w2, gfin, final=final, tm=_row_tile(bs * ts, 512), tf=tf)
        else:
            w1, w3, w2 = moe_w1[l // 2].astype(BF16), moe_w3[l // 2].astype(BF16), moe_w2[l // 2].astype(BF16)
            rt = moe_router[l // 2].T
            tf = _ff_tile(w1.shape[2])
            comb_p, sel_p = _router(xp, gf, rt, _row_tile(bp * tp, 512))
            comb_s, sel_s = _router(xs, gf, rt, _row_tile(bs * ts, 512))
            xp = _moe(xp, gf, sel_p, sel_p.T, comb_p.T, w1, w3, w2, gfin, final=final,
                      nb=_row_tile(bp * tp, 1024), tf=tf)
            xs = _moe(xs, gf, sel_s, sel_s.T, comb_s.T, w1, w3, w2, gfin, final=final,
                      nb=_row_tile(bs * ts, 1024), tf=tf)
        outs["kp"].append(_from_feature_major(kp, bp, tp, A_KV_HEADS))
        outs["vp"].append(_from_feature_major(vp, bp, tp, A_KV_HEADS))
        outs["ikp"].append(_from_feature_major(ikp, bp, tp, None)); outs["sp"].append(sp)
        outs["mkp"].append(mk_p.reshape(bp, n_mem, MEM_HEADS, MEM_HEAD_DIM))
        outs["mvp"].append(mv_p.reshape(bp, n_mem, MEM_HEADS, MEM_HEAD_DIM))
        outs["ks"].append(_from_feature_major(ks, bs, ts, A_KV_HEADS))
        outs["vs"].append(_from_feature_major(vs, bs, ts, A_KV_HEADS))
        outs["iks"].append(_from_feature_major(iks, bs, ts, None)); outs["ss"].append(ss)
    y_prompt = xp.reshape(bp, tp, D_MODEL)
    y_sample = xs.reshape(bs, ts, D_MODEL)
    st = lambda k: jnp.stack(outs[k])
    return (y_prompt, y_sample, st("kp"), st("vp"), st("ikp"), st("sp"), st("mkp"), st("mvp"),
            st("ks"), st("vs"), st("iks"), st("ss"))
```

```python
import functools

import numpy as np
import jax
import jax.numpy as jnp
from jax import lax
from jax.experimental import pallas as pl
from jax.experimental.pallas import tpu as pltpu

F32 = jnp.float32
BF16 = jnp.bfloat16
I32 = jnp.int32
I16 = jnp.int16

D_MODEL = 1024
CHUNK = 64
EPS = 1e-6
ROPE_THETA = 500000.0
HEAD_DIM = 64
ROT_DIM = HEAD_DIM // 4
A_HEADS = 8
A_KV_HEADS = 2
A_GROUP = A_HEADS // A_KV_HEADS
A_WIDTH = A_HEADS * HEAD_DIM
KV_WIDTH = A_KV_HEADS * HEAD_DIM
IDX_HEADS = 4
IDX_WIDTH = IDX_HEADS * HEAD_DIM
TOPK_MAX = 256
B_HEADS = 8
B_WIDTH = B_HEADS * HEAD_DIM
B_CHUNK = 16
MEM_HEADS = 4
MEM_HEAD_DIM = D_MODEL // MEM_HEADS
N_EXPERTS = 8
LANES = 128
HALF = 256
IN_WIDTH_PAD = 3200
VMEM_LIMIT = 56 * 1024 * 1024
INT_MIN = -2147483648
NEG_BIG = -1e30
LOG2E = 1.4426950408889634

_NT = (((1,), (1,)), ((), ()))
_TN = (((0,), (0,)), ((), ()))


def _cparams(*sem):
    return pltpu.CompilerParams(dimension_semantics=sem, vmem_limit_bytes=VMEM_LIMIT)


def _rms(x, g):
    return x * lax.rsqrt(jnp.mean(x * x, axis=-1, keepdims=True) + EPS) * g


def _sigmoid(x):
    return 1.0 / (1.0 + jnp.exp(-x))


def _dot(a, b):
    return jnp.dot(a, b, preferred_element_type=F32)


def _split2(x):
    hi = x.astype(BF16)
    lo = (x - hi.astype(F32)).astype(BF16)
    return hi, lo


def _split3(x):
    hi = x.astype(BF16)
    r = x - hi.astype(F32)
    mid = r.astype(BF16)
    lo = (r - mid.astype(F32)).astype(BF16)
    return hi, mid, lo


def _rope128(xb, c, s1, s2):
    return xb * c + pltpu.roll(xb, LANES - ROT_DIM // 2, 1) * s1 + pltpu.roll(xb, ROT_DIM // 2, 1) * s2


def _inproj_kernel(x_ref, g_ref, w_ref, c_ref, s1_ref, s2_ref,
                   qa_ref, hb_ref, iq_ref, kt_ref, vt_ref, ikt_ref, khm_ref, vaug_ref, ikb_ref, iwt_ref):
    xn = _rms(x_ref[...], g_ref[...]).astype(BF16)
    c, s1, s2 = c_ref[...], s1_ref[...], s2_ref[...]
    for j in range(A_WIDTH // LANES):
        h = _dot(xn, w_ref[:, j * LANES:(j + 1) * LANES])
        qa_ref[:, j * LANES:(j + 1) * LANES] = _rope128(h, c, s1, s2)
    for j in range(4):
        lo = A_WIDTH + j * B_WIDTH
        hb_ref[:, j * B_WIDTH:(j + 1) * B_WIDTH] = _dot(xn, w_ref[:, lo:lo + B_WIDTH])
    base = A_WIDTH + 4 * B_WIDTH
    k = _rope128(_dot(xn, w_ref[:, base:base + 128]), c, s1, s2)
    v = _dot(xn, w_ref[:, base + 128:base + 256])
    iq_ref[:, 0:128] = _rope128(_dot(xn, w_ref[:, base + 256:base + 384]), c, s1, s2)
    iq_ref[:, 128:256] = _rope128(_dot(xn, w_ref[:, base + 384:base + 512]), c, s1, s2)
    lane = lax.broadcasted_iota(I32, c.shape, 1)
    first = lane < HEAD_DIM
    ikw = _rope128(_dot(xn, w_ref[:, base + 512:base + 640]),
                   jnp.where(first, c, 1.0), jnp.where(first, s1, 0.0), jnp.where(first, s2, 0.0))
    kt_ref[...] = k.T
    vt_ref[...] = v.T
    ikw_t = ikw.T
    ikt_ref[...] = ikw_t[:HEAD_DIM]
    iwt_ref[...] = ikw_t[HEAD_DIM:HEAD_DIM + 8]
    ikb_ref[...] = ikw[:, :HEAD_DIM].astype(BF16)
    ones_col = jnp.where(lane == HEAD_DIM, 1.0, 0.0)
    for j in range(A_KV_HEADS):
        khm_ref[j] = k[:, j * HEAD_DIM:(j + 1) * HEAD_DIM].astype(BF16)
        vj = v if j == 0 else pltpu.roll(v, LANES - j * HEAD_DIM, 1)
        vaug_ref[j] = jnp.where(first, vj, ones_col).astype(BF16)


def _in_proj(x, g, w, tabs, tm):
    n = x.shape[0]
    nt = tabs[0].shape[0] // tm
    row = lambda i: (i, 0)
    tab = lambda i: (i % nt, 0)
    whole = lambda i: (0, 0)
    tmin = lambda i: (0, i)
    hmaj = lambda i: (0, i, 0)
    return pl.pallas_call(
        _inproj_kernel,
        grid=(n // tm,),
        in_specs=[pl.BlockSpec((tm, D_MODEL), row), pl.BlockSpec((1, D_MODEL), whole),
                  pl.BlockSpec((D_MODEL, IN_WIDTH_PAD), whole),
                  pl.BlockSpec((tm, LANES), tab), pl.BlockSpec((tm, LANES), tab), pl.BlockSpec((tm, LANES), tab)],
        out_specs=[pl.BlockSpec((tm, A_WIDTH), row), pl.BlockSpec((tm, 4 * B_WIDTH), row),
                   pl.BlockSpec((tm, IDX_WIDTH), row),
                   pl.BlockSpec((KV_WIDTH, tm), tmin), pl.BlockSpec((KV_WIDTH, tm), tmin),
                   pl.BlockSpec((HEAD_DIM, tm), tmin),
                   pl.BlockSpec((A_KV_HEADS, tm, HEAD_DIM), hmaj), pl.BlockSpec((A_KV_HEADS, tm, LANES), hmaj),
                   pl.BlockSpec((tm, HEAD_DIM), row), pl.BlockSpec((8, tm), tmin)],
        out_shape=[jax.ShapeDtypeStruct((n, A_WIDTH), F32), jax.ShapeDtypeStruct((n, 4 * B_WIDTH), F32),
                   jax.ShapeDtypeStruct((n, IDX_WIDTH), F32),
                   jax.ShapeDtypeStruct((KV_WIDTH, n), F32), jax.ShapeDtypeStruct((KV_WIDTH, n), F32),
                   jax.ShapeDtypeStruct((HEAD_DIM, n), F32),
                   jax.ShapeDtypeStruct((A_KV_HEADS, n, HEAD_DIM), BF16),
                   jax.ShapeDtypeStruct((A_KV_HEADS, n, LANES), BF16),
                   jax.ShapeDtypeStruct((n, HEAD_DIM), BF16), jax.ShapeDtypeStruct((8, n), F32)],
        compiler_params=_cparams("parallel"),
        name="in_proj",
    )(x, g, w, *tabs)


def _rope_tables(pos):
    half = ROT_DIM // 2
    inv_freq = 1.0 / (ROPE_THETA ** (jnp.arange(half, dtype=F32) * (2.0 / ROT_DIM)))
    ang = pos.astype(F32)[:, None] * inv_freq[None, :]
    cos, sin = jnp.cos(ang), jnp.sin(ang)
    t = pos.shape[0]
    pad = HEAD_DIM - ROT_DIM
    c = jnp.concatenate([cos, cos, jnp.ones((t, pad), F32)], axis=1)
    s1 = jnp.concatenate([-sin, jnp.zeros((t, half + pad), F32)], axis=1)
    s2 = jnp.concatenate([jnp.zeros((t, half), F32), sin, jnp.zeros((t, pad), F32)], axis=1)
    rep = LANES // HEAD_DIM
    return tuple(jnp.tile(a, (1, rep)) for a in (c, s1, s2))


def _regroup_w_in(w):
    o = np.cumsum([0, A_WIDTH, KV_WIDTH, KV_WIDTH, IDX_WIDTH, HEAD_DIM, IDX_HEADS, B_WIDTH, B_WIDTH, B_WIDTH, B_WIDTH])
    seg = lambda i: w[:, o[i]:o[i + 1]]
    used = A_WIDTH + 4 * B_WIDTH + 2 * KV_WIDTH + IDX_WIDTH + HEAD_DIM + IDX_HEADS
    pad = jnp.zeros((w.shape[0], IN_WIDTH_PAD - used), w.dtype)
    return jnp.concatenate([seg(0), seg(6), seg(7), seg(8), seg(9), seg(1), seg(2), seg(3), seg(4), seg(5), pad],
                           axis=1).astype(BF16)


def _dsa_kernel(q_ref, iq_ref, iw_ref, k_ref, v_ref, ik_ref, o_ref, key_ref, hi_ref, lo_ref, acc_ref, m_ref,
                s0_ref, s1_ref, alpha_ref, p_ref, bias_ref, *, l_real, pos0, kb_rows, topk, tq):
    i = pl.program_id(1)
    qpos0 = pos0 + i * tq
    kend = jnp.minimum(((qpos0 + tq - 1) // CHUNK + 1) * CHUNK, l_real)
    cb_rows = 2 * kb_rows
    nkc = (kend + cb_rows - 1) // cb_rows
    nkb = 2 * nkc
    qchunk = (qpos0 + lax.broadcasted_iota(I32, (1, tq), 1)) // CHUNK
    row_iota = lax.broadcasted_iota(I32, (kb_rows, tq), 0)

    iq = iq_ref[...].astype(BF16)
    iq4 = jnp.concatenate([iq[:, h * HEAD_DIM:(h + 1) * HEAD_DIM] for h in range(IDX_HEADS)], axis=0)
    iw = iw_ref[...] * (IDX_HEADS ** -0.5 * HEAD_DIM ** -0.5)

    def score_body(kb, carry):
        r0 = pl.multiple_of(kb * kb_rows, kb_rows)
        ikb = ik_ref[pl.ds(r0, kb_rows), :]
        lg4 = lax.dot_general(ikb, iq4, _NT, preferred_element_type=F32)
        acc = jnp.zeros((kb_rows, tq), F32)
        for h in range(IDX_HEADS):
            acc = acc + iw[h:h + 1, :] * jnp.maximum(lg4[:, h * tq:(h + 1) * tq], 0.0)
        bits = pltpu.bitcast(acc + 0.0, I32)
        key = bits ^ ((bits >> 31) & 0x7FFFFFFF)
        kpos = r0 + row_iota
        adm = (kpos // CHUNK <= qchunk) & (kpos < l_real)
        key = jnp.where(adm, key, INT_MIN)
        key_ref[pl.ds(r0, kb_rows), :] = key
        hi_ref[pl.ds(r0, kb_rows), :] = (key >> 16).astype(I16)
        return carry

    lax.fori_loop(0, nkb, score_body, 0)

    def count(pred):
        def body(kc, c):
            r0 = pl.multiple_of(kc * cb_rows, cb_rows)
            hit = pred(key_ref[pl.ds(r0, cb_rows), :], r0)
            return c + jnp.where(hit, 1, 0).reshape(cb_rows // 8, 8, tq).sum(axis=0)
        c = lax.fori_loop(0, nkc, body, jnp.zeros((8, tq), I32))
        return c.sum(axis=0, keepdims=True)

    def count16(ref, pred):
        one, zero = jnp.ones((), BF16), jnp.zeros((), BF16)

        def body(kc, c):
            r0 = pl.multiple_of(kc * cb_rows, cb_rows)
            w = jnp.where(pred(ref[pl.ds(r0, cb_rows), :]), one, zero).reshape(cb_rows // 16, 16, tq)
            parts = [w[r] for r in range(cb_rows // 16)]
            while len(parts) > 1:
                parts = [a + b for a, b in zip(parts[::2], parts[1::2])] + parts[len(parts) & ~1:]
            return c + parts[0]
        c = lax.fori_loop(0, nkc, body, jnp.zeros((16, tq), BF16))
        return c.astype(F32).sum(axis=0, keepdims=True).astype(I32)

    def bisect16(ref, kth):
        c0 = count16(ref, lambda blk: blk >= jnp.zeros((), I16))
        start = jnp.where(c0 >= kth, 0, -32768).astype(I32)

        def bit_body(it, prefix):
            cand = prefix | lax.shift_left(jnp.int32(1), 14 - it)
            c = count16(ref, lambda blk: blk >= cand.astype(I16))
            return jnp.where(c >= kth, cand, prefix)

        return lax.fori_loop(0, 15, bit_body, start)

    tau_hi = bisect16(hi_ref, topk)
    tau_hi16 = tau_hi.astype(I16)
    kth_lo = topk - count16(hi_ref, lambda blk: blk > tau_hi16)

    def lo_body(kc, carry):
        r0 = pl.multiple_of(kc * cb_rows, cb_rows)
        lo = ((key_ref[pl.ds(r0, cb_rows), :] & 0xFFFF) - 32768).astype(I16)
        lo_ref[pl.ds(r0, cb_rows), :] = jnp.where(hi_ref[pl.ds(r0, cb_rows), :] == tau_hi16, lo,
                                                 jnp.full((), -32768, I16))
        return carry

    lax.fori_loop(0, nkc, lo_body, 0)
    tau_lo = bisect16(lo_ref, kth_lo)
    tau = lax.shift_left(tau_hi, 16) | (tau_lo + 32768)
    n_ge = count(lambda blk, r0: blk >= tau)

    @pl.when(jnp.max(jnp.where((n_ge > topk) & (tau != INT_MIN), 1, 0)) > 0)
    def _():
        need = (topk - count(lambda blk, r0: blk > tau)).astype(F32)
        ri = lax.broadcasted_iota(I32, (kb_rows, kb_rows), 0)
        ci = lax.broadcasted_iota(I32, (kb_rows, kb_rows), 1)
        upto = jnp.where(ci <= ri, 1.0, 0.0).astype(BF16)

        def strike(kc, seen):
            for half in range(2):
                rows = pl.ds(pl.multiple_of(kc * cb_rows + half * kb_rows, kb_rows), kb_rows)
                key = key_ref[rows, :]
                tie = key == tau
                rank = _dot(upto, jnp.where(tie, 1.0, 0.0).astype(BF16)) + seen
                key_ref[rows, :] = jnp.where(tie & (rank > need), INT_MIN, key)
                seen = rank[kb_rows - 1:kb_rows, :]
            return seen

        lax.fori_loop(0, nkc, strike, jnp.zeros((1, tq), F32))

    lane_rep = kb_rows // LANES
    tau_c = jnp.tile(jnp.broadcast_to(tau, (LANES, tq)).T, (1, lane_rep))
    acc_ref[...] = jnp.zeros(acc_ref.shape, F32)
    m_ref[...] = jnp.full(m_ref.shape, NEG_BIG, F32)
    qs = (q_ref[...] * (HEAD_DIM ** -0.5 * LOG2E)).astype(BF16)
    q4 = [jnp.concatenate([qs[:, (j * A_GROUP + g) * HEAD_DIM:(j * A_GROUP + g + 1) * HEAD_DIM]
                           for g in range(A_GROUP)], axis=0) for j in range(A_KV_HEADS)]

    def qk_scores(kb, s_slot):
        r0 = pl.multiple_of(jnp.minimum(kb, nkb - 1) * kb_rows, kb_rows)
        for j in range(A_KV_HEADS):
            s_slot[j] = lax.dot_general(q4[j], k_ref[j, pl.ds(r0, kb_rows), :], _NT,
                                        preferred_element_type=F32)

    def softmax_pv(kb, s_slot):
        r0 = pl.multiple_of(kb * kb_rows, kb_rows)
        key_t = key_ref[pl.ds(r0, kb_rows), :].T
        bias_ref[...] = jnp.where((key_t >= tau_c) & (key_t != INT_MIN), 0.0, NEG_BIG)
        for j in range(A_KV_HEADS):
            for g in range(A_GROUP):
                rows = slice(g * tq, (g + 1) * tq)
                m_old = m_ref[j, g]
                m_new = jnp.maximum(m_old, jnp.max(s_slot[j, rows, :] + bias_ref[...], axis=1, keepdims=True))
                alpha_ref[j, g] = jnp.exp2(m_old - m_new)
                m_ref[j, g] = m_new
            for g in range(A_GROUP):
                rows = slice(g * tq, (g + 1) * tq)
                sm = s_slot[j, rows, :] + bias_ref[...]
                p_ref[j, rows, :] = jnp.exp2(sm - jnp.tile(m_ref[j, g], (1, lane_rep))).astype(BF16)
            pv = _dot(p_ref[j], v_ref[j, pl.ds(r0, kb_rows), :])
            acc_ref[j] = alpha_ref[j] * acc_ref[j] + pv.reshape(A_GROUP, tq, LANES)

    qk_scores(0, s0_ref)

    def pair_body(i, carry):
        kb = 2 * i
        qk_scores(kb + 1, s1_ref)
        softmax_pv(kb, s0_ref)
        qk_scores(kb + 2, s0_ref)
        softmax_pv(kb + 1, s1_ref)
        return carry

    lax.fori_loop(0, nkc, pair_body, 0)

    outs = []
    for j in range(A_KV_HEADS):
        acc = acc_ref[j]
        for g in range(A_GROUP):
            outs.append(acc[g, :, :HEAD_DIM] * (1.0 / acc[g, :, HEAD_DIM:HEAD_DIM + 1]))
    o_ref[...] = jnp.concatenate(outs, axis=1)


def _dsa(q, iq, iw_t, k_hm, v_aug, ik, *, bsz, l_real, pos0, kb_rows, topk, tq):
    t_len = q.shape[0] // bsz
    l_pad = ik.shape[0] // bsz
    nq = t_len // tq
    assert l_pad // 16 <= 256, "packed hit counts are summed in bf16, exact only up to 256 per slot"
    kern = functools.partial(_dsa_kernel, l_real=l_real, pos0=pos0, kb_rows=kb_rows, topk=topk, tq=tq)
    qrow = lambda b, i: (b * nq + i, 0)
    return pl.pallas_call(
        kern,
        grid=(bsz, nq),
        in_specs=[pl.BlockSpec((tq, A_WIDTH), qrow), pl.BlockSpec((tq, IDX_WIDTH), qrow),
                  pl.BlockSpec((8, tq), lambda b, i: (0, b * nq + i)),
                  pl.BlockSpec((A_KV_HEADS, l_pad, HEAD_DIM), lambda b, i: (0, b, 0)),
                  pl.BlockSpec((A_KV_HEADS, l_pad, LANES), lambda b, i: (0, b, 0)),
                  pl.BlockSpec((l_pad, HEAD_DIM), lambda b, i: (b, 0))],
        out_specs=pl.BlockSpec((tq, A_WIDTH), qrow),
        out_shape=jax.ShapeDtypeStruct((bsz * t_len, A_WIDTH), F32),
        scratch_shapes=[pltpu.VMEM((l_pad, tq), I32), pltpu.VMEM((l_pad, tq), I16), pltpu.VMEM((l_pad, tq), I16),
                        pltpu.VMEM((A_KV_HEADS, A_GROUP, tq, LANES), F32),
                        pltpu.VMEM((A_KV_HEADS, A_GROUP, tq, LANES), F32),
                        pltpu.VMEM((A_KV_HEADS, A_GROUP * tq, kb_rows), F32),
                        pltpu.VMEM((A_KV_HEADS, A_GROUP * tq, kb_rows), F32),
                        pltpu.VMEM((A_KV_HEADS, A_GROUP, tq, LANES), F32),
                        pltpu.VMEM((A_KV_HEADS, A_GROUP * tq, kb_rows), BF16),
                        pltpu.VMEM((tq, kb_rows), F32)],
        compiler_params=_cparams("parallel", "arbitrary"),
        name="dsa",
    )(q, iq, iw_t, k_hm, v_aug, ik)


def _hgrn_kernel(bq_ref, bf_ref, bi_ref, bg_ref, lb_ref, gn_ref, s0_ref, o_ref, sfin_ref,
                 st_ref, q_s, kk_s, b_s, qe_s, kd_s, dec_s, o_s, e_s, *, layer, tb):
    tstep = pl.program_id(1)
    nchunk = tb // B_CHUNK

    @pl.when(tstep == 0)
    def _():
        st_ref[...] = s0_ref[0]

    lbr = lb_ref[...]
    lbe = jnp.exp(lbr - jnp.max(lbr, axis=0, keepdims=True))
    lbs = lbe / jnp.sum(lbe, axis=0, keepdims=True)
    cum = lbs[0:1, :]
    first = cum
    for r in range(1, layer + 1):
        cum = cum + lbs[r:r + 1, :]
    lb = cum - first

    f = lb + (1.0 - lb) * _sigmoid(bf_ref[...])
    bq = bq_ref[...]
    q = bq * _sigmoid(bq) * (HEAD_DIM ** -0.5)
    g3 = _split3(jnp.log(f))
    ri = lax.broadcasted_iota(I32, (tb, tb), 0)
    ci = lax.broadcasted_iota(I32, (tb, tb), 1)
    same = (ri // B_CHUNK) == (ci // B_CHUNK)
    tri = jnp.where(same & (ci <= ri), 1.0, 0.0).astype(BF16)
    ones = jnp.where(same, 1.0, 0.0).astype(BF16)
    b = _dot(tri, g3[0]) + _dot(tri, g3[1]) + _dot(tri, g3[2])
    blast = _dot(ones, g3[0]) + _dot(ones, g3[1]) + _dot(ones, g3[2])
    q_s[...] = q
    kk_s[...] = 1.0 - f
    b_s[...] = b
    qe_s[...] = q * jnp.exp(b)
    kd_s[...] = (1.0 - f) * jnp.exp(blast - b)
    dec_s[...] = jnp.exp(blast)

    hi = lax.broadcasted_iota(I32, (HALF, HALF), 0) // HEAD_DIM
    hj = lax.broadcasted_iota(I32, (HALF, HALF), 1) // HEAD_DIM
    bd_mask = hi == hj
    bd = jnp.where(bd_mask, 1.0, 0.0).astype(BF16)
    t_iota = lax.broadcasted_iota(I32, (B_CHUNK, B_WIDTH), 0)

    def seg_sum(x, two_pass):
        out = []
        for half in range(2):
            xs = x[:, half * HALF:(half + 1) * HALF]
            if two_pass:
                xh, xl = _split2(xs)
                out.append(_dot(xh, bd) + _dot(xl, bd))
            else:
                out.append(_dot(xs.astype(BF16), bd))
        return jnp.concatenate(out, axis=1)

    def chunk_body(c, carry):
        r0 = pl.multiple_of(c * B_CHUNK, B_CHUNK)
        rows = pl.ds(r0, B_CHUNK)
        qc, kc, bc, vc = q_s[rows, :], kk_s[rows, :], b_s[rows, :], bi_ref[rows, :]
        qe, kd = qe_s[rows, :].astype(BF16), kd_s[rows, :].astype(BF16)
        dec = dec_s[pl.ds(r0, 1), :]
        for s in range(B_CHUNK):
            ex = jnp.exp(jnp.where(t_iota >= s, bc - bc[s:s + 1, :], NEG_BIG))
            e_s[s * B_CHUNK:(s + 1) * B_CHUNK, :] = qc * kc[s:s + 1, :] * ex
        aexp = seg_sum(e_s[...], False)
        o = jnp.zeros((B_CHUNK, B_WIDTH), F32)
        for s in range(B_CHUNK):
            o = o + aexp[s * B_CHUNK:(s + 1) * B_CHUNK, :] * vc[s:s + 1, :]
        inter = []
        vb = vc.astype(BF16)
        for half in range(2):
            cols = slice(half * HALF, (half + 1) * HALF)
            st = st_ref[half]
            inter.append(lax.dot_general(qe[:, cols], st.astype(BF16), _NT, preferred_element_type=F32))
            upd = lax.dot_general(vb[:, cols], kd[:, cols], _TN, preferred_element_type=F32)
            st_ref[half] = st * dec[:, cols] + jnp.where(bd_mask, upd, 0.0)
        o_s[rows, :] = o + jnp.concatenate(inter, axis=1)
        return carry

    lax.fori_loop(0, nchunk, chunk_body, 0)

    o = o_s[...]
    ms = seg_sum(o * o, True) * (1.0 / HEAD_DIM)
    bg = bg_ref[...]
    o_ref[...] = o * lax.rsqrt(ms + EPS) * gn_ref[...] * (bg * _sigmoid(bg))

    @pl.when(tstep == pl.num_programs(1) - 1)
    def _():
        sfin_ref[0] = st_ref[...]


def _hgrn(hb, lb_raw, gn_tiled, s0_bd, *, layer, bsz, t_len, tb):
    nt = t_len // tb
    col = lambda j: (lambda b, t: (b * nt + t, j))
    scr = lambda r: pltpu.VMEM((r, B_WIDTH), F32)
    kern = functools.partial(_hgrn_kernel, layer=layer, tb=tb)
    return pl.pallas_call(
        kern,
        grid=(bsz, nt),
        in_specs=[pl.BlockSpec((tb, B_WIDTH), col(0)), pl.BlockSpec((tb, B_WIDTH), col(1)),
                  pl.BlockSpec((tb, B_WIDTH), col(2)), pl.BlockSpec((tb, B_WIDTH), col(3)),
                  pl.BlockSpec(lb_raw.shape, lambda b, t: (0, 0)),
                  pl.BlockSpec((1, B_WIDTH), lambda b, t: (0, 0)),
                  pl.BlockSpec((1, 2, HALF, HALF), lambda b, t: (b, 0, 0, 0))],
        out_specs=[pl.BlockSpec((tb, B_WIDTH), lambda b, t: (b * nt + t, 0)),
                   pl.BlockSpec((1, 2, HALF, HALF), lambda b, t: (b, 0, 0, 0))],
        out_shape=[jax.ShapeDtypeStruct((bsz * t_len, B_WIDTH), F32),
                   jax.ShapeDtypeStruct((bsz, 2, HALF, HALF), F32)],
        scratch_shapes=[pltpu.VMEM((2, HALF, HALF), F32), scr(tb), scr(tb), scr(tb), scr(tb), scr(tb), scr(tb),
                        scr(tb), scr(B_CHUNK * B_CHUNK)],
        compiler_params=_cparams("parallel", "arbitrary"),
        name="hgrn2",
    )(hb, hb, hb, hb, lb_raw, gn_tiled, s0_bd)


def _state_to_tiles(s):
    bsz = s.shape[0]
    st = jnp.swapaxes(s, -1, -2).reshape(bsz, 2, B_HEADS // 2, HEAD_DIM, HEAD_DIM)
    eye = jnp.eye(B_HEADS // 2, dtype=s.dtype)
    t = st[:, :, :, :, None, :] * eye[None, None, :, None, :, None]
    return t.reshape(bsz, 2, HALF, HALF)


def _tiles_to_state(t):
    bsz = t.shape[0]
    t6 = t.reshape(bsz, 2, B_HEADS // 2, HEAD_DIM, B_HEADS // 2, HEAD_DIM)
    diag = jnp.stack([t6[:, :, h, :, h, :] for h in range(B_HEADS // 2)], axis=2)
    return jnp.swapaxes(diag.reshape(bsz, B_HEADS, HEAD_DIM, HEAD_DIM), -1, -2)


def _outproj_kernel(a_ref, b_ref, wa_ref, wb_ref, x_ref, o_ref):
    o_ref[...] = (x_ref[...] + _dot(a_ref[...].astype(BF16), wa_ref[...])
                  + _dot(b_ref[...].astype(BF16), wb_ref[...]))


def _out_proj(a, b, wa, wb, x, tm):
    n = x.shape[0]
    row = lambda i: (i, 0)
    whole = lambda i: (0, 0)
    return pl.pallas_call(
        _outproj_kernel,
        grid=(n // tm,),
        in_specs=[pl.BlockSpec((tm, A_WIDTH), row), pl.BlockSpec((tm, B_WIDTH), row),
                  pl.BlockSpec((A_WIDTH, D_MODEL), whole), pl.BlockSpec((B_WIDTH, D_MODEL), whole),
                  pl.BlockSpec((tm, D_MODEL), row)],
        out_specs=pl.BlockSpec((tm, D_MODEL), row),
        out_shape=jax.ShapeDtypeStruct((n, D_MODEL), F32),
        compiler_params=_cparams("parallel"),
        name="out_proj",
    )(a, b, wa, wb, x)


def _normmm_kernel(x_ref, g_ref, w_ref, o_ref):
    o_ref[...] = _dot(_rms(x_ref[...], g_ref[...]).astype(BF16), w_ref[...])


def _norm_matmul(x, g, w, tm):
    n, m = x.shape[0], w.shape[1]
    return pl.pallas_call(
        _normmm_kernel,
        grid=(n // tm,),
        in_specs=[pl.BlockSpec((tm, D_MODEL), lambda i: (i, 0)), pl.BlockSpec((1, D_MODEL), lambda i: (0, 0)),
                  pl.BlockSpec((D_MODEL, m), lambda i: (0, 0))],
        out_specs=pl.BlockSpec((tm, m), lambda i: (i, 0)),
        out_shape=jax.ShapeDtypeStruct((n, m), F32),
        compiler_params=_cparams("parallel"),
        name="norm_matmul",
    )(x, g, w)


def _memattn_kernel(x_ref, g_ref, wq_ref, mk_ref, mv_ref, wo_ref, o_ref):
    x = x_ref[...]
    q = _dot(_rms(x, g_ref[...]).astype(BF16), wq_ref[...])
    heads = []
    for h in range(MEM_HEADS):
        cols = slice(h * MEM_HEAD_DIM, (h + 1) * MEM_HEAD_DIM)
        s = lax.dot_general(q[:, cols].astype(BF16), mk_ref[0, :, cols], _NT,
                            preferred_element_type=F32) * (MEM_HEAD_DIM ** -0.5)
        p = jnp.exp(s - jnp.max(s, axis=-1, keepdims=True))
        p = p * (1.0 / jnp.sum(p, axis=-1, keepdims=True))
        heads.append(_dot(p.astype(BF16), mv_ref[0, :, cols]).astype(BF16))
    o_ref[...] = x + _dot(jnp.concatenate(heads, axis=1), wo_ref[...])


def _mem_attn(x, g, wq, mk, mv, wo, rows_per_batch, tm):
    n = x.shape[0]
    n_mem = mk.shape[1]
    per = rows_per_batch // tm
    row = lambda i: (i, 0)
    whole = lambda i: (0, 0)
    mem = lambda i: (i // per, 0, 0)
    return pl.pallas_call(
        _memattn_kernel,
        grid=(n // tm,),
        in_specs=[pl.BlockSpec((tm, D_MODEL), row), pl.BlockSpec((1, D_MODEL), whole),
                  pl.BlockSpec((D_MODEL, D_MODEL), whole),
                  pl.BlockSpec((1, n_mem, D_MODEL), mem), pl.BlockSpec((1, n_mem, D_MODEL), mem),
                  pl.BlockSpec((D_MODEL, D_MODEL), whole)],
        out_specs=pl.BlockSpec((tm, D_MODEL), row),
        out_shape=jax.ShapeDtypeStruct((n, D_MODEL), F32),
        compiler_params=_cparams("parallel"),
        name="mem_attn",
    )(x, g, wq, mk, mv, wo)


def _swiglu_mid(xn, w1, w3):
    a = _dot(xn, w1)
    return (a * _sigmoid(a) * _dot(xn, w3)).astype(BF16)


def _ffn_kernel(x_ref, g_ref, w1_ref, w3_ref, w2_ref, gf_ref, o_ref, xn_ref, *, final):
    j = pl.program_id(1)

    @pl.when(j == 0)
    def _():
        x = x_ref[...]
        xn_ref[...] = _rms(x, g_ref[...]).astype(BF16)
        o_ref[...] = x

    o_ref[...] += _dot(_swiglu_mid(xn_ref[...], w1_ref[...], w3_ref[...]), w2_ref[...])

    if final:
        @pl.when(j == pl.num_programs(1) - 1)
        def _():
            o_ref[...] = _rms(o_ref[...], gf_ref[...])


def _ffn(x, g, w1, w3, w2, gf, *, final, tm, tf):
    n = x.shape[0]
    d_ff = w1.shape[1]
    row = lambda i, j: (i, 0)
    whole = lambda i, j: (0, 0)
    return pl.pallas_call(
        functools.partial(_ffn_kernel, final=final),
        grid=(n // tm, d_ff // tf),
        in_specs=[pl.BlockSpec((tm, D_MODEL), row), pl.BlockSpec((1, D_MODEL), whole),
                  pl.BlockSpec((D_MODEL, tf), lambda i, j: (0, j)), pl.BlockSpec((D_MODEL, tf), lambda i, j: (0, j)),
                  pl.BlockSpec((tf, D_MODEL), lambda i, j: (j, 0)), pl.BlockSpec((1, D_MODEL), whole)],
        out_specs=pl.BlockSpec((tm, D_MODEL), row),
        out_shape=jax.ShapeDtypeStruct((n, D_MODEL), F32),
        scratch_shapes=[pltpu.VMEM((tm, D_MODEL), BF16)],
        compiler_params=_cparams("parallel", "arbitrary"),
        name="ffn",
    )(x, g, w1, w3, w2, gf)


def _router_kernel(x_ref, g_ref, rt_ref, comb_ref, sel_ref):
    xn = _rms(x_ref[...], g_ref[...])
    logits = lax.dot_general(rt_ref[...], xn, _NT, preferred_element_type=F32,
                             precision=lax.Precision.HIGHEST)
    e_idx = lax.broadcasted_iota(I32, logits.shape, 0)
    m1 = jnp.max(logits, axis=0, keepdims=True)
    i1 = jnp.min(jnp.where(logits == m1, e_idx, N_EXPERTS), axis=0, keepdims=True)
    rest = jnp.where(e_idx == i1, -jnp.inf, logits)
    m2 = jnp.max(rest, axis=0, keepdims=True)
    i2 = jnp.min(jnp.where(rest == m2, e_idx, N_EXPERTS), axis=0, keepdims=True)
    e2 = jnp.exp(m2 - m1)
    g1 = 1.0 / (1.0 + e2)
    comb_ref[...] = jnp.where(e_idx == i1, g1, 0.0) + jnp.where(e_idx == i2, e2 * g1, 0.0)
    sel_ref[...] = jnp.where((e_idx == i1) | (e_idx == i2), 1.0, 0.0)


def _router(x, g, rt, tm):
    n = x.shape[0]
    return pl.pallas_call(
        _router_kernel,
        grid=(n // tm,),
        in_specs=[pl.BlockSpec((tm, D_MODEL), lambda i: (i, 0)), pl.BlockSpec((1, D_MODEL), lambda i: (0, 0)),
                  pl.BlockSpec((N_EXPERTS, D_MODEL), lambda i: (0, 0))],
        out_specs=[pl.BlockSpec((N_EXPERTS, tm), lambda i: (0, i)), pl.BlockSpec((N_EXPERTS, tm), lambda i: (0, i))],
        out_shape=[jax.ShapeDtypeStruct((N_EXPERTS, n), F32), jax.ShapeDtypeStruct((N_EXPERTS, n), F32)],
        compiler_params=_cparams("parallel"),
        name="moe_router",
    )(x, g, rt)


def _moe_kernel(x_ref, g_ref, selt_ref, sel_ref, comb_ref, w1_ref, w3_ref, w2_ref, gf_ref, o_ref,
                xn_ref, posr_ref, posc_ref, cnt_ref, xg_ref, yg_ref, *, final, nb, sub):
    e = pl.program_id(1)
    j = pl.program_id(2)
    last_j = pl.num_programs(2) - 1

    @pl.when((e == 0) & (j == 0))
    def _():
        x = x_ref[...]
        xn_ref[...] = _rms(x, g_ref[...]).astype(BF16)
        o_ref[...] = x
        ri = lax.broadcasted_iota(I32, (sub, sub), 0)
        ci = lax.broadcasted_iota(I32, (sub, sub), 1)
        before_r = jnp.where(ri < ci, 1.0, 0.0).astype(BF16)
        before_c = jnp.where(ci < ri, 1.0, 0.0).astype(BF16)
        run_r = jnp.zeros((N_EXPERTS, 1), F32)
        run_c = jnp.zeros((1, N_EXPERTS), F32)
        for c in range(nb // sub):
            cols = slice(c * sub, (c + 1) * sub)
            mr = selt_ref[:, cols]
            posr_ref[:, cols] = jnp.where(mr > 0.0, _dot(mr.astype(BF16), before_r) + run_r, -1.0)
            run_r = run_r + jnp.sum(mr, axis=1, keepdims=True)
            mc = sel_ref[cols, :]
            posc_ref[cols, :] = jnp.where(mc > 0.0, _dot(before_c, mc.astype(BF16)) + run_c, -1.0)
            run_c = run_c + jnp.sum(mc, axis=0, keepdims=True)
        for k in range(N_EXPERTS):
            cnt_ref[k] = run_c[0, k].astype(I32)

    cnt = cnt_ref[e]
    nfull = cnt // sub
    rem = cnt - nfull * sub
    tail_sizes = [m for m in (64, 128) if m < sub] + [sub]

    def for_tiles(fn):
        def body(s, carry):
            fn(pl.multiple_of(s * sub, sub), sub)
            return carry
        lax.fori_loop(0, nfull, body, 0)
        r0 = pl.multiple_of(nfull * sub, sub)
        lo = 0
        for m in tail_sizes:
            @pl.when((rem > lo) & (rem <= m))
            def _():
                fn(r0, m)
            lo = m

    def gather_matrix(r0, m):
        slot_r = posr_ref[pl.ds(e, 1), :]
        want = (r0 + lax.broadcasted_iota(I32, (m, nb), 0)).astype(F32)
        return jnp.where(slot_r == want, 1.0, 0.0).astype(BF16)

    def gather(r0, m):
        xg_ref[pl.ds(r0, m), :] = _dot(gather_matrix(r0, m), xn_ref[...]).astype(BF16)

    def ffn(r0, m):
        rows = pl.ds(r0, m)
        part = _dot(_swiglu_mid(xg_ref[rows, :], w1_ref[0], w3_ref[0]), w2_ref[0])

        @pl.when(j == 0)
        def _():
            yg_ref[rows, :] = part

        @pl.when(j > 0)
        def _():
            yg_ref[rows, :] += part

    @pl.when(j == 0)
    def _():
        for_tiles(gather)

    for_tiles(ffn)

    @pl.when(j == last_j)
    def _():
        c3 = _split3(comb_ref[...])
        lane8 = lax.broadcasted_iota(I32, (nb, N_EXPERTS), 1)
        slot_c = jnp.sum(jnp.where(lane8 == e, posc_ref[...], 0.0), axis=1, keepdims=True)

        def scatter(r0, m):
            pm = gather_matrix(r0, m)
            gates = _dot(pm, c3[0]) + _dot(pm, c3[1]) + _dot(pm, c3[2])
            lane_e = lax.broadcasted_iota(I32, (m, N_EXPERTS), 1) == e
            gate = jnp.sum(jnp.where(lane_e, gates, 0.0), axis=1, keepdims=True)
            y = (yg_ref[pl.ds(r0, m), :] * gate).astype(BF16)
            want = (r0 + lax.broadcasted_iota(I32, (nb, m), 1)).astype(F32)
            o_ref[...] += _dot(jnp.where(slot_c == want, 1.0, 0.0).astype(BF16), y)

        for_tiles(scatter)

        if final:
            @pl.when(e == pl.num_programs(1) - 1)
            def _():
                o_ref[...] = _rms(o_ref[...], gf_ref[...])


def _moe(x, g, sel_t, sel, comb, w1, w3, w2, gf, *, final, nb, tf):
    n = x.shape[0]
    d_ff = w1.shape[2]
    sub = min(nb, 256)
    row = lambda i, e, j: (i, 0)
    whole = lambda i, e, j: (0, 0)
    return pl.pallas_call(
        functools.partial(_moe_kernel, final=final, nb=nb, sub=sub),
        grid=(n // nb, N_EXPERTS, d_ff // tf),
        in_specs=[pl.BlockSpec((nb, D_MODEL), row), pl.BlockSpec((1, D_MODEL), whole),
                  pl.BlockSpec((N_EXPERTS, nb), lambda i, e, j: (0, i)),
                  pl.BlockSpec((nb, N_EXPERTS), row), pl.BlockSpec((nb, N_EXPERTS), row),
                  pl.BlockSpec((1, D_MODEL, tf), lambda i, e, j: (e, 0, j)),
                  pl.BlockSpec((1, D_MODEL, tf), lambda i, e, j: (e, 0, j)),
                  pl.BlockSpec((1, tf, D_MODEL), lambda i, e, j: (e, j, 0)), pl.BlockSpec((1, D_MODEL), whole)],
        out_specs=pl.BlockSpec((nb, D_MODEL), row),
        out_shape=jax.ShapeDtypeStruct((n, D_MODEL), F32),
        scratch_shapes=[pltpu.VMEM((nb, D_MODEL), BF16), pltpu.VMEM((N_EXPERTS, nb), F32),
                        pltpu.VMEM((nb, N_EXPERTS), F32), pltpu.SMEM((N_EXPERTS,), I32),
                        pltpu.VMEM((nb, D_MODEL), BF16), pltpu.VMEM((nb, D_MODEL), F32)],
        compiler_params=_cparams("parallel", "arbitrary", "arbitrary"),
        name="moe",
    )(x, g, sel_t, sel, comb, w1, w3, w2, gf)


def _rmsnorm_kernel(x_ref, g_ref, o_ref):
    o_ref[...] = _rms(x_ref[...], g_ref[...])


def _rmsnorm(x, g, tm):
    n = x.shape[0]
    return pl.pallas_call(
        _rmsnorm_kernel,
        grid=(n // tm,),
        in_specs=[pl.BlockSpec((tm, D_MODEL), lambda i: (i, 0)), pl.BlockSpec((1, D_MODEL), lambda i: (0, 0))],
        out_specs=pl.BlockSpec((tm, D_MODEL), lambda i: (i, 0)),
        out_shape=jax.ShapeDtypeStruct((n, D_MODEL), F32),
        compiler_params=_cparams("parallel"),
        name="final_norm",
    )(x, g)


def _row_tile(n, cap):
    t = min(n, cap)
    assert n % t == 0, (n, cap)
    return t


def _ff_tile(d_ff):
    for parts in (2, 4, 11, 22):
        if d_ff % parts == 0 and (d_ff // parts) % LANES == 0:
            return d_ff // parts
    return d_ff


def _mixer(x, tabs, w, *, bsz, t_len, pos0, past, s0, layer):
    n = bsz * t_len
    qa, hb, iq, k_t, v_t, ik_t, k_hm, v_aug, ik_b, iw_t = _in_proj(x, w["norm_mix"], w["w_in"], tabs,
                                                                   _row_tile(n, 256))
    l_real = t_len if past is None else past[0].shape[1] + t_len
    topk = min(TOPK_MAX, l_real // 4)
    tq = 2 * LANES if t_len % (2 * LANES) == 0 else LANES
    kb_rows = 256 if l_real % 256 == 0 else LANES
    l_pad = -(-l_real // (2 * kb_rows)) * (2 * kb_rows)
    t_pad = -(-t_len // tq) * tq
    if past is not None:
        pk, pv, pik = past
        one = jnp.ones(pv.shape[:2] + (A_KV_HEADS, 1), BF16)
        zero = jnp.zeros(pv.shape[:2] + (A_KV_HEADS, LANES - HEAD_DIM - 1), BF16)
        pv_aug = jnp.concatenate([pv.astype(BF16), one, zero], axis=-1)
        hm = lambda a: jnp.transpose(a, (2, 0, 1, 3))
        k_hm = jnp.concatenate([hm(pk.astype(BF16)), k_hm.reshape(A_KV_HEADS, bsz, t_len, HEAD_DIM)], axis=2)
        v_aug = jnp.concatenate([hm(pv_aug), v_aug.reshape(A_KV_HEADS, bsz, t_len, LANES)], axis=2)
        ik_b = jnp.concatenate([pik.astype(BF16), ik_b.reshape(bsz, t_len, HEAD_DIM)], axis=1)
    if l_pad != l_real or past is not None:
        padl = l_pad - l_real
        k_hm = jnp.pad(k_hm.reshape(A_KV_HEADS, bsz, l_real, HEAD_DIM), ((0, 0), (0, 0), (0, padl), (0, 0)))
        v_aug = jnp.pad(v_aug.reshape(A_KV_HEADS, bsz, l_real, LANES), ((0, 0), (0, 0), (0, padl), (0, 0)))
        ik_b = jnp.pad(ik_b.reshape(bsz, l_real, HEAD_DIM), ((0, 0), (0, padl), (0, 0)))
        k_hm = k_hm.reshape(A_KV_HEADS, bsz * l_pad, HEAD_DIM)
        v_aug = v_aug.reshape(A_KV_HEADS, bsz * l_pad, LANES)
        ik_b = ik_b.reshape(bsz * l_pad, HEAD_DIM)
    q_in, iq_in, iw_in = qa, iq, iw_t
    if t_pad != t_len:
        padq = lambda a: jnp.pad(a.reshape(bsz, t_len, -1),
                                 ((0, 0), (0, t_pad - t_len), (0, 0))).reshape(bsz * t_pad, -1)
        q_in, iq_in = padq(qa), padq(iq)
        iw_in = jnp.pad(iw_t.reshape(8, bsz, t_len), ((0, 0), (0, 0), (0, t_pad - t_len))).reshape(8, bsz * t_pad)
    oa = _dsa(q_in, iq_in, iw_in, k_hm, v_aug, ik_b, bsz=bsz, l_real=l_real, pos0=pos0, kb_rows=kb_rows,
              topk=topk, tq=tq)
    if t_pad != t_len:
        oa = oa.reshape(bsz, t_pad, A_WIDTH)[:, :t_len].reshape(n, A_WIDTH)
    ob, s_tiles = _hgrn(hb, w["hgrn_lb"], w["gnorm"], _state_to_tiles(s0), layer=layer, bsz=bsz, t_len=t_len,
                        tb=_row_tile(t_len, 256))
    x1 = _out_proj(oa, ob, w["w_out_a"], w["w_out_b"], x, _row_tile(n, 512))
    return x1, k_t, v_t, ik_t, _tiles_to_state(s_tiles)


def _from_feature_major(a, bsz, t_len, heads):
    if heads is None:
        return jnp.transpose(a.reshape(HEAD_DIM, bsz, t_len), (1, 2, 0))
    return jnp.transpose(a.reshape(heads, HEAD_DIM, bsz, t_len), (2, 3, 0, 1))


def kernel(x_prompt, x_sample, cache_k, cache_v, cache_idx_k, state_hgrn, cache_mem_k, cache_mem_v, mem_prompt,
           norm_mix, w_in, hgrn_lb, hgrn_gnorm, w_out, norm_mem, norm_memkv, w_mq, w_mk, w_mv, w_mo,
           norm_ffn, ffn_w1, ffn_w3, ffn_w2, moe_router, moe_w1, moe_w3, moe_w2, norm_final):
    depth = w_in.shape[0]
    bp, tp, _ = x_prompt.shape
    bs, ts, _ = x_sample.shape
    past = cache_k.shape[2]
    n_mem = mem_prompt.shape[1]
    tabs_p = _rope_tables(jnp.arange(tp, dtype=I32))
    tabs_s = tuple(jnp.tile(a, (bs, 1)) for a in _rope_tables(past + jnp.arange(ts, dtype=I32)))
    xp = x_prompt.reshape(bp * tp, D_MODEL)
    xs = x_sample.reshape(bs * ts, D_MODEL)
    mem = mem_prompt.reshape(bp * n_mem, D_MODEL)
    s0_p = jnp.zeros((bp, B_HEADS, HEAD_DIM, HEAD_DIM), F32)
    gfin = norm_final.reshape(1, D_MODEL)
    row2 = lambda a: a.reshape(1, -1)
    outs = {k: [] for k in ("kp", "vp", "ikp", "sp", "mkp", "mvp", "ks", "vs", "iks", "ss")}
    for l in range(depth):
        w = {"norm_mix": row2(norm_mix[l]), "w_in": _regroup_w_in(w_in[l]), "hgrn_lb": hgrn_lb.astype(F32),
             "gnorm": row2(jnp.tile(hgrn_gnorm[l], B_HEADS)),
             "w_out_a": w_out[l, :A_WIDTH].astype(BF16), "w_out_b": w_out[l, A_WIDTH:].astype(BF16)}
        xp, kp, vp, ikp, sp = _mixer(xp, tabs_p, w, bsz=bp, t_len=tp, pos0=0, past=None, s0=s0_p, layer=l)
        xs, ks, vs, iks, ss = _mixer(xs, tabs_s, w, bsz=bs, t_len=ts, pos0=past,
                                     past=(cache_k[l], cache_v[l], cache_idx_k[l]), s0=state_hgrn[l], layer=l)
        wkv = jnp.concatenate([w_mk[l], w_mv[l]], axis=1).astype(BF16)
        mkv = _norm_matmul(mem, row2(norm_memkv[l]), wkv, _row_tile(bp * n_mem, 256))
        mk_p = mkv[:, :D_MODEL].reshape(bp, n_mem, D_MODEL)
        mv_p = mkv[:, D_MODEL:].reshape(bp, n_mem, D_MODEL)
        wq, wo, gm = w_mq[l].astype(BF16), w_mo[l].astype(BF16), row2(norm_mem[l])
        xp = _mem_attn(xp, gm, wq, mk_p.astype(BF16), mv_p.astype(BF16), wo, tp, _row_tile(tp, 256))
        xs = _mem_attn(xs, gm, wq, cache_mem_k[l].reshape(bs, n_mem, D_MODEL).astype(BF16),
                       cache_mem_v[l].reshape(bs, n_mem, D_MODEL).astype(BF16), wo, ts, _row_tile(ts, 256))
        final = l == depth - 1
        gf = row2(norm_ffn[l])
        if l % 2 == 0:
            w1, w3, w2 = ffn_w1[l // 2].astype(BF16), ffn_w3[l // 2].astype(BF16), ffn_w2[l // 2].astype(BF16)
            tf = _ff_tile(w1.shape[1])
            xp = _ffn(xp, gf, w1, w3, w2, gfin, final=final, tm=_row_tile(bp * tp, 512), tf=tf)
            xs = _ffn(xs, gf, w1, w3, w2, gfin, final=final, tm=_row_tile(bs * ts, 512), tf=tf)
        else:
            w1, w3, w2 = moe_w1[l // 2].astype(BF16), moe_w3[l // 2].astype(BF16), moe_w2[l // 2].astype(BF16)
            rt = moe_router[l // 2].T
            tf = _ff_tile(w1.shape[2])
            comb_p, sel_p = _router(xp, gf, rt, _row_tile(bp * tp, 512))
            comb_s, sel_s = _router(xs, gf, rt, _row_tile(bs * ts, 512))
            xp = _moe(xp, gf, sel_p, sel_p.T, comb_p.T, w1, w3, w2, gfin, final=final,
                      nb=_row_tile(bp * tp, 1024), tf=tf)
            xs = _moe(xs, gf, sel_s, sel_s.T, comb_s.T, w1, w3, w2, gfin, final=final,
                      nb=_row_tile(bs * ts, 1024), tf=tf)
        outs["kp"].append(_from_feature_major(kp, bp, tp, A_KV_HEADS))
        outs["vp"].append(_from_feature_major(vp, bp, tp, A_KV_HEADS))
        outs["ikp"].append(_from_feature_major(ikp, bp, tp, None)); outs["sp"].append(sp)
        outs["mkp"].append(mk_p.reshape(bp, n_mem, MEM_HEADS, MEM_HEAD_DIM))
        outs["mvp"].append(mv_p.reshape(bp, n_mem, MEM_HEADS, MEM_HEAD_DIM))
        outs["ks"].append(_from_feature_major(ks, bs, ts, A_KV_HEADS))
        outs["vs"].append(_from_feature_major(vs, bs, ts, A_KV_HEADS))
        outs["iks"].append(_from_feature_major(iks, bs, ts, None)); outs["ss"].append(ss)
    y_prompt = xp.reshape(bp, tp, D_MODEL)
    y_sample = xs.reshape(bs, ts, D_MODEL)
    st = lambda k: jnp.stack(outs[k])
    return (y_prompt, y_sample, st("kp"), st("vp"), st("ikp"), st("sp"), st("mkp"), st("mvp"),
            st("ks"), st("vs"), st("iks"), st("ss"))
```

```python
import functools

import numpy as np
import jax
import jax.numpy as jnp
from jax import lax
from jax.experimental import pallas as pl
from jax.experimental.pallas import tpu as pltpu

F32 = jnp.float32
BF16 = jnp.bfloat16
I32 = jnp.int32
I16 = jnp.int16

D_MODEL = 1024
CHUNK = 64
EPS = 1e-6
ROPE_THETA = 500000.0
HEAD_DIM = 64
ROT_DIM = HEAD_DIM // 4
A_HEADS = 8
A_KV_HEADS = 2
A_GROUP = A_HEADS // A_KV_HEADS
A_WIDTH = A_HEADS * HEAD_DIM
KV_WIDTH = A_KV_HEADS * HEAD_DIM
IDX_HEADS = 4
IDX_WIDTH = IDX_HEADS * HEAD_DIM
TOPK_MAX = 256
B_HEADS = 8
B_WIDTH = B_HEADS * HEAD_DIM
B_CHUNK = 16
MEM_HEADS = 4
MEM_HEAD_DIM = D_MODEL // MEM_HEADS
N_EXPERTS = 8
LANES = 128
HALF = 256
IN_SPLITS = (A_WIDTH, KV_WIDTH, KV_WIDTH, IDX_WIDTH, HEAD_DIM, IDX_HEADS, B_WIDTH, B_WIDTH, B_WIDTH, B_WIDTH)
IN_WIDTH = sum(IN_SPLITS)
IN_WIDTH_PAD = 3200
VMEM_LIMIT = 56 * 1024 * 1024
INT_MIN = -2147483648
NEG_BIG = -1e30
LOG2E = 1.4426950408889634

_NT = (((1,), (1,)), ((), ()))
_TN = (((0,), (0,)), ((), ()))


def _cparams(*sem):
    return pltpu.CompilerParams(dimension_semantics=sem, vmem_limit_bytes=VMEM_LIMIT)


def _rms(x, g):
    return x * lax.rsqrt(jnp.mean(x * x, axis=-1, keepdims=True) + EPS) * g


def _sigmoid(x):
    return 1.0 / (1.0 + jnp.exp(-x))


def _dot(a, b):
    return jnp.dot(a, b, preferred_element_type=F32)


def _split2(x):
    hi = x.astype(BF16)
    lo = (x - hi.astype(F32)).astype(BF16)
    return hi, lo


def _split3(x):
    hi = x.astype(BF16)
    r = x - hi.astype(F32)
    mid = r.astype(BF16)
    lo = (r - mid.astype(F32)).astype(BF16)
    return hi, mid, lo


def _rope128(xb, c, s1, s2):
    return xb * c + pltpu.roll(xb, LANES - ROT_DIM // 2, 1) * s1 + pltpu.roll(xb, ROT_DIM // 2, 1) * s2


def _inproj_kernel(x_ref, g_ref, wraw_ref, c_ref, s1_ref, s2_ref,
                   qa_ref, hb_ref, iq_ref, kt_ref, vt_ref, ikt_ref, khm_ref, vaug_ref, ikb_ref, iwt_ref, w_ref):
    @pl.when(pl.program_id(0) == 0)
    def _():
        src = np.cumsum([0] + list(IN_SPLITS))
        dst = 0
        for i in (0, 6, 7, 8, 9, 1, 2, 3):
            width = int(src[i + 1] - src[i])
            w_ref[:, dst:dst + width] = wraw_ref[:, int(src[i]):int(src[i + 1])].astype(BF16)
            dst += width
        tail = jnp.concatenate([wraw_ref[:, int(src[4]):int(src[6])],
                                jnp.zeros((D_MODEL, LANES - HEAD_DIM - IDX_HEADS), F32)], axis=1)
        w_ref[:, dst:dst + LANES] = tail.astype(BF16)

    xn = _rms(x_ref[...], g_ref[...]).astype(BF16)
    c, s1, s2 = c_ref[...], s1_ref[...], s2_ref[...]
    for j in range(A_WIDTH // LANES):
        h = _dot(xn, w_ref[:, j * LANES:(j + 1) * LANES])
        qa_ref[:, j * LANES:(j + 1) * LANES] = _rope128(h, c, s1, s2)
    for j in range(4):
        lo = A_WIDTH + j * B_WIDTH
        hb_ref[:, j * B_WIDTH:(j + 1) * B_WIDTH] = _dot(xn, w_ref[:, lo:lo + B_WIDTH])
    base = A_WIDTH + 4 * B_WIDTH
    k = _rope128(_dot(xn, w_ref[:, base:base + 128]), c, s1, s2)
    v = _dot(xn, w_ref[:, base + 128:base + 256])
    iq_ref[:, 0:128] = _rope128(_dot(xn, w_ref[:, base + 256:base + 384]), c, s1, s2)
    iq_ref[:, 128:256] = _rope128(_dot(xn, w_ref[:, base + 384:base + 512]), c, s1, s2)
    lane = lax.broadcasted_iota(I32, c.shape, 1)
    first = lane < HEAD_DIM
    ikw = _rope128(_dot(xn, w_ref[:, base + 512:base + 640]),
                   jnp.where(first, c, 1.0), jnp.where(first, s1, 0.0), jnp.where(first, s2, 0.0))
    kt_ref[...] = k.T
    vt_ref[...] = v.T
    ikw_t = ikw.T
    ikt_ref[...] = ikw_t[:HEAD_DIM]
    iwt_ref[...] = ikw_t[HEAD_DIM:HEAD_DIM + 8]
    ikb_ref[...] = ikw[:, :HEAD_DIM].astype(BF16)
    ones_col = jnp.where(lane == HEAD_DIM, 1.0, 0.0)
    for j in range(A_KV_HEADS):
        khm_ref[j] = k[:, j * HEAD_DIM:(j + 1) * HEAD_DIM].astype(BF16)
        vj = v if j == 0 else pltpu.roll(v, LANES - j * HEAD_DIM, 1)
        vaug_ref[j] = jnp.where(first, vj, ones_col).astype(BF16)


def _in_proj(x, g, w, tabs, tm):
    n = x.shape[0]
    nt = tabs[0].shape[0] // tm
    row = lambda i: (i, 0)
    tab = lambda i: (i % nt, 0)
    whole = lambda i: (0, 0)
    tmin = lambda i: (0, i)
    hmaj = lambda i: (0, i, 0)
    return pl.pallas_call(
        _inproj_kernel,
        grid=(n // tm,),
        in_specs=[pl.BlockSpec((tm, D_MODEL), row), pl.BlockSpec((1, D_MODEL), whole),
                  pl.BlockSpec((D_MODEL, IN_WIDTH), whole, pipeline_mode=pl.Buffered(1)),
                  pl.BlockSpec((tm, LANES), tab), pl.BlockSpec((tm, LANES), tab), pl.BlockSpec((tm, LANES), tab)],
        out_specs=[pl.BlockSpec((tm, A_WIDTH), row), pl.BlockSpec((tm, 4 * B_WIDTH), row),
                   pl.BlockSpec((tm, IDX_WIDTH), row),
                   pl.BlockSpec((KV_WIDTH, tm), tmin), pl.BlockSpec((KV_WIDTH, tm), tmin),
                   pl.BlockSpec((HEAD_DIM, tm), tmin),
                   pl.BlockSpec((A_KV_HEADS, tm, HEAD_DIM), hmaj), pl.BlockSpec((A_KV_HEADS, tm, LANES), hmaj),
                   pl.BlockSpec((tm, HEAD_DIM), row), pl.BlockSpec((8, tm), tmin)],
        out_shape=[jax.ShapeDtypeStruct((n, A_WIDTH), F32), jax.ShapeDtypeStruct((n, 4 * B_WIDTH), F32),
                   jax.ShapeDtypeStruct((n, IDX_WIDTH), F32),
                   jax.ShapeDtypeStruct((KV_WIDTH, n), F32), jax.ShapeDtypeStruct((KV_WIDTH, n), F32),
                   jax.ShapeDtypeStruct((HEAD_DIM, n), F32),
                   jax.ShapeDtypeStruct((A_KV_HEADS, n, HEAD_DIM), BF16),
                   jax.ShapeDtypeStruct((A_KV_HEADS, n, LANES), BF16),
                   jax.ShapeDtypeStruct((n, HEAD_DIM), BF16), jax.ShapeDtypeStruct((8, n), F32)],
        scratch_shapes=[pltpu.VMEM((D_MODEL, IN_WIDTH_PAD), BF16)],
        compiler_params=_cparams("arbitrary"),
        name="in_proj",
    )(x, g, w, *tabs)


def _rope_tables(pos):
    half = ROT_DIM // 2
    inv_freq = 1.0 / (ROPE_THETA ** (jnp.arange(half, dtype=F32) * (2.0 / ROT_DIM)))
    ang = pos.astype(F32)[:, None] * inv_freq[None, :]
    cos, sin = jnp.cos(ang), jnp.sin(ang)
    t = pos.shape[0]
    pad = HEAD_DIM - ROT_DIM
    c = jnp.concatenate([cos, cos, jnp.ones((t, pad), F32)], axis=1)
    s1 = jnp.concatenate([-sin, jnp.zeros((t, half + pad), F32)], axis=1)
    s2 = jnp.concatenate([jnp.zeros((t, half), F32), sin, jnp.zeros((t, pad), F32)], axis=1)
    rep = LANES // HEAD_DIM
    return tuple(jnp.tile(a, (1, rep)) for a in (c, s1, s2))


def _dsa_kernel(q_ref, iq_ref, iw_ref, k_ref, v_ref, ik_ref, o_ref, key_ref, hi_ref, lo_ref, acc_ref, m_ref,
                s0_ref, s1_ref, alpha_ref, p_ref, bias_ref, mt0_ref, mt1_ref, *, l_real, pos0, kb_rows, topk, tq):
    i = pl.program_id(1)
    qpos0 = pos0 + i * tq
    kend = jnp.minimum(((qpos0 + tq - 1) // CHUNK + 1) * CHUNK, l_real)
    cb_rows = 2 * kb_rows
    nkc = (kend + cb_rows - 1) // cb_rows
    nkb = 2 * nkc
    qchunk = (qpos0 + lax.broadcasted_iota(I32, (1, tq), 1)) // CHUNK
    row_iota = lax.broadcasted_iota(I32, (kb_rows, tq), 0)

    iq = iq_ref[...].astype(BF16)
    iq4 = jnp.concatenate([iq[:, h * HEAD_DIM:(h + 1) * HEAD_DIM] for h in range(IDX_HEADS)], axis=0)
    iw = iw_ref[...] * (IDX_HEADS ** -0.5 * HEAD_DIM ** -0.5)

    def score_body(kb, carry):
        r0 = pl.multiple_of(kb * kb_rows, kb_rows)
        ikb = ik_ref[pl.ds(r0, kb_rows), :]
        lg4 = lax.dot_general(ikb, iq4, _NT, preferred_element_type=F32)
        acc = jnp.zeros((kb_rows, tq), F32)
        for h in range(IDX_HEADS):
            acc = acc + iw[h:h + 1, :] * jnp.maximum(lg4[:, h * tq:(h + 1) * tq], 0.0)
        bits = pltpu.bitcast(acc + 0.0, I32)
        key = bits ^ ((bits >> 31) & 0x7FFFFFFF)
        kpos = r0 + row_iota
        adm = (kpos // CHUNK <= qchunk) & (kpos < l_real)
        key = jnp.where(adm, key, INT_MIN)
        key_ref[pl.ds(r0, kb_rows), :] = key
        hi_ref[pl.ds(r0, kb_rows), :] = (key >> 16).astype(I16)
        return carry

    lax.fori_loop(0, nkb, score_body, 0)

    def count(pred):
        def body(kc, c):
            r0 = pl.multiple_of(kc * cb_rows, cb_rows)
            hit = pred(key_ref[pl.ds(r0, cb_rows), :], r0)
            return c + jnp.where(hit, 1, 0).reshape(cb_rows // 8, 8, tq).sum(axis=0)
        c = lax.fori_loop(0, nkc, body, jnp.zeros((8, tq), I32))
        return c.sum(axis=0, keepdims=True)

    def count16(ref, pred):
        one, zero = jnp.ones((), BF16), jnp.zeros((), BF16)

        def body(kc, c):
            r0 = pl.multiple_of(kc * cb_rows, cb_rows)
            w = jnp.where(pred(ref[pl.ds(r0, cb_rows), :]), one, zero).reshape(cb_rows // 16, 16, tq)
            parts = [w[r] for r in range(cb_rows // 16)]
            while len(parts) > 1:
                parts = [a + b for a, b in zip(parts[::2], parts[1::2])] + parts[len(parts) & ~1:]
            return c + parts[0]
        c = lax.fori_loop(0, nkc, body, jnp.zeros((16, tq), BF16))
        return c.astype(F32).sum(axis=0, keepdims=True).astype(I32)

    def bisect16(ref, kth):
        c0 = count16(ref, lambda blk: blk >= jnp.zeros((), I16))
        start = jnp.where(c0 >= kth, 0, -32768).astype(I32)

        def bit_body(it, prefix):
            cand = prefix | lax.shift_left(jnp.int32(1), 14 - it)
            c = count16(ref, lambda blk: blk >= cand.astype(I16))
            return jnp.where(c >= kth, cand, prefix)

        return lax.fori_loop(0, 15, bit_body, start)

    tau_hi = bisect16(hi_ref, topk)
    tau_hi16 = tau_hi.astype(I16)
    kth_lo = topk - count16(hi_ref, lambda blk: blk > tau_hi16)

    def lo_body(kc, carry):
        r0 = pl.multiple_of(kc * cb_rows, cb_rows)
        lo = ((key_ref[pl.ds(r0, cb_rows), :] & 0xFFFF) - 32768).astype(I16)
        lo_ref[pl.ds(r0, cb_rows), :] = jnp.where(hi_ref[pl.ds(r0, cb_rows), :] == tau_hi16, lo,
                                                 jnp.full((), -32768, I16))
        return carry

    lax.fori_loop(0, nkc, lo_body, 0)
    tau_lo = bisect16(lo_ref, kth_lo)
    tau = lax.shift_left(tau_hi, 16) | (tau_lo + 32768)
    n_ge = count(lambda blk, r0: blk >= tau)

    @pl.when(jnp.max(jnp.where((n_ge > topk) & (tau != INT_MIN), 1, 0)) > 0)
    def _():
        need = (topk - count(lambda blk, r0: blk > tau)).astype(F32)
        ri = lax.broadcasted_iota(I32, (kb_rows, kb_rows), 0)
        ci = lax.broadcasted_iota(I32, (kb_rows, kb_rows), 1)
        upto = jnp.where(ci <= ri, 1.0, 0.0).astype(BF16)

        def strike(kc, seen):
            for half in range(2):
                rows = pl.ds(pl.multiple_of(kc * cb_rows + half * kb_rows, kb_rows), kb_rows)
                key = key_ref[rows, :]
                tie = key == tau
                rank = _dot(upto, jnp.where(tie, 1.0, 0.0).astype(BF16)) + seen
                key_ref[rows, :] = jnp.where(tie & (rank > need), INT_MIN, key)
                seen = rank[kb_rows - 1:kb_rows, :]
            return seen

        lax.fori_loop(0, nkc, strike, jnp.zeros((1, tq), F32))

    lane_rep = kb_rows // LANES
    tau_c = jnp.tile(jnp.broadcast_to(tau, (LANES, tq)).T, (1, lane_rep))
    acc_ref[...] = jnp.zeros(acc_ref.shape, F32)
    m_ref[...] = jnp.full(m_ref.shape, NEG_BIG, F32)
    qs = (q_ref[...] * (HEAD_DIM ** -0.5 * LOG2E)).astype(BF16)
    q4 = [jnp.concatenate([qs[:, (j * A_GROUP + g) * HEAD_DIM:(j * A_GROUP + g + 1) * HEAD_DIM]
                           for g in range(A_GROUP)], axis=0) for j in range(A_KV_HEADS)]

    def qk_scores(kb, s_slot, mt_slot):
        r0 = pl.multiple_of(jnp.minimum(kb, nkb - 1) * kb_rows, kb_rows)
        key_t = key_ref[pl.ds(r0, kb_rows), :].T
        bias_ref[...] = jnp.where((key_t >= tau_c) & (key_t != INT_MIN), 0.0, NEG_BIG)
        for j in range(A_KV_HEADS):
            s = lax.dot_general(q4[j], k_ref[j, pl.ds(r0, kb_rows), :], _NT, preferred_element_type=F32)
            for g in range(A_GROUP):
                rows = slice(g * tq, (g + 1) * tq)
                sm = s[rows, :] + bias_ref[...]
                s_slot[j, rows, :] = sm
                mt_slot[j, g] = jnp.broadcast_to(jnp.max(sm, axis=1, keepdims=True), (tq, LANES))

    def softmax_pv(kb, s_slot, mt_slot):
        r0 = pl.multiple_of(kb * kb_rows, kb_rows)
        for j in range(A_KV_HEADS):
            for g in range(A_GROUP):
                rows = slice(g * tq, (g + 1) * tq)
                m_old = m_ref[j, g]
                m_new = jnp.maximum(m_old, mt_slot[j, g])
                alpha_ref[j, g] = jnp.exp2(m_old - m_new)
                m_ref[j, g] = m_new
                p_ref[j, rows, :] = jnp.exp2(s_slot[j, rows, :] - jnp.tile(m_new, (1, lane_rep))).astype(BF16)
            pv = _dot(p_ref[j], v_ref[j, pl.ds(r0, kb_rows), :])
            acc_ref[j] = alpha_ref[j] * acc_ref[j] + pv.reshape(A_GROUP, tq, LANES)

    qk_scores(0, s0_ref, mt0_ref)

    def pair_body(i, carry):
        kb = 2 * i
        qk_scores(kb + 1, s1_ref, mt1_ref)
        softmax_pv(kb, s0_ref, mt0_ref)
        qk_scores(kb + 2, s0_ref, mt0_ref)
        softmax_pv(kb + 1, s1_ref, mt1_ref)
        return carry

    lax.fori_loop(0, nkc, pair_body, 0)

    outs = []
    for j in range(A_KV_HEADS):
        acc = acc_ref[j]
        for g in range(A_GROUP):
            outs.append(acc[g, :, :HEAD_DIM] * (1.0 / acc[g, :, HEAD_DIM:HEAD_DIM + 1]))
    o_ref[...] = jnp.concatenate(outs, axis=1)


def _dsa(q, iq, iw_t, k_hm, v_aug, ik, *, bsz, l_real, pos0, kb_rows, topk, tq):
    t_len = q.shape[0] // bsz
    l_pad = ik.shape[0] // bsz
    nq = t_len // tq
    assert l_pad // 16 <= 256, "packed hit counts are summed in bf16, exact only up to 256 per slot"
    kern = functools.partial(_dsa_kernel, l_real=l_real, pos0=pos0, kb_rows=kb_rows, topk=topk, tq=tq)
    qrow = lambda b, i: (b * nq + i, 0)
    return pl.pallas_call(
        kern,
        grid=(bsz, nq),
        in_specs=[pl.BlockSpec((tq, A_WIDTH), qrow), pl.BlockSpec((tq, IDX_WIDTH), qrow),
                  pl.BlockSpec((8, tq), lambda b, i: (0, b * nq + i)),
                  pl.BlockSpec((A_KV_HEADS, l_pad, HEAD_DIM), lambda b, i: (0, b, 0)),
                  pl.BlockSpec((A_KV_HEADS, l_pad, LANES), lambda b, i: (0, b, 0)),
                  pl.BlockSpec((l_pad, HEAD_DIM), lambda b, i: (b, 0))],
        out_specs=pl.BlockSpec((tq, A_WIDTH), qrow),
        out_shape=jax.ShapeDtypeStruct((bsz * t_len, A_WIDTH), F32),
        scratch_shapes=[pltpu.VMEM((l_pad, tq), I32), pltpu.VMEM((l_pad, tq), I16), pltpu.VMEM((l_pad, tq), I16),
                        pltpu.VMEM((A_KV_HEADS, A_GROUP, tq, LANES), F32),
                        pltpu.VMEM((A_KV_HEADS, A_GROUP, tq, LANES), F32),
                        pltpu.VMEM((A_KV_HEADS, A_GROUP * tq, kb_rows), F32),
                        pltpu.VMEM((A_KV_HEADS, A_GROUP * tq, kb_rows), F32),
                        pltpu.VMEM((A_KV_HEADS, A_GROUP, tq, LANES), F32),
                        pltpu.VMEM((A_KV_HEADS, A_GROUP * tq, kb_rows), BF16),
                        pltpu.VMEM((tq, kb_rows), F32),
                        pltpu.VMEM((A_KV_HEADS, A_GROUP, tq, LANES), F32),
                        pltpu.VMEM((A_KV_HEADS, A_GROUP, tq, LANES), F32)],
        compiler_params=_cparams("parallel", "arbitrary"),
        name="dsa",
    )(q, iq, iw_t, k_hm, v_aug, ik)


def _hgrn_kernel(bq_ref, bf_ref, bi_ref, bg_ref, lb_ref, gn_ref, s0_ref, o_ref, sfin_ref,
                 st_ref, q_s, kk_s, b_s, qe_s, kd_s, dec_s, o_s, e_s, *, layer, tb):
    tstep = pl.program_id(1)
    nchunk = tb // B_CHUNK

    @pl.when(tstep == 0)
    def _():
        st_ref[...] = s0_ref[0]

    lbr = lb_ref[...]
    lbe = jnp.exp(lbr - jnp.max(lbr, axis=0, keepdims=True))
    lbs = lbe / jnp.sum(lbe, axis=0, keepdims=True)
    cum = lbs[0:1, :]
    first = cum
    for r in range(1, layer + 1):
        cum = cum + lbs[r:r + 1, :]
    lb = cum - first

    f = lb + (1.0 - lb) * _sigmoid(bf_ref[...])
    bq = bq_ref[...]
    q = bq * _sigmoid(bq) * (HEAD_DIM ** -0.5)
    g3 = _split3(jnp.log(f))
    ri = lax.broadcasted_iota(I32, (tb, tb), 0)
    ci = lax.broadcasted_iota(I32, (tb, tb), 1)
    same = (ri // B_CHUNK) == (ci // B_CHUNK)
    tri = jnp.where(same & (ci <= ri), 1.0, 0.0).astype(BF16)
    ones = jnp.where(same, 1.0, 0.0).astype(BF16)
    b = _dot(tri, g3[0]) + _dot(tri, g3[1]) + _dot(tri, g3[2])
    blast = _dot(ones, g3[0]) + _dot(ones, g3[1]) + _dot(ones, g3[2])
    q_s[...] = q
    kk_s[...] = 1.0 - f
    b_s[...] = b
    qe_s[...] = q * jnp.exp(b)
    kd_s[...] = (1.0 - f) * jnp.exp(blast - b)
    dec_s[...] = jnp.exp(blast)

    hi = lax.broadcasted_iota(I32, (HALF, HALF), 0) // HEAD_DIM
    hj = lax.broadcasted_iota(I32, (HALF, HALF), 1) // HEAD_DIM
    bd_mask = hi == hj
    bd = jnp.where(bd_mask, 1.0, 0.0).astype(BF16)
    t_iota = lax.broadcasted_iota(I32, (B_CHUNK, B_WIDTH), 0)

    def seg_sum(x, two_pass):
        out = []
        for half in range(2):
            xs = x[:, half * HALF:(half + 1) * HALF]
            if two_pass:
                xh, xl = _split2(xs)
                out.append(_dot(xh, bd) + _dot(xl, bd))
            else:
                out.append(_dot(xs.astype(BF16), bd))
        return jnp.concatenate(out, axis=1)

    def chunk_body(c, carry):
        r0 = pl.multiple_of(c * B_CHUNK, B_CHUNK)
        rows = pl.ds(r0, B_CHUNK)
        qc, kc, bc, vc = q_s[rows, :], kk_s[rows, :], b_s[rows, :], bi_ref[rows, :]
        qe, kd = qe_s[rows, :].astype(BF16), kd_s[rows, :].astype(BF16)
        dec = dec_s[pl.ds(r0, 1), :]
        for s in range(B_CHUNK):
            ex = jnp.exp(jnp.where(t_iota >= s, bc - bc[s:s + 1, :], NEG_BIG))
            e_s[s * B_CHUNK:(s + 1) * B_CHUNK, :] = qc * kc[s:s + 1, :] * ex
        aexp = seg_sum(e_s[...], False)
        o = jnp.zeros((B_CHUNK, B_WIDTH), F32)
        for s in range(B_CHUNK):
            o = o + aexp[s * B_CHUNK:(s + 1) * B_CHUNK, :] * vc[s:s + 1, :]
        inter = []
        vb = vc.astype(BF16)
        for half in range(2):
            cols = slice(half * HALF, (half + 1) * HALF)
            st = st_ref[half]
            inter.append(lax.dot_general(qe[:, cols], st.astype(BF16), _NT, preferred_element_type=F32))
            upd = lax.dot_general(vb[:, cols], kd[:, cols], _TN, preferred_element_type=F32)
            st_ref[half] = st * dec[:, cols] + jnp.where(bd_mask, upd, 0.0)
        o_s[rows, :] = o + jnp.concatenate(inter, axis=1)
        return carry

    lax.fori_loop(0, nchunk, chunk_body, 0)

    o = o_s[...]
    ms = seg_sum(o * o, True) * (1.0 / HEAD_DIM)
    bg = bg_ref[...]
    o_ref[...] = o * lax.rsqrt(ms + EPS) * gn_ref[...] * (bg * _sigmoid(bg))

    @pl.when(tstep == pl.num_programs(1) - 1)
    def _():
        sfin_ref[0] = st_ref[...]


def _hgrn(hb, lb_raw, gn_tiled, s0_bd, *, layer, bsz, t_len, tb):
    nt = t_len // tb
    col = lambda j: (lambda b, t: (b * nt + t, j))
    scr = lambda r: pltpu.VMEM((r, B_WIDTH), F32)
    kern = functools.partial(_hgrn_kernel, layer=layer, tb=tb)
    return pl.pallas_call(
        kern,
        grid=(bsz, nt),
        in_specs=[pl.BlockSpec((tb, B_WIDTH), col(0)), pl.BlockSpec((tb, B_WIDTH), col(1)),
                  pl.BlockSpec((tb, B_WIDTH), col(2)), pl.BlockSpec((tb, B_WIDTH), col(3)),
                  pl.BlockSpec(lb_raw.shape, lambda b, t: (0, 0)),
                  pl.BlockSpec((1, B_WIDTH), lambda b, t: (0, 0)),
                  pl.BlockSpec((1, 2, HALF, HALF), lambda b, t: (b, 0, 0, 0))],
        out_specs=[pl.BlockSpec((tb, B_WIDTH), lambda b, t: (b * nt + t, 0)),
                   pl.BlockSpec((1, 2, HALF, HALF), lambda b, t: (b, 0, 0, 0))],
        out_shape=[jax.ShapeDtypeStruct((bsz * t_len, B_WIDTH), F32),
                   jax.ShapeDtypeStruct((bsz, 2, HALF, HALF), F32)],
        scratch_shapes=[pltpu.VMEM((2, HALF, HALF), F32), scr(tb), scr(tb), scr(tb), scr(tb), scr(tb), scr(tb),
                        scr(tb), scr(B_CHUNK * B_CHUNK)],
        compiler_params=_cparams("parallel", "arbitrary"),
        name="hgrn2",
    )(hb, hb, hb, hb, lb_raw, gn_tiled, s0_bd)


def _state_to_tiles(s):
    bsz = s.shape[0]
    st = jnp.swapaxes(s, -1, -2).reshape(bsz, 2, B_HEADS // 2, HEAD_DIM, HEAD_DIM)
    eye = jnp.eye(B_HEADS // 2, dtype=s.dtype)
    t = st[:, :, :, :, None, :] * eye[None, None, :, None, :, None]
    return t.reshape(bsz, 2, HALF, HALF)


def _tiles_to_state(t):
    bsz = t.shape[0]
    t6 = t.reshape(bsz, 2, B_HEADS // 2, HEAD_DIM, B_HEADS // 2, HEAD_DIM)
    diag = jnp.stack([t6[:, :, h, :, h, :] for h in range(B_HEADS // 2)], axis=2)
    return jnp.swapaxes(diag.reshape(bsz, B_HEADS, HEAD_DIM, HEAD_DIM), -1, -2)


def _outproj_kernel(a_ref, b_ref, wa_ref, wb_ref, x_ref, o_ref):
    o_ref[...] = (x_ref[...] + _dot(a_ref[...].astype(BF16), wa_ref[...])
                  + _dot(b_ref[...].astype(BF16), wb_ref[...]))


def _out_proj(a, b, wa, wb, x, tm):
    n = x.shape[0]
    row = lambda i: (i, 0)
    whole = lambda i: (0, 0)
    return pl.pallas_call(
        _outproj_kernel,
        grid=(n // tm,),
        in_specs=[pl.BlockSpec((tm, A_WIDTH), row), pl.BlockSpec((tm, B_WIDTH), row),
                  pl.BlockSpec((A_WIDTH, D_MODEL), whole), pl.BlockSpec((B_WIDTH, D_MODEL), whole),
                  pl.BlockSpec((tm, D_MODEL), row)],
        out_specs=pl.BlockSpec((tm, D_MODEL), row),
        out_shape=jax.ShapeDtypeStruct((n, D_MODEL), F32),
        compiler_params=_cparams("parallel"),
        name="out_proj",
    )(a, b, wa, wb, x)


def _normmm_kernel(x_ref, g_ref, w_ref, o_ref):
    o_ref[...] = _dot(_rms(x_ref[...], g_ref[...]).astype(BF16), w_ref[...])


def _norm_matmul(x, g, w, tm):
    n, m = x.shape[0], w.shape[1]
    return pl.pallas_call(
        _normmm_kernel,
        grid=(n // tm,),
        in_specs=[pl.BlockSpec((tm, D_MODEL), lambda i: (i, 0)), pl.BlockSpec((1, D_MODEL), lambda i: (0, 0)),
                  pl.BlockSpec((D_MODEL, m), lambda i: (0, 0))],
        out_specs=pl.BlockSpec((tm, m), lambda i: (i, 0)),
        out_shape=jax.ShapeDtypeStruct((n, m), F32),
        compiler_params=_cparams("parallel"),
        name="norm_matmul",
    )(x, g, w)


def _memattn_kernel(x_ref, g_ref, wq_ref, mk_ref, mv_ref, wo_ref, o_ref):
    x = x_ref[...]
    q = _dot(_rms(x, g_ref[...]).astype(BF16), wq_ref[...])
    heads = []
    for h in range(MEM_HEADS):
        cols = slice(h * MEM_HEAD_DIM, (h + 1) * MEM_HEAD_DIM)
        s = lax.dot_general(q[:, cols].astype(BF16), mk_ref[0, :, cols], _NT,
                            preferred_element_type=F32) * (MEM_HEAD_DIM ** -0.5)
        p = jnp.exp(s - jnp.max(s, axis=-1, keepdims=True))
        p = p * (1.0 / jnp.sum(p, axis=-1, keepdims=True))
        heads.append(_dot(p.astype(BF16), mv_ref[0, :, cols]).astype(BF16))
    o_ref[...] = x + _dot(jnp.concatenate(heads, axis=1), wo_ref[...])


def _mem_attn(x, g, wq, mk, mv, wo, rows_per_batch, tm):
    n = x.shape[0]
    n_mem = mk.shape[1]
    per = rows_per_batch // tm
    row = lambda i: (i, 0)
    whole = lambda i: (0, 0)
    mem = lambda i: (i // per, 0, 0)
    return pl.pallas_call(
        _memattn_kernel,
        grid=(n // tm,),
        in_specs=[pl.BlockSpec((tm, D_MODEL), row), pl.BlockSpec((1, D_MODEL), whole),
                  pl.BlockSpec((D_MODEL, D_MODEL), whole),
                  pl.BlockSpec((1, n_mem, D_MODEL), mem), pl.BlockSpec((1, n_mem, D_MODEL), mem),
                  pl.BlockSpec((D_MODEL, D_MODEL), whole)],
        out_specs=pl.BlockSpec((tm, D_MODEL), row),
        out_shape=jax.ShapeDtypeStruct((n, D_MODEL), F32),
        compiler_params=_cparams("parallel"),
        name="mem_attn",
    )(x, g, wq, mk, mv, wo)


def _swiglu_mid(xn, w1, w3):
    a = _dot(xn, w1)
    return (a * _sigmoid(a) * _dot(xn, w3)).astype(BF16)


def _ffn_kernel(x_ref, g_ref, w1_ref, w3_ref, w2_ref, gf_ref, o_ref, xn_ref, *, final):
    j = pl.program_id(1)

    @pl.when(j == 0)
    def _():
        x = x_ref[...]
        xn_ref[...] = _rms(x, g_ref[...]).astype(BF16)
        o_ref[...] = x

    o_ref[...] += _dot(_swiglu_mid(xn_ref[...], w1_ref[...], w3_ref[...]), w2_ref[...])

    if final:
        @pl.when(j == pl.num_programs(1) - 1)
        def _():
            o_ref[...] = _rms(o_ref[...], gf_ref[...])


def _ffn(x, g, w1, w3, w2, gf, *, final, tm, tf):
    n = x.shape[0]
    d_ff = w1.shape[1]
    row = lambda i, j: (i, 0)
    whole = lambda i, j: (0, 0)
    return pl.pallas_call(
        functools.partial(_ffn_kernel, final=final),
        grid=(n // tm, d_ff // tf),
        in_specs=[pl.BlockSpec((tm, D_MODEL), row), pl.BlockSpec((1, D_MODEL), whole),
                  pl.BlockSpec((D_MODEL, tf), lambda i, j: (0, j)), pl.BlockSpec((D_MODEL, tf), lambda i, j: (0, j)),
                  pl.BlockSpec((tf, D_MODEL), lambda i, j: (j, 0)), pl.BlockSpec((1, D_MODEL), whole)],
        out_specs=pl.BlockSpec((tm, D_MODEL), row),
        out_shape=jax.ShapeDtypeStruct((n, D_MODEL), F32),
        scratch_shapes=[pltpu.VMEM((tm, D_MODEL), BF16)],
        compiler_params=_cparams("parallel", "arbitrary"),
        name="ffn",
    )(x, g, w1, w3, w2, gf)


def _router_kernel(x_ref, g_ref, rt_ref, comb_ref, sel_ref):
    xn = _rms(x_ref[...], g_ref[...])
    logits = lax.dot_general(rt_ref[...], xn, _NT, preferred_element_type=F32,
                             precision=lax.Precision.HIGHEST)
    e_idx = lax.broadcasted_iota(I32, logits.shape, 0)
    m1 = jnp.max(logits, axis=0, keepdims=True)
    i1 = jnp.min(jnp.where(logits == m1, e_idx, N_EXPERTS), axis=0, keepdims=True)
    rest = jnp.where(e_idx == i1, -jnp.inf, logits)
    m2 = jnp.max(rest, axis=0, keepdims=True)
    i2 = jnp.min(jnp.where(rest == m2, e_idx, N_EXPERTS), axis=0, keepdims=True)
    e2 = jnp.exp(m2 - m1)
    g1 = 1.0 / (1.0 + e2)
    comb_ref[...] = jnp.where(e_idx == i1, g1, 0.0) + jnp.where(e_idx == i2, e2 * g1, 0.0)
    sel_ref[...] = jnp.where((e_idx == i1) | (e_idx == i2), 1.0, 0.0)


def _router(x, g, rt, tm):
    n = x.shape[0]
    return pl.pallas_call(
        _router_kernel,
        grid=(n // tm,),
        in_specs=[pl.BlockSpec((tm, D_MODEL), lambda i: (i, 0)), pl.BlockSpec((1, D_MODEL), lambda i: (0, 0)),
                  pl.BlockSpec((N_EXPERTS, D_MODEL), lambda i: (0, 0))],
        out_specs=[pl.BlockSpec((N_EXPERTS, tm), lambda i: (0, i)), pl.BlockSpec((N_EXPERTS, tm), lambda i: (0, i))],
        out_shape=[jax.ShapeDtypeStruct((N_EXPERTS, n), F32), jax.ShapeDtypeStruct((N_EXPERTS, n), F32)],
        compiler_params=_cparams("parallel"),
        name="moe_router",
    )(x, g, rt)


def _moe_kernel(x_ref, g_ref, selt_ref, sel_ref, comb_ref, w1_ref, w3_ref, w2_ref, gf_ref, o_ref,
                xn_ref, posr_ref, posc_ref, cnt_ref, xg_ref, yg_ref, *, final, nb, sub):
    e = pl.program_id(1)
    j = pl.program_id(2)
    last_j = pl.num_programs(2) - 1

    @pl.when((e == 0) & (j == 0))
    def _():
        x = x_ref[...]
        xn_ref[...] = _rms(x, g_ref[...]).astype(BF16)
        o_ref[...] = x
        ri = lax.broadcasted_iota(I32, (sub, sub), 0)
        ci = lax.broadcasted_iota(I32, (sub, sub), 1)
        before_r = jnp.where(ri < ci, 1.0, 0.0).astype(BF16)
        before_c = jnp.where(ci < ri, 1.0, 0.0).astype(BF16)
        run_r = jnp.zeros((N_EXPERTS, 1), F32)
        run_c = jnp.zeros((1, N_EXPERTS), F32)
        for c in range(nb // sub):
            cols = slice(c * sub, (c + 1) * sub)
            mr = selt_ref[:, cols]
            posr_ref[:, cols] = jnp.where(mr > 0.0, _dot(mr.astype(BF16), before_r) + run_r, -1.0)
            run_r = run_r + jnp.sum(mr, axis=1, keepdims=True)
            mc = sel_ref[cols, :]
            posc_ref[cols, :] = jnp.where(mc > 0.0, _dot(before_c, mc.astype(BF16)) + run_c, -1.0)
            run_c = run_c + jnp.sum(mc, axis=0, keepdims=True)
        for k in range(N_EXPERTS):
            cnt_ref[k] = run_c[0, k].astype(I32)

    cnt = cnt_ref[e]
    nfull = cnt // sub
    rem = cnt - nfull * sub
    tail_sizes = [m for m in (64, 128) if m < sub] + [sub]

    def for_tiles(fn):
        def body(s, carry):
            fn(pl.multiple_of(s * sub, sub), sub)
            return carry
        lax.fori_loop(0, nfull, body, 0)
        r0 = pl.multiple_of(nfull * sub, sub)
        lo = 0
        for m in tail_sizes:
            @pl.when((rem > lo) & (rem <= m))
            def _():
                fn(r0, m)
            lo = m

    def gather_matrix(r0, m):
        slot_r = posr_ref[pl.ds(e, 1), :]
        want = (r0 + lax.broadcasted_iota(I32, (m, nb), 0)).astype(F32)
        return jnp.where(slot_r == want, 1.0, 0.0).astype(BF16)

    def gather(r0, m):
        xg_ref[pl.ds(r0, m), :] = _dot(gather_matrix(r0, m), xn_ref[...]).astype(BF16)

    def ffn(r0, m):
        rows = pl.ds(r0, m)
        part = _dot(_swiglu_mid(xg_ref[rows, :], w1_ref[0], w3_ref[0]), w2_ref[0])

        @pl.when(j == 0)
        def _():
            yg_ref[rows, :] = part

        @pl.when(j > 0)
        def _():
            yg_ref[rows, :] += part

    @pl.when(j == 0)
    def _():
        for_tiles(gather)

    for_tiles(ffn)

    @pl.when(j == last_j)
    def _():
        c3 = _split3(comb_ref[...])
        lane8 = lax.broadcasted_iota(I32, (nb, N_EXPERTS), 1)
        slot_c = jnp.sum(jnp.where(lane8 == e, posc_ref[...], 0.0), axis=1, keepdims=True)

        def scatter(r0, m):
            pm = gather_matrix(r0, m)
            gates = _dot(pm, c3[0]) + _dot(pm, c3[1]) + _dot(pm, c3[2])
            lane_e = lax.broadcasted_iota(I32, (m, N_EXPERTS), 1) == e
            gate = jnp.sum(jnp.where(lane_e, gates, 0.0), axis=1, keepdims=True)
            y = (yg_ref[pl.ds(r0, m), :] * gate).astype(BF16)
            want = (r0 + lax.broadcasted_iota(I32, (nb, m), 1)).astype(F32)
            o_ref[...] += _dot(jnp.where(slot_c == want, 1.0, 0.0).astype(BF16), y)

        for_tiles(scatter)

        if final:
            @pl.when(e == pl.num_programs(1) - 1)
            def _():
                o_ref[...] = _rms(o_ref[...], gf_ref[...])


def _moe(x, g, sel_t, sel, comb, w1, w3, w2, gf, *, final, nb, tf):
    n = x.shape[0]
    d_ff = w1.shape[2]
    sub = min(nb, 256)
    row = lambda i, e, j: (i, 0)
    whole = lambda i, e, j: (0, 0)
    return pl.pallas_call(
        functools.partial(_moe_kernel, final=final, nb=nb, sub=sub),
        grid=(n // nb, N_EXPERTS, d_ff // tf),
        in_specs=[pl.BlockSpec((nb, D_MODEL), row), pl.BlockSpec((1, D_MODEL), whole),
                  pl.BlockSpec((N_EXPERTS, nb), lambda i, e, j: (0, i)),
                  pl.BlockSpec((nb, N_EXPERTS), row), pl.BlockSpec((nb, N_EXPERTS), row),
                  pl.BlockSpec((1, D_MODEL, tf), lambda i, e, j: (e, 0, j)),
                  pl.BlockSpec((1, D_MODEL, tf), lambda i, e, j: (e, 0, j)),
                  pl.BlockSpec((1, tf, D_MODEL), lambda i, e, j: (e, j, 0)), pl.BlockSpec((1, D_MODEL), whole)],
        out_specs=pl.BlockSpec((nb, D_MODEL), row),
        out_shape=jax.ShapeDtypeStruct((n, D_MODEL), F32),
        scratch_shapes=[pltpu.VMEM((nb, D_MODEL), BF16), pltpu.VMEM((N_EXPERTS, nb), F32),
                        pltpu.VMEM((nb, N_EXPERTS), F32), pltpu.SMEM((N_EXPERTS,), I32),
                        pltpu.VMEM((nb, D_MODEL), BF16), pltpu.VMEM((nb, D_MODEL), F32)],
        compiler_params=_cparams("parallel", "arbitrary", "arbitrary"),
        name="moe",
    )(x, g, sel_t, sel, comb, w1, w3, w2, gf)


def _rmsnorm_kernel(x_ref, g_ref, o_ref):
    o_ref[...] = _rms(x_ref[...], g_ref[...])


def _rmsnorm(x, g, tm):
    n = x.shape[0]
    return pl.pallas_call(
        _rmsnorm_kernel,
        grid=(n // tm,),
        in_specs=[pl.BlockSpec((tm, D_MODEL), lambda i: (i, 0)), pl.BlockSpec((1, D_MODEL), lambda i: (0, 0))],
        out_specs=pl.BlockSpec((tm, D_MODEL), lambda i: (i, 0)),
        out_shape=jax.ShapeDtypeStruct((n, D_MODEL), F32),
        compiler_params=_cparams("parallel"),
        name="final_norm",
    )(x, g)


def _row_tile(n, cap):
    t = min(n, cap)
    assert n % t == 0, (n, cap)
    return t


def _ff_tile(d_ff):
    for parts in (2, 4, 11, 22):
        if d_ff % parts == 0 and (d_ff // parts) % LANES == 0:
            return d_ff // parts
    return d_ff


def _mixer(x, tabs, w, *, bsz, t_len, pos0, past, s0, layer):
    n = bsz * t_len
    qa, hb, iq, k_t, v_t, ik_t, k_hm, v_aug, ik_b, iw_t = _in_proj(x, w["norm_mix"], w["w_in"], tabs,
                                                                   _row_tile(n, 256))
    l_real = t_len if past is None else past[0].shape[1] + t_len
    topk = min(TOPK_MAX, l_real // 4)
    tq = 2 * LANES if t_len % (2 * LANES) == 0 else LANES
    kb_rows = 256 if l_real % 256 == 0 else LANES
    l_pad = -(-l_real // (2 * kb_rows)) * (2 * kb_rows)
    t_pad = -(-t_len // tq) * tq
    if past is not None:
        pk, pv, pik = past
        one = jnp.ones(pv.shape[:2] + (A_KV_HEADS, 1), BF16)
        zero = jnp.zeros(pv.shape[:2] + (A_KV_HEADS, LANES - HEAD_DIM - 1), BF16)
        pv_aug = jnp.concatenate([pv.astype(BF16), one, zero], axis=-1)
        hm = lambda a: jnp.transpose(a, (2, 0, 1, 3))
        k_hm = jnp.concatenate([hm(pk.astype(BF16)), k_hm.reshape(A_KV_HEADS, bsz, t_len, HEAD_DIM)], axis=2)
        v_aug = jnp.concatenate([hm(pv_aug), v_aug.reshape(A_KV_HEADS, bsz, t_len, LANES)], axis=2)
        ik_b = jnp.concatenate([pik.astype(BF16), ik_b.reshape(bsz, t_len, HEAD_DIM)], axis=1)
    if l_pad != l_real or past is not None:
        padl = l_pad - l_real
        k_hm = jnp.pad(k_hm.reshape(A_KV_HEADS, bsz, l_real, HEAD_DIM), ((0, 0), (0, 0), (0, padl), (0, 0)))
        v_aug = jnp.pad(v_aug.reshape(A_KV_HEADS, bsz, l_real, LANES), ((0, 0), (0, 0), (0, padl), (0, 0)))
        ik_b = jnp.pad(ik_b.reshape(bsz, l_real, HEAD_DIM), ((0, 0), (0, padl), (0, 0)))
        k_hm = k_hm.reshape(A_KV_HEADS, bsz * l_pad, HEAD_DIM)
        v_aug = v_aug.reshape(A_KV_HEADS, bsz * l_pad, LANES)
        ik_b = ik_b.reshape(bsz * l_pad, HEAD_DIM)
    q_in, iq_in, iw_in = qa, iq, iw_t
    if t_pad != t_len:
        padq = lambda a: jnp.pad(a.reshape(bsz, t_len, -1),
                                 ((0, 0), (0, t_pad - t_len), (0, 0))).reshape(bsz * t_pad, -1)
        q_in, iq_in = padq(qa), padq(iq)
        iw_in = jnp.pad(iw_t.reshape(8, bsz, t_len), ((0, 0), (0, 0), (0, t_pad - t_len))).reshape(8, bsz * t_pad)
    oa = _dsa(q_in, iq_in, iw_in, k_hm, v_aug, ik_b, bsz=bsz, l_real=l_real, pos0=pos0, kb_rows=kb_rows,
              topk=topk, tq=tq)
    if t_pad != t_len:
        oa = oa.reshape(bsz, t_pad, A_WIDTH)[:, :t_len].reshape(n, A_WIDTH)
    ob, s_tiles = _hgrn(hb, w["hgrn_lb"], w["gnorm"], _state_to_tiles(s0), layer=layer, bsz=bsz, t_len=t_len,
                        tb=_row_tile(t_len, 256))
    x1 = _out_proj(oa, ob, w["w_out_a"], w["w_out_b"], x, _row_tile(n, 512))
    return x1, k_t, v_t, ik_t, _tiles_to_state(s_tiles)


def _from_feature_major(a, bsz, t_len, heads):
    if heads is None:
        return jnp.transpose(a.reshape(HEAD_DIM, bsz, t_len), (1, 2, 0))
    return jnp.transpose(a.reshape(heads, HEAD_DIM, bsz, t_len), (2, 3, 0, 1))


def kernel(x_prompt, x_sample, cache_k, cache_v, cache_idx_k, state_hgrn, cache_mem_k, cache_mem_v, mem_prompt,
           norm_mix, w_in, hgrn_lb, hgrn_gnorm, w_out, norm_mem, norm_memkv, w_mq, w_mk, w_mv, w_mo,
           norm_ffn, ffn_w1, ffn_w3, ffn_w2, moe_router, moe_w1, moe_w3, moe_w2, norm_final):
    depth = w_in.shape[0]
    bp, tp, _ = x_prompt.shape
    bs, ts, _ = x_sample.shape
    past = cache_k.shape[2]
    n_mem = mem_prompt.shape[1]
    tabs_p = _rope_tables(jnp.arange(tp, dtype=I32))
    tabs_s = tuple(jnp.tile(a, (bs, 1)) for a in _rope_tables(past + jnp.arange(ts, dtype=I32)))
    xp = x_prompt.reshape(bp * tp, D_MODEL)
    xs = x_sample.reshape(bs * ts, D_MODEL)
    mem = mem_prompt.reshape(bp * n_mem, D_MODEL)
    s0_p = jnp.zeros((bp, B_HEADS, HEAD_DIM, HEAD_DIM), F32)
    gfin = norm_final.reshape(1, D_MODEL)
    row2 = lambda a: a.reshape(1, -1)
    outs = {k: [] for k in ("kp", "vp", "ikp", "sp", "mkp", "mvp", "ks", "vs", "iks", "ss")}
    for l in range(depth):
        w = {"norm_mix": row2(norm_mix[l]), "w_in": w_in[l], "hgrn_lb": hgrn_lb.astype(F32),
             "gnorm": row2(jnp.tile(hgrn_gnorm[l], B_HEADS)),
             "w_out_a": w_out[l, :A_WIDTH].astype(BF16), "w_out_b": w_out[l, A_WIDTH:].astype(BF16)}
        xp, kp, vp, ikp, sp = _mixer(xp, tabs_p, w, bsz=bp, t_len=tp, pos0=0, past=None, s0=s0_p, layer=l)
        xs, ks, vs, iks, ss = _mixer(xs, tabs_s, w, bsz=bs, t_len=ts, pos0=past,
                                     past=(cache_k[l], cache_v[l], cache_idx_k[l]), s0=state_hgrn[l], layer=l)
        wkv = jnp.concatenate([w_mk[l], w_mv[l]], axis=1).astype(BF16)
        mkv = _norm_matmul(mem, row2(norm_memkv[l]), wkv, _row_tile(bp * n_mem, 256))
        mk_p = mkv[:, :D_MODEL].reshape(bp, n_mem, D_MODEL)
        mv_p = mkv[:, D_MODEL:].reshape(bp, n_mem, D_MODEL)
        wq, wo, gm = w_mq[l].astype(BF16), w_mo[l].astype(BF16), row2(norm_mem[l])
        xp = _mem_attn(xp, gm, wq, mk_p.astype(BF16), mv_p.astype(BF16), wo, tp, _row_tile(tp, 512))
        xs = _mem_attn(xs, gm, wq, cache_mem_k[l].reshape(bs, n_mem, D_MODEL).astype(BF16),
                       cache_mem_v[l].reshape(bs, n_mem, D_MODEL).astype(BF16), wo, ts, _row_tile(ts, 256))
        final = l == depth - 1
        gf = row2(norm_ffn[l])
        if l % 2 == 0:
            w1, w3, w2 = ffn_w1[l // 2].astype(BF16), ffn_w3[l // 2].astype(BF16), ffn_w2[l // 2].astype(BF16)
            tf = _ff_tile(w1.shape[1])
            xp = _ffn(xp, gf, w1, w3, w2, gfin, final=final, tm=_row_tile(bp * tp, 512), tf=tf)
            xs = _ffn(xs, gf, w1, w3, w2, gfin, final=final, tm=_row_tile(bs * ts, 512), tf=tf)
        else:
            w1, w3, w2 = moe_w1[l // 2].astype(BF16), moe_w3[l // 2].astype(BF16), moe_w2[l // 2].astype(BF16)
            rt = moe_router[l // 2].T
            tf = _ff_tile(w1.shape[2])
            comb_p, sel_p = _router(xp, gf, rt, _row_tile(bp * tp, 512))
            comb_s, sel_s = _router(xs, gf, rt, _row_tile(bs * ts, 512))
            xp = _moe(xp, gf, sel_p, sel_p.T, comb_p.T, w1, w3, w2, gfin, final=final,
                      nb=_row_tile(bp * tp, 1024), tf=tf)
            xs = _moe(xs, gf, sel_s, sel_s.T, comb_s.T, w1, w3, w2, gfin, final=final,
                      nb=_row_tile(bs * ts, 1024), tf=tf)
        outs["kp"].append(_from_feature_major(kp, bp, tp, A_KV_HEADS))
        outs["vp"].append(_from_feature_major(vp, bp, tp, A_KV_HEADS))
        outs["ikp"].append(_from_feature_major(ikp, bp, tp, None)); outs["sp"].append(sp)
        outs["mkp"].append(mk_p.reshape(bp, n_mem, MEM_HEADS, MEM_HEAD_DIM))
        outs["mvp"].append(mv_p.reshape(bp, n_mem, MEM_HEADS, MEM_HEAD_DIM))
        outs["ks"].append(_from_feature_major(ks, bs, ts, A_KV_HEADS))
        outs["vs"].append(_from_feature_major(vs, bs, ts, A_KV_HEADS))
        outs["iks"].append(_from_feature_major(iks, bs, ts, None)); outs["ss"].append(ss)
    y_prompt = xp.reshape(bp, tp, D_MODEL)
    y_sample = xs.reshape(bs, ts, D_MODEL)
    st = lambda k: jnp.stack(outs[k])
    return (y_prompt, y_sample, st("kp"), st("vp"), st("ikp"), st("sp"), st("mkp"), st("mvp"),
            st("ks"), st("vs"), st("iks"), st("ss"))
```

```python
import functools

import numpy as np
import jax
import jax.numpy as jnp
from jax import lax
from jax.experimental import pallas as pl
from jax.experimental.pallas import tpu as pltpu

F32 = jnp.float32
BF16 = jnp.bfloat16
I32 = jnp.int32
I16 = jnp.int16

D_MODEL = 1024
CHUNK = 64
EPS = 1e-6
ROPE_THETA = 500000.0
HEAD_DIM = 64
ROT_DIM = HEAD_DIM // 4
A_HEADS = 8
A_KV_HEADS = 2
A_GROUP = A_HEADS // A_KV_HEADS
A_WIDTH = A_HEADS * HEAD_DIM
KV_WIDTH = A_KV_HEADS * HEAD_DIM
IDX_HEADS = 4
IDX_WIDTH = IDX_HEADS * HEAD_DIM
TOPK_MAX = 256
B_HEADS = 8
B_WIDTH = B_HEADS * HEAD_DIM
B_CHUNK = 16
MEM_HEADS = 4
MEM_HEAD_DIM = D_MODEL // MEM_HEADS
N_EXPERTS = 8
LANES = 128
HALF = 256
IN_SPLITS = (A_WIDTH, KV_WIDTH, KV_WIDTH, IDX_WIDTH, HEAD_DIM, IDX_HEADS, B_WIDTH, B_WIDTH, B_WIDTH, B_WIDTH)
IN_WIDTH = sum(IN_SPLITS)
IN_WIDTH_PAD = 3200
VMEM_LIMIT = 56 * 1024 * 1024
INT_MIN = -2147483648
NEG_BIG = -1e30
LOG2E = 1.4426950408889634

_NT = (((1,), (1,)), ((), ()))
_TN = (((0,), (0,)), ((), ()))


def _cparams(*sem):
    return pltpu.CompilerParams(dimension_semantics=sem, vmem_limit_bytes=VMEM_LIMIT)


def _rms(x, g):
    return x * lax.rsqrt(jnp.mean(x * x, axis=-1, keepdims=True) + EPS) * g


def _sigmoid(x):
    return 1.0 / (1.0 + jnp.exp(-x))


def _dot(a, b):
    return jnp.dot(a, b, preferred_element_type=F32)


def _split2(x):
    hi = x.astype(BF16)
    lo = (x - hi.astype(F32)).astype(BF16)
    return hi, lo


def _split3(x):
    hi = x.astype(BF16)
    r = x - hi.astype(F32)
    mid = r.astype(BF16)
    lo = (r - mid.astype(F32)).astype(BF16)
    return hi, mid, lo


def _rope128(xb, c, s1, s2):
    return xb * c + pltpu.roll(xb, LANES - ROT_DIM // 2, 1) * s1 + pltpu.roll(xb, ROT_DIM // 2, 1) * s2


def _inproj_kernel(x_ref, g_ref, wraw_ref, c_ref, s1_ref, s2_ref,
                   qa_ref, hb_ref, iq_ref, kt_ref, vt_ref, ikt_ref, khm_ref, vaug_ref, ikb_ref, iwt_ref, w_ref):
    @pl.when(pl.program_id(0) == 0)
    def _():
        src = np.cumsum([0] + list(IN_SPLITS))
        dst = 0
        for i in (0, 6, 7, 8, 9, 1, 2, 3):
            width = int(src[i + 1] - src[i])
            w_ref[:, dst:dst + width] = wraw_ref[0, :, int(src[i]):int(src[i + 1])].astype(BF16)
            dst += width
        tail = jnp.concatenate([wraw_ref[0, :, int(src[4]):int(src[6])],
                                jnp.zeros((D_MODEL, LANES - HEAD_DIM - IDX_HEADS), F32)], axis=1)
        w_ref[:, dst:dst + LANES] = tail.astype(BF16)

    xn = _rms(x_ref[...], g_ref[...]).astype(BF16)
    c, s1, s2 = c_ref[...], s1_ref[...], s2_ref[...]
    for j in range(A_WIDTH // LANES):
        h = _dot(xn, w_ref[:, j * LANES:(j + 1) * LANES])
        qa_ref[:, j * LANES:(j + 1) * LANES] = _rope128(h, c, s1, s2)
    for j in range(4):
        lo = A_WIDTH + j * B_WIDTH
        hb_ref[:, j * B_WIDTH:(j + 1) * B_WIDTH] = _dot(xn, w_ref[:, lo:lo + B_WIDTH])
    base = A_WIDTH + 4 * B_WIDTH
    k = _rope128(_dot(xn, w_ref[:, base:base + 128]), c, s1, s2)
    v = _dot(xn, w_ref[:, base + 128:base + 256])
    iq_ref[:, 0:128] = _rope128(_dot(xn, w_ref[:, base + 256:base + 384]), c, s1, s2)
    iq_ref[:, 128:256] = _rope128(_dot(xn, w_ref[:, base + 384:base + 512]), c, s1, s2)
    lane = lax.broadcasted_iota(I32, c.shape, 1)
    first = lane < HEAD_DIM
    ikw = _rope128(_dot(xn, w_ref[:, base + 512:base + 640]),
                   jnp.where(first, c, 1.0), jnp.where(first, s1, 0.0), jnp.where(first, s2, 0.0))
    kt_ref[...] = k.T.reshape(kt_ref.shape)
    vt_ref[...] = v.T.reshape(vt_ref.shape)
    ikw_t = ikw.T
    ikt_ref[...] = ikw_t[:HEAD_DIM].reshape(ikt_ref.shape)
    iwt_ref[...] = ikw_t[HEAD_DIM:HEAD_DIM + 8]
    ikb_ref[...] = ikw[:, :HEAD_DIM].astype(BF16)
    ones_col = jnp.where(lane == HEAD_DIM, 1.0, 0.0)
    for j in range(A_KV_HEADS):
        khm_ref[j] = k[:, j * HEAD_DIM:(j + 1) * HEAD_DIM].astype(BF16)
        vj = v if j == 0 else pltpu.roll(v, LANES - j * HEAD_DIM, 1)
        vaug_ref[j] = jnp.where(first, vj, ones_col).astype(BF16)


def _in_proj(x, g, w_all, layer, tabs, t_len, tm):
    n = x.shape[0]
    nt = tabs[0].shape[0] // tm
    per_batch = t_len % tm == 0
    ntb = t_len // tm if per_batch else 1
    row = lambda i: (i, 0)
    tab = lambda i: (i % nt, 0)
    whole = lambda i: (0, 0)
    tmin = lambda i: (0, i)
    hmaj = lambda i: (0, i, 0)
    if per_batch:
        kv_spec = pl.BlockSpec((1, A_KV_HEADS, HEAD_DIM, tm), lambda i: (i // ntb, 0, 0, i % ntb))
        ik_spec = pl.BlockSpec((1, HEAD_DIM, tm), lambda i: (i // ntb, 0, i % ntb))
        kv_shape = jax.ShapeDtypeStruct((n // t_len, A_KV_HEADS, HEAD_DIM, t_len), F32)
        ik_shape = jax.ShapeDtypeStruct((n // t_len, HEAD_DIM, t_len), F32)
    else:
        kv_spec, ik_spec = pl.BlockSpec((KV_WIDTH, tm), tmin), pl.BlockSpec((HEAD_DIM, tm), tmin)
        kv_shape, ik_shape = jax.ShapeDtypeStruct((KV_WIDTH, n), F32), jax.ShapeDtypeStruct((HEAD_DIM, n), F32)
    return pl.pallas_call(
        _inproj_kernel,
        grid=(n // tm,),
        in_specs=[pl.BlockSpec((tm, D_MODEL), row), pl.BlockSpec((1, D_MODEL), whole),
                  pl.BlockSpec((1, D_MODEL, IN_WIDTH), lambda i: (layer, 0, 0), pipeline_mode=pl.Buffered(1)),
                  pl.BlockSpec((tm, LANES), tab), pl.BlockSpec((tm, LANES), tab), pl.BlockSpec((tm, LANES), tab)],
        out_specs=[pl.BlockSpec((tm, A_WIDTH), row), pl.BlockSpec((tm, 4 * B_WIDTH), row),
                   pl.BlockSpec((tm, IDX_WIDTH), row),
                   kv_spec, kv_spec, ik_spec,
                   pl.BlockSpec((A_KV_HEADS, tm, HEAD_DIM), hmaj), pl.BlockSpec((A_KV_HEADS, tm, LANES), hmaj),
                   pl.BlockSpec((tm, HEAD_DIM), row), pl.BlockSpec((8, tm), tmin)],
        out_shape=[jax.ShapeDtypeStruct((n, A_WIDTH), F32), jax.ShapeDtypeStruct((n, 4 * B_WIDTH), F32),
                   jax.ShapeDtypeStruct((n, IDX_WIDTH), F32),
                   kv_shape, kv_shape, ik_shape,
                   jax.ShapeDtypeStruct((A_KV_HEADS, n, HEAD_DIM), BF16),
                   jax.ShapeDtypeStruct((A_KV_HEADS, n, LANES), BF16),
                   jax.ShapeDtypeStruct((n, HEAD_DIM), BF16), jax.ShapeDtypeStruct((8, n), F32)],
        scratch_shapes=[pltpu.VMEM((D_MODEL, IN_WIDTH_PAD), BF16)],
        compiler_params=_cparams("arbitrary"),
        name="in_proj",
    )(x, g, w_all, *tabs)


def _rope_tables(pos):
    half = ROT_DIM // 2
    inv_freq = 1.0 / (ROPE_THETA ** (jnp.arange(half, dtype=F32) * (2.0 / ROT_DIM)))
    ang = pos.astype(F32)[:, None] * inv_freq[None, :]
    cos, sin = jnp.cos(ang), jnp.sin(ang)
    t = pos.shape[0]
    pad = HEAD_DIM - ROT_DIM
    c = jnp.concatenate([cos, cos, jnp.ones((t, pad), F32)], axis=1)
    s1 = jnp.concatenate([-sin, jnp.zeros((t, half + pad), F32)], axis=1)
    s2 = jnp.concatenate([jnp.zeros((t, half), F32), sin, jnp.zeros((t, pad), F32)], axis=1)
    rep = LANES // HEAD_DIM
    return tuple(jnp.tile(a, (1, rep)) for a in (c, s1, s2))


def _dsa_kernel(q_ref, iq_ref, iw_ref, k_ref, v_ref, ik_ref, o_ref, key_ref, hi_ref, lo_ref, acc_ref, m_ref,
                s0_ref, s1_ref, alpha_ref, p_ref, bias_ref, mt0_ref, mt1_ref, *, l_real, pos0, kb_rows, topk, tq):
    i = pl.program_id(1)
    qpos0 = pos0 + i * tq
    kend = jnp.minimum(((qpos0 + tq - 1) // CHUNK + 1) * CHUNK, l_real)
    cb_rows = 2 * kb_rows
    nkc = (kend + cb_rows - 1) // cb_rows
    nkb = 2 * nkc
    qchunk = (qpos0 + lax.broadcasted_iota(I32, (1, tq), 1)) // CHUNK
    row_iota = lax.broadcasted_iota(I32, (kb_rows, tq), 0)

    iq = iq_ref[...].astype(BF16)
    iq4 = jnp.concatenate([iq[:, h * HEAD_DIM:(h + 1) * HEAD_DIM] for h in range(IDX_HEADS)], axis=0)
    iw = iw_ref[...] * (IDX_HEADS ** -0.5 * HEAD_DIM ** -0.5)

    def score_body(kb, carry):
        r0 = pl.multiple_of(kb * kb_rows, kb_rows)
        ikb = ik_ref[pl.ds(r0, kb_rows), :]
        lg4 = lax.dot_general(ikb, iq4, _NT, preferred_element_type=F32)
        acc = jnp.zeros((kb_rows, tq), F32)
        for h in range(IDX_HEADS):
            acc = acc + iw[h:h + 1, :] * jnp.maximum(lg4[:, h * tq:(h + 1) * tq], 0.0)
        bits = pltpu.bitcast(acc + 0.0, I32)
        key = bits ^ ((bits >> 31) & 0x7FFFFFFF)
        kpos = r0 + row_iota
        adm = (kpos // CHUNK <= qchunk) & (kpos < l_real)
        key = jnp.where(adm, key, INT_MIN)
        key_ref[pl.ds(r0, kb_rows), :] = key
        hi_ref[pl.ds(r0, kb_rows), :] = (key >> 16).astype(I16)
        return carry

    lax.fori_loop(0, nkb, score_body, 0)

    def count(pred):
        def body(kc, c):
            r0 = pl.multiple_of(kc * cb_rows, cb_rows)
            hit = pred(key_ref[pl.ds(r0, cb_rows), :], r0)
            return c + jnp.where(hit, 1, 0).reshape(cb_rows // 8, 8, tq).sum(axis=0)
        c = lax.fori_loop(0, nkc, body, jnp.zeros((8, tq), I32))
        return c.sum(axis=0, keepdims=True)

    def count16(ref, pred):
        one, zero = jnp.ones((), BF16), jnp.zeros((), BF16)

        def body(kc, c):
            r0 = pl.multiple_of(kc * cb_rows, cb_rows)
            w = jnp.where(pred(ref[pl.ds(r0, cb_rows), :]), one, zero).reshape(cb_rows // 16, 16, tq)
            parts = [w[r] for r in range(cb_rows // 16)]
            while len(parts) > 1:
                parts = [a + b for a, b in zip(parts[::2], parts[1::2])] + parts[len(parts) & ~1:]
            return c + parts[0]
        c = lax.fori_loop(0, nkc, body, jnp.zeros((16, tq), BF16))
        return c.astype(F32).sum(axis=0, keepdims=True).astype(I32)

    def bisect16(ref, kth):
        c0 = count16(ref, lambda blk: blk >= jnp.zeros((), I16))
        start = jnp.where(c0 >= kth, 0, -32768).astype(I32)

        def bit_body(it, prefix):
            cand = prefix | lax.shift_left(jnp.int32(1), 14 - it)
            c = count16(ref, lambda blk: blk >= cand.astype(I16))
            return jnp.where(c >= kth, cand, prefix)

        return lax.fori_loop(0, 15, bit_body, start)

    tau_hi = bisect16(hi_ref, topk)
    tau_hi16 = tau_hi.astype(I16)
    kth_lo = topk - count16(hi_ref, lambda blk: blk > tau_hi16)

    def lo_body(kc, carry):
        r0 = pl.multiple_of(kc * cb_rows, cb_rows)
        lo = ((key_ref[pl.ds(r0, cb_rows), :] & 0xFFFF) - 32768).astype(I16)
        lo_ref[pl.ds(r0, cb_rows), :] = jnp.where(hi_ref[pl.ds(r0, cb_rows), :] == tau_hi16, lo,
                                                 jnp.full((), -32768, I16))
        return carry

    lax.fori_loop(0, nkc, lo_body, 0)
    tau_lo = bisect16(lo_ref, kth_lo)
    tau = lax.shift_left(tau_hi, 16) | (tau_lo + 32768)
    n_ge = count(lambda blk, r0: blk >= tau)

    @pl.when(jnp.max(jnp.where((n_ge > topk) & (tau != INT_MIN), 1, 0)) > 0)
    def _():
        need = (topk - count(lambda blk, r0: blk > tau)).astype(F32)
        ri = lax.broadcasted_iota(I32, (kb_rows, kb_rows), 0)
        ci = lax.broadcasted_iota(I32, (kb_rows, kb_rows), 1)
        upto = jnp.where(ci <= ri, 1.0, 0.0).astype(BF16)

        def strike(kc, seen):
            for half in range(2):
                rows = pl.ds(pl.multiple_of(kc * cb_rows + half * kb_rows, kb_rows), kb_rows)
                key = key_ref[rows, :]
                tie = key == tau
                rank = _dot(upto, jnp.where(tie, 1.0, 0.0).astype(BF16)) + seen
                key_ref[rows, :] = jnp.where(tie & (rank > need), INT_MIN, key)
                seen = rank[kb_rows - 1:kb_rows, :]
            return seen

        lax.fori_loop(0, nkc, strike, jnp.zeros((1, tq), F32))

    lane_rep = kb_rows // LANES
    tau_c = jnp.tile(jnp.broadcast_to(tau, (LANES, tq)).T, (1, lane_rep))
    acc_ref[...] = jnp.zeros(acc_ref.shape, F32)
    m_ref[...] = jnp.full(m_ref.shape, NEG_BIG, F32)
    qs = (q_ref[...] * (HEAD_DIM ** -0.5 * LOG2E)).astype(BF16)
    q4 = [jnp.concatenate([qs[:, (j * A_GROUP + g) * HEAD_DIM:(j * A_GROUP + g + 1) * HEAD_DIM]
                           for g in range(A_GROUP)], axis=0) for j in range(A_KV_HEADS)]

    def qk_scores(kb, s_slot, mt_slot):
        r0 = pl.multiple_of(jnp.minimum(kb, nkb - 1) * kb_rows, kb_rows)
        key_t = key_ref[pl.ds(r0, kb_rows), :].T
        bias_ref[...] = jnp.where((key_t >= tau_c) & (key_t != INT_MIN), 0.0, NEG_BIG)
        for j in range(A_KV_HEADS):
            s = lax.dot_general(q4[j], k_ref[j, pl.ds(r0, kb_rows), :], _NT, preferred_element_type=F32)
            for g in range(A_GROUP):
                rows = slice(g * tq, (g + 1) * tq)
                sm = s[rows, :] + bias_ref[...]
                s_slot[j, rows, :] = sm
                mt_slot[j, g] = jnp.broadcast_to(jnp.max(sm, axis=1, keepdims=True), (tq, LANES))

    def softmax_pv(kb, s_slot, mt_slot):
        r0 = pl.multiple_of(kb * kb_rows, kb_rows)
        for j in range(A_KV_HEADS):
            for g in range(A_GROUP):
                rows = slice(g * tq, (g + 1) * tq)
                m_old = m_ref[j, g]
                m_new = jnp.maximum(m_old, mt_slot[j, g])
                alpha_ref[j, g] = jnp.exp2(m_old - m_new)
                m_ref[j, g] = m_new
                p_ref[j, rows, :] = jnp.exp2(s_slot[j, rows, :] - jnp.tile(m_new, (1, lane_rep))).astype(BF16)
            pv = _dot(p_ref[j], v_ref[j, pl.ds(r0, kb_rows), :])
            acc_ref[j] = alpha_ref[j] * acc_ref[j] + pv.reshape(A_GROUP, tq, LANES)

    qk_scores(0, s0_ref, mt0_ref)

    def pair_body(i, carry):
        kb = 2 * i
        qk_scores(kb + 1, s1_ref, mt1_ref)
        softmax_pv(kb, s0_ref, mt0_ref)
        qk_scores(kb + 2, s0_ref, mt0_ref)
        softmax_pv(kb + 1, s1_ref, mt1_ref)
        return carry

    lax.fori_loop(0, nkc, pair_body, 0)

    outs = []
    for j in range(A_KV_HEADS):
        acc = acc_ref[j]
        for g in range(A_GROUP):
            outs.append(acc[g, :, :HEAD_DIM] * (1.0 / acc[g, :, HEAD_DIM:HEAD_DIM + 1]))
    o_ref[...] = jnp.concatenate(outs, axis=1)


def _dsa(q, iq, iw_t, k_hm, v_aug, ik, *, bsz, l_real, pos0, kb_rows, topk, tq):
    t_len = q.shape[0] // bsz
    l_pad = ik.shape[0] // bsz
    nq = t_len // tq
    assert l_pad // 16 <= 256, "packed hit counts are summed in bf16, exact only up to 256 per slot"
    kern = functools.partial(_dsa_kernel, l_real=l_real, pos0=pos0, kb_rows=kb_rows, topk=topk, tq=tq)
    qrow = lambda b, i: (b * nq + i, 0)
    return pl.pallas_call(
        kern,
        grid=(bsz, nq),
        in_specs=[pl.BlockSpec((tq, A_WIDTH), qrow), pl.BlockSpec((tq, IDX_WIDTH), qrow),
                  pl.BlockSpec((8, tq), lambda b, i: (0, b * nq + i)),
                  pl.BlockSpec((A_KV_HEADS, l_pad, HEAD_DIM), lambda b, i: (0, b, 0)),
                  pl.BlockSpec((A_KV_HEADS, l_pad, LANES), lambda b, i: (0, b, 0)),
                  pl.BlockSpec((l_pad, HEAD_DIM), lambda b, i: (b, 0))],
        out_specs=pl.BlockSpec((tq, A_WIDTH), qrow),
        out_shape=jax.ShapeDtypeStruct((bsz * t_len, A_WIDTH), F32),
        scratch_shapes=[pltpu.VMEM((l_pad, tq), I32), pltpu.VMEM((l_pad, tq), I16), pltpu.VMEM((l_pad, tq), I16),
                        pltpu.VMEM((A_KV_HEADS, A_GROUP, tq, LANES), F32),
                        pltpu.VMEM((A_KV_HEADS, A_GROUP, tq, LANES), F32),
                        pltpu.VMEM((A_KV_HEADS, A_GROUP * tq, kb_rows), F32),
                        pltpu.VMEM((A_KV_HEADS, A_GROUP * tq, kb_rows), F32),
                        pltpu.VMEM((A_KV_HEADS, A_GROUP, tq, LANES), F32),
                        pltpu.VMEM((A_KV_HEADS, A_GROUP * tq, kb_rows), BF16),
                        pltpu.VMEM((tq, kb_rows), F32),
                        pltpu.VMEM((A_KV_HEADS, A_GROUP, tq, LANES), F32),
                        pltpu.VMEM((A_KV_HEADS, A_GROUP, tq, LANES), F32)],
        compiler_params=_cparams("parallel", "arbitrary"),
        name="dsa",
    )(q, iq, iw_t, k_hm, v_aug, ik)


def _hgrn_kernel(bq_ref, bf_ref, bi_ref, bg_ref, lb_ref, gn_ref, s0_ref, o_ref, sfin_ref,
                 st_ref, q_s, kk_s, b_s, qe_s, kd_s, dec_s, o_s, e_s, *, layer, tb):
    tstep = pl.program_id(1)
    nchunk = tb // B_CHUNK

    @pl.when(tstep == 0)
    def _():
        st_ref[...] = s0_ref[0]

    lbr = lb_ref[...]
    lbe = jnp.exp(lbr - jnp.max(lbr, axis=0, keepdims=True))
    lbs = lbe / jnp.sum(lbe, axis=0, keepdims=True)
    cum = lbs[0:1, :]
    first = cum
    for r in range(1, layer + 1):
        cum = cum + lbs[r:r + 1, :]
    lb = cum - first

    f = lb + (1.0 - lb) * _sigmoid(bf_ref[...])
    bq = bq_ref[...]
    q = bq * _sigmoid(bq) * (HEAD_DIM ** -0.5)
    g3 = _split3(jnp.log(f))
    ri = lax.broadcasted_iota(I32, (tb, tb), 0)
    ci = lax.broadcasted_iota(I32, (tb, tb), 1)
    same = (ri // B_CHUNK) == (ci // B_CHUNK)
    tri = jnp.where(same & (ci <= ri), 1.0, 0.0).astype(BF16)
    ones = jnp.where(same, 1.0, 0.0).astype(BF16)
    b = _dot(tri, g3[0]) + _dot(tri, g3[1]) + _dot(tri, g3[2])
    blast = _dot(ones, g3[0]) + _dot(ones, g3[1]) + _dot(ones, g3[2])
    q_s[...] = q
    kk_s[...] = 1.0 - f
    b_s[...] = b
    qe_s[...] = q * jnp.exp(b)
    kd_s[...] = (1.0 - f) * jnp.exp(blast - b)
    dec_s[...] = jnp.exp(blast)

    hi = lax.broadcasted_iota(I32, (HALF, HALF), 0) // HEAD_DIM
    hj = lax.broadcasted_iota(I32, (HALF, HALF), 1) // HEAD_DIM
    bd_mask = hi == hj
    bd = jnp.where(bd_mask, 1.0, 0.0).astype(BF16)
    t_iota = lax.broadcasted_iota(I32, (B_CHUNK, B_WIDTH), 0)

    def seg_sum(x, two_pass):
        out = []
        for half in range(2):
            xs = x[:, half * HALF:(half + 1) * HALF]
            if two_pass:
                xh, xl = _split2(xs)
                out.append(_dot(xh, bd) + _dot(xl, bd))
            else:
                out.append(_dot(xs.astype(BF16), bd))
        return jnp.concatenate(out, axis=1)

    def chunk_body(c, carry):
        r0 = pl.multiple_of(c * B_CHUNK, B_CHUNK)
        rows = pl.ds(r0, B_CHUNK)
        qc, kc, bc, vc = q_s[rows, :], kk_s[rows, :], b_s[rows, :], bi_ref[rows, :]
        qe, kd = qe_s[rows, :].astype(BF16), kd_s[rows, :].astype(BF16)
        dec = dec_s[pl.ds(r0, 1), :]
        for s in range(B_CHUNK):
            ex = jnp.exp(jnp.where(t_iota >= s, bc - bc[s:s + 1, :], NEG_BIG))
            e_s[s * B_CHUNK:(s + 1) * B_CHUNK, :] = qc * kc[s:s + 1, :] * ex
        aexp = seg_sum(e_s[...], False)
        o = jnp.zeros((B_CHUNK, B_WIDTH), F32)
        for s in range(B_CHUNK):
            o = o + aexp[s * B_CHUNK:(s + 1) * B_CHUNK, :] * vc[s:s + 1, :]
        inter = []
        vb = vc.astype(BF16)
        for half in range(2):
            cols = slice(half * HALF, (half + 1) * HALF)
            st = st_ref[half]
            inter.append(lax.dot_general(qe[:, cols], st.astype(BF16), _NT, preferred_element_type=F32))
            upd = lax.dot_general(vb[:, cols], kd[:, cols], _TN, preferred_element_type=F32)
            st_ref[half] = st * dec[:, cols] + jnp.where(bd_mask, upd, 0.0)
        o_s[rows, :] = o + jnp.concatenate(inter, axis=1)
        return carry

    lax.fori_loop(0, nchunk, chunk_body, 0)

    o = o_s[...]
    ms = seg_sum(o * o, True) * (1.0 / HEAD_DIM)
    bg = bg_ref[...]
    o_ref[...] = o * lax.rsqrt(ms + EPS) * gn_ref[...] * (bg * _sigmoid(bg))

    @pl.when(tstep == pl.num_programs(1) - 1)
    def _():
        sfin_ref[0] = st_ref[...]


def _hgrn(hb, lb_raw, gn_tiled, s0_bd, *, layer, bsz, t_len, tb):
    nt = t_len // tb
    col = lambda j: (lambda b, t: (b * nt + t, j))
    scr = lambda r: pltpu.VMEM((r, B_WIDTH), F32)
    kern = functools.partial(_hgrn_kernel, layer=layer, tb=tb)
    return pl.pallas_call(
        kern,
        grid=(bsz, nt),
        in_specs=[pl.BlockSpec((tb, B_WIDTH), col(0)), pl.BlockSpec((tb, B_WIDTH), col(1)),
                  pl.BlockSpec((tb, B_WIDTH), col(2)), pl.BlockSpec((tb, B_WIDTH), col(3)),
                  pl.BlockSpec(lb_raw.shape, lambda b, t: (0, 0)),
                  pl.BlockSpec((1, B_WIDTH), lambda b, t: (0, 0)),
                  pl.BlockSpec((1, 2, HALF, HALF), lambda b, t: (b, 0, 0, 0))],
        out_specs=[pl.BlockSpec((tb, B_WIDTH), lambda b, t: (b * nt + t, 0)),
                   pl.BlockSpec((1, 2, HALF, HALF), lambda b, t: (b, 0, 0, 0))],
        out_shape=[jax.ShapeDtypeStruct((bsz * t_len, B_WIDTH), F32),
                   jax.ShapeDtypeStruct((bsz, 2, HALF, HALF), F32)],
        scratch_shapes=[pltpu.VMEM((2, HALF, HALF), F32), scr(tb), scr(tb), scr(tb), scr(tb), scr(tb), scr(tb),
                        scr(tb), scr(B_CHUNK * B_CHUNK)],
        compiler_params=_cparams("parallel", "arbitrary"),
        name="hgrn2",
    )(hb, hb, hb, hb, lb_raw, gn_tiled, s0_bd)


def _state_to_tiles(s):
    bsz = s.shape[0]
    st = jnp.swapaxes(s, -1, -2).reshape(bsz, 2, B_HEADS // 2, HEAD_DIM, HEAD_DIM)
    eye = jnp.eye(B_HEADS // 2, dtype=s.dtype)
    t = st[:, :, :, :, None, :] * eye[None, None, :, None, :, None]
    return t.reshape(bsz, 2, HALF, HALF)


def _tiles_to_state(t):
    bsz = t.shape[0]
    t6 = t.reshape(bsz, 2, B_HEADS // 2, HEAD_DIM, B_HEADS // 2, HEAD_DIM)
    diag = jnp.stack([t6[:, :, h, :, h, :] for h in range(B_HEADS // 2)], axis=2)
    return jnp.swapaxes(diag.reshape(bsz, B_HEADS, HEAD_DIM, HEAD_DIM), -1, -2)


def _outproj_kernel(a_ref, b_ref, wa_ref, wb_ref, x_ref, o_ref):
    o_ref[...] = (x_ref[...] + _dot(a_ref[...].astype(BF16), wa_ref[...])
                  + _dot(b_ref[...].astype(BF16), wb_ref[...]))


def _out_proj(a, b, wa, wb, x, tm):
    n = x.shape[0]
    row = lambda i: (i, 0)
    whole = lambda i: (0, 0)
    return pl.pallas_call(
        _outproj_kernel,
        grid=(n // tm,),
        in_specs=[pl.BlockSpec((tm, A_WIDTH), row), pl.BlockSpec((tm, B_WIDTH), row),
                  pl.BlockSpec((A_WIDTH, D_MODEL), whole), pl.BlockSpec((B_WIDTH, D_MODEL), whole),
                  pl.BlockSpec((tm, D_MODEL), row)],
        out_specs=pl.BlockSpec((tm, D_MODEL), row),
        out_shape=jax.ShapeDtypeStruct((n, D_MODEL), F32),
        compiler_params=_cparams("parallel"),
        name="out_proj",
    )(a, b, wa, wb, x)


def _normmm_kernel(x_ref, g_ref, w_ref, o_ref):
    o_ref[...] = _dot(_rms(x_ref[...], g_ref[...]).astype(BF16), w_ref[...])


def _norm_matmul(x, g, w, tm):
    n, m = x.shape[0], w.shape[1]
    return pl.pallas_call(
        _normmm_kernel,
        grid=(n // tm,),
        in_specs=[pl.BlockSpec((tm, D_MODEL), lambda i: (i, 0)), pl.BlockSpec((1, D_MODEL), lambda i: (0, 0)),
                  pl.BlockSpec((D_MODEL, m), lambda i: (0, 0))],
        out_specs=pl.BlockSpec((tm, m), lambda i: (i, 0)),
        out_shape=jax.ShapeDtypeStruct((n, m), F32),
        compiler_params=_cparams("parallel"),
        name="norm_matmul",
    )(x, g, w)


def _memattn_kernel(x_ref, g_ref, wq_ref, mk_ref, mv_ref, wo_ref, o_ref):
    x = x_ref[...]
    q = _dot(_rms(x, g_ref[...]).astype(BF16), wq_ref[...])
    heads = []
    for h in range(MEM_HEADS):
        cols = slice(h * MEM_HEAD_DIM, (h + 1) * MEM_HEAD_DIM)
        s = lax.dot_general(q[:, cols].astype(BF16), mk_ref[0, :, cols], _NT,
                            preferred_element_type=F32) * (MEM_HEAD_DIM ** -0.5)
        p = jnp.exp(s - jnp.max(s, axis=-1, keepdims=True))
        p = p * (1.0 / jnp.sum(p, axis=-1, keepdims=True))
        heads.append(_dot(p.astype(BF16), mv_ref[0, :, cols]).astype(BF16))
    o_ref[...] = x + _dot(jnp.concatenate(heads, axis=1), wo_ref[...])


def _mem_attn(x, g, wq, mk, mv, wo, rows_per_batch, tm):
    n = x.shape[0]
    n_mem = mk.shape[1]
    per = rows_per_batch // tm
    row = lambda i: (i, 0)
    whole = lambda i: (0, 0)
    mem = lambda i: (i // per, 0, 0)
    return pl.pallas_call(
        _memattn_kernel,
        grid=(n // tm,),
        in_specs=[pl.BlockSpec((tm, D_MODEL), row), pl.BlockSpec((1, D_MODEL), whole),
                  pl.BlockSpec((D_MODEL, D_MODEL), whole),
                  pl.BlockSpec((1, n_mem, D_MODEL), mem), pl.BlockSpec((1, n_mem, D_MODEL), mem),
                  pl.BlockSpec((D_MODEL, D_MODEL), whole)],
        out_specs=pl.BlockSpec((tm, D_MODEL), row),
        out_shape=jax.ShapeDtypeStruct((n, D_MODEL), F32),
        compiler_params=_cparams("parallel"),
        name="mem_attn",
    )(x, g, wq, mk, mv, wo)


def _swiglu_mid(xn, w1, w3):
    a = _dot(xn, w1)
    return (a * _sigmoid(a) * _dot(xn, w3)).astype(BF16)


def _ffn_kernel(x_ref, g_ref, w1_ref, w3_ref, w2_ref, gf_ref, o_ref, xn_ref, *, final):
    j = pl.program_id(1)

    @pl.when(j == 0)
    def _():
        x = x_ref[...]
        xn_ref[...] = _rms(x, g_ref[...]).astype(BF16)
        o_ref[...] = x

    o_ref[...] += _dot(_swiglu_mid(xn_ref[...], w1_ref[...], w3_ref[...]), w2_ref[...])

    if final:
        @pl.when(j == pl.num_programs(1) - 1)
        def _():
            o_ref[...] = _rms(o_ref[...], gf_ref[...])


def _ffn(x, g, w1, w3, w2, gf, *, final, tm, tf):
    n = x.shape[0]
    d_ff = w1.shape[1]
    row = lambda i, j: (i, 0)
    whole = lambda i, j: (0, 0)
    return pl.pallas_call(
        functools.partial(_ffn_kernel, final=final),
        grid=(n // tm, d_ff // tf),
        in_specs=[pl.BlockSpec((tm, D_MODEL), row), pl.BlockSpec((1, D_MODEL), whole),
                  pl.BlockSpec((D_MODEL, tf), lambda i, j: (0, j)), pl.BlockSpec((D_MODEL, tf), lambda i, j: (0, j)),
                  pl.BlockSpec((tf, D_MODEL), lambda i, j: (j, 0)), pl.BlockSpec((1, D_MODEL), whole)],
        out_specs=pl.BlockSpec((tm, D_MODEL), row),
        out_shape=jax.ShapeDtypeStruct((n, D_MODEL), F32),
        scratch_shapes=[pltpu.VMEM((tm, D_MODEL), BF16)],
        compiler_params=_cparams("parallel", "arbitrary"),
        name="ffn",
    )(x, g, w1, w3, w2, gf)


def _router_kernel(x_ref, g_ref, rt_ref, comb_ref, sel_ref):
    xn = _rms(x_ref[...], g_ref[...])
    logits = lax.dot_general(rt_ref[...], xn, _NT, preferred_element_type=F32,
                             precision=lax.Precision.HIGHEST)
    e_idx = lax.broadcasted_iota(I32, logits.shape, 0)
    m1 = jnp.max(logits, axis=0, keepdims=True)
    i1 = jnp.min(jnp.where(logits == m1, e_idx, N_EXPERTS), axis=0, keepdims=True)
    rest = jnp.where(e_idx == i1, -jnp.inf, logits)
    m2 = jnp.max(rest, axis=0, keepdims=True)
    i2 = jnp.min(jnp.where(rest == m2, e_idx, N_EXPERTS), axis=0, keepdims=True)
    e2 = jnp.exp(m2 - m1)
    g1 = 1.0 / (1.0 + e2)
    comb_ref[...] = jnp.where(e_idx == i1, g1, 0.0) + jnp.where(e_idx == i2, e2 * g1, 0.0)
    sel_ref[...] = jnp.where((e_idx == i1) | (e_idx == i2), 1.0, 0.0)


def _router(x, g, rt, tm):
    n = x.shape[0]
    return pl.pallas_call(
        _router_kernel,
        grid=(n // tm,),
        in_specs=[pl.BlockSpec((tm, D_MODEL), lambda i: (i, 0)), pl.BlockSpec((1, D_MODEL), lambda i: (0, 0)),
                  pl.BlockSpec((N_EXPERTS, D_MODEL), lambda i: (0, 0))],
        out_specs=[pl.BlockSpec((N_EXPERTS, tm), lambda i: (0, i)), pl.BlockSpec((N_EXPERTS, tm), lambda i: (0, i))],
        out_shape=[jax.ShapeDtypeStruct((N_EXPERTS, n), F32), jax.ShapeDtypeStruct((N_EXPERTS, n), F32)],
        compiler_params=_cparams("parallel"),
        name="moe_router",
    )(x, g, rt)


def _moe_kernel(x_ref, g_ref, selt_ref, sel_ref, comb_ref, w1_ref, w3_ref, w2_ref, gf_ref, o_ref,
                xn_ref, posr_ref, posc_ref, cnt_ref, xg_ref, yg_ref, *, final, nb, sub):
    e = pl.program_id(1)
    j = pl.program_id(2)
    last_j = pl.num_programs(2) - 1

    @pl.when((e == 0) & (j == 0))
    def _():
        x = x_ref[...]
        xn_ref[...] = _rms(x, g_ref[...]).astype(BF16)
        o_ref[...] = x
        ri = lax.broadcasted_iota(I32, (sub, sub), 0)
        ci = lax.broadcasted_iota(I32, (sub, sub), 1)
        before_r = jnp.where(ri < ci, 1.0, 0.0).astype(BF16)
        before_c = jnp.where(ci < ri, 1.0, 0.0).astype(BF16)
        run_r = jnp.zeros((N_EXPERTS, 1), F32)
        run_c = jnp.zeros((1, N_EXPERTS), F32)
        for c in range(nb // sub):
            cols = slice(c * sub, (c + 1) * sub)
            mr = selt_ref[:, cols]
            posr_ref[:, cols] = jnp.where(mr > 0.0, _dot(mr.astype(BF16), before_r) + run_r, -1.0)
            run_r = run_r + jnp.sum(mr, axis=1, keepdims=True)
            mc = sel_ref[cols, :]
            posc_ref[cols, :] = jnp.where(mc > 0.0, _dot(before_c, mc.astype(BF16)) + run_c, -1.0)
            run_c = run_c + jnp.sum(mc, axis=0, keepdims=True)
        for k in range(N_EXPERTS):
            cnt_ref[k] = run_c[0, k].astype(I32)

    cnt = cnt_ref[e]
    nfull = cnt // sub
    rem = cnt - nfull * sub
    tail_sizes = [m for m in (64, 128) if m < sub] + [sub]

    def for_tiles(fn):
        def body(s, carry):
            fn(pl.multiple_of(s * sub, sub), sub)
            return carry
        lax.fori_loop(0, nfull, body, 0)
        r0 = pl.multiple_of(nfull * sub, sub)
        lo = 0
        for m in tail_sizes:
            @pl.when((rem > lo) & (rem <= m))
            def _():
                fn(r0, m)
            lo = m

    def gather_matrix(r0, m):
        slot_r = posr_ref[pl.ds(e, 1), :]
        want = (r0 + lax.broadcasted_iota(I32, (m, nb), 0)).astype(F32)
        return jnp.where(slot_r == want, 1.0, 0.0).astype(BF16)

    def gather(r0, m):
        xg_ref[pl.ds(r0, m), :] = _dot(gather_matrix(r0, m), xn_ref[...]).astype(BF16)

    def ffn(r0, m):
        rows = pl.ds(r0, m)
        part = _dot(_swiglu_mid(xg_ref[rows, :], w1_ref[0], w3_ref[0]), w2_ref[0])

        @pl.when(j == 0)
        def _():
            yg_ref[rows, :] = part

        @pl.when(j > 0)
        def _():
            yg_ref[rows, :] += part

    @pl.when(j == 0)
    def _():
        for_tiles(gather)

    for_tiles(ffn)

    @pl.when(j == last_j)
    def _():
        c3 = _split3(comb_ref[...])
        lane8 = lax.broadcasted_iota(I32, (nb, N_EXPERTS), 1)
        slot_c = jnp.sum(jnp.where(lane8 == e, posc_ref[...], 0.0), axis=1, keepdims=True)

        def scatter(r0, m):
            pm = gather_matrix(r0, m)
            gates = _dot(pm, c3[0]) + _dot(pm, c3[1]) + _dot(pm, c3[2])
            lane_e = lax.broadcasted_iota(I32, (m, N_EXPERTS), 1) == e
            gate = jnp.sum(jnp.where(lane_e, gates, 0.0), axis=1, keepdims=True)
            y = (yg_ref[pl.ds(r0, m), :] * gate).astype(BF16)
            want = (r0 + lax.broadcasted_iota(I32, (nb, m), 1)).astype(F32)
            o_ref[...] += _dot(jnp.where(slot_c == want, 1.0, 0.0).astype(BF16), y)

        for_tiles(scatter)

        if final:
            @pl.when(e == pl.num_programs(1) - 1)
            def _():
                o_ref[...] = _rms(o_ref[...], gf_ref[...])


def _moe(x, g, sel_t, sel, comb, w1, w3, w2, gf, *, final, nb, tf):
    n = x.shape[0]
    d_ff = w1.shape[2]
    sub = min(nb, 256)
    row = lambda i, e, j: (i, 0)
    whole = lambda i, e, j: (0, 0)
    return pl.pallas_call(
        functools.partial(_moe_kernel, final=final, nb=nb, sub=sub),
        grid=(n // nb, N_EXPERTS, d_ff // tf),
        in_specs=[pl.BlockSpec((nb, D_MODEL), row), pl.BlockSpec((1, D_MODEL), whole),
                  pl.BlockSpec((N_EXPERTS, nb), lambda i, e, j: (0, i)),
                  pl.BlockSpec((nb, N_EXPERTS), row), pl.BlockSpec((nb, N_EXPERTS), row),
                  pl.BlockSpec((1, D_MODEL, tf), lambda i, e, j: (e, 0, j)),
                  pl.BlockSpec((1, D_MODEL, tf), lambda i, e, j: (e, 0, j)),
                  pl.BlockSpec((1, tf, D_MODEL), lambda i, e, j: (e, j, 0)), pl.BlockSpec((1, D_MODEL), whole)],
        out_specs=pl.BlockSpec((nb, D_MODEL), row),
        out_shape=jax.ShapeDtypeStruct((n, D_MODEL), F32),
        scratch_shapes=[pltpu.VMEM((nb, D_MODEL), BF16), pltpu.VMEM((N_EXPERTS, nb), F32),
                        pltpu.VMEM((nb, N_EXPERTS), F32), pltpu.SMEM((N_EXPERTS,), I32),
                        pltpu.VMEM((nb, D_MODEL), BF16), pltpu.VMEM((nb, D_MODEL), F32)],
        compiler_params=_cparams("parallel", "arbitrary", "arbitrary"),
        name="moe",
    )(x, g, sel_t, sel, comb, w1, w3, w2, gf)


def _rmsnorm_kernel(x_ref, g_ref, o_ref):
    o_ref[...] = _rms(x_ref[...], g_ref[...])


def _rmsnorm(x, g, tm):
    n = x.shape[0]
    return pl.pallas_call(
        _rmsnorm_kernel,
        grid=(n // tm,),
        in_specs=[pl.BlockSpec((tm, D_MODEL), lambda i: (i, 0)), pl.BlockSpec((1, D_MODEL), lambda i: (0, 0))],
        out_specs=pl.BlockSpec((tm, D_MODEL), lambda i: (i, 0)),
        out_shape=jax.ShapeDtypeStruct((n, D_MODEL), F32),
        compiler_params=_cparams("parallel"),
        name="final_norm",
    )(x, g)


def _row_tile(n, cap):
    t = min(n, cap)
    assert n % t == 0, (n, cap)
    return t


def _ff_tile(d_ff):
    for parts in (2, 4, 11, 22):
        if d_ff % parts == 0 and (d_ff // parts) % LANES == 0:
            return d_ff // parts
    return d_ff


def _mixer(x, tabs, w, *, bsz, t_len, pos0, past, s0, layer):
    n = bsz * t_len
    qa, hb, iq, k_t, v_t, ik_t, k_hm, v_aug, ik_b, iw_t = _in_proj(x, w["norm_mix"], w["w_in"], layer, tabs,
                                                                   t_len, _row_tile(n, 256))
    l_real = t_len if past is None else past[0].shape[1] + t_len
    topk = min(TOPK_MAX, l_real // 4)
    tq = 2 * LANES if t_len % (2 * LANES) == 0 else LANES
    kb_rows = 256 if l_real % 256 == 0 else LANES
    l_pad = -(-l_real // (2 * kb_rows)) * (2 * kb_rows)
    t_pad = -(-t_len // tq) * tq
    if past is not None:
        pk, pv, pik = past
        one = jnp.ones(pv.shape[:2] + (A_KV_HEADS, 1), BF16)
        zero = jnp.zeros(pv.shape[:2] + (A_KV_HEADS, LANES - HEAD_DIM - 1), BF16)
        pv_aug = jnp.concatenate([pv.astype(BF16), one, zero], axis=-1)
        hm = lambda a: jnp.transpose(a, (2, 0, 1, 3))
        k_hm = jnp.concatenate([hm(pk.astype(BF16)), k_hm.reshape(A_KV_HEADS, bsz, t_len, HEAD_DIM)], axis=2)
        v_aug = jnp.concatenate([hm(pv_aug), v_aug.reshape(A_KV_HEADS, bsz, t_len, LANES)], axis=2)
        ik_b = jnp.concatenate([pik.astype(BF16), ik_b.reshape(bsz, t_len, HEAD_DIM)], axis=1)
    if l_pad != l_real or past is not None:
        padl = l_pad - l_real
        k_hm = jnp.pad(k_hm.reshape(A_KV_HEADS, bsz, l_real, HEAD_DIM), ((0, 0), (0, 0), (0, padl), (0, 0)))
        v_aug = jnp.pad(v_aug.reshape(A_KV_HEADS, bsz, l_real, LANES), ((0, 0), (0, 0), (0, padl), (0, 0)))
        ik_b = jnp.pad(ik_b.reshape(bsz, l_real, HEAD_DIM), ((0, 0), (0, padl), (0, 0)))
        k_hm = k_hm.reshape(A_KV_HEADS, bsz * l_pad, HEAD_DIM)
        v_aug = v_aug.reshape(A_KV_HEADS, bsz * l_pad, LANES)
        ik_b = ik_b.reshape(bsz * l_pad, HEAD_DIM)
    q_in, iq_in, iw_in = qa, iq, iw_t
    if t_pad != t_len:
        padq = lambda a: jnp.pad(a.reshape(bsz, t_len, -1),
                                 ((0, 0), (0, t_pad - t_len), (0, 0))).reshape(bsz * t_pad, -1)
        q_in, iq_in = padq(qa), padq(iq)
        iw_in = jnp.pad(iw_t.reshape(8, bsz, t_len), ((0, 0), (0, 0), (0, t_pad - t_len))).reshape(8, bsz * t_pad)
    oa = _dsa(q_in, iq_in, iw_in, k_hm, v_aug, ik_b, bsz=bsz, l_real=l_real, pos0=pos0, kb_rows=kb_rows,
              topk=topk, tq=tq)
    if t_pad != t_len:
        oa = oa.reshape(bsz, t_pad, A_WIDTH)[:, :t_len].reshape(n, A_WIDTH)
    ob, s_tiles = _hgrn(hb, w["hgrn_lb"], w["gnorm"], _state_to_tiles(s0), layer=layer, bsz=bsz, t_len=t_len,
                        tb=_row_tile(t_len, 256))
    x1 = _out_proj(oa, ob, w["w_out_a"], w["w_out_b"], x, _row_tile(n, 512))
    return x1, k_t, v_t, ik_t, _tiles_to_state(s_tiles)


def _from_time_minor(a, bsz, t_len, heads):
    if a.ndim == 2:
        a = a.reshape((-1, HEAD_DIM, bsz, t_len))
        a = jnp.transpose(a, (2, 3, 0, 1))
    else:
        a = jnp.moveaxis(a, -1, 1)
        a = a.reshape(bsz, t_len, -1, HEAD_DIM)
    return a if heads is not None else a.reshape(bsz, t_len, HEAD_DIM)


def kernel(x_prompt, x_sample, cache_k, cache_v, cache_idx_k, state_hgrn, cache_mem_k, cache_mem_v, mem_prompt,
           norm_mix, w_in, hgrn_lb, hgrn_gnorm, w_out, norm_mem, norm_memkv, w_mq, w_mk, w_mv, w_mo,
           norm_ffn, ffn_w1, ffn_w3, ffn_w2, moe_router, moe_w1, moe_w3, moe_w2, norm_final):
    depth = w_in.shape[0]
    bp, tp, _ = x_prompt.shape
    bs, ts, _ = x_sample.shape
    past = cache_k.shape[2]
    n_mem = mem_prompt.shape[1]
    tabs_p = _rope_tables(jnp.arange(tp, dtype=I32))
    tabs_s = tuple(jnp.tile(a, (bs, 1)) for a in _rope_tables(past + jnp.arange(ts, dtype=I32)))
    xp = x_prompt.reshape(bp * tp, D_MODEL)
    xs = x_sample.reshape(bs * ts, D_MODEL)
    mem = mem_prompt.reshape(bp * n_mem, D_MODEL)
    s0_p = jnp.zeros((bp, B_HEADS, HEAD_DIM, HEAD_DIM), F32)
    gfin = norm_final.reshape(1, D_MODEL)
    row2 = lambda a: a.reshape(1, -1)
    outs = {k: [] for k in ("kp", "vp", "ikp", "sp", "mkp", "mvp", "ks", "vs", "iks", "ss")}
    for l in range(depth):
        w = {"norm_mix": row2(norm_mix[l]), "w_in": w_in, "hgrn_lb": hgrn_lb.astype(F32),
             "gnorm": row2(jnp.tile(hgrn_gnorm[l], B_HEADS)),
             "w_out_a": w_out[l, :A_WIDTH].astype(BF16), "w_out_b": w_out[l, A_WIDTH:].astype(BF16)}
        xp, kp, vp, ikp, sp = _mixer(xp, tabs_p, w, bsz=bp, t_len=tp, pos0=0, past=None, s0=s0_p, layer=l)
        xs, ks, vs, iks, ss = _mixer(xs, tabs_s, w, bsz=bs, t_len=ts, pos0=past,
                                     past=(cache_k[l], cache_v[l], cache_idx_k[l]), s0=state_hgrn[l], layer=l)
        wkv = jnp.concatenate([w_mk[l], w_mv[l]], axis=1).astype(BF16)
        mkv = _norm_matmul(mem, row2(norm_memkv[l]), wkv, _row_tile(bp * n_mem, 256))
        mk_p = mkv[:, :D_MODEL].reshape(bp, n_mem, D_MODEL)
        mv_p = mkv[:, D_MODEL:].reshape(bp, n_mem, D_MODEL)
        wq, wo, gm = w_mq[l].astype(BF16), w_mo[l].astype(BF16), row2(norm_mem[l])
        xp = _mem_attn(xp, gm, wq, mk_p.astype(BF16), mv_p.astype(BF16), wo, tp, _row_tile(tp, 512))
        xs = _mem_attn(xs, gm, wq, cache_mem_k[l].reshape(bs, n_mem, D_MODEL).astype(BF16),
                       cache_mem_v[l].reshape(bs, n_mem, D_MODEL).astype(BF16), wo, ts, _row_tile(ts, 256))
        final = l == depth - 1
        gf = row2(norm_ffn[l])
        if l % 2 == 0:
            w1, w3, w2 = ffn_w1[l // 2].astype(BF16), ffn_w3[l // 2].astype(BF16), ffn_w2[l // 2].astype(BF16)
            tf = _ff_tile(w1.shape[1])
            xp = _ffn(xp, gf, w1, w3, w2, gfin, final=final, tm=_row_tile(bp * tp, 512), tf=tf)
            xs = _ffn(xs, gf, w1, w3, w2, gfin, final=final, tm=_row_tile(bs * ts, 512), tf=tf)
        else:
            w1, w3, w2 = moe_w1[l // 2].astype(BF16), moe_w3[l // 2].astype(BF16), moe_w2[l // 2].astype(BF16)
            rt = moe_router[l // 2].T
            tf = _ff_tile(w1.shape[2])
            comb_p, sel_p = _router(xp, gf, rt, _row_tile(bp * tp, 512))
            comb_s, sel_s = _router(xs, gf, rt, _row_tile(bs * ts, 512))
            xp = _moe(xp, gf, sel_p, sel_p.T, comb_p.T, w1, w3, w2, gfin, final=final,
                      nb=_row_tile(bp * tp, 1024), tf=tf)
            xs = _moe(xs, gf, sel_s, sel_s.T, comb_s.T, w1, w3, w2, gfin, final=final,
                      nb=_row_tile(bs * ts, 1024), tf=tf)
        outs["kp"].append(_from_time_minor(kp, bp, tp, A_KV_HEADS))
        outs["vp"].append(_from_time_minor(vp, bp, tp, A_KV_HEADS))
        outs["ikp"].append(_from_time_minor(ikp, bp, tp, None)); outs["sp"].append(sp)
        outs["mkp"].append(mk_p.reshape(bp, n_mem, MEM_HEADS, MEM_HEAD_DIM))
        outs["mvp"].append(mv_p.reshape(bp, n_mem, MEM_HEADS, MEM_HEAD_DIM))
        outs["ks"].append(_from_time_minor(ks, bs, ts, A_KV_HEADS))
        outs["vs"].append(_from_time_minor(vs, bs, ts, A_KV_HEADS))
        outs["iks"].append(_from_time_minor(iks, bs, ts, None)); outs["ss"].append(ss)
    y_prompt = xp.reshape(bp, tp, D_MODEL)
    y_sample = xs.reshape(bs, ts, D_MODEL)
    st = lambda k: jnp.stack(outs[k])
    return (y_prompt, y_sample, st("kp"), st("vp"), st("ikp"), st("sp"), st("mkp"), st("mvp"),
            st("ks"), st("vs"), st("iks"), st("ss"))
```

```python
import functools

import numpy as np
import jax
import jax.numpy as jnp
from jax import lax
from jax.experimental import pallas as pl
from jax.experimental.pallas import tpu as pltpu

F32 = jnp.float32
BF16 = jnp.bfloat16
I32 = jnp.int32
I16 = jnp.int16

D_MODEL = 1024
CHUNK = 64
EPS = 1e-6
ROPE_THETA = 500000.0
HEAD_DIM = 64
ROT_DIM = HEAD_DIM // 4
A_HEADS = 8
A_KV_HEADS = 2
A_GROUP = A_HEADS // A_KV_HEADS
A_WIDTH = A_HEADS * HEAD_DIM
KV_WIDTH = A_KV_HEADS * HEAD_DIM
IDX_HEADS = 4
IDX_WIDTH = IDX_HEADS * HEAD_DIM
TOPK_MAX = 256
B_HEADS = 8
B_WIDTH = B_HEADS * HEAD_DIM
B_CHUNK = 16
MEM_HEADS = 4
MEM_HEAD_DIM = D_MODEL // MEM_HEADS
N_EXPERTS = 8
MOE_TILE_ROWS = 288
LANES = 128
HALF = 256
IN_SPLITS = (A_WIDTH, KV_WIDTH, KV_WIDTH, IDX_WIDTH, HEAD_DIM, IDX_HEADS, B_WIDTH, B_WIDTH, B_WIDTH, B_WIDTH)
IN_WIDTH = sum(IN_SPLITS)
IN_WIDTH_PAD = 3200
VMEM_LIMIT = 56 * 1024 * 1024
INT_MIN = -2147483648
NEG_BIG = -1e30
LOG2E = 1.4426950408889634

_NT = (((1,), (1,)), ((), ()))
_TN = (((0,), (0,)), ((), ()))


def _cparams(*sem):
    return pltpu.CompilerParams(dimension_semantics=sem, vmem_limit_bytes=VMEM_LIMIT)


def _rms(x, g):
    return x * lax.rsqrt(jnp.mean(x * x, axis=-1, keepdims=True) + EPS) * g


def _sigmoid(x):
    return 1.0 / (1.0 + jnp.exp(-x))


def _dot(a, b):
    return jnp.dot(a, b, preferred_element_type=F32)


def _split2(x):
    hi = x.astype(BF16)
    lo = (x - hi.astype(F32)).astype(BF16)
    return hi, lo


def _split3(x):
    hi = x.astype(BF16)
    r = x - hi.astype(F32)
    mid = r.astype(BF16)
    lo = (r - mid.astype(F32)).astype(BF16)
    return hi, mid, lo


def _rope128(xb, c, s1, s2):
    return xb * c + pltpu.roll(xb, LANES - ROT_DIM // 2, 1) * s1 + pltpu.roll(xb, ROT_DIM // 2, 1) * s2


def _inproj_kernel(x_ref, g_ref, wraw_ref, c_ref, s1_ref, s2_ref,
                   qa_ref, hb_ref, iq_ref, kt_ref, vt_ref, ikt_ref, khm_ref, vaug_ref, ikb_ref, iwt_ref, w_ref):
    @pl.when(pl.program_id(0) == 0)
    def _():
        src = np.cumsum([0] + list(IN_SPLITS))
        dst = 0
        for i in (0, 6, 7, 8, 9, 1, 2, 3):
            width = int(src[i + 1] - src[i])
            w_ref[:, dst:dst + width] = wraw_ref[0, :, int(src[i]):int(src[i + 1])].astype(BF16)
            dst += width
        tail = jnp.concatenate([wraw_ref[0, :, int(src[4]):int(src[6])],
                                jnp.zeros((D_MODEL, LANES - HEAD_DIM - IDX_HEADS), F32)], axis=1)
        w_ref[:, dst:dst + LANES] = tail.astype(BF16)

    xn = _rms(x_ref[...], g_ref[...]).astype(BF16)
    c, s1, s2 = c_ref[...], s1_ref[...], s2_ref[...]
    for j in range(A_WIDTH // LANES):
        h = _dot(xn, w_ref[:, j * LANES:(j + 1) * LANES])
        qa_ref[:, j * LANES:(j + 1) * LANES] = _rope128(h, c, s1, s2)
    for j in range(4):
        lo = A_WIDTH + j * B_WIDTH
        hb_ref[:, j * B_WIDTH:(j + 1) * B_WIDTH] = _dot(xn, w_ref[:, lo:lo + B_WIDTH])
    base = A_WIDTH + 4 * B_WIDTH
    k = _rope128(_dot(xn, w_ref[:, base:base + 128]), c, s1, s2)
    v = _dot(xn, w_ref[:, base + 128:base + 256])
    iq_ref[:, 0:128] = _rope128(_dot(xn, w_ref[:, base + 256:base + 384]), c, s1, s2)
    iq_ref[:, 128:256] = _rope128(_dot(xn, w_ref[:, base + 384:base + 512]), c, s1, s2)
    lane = lax.broadcasted_iota(I32, c.shape, 1)
    first = lane < HEAD_DIM
    ikw = _rope128(_dot(xn, w_ref[:, base + 512:base + 640]),
                   jnp.where(first, c, 1.0), jnp.where(first, s1, 0.0), jnp.where(first, s2, 0.0))
    kt_ref[...] = k.T.reshape(kt_ref.shape)
    vt_ref[...] = v.T.reshape(vt_ref.shape)
    ikw_t = ikw.T
    ikt_ref[...] = ikw_t[:HEAD_DIM].reshape(ikt_ref.shape)
    iwt_ref[...] = ikw_t[HEAD_DIM:HEAD_DIM + 8]
    ikb_ref[...] = ikw[:, :HEAD_DIM].astype(BF16)
    ones_col = jnp.where(lane == HEAD_DIM, 1.0, 0.0)
    for j in range(A_KV_HEADS):
        khm_ref[j] = k[:, j * HEAD_DIM:(j + 1) * HEAD_DIM].astype(BF16)
        vj = v if j == 0 else pltpu.roll(v, LANES - j * HEAD_DIM, 1)
        vaug_ref[j] = jnp.where(first, vj, ones_col).astype(BF16)


def _in_proj(x, g, w_all, layer, tabs, t_len, tm):
    n = x.shape[0]
    nt = tabs[0].shape[0] // tm
    per_batch = t_len % tm == 0
    ntb = t_len // tm if per_batch else 1
    row = lambda i: (i, 0)
    tab = lambda i: (i % nt, 0)
    whole = lambda i: (0, 0)
    tmin = lambda i: (0, i)
    hmaj = lambda i: (0, i, 0)
    if per_batch:
        kv_spec = pl.BlockSpec((1, A_KV_HEADS, HEAD_DIM, tm), lambda i: (i // ntb, 0, 0, i % ntb))
        ik_spec = pl.BlockSpec((1, HEAD_DIM, tm), lambda i: (i // ntb, 0, i % ntb))
        kv_shape = jax.ShapeDtypeStruct((n // t_len, A_KV_HEADS, HEAD_DIM, t_len), F32)
        ik_shape = jax.ShapeDtypeStruct((n // t_len, HEAD_DIM, t_len), F32)
    else:
        kv_spec, ik_spec = pl.BlockSpec((KV_WIDTH, tm), tmin), pl.BlockSpec((HEAD_DIM, tm), tmin)
        kv_shape, ik_shape = jax.ShapeDtypeStruct((KV_WIDTH, n), F32), jax.ShapeDtypeStruct((HEAD_DIM, n), F32)
    return pl.pallas_call(
        _inproj_kernel,
        grid=(n // tm,),
        in_specs=[pl.BlockSpec((tm, D_MODEL), row), pl.BlockSpec((1, D_MODEL), whole),
                  pl.BlockSpec((1, D_MODEL, IN_WIDTH), lambda i: (layer, 0, 0), pipeline_mode=pl.Buffered(1)),
                  pl.BlockSpec((tm, LANES), tab), pl.BlockSpec((tm, LANES), tab), pl.BlockSpec((tm, LANES), tab)],
        out_specs=[pl.BlockSpec((tm, A_WIDTH), row), pl.BlockSpec((tm, 4 * B_WIDTH), row),
                   pl.BlockSpec((tm, IDX_WIDTH), row),
                   kv_spec, kv_spec, ik_spec,
                   pl.BlockSpec((A_KV_HEADS, tm, HEAD_DIM), hmaj), pl.BlockSpec((A_KV_HEADS, tm, LANES), hmaj),
                   pl.BlockSpec((tm, HEAD_DIM), row), pl.BlockSpec((8, tm), tmin)],
        out_shape=[jax.ShapeDtypeStruct((n, A_WIDTH), F32), jax.ShapeDtypeStruct((n, 4 * B_WIDTH), F32),
                   jax.ShapeDtypeStruct((n, IDX_WIDTH), F32),
                   kv_shape, kv_shape, ik_shape,
                   jax.ShapeDtypeStruct((A_KV_HEADS, n, HEAD_DIM), BF16),
                   jax.ShapeDtypeStruct((A_KV_HEADS, n, LANES), BF16),
                   jax.ShapeDtypeStruct((n, HEAD_DIM), BF16), jax.ShapeDtypeStruct((8, n), F32)],
        scratch_shapes=[pltpu.VMEM((D_MODEL, IN_WIDTH_PAD), BF16)],
        compiler_params=_cparams("arbitrary"),
        name="in_proj",
    )(x, g, w_all, *tabs)


def _rope_tables(pos):
    half = ROT_DIM // 2
    inv_freq = 1.0 / (ROPE_THETA ** (jnp.arange(half, dtype=F32) * (2.0 / ROT_DIM)))
    ang = pos.astype(F32)[:, None] * inv_freq[None, :]
    cos, sin = jnp.cos(ang), jnp.sin(ang)
    t = pos.shape[0]
    pad = HEAD_DIM - ROT_DIM
    c = jnp.concatenate([cos, cos, jnp.ones((t, pad), F32)], axis=1)
    s1 = jnp.concatenate([-sin, jnp.zeros((t, half + pad), F32)], axis=1)
    s2 = jnp.concatenate([jnp.zeros((t, half), F32), sin, jnp.zeros((t, pad), F32)], axis=1)
    rep = LANES // HEAD_DIM
    return tuple(jnp.tile(a, (1, rep)) for a in (c, s1, s2))


def _dsa_kernel(q_ref, iq_ref, iw_ref, k_ref, v_ref, ik_ref, o_ref, key_ref, hi_ref, lo_ref, acc_ref, m_ref,
                s0_ref, s1_ref, alpha_ref, p_ref, bias_ref, mt0_ref, mt1_ref, *, l_real, pos0, kb_rows, topk, tq):
    i = pl.program_id(1)
    qpos0 = pos0 + i * tq
    kend = jnp.minimum(((qpos0 + tq - 1) // CHUNK + 1) * CHUNK, l_real)
    cb_rows = 2 * kb_rows
    nkc = (kend + cb_rows - 1) // cb_rows
    nkb = 2 * nkc
    qchunk = (qpos0 + lax.broadcasted_iota(I32, (1, tq), 1)) // CHUNK
    row_iota = lax.broadcasted_iota(I32, (kb_rows, tq), 0)

    iq = iq_ref[...].astype(BF16)
    iq4 = jnp.concatenate([iq[:, h * HEAD_DIM:(h + 1) * HEAD_DIM] for h in range(IDX_HEADS)], axis=0)
    iw = iw_ref[...] * (IDX_HEADS ** -0.5 * HEAD_DIM ** -0.5)

    def score_body(kb, carry):
        r0 = pl.multiple_of(kb * kb_rows, kb_rows)
        ikb = ik_ref[pl.ds(r0, kb_rows), :]
        lg4 = lax.dot_general(ikb, iq4, _NT, preferred_element_type=F32)
        acc = jnp.zeros((kb_rows, tq), F32)
        for h in range(IDX_HEADS):
            acc = acc + iw[h:h + 1, :] * jnp.maximum(lg4[:, h * tq:(h + 1) * tq], 0.0)
        bits = pltpu.bitcast(acc + 0.0, I32)
        key = bits ^ ((bits >> 31) & 0x7FFFFFFF)
        kpos = r0 + row_iota
        adm = (kpos // CHUNK <= qchunk) & (kpos < l_real)
        key = jnp.where(adm, key, INT_MIN)
        key_ref[pl.ds(r0, kb_rows), :] = key
        hi_ref[pl.ds(r0, kb_rows), :] = (key >> 16).astype(I16)
        return carry

    lax.fori_loop(0, nkb, score_body, 0)

    def count(pred):
        def body(kc, c):
            r0 = pl.multiple_of(kc * cb_rows, cb_rows)
            hit = pred(key_ref[pl.ds(r0, cb_rows), :], r0)
            return c + jnp.where(hit, 1, 0).reshape(cb_rows // 8, 8, tq).sum(axis=0)
        c = lax.fori_loop(0, nkc, body, jnp.zeros((8, tq), I32))
        return c.sum(axis=0, keepdims=True)

    def count16(ref, pred):
        one, zero = jnp.ones((), BF16), jnp.zeros((), BF16)

        def body(kc, c):
            r0 = pl.multiple_of(kc * cb_rows, cb_rows)
            w = jnp.where(pred(ref[pl.ds(r0, cb_rows), :]), one, zero).reshape(cb_rows // 16, 16, tq)
            parts = [w[r] for r in range(cb_rows // 16)]
            while len(parts) > 1:
                parts = [a + b for a, b in zip(parts[::2], parts[1::2])] + parts[len(parts) & ~1:]
            return c + parts[0]
        c = lax.fori_loop(0, nkc, body, jnp.zeros((16, tq), BF16))
        return c.astype(F32).sum(axis=0, keepdims=True).astype(I32)

    def bisect16(ref, kth):
        c0 = count16(ref, lambda blk: blk >= jnp.zeros((), I16))
        start = jnp.where(c0 >= kth, 0, -32768).astype(I32)

        def bit_body(it, prefix):
            cand = prefix | lax.shift_left(jnp.int32(1), 14 - it)
            c = count16(ref, lambda blk: blk >= cand.astype(I16))
            return jnp.where(c >= kth, cand, prefix)

        return lax.fori_loop(0, 15, bit_body, start)

    tau_hi = bisect16(hi_ref, topk)
    tau_hi16 = tau_hi.astype(I16)
    kth_lo = topk - count16(hi_ref, lambda blk: blk > tau_hi16)

    def lo_body(kc, carry):
        r0 = pl.multiple_of(kc * cb_rows, cb_rows)
        lo = ((key_ref[pl.ds(r0, cb_rows), :] & 0xFFFF) - 32768).astype(I16)
        lo_ref[pl.ds(r0, cb_rows), :] = jnp.where(hi_ref[pl.ds(r0, cb_rows), :] == tau_hi16, lo,
                                                 jnp.full((), -32768, I16))
        return carry

    lax.fori_loop(0, nkc, lo_body, 0)
    tau_lo = bisect16(lo_ref, kth_lo)
    tau = lax.shift_left(tau_hi, 16) | (tau_lo + 32768)
    n_ge = count(lambda blk, r0: blk >= tau)

    @pl.when(jnp.max(jnp.where((n_ge > topk) & (tau != INT_MIN), 1, 0)) > 0)
    def _():
        need = (topk - count(lambda blk, r0: blk > tau)).astype(F32)
        ri = lax.broadcasted_iota(I32, (kb_rows, kb_rows), 0)
        ci = lax.broadcasted_iota(I32, (kb_rows, kb_rows), 1)
        upto = jnp.where(ci <= ri, 1.0, 0.0).astype(BF16)

        def strike(kc, seen):
            for half in range(2):
                rows = pl.ds(pl.multiple_of(kc * cb_rows + half * kb_rows, kb_rows), kb_rows)
                key = key_ref[rows, :]
                tie = key == tau
                rank = _dot(upto, jnp.where(tie, 1.0, 0.0).astype(BF16)) + seen
                key_ref[rows, :] = jnp.where(tie & (rank > need), INT_MIN, key)
                seen = rank[kb_rows - 1:kb_rows, :]
            return seen

        lax.fori_loop(0, nkc, strike, jnp.zeros((1, tq), F32))

    lane_rep = kb_rows // LANES
    tau_c = jnp.tile(jnp.broadcast_to(tau, (LANES, tq)).T, (1, lane_rep))
    acc_ref[...] = jnp.zeros(acc_ref.shape, F32)
    m_ref[...] = jnp.full(m_ref.shape, NEG_BIG, F32)
    qs = (q_ref[...] * (HEAD_DIM ** -0.5 * LOG2E)).astype(BF16)
    q4 = [jnp.concatenate([qs[:, (j * A_GROUP + g) * HEAD_DIM:(j * A_GROUP + g + 1) * HEAD_DIM]
                           for g in range(A_GROUP)], axis=0) for j in range(A_KV_HEADS)]

    def qk_scores(kb, s_slot, mt_slot):
        r0 = pl.multiple_of(jnp.minimum(kb, nkb - 1) * kb_rows, kb_rows)
        key_t = key_ref[pl.ds(r0, kb_rows), :].T
        bias_ref[...] = jnp.where((key_t >= tau_c) & (key_t != INT_MIN), 0.0, NEG_BIG)
        for j in range(A_KV_HEADS):
            s = lax.dot_general(q4[j], k_ref[j, pl.ds(r0, kb_rows), :], _NT, preferred_element_type=F32)
            for g in range(A_GROUP):
                rows = slice(g * tq, (g + 1) * tq)
                sm = s[rows, :] + bias_ref[...]
                s_slot[j, rows, :] = sm
                mt_slot[j, g] = jnp.broadcast_to(jnp.max(sm, axis=1, keepdims=True), (tq, LANES))

    def softmax_pv(kb, s_slot, mt_slot):
        r0 = pl.multiple_of(kb * kb_rows, kb_rows)
        for j in range(A_KV_HEADS):
            for g in range(A_GROUP):
                rows = slice(g * tq, (g + 1) * tq)
                m_old = m_ref[j, g]
                m_new = jnp.maximum(m_old, mt_slot[j, g])
                alpha_ref[j, g] = jnp.exp2(m_old - m_new)
                m_ref[j, g] = m_new
                p_ref[j, rows, :] = jnp.exp2(s_slot[j, rows, :] - jnp.tile(m_new, (1, lane_rep))).astype(BF16)
            pv = _dot(p_ref[j], v_ref[j, pl.ds(r0, kb_rows), :])
            acc_ref[j] = alpha_ref[j] * acc_ref[j] + pv.reshape(A_GROUP, tq, LANES)

    qk_scores(0, s0_ref, mt0_ref)

    def pair_body(i, carry):
        kb = 2 * i
        qk_scores(kb + 1, s1_ref, mt1_ref)
        softmax_pv(kb, s0_ref, mt0_ref)
        qk_scores(kb + 2, s0_ref, mt0_ref)
        softmax_pv(kb + 1, s1_ref, mt1_ref)
        return carry

    lax.fori_loop(0, nkc, pair_body, 0)

    outs = []
    for j in range(A_KV_HEADS):
        acc = acc_ref[j]
        for g in range(A_GROUP):
            outs.append(acc[g, :, :HEAD_DIM] * (1.0 / acc[g, :, HEAD_DIM:HEAD_DIM + 1]))
    o_ref[...] = jnp.concatenate(outs, axis=1)


def _dsa(q, iq, iw_t, k_hm, v_aug, ik, *, bsz, l_real, pos0, kb_rows, topk, tq):
    t_len = q.shape[0] // bsz
    l_pad = ik.shape[0] // bsz
    nq = t_len // tq
    assert l_pad // 16 <= 256, "packed hit counts are summed in bf16, exact only up to 256 per slot"
    kern = functools.partial(_dsa_kernel, l_real=l_real, pos0=pos0, kb_rows=kb_rows, topk=topk, tq=tq)
    qrow = lambda b, i: (b * nq + i, 0)
    return pl.pallas_call(
        kern,
        grid=(bsz, nq),
        in_specs=[pl.BlockSpec((tq, A_WIDTH), qrow), pl.BlockSpec((tq, IDX_WIDTH), qrow),
                  pl.BlockSpec((8, tq), lambda b, i: (0, b * nq + i)),
                  pl.BlockSpec((A_KV_HEADS, l_pad, HEAD_DIM), lambda b, i: (0, b, 0)),
                  pl.BlockSpec((A_KV_HEADS, l_pad, LANES), lambda b, i: (0, b, 0)),
                  pl.BlockSpec((l_pad, HEAD_DIM), lambda b, i: (b, 0))],
        out_specs=pl.BlockSpec((tq, A_WIDTH), qrow),
        out_shape=jax.ShapeDtypeStruct((bsz * t_len, A_WIDTH), F32),
        scratch_shapes=[pltpu.VMEM((l_pad, tq), I32), pltpu.VMEM((l_pad, tq), I16), pltpu.VMEM((l_pad, tq), I16),
                        pltpu.VMEM((A_KV_HEADS, A_GROUP, tq, LANES), F32),
                        pltpu.VMEM((A_KV_HEADS, A_GROUP, tq, LANES), F32),
                        pltpu.VMEM((A_KV_HEADS, A_GROUP * tq, kb_rows), F32),
                        pltpu.VMEM((A_KV_HEADS, A_GROUP * tq, kb_rows), F32),
                        pltpu.VMEM((A_KV_HEADS, A_GROUP, tq, LANES), F32),
                        pltpu.VMEM((A_KV_HEADS, A_GROUP * tq, kb_rows), BF16),
                        pltpu.VMEM((tq, kb_rows), F32),
                        pltpu.VMEM((A_KV_HEADS, A_GROUP, tq, LANES), F32),
                        pltpu.VMEM((A_KV_HEADS, A_GROUP, tq, LANES), F32)],
        compiler_params=_cparams("parallel", "arbitrary"),
        name="dsa",
    )(q, iq, iw_t, k_hm, v_aug, ik)


def _hgrn_kernel(bq_ref, bf_ref, bi_ref, bg_ref, lb_ref, gn_ref, s0_ref, o_ref, sfin_ref,
                 st_ref, q_s, kk_s, b_s, qe_s, kd_s, dec_s, o_s, e_s, *, layer, tb, nseq):
    tstep = pl.program_id(1)
    nchunk = tb // B_CHUNK

    @pl.when(tstep == 0)
    def _():
        st_ref[...] = s0_ref[...]

    lbr = lb_ref[...]
    lbe = jnp.exp(lbr - jnp.max(lbr, axis=0, keepdims=True))
    lbs = lbe / jnp.sum(lbe, axis=0, keepdims=True)
    cum = lbs[0:1, :]
    first = cum
    for r in range(1, layer + 1):
        cum = cum + lbs[r:r + 1, :]
    lb = cum - first

    ri = lax.broadcasted_iota(I32, (tb, tb), 0)
    ci = lax.broadcasted_iota(I32, (tb, tb), 1)
    same = (ri // B_CHUNK) == (ci // B_CHUNK)
    tri = jnp.where(same & (ci <= ri), 1.0, 0.0).astype(BF16)
    ones = jnp.where(same, 1.0, 0.0).astype(BF16)
    for u in range(nseq):
        f = lb + (1.0 - lb) * _sigmoid(bf_ref[u])
        bq = bq_ref[u]
        q = bq * _sigmoid(bq) * (HEAD_DIM ** -0.5)
        g3 = _split3(jnp.log(f))
        b = _dot(tri, g3[0]) + _dot(tri, g3[1]) + _dot(tri, g3[2])
        blast = _dot(ones, g3[0]) + _dot(ones, g3[1]) + _dot(ones, g3[2])
        q_s[u] = q
        kk_s[u] = 1.0 - f
        b_s[u] = b
        qe_s[u] = q * jnp.exp(b)
        kd_s[u] = (1.0 - f) * jnp.exp(blast - b)
        dec_s[u] = jnp.exp(blast)

    hi = lax.broadcasted_iota(I32, (HALF, HALF), 0) // HEAD_DIM
    hj = lax.broadcasted_iota(I32, (HALF, HALF), 1) // HEAD_DIM
    bd_mask = hi == hj
    bd = jnp.where(bd_mask, 1.0, 0.0).astype(BF16)
    t_iota = lax.broadcasted_iota(I32, (B_CHUNK, B_WIDTH), 0)

    def seg_sum(x, two_pass):
        out = []
        for half in range(2):
            xs = x[:, half * HALF:(half + 1) * HALF]
            if two_pass:
                xh, xl = _split2(xs)
                out.append(_dot(xh, bd) + _dot(xl, bd))
            else:
                out.append(_dot(xs.astype(BF16), bd))
        return jnp.concatenate(out, axis=1)

    def chunk_body(c, carry):
        r0 = pl.multiple_of(c * B_CHUNK, B_CHUNK)
        rows = pl.ds(r0, B_CHUNK)
        for u in range(nseq):
            qc, kc, bc, vc = q_s[u, rows, :], kk_s[u, rows, :], b_s[u, rows, :], bi_ref[u, rows, :]
            qe, kd = qe_s[u, rows, :].astype(BF16), kd_s[u, rows, :].astype(BF16)
            dec = dec_s[u, pl.ds(r0, 1), :]
            for s in range(B_CHUNK):
                ex = jnp.exp(jnp.where(t_iota >= s, bc - bc[s:s + 1, :], NEG_BIG))
                e_s[u, s * B_CHUNK:(s + 1) * B_CHUNK, :] = qc * kc[s:s + 1, :] * ex
            aexp = seg_sum(e_s[u], False)
            o = jnp.zeros((B_CHUNK, B_WIDTH), F32)
            for s in range(B_CHUNK):
                o = o + aexp[s * B_CHUNK:(s + 1) * B_CHUNK, :] * vc[s:s + 1, :]
            inter = []
            vb = vc.astype(BF16)
            for half in range(2):
                cols = slice(half * HALF, (half + 1) * HALF)
                st = st_ref[u, half]
                inter.append(lax.dot_general(qe[:, cols], st.astype(BF16), _NT, preferred_element_type=F32))
                upd = lax.dot_general(vb[:, cols], kd[:, cols], _TN, preferred_element_type=F32)
                st_ref[u, half] = st * dec[:, cols] + jnp.where(bd_mask, upd, 0.0)
            o_s[u, rows, :] = o + jnp.concatenate(inter, axis=1)
        return carry

    lax.fori_loop(0, nchunk, chunk_body, 0)

    for u in range(nseq):
        o = o_s[u]
        ms = seg_sum(o * o, True) * (1.0 / HEAD_DIM)
        bg = bg_ref[u]
        o_ref[u] = o * lax.rsqrt(ms + EPS) * gn_ref[...] * (bg * _sigmoid(bg))

    @pl.when(tstep == pl.num_programs(1) - 1)
    def _():
        sfin_ref[...] = st_ref[...]


def _hgrn(hb, lb_raw, gn_tiled, s0_bd, *, layer, bsz, t_len, tb):
    nt = t_len // tb
    nseq = 2 if bsz % 2 == 0 else 1
    hb3 = hb.reshape(bsz, t_len, 4 * B_WIDTH)
    col = lambda j: (lambda b, t: (b, t, j))
    scr = lambda r: pltpu.VMEM((nseq, r, B_WIDTH), F32)
    kern = functools.partial(_hgrn_kernel, layer=layer, tb=tb, nseq=nseq)
    blk = (nseq, tb, B_WIDTH)
    out, s_fin = pl.pallas_call(
        kern,
        grid=(bsz // nseq, nt),
        in_specs=[pl.BlockSpec(blk, col(0)), pl.BlockSpec(blk, col(1)), pl.BlockSpec(blk, col(2)),
                  pl.BlockSpec(blk, col(3)),
                  pl.BlockSpec(lb_raw.shape, lambda b, t: (0, 0)),
                  pl.BlockSpec((1, B_WIDTH), lambda b, t: (0, 0)),
                  pl.BlockSpec((nseq, 2, HALF, HALF), lambda b, t: (b, 0, 0, 0))],
        out_specs=[pl.BlockSpec(blk, lambda b, t: (b, t, 0)),
                   pl.BlockSpec((nseq, 2, HALF, HALF), lambda b, t: (b, 0, 0, 0))],
        out_shape=[jax.ShapeDtypeStruct((bsz, t_len, B_WIDTH), F32),
                   jax.ShapeDtypeStruct((bsz, 2, HALF, HALF), F32)],
        scratch_shapes=[pltpu.VMEM((nseq, 2, HALF, HALF), F32), scr(tb), scr(tb), scr(tb), scr(tb), scr(tb),
                        scr(tb), scr(tb), scr(B_CHUNK * B_CHUNK)],
        compiler_params=_cparams("parallel", "arbitrary"),
        name="hgrn2",
    )(hb3, hb3, hb3, hb3, lb_raw, gn_tiled, s0_bd)
    return out.reshape(bsz * t_len, B_WIDTH), s_fin


def _state_to_tiles(s):
    bsz = s.shape[0]
    st = jnp.swapaxes(s, -1, -2).reshape(bsz, 2, B_HEADS // 2, HEAD_DIM, HEAD_DIM)
    eye = jnp.eye(B_HEADS // 2, dtype=s.dtype)
    t = st[:, :, :, :, None, :] * eye[None, None, :, None, :, None]
    return t.reshape(bsz, 2, HALF, HALF)


def _tiles_to_state(t):
    bsz = t.shape[0]
    t6 = t.reshape(bsz, 2, B_HEADS // 2, HEAD_DIM, B_HEADS // 2, HEAD_DIM)
    diag = jnp.stack([t6[:, :, h, :, h, :] for h in range(B_HEADS // 2)], axis=2)
    return jnp.swapaxes(diag.reshape(bsz, B_HEADS, HEAD_DIM, HEAD_DIM), -1, -2)


def _outproj_kernel(a_ref, b_ref, wa_ref, wb_ref, x_ref, o_ref):
    o_ref[...] = (x_ref[...] + _dot(a_ref[...].astype(BF16), wa_ref[...])
                  + _dot(b_ref[...].astype(BF16), wb_ref[...]))


def _out_proj(a, b, wa, wb, x, tm):
    n = x.shape[0]
    row = lambda i: (i, 0)
    whole = lambda i: (0, 0)
    return pl.pallas_call(
        _outproj_kernel,
        grid=(n // tm,),
        in_specs=[pl.BlockSpec((tm, A_WIDTH), row), pl.BlockSpec((tm, B_WIDTH), row),
                  pl.BlockSpec((A_WIDTH, D_MODEL), whole), pl.BlockSpec((B_WIDTH, D_MODEL), whole),
                  pl.BlockSpec((tm, D_MODEL), row)],
        out_specs=pl.BlockSpec((tm, D_MODEL), row),
        out_shape=jax.ShapeDtypeStruct((n, D_MODEL), F32),
        compiler_params=_cparams("parallel"),
        name="out_proj",
    )(a, b, wa, wb, x)


def _normmm_kernel(x_ref, g_ref, w_ref, o_ref):
    o_ref[...] = _dot(_rms(x_ref[...], g_ref[...]).astype(BF16), w_ref[...])


def _norm_matmul(x, g, w, tm):
    n, m = x.shape[0], w.shape[1]
    return pl.pallas_call(
        _normmm_kernel,
        grid=(n // tm,),
        in_specs=[pl.BlockSpec((tm, D_MODEL), lambda i: (i, 0)), pl.BlockSpec((1, D_MODEL), lambda i: (0, 0)),
                  pl.BlockSpec((D_MODEL, m), lambda i: (0, 0))],
        out_specs=pl.BlockSpec((tm, m), lambda i: (i, 0)),
        out_shape=jax.ShapeDtypeStruct((n, m), F32),
        compiler_params=_cparams("parallel"),
        name="norm_matmul",
    )(x, g, w)


def _memattn_kernel(x_ref, g_ref, wq_ref, mk_ref, mv_ref, wo_ref, o_ref):
    x = x_ref[...]
    q = _dot(_rms(x, g_ref[...]).astype(BF16), wq_ref[...])
    heads = []
    for h in range(MEM_HEADS):
        cols = slice(h * MEM_HEAD_DIM, (h + 1) * MEM_HEAD_DIM)
        s = lax.dot_general(q[:, cols].astype(BF16), mk_ref[0, :, cols], _NT,
                            preferred_element_type=F32) * (MEM_HEAD_DIM ** -0.5)
        p = jnp.exp(s - jnp.max(s, axis=-1, keepdims=True))
        p = p * (1.0 / jnp.sum(p, axis=-1, keepdims=True))
        heads.append(_dot(p.astype(BF16), mv_ref[0, :, cols]).astype(BF16))
    o_ref[...] = x + _dot(jnp.concatenate(heads, axis=1), wo_ref[...])


def _mem_attn(x, g, wq, mk, mv, wo, rows_per_batch, tm):
    n = x.shape[0]
    n_mem = mk.shape[1]
    per = rows_per_batch // tm
    row = lambda i: (i, 0)
    whole = lambda i: (0, 0)
    mem = lambda i: (i // per, 0, 0)
    return pl.pallas_call(
        _memattn_kernel,
        grid=(n // tm,),
        in_specs=[pl.BlockSpec((tm, D_MODEL), row), pl.BlockSpec((1, D_MODEL), whole),
                  pl.BlockSpec((D_MODEL, D_MODEL), whole),
                  pl.BlockSpec((1, n_mem, D_MODEL), mem), pl.BlockSpec((1, n_mem, D_MODEL), mem),
                  pl.BlockSpec((D_MODEL, D_MODEL), whole)],
        out_specs=pl.BlockSpec((tm, D_MODEL), row),
        out_shape=jax.ShapeDtypeStruct((n, D_MODEL), F32),
        compiler_params=_cparams("parallel"),
        name="mem_attn",
    )(x, g, wq, mk, mv, wo)


def _swiglu_mid(xn, w1, w3):
    a = _dot(xn, w1)
    return (a * _sigmoid(a) * _dot(xn, w3)).astype(BF16)


def _ffn_kernel(x_ref, g_ref, w1_ref, w3_ref, w2_ref, gf_ref, o_ref, xn_ref, *, final):
    j = pl.program_id(1)

    @pl.when(j == 0)
    def _():
        x = x_ref[...]
        xn_ref[...] = _rms(x, g_ref[...]).astype(BF16)
        o_ref[...] = x

    o_ref[...] += _dot(_swiglu_mid(xn_ref[...], w1_ref[...], w3_ref[...]), w2_ref[...])

    if final:
        @pl.when(j == pl.num_programs(1) - 1)
        def _():
            o_ref[...] = _rms(o_ref[...], gf_ref[...])


def _ffn(x, g, w1, w3, w2, gf, *, final, tm, tf):
    n = x.shape[0]
    d_ff = w1.shape[1]
    row = lambda i, j: (i, 0)
    whole = lambda i, j: (0, 0)
    return pl.pallas_call(
        functools.partial(_ffn_kernel, final=final),
        grid=(n // tm, d_ff // tf),
        in_specs=[pl.BlockSpec((tm, D_MODEL), row), pl.BlockSpec((1, D_MODEL), whole),
                  pl.BlockSpec((D_MODEL, tf), lambda i, j: (0, j)), pl.BlockSpec((D_MODEL, tf), lambda i, j: (0, j)),
                  pl.BlockSpec((tf, D_MODEL), lambda i, j: (j, 0)), pl.BlockSpec((1, D_MODEL), whole)],
        out_specs=pl.BlockSpec((tm, D_MODEL), row),
        out_shape=jax.ShapeDtypeStruct((n, D_MODEL), F32),
        scratch_shapes=[pltpu.VMEM((tm, D_MODEL), BF16)],
        compiler_params=_cparams("parallel", "arbitrary"),
        name="ffn",
    )(x, g, w1, w3, w2, gf)


def _router_kernel(x_ref, g_ref, rt_ref, comb_ref, sel_ref):
    xn = _rms(x_ref[...], g_ref[...])
    logits = lax.dot_general(rt_ref[...], xn, _NT, preferred_element_type=F32,
                             precision=lax.Precision.HIGHEST)
    e_idx = lax.broadcasted_iota(I32, logits.shape, 0)
    m1 = jnp.max(logits, axis=0, keepdims=True)
    i1 = jnp.min(jnp.where(logits == m1, e_idx, N_EXPERTS), axis=0, keepdims=True)
    rest = jnp.where(e_idx == i1, -jnp.inf, logits)
    m2 = jnp.max(rest, axis=0, keepdims=True)
    i2 = jnp.min(jnp.where(rest == m2, e_idx, N_EXPERTS), axis=0, keepdims=True)
    e2 = jnp.exp(m2 - m1)
    g1 = 1.0 / (1.0 + e2)
    comb_ref[...] = jnp.where(e_idx == i1, g1, 0.0) + jnp.where(e_idx == i2, e2 * g1, 0.0)
    sel_ref[...] = jnp.where((e_idx == i1) | (e_idx == i2), 1.0, 0.0)


def _router(x, g, rt, tm):
    n = x.shape[0]
    return pl.pallas_call(
        _router_kernel,
        grid=(n // tm,),
        in_specs=[pl.BlockSpec((tm, D_MODEL), lambda i: (i, 0)), pl.BlockSpec((1, D_MODEL), lambda i: (0, 0)),
                  pl.BlockSpec((N_EXPERTS, D_MODEL), lambda i: (0, 0))],
        out_specs=[pl.BlockSpec((N_EXPERTS, tm), lambda i: (0, i)), pl.BlockSpec((N_EXPERTS, tm), lambda i: (0, i))],
        out_shape=[jax.ShapeDtypeStruct((N_EXPERTS, n), F32), jax.ShapeDtypeStruct((N_EXPERTS, n), F32)],
        compiler_params=_cparams("parallel"),
        name="moe_router",
    )(x, g, rt)


def _moe_kernel(x_ref, g_ref, selt_ref, sel_ref, comb_ref, w1_ref, w3_ref, w2_ref, gf_ref, o_ref,
                xn_ref, posr_ref, posc_ref, cnt_ref, xg_ref, yg_ref, *, final, nb, sub):
    e = pl.program_id(1)
    j = pl.program_id(2)
    last_j = pl.num_programs(2) - 1

    @pl.when((e == 0) & (j == 0))
    def _():
        x = x_ref[...]
        xn_ref[...] = _rms(x, g_ref[...]).astype(BF16)
        o_ref[...] = x
        cc = min(nb, 256)
        ri = lax.broadcasted_iota(I32, (cc, cc), 0)
        ci = lax.broadcasted_iota(I32, (cc, cc), 1)
        before_r = jnp.where(ri < ci, 1.0, 0.0).astype(BF16)
        before_c = jnp.where(ci < ri, 1.0, 0.0).astype(BF16)
        run_r = jnp.zeros((N_EXPERTS, 1), F32)
        run_c = jnp.zeros((1, N_EXPERTS), F32)
        for c in range(nb // cc):
            cols = slice(c * cc, (c + 1) * cc)
            mr = selt_ref[:, cols]
            posr_ref[:, cols] = jnp.where(mr > 0.0, _dot(mr.astype(BF16), before_r) + run_r, -1.0)
            run_r = run_r + jnp.sum(mr, axis=1, keepdims=True)
            mc = sel_ref[cols, :]
            posc_ref[cols, :] = jnp.where(mc > 0.0, _dot(before_c, mc.astype(BF16)) + run_c, -1.0)
            run_c = run_c + jnp.sum(mc, axis=0, keepdims=True)
        for k in range(N_EXPERTS):
            cnt_ref[k] = run_c[0, k].astype(I32)

    cnt = cnt_ref[e]
    nfull = cnt // sub
    rem = cnt - nfull * sub
    tail_sizes = [m for m in (64, 128) if m < sub] + [sub]

    def for_tiles(fn):
        def body(s, carry):
            fn(pl.multiple_of(s * sub, sub), sub)
            return carry
        lax.fori_loop(0, nfull, body, 0)
        r0 = pl.multiple_of(nfull * sub, sub)
        lo = 0
        for m in tail_sizes:
            @pl.when((rem > lo) & (rem <= m))
            def _():
                fn(r0, m)
            lo = m

    def gather_matrix(r0, m):
        slot_r = posr_ref[pl.ds(e, 1), :]
        want = (r0 + lax.broadcasted_iota(I32, (m, nb), 0)).astype(F32)
        return jnp.where(slot_r == want, 1.0, 0.0).astype(BF16)

    def gather(r0, m):
        xg_ref[pl.ds(r0, m), :] = _dot(gather_matrix(r0, m), xn_ref[...]).astype(BF16)

    def ffn(r0, m):
        rows = pl.ds(r0, m)
        part = _dot(_swiglu_mid(xg_ref[rows, :], w1_ref[0], w3_ref[0]), w2_ref[0])

        @pl.when(j == 0)
        def _():
            yg_ref[rows, :] = part

        @pl.when(j > 0)
        def _():
            yg_ref[rows, :] += part

    @pl.when(j == 0)
    def _():
        for_tiles(gather)

    for_tiles(ffn)

    @pl.when(j == last_j)
    def _():
        c3 = _split3(comb_ref[...])
        lane8 = lax.broadcasted_iota(I32, (nb, N_EXPERTS), 1)
        slot_c = jnp.sum(jnp.where(lane8 == e, posc_ref[...], 0.0), axis=1, keepdims=True)

        def scatter(r0, m):
            pm = gather_matrix(r0, m)
            gates = _dot(pm, c3[0]) + _dot(pm, c3[1]) + _dot(pm, c3[2])
            lane_e = lax.broadcasted_iota(I32, (m, N_EXPERTS), 1) == e
            gate = jnp.sum(jnp.where(lane_e, gates, 0.0), axis=1, keepdims=True)
            y = (yg_ref[pl.ds(r0, m), :] * gate).astype(BF16)
            want = (r0 + lax.broadcasted_iota(I32, (nb, m), 1)).astype(F32)
            o_ref[...] += _dot(jnp.where(slot_c == want, 1.0, 0.0).astype(BF16), y)

        for_tiles(scatter)

        if final:
            @pl.when(e == pl.num_programs(1) - 1)
            def _():
                o_ref[...] = _rms(o_ref[...], gf_ref[...])


def _moe(x, g, sel_t, sel, comb, w1, w3, w2, gf, *, final, nb, tf):
    n = x.shape[0]
    d_ff = w1.shape[2]
    sub = min(nb, MOE_TILE_ROWS)
    row = lambda i, e, j: (i, 0)
    whole = lambda i, e, j: (0, 0)
    return pl.pallas_call(
        functools.partial(_moe_kernel, final=final, nb=nb, sub=sub),
        grid=(n // nb, N_EXPERTS, d_ff // tf),
        in_specs=[pl.BlockSpec((nb, D_MODEL), row), pl.BlockSpec((1, D_MODEL), whole),
                  pl.BlockSpec((N_EXPERTS, nb), lambda i, e, j: (0, i)),
                  pl.BlockSpec((nb, N_EXPERTS), row), pl.BlockSpec((nb, N_EXPERTS), row),
                  pl.BlockSpec((1, D_MODEL, tf), lambda i, e, j: (e, 0, j)),
                  pl.BlockSpec((1, D_MODEL, tf), lambda i, e, j: (e, 0, j)),
                  pl.BlockSpec((1, tf, D_MODEL), lambda i, e, j: (e, j, 0)), pl.BlockSpec((1, D_MODEL), whole)],
        out_specs=pl.BlockSpec((nb, D_MODEL), row),
        out_shape=jax.ShapeDtypeStruct((n, D_MODEL), F32),
        scratch_shapes=[pltpu.VMEM((nb, D_MODEL), BF16), pltpu.VMEM((N_EXPERTS, nb), F32),
                        pltpu.VMEM((nb, N_EXPERTS), F32), pltpu.SMEM((N_EXPERTS,), I32),
                        pltpu.VMEM((nb + sub, D_MODEL), BF16), pltpu.VMEM((nb + sub, D_MODEL), F32)],
        compiler_params=_cparams("parallel", "arbitrary", "arbitrary"),
        name="moe",
    )(x, g, sel_t, sel, comb, w1, w3, w2, gf)


def _rmsnorm_kernel(x_ref, g_ref, o_ref):
    o_ref[...] = _rms(x_ref[...], g_ref[...])


def _rmsnorm(x, g, tm):
    n = x.shape[0]
    return pl.pallas_call(
        _rmsnorm_kernel,
        grid=(n // tm,),
        in_specs=[pl.BlockSpec((tm, D_MODEL), lambda i: (i, 0)), pl.BlockSpec((1, D_MODEL), lambda i: (0, 0))],
        out_specs=pl.BlockSpec((tm, D_MODEL), lambda i: (i, 0)),
        out_shape=jax.ShapeDtypeStruct((n, D_MODEL), F32),
        compiler_params=_cparams("parallel"),
        name="final_norm",
    )(x, g)


def _row_tile(n, cap):
    t = min(n, cap)
    assert n % t == 0, (n, cap)
    return t


def _ff_tile(d_ff):
    for parts in (2, 4, 11, 22):
        if d_ff % parts == 0 and (d_ff // parts) % LANES == 0:
            return d_ff // parts
    return d_ff


def _mixer(x, tabs, w, *, bsz, t_len, pos0, past, s0, layer):
    n = bsz * t_len
    qa, hb, iq, k_t, v_t, ik_t, k_hm, v_aug, ik_b, iw_t = _in_proj(x, w["norm_mix"], w["w_in"], layer, tabs,
                                                                   t_len, _row_tile(n, 512))
    l_real = t_len if past is None else past[0].shape[1] + t_len
    topk = min(TOPK_MAX, l_real // 4)
    tq = 2 * LANES if t_len % (2 * LANES) == 0 else LANES
    kb_rows = 256 if l_real % 256 == 0 else LANES
    l_pad = -(-l_real // (2 * kb_rows)) * (2 * kb_rows)
    t_pad = -(-t_len // tq) * tq
    if past is not None:
        pk, pv, pik = past
        one = jnp.ones(pv.shape[:2] + (A_KV_HEADS, 1), BF16)
        zero = jnp.zeros(pv.shape[:2] + (A_KV_HEADS, LANES - HEAD_DIM - 1), BF16)
        pv_aug = jnp.concatenate([pv.astype(BF16), one, zero], axis=-1)
        hm = lambda a: jnp.transpose(a, (2, 0, 1, 3))
        k_hm = jnp.concatenate([hm(pk.astype(BF16)), k_hm.reshape(A_KV_HEADS, bsz, t_len, HEAD_DIM)], axis=2)
        v_aug = jnp.concatenate([hm(pv_aug), v_aug.reshape(A_KV_HEADS, bsz, t_len, LANES)], axis=2)
        ik_b = jnp.concatenate([pik.astype(BF16), ik_b.reshape(bsz, t_len, HEAD_DIM)], axis=1)
    if l_pad != l_real or past is not None:
        padl = l_pad - l_real
        k_hm = jnp.pad(k_hm.reshape(A_KV_HEADS, bsz, l_real, HEAD_DIM), ((0, 0), (0, 0), (0, padl), (0, 0)))
        v_aug = jnp.pad(v_aug.reshape(A_KV_HEADS, bsz, l_real, LANES), ((0, 0), (0, 0), (0, padl), (0, 0)))
        ik_b = jnp.pad(ik_b.reshape(bsz, l_real, HEAD_DIM), ((0, 0), (0, padl), (0, 0)))
        k_hm = k_hm.reshape(A_KV_HEADS, bsz * l_pad, HEAD_DIM)
        v_aug = v_aug.reshape(A_KV_HEADS, bsz * l_pad, LANES)
        ik_b = ik_b.reshape(bsz * l_pad, HEAD_DIM)
    q_in, iq_in, iw_in = qa, iq, iw_t
    if t_pad != t_len:
        padq = lambda a: jnp.pad(a.reshape(bsz, t_len, -1),
                                 ((0, 0), (0, t_pad - t_len), (0, 0))).reshape(bsz * t_pad, -1)
        q_in, iq_in = padq(qa), padq(iq)
        iw_in = jnp.pad(iw_t.reshape(8, bsz, t_len), ((0, 0), (0, 0), (0, t_pad - t_len))).reshape(8, bsz * t_pad)
    oa = _dsa(q_in, iq_in, iw_in, k_hm, v_aug, ik_b, bsz=bsz, l_real=l_real, pos0=pos0, kb_rows=kb_rows,
              topk=topk, tq=tq)
    if t_pad != t_len:
        oa = oa.reshape(bsz, t_pad, A_WIDTH)[:, :t_len].reshape(n, A_WIDTH)
    ob, s_tiles = _hgrn(hb, w["hgrn_lb"], w["gnorm"], _state_to_tiles(s0), layer=layer, bsz=bsz, t_len=t_len,
                        tb=_row_tile(t_len, 256))
    x1 = _out_proj(oa, ob, w["w_out_a"], w["w_out_b"], x, _row_tile(n, 512))
    return x1, k_t, v_t, ik_t, _tiles_to_state(s_tiles)


def _from_time_minor(a, bsz, t_len, heads):
    if a.ndim == 2:
        a = a.reshape((-1, HEAD_DIM, bsz, t_len))
        a = jnp.transpose(a, (2, 3, 0, 1))
    else:
        a = jnp.moveaxis(a, -1, 1)
        a = a.reshape(bsz, t_len, -1, HEAD_DIM)
    return a if heads is not None else a.reshape(bsz, t_len, HEAD_DIM)


def kernel(x_prompt, x_sample, cache_k, cache_v, cache_idx_k, state_hgrn, cache_mem_k, cache_mem_v, mem_prompt,
           norm_mix, w_in, hgrn_lb, hgrn_gnorm, w_out, norm_mem, norm_memkv, w_mq, w_mk, w_mv, w_mo,
           norm_ffn, ffn_w1, ffn_w3, ffn_w2, moe_router, moe_w1, moe_w3, moe_w2, norm_final):
    depth = w_in.shape[0]
    bp, tp, _ = x_prompt.shape
    bs, ts, _ = x_sample.shape
    past = cache_k.shape[2]
    n_mem = mem_prompt.shape[1]
    tabs_p = _rope_tables(jnp.arange(tp, dtype=I32))
    tabs_s = tuple(jnp.tile(a, (bs, 1)) for a in _rope_tables(past + jnp.arange(ts, dtype=I32)))
    xp = x_prompt.reshape(bp * tp, D_MODEL)
    xs = x_sample.reshape(bs * ts, D_MODEL)
    mem = mem_prompt.reshape(bp * n_mem, D_MODEL)
    s0_p = jnp.zeros((bp, B_HEADS, HEAD_DIM, HEAD_DIM), F32)
    gfin = norm_final.reshape(1, D_MODEL)
    row2 = lambda a: a.reshape(1, -1)
    outs = {k: [] for k in ("kp", "vp", "ikp", "sp", "mkp", "mvp", "ks", "vs", "iks", "ss")}
    for l in range(depth):
        w = {"norm_mix": row2(norm_mix[l]), "w_in": w_in, "hgrn_lb": hgrn_lb.astype(F32),
             "gnorm": row2(jnp.tile(hgrn_gnorm[l], B_HEADS)),
             "w_out_a": w_out[l, :A_WIDTH].astype(BF16), "w_out_b": w_out[l, A_WIDTH:].astype(BF16)}
        xp, kp, vp, ikp, sp = _mixer(xp, tabs_p, w, bsz=bp, t_len=tp, pos0=0, past=None, s0=s0_p, layer=l)
        xs, ks, vs, iks, ss = _mixer(xs, tabs_s, w, bsz=bs, t_len=ts, pos0=past,
                                     past=(cache_k[l], cache_v[l], cache_idx_k[l]), s0=state_hgrn[l], layer=l)
        wkv = jnp.concatenate([w_mk[l], w_mv[l]], axis=1).astype(BF16)
        mkv = _norm_matmul(mem, row2(norm_memkv[l]), wkv, _row_tile(bp * n_mem, 256))
        mk_p = mkv[:, :D_MODEL].reshape(bp, n_mem, D_MODEL)
        mv_p = mkv[:, D_MODEL:].reshape(bp, n_mem, D_MODEL)
        wq, wo, gm = w_mq[l].astype(BF16), w_mo[l].astype(BF16), row2(norm_mem[l])
        xp = _mem_attn(xp, gm, wq, mk_p.astype(BF16), mv_p.astype(BF16), wo, tp, _row_tile(tp, 512))
        xs = _mem_attn(xs, gm, wq, cache_mem_k[l].reshape(bs, n_mem, D_MODEL).astype(BF16),
                       cache_mem_v[l].reshape(bs, n_mem, D_MODEL).astype(BF16), wo, ts, _row_tile(ts, 256))
        final = l == depth - 1
        gf = row2(norm_ffn[l])
        if l % 2 == 0:
            w1, w3, w2 = ffn_w1[l // 2].astype(BF16), ffn_w3[l // 2].astype(BF16), ffn_w2[l // 2].astype(BF16)
            tf = _ff_tile(w1.shape[1])
            xp = _ffn(xp, gf, w1, w3, w2, gfin, final=final, tm=_row_tile(bp * tp, 512), tf=tf)
            xs = _ffn(xs, gf, w1, w3, w2, gfin, final=final, tm=_row_tile(bs * ts, 512), tf=tf)
        else:
            w1, w3, w2 = moe_w1[l // 2].astype(BF16), moe_w3[l // 2].astype(BF16), moe_w2[l // 2].astype(BF16)
            rt = moe_router[l // 2].T
            tf = _ff_tile(w1.shape[2])
            comb_p, sel_p = _router(xp, gf, rt, _row_tile(bp * tp, 512))
            comb_s, sel_s = _router(xs, gf, rt, _row_tile(bs * ts, 512))
            xp = _moe(xp, gf, sel_p, sel_p.T, comb_p.T, w1, w3, w2, gfin, final=final,
                      nb=_row_tile(bp * tp, 1024), tf=tf)
            xs = _moe(xs, gf, sel_s, sel_s.T, comb_s.T, w1, w3, w2, gfin, final=final,
                      nb=_row_tile(bs * ts, 1024), tf=tf)
        outs["kp"].append(_from_time_minor(kp, bp, tp, A_KV_HEADS))
        outs["vp"].append(_from_time_minor(vp, bp, tp, A_KV_HEADS))
        outs["ikp"].append(_from_time_minor(ikp, bp, tp, None)); outs["sp"].append(sp)
        outs["mkp"].append(mk_p.reshape(bp, n_mem, MEM_HEADS, MEM_HEAD_DIM))
        outs["mvp"].append(mv_p.reshape(bp, n_mem, MEM_HEADS, MEM_HEAD_DIM))
        outs["ks"].append(_from_time_minor(ks, bs, ts, A_KV_HEADS))
        outs["vs"].append(_from_time_minor(vs, bs, ts, A_KV_HEADS))
        outs["iks"].append(_from_time_minor(iks, bs, ts, None)); outs["ss"].append(ss)
    y_prompt = xp.reshape(bp, tp, D_MODEL)
    y_sample = xs.reshape(bs, ts, D_MODEL)
    st = lambda k: jnp.stack(outs[k])
    return (y_prompt, y_sample, st("kp"), st("vp"), st("ikp"), st("sp"), st("mkp"), st("mvp"),
            st("ks"), st("vs"), st("iks"), st("ss"))
```

```python
import functools

import numpy as np
import jax
import jax.numpy as jnp
from jax import lax
from jax.experimental import pallas as pl
from jax.experimental.pallas import tpu as pltpu

F32 = jnp.float32
BF16 = jnp.bfloat16
I32 = jnp.int32
I16 = jnp.int16

D_MODEL = 1024
CHUNK = 64
EPS = 1e-6
ROPE_THETA = 500000.0
HEAD_DIM = 64
ROT_DIM = HEAD_DIM // 4
A_HEADS = 8
A_KV_HEADS = 2
A_GROUP = A_HEADS // A_KV_HEADS
A_WIDTH = A_HEADS * HEAD_DIM
KV_WIDTH = A_KV_HEADS * HEAD_DIM
IDX_HEADS = 4
IDX_WIDTH = IDX_HEADS * HEAD_DIM
TOPK_MAX = 256
B_HEADS = 8
B_WIDTH = B_HEADS * HEAD_DIM
B_CHUNK = 16
MEM_HEADS = 4
MEM_HEAD_DIM = D_MODEL // MEM_HEADS
N_EXPERTS = 8
MOE_TILE_ROWS = 256
LANES = 128
HALF = 256
IN_SPLITS = (A_WIDTH, KV_WIDTH, KV_WIDTH, IDX_WIDTH, HEAD_DIM, IDX_HEADS, B_WIDTH, B_WIDTH, B_WIDTH, B_WIDTH)
IN_WIDTH = sum(IN_SPLITS)
IN_WIDTH_PAD = 3200
VMEM_LIMIT = 56 * 1024 * 1024
INT_MIN = -2147483648
NEG_BIG = -1e30
LOG2E = 1.4426950408889634

_NT = (((1,), (1,)), ((), ()))
_TN = (((0,), (0,)), ((), ()))


def _cparams(*sem):
    return pltpu.CompilerParams(dimension_semantics=sem, vmem_limit_bytes=VMEM_LIMIT)


def _rms(x, g):
    return x * lax.rsqrt(jnp.mean(x * x, axis=-1, keepdims=True) + EPS) * g


def _sigmoid(x):
    return 1.0 / (1.0 + jnp.exp(-x))


def _dot(a, b):
    return jnp.dot(a, b, preferred_element_type=F32)


def _split2(x):
    hi = x.astype(BF16)
    lo = (x - hi.astype(F32)).astype(BF16)
    return hi, lo


def _split3(x):
    hi = x.astype(BF16)
    r = x - hi.astype(F32)
    mid = r.astype(BF16)
    lo = (r - mid.astype(F32)).astype(BF16)
    return hi, mid, lo


def _rope128(xb, c, s1, s2):
    return xb * c + pltpu.roll(xb, LANES - ROT_DIM // 2, 1) * s1 + pltpu.roll(xb, ROT_DIM // 2, 1) * s2


def _inproj_kernel(x_ref, g_ref, wraw_ref, c_ref, s1_ref, s2_ref,
                   qa_ref, hb_ref, iq_ref, kt_ref, vt_ref, ikt_ref, khm_ref, vaug_ref, ikb_ref, iwt_ref, w_ref):
    @pl.when(pl.program_id(0) == 0)
    def _():
        src = np.cumsum([0] + list(IN_SPLITS))
        dst = 0
        for i in (0, 6, 7, 8, 9, 1, 2, 3):
            width = int(src[i + 1] - src[i])
            w_ref[:, dst:dst + width] = wraw_ref[0, :, int(src[i]):int(src[i + 1])].astype(BF16)
            dst += width
        tail = jnp.concatenate([wraw_ref[0, :, int(src[4]):int(src[6])],
                                jnp.zeros((D_MODEL, LANES - HEAD_DIM - IDX_HEADS), F32)], axis=1)
        w_ref[:, dst:dst + LANES] = tail.astype(BF16)

    xn = _rms(x_ref[...], g_ref[...]).astype(BF16)
    c, s1, s2 = c_ref[...], s1_ref[...], s2_ref[...]
    for j in range(A_WIDTH // LANES):
        h = _dot(xn, w_ref[:, j * LANES:(j + 1) * LANES])
        qa_ref[:, j * LANES:(j + 1) * LANES] = _rope128(h, c, s1, s2)
    for j in range(4):
        lo = A_WIDTH + j * B_WIDTH
        hb_ref[:, j * B_WIDTH:(j + 1) * B_WIDTH] = _dot(xn, w_ref[:, lo:lo + B_WIDTH])
    base = A_WIDTH + 4 * B_WIDTH
    k = _rope128(_dot(xn, w_ref[:, base:base + 128]), c, s1, s2)
    v = _dot(xn, w_ref[:, base + 128:base + 256])
    iq_ref[:, 0:128] = _rope128(_dot(xn, w_ref[:, base + 256:base + 384]), c, s1, s2)
    iq_ref[:, 128:256] = _rope128(_dot(xn, w_ref[:, base + 384:base + 512]), c, s1, s2)
    lane = lax.broadcasted_iota(I32, c.shape, 1)
    first = lane < HEAD_DIM
    ikw = _rope128(_dot(xn, w_ref[:, base + 512:base + 640]),
                   jnp.where(first, c, 1.0), jnp.where(first, s1, 0.0), jnp.where(first, s2, 0.0))
    kt_ref[...] = k.T.reshape(kt_ref.shape)
    vt_ref[...] = v.T.reshape(vt_ref.shape)
    ikw_t = ikw.T
    ikt_ref[...] = ikw_t[:HEAD_DIM].reshape(ikt_ref.shape)
    iwt_ref[...] = ikw_t[HEAD_DIM:HEAD_DIM + 8]
    ikb_ref[...] = ikw[:, :HEAD_DIM].astype(BF16)
    ones_col = jnp.where(lane == HEAD_DIM, 1.0, 0.0)
    for j in range(A_KV_HEADS):
        khm_ref[j] = k[:, j * HEAD_DIM:(j + 1) * HEAD_DIM].astype(BF16)
        vj = v if j == 0 else pltpu.roll(v, LANES - j * HEAD_DIM, 1)
        vaug_ref[j] = jnp.where(first, vj, ones_col).astype(BF16)


def _in_proj(x, g, w_all, layer, tabs, t_len, tm):
    n = x.shape[0]
    nt = tabs[0].shape[0] // tm
    per_batch = t_len % tm == 0
    ntb = t_len // tm if per_batch else 1
    row = lambda i: (i, 0)
    tab = lambda i: (i % nt, 0)
    whole = lambda i: (0, 0)
    tmin = lambda i: (0, i)
    hmaj = lambda i: (0, i, 0)
    if per_batch:
        kv_spec = pl.BlockSpec((1, A_KV_HEADS, HEAD_DIM, tm), lambda i: (i // ntb, 0, 0, i % ntb))
        ik_spec = pl.BlockSpec((1, HEAD_DIM, tm), lambda i: (i // ntb, 0, i % ntb))
        kv_shape = jax.ShapeDtypeStruct((n // t_len, A_KV_HEADS, HEAD_DIM, t_len), F32)
        ik_shape = jax.ShapeDtypeStruct((n // t_len, HEAD_DIM, t_len), F32)
    else:
        kv_spec, ik_spec = pl.BlockSpec((KV_WIDTH, tm), tmin), pl.BlockSpec((HEAD_DIM, tm), tmin)
        kv_shape, ik_shape = jax.ShapeDtypeStruct((KV_WIDTH, n), F32), jax.ShapeDtypeStruct((HEAD_DIM, n), F32)
    return pl.pallas_call(
        _inproj_kernel,
        grid=(n // tm,),
        in_specs=[pl.BlockSpec((tm, D_MODEL), row), pl.BlockSpec((1, D_MODEL), whole),
                  pl.BlockSpec((1, D_MODEL, IN_WIDTH), lambda i: (layer, 0, 0), pipeline_mode=pl.Buffered(1)),
                  pl.BlockSpec((tm, LANES), tab), pl.BlockSpec((tm, LANES), tab), pl.BlockSpec((tm, LANES), tab)],
        out_specs=[pl.BlockSpec((tm, A_WIDTH), row), pl.BlockSpec((tm, 4 * B_WIDTH), row),
                   pl.BlockSpec((tm, IDX_WIDTH), row),
                   kv_spec, kv_spec, ik_spec,
                   pl.BlockSpec((A_KV_HEADS, tm, HEAD_DIM), hmaj), pl.BlockSpec((A_KV_HEADS, tm, LANES), hmaj),
                   pl.BlockSpec((tm, HEAD_DIM), row), pl.BlockSpec((8, tm), tmin)],
        out_shape=[jax.ShapeDtypeStruct((n, A_WIDTH), F32), jax.ShapeDtypeStruct((n, 4 * B_WIDTH), F32),
                   jax.ShapeDtypeStruct((n, IDX_WIDTH), F32),
                   kv_shape, kv_shape, ik_shape,
                   jax.ShapeDtypeStruct((A_KV_HEADS, n, HEAD_DIM), BF16),
                   jax.ShapeDtypeStruct((A_KV_HEADS, n, LANES), BF16),
                   jax.ShapeDtypeStruct((n, HEAD_DIM), BF16), jax.ShapeDtypeStruct((8, n), F32)],
        scratch_shapes=[pltpu.VMEM((D_MODEL, IN_WIDTH_PAD), BF16)],
        compiler_params=_cparams("arbitrary"),
        name="in_proj",
    )(x, g, w_all, *tabs)


def _rope_tables(pos):
    half = ROT_DIM // 2
    inv_freq = 1.0 / (ROPE_THETA ** (jnp.arange(half, dtype=F32) * (2.0 / ROT_DIM)))
    ang = pos.astype(F32)[:, None] * inv_freq[None, :]
    cos, sin = jnp.cos(ang), jnp.sin(ang)
    t = pos.shape[0]
    pad = HEAD_DIM - ROT_DIM
    c = jnp.concatenate([cos, cos, jnp.ones((t, pad), F32)], axis=1)
    s1 = jnp.concatenate([-sin, jnp.zeros((t, half + pad), F32)], axis=1)
    s2 = jnp.concatenate([jnp.zeros((t, half), F32), sin, jnp.zeros((t, pad), F32)], axis=1)
    rep = LANES // HEAD_DIM
    return tuple(jnp.tile(a, (1, rep)) for a in (c, s1, s2))


def _dsa_kernel(q_ref, iq_ref, iw_ref, k_ref, v_ref, ik_ref, o_ref, key_ref, hi_ref, lo_ref, acc_ref, m_ref,
                s0_ref, s1_ref, alpha_ref, p_ref, bias_ref, mt0_ref, mt1_ref, *, l_real, pos0, kb_rows, topk, tq):
    i = pl.program_id(1)
    qpos0 = pos0 + i * tq
    kend = jnp.minimum(((qpos0 + tq - 1) // CHUNK + 1) * CHUNK, l_real)
    cb_rows = 2 * kb_rows
    nkc = (kend + cb_rows - 1) // cb_rows
    nkb = 2 * nkc
    qchunk = (qpos0 + lax.broadcasted_iota(I32, (1, tq), 1)) // CHUNK
    row_iota = lax.broadcasted_iota(I32, (kb_rows, tq), 0)

    iq = iq_ref[...].astype(BF16)
    iq4 = jnp.concatenate([iq[:, h * HEAD_DIM:(h + 1) * HEAD_DIM] for h in range(IDX_HEADS)], axis=0)
    iw = iw_ref[...] * (IDX_HEADS ** -0.5 * HEAD_DIM ** -0.5)

    def score_body(kb, carry):
        r0 = pl.multiple_of(kb * kb_rows, kb_rows)
        ikb = ik_ref[pl.ds(r0, kb_rows), :]
        lg4 = lax.dot_general(ikb, iq4, _NT, preferred_element_type=F32)
        acc = jnp.zeros((kb_rows, tq), F32)
        for h in range(IDX_HEADS):
            acc = acc + iw[h:h + 1, :] * jnp.maximum(lg4[:, h * tq:(h + 1) * tq], 0.0)
        bits = pltpu.bitcast(acc + 0.0, I32)
        key = bits ^ ((bits >> 31) & 0x7FFFFFFF)
        kpos = r0 + row_iota
        adm = (kpos // CHUNK <= qchunk) & (kpos < l_real)
        key = jnp.where(adm, key, INT_MIN)
        key_ref[pl.ds(r0, kb_rows), :] = key
        hi_ref[pl.ds(r0, kb_rows), :] = (key >> 16).astype(I16)
        return carry

    lax.fori_loop(0, nkb, score_body, 0)

    def count(pred):
        def body(kc, c):
            r0 = pl.multiple_of(kc * cb_rows, cb_rows)
            hit = pred(key_ref[pl.ds(r0, cb_rows), :], r0)
            return c + jnp.where(hit, 1, 0).reshape(cb_rows // 8, 8, tq).sum(axis=0)
        c = lax.fori_loop(0, nkc, body, jnp.zeros((8, tq), I32))
        return c.sum(axis=0, keepdims=True)

    def count16(ref, pred):
        one, zero = jnp.ones((), BF16), jnp.zeros((), BF16)

        def body(kc, c):
            r0 = pl.multiple_of(kc * cb_rows, cb_rows)
            w = jnp.where(pred(ref[pl.ds(r0, cb_rows), :]), one, zero).reshape(cb_rows // 16, 16, tq)
            parts = [w[r] for r in range(cb_rows // 16)]
            while len(parts) > 1:
                parts = [a + b for a, b in zip(parts[::2], parts[1::2])] + parts[len(parts) & ~1:]
            return c + parts[0]
        c = lax.fori_loop(0, nkc, body, jnp.zeros((16, tq), BF16))
        return c.astype(F32).sum(axis=0, keepdims=True).astype(I32)

    def bisect16(ref, kth):
        c0 = count16(ref, lambda blk: blk >= jnp.zeros((), I16))
        start = jnp.where(c0 >= kth, 0, -32768).astype(I32)

        def bit_body(it, prefix):
            cand = prefix | lax.shift_left(jnp.int32(1), 14 - it)
            c = count16(ref, lambda blk: blk >= cand.astype(I16))
            return jnp.where(c >= kth, cand, prefix)

        return lax.fori_loop(0, 15, bit_body, start)

    tau_hi = bisect16(hi_ref, topk)
    tau_hi16 = tau_hi.astype(I16)
    kth_lo = topk - count16(hi_ref, lambda blk: blk > tau_hi16)

    def lo_body(kc, carry):
        r0 = pl.multiple_of(kc * cb_rows, cb_rows)
        lo = ((key_ref[pl.ds(r0, cb_rows), :] & 0xFFFF) - 32768).astype(I16)
        lo_ref[pl.ds(r0, cb_rows), :] = jnp.where(hi_ref[pl.ds(r0, cb_rows), :] == tau_hi16, lo,
                                                 jnp.full((), -32768, I16))
        return carry

    lax.fori_loop(0, nkc, lo_body, 0)
    tau_lo = bisect16(lo_ref, kth_lo)
    tau = lax.shift_left(tau_hi, 16) | (tau_lo + 32768)
    n_ge = count(lambda blk, r0: blk >= tau)

    @pl.when(jnp.max(jnp.where((n_ge > topk) & (tau != INT_MIN), 1, 0)) > 0)
    def _():
        need = (topk - count(lambda blk, r0: blk > tau)).astype(F32)
        ri = lax.broadcasted_iota(I32, (kb_rows, kb_rows), 0)
        ci = lax.broadcasted_iota(I32, (kb_rows, kb_rows), 1)
        upto = jnp.where(ci <= ri, 1.0, 0.0).astype(BF16)

        def strike(kc, seen):
            for half in range(2):
                rows = pl.ds(pl.multiple_of(kc * cb_rows + half * kb_rows, kb_rows), kb_rows)
                key = key_ref[rows, :]
                tie = key == tau
                rank = _dot(upto, jnp.where(tie, 1.0, 0.0).astype(BF16)) + seen
                key_ref[rows, :] = jnp.where(tie & (rank > need), INT_MIN, key)
                seen = rank[kb_rows - 1:kb_rows, :]
            return seen

        lax.fori_loop(0, nkc, strike, jnp.zeros((1, tq), F32))

    lane_rep = kb_rows // LANES
    tau_c = jnp.tile(jnp.broadcast_to(tau, (LANES, tq)).T, (1, lane_rep))
    acc_ref[...] = jnp.zeros(acc_ref.shape, F32)
    m_ref[...] = jnp.full(m_ref.shape, NEG_BIG, F32)
    qs = (q_ref[...] * (HEAD_DIM ** -0.5 * LOG2E)).astype(BF16)
    q4 = [jnp.concatenate([qs[:, (j * A_GROUP + g) * HEAD_DIM:(j * A_GROUP + g + 1) * HEAD_DIM]
                           for g in range(A_GROUP)], axis=0) for j in range(A_KV_HEADS)]

    def qk_scores(kb, s_slot, mt_slot):
        r0 = pl.multiple_of(jnp.minimum(kb, nkb - 1) * kb_rows, kb_rows)
        key_t = key_ref[pl.ds(r0, kb_rows), :].T
        bias_ref[...] = jnp.where((key_t >= tau_c) & (key_t != INT_MIN), 0.0, NEG_BIG)
        for j in range(A_KV_HEADS):
            s = lax.dot_general(q4[j], k_ref[j, pl.ds(r0, kb_rows), :], _NT, preferred_element_type=F32)
            for g in range(A_GROUP):
                rows = slice(g * tq, (g + 1) * tq)
                sm = s[rows, :] + bias_ref[...]
                s_slot[j, rows, :] = sm
                mt_slot[j, g] = jnp.broadcast_to(jnp.max(sm, axis=1, keepdims=True), (tq, LANES))

    def softmax_pv(kb, s_slot, mt_slot):
        r0 = pl.multiple_of(kb * kb_rows, kb_rows)
        for j in range(A_KV_HEADS):
            for g in range(A_GROUP):
                rows = slice(g * tq, (g + 1) * tq)
                m_old = m_ref[j, g]
                m_new = jnp.maximum(m_old, mt_slot[j, g])
                alpha_ref[j, g] = jnp.exp2(m_old - m_new)
                m_ref[j, g] = m_new
                p_ref[j, rows, :] = jnp.exp2(s_slot[j, rows, :] - jnp.tile(m_new, (1, lane_rep))).astype(BF16)
            pv = _dot(p_ref[j], v_ref[j, pl.ds(r0, kb_rows), :])
            acc_ref[j] = alpha_ref[j] * acc_ref[j] + pv.reshape(A_GROUP, tq, LANES)

    qk_scores(0, s0_ref, mt0_ref)

    def pair_body(i, carry):
        kb = 2 * i
        qk_scores(kb + 1, s1_ref, mt1_ref)
        softmax_pv(kb, s0_ref, mt0_ref)
        qk_scores(kb + 2, s0_ref, mt0_ref)
        softmax_pv(kb + 1, s1_ref, mt1_ref)
        return carry

    lax.fori_loop(0, nkc, pair_body, 0)

    outs = []
    for j in range(A_KV_HEADS):
        acc = acc_ref[j]
        for g in range(A_GROUP):
            outs.append(acc[g, :, :HEAD_DIM] * (1.0 / acc[g, :, HEAD_DIM:HEAD_DIM + 1]))
    o_ref[...] = jnp.concatenate(outs, axis=1)


def _dsa(q, iq, iw_t, k_hm, v_aug, ik, *, bsz, l_real, pos0, kb_rows, topk, tq):
    t_len = q.shape[0] // bsz
    l_pad = ik.shape[0] // bsz
    nq = t_len // tq
    assert l_pad // 16 <= 256, "packed hit counts are summed in bf16, exact only up to 256 per slot"
    kern = functools.partial(_dsa_kernel, l_real=l_real, pos0=pos0, kb_rows=kb_rows, topk=topk, tq=tq)
    qrow = lambda b, i: (b * nq + i, 0)
    return pl.pallas_call(
        kern,
        grid=(bsz, nq),
        in_specs=[pl.BlockSpec((tq, A_WIDTH), qrow), pl.BlockSpec((tq, IDX_WIDTH), qrow),
                  pl.BlockSpec((8, tq), lambda b, i: (0, b * nq + i)),
                  pl.BlockSpec((A_KV_HEADS, l_pad, HEAD_DIM), lambda b, i: (0, b, 0)),
                  pl.BlockSpec((A_KV_HEADS, l_pad, LANES), lambda b, i: (0, b, 0)),
                  pl.BlockSpec((l_pad, HEAD_DIM), lambda b, i: (b, 0))],
        out_specs=pl.BlockSpec((tq, A_WIDTH), qrow),
        out_shape=jax.ShapeDtypeStruct((bsz * t_len, A_WIDTH), F32),
        scratch_shapes=[pltpu.VMEM((l_pad, tq), I32), pltpu.VMEM((l_pad, tq), I16), pltpu.VMEM((l_pad, tq), I16),
                        pltpu.VMEM((A_KV_HEADS, A_GROUP, tq, LANES), F32),
                        pltpu.VMEM((A_KV_HEADS, A_GROUP, tq, LANES), F32),
                        pltpu.VMEM((A_KV_HEADS, A_GROUP * tq, kb_rows), F32),
                        pltpu.VMEM((A_KV_HEADS, A_GROUP * tq, kb_rows), F32),
                        pltpu.VMEM((A_KV_HEADS, A_GROUP, tq, LANES), F32),
                        pltpu.VMEM((A_KV_HEADS, A_GROUP * tq, kb_rows), BF16),
                        pltpu.VMEM((tq, kb_rows), F32),
                        pltpu.VMEM((A_KV_HEADS, A_GROUP, tq, LANES), F32),
                        pltpu.VMEM((A_KV_HEADS, A_GROUP, tq, LANES), F32)],
        compiler_params=_cparams("parallel", "arbitrary"),
        name="dsa",
    )(q, iq, iw_t, k_hm, v_aug, ik)


def _hgrn_kernel(bq_ref, bf_ref, bi_ref, bg_ref, lb_ref, gn_ref, s0_ref, o_ref, sfin_ref,
                 st_ref, q_s, kk_s, b_s, qe_s, kd_s, dec_s, o_s, e_s, *, layer, tb, nseq):
    tstep = pl.program_id(1)
    nchunk = tb // B_CHUNK

    @pl.when(tstep == 0)
    def _():
        st_ref[...] = s0_ref[...]

    lbr = lb_ref[...]
    lbe = jnp.exp(lbr - jnp.max(lbr, axis=0, keepdims=True))
    lbs = lbe / jnp.sum(lbe, axis=0, keepdims=True)
    cum = lbs[0:1, :]
    first = cum
    for r in range(1, layer + 1):
        cum = cum + lbs[r:r + 1, :]
    lb = cum - first

    ri = lax.broadcasted_iota(I32, (tb, tb), 0)
    ci = lax.broadcasted_iota(I32, (tb, tb), 1)
    same = (ri // B_CHUNK) == (ci // B_CHUNK)
    tri = jnp.where(same & (ci <= ri), 1.0, 0.0).astype(BF16)
    ones = jnp.where(same, 1.0, 0.0).astype(BF16)
    for u in range(nseq):
        f = lb + (1.0 - lb) * _sigmoid(bf_ref[u])
        bq = bq_ref[u]
        q = bq * _sigmoid(bq) * (HEAD_DIM ** -0.5)
        g3 = _split3(jnp.log(f))
        b = _dot(tri, g3[0]) + _dot(tri, g3[1]) + _dot(tri, g3[2])
        blast = _dot(ones, g3[0]) + _dot(ones, g3[1]) + _dot(ones, g3[2])
        q_s[u] = q
        kk_s[u] = 1.0 - f
        b_s[u] = b
        qe_s[u] = q * jnp.exp(b)
        kd_s[u] = (1.0 - f) * jnp.exp(blast - b)
        dec_s[u] = jnp.exp(blast)

    hi = lax.broadcasted_iota(I32, (HALF, HALF), 0) // HEAD_DIM
    hj = lax.broadcasted_iota(I32, (HALF, HALF), 1) // HEAD_DIM
    bd_mask = hi == hj
    bd = jnp.where(bd_mask, 1.0, 0.0).astype(BF16)
    t_iota = lax.broadcasted_iota(I32, (B_CHUNK, B_WIDTH), 0)

    def seg_sum(x, two_pass):
        out = []
        for half in range(2):
            xs = x[:, half * HALF:(half + 1) * HALF]
            if two_pass:
                xh, xl = _split2(xs)
                out.append(_dot(xh, bd) + _dot(xl, bd))
            else:
                out.append(_dot(xs.astype(BF16), bd))
        return jnp.concatenate(out, axis=1)

    def chunk_body(c, carry):
        r0 = pl.multiple_of(c * B_CHUNK, B_CHUNK)
        rows = pl.ds(r0, B_CHUNK)
        for u in range(nseq):
            qc, kc, bc, vc = q_s[u, rows, :], kk_s[u, rows, :], b_s[u, rows, :], bi_ref[u, rows, :]
            qe, kd = qe_s[u, rows, :].astype(BF16), kd_s[u, rows, :].astype(BF16)
            dec = dec_s[u, pl.ds(r0, 1), :]
            for s in range(B_CHUNK):
                ex = jnp.exp(jnp.where(t_iota >= s, bc - bc[s:s + 1, :], NEG_BIG))
                e_s[u, s * B_CHUNK:(s + 1) * B_CHUNK, :] = qc * kc[s:s + 1, :] * ex
            aexp = seg_sum(e_s[u], False)
            o = jnp.zeros((B_CHUNK, B_WIDTH), F32)
            for s in range(B_CHUNK):
                o = o + aexp[s * B_CHUNK:(s + 1) * B_CHUNK, :] * vc[s:s + 1, :]
            inter = []
            vb = vc.astype(BF16)
            for half in range(2):
                cols = slice(half * HALF, (half + 1) * HALF)
                st = st_ref[u, half]
                inter.append(lax.dot_general(qe[:, cols], st.astype(BF16), _NT, preferred_element_type=F32))
                upd = lax.dot_general(vb[:, cols], kd[:, cols], _TN, preferred_element_type=F32)
                st_ref[u, half] = st * dec[:, cols] + jnp.where(bd_mask, upd, 0.0)
            o_s[u, rows, :] = o + jnp.concatenate(inter, axis=1)
        return carry

    lax.fori_loop(0, nchunk, chunk_body, 0)

    for u in range(nseq):
        o = o_s[u]
        ms = seg_sum(o * o, True) * (1.0 / HEAD_DIM)
        bg = bg_ref[u]
        o_ref[u] = o * lax.rsqrt(ms + EPS) * gn_ref[...] * (bg * _sigmoid(bg))

    @pl.when(tstep == pl.num_programs(1) - 1)
    def _():
        sfin_ref[...] = st_ref[...]


def _hgrn(hb, lb_raw, gn_tiled, s0_bd, *, layer, bsz, t_len, tb):
    nt = t_len // tb
    nseq = 2 if bsz % 2 == 0 else 1
    hb3 = hb.reshape(bsz, t_len, 4 * B_WIDTH)
    col = lambda j: (lambda b, t: (b, t, j))
    scr = lambda r: pltpu.VMEM((nseq, r, B_WIDTH), F32)
    kern = functools.partial(_hgrn_kernel, layer=layer, tb=tb, nseq=nseq)
    blk = (nseq, tb, B_WIDTH)
    out, s_fin = pl.pallas_call(
        kern,
        grid=(bsz // nseq, nt),
        in_specs=[pl.BlockSpec(blk, col(0)), pl.BlockSpec(blk, col(1)), pl.BlockSpec(blk, col(2)),
                  pl.BlockSpec(blk, col(3)),
                  pl.BlockSpec(lb_raw.shape, lambda b, t: (0, 0)),
                  pl.BlockSpec((1, B_WIDTH), lambda b, t: (0, 0)),
                  pl.BlockSpec((nseq, 2, HALF, HALF), lambda b, t: (b, 0, 0, 0))],
        out_specs=[pl.BlockSpec(blk, lambda b, t: (b, t, 0)),
                   pl.BlockSpec((nseq, 2, HALF, HALF), lambda b, t: (b, 0, 0, 0))],
        out_shape=[jax.ShapeDtypeStruct((bsz, t_len, B_WIDTH), F32),
                   jax.ShapeDtypeStruct((bsz, 2, HALF, HALF), F32)],
        scratch_shapes=[pltpu.VMEM((nseq, 2, HALF, HALF), F32), scr(tb), scr(tb), scr(tb), scr(tb), scr(tb),
                        scr(tb), scr(tb), scr(B_CHUNK * B_CHUNK)],
        compiler_params=_cparams("parallel", "arbitrary"),
        name="hgrn2",
    )(hb3, hb3, hb3, hb3, lb_raw, gn_tiled, s0_bd)
    return out.reshape(bsz * t_len, B_WIDTH), s_fin


def _state_to_tiles(s):
    bsz = s.shape[0]
    st = jnp.swapaxes(s, -1, -2).reshape(bsz, 2, B_HEADS // 2, HEAD_DIM, HEAD_DIM)
    eye = jnp.eye(B_HEADS // 2, dtype=s.dtype)
    t = st[:, :, :, :, None, :] * eye[None, None, :, None, :, None]
    return t.reshape(bsz, 2, HALF, HALF)


def _tiles_to_state(t):
    bsz = t.shape[0]
    t6 = t.reshape(bsz, 2, B_HEADS // 2, HEAD_DIM, B_HEADS // 2, HEAD_DIM)
    diag = jnp.stack([t6[:, :, h, :, h, :] for h in range(B_HEADS // 2)], axis=2)
    return jnp.swapaxes(diag.reshape(bsz, B_HEADS, HEAD_DIM, HEAD_DIM), -1, -2)


def _outproj_kernel(a_ref, b_ref, wa_ref, wb_ref, x_ref, o_ref):
    o_ref[...] = (x_ref[...] + _dot(a_ref[...].astype(BF16), wa_ref[...])
                  + _dot(b_ref[...].astype(BF16), wb_ref[...]))


def _out_proj(a, b, wa, wb, x, tm):
    n = x.shape[0]
    row = lambda i: (i, 0)
    whole = lambda i: (0, 0)
    return pl.pallas_call(
        _outproj_kernel,
        grid=(n // tm,),
        in_specs=[pl.BlockSpec((tm, A_WIDTH), row), pl.BlockSpec((tm, B_WIDTH), row),
                  pl.BlockSpec((A_WIDTH, D_MODEL), whole), pl.BlockSpec((B_WIDTH, D_MODEL), whole),
                  pl.BlockSpec((tm, D_MODEL), row)],
        out_specs=pl.BlockSpec((tm, D_MODEL), row),
        out_shape=jax.ShapeDtypeStruct((n, D_MODEL), F32),
        compiler_params=_cparams("parallel"),
        name="out_proj",
    )(a, b, wa, wb, x)


def _normmm_kernel(x_ref, g_ref, w_ref, o_ref):
    o_ref[...] = _dot(_rms(x_ref[...], g_ref[...]).astype(BF16), w_ref[...])


def _norm_matmul(x, g, w, tm):
    n, m = x.shape[0], w.shape[1]
    return pl.pallas_call(
        _normmm_kernel,
        grid=(n // tm,),
        in_specs=[pl.BlockSpec((tm, D_MODEL), lambda i: (i, 0)), pl.BlockSpec((1, D_MODEL), lambda i: (0, 0)),
                  pl.BlockSpec((D_MODEL, m), lambda i: (0, 0))],
        out_specs=pl.BlockSpec((tm, m), lambda i: (i, 0)),
        out_shape=jax.ShapeDtypeStruct((n, m), F32),
        compiler_params=_cparams("parallel"),
        name="norm_matmul",
    )(x, g, w)


def _memattn_kernel(x_ref, g_ref, wq_ref, mk_ref, mv_ref, wo_ref, o_ref):
    x = x_ref[...]
    q = _dot(_rms(x, g_ref[...]).astype(BF16), wq_ref[...])
    heads = []
    for h in range(MEM_HEADS):
        cols = slice(h * MEM_HEAD_DIM, (h + 1) * MEM_HEAD_DIM)
        s = lax.dot_general(q[:, cols].astype(BF16), mk_ref[0, :, cols], _NT,
                            preferred_element_type=F32) * (MEM_HEAD_DIM ** -0.5)
        p = jnp.exp(s - jnp.max(s, axis=-1, keepdims=True))
        p = p * (1.0 / jnp.sum(p, axis=-1, keepdims=True))
        heads.append(_dot(p.astype(BF16), mv_ref[0, :, cols]).astype(BF16))
    o_ref[...] = x + _dot(jnp.concatenate(heads, axis=1), wo_ref[...])


def _mem_attn(x, g, wq, mk, mv, wo, rows_per_batch, tm):
    n = x.shape[0]
    n_mem = mk.shape[1]
    per = rows_per_batch // tm
    row = lambda i: (i, 0)
    whole = lambda i: (0, 0)
    mem = lambda i: (i // per, 0, 0)
    return pl.pallas_call(
        _memattn_kernel,
        grid=(n // tm,),
        in_specs=[pl.BlockSpec((tm, D_MODEL), row), pl.BlockSpec((1, D_MODEL), whole),
                  pl.BlockSpec((D_MODEL, D_MODEL), whole),
                  pl.BlockSpec((1, n_mem, D_MODEL), mem), pl.BlockSpec((1, n_mem, D_MODEL), mem),
                  pl.BlockSpec((D_MODEL, D_MODEL), whole)],
        out_specs=pl.BlockSpec((tm, D_MODEL), row),
        out_shape=jax.ShapeDtypeStruct((n, D_MODEL), F32),
        compiler_params=_cparams("parallel"),
        name="mem_attn",
    )(x, g, wq, mk, mv, wo)


def _swiglu_mid(xn, w1, w3):
    a = _dot(xn, w1)
    return (a * _sigmoid(a) * _dot(xn, w3)).astype(BF16)


def _ffn_kernel(x_ref, g_ref, w1_ref, w3_ref, w2_ref, gf_ref, o_ref, xn_ref, *, final):
    j = pl.program_id(1)

    @pl.when(j == 0)
    def _():
        x = x_ref[...]
        xn_ref[...] = _rms(x, g_ref[...]).astype(BF16)
        o_ref[...] = x

    o_ref[...] += _dot(_swiglu_mid(xn_ref[...], w1_ref[...], w3_ref[...]), w2_ref[...])

    if final:
        @pl.when(j == pl.num_programs(1) - 1)
        def _():
            o_ref[...] = _rms(o_ref[...], gf_ref[...])


def _ffn(x, g, w1, w3, w2, gf, *, final, tm, tf):
    n = x.shape[0]
    d_ff = w1.shape[1]
    row = lambda i, j: (i, 0)
    whole = lambda i, j: (0, 0)
    return pl.pallas_call(
        functools.partial(_ffn_kernel, final=final),
        grid=(n // tm, d_ff // tf),
        in_specs=[pl.BlockSpec((tm, D_MODEL), row), pl.BlockSpec((1, D_MODEL), whole),
                  pl.BlockSpec((D_MODEL, tf), lambda i, j: (0, j)), pl.BlockSpec((D_MODEL, tf), lambda i, j: (0, j)),
                  pl.BlockSpec((tf, D_MODEL), lambda i, j: (j, 0)), pl.BlockSpec((1, D_MODEL), whole)],
        out_specs=pl.BlockSpec((tm, D_MODEL), row),
        out_shape=jax.ShapeDtypeStruct((n, D_MODEL), F32),
        scratch_shapes=[pltpu.VMEM((tm, D_MODEL), BF16)],
        compiler_params=_cparams("parallel", "arbitrary"),
        name="ffn",
    )(x, g, w1, w3, w2, gf)


def _router_kernel(x_ref, g_ref, rt_ref, comb_ref, sel_ref):
    xn = _rms(x_ref[...], g_ref[...])
    logits = lax.dot_general(rt_ref[...], xn, _NT, preferred_element_type=F32,
                             precision=lax.Precision.HIGHEST)
    e_idx = lax.broadcasted_iota(I32, logits.shape, 0)
    m1 = jnp.max(logits, axis=0, keepdims=True)
    i1 = jnp.min(jnp.where(logits == m1, e_idx, N_EXPERTS), axis=0, keepdims=True)
    rest = jnp.where(e_idx == i1, -jnp.inf, logits)
    m2 = jnp.max(rest, axis=0, keepdims=True)
    i2 = jnp.min(jnp.where(rest == m2, e_idx, N_EXPERTS), axis=0, keepdims=True)
    e2 = jnp.exp(m2 - m1)
    g1 = 1.0 / (1.0 + e2)
    comb_ref[...] = jnp.where(e_idx == i1, g1, 0.0) + jnp.where(e_idx == i2, e2 * g1, 0.0)
    sel_ref[...] = jnp.where((e_idx == i1) | (e_idx == i2), 1.0, 0.0)


def _router(x, g, rt, tm):
    n = x.shape[0]
    return pl.pallas_call(
        _router_kernel,
        grid=(n // tm,),
        in_specs=[pl.BlockSpec((tm, D_MODEL), lambda i: (i, 0)), pl.BlockSpec((1, D_MODEL), lambda i: (0, 0)),
                  pl.BlockSpec((N_EXPERTS, D_MODEL), lambda i: (0, 0))],
        out_specs=[pl.BlockSpec((N_EXPERTS, tm), lambda i: (0, i)), pl.BlockSpec((N_EXPERTS, tm), lambda i: (0, i))],
        out_shape=[jax.ShapeDtypeStruct((N_EXPERTS, n), F32), jax.ShapeDtypeStruct((N_EXPERTS, n), F32)],
        compiler_params=_cparams("parallel"),
        name="moe_router",
    )(x, g, rt)


def _moe_kernel(x_ref, g_ref, selt_ref, sel_ref, comb_ref, w1_ref, w3_ref, w2_ref, gf_ref, o_ref,
                xn_ref, posr_ref, posc_ref, cnt_ref, xg_ref, yg_ref, *, final, nb, sub):
    e = pl.program_id(1)
    j = pl.program_id(2)
    last_j = pl.num_programs(2) - 1

    @pl.when((e == 0) & (j == 0))
    def _():
        x = x_ref[...]
        xn_ref[...] = _rms(x, g_ref[...]).astype(BF16)
        o_ref[...] = x
        cc = min(nb, 256)
        ri = lax.broadcasted_iota(I32, (cc, cc), 0)
        ci = lax.broadcasted_iota(I32, (cc, cc), 1)
        before_r = jnp.where(ri < ci, 1.0, 0.0).astype(BF16)
        before_c = jnp.where(ci < ri, 1.0, 0.0).astype(BF16)
        run_r = jnp.zeros((N_EXPERTS, 1), F32)
        run_c = jnp.zeros((1, N_EXPERTS), F32)
        for c in range(nb // cc):
            cols = slice(c * cc, (c + 1) * cc)
            mr = selt_ref[:, cols]
            posr_ref[:, cols] = jnp.where(mr > 0.0, _dot(mr.astype(BF16), before_r) + run_r, -1.0)
            run_r = run_r + jnp.sum(mr, axis=1, keepdims=True)
            mc = sel_ref[cols, :]
            posc_ref[cols, :] = jnp.where(mc > 0.0, _dot(before_c, mc.astype(BF16)) + run_c, -1.0)
            run_c = run_c + jnp.sum(mc, axis=0, keepdims=True)
        for k in range(N_EXPERTS):
            cnt_ref[k] = run_c[0, k].astype(I32)

    cnt = cnt_ref[e]
    nfull = cnt // sub
    rem = cnt - nfull * sub
    tail_sizes = [m for m in (64, 128) if m < sub] + [sub]

    def for_tiles(fn):
        def body(s, carry):
            fn(pl.multiple_of(s * sub, sub), sub)
            return carry
        lax.fori_loop(0, nfull, body, 0)
        r0 = pl.multiple_of(nfull * sub, sub)
        lo = 0
        for m in tail_sizes:
            @pl.when((rem > lo) & (rem <= m))
            def _():
                fn(r0, m)
            lo = m

    def gather_matrix(r0, m):
        slot_r = posr_ref[pl.ds(e, 1), :]
        want = (r0 + lax.broadcasted_iota(I32, (m, nb), 0)).astype(F32)
        return jnp.where(slot_r == want, 1.0, 0.0).astype(BF16)

    def gather(r0, m):
        xg_ref[pl.ds(r0, m), :] = _dot(gather_matrix(r0, m), xn_ref[...]).astype(BF16)

    def ffn(r0, m):
        rows = pl.ds(r0, m)
        part = _dot(_swiglu_mid(xg_ref[rows, :], w1_ref[0], w3_ref[0]), w2_ref[0])

        @pl.when(j == 0)
        def _():
            yg_ref[rows, :] = part

        @pl.when(j > 0)
        def _():
            yg_ref[rows, :] += part

    @pl.when(j == 0)
    def _():
        for_tiles(gather)

    for_tiles(ffn)

    @pl.when(j == last_j)
    def _():
        c3 = _split3(comb_ref[...])
        lane8 = lax.broadcasted_iota(I32, (nb, N_EXPERTS), 1)
        slot_c = jnp.sum(jnp.where(lane8 == e, posc_ref[...], 0.0), axis=1, keepdims=True)

        def scatter(r0, m):
            pm = gather_matrix(r0, m)
            gates = _dot(pm, c3[0]) + _dot(pm, c3[1]) + _dot(pm, c3[2])
            lane_e = lax.broadcasted_iota(I32, (m, N_EXPERTS), 1) == e
            gate = jnp.sum(jnp.where(lane_e, gates, 0.0), axis=1, keepdims=True)
            y = (yg_ref[pl.ds(r0, m), :] * gate).astype(BF16)
            want = (r0 + lax.broadcasted_iota(I32, (nb, m), 1)).astype(F32)
            o_ref[...] += _dot(jnp.where(slot_c == want, 1.0, 0.0).astype(BF16), y)

        for_tiles(scatter)

        if final:
            @pl.when(e == pl.num_programs(1) - 1)
            def _():
                o_ref[...] = _rms(o_ref[...], gf_ref[...])


def _moe(x, g, sel_t, sel, comb, w1, w3, w2, gf, *, final, nb, tf):
    n = x.shape[0]
    d_ff = w1.shape[2]
    sub = min(nb, MOE_TILE_ROWS)
    row = lambda i, e, j: (i, 0)
    whole = lambda i, e, j: (0, 0)
    return pl.pallas_call(
        functools.partial(_moe_kernel, final=final, nb=nb, sub=sub),
        grid=(n // nb, N_EXPERTS, d_ff // tf),
        in_specs=[pl.BlockSpec((nb, D_MODEL), row), pl.BlockSpec((1, D_MODEL), whole),
                  pl.BlockSpec((N_EXPERTS, nb), lambda i, e, j: (0, i)),
                  pl.BlockSpec((nb, N_EXPERTS), row), pl.BlockSpec((nb, N_EXPERTS), row),
                  pl.BlockSpec((1, D_MODEL, tf), lambda i, e, j: (e, 0, j)),
                  pl.BlockSpec((1, D_MODEL, tf), lambda i, e, j: (e, 0, j)),
                  pl.BlockSpec((1, tf, D_MODEL), lambda i, e, j: (e, j, 0)), pl.BlockSpec((1, D_MODEL), whole)],
        out_specs=pl.BlockSpec((nb, D_MODEL), row),
        out_shape=jax.ShapeDtypeStruct((n, D_MODEL), F32),
        scratch_shapes=[pltpu.VMEM((nb, D_MODEL), BF16), pltpu.VMEM((N_EXPERTS, nb), F32),
                        pltpu.VMEM((nb, N_EXPERTS), F32), pltpu.SMEM((N_EXPERTS,), I32),
                        pltpu.VMEM((nb + sub, D_MODEL), BF16), pltpu.VMEM((nb + sub, D_MODEL), F32)],
        compiler_params=_cparams("parallel", "arbitrary", "arbitrary"),
        name="moe",
    )(x, g, sel_t, sel, comb, w1, w3, w2, gf)


def _rmsnorm_kernel(x_ref, g_ref, o_ref):
    o_ref[...] = _rms(x_ref[...], g_ref[...])


def _rmsnorm(x, g, tm):
    n = x.shape[0]
    return pl.pallas_call(
        _rmsnorm_kernel,
        grid=(n // tm,),
        in_specs=[pl.BlockSpec((tm, D_MODEL), lambda i: (i, 0)), pl.BlockSpec((1, D_MODEL), lambda i: (0, 0))],
        out_specs=pl.BlockSpec((tm, D_MODEL), lambda i: (i, 0)),
        out_shape=jax.ShapeDtypeStruct((n, D_MODEL), F32),
        compiler_params=_cparams("parallel"),
        name="final_norm",
    )(x, g)


def _row_tile(n, cap):
    t = min(n, cap)
    assert n % t == 0, (n, cap)
    return t


def _ff_tile(d_ff):
    for parts in (2, 4, 11, 22):
        if d_ff % parts == 0 and (d_ff // parts) % LANES == 0:
            return d_ff // parts
    return d_ff


def _mixer(x, tabs, w, *, bsz, t_len, pos0, past, s0, layer):
    n = bsz * t_len
    qa, hb, iq, k_t, v_t, ik_t, k_hm, v_aug, ik_b, iw_t = _in_proj(x, w["norm_mix"], w["w_in"], layer, tabs,
                                                                   t_len, _row_tile(n, 512))
    l_real = t_len if past is None else past[0].shape[1] + t_len
    topk = min(TOPK_MAX, l_real // 4)
    tq = 2 * LANES if t_len % (2 * LANES) == 0 else LANES
    kb_rows = 256 if l_real % 256 == 0 else LANES
    l_pad = -(-l_real // (2 * kb_rows)) * (2 * kb_rows)
    t_pad = -(-t_len // tq) * tq
    if past is not None:
        pk, pv, pik = past
        one = jnp.ones(pv.shape[:2] + (A_KV_HEADS, 1), BF16)
        zero = jnp.zeros(pv.shape[:2] + (A_KV_HEADS, LANES - HEAD_DIM - 1), BF16)
        pv_aug = jnp.concatenate([pv.astype(BF16), one, zero], axis=-1)
        hm = lambda a: jnp.transpose(a, (2, 0, 1, 3))
        k_hm = jnp.concatenate([hm(pk.astype(BF16)), k_hm.reshape(A_KV_HEADS, bsz, t_len, HEAD_DIM)], axis=2)
        v_aug = jnp.concatenate([hm(pv_aug), v_aug.reshape(A_KV_HEADS, bsz, t_len, LANES)], axis=2)
        ik_b = jnp.concatenate([pik.astype(BF16), ik_b.reshape(bsz, t_len, HEAD_DIM)], axis=1)
    if l_pad != l_real or past is not None:
        padl = l_pad - l_real
        k_hm = jnp.pad(k_hm.reshape(A_KV_HEADS, bsz, l_real, HEAD_DIM), ((0, 0), (0, 0), (0, padl), (0, 0)))
        v_aug = jnp.pad(v_aug.reshape(A_KV_HEADS, bsz, l_real, LANES), ((0, 0), (0, 0), (0, padl), (0, 0)))
        ik_b = jnp.pad(ik_b.reshape(bsz, l_real, HEAD_DIM), ((0, 0), (0, padl), (0, 0)))
        k_hm = k_hm.reshape(A_KV_HEADS, bsz * l_pad, HEAD_DIM)
        v_aug = v_aug.reshape(A_KV_HEADS, bsz * l_pad, LANES)
        ik_b = ik_b.reshape(bsz * l_pad, HEAD_DIM)
    q_in, iq_in, iw_in = qa, iq, iw_t
    if t_pad != t_len:
        padq = lambda a: jnp.pad(a.reshape(bsz, t_len, -1),
                                 ((0, 0), (0, t_pad - t_len), (0, 0))).reshape(bsz * t_pad, -1)
        q_in, iq_in = padq(qa), padq(iq)
        iw_in = jnp.pad(iw_t.reshape(8, bsz, t_len), ((0, 0), (0, 0), (0, t_pad - t_len))).reshape(8, bsz * t_pad)
    oa = _dsa(q_in, iq_in, iw_in, k_hm, v_aug, ik_b, bsz=bsz, l_real=l_real, pos0=pos0, kb_rows=kb_rows,
              topk=topk, tq=tq)
    if t_pad != t_len:
        oa = oa.reshape(bsz, t_pad, A_WIDTH)[:, :t_len].reshape(n, A_WIDTH)
    ob, s_tiles = _hgrn(hb, w["hgrn_lb"], w["gnorm"], _state_to_tiles(s0), layer=layer, bsz=bsz, t_len=t_len,
                        tb=_row_tile(t_len, 256))
    x1 = _out_proj(oa, ob, w["w_out_a"], w["w_out_b"], x, _row_tile(n, 512))
    return x1, k_t, v_t, ik_t, _tiles_to_state(s_tiles)


def _from_time_minor(a, bsz, t_len, heads):
    if a.ndim == 2:
        a = a.reshape((-1, HEAD_DIM, bsz, t_len))
        a = jnp.transpose(a, (2, 3, 0, 1))
    else:
        a = jnp.moveaxis(a, -1, 1)
        a = a.reshape(bsz, t_len, -1, HEAD_DIM)
    return a if heads is not None else a.reshape(bsz, t_len, HEAD_DIM)


def kernel(x_prompt, x_sample, cache_k, cache_v, cache_idx_k, state_hgrn, cache_mem_k, cache_mem_v, mem_prompt,
           norm_mix, w_in, hgrn_lb, hgrn_gnorm, w_out, norm_mem, norm_memkv, w_mq, w_mk, w_mv, w_mo,
           norm_ffn, ffn_w1, ffn_w3, ffn_w2, moe_router, moe_w1, moe_w3, moe_w2, norm_final):
    depth = w_in.shape[0]
    bp, tp, _ = x_prompt.shape
    bs, ts, _ = x_sample.shape
    past = cache_k.shape[2]
    n_mem = mem_prompt.shape[1]
    tabs_p = _rope_tables(jnp.arange(tp, dtype=I32))
    tabs_s = tuple(jnp.tile(a, (bs, 1)) for a in _rope_tables(past + jnp.arange(ts, dtype=I32)))
    xp = x_prompt.reshape(bp * tp, D_MODEL)
    xs = x_sample.reshape(bs * ts, D_MODEL)
    mem = mem_prompt.reshape(bp * n_mem, D_MODEL)
    s0_p = jnp.zeros((bp, B_HEADS, HEAD_DIM, HEAD_DIM), F32)
    gfin = norm_final.reshape(1, D_MODEL)
    row2 = lambda a: a.reshape(1, -1)
    outs = {k: [] for k in ("kp", "vp", "ikp", "sp", "mkp", "mvp", "ks", "vs", "iks", "ss")}
    for l in range(depth):
        w = {"norm_mix": row2(norm_mix[l]), "w_in": w_in, "hgrn_lb": hgrn_lb.astype(F32),
             "gnorm": row2(jnp.tile(hgrn_gnorm[l], B_HEADS)),
             "w_out_a": w_out[l, :A_WIDTH].astype(BF16), "w_out_b": w_out[l, A_WIDTH:].astype(BF16)}
        xp, kp, vp, ikp, sp = _mixer(xp, tabs_p, w, bsz=bp, t_len=tp, pos0=0, past=None, s0=s0_p, layer=l)
        xs, ks, vs, iks, ss = _mixer(xs, tabs_s, w, bsz=bs, t_len=ts, pos0=past,
                                     past=(cache_k[l], cache_v[l], cache_idx_k[l]), s0=state_hgrn[l], layer=l)
        wkv = jnp.concatenate([w_mk[l], w_mv[l]], axis=1).astype(BF16)
        mkv = _norm_matmul(mem, row2(norm_memkv[l]), wkv, _row_tile(bp * n_mem, 256))
        mk_p = mkv[:, :D_MODEL].reshape(bp, n_mem, D_MODEL)
        mv_p = mkv[:, D_MODEL:].reshape(bp, n_mem, D_MODEL)
        wq, wo, gm = w_mq[l].astype(BF16), w_mo[l].astype(BF16), row2(norm_mem[l])
        xp = _mem_attn(xp, gm, wq, mk_p.astype(BF16), mv_p.astype(BF16), wo, tp, _row_tile(tp, 512))
        xs = _mem_attn(xs, gm, wq, cache_mem_k[l].reshape(bs, n_mem, D_MODEL).astype(BF16),
                       cache_mem_v[l].reshape(bs, n_mem, D_MODEL).astype(BF16), wo, ts, _row_tile(ts, 256))
        final = l == depth - 1
        gf = row2(norm_ffn[l])
        if l % 2 == 0:
            w1, w3, w2 = ffn_w1[l // 2].astype(BF16), ffn_w3[l // 2].astype(BF16), ffn_w2[l // 2].astype(BF16)
            tf = _ff_tile(w1.shape[1])
            xp = _ffn(xp, gf, w1, w3, w2, gfin, final=final, tm=_row_tile(bp * tp, 512), tf=tf)
            xs = _ffn(xs, gf, w1, w3, w2, gfin, final=final, tm=_row_tile(bs * ts, 512), tf=tf)
        else:
            w1, w3, w2 = moe_w1[l // 2].astype(BF16), moe_w3[l // 2].astype(BF16), moe_w2[l // 2].astype(BF16)
            rt = moe_router[l // 2].T
            tf = _ff_tile(w1.shape[2])
            comb_p, sel_p = _router(xp, gf, rt, _row_tile(bp * tp, 512))
            comb_s, sel_s = _router(xs, gf, rt, _row_tile(bs * ts, 512))
            xp = _moe(xp, gf, sel_p, sel_p.T, comb_p.T, w1, w3, w2, gfin, final=final,
                      nb=_row_tile(bp * tp, 1024), tf=tf)
            xs = _moe(xs, gf, sel_s, sel_s.T, comb_s.T, w1, w3, w2, gfin, final=final,
                      nb=_row_tile(bs * ts, 1024), tf=tf)
        outs["kp"].append(_from_time_minor(kp, bp, tp, A_KV_HEADS))
        outs["vp"].append(_from_time_minor(vp, bp, tp, A_KV_HEADS))
        outs["ikp"].append(_from_time_minor(ikp, bp, tp, None)); outs["sp"].append(sp)
        outs["mkp"].append(mk_p.reshape(bp, n_mem, MEM_HEADS, MEM_HEAD_DIM))
        outs["mvp"].append(mv_p.reshape(bp, n_mem, MEM_HEADS, MEM_HEAD_DIM))
        outs["ks"].append(_from_time_minor(ks, bs, ts, A_KV_HEADS))
        outs["vs"].append(_from_time_minor(vs, bs, ts, A_KV_HEADS))
        outs["iks"].append(_from_time_minor(iks, bs, ts, None)); outs["ss"].append(ss)
    y_prompt = xp.reshape(bp, tp, D_MODEL)
    y_sample = xs.reshape(bs, ts, D_MODEL)
    st = lambda k: jnp.stack(outs[k])
    return (y_prompt, y_sample, st("kp"), st("vp"), st("ikp"), st("sp"), st("mkp"), st("mvp"),
            st("ks"), st("vs"), st("iks"), st("ss"))
```

```python
import functools

import numpy as np
import jax
import jax.numpy as jnp
from jax import lax
from jax.experimental import pallas as pl
from jax.experimental.pallas import tpu as pltpu

F32 = jnp.float32
BF16 = jnp.bfloat16
I32 = jnp.int32
I16 = jnp.int16

D_MODEL = 1024
CHUNK = 64
EPS = 1e-6
ROPE_THETA = 500000.0
HEAD_DIM = 64
ROT_DIM = HEAD_DIM // 4
A_HEADS = 8
A_KV_HEADS = 2
A_GROUP = A_HEADS // A_KV_HEADS
A_WIDTH = A_HEADS * HEAD_DIM
KV_WIDTH = A_KV_HEADS * HEAD_DIM
IDX_HEADS = 4
IDX_WIDTH = IDX_HEADS * HEAD_DIM
TOPK_MAX = 256
B_HEADS = 8
B_WIDTH = B_HEADS * HEAD_DIM
B_CHUNK = 16
MEM_HEADS = 4
MEM_HEAD_DIM = D_MODEL // MEM_HEADS
N_EXPERTS = 8
MOE_TILE_ROWS = 256
LANES = 128
HALF = 256
IN_SPLITS = (A_WIDTH, KV_WIDTH, KV_WIDTH, IDX_WIDTH, HEAD_DIM, IDX_HEADS, B_WIDTH, B_WIDTH, B_WIDTH, B_WIDTH)
IN_WIDTH = sum(IN_SPLITS)
IN_WIDTH_PAD = 3200
VMEM_LIMIT = 56 * 1024 * 1024
INT_MIN = -2147483648
NEG_BIG = -1e30
LOG2E = 1.4426950408889634

_NT = (((1,), (1,)), ((), ()))
_TN = (((0,), (0,)), ((), ()))


def _cparams(*sem):
    return pltpu.CompilerParams(dimension_semantics=sem, vmem_limit_bytes=VMEM_LIMIT)


def _rms(x, g):
    return x * lax.rsqrt(jnp.mean(x * x, axis=-1, keepdims=True) + EPS) * g


def _sigmoid(x):
    return 1.0 / (1.0 + jnp.exp(-x))


def _dot(a, b):
    return jnp.dot(a, b, preferred_element_type=F32)


def _split2(x):
    hi = x.astype(BF16)
    lo = (x - hi.astype(F32)).astype(BF16)
    return hi, lo


def _split3(x):
    hi = x.astype(BF16)
    r = x - hi.astype(F32)
    mid = r.astype(BF16)
    lo = (r - mid.astype(F32)).astype(BF16)
    return hi, mid, lo


def _rope128(xb, c, s1, s2):
    return xb * c + pltpu.roll(xb, LANES - ROT_DIM // 2, 1) * s1 + pltpu.roll(xb, ROT_DIM // 2, 1) * s2


def _inproj_kernel(x_ref, g_ref, wraw_ref, c_ref, s1_ref, s2_ref,
                   qa_ref, hb_ref, iq_ref, kt_ref, vt_ref, ikt_ref, khm_ref, vaug_ref, ikb_ref, iwt_ref, w_ref):
    @pl.when(pl.program_id(0) == 0)
    def _():
        src = np.cumsum([0] + list(IN_SPLITS))
        dst = 0
        for i in (0, 6, 7, 8, 9, 1, 2, 3):
            width = int(src[i + 1] - src[i])
            w_ref[:, dst:dst + width] = wraw_ref[0, :, int(src[i]):int(src[i + 1])].astype(BF16)
            dst += width
        tail = jnp.concatenate([wraw_ref[0, :, int(src[4]):int(src[6])],
                                jnp.zeros((D_MODEL, LANES - HEAD_DIM - IDX_HEADS), F32)], axis=1)
        w_ref[:, dst:dst + LANES] = tail.astype(BF16)

    xn = _rms(x_ref[...], g_ref[...]).astype(BF16)
    c, s1, s2 = c_ref[...], s1_ref[...], s2_ref[...]
    for j in range(A_WIDTH // HALF):
        h = _dot(xn, w_ref[:, j * HALF:(j + 1) * HALF])
        for i in range(HALF // LANES):
            lo = j * HALF + i * LANES
            qa_ref[:, lo:lo + LANES] = _rope128(h[:, i * LANES:(i + 1) * LANES], c, s1, s2)
    for j in range(4):
        lo = A_WIDTH + j * B_WIDTH
        hb_ref[:, j * B_WIDTH:(j + 1) * B_WIDTH] = _dot(xn, w_ref[:, lo:lo + B_WIDTH])
    base = A_WIDTH + 4 * B_WIDTH
    kv = _dot(xn, w_ref[:, base:base + HALF])
    k = _rope128(kv[:, :LANES], c, s1, s2)
    v = kv[:, LANES:]
    iq = _dot(xn, w_ref[:, base + HALF:base + 2 * HALF])
    iq_ref[:, 0:128] = _rope128(iq[:, :LANES], c, s1, s2)
    iq_ref[:, 128:256] = _rope128(iq[:, LANES:], c, s1, s2)
    lane = lax.broadcasted_iota(I32, c.shape, 1)
    first = lane < HEAD_DIM
    ikw = _rope128(_dot(xn, w_ref[:, base + 512:base + 640]),
                   jnp.where(first, c, 1.0), jnp.where(first, s1, 0.0), jnp.where(first, s2, 0.0))
    kt_ref[...] = k.T.reshape(kt_ref.shape)
    vt_ref[...] = v.T.reshape(vt_ref.shape)
    ikw_t = ikw.T
    ikt_ref[...] = ikw_t[:HEAD_DIM].reshape(ikt_ref.shape)
    iwt_ref[...] = ikw_t[HEAD_DIM:HEAD_DIM + 8]
    ikb_ref[...] = ikw[:, :HEAD_DIM].astype(BF16)
    ones_col = jnp.where(lane == HEAD_DIM, 1.0, 0.0)
    for j in range(A_KV_HEADS):
        khm_ref[j] = k[:, j * HEAD_DIM:(j + 1) * HEAD_DIM].astype(BF16)
        vj = v if j == 0 else pltpu.roll(v, LANES - j * HEAD_DIM, 1)
        vaug_ref[j] = jnp.where(first, vj, ones_col).astype(BF16)


def _in_proj(x, g, w_all, layer, tabs, t_len, tm):
    n = x.shape[0]
    nt = tabs[0].shape[0] // tm
    per_batch = t_len % tm == 0
    ntb = t_len // tm if per_batch else 1
    row = lambda i: (i, 0)
    tab = lambda i: (i % nt, 0)
    whole = lambda i: (0, 0)
    tmin = lambda i: (0, i)
    hmaj = lambda i: (0, i, 0)
    if per_batch:
        kv_spec = pl.BlockSpec((1, A_KV_HEADS, HEAD_DIM, tm), lambda i: (i // ntb, 0, 0, i % ntb))
        ik_spec = pl.BlockSpec((1, HEAD_DIM, tm), lambda i: (i // ntb, 0, i % ntb))
        kv_shape = jax.ShapeDtypeStruct((n // t_len, A_KV_HEADS, HEAD_DIM, t_len), F32)
        ik_shape = jax.ShapeDtypeStruct((n // t_len, HEAD_DIM, t_len), F32)
    else:
        kv_spec, ik_spec = pl.BlockSpec((KV_WIDTH, tm), tmin), pl.BlockSpec((HEAD_DIM, tm), tmin)
        kv_shape, ik_shape = jax.ShapeDtypeStruct((KV_WIDTH, n), F32), jax.ShapeDtypeStruct((HEAD_DIM, n), F32)
    return pl.pallas_call(
        _inproj_kernel,
        grid=(n // tm,),
        in_specs=[pl.BlockSpec((tm, D_MODEL), row), pl.BlockSpec((1, D_MODEL), whole),
                  pl.BlockSpec((1, D_MODEL, IN_WIDTH), lambda i: (layer, 0, 0), pipeline_mode=pl.Buffered(1)),
                  pl.BlockSpec((tm, LANES), tab), pl.BlockSpec((tm, LANES), tab), pl.BlockSpec((tm, LANES), tab)],
        out_specs=[pl.BlockSpec((tm, A_WIDTH), row), pl.BlockSpec((tm, 4 * B_WIDTH), row),
                   pl.BlockSpec((tm, IDX_WIDTH), row),
                   kv_spec, kv_spec, ik_spec,
                   pl.BlockSpec((A_KV_HEADS, tm, HEAD_DIM), hmaj), pl.BlockSpec((A_KV_HEADS, tm, LANES), hmaj),
                   pl.BlockSpec((tm, HEAD_DIM), row), pl.BlockSpec((8, tm), tmin)],
        out_shape=[jax.ShapeDtypeStruct((n, A_WIDTH), F32), jax.ShapeDtypeStruct((n, 4 * B_WIDTH), F32),
                   jax.ShapeDtypeStruct((n, IDX_WIDTH), F32),
                   kv_shape, kv_shape, ik_shape,
                   jax.ShapeDtypeStruct((A_KV_HEADS, n, HEAD_DIM), BF16),
                   jax.ShapeDtypeStruct((A_KV_HEADS, n, LANES), BF16),
                   jax.ShapeDtypeStruct((n, HEAD_DIM), BF16), jax.ShapeDtypeStruct((8, n), F32)],
        scratch_shapes=[pltpu.VMEM((D_MODEL, IN_WIDTH_PAD), BF16)],
        compiler_params=_cparams("arbitrary"),
        name="in_proj",
    )(x, g, w_all, *tabs)


def _rope_tables(pos):
    half = ROT_DIM // 2
    inv_freq = 1.0 / (ROPE_THETA ** (jnp.arange(half, dtype=F32) * (2.0 / ROT_DIM)))
    ang = pos.astype(F32)[:, None] * inv_freq[None, :]
    cos, sin = jnp.cos(ang), jnp.sin(ang)
    t = pos.shape[0]
    pad = HEAD_DIM - ROT_DIM
    c = jnp.concatenate([cos, cos, jnp.ones((t, pad), F32)], axis=1)
    s1 = jnp.concatenate([-sin, jnp.zeros((t, half + pad), F32)], axis=1)
    s2 = jnp.concatenate([jnp.zeros((t, half), F32), sin, jnp.zeros((t, pad), F32)], axis=1)
    rep = LANES // HEAD_DIM
    return tuple(jnp.tile(a, (1, rep)) for a in (c, s1, s2))


def _dsa_kernel(q_ref, iq_ref, iw_ref, k_ref, v_ref, ik_ref, o_ref, key_ref, hi_ref, lo_ref, acc_ref, m_ref,
                s0_ref, s1_ref, alpha_ref, p_ref, bias_ref, mt0_ref, mt1_ref, *, l_real, pos0, kb_rows, topk, tq):
    i = pl.program_id(1)
    qpos0 = pos0 + i * tq
    kend = jnp.minimum(((qpos0 + tq - 1) // CHUNK + 1) * CHUNK, l_real)
    cb_rows = 2 * kb_rows
    nkc = (kend + cb_rows - 1) // cb_rows
    nkb = 2 * nkc
    qchunk = (qpos0 + lax.broadcasted_iota(I32, (1, tq), 1)) // CHUNK
    row_iota = lax.broadcasted_iota(I32, (kb_rows, tq), 0)

    iq = iq_ref[...].astype(BF16)
    iq4 = jnp.concatenate([iq[:, h * HEAD_DIM:(h + 1) * HEAD_DIM] for h in range(IDX_HEADS)], axis=0)
    iw = iw_ref[...] * (IDX_HEADS ** -0.5 * HEAD_DIM ** -0.5)

    def score_body(kb, carry):
        r0 = pl.multiple_of(kb * kb_rows, kb_rows)
        ikb = ik_ref[pl.ds(r0, kb_rows), :]
        lg4 = lax.dot_general(ikb, iq4, _NT, preferred_element_type=F32)
        acc = jnp.zeros((kb_rows, tq), F32)
        for h in range(IDX_HEADS):
            acc = acc + iw[h:h + 1, :] * jnp.maximum(lg4[:, h * tq:(h + 1) * tq], 0.0)
        bits = pltpu.bitcast(acc + 0.0, I32)
        key = bits ^ ((bits >> 31) & 0x7FFFFFFF)
        kpos = r0 + row_iota
        adm = (kpos // CHUNK <= qchunk) & (kpos < l_real)
        key = jnp.where(adm, key, INT_MIN)
        key_ref[pl.ds(r0, kb_rows), :] = key
        hi_ref[pl.ds(r0, kb_rows), :] = (key >> 16).astype(I16)
        return carry

    lax.fori_loop(0, nkb, score_body, 0)

    def count(pred):
        def body(kc, c):
            r0 = pl.multiple_of(kc * cb_rows, cb_rows)
            hit = pred(key_ref[pl.ds(r0, cb_rows), :], r0)
            return c + jnp.where(hit, 1, 0).reshape(cb_rows // 8, 8, tq).sum(axis=0)
        c = lax.fori_loop(0, nkc, body, jnp.zeros((8, tq), I32))
        return c.sum(axis=0, keepdims=True)

    def count16(ref, pred):
        one, zero = jnp.ones((), BF16), jnp.zeros((), BF16)

        def body(kc, c):
            r0 = pl.multiple_of(kc * cb_rows, cb_rows)
            w = jnp.where(pred(ref[pl.ds(r0, cb_rows), :]), one, zero).reshape(cb_rows // 16, 16, tq)
            parts = [w[r] for r in range(cb_rows // 16)]
            while len(parts) > 1:
                parts = [a + b for a, b in zip(parts[::2], parts[1::2])] + parts[len(parts) & ~1:]
            return c + parts[0]
        c = lax.fori_loop(0, nkc, body, jnp.zeros((16, tq), BF16))
        return c.astype(F32).sum(axis=0, keepdims=True).astype(I32)

    def bisect16(ref, kth):
        c0 = count16(ref, lambda blk: blk >= jnp.zeros((), I16))
        start = jnp.where(c0 >= kth, 0, -32768).astype(I32)

        def bit_body(it, prefix):
            cand = prefix | lax.shift_left(jnp.int32(1), 14 - it)
            c = count16(ref, lambda blk: blk >= cand.astype(I16))
            return jnp.where(c >= kth, cand, prefix)

        return lax.fori_loop(0, 15, bit_body, start)

    tau_hi = bisect16(hi_ref, topk)
    tau_hi16 = tau_hi.astype(I16)
    kth_lo = topk - count16(hi_ref, lambda blk: blk > tau_hi16)

    def lo_body(kc, carry):
        r0 = pl.multiple_of(kc * cb_rows, cb_rows)
        lo = ((key_ref[pl.ds(r0, cb_rows), :] & 0xFFFF) - 32768).astype(I16)
        lo_ref[pl.ds(r0, cb_rows), :] = jnp.where(hi_ref[pl.ds(r0, cb_rows), :] == tau_hi16, lo,
                                                 jnp.full((), -32768, I16))
        return carry

    lax.fori_loop(0, nkc, lo_body, 0)
    tau_lo = bisect16(lo_ref, kth_lo)
    tau = lax.shift_left(tau_hi, 16) | (tau_lo + 32768)
    n_ge = count(lambda blk, r0: blk >= tau)

    @pl.when(jnp.max(jnp.where((n_ge > topk) & (tau != INT_MIN), 1, 0)) > 0)
    def _():
        need = (topk - count(lambda blk, r0: blk > tau)).astype(F32)
        ri = lax.broadcasted_iota(I32, (kb_rows, kb_rows), 0)
        ci = lax.broadcasted_iota(I32, (kb_rows, kb_rows), 1)
        upto = jnp.where(ci <= ri, 1.0, 0.0).astype(BF16)

        def strike(kc, seen):
            for half in range(2):
                rows = pl.ds(pl.multiple_of(kc * cb_rows + half * kb_rows, kb_rows), kb_rows)
                key = key_ref[rows, :]
                tie = key == tau
                rank = _dot(upto, jnp.where(tie, 1.0, 0.0).astype(BF16)) + seen
                key_ref[rows, :] = jnp.where(tie & (rank > need), INT_MIN, key)
                seen = rank[kb_rows - 1:kb_rows, :]
            return seen

        lax.fori_loop(0, nkc, strike, jnp.zeros((1, tq), F32))

    lane_rep = kb_rows // LANES
    tau_c = jnp.tile(jnp.broadcast_to(tau, (LANES, tq)).T, (1, lane_rep))
    acc_ref[...] = jnp.zeros(acc_ref.shape, F32)
    m_ref[...] = jnp.full(m_ref.shape, NEG_BIG, F32)
    qs = (q_ref[...] * (HEAD_DIM ** -0.5 * LOG2E)).astype(BF16)
    q4 = [jnp.concatenate([qs[:, (j * A_GROUP + g) * HEAD_DIM:(j * A_GROUP + g + 1) * HEAD_DIM]
                           for g in range(A_GROUP)], axis=0) for j in range(A_KV_HEADS)]

    def qk_scores(kb, s_slot, mt_slot):
        r0 = pl.multiple_of(jnp.minimum(kb, nkb - 1) * kb_rows, kb_rows)
        key_t = key_ref[pl.ds(r0, kb_rows), :].T
        bias_ref[...] = jnp.where((key_t >= tau_c) & (key_t != INT_MIN), 0.0, NEG_BIG)
        for j in range(A_KV_HEADS):
            s = lax.dot_general(q4[j], k_ref[j, pl.ds(r0, kb_rows), :], _NT, preferred_element_type=F32)
            for g in range(A_GROUP):
                rows = slice(g * tq, (g + 1) * tq)
                sm = s[rows, :] + bias_ref[...]
                s_slot[j, rows, :] = sm
                mt_slot[j, g] = jnp.broadcast_to(jnp.max(sm, axis=1, keepdims=True), (tq, LANES))

    def softmax_pv(kb, s_slot, mt_slot):
        r0 = pl.multiple_of(kb * kb_rows, kb_rows)
        for j in range(A_KV_HEADS):
            for g in range(A_GROUP):
                rows = slice(g * tq, (g + 1) * tq)
                m_old = m_ref[j, g]
                m_new = jnp.maximum(m_old, mt_slot[j, g])
                alpha_ref[j, g] = jnp.exp2(m_old - m_new)
                m_ref[j, g] = m_new
                p_ref[j, rows, :] = jnp.exp2(s_slot[j, rows, :] - jnp.tile(m_new, (1, lane_rep))).astype(BF16)
            pv = _dot(p_ref[j], v_ref[j, pl.ds(r0, kb_rows), :])
            acc_ref[j] = alpha_ref[j] * acc_ref[j] + pv.reshape(A_GROUP, tq, LANES)

    qk_scores(0, s0_ref, mt0_ref)

    def pair_body(i, carry):
        kb = 2 * i
        qk_scores(kb + 1, s1_ref, mt1_ref)
        softmax_pv(kb, s0_ref, mt0_ref)
        qk_scores(kb + 2, s0_ref, mt0_ref)
        softmax_pv(kb + 1, s1_ref, mt1_ref)
        return carry

    lax.fori_loop(0, nkc, pair_body, 0)

    outs = []
    for j in range(A_KV_HEADS):
        acc = acc_ref[j]
        for g in range(A_GROUP):
            outs.append(acc[g, :, :HEAD_DIM] * (1.0 / acc[g, :, HEAD_DIM:HEAD_DIM + 1]))
    o_ref[...] = jnp.concatenate(outs, axis=1)


def _dsa(q, iq, iw_t, k_hm, v_aug, ik, *, bsz, l_real, pos0, kb_rows, topk, tq):
    t_len = q.shape[0] // bsz
    l_pad = ik.shape[0] // bsz
    nq = t_len // tq
    assert l_pad // 16 <= 256, "packed hit counts are summed in bf16, exact only up to 256 per slot"
    kern = functools.partial(_dsa_kernel, l_real=l_real, pos0=pos0, kb_rows=kb_rows, topk=topk, tq=tq)
    qrow = lambda b, i: (b * nq + i, 0)
    return pl.pallas_call(
        kern,
        grid=(bsz, nq),
        in_specs=[pl.BlockSpec((tq, A_WIDTH), qrow), pl.BlockSpec((tq, IDX_WIDTH), qrow),
                  pl.BlockSpec((8, tq), lambda b, i: (0, b * nq + i)),
                  pl.BlockSpec((A_KV_HEADS, l_pad, HEAD_DIM), lambda b, i: (0, b, 0)),
                  pl.BlockSpec((A_KV_HEADS, l_pad, LANES), lambda b, i: (0, b, 0)),
                  pl.BlockSpec((l_pad, HEAD_DIM), lambda b, i: (b, 0))],
        out_specs=pl.BlockSpec((tq, A_WIDTH), qrow),
        out_shape=jax.ShapeDtypeStruct((bsz * t_len, A_WIDTH), F32),
        scratch_shapes=[pltpu.VMEM((l_pad, tq), I32), pltpu.VMEM((l_pad, tq), I16), pltpu.VMEM((l_pad, tq), I16),
                        pltpu.VMEM((A_KV_HEADS, A_GROUP, tq, LANES), F32),
                        pltpu.VMEM((A_KV_HEADS, A_GROUP, tq, LANES), F32),
                        pltpu.VMEM((A_KV_HEADS, A_GROUP * tq, kb_rows), F32),
                        pltpu.VMEM((A_KV_HEADS, A_GROUP * tq, kb_rows), F32),
                        pltpu.VMEM((A_KV_HEADS, A_GROUP, tq, LANES), F32),
                        pltpu.VMEM((A_KV_HEADS, A_GROUP * tq, kb_rows), BF16),
                        pltpu.VMEM((tq, kb_rows), F32),
                        pltpu.VMEM((A_KV_HEADS, A_GROUP, tq, LANES), F32),
                        pltpu.VMEM((A_KV_HEADS, A_GROUP, tq, LANES), F32)],
        compiler_params=_cparams("parallel", "arbitrary"),
        name="dsa",
    )(q, iq, iw_t, k_hm, v_aug, ik)


def _hgrn_kernel(bq_ref, bf_ref, bi_ref, bg_ref, lb_ref, gn_ref, s0_ref, o_ref, sfin_ref,
                 st_ref, q_s, kk_s, b_s, qe_s, kd_s, dec_s, o_s, e_s, *, layer, tb, nseq):
    tstep = pl.program_id(1)
    nchunk = tb // B_CHUNK

    @pl.when(tstep == 0)
    def _():
        st_ref[...] = s0_ref[...]

    lbr = lb_ref[...]
    lbe = jnp.exp(lbr - jnp.max(lbr, axis=0, keepdims=True))
    lbs = lbe / jnp.sum(lbe, axis=0, keepdims=True)
    cum = lbs[0:1, :]
    first = cum
    for r in range(1, layer + 1):
        cum = cum + lbs[r:r + 1, :]
    lb = cum - first

    ri = lax.broadcasted_iota(I32, (tb, tb), 0)
    ci = lax.broadcasted_iota(I32, (tb, tb), 1)
    same = (ri // B_CHUNK) == (ci // B_CHUNK)
    tri = jnp.where(same & (ci <= ri), 1.0, 0.0).astype(BF16)
    ones = jnp.where(same, 1.0, 0.0).astype(BF16)
    for u in range(nseq):
        f = lb + (1.0 - lb) * _sigmoid(bf_ref[u])
        bq = bq_ref[u]
        q = bq * _sigmoid(bq) * (HEAD_DIM ** -0.5)
        g3 = _split3(jnp.log(f))
        b = _dot(tri, g3[0]) + _dot(tri, g3[1]) + _dot(tri, g3[2])
        blast = _dot(ones, g3[0]) + _dot(ones, g3[1]) + _dot(ones, g3[2])
        q_s[u] = q
        kk_s[u] = 1.0 - f
        b_s[u] = b
        qe_s[u] = q * jnp.exp(b)
        kd_s[u] = (1.0 - f) * jnp.exp(blast - b)
        dec_s[u] = jnp.exp(blast)

    hi = lax.broadcasted_iota(I32, (HALF, HALF), 0) // HEAD_DIM
    hj = lax.broadcasted_iota(I32, (HALF, HALF), 1) // HEAD_DIM
    bd_mask = hi == hj
    bd = jnp.where(bd_mask, 1.0, 0.0).astype(BF16)
    t_iota = lax.broadcasted_iota(I32, (B_CHUNK, B_WIDTH), 0)

    def seg_sum(x, two_pass):
        out = []
        for half in range(2):
            xs = x[:, half * HALF:(half + 1) * HALF]
            if two_pass:
                xh, xl = _split2(xs)
                out.append(_dot(xh, bd) + _dot(xl, bd))
            else:
                out.append(_dot(xs.astype(BF16), bd))
        return jnp.concatenate(out, axis=1)

    def chunk_body(c, carry):
        r0 = pl.multiple_of(c * B_CHUNK, B_CHUNK)
        rows = pl.ds(r0, B_CHUNK)
        for u in range(nseq):
            qc, kc, bc, vc = q_s[u, rows, :], kk_s[u, rows, :], b_s[u, rows, :], bi_ref[u, rows, :]
            qe, kd = qe_s[u, rows, :].astype(BF16), kd_s[u, rows, :].astype(BF16)
            dec = dec_s[u, pl.ds(r0, 1), :]
            for s in range(B_CHUNK):
                ex = jnp.exp(jnp.where(t_iota >= s, bc - bc[s:s + 1, :], NEG_BIG))
                e_s[u, s * B_CHUNK:(s + 1) * B_CHUNK, :] = qc * kc[s:s + 1, :] * ex
            aexp = seg_sum(e_s[u], False)
            o = jnp.zeros((B_CHUNK, B_WIDTH), F32)
            for s in range(B_CHUNK):
                o = o + aexp[s * B_CHUNK:(s + 1) * B_CHUNK, :] * vc[s:s + 1, :]
            inter = []
            vb = vc.astype(BF16)
            for half in range(2):
                cols = slice(half * HALF, (half + 1) * HALF)
                st = st_ref[u, half]
                inter.append(lax.dot_general(qe[:, cols], st.astype(BF16), _NT, preferred_element_type=F32))
                upd = lax.dot_general(vb[:, cols], kd[:, cols], _TN, preferred_element_type=F32)
                st_ref[u, half] = st * dec[:, cols] + jnp.where(bd_mask, upd, 0.0)
            o_s[u, rows, :] = o + jnp.concatenate(inter, axis=1)
        return carry

    lax.fori_loop(0, nchunk, chunk_body, 0)

    for u in range(nseq):
        o = o_s[u]
        ms = seg_sum(o * o, True) * (1.0 / HEAD_DIM)
        bg = bg_ref[u]
        o_ref[u] = o * lax.rsqrt(ms + EPS) * gn_ref[...] * (bg * _sigmoid(bg))

    @pl.when(tstep == pl.num_programs(1) - 1)
    def _():
        sfin_ref[...] = st_ref[...]


def _hgrn(hb, lb_raw, gn_tiled, s0_bd, *, layer, bsz, t_len, tb):
    nt = t_len // tb
    nseq = 2 if bsz % 2 == 0 else 1
    hb3 = hb.reshape(bsz, t_len, 4 * B_WIDTH)
    col = lambda j: (lambda b, t: (b, t, j))
    scr = lambda r: pltpu.VMEM((nseq, r, B_WIDTH), F32)
    kern = functools.partial(_hgrn_kernel, layer=layer, tb=tb, nseq=nseq)
    blk = (nseq, tb, B_WIDTH)
    out, s_fin = pl.pallas_call(
        kern,
        grid=(bsz // nseq, nt),
        in_specs=[pl.BlockSpec(blk, col(0)), pl.BlockSpec(blk, col(1)), pl.BlockSpec(blk, col(2)),
                  pl.BlockSpec(blk, col(3)),
                  pl.BlockSpec(lb_raw.shape, lambda b, t: (0, 0)),
                  pl.BlockSpec((1, B_WIDTH), lambda b, t: (0, 0)),
                  pl.BlockSpec((nseq, 2, HALF, HALF), lambda b, t: (b, 0, 0, 0))],
        out_specs=[pl.BlockSpec(blk, lambda b, t: (b, t, 0)),
                   pl.BlockSpec((nseq, 2, HALF, HALF), lambda b, t: (b, 0, 0, 0))],
        out_shape=[jax.ShapeDtypeStruct((bsz, t_len, B_WIDTH), F32),
                   jax.ShapeDtypeStruct((bsz, 2, HALF, HALF), F32)],
        scratch_shapes=[pltpu.VMEM((nseq, 2, HALF, HALF), F32), scr(tb), scr(tb), scr(tb), scr(tb), scr(tb),
                        scr(tb), scr(tb), scr(B_CHUNK * B_CHUNK)],
        compiler_params=_cparams("parallel", "arbitrary"),
        name="hgrn2",
    )(hb3, hb3, hb3, hb3, lb_raw, gn_tiled, s0_bd)
    return out.reshape(bsz * t_len, B_WIDTH), s_fin


def _state_to_tiles(s):
    bsz = s.shape[0]
    st = jnp.swapaxes(s, -1, -2).reshape(bsz, 2, B_HEADS // 2, HEAD_DIM, HEAD_DIM)
    eye = jnp.eye(B_HEADS // 2, dtype=s.dtype)
    t = st[:, :, :, :, None, :] * eye[None, None, :, None, :, None]
    return t.reshape(bsz, 2, HALF, HALF)


def _tiles_to_state(t):
    bsz = t.shape[0]
    t6 = t.reshape(bsz, 2, B_HEADS // 2, HEAD_DIM, B_HEADS // 2, HEAD_DIM)
    diag = jnp.stack([t6[:, :, h, :, h, :] for h in range(B_HEADS // 2)], axis=2)
    return jnp.swapaxes(diag.reshape(bsz, B_HEADS, HEAD_DIM, HEAD_DIM), -1, -2)


def _outproj_kernel(a_ref, b_ref, wa_ref, wb_ref, x_ref, o_ref):
    o_ref[...] = (x_ref[...] + _dot(a_ref[...].astype(BF16), wa_ref[...])
                  + _dot(b_ref[...].astype(BF16), wb_ref[...]))


def _out_proj(a, b, wa, wb, x, tm):
    n = x.shape[0]
    row = lambda i: (i, 0)
    whole = lambda i: (0, 0)
    return pl.pallas_call(
        _outproj_kernel,
        grid=(n // tm,),
        in_specs=[pl.BlockSpec((tm, A_WIDTH), row), pl.BlockSpec((tm, B_WIDTH), row),
                  pl.BlockSpec((A_WIDTH, D_MODEL), whole), pl.BlockSpec((B_WIDTH, D_MODEL), whole),
                  pl.BlockSpec((tm, D_MODEL), row)],
        out_specs=pl.BlockSpec((tm, D_MODEL), row),
        out_shape=jax.ShapeDtypeStruct((n, D_MODEL), F32),
        compiler_params=_cparams("parallel"),
        name="out_proj",
    )(a, b, wa, wb, x)


def _normmm_kernel(x_ref, g_ref, w_ref, o_ref):
    o_ref[...] = _dot(_rms(x_ref[...], g_ref[...]).astype(BF16), w_ref[...])


def _norm_matmul(x, g, w, tm):
    n, m = x.shape[0], w.shape[1]
    return pl.pallas_call(
        _normmm_kernel,
        grid=(n // tm,),
        in_specs=[pl.BlockSpec((tm, D_MODEL), lambda i: (i, 0)), pl.BlockSpec((1, D_MODEL), lambda i: (0, 0)),
                  pl.BlockSpec((D_MODEL, m), lambda i: (0, 0))],
        out_specs=pl.BlockSpec((tm, m), lambda i: (i, 0)),
        out_shape=jax.ShapeDtypeStruct((n, m), F32),
        compiler_params=_cparams("parallel"),
        name="norm_matmul",
    )(x, g, w)


def _memattn_kernel(x_ref, g_ref, wq_ref, mk_ref, mv_ref, wo_ref, o_ref):
    x = x_ref[...]
    q = _dot(_rms(x, g_ref[...]).astype(BF16), wq_ref[...])
    heads = []
    for h in range(MEM_HEADS):
        cols = slice(h * MEM_HEAD_DIM, (h + 1) * MEM_HEAD_DIM)
        s = lax.dot_general(q[:, cols].astype(BF16), mk_ref[0, :, cols], _NT,
                            preferred_element_type=F32) * (MEM_HEAD_DIM ** -0.5)
        p = jnp.exp(s - jnp.max(s, axis=-1, keepdims=True))
        p = p * (1.0 / jnp.sum(p, axis=-1, keepdims=True))
        heads.append(_dot(p.astype(BF16), mv_ref[0, :, cols]).astype(BF16))
    o_ref[...] = x + _dot(jnp.concatenate(heads, axis=1), wo_ref[...])


def _mem_attn(x, g, wq, mk, mv, wo, rows_per_batch, tm):
    n = x.shape[0]
    n_mem = mk.shape[1]
    per = rows_per_batch // tm
    row = lambda i: (i, 0)
    whole = lambda i: (0, 0)
    mem = lambda i: (i // per, 0, 0)
    return pl.pallas_call(
        _memattn_kernel,
        grid=(n // tm,),
        in_specs=[pl.BlockSpec((tm, D_MODEL), row), pl.BlockSpec((1, D_MODEL), whole),
                  pl.BlockSpec((D_MODEL, D_MODEL), whole),
                  pl.BlockSpec((1, n_mem, D_MODEL), mem), pl.BlockSpec((1, n_mem, D_MODEL), mem),
                  pl.BlockSpec((D_MODEL, D_MODEL), whole)],
        out_specs=pl.BlockSpec((tm, D_MODEL), row),
        out_shape=jax.ShapeDtypeStruct((n, D_MODEL), F32),
        compiler_params=_cparams("parallel"),
        name="mem_attn",
    )(x, g, wq, mk, mv, wo)


def _swiglu_mid(xn, w1, w3):
    a = _dot(xn, w1)
    return (a * _sigmoid(a) * _dot(xn, w3)).astype(BF16)


def _ffn_kernel(x_ref, g_ref, w1_ref, w3_ref, w2_ref, gf_ref, o_ref, xn_ref, *, final):
    j = pl.program_id(1)

    @pl.when(j == 0)
    def _():
        x = x_ref[...]
        xn_ref[...] = _rms(x, g_ref[...]).astype(BF16)
        o_ref[...] = x

    o_ref[...] += _dot(_swiglu_mid(xn_ref[...], w1_ref[...], w3_ref[...]), w2_ref[...])

    if final:
        @pl.when(j == pl.num_programs(1) - 1)
        def _():
            o_ref[...] = _rms(o_ref[...], gf_ref[...])


def _ffn(x, g, w1, w3, w2, gf, *, final, tm, tf):
    n = x.shape[0]
    d_ff = w1.shape[1]
    row = lambda i, j: (i, 0)
    whole = lambda i, j: (0, 0)
    return pl.pallas_call(
        functools.partial(_ffn_kernel, final=final),
        grid=(n // tm, d_ff // tf),
        in_specs=[pl.BlockSpec((tm, D_MODEL), row), pl.BlockSpec((1, D_MODEL), whole),
                  pl.BlockSpec((D_MODEL, tf), lambda i, j: (0, j)), pl.BlockSpec((D_MODEL, tf), lambda i, j: (0, j)),
                  pl.BlockSpec((tf, D_MODEL), lambda i, j: (j, 0)), pl.BlockSpec((1, D_MODEL), whole)],
        out_specs=pl.BlockSpec((tm, D_MODEL), row),
        out_shape=jax.ShapeDtypeStruct((n, D_MODEL), F32),
        scratch_shapes=[pltpu.VMEM((tm, D_MODEL), BF16)],
        compiler_params=_cparams("parallel", "arbitrary"),
        name="ffn",
    )(x, g, w1, w3, w2, gf)


def _router_kernel(x_ref, g_ref, rt_ref, comb_ref, sel_ref):
    xn = _rms(x_ref[...], g_ref[...])
    logits = lax.dot_general(rt_ref[...], xn, _NT, preferred_element_type=F32,
                             precision=lax.Precision.HIGHEST)
    e_idx = lax.broadcasted_iota(I32, logits.shape, 0)
    m1 = jnp.max(logits, axis=0, keepdims=True)
    i1 = jnp.min(jnp.where(logits == m1, e_idx, N_EXPERTS), axis=0, keepdims=True)
    rest = jnp.where(e_idx == i1, -jnp.inf, logits)
    m2 = jnp.max(rest, axis=0, keepdims=True)
    i2 = jnp.min(jnp.where(rest == m2, e_idx, N_EXPERTS), axis=0, keepdims=True)
    e2 = jnp.exp(m2 - m1)
    g1 = 1.0 / (1.0 + e2)
    comb_ref[...] = jnp.where(e_idx == i1, g1, 0.0) + jnp.where(e_idx == i2, e2 * g1, 0.0)
    sel_ref[...] = jnp.where((e_idx == i1) | (e_idx == i2), 1.0, 0.0)


def _router(x, g, rt, tm):
    n = x.shape[0]
    return pl.pallas_call(
        _router_kernel,
        grid=(n // tm,),
        in_specs=[pl.BlockSpec((tm, D_MODEL), lambda i: (i, 0)), pl.BlockSpec((1, D_MODEL), lambda i: (0, 0)),
                  pl.BlockSpec((N_EXPERTS, D_MODEL), lambda i: (0, 0))],
        out_specs=[pl.BlockSpec((N_EXPERTS, tm), lambda i: (0, i)), pl.BlockSpec((N_EXPERTS, tm), lambda i: (0, i))],
        out_shape=[jax.ShapeDtypeStruct((N_EXPERTS, n), F32), jax.ShapeDtypeStruct((N_EXPERTS, n), F32)],
        compiler_params=_cparams("parallel"),
        name="moe_router",
    )(x, g, rt)


def _moe_kernel(x_ref, g_ref, selt_ref, sel_ref, comb_ref, w1_ref, w3_ref, w2_ref, gf_ref, o_ref,
                xn_ref, posr_ref, posc_ref, cnt_ref, xg_ref, yg_ref, *, final, nb, sub):
    e = pl.program_id(1)
    j = pl.program_id(2)
    last_j = pl.num_programs(2) - 1

    @pl.when((e == 0) & (j == 0))
    def _():
        x = x_ref[...]
        xn_ref[...] = _rms(x, g_ref[...]).astype(BF16)
        o_ref[...] = x
        cc = min(nb, 256)
        ri = lax.broadcasted_iota(I32, (cc, cc), 0)
        ci = lax.broadcasted_iota(I32, (cc, cc), 1)
        before_r = jnp.where(ri < ci, 1.0, 0.0).astype(BF16)
        before_c = jnp.where(ci < ri, 1.0, 0.0).astype(BF16)
        run_r = jnp.zeros((N_EXPERTS, 1), F32)
        run_c = jnp.zeros((1, N_EXPERTS), F32)
        for c in range(nb // cc):
            cols = slice(c * cc, (c + 1) * cc)
            mr = selt_ref[:, cols]
            posr_ref[:, cols] = jnp.where(mr > 0.0, _dot(mr.astype(BF16), before_r) + run_r, -1.0)
            run_r = run_r + jnp.sum(mr, axis=1, keepdims=True)
            mc = sel_ref[cols, :]
            posc_ref[cols, :] = jnp.where(mc > 0.0, _dot(before_c, mc.astype(BF16)) + run_c, -1.0)
            run_c = run_c + jnp.sum(mc, axis=0, keepdims=True)
        for k in range(N_EXPERTS):
            cnt_ref[k] = run_c[0, k].astype(I32)

    cnt = cnt_ref[e]
    nfull = cnt // sub
    rem = cnt - nfull * sub
    tail_sizes = [m for m in (64, 128) if m < sub] + [sub]

    def for_tiles(fn):
        def body(s, carry):
            fn(pl.multiple_of(s * sub, sub), sub)
            return carry
        lax.fori_loop(0, nfull, body, 0)
        r0 = pl.multiple_of(nfull * sub, sub)
        lo = 0
        for m in tail_sizes:
            @pl.when((rem > lo) & (rem <= m))
            def _():
                fn(r0, m)
            lo = m

    def gather_matrix(r0, m):
        slot_r = posr_ref[pl.ds(e, 1), :]
        want = (r0 + lax.broadcasted_iota(I32, (m, nb), 0)).astype(F32)
        return jnp.where(slot_r == want, 1.0, 0.0).astype(BF16)

    def gather(r0, m):
        xg_ref[pl.ds(r0, m), :] = _dot(gather_matrix(r0, m), xn_ref[...]).astype(BF16)

    def ffn(r0, m):
        rows = pl.ds(r0, m)
        part = _dot(_swiglu_mid(xg_ref[rows, :], w1_ref[0], w3_ref[0]), w2_ref[0])

        @pl.when(j == 0)
        def _():
            yg_ref[rows, :] = part

        @pl.when(j > 0)
        def _():
            yg_ref[rows, :] += part

    @pl.when(j == 0)
    def _():
        for_tiles(gather)

    for_tiles(ffn)

    @pl.when(j == last_j)
    def _():
        lane8 = lax.broadcasted_iota(I32, (nb, N_EXPERTS), 1)
        slot_c = jnp.sum(jnp.where(lane8 == e, posc_ref[...], 0.0), axis=1, keepdims=True)
        gate_c = jnp.sum(jnp.where(lane8 == e, comb_ref[...], 0.0), axis=1, keepdims=True)
        gate3 = jnp.concatenate(_split3(gate_c), axis=1)

        def scatter(r0, m):
            gate = jnp.sum(_dot(gather_matrix(r0, m), gate3), axis=1, keepdims=True)
            y = (yg_ref[pl.ds(r0, m), :] * gate).astype(BF16)
            want = (r0 + lax.broadcasted_iota(I32, (nb, m), 1)).astype(F32)
            o_ref[...] += _dot(jnp.where(slot_c == want, 1.0, 0.0).astype(BF16), y)

        for_tiles(scatter)

        if final:
            @pl.when(e == pl.num_programs(1) - 1)
            def _():
                o_ref[...] = _rms(o_ref[...], gf_ref[...])


def _moe(x, g, sel_t, sel, comb, w1, w3, w2, gf, *, final, nb, tf):
    n = x.shape[0]
    d_ff = w1.shape[2]
    sub = min(nb, MOE_TILE_ROWS)
    row = lambda i, e, j: (i, 0)
    whole = lambda i, e, j: (0, 0)
    return pl.pallas_call(
        functools.partial(_moe_kernel, final=final, nb=nb, sub=sub),
        grid=(n // nb, N_EXPERTS, d_ff // tf),
        in_specs=[pl.BlockSpec((nb, D_MODEL), row), pl.BlockSpec((1, D_MODEL), whole),
                  pl.BlockSpec((N_EXPERTS, nb), lambda i, e, j: (0, i)),
                  pl.BlockSpec((nb, N_EXPERTS), row), pl.BlockSpec((nb, N_EXPERTS), row),
                  pl.BlockSpec((1, D_MODEL, tf), lambda i, e, j: (e, 0, j)),
                  pl.BlockSpec((1, D_MODEL, tf), lambda i, e, j: (e, 0, j)),
                  pl.BlockSpec((1, tf, D_MODEL), lambda i, e, j: (e, j, 0)), pl.BlockSpec((1, D_MODEL), whole)],
        out_specs=pl.BlockSpec((nb, D_MODEL), row),
        out_shape=jax.ShapeDtypeStruct((n, D_MODEL), F32),
        scratch_shapes=[pltpu.VMEM((nb, D_MODEL), BF16), pltpu.VMEM((N_EXPERTS, nb), F32),
                        pltpu.VMEM((nb, N_EXPERTS), F32), pltpu.SMEM((N_EXPERTS,), I32),
                        pltpu.VMEM((nb + sub, D_MODEL), BF16), pltpu.VMEM((nb + sub, D_MODEL), F32)],
        compiler_params=_cparams("parallel", "arbitrary", "arbitrary"),
        name="moe",
    )(x, g, sel_t, sel, comb, w1, w3, w2, gf)


def _row_tile(n, cap):
    t = min(n, cap)
    assert n % t == 0, (n, cap)
    return t


def _ff_tile(d_ff):
    for parts in (2, 4, 11, 22):
        if d_ff % parts == 0 and (d_ff // parts) % LANES == 0:
            return d_ff // parts
    return d_ff


def _mixer(x, tabs, w, *, bsz, t_len, pos0, past, s0, layer):
    n = bsz * t_len
    qa, hb, iq, k_t, v_t, ik_t, k_hm, v_aug, ik_b, iw_t = _in_proj(x, w["norm_mix"], w["w_in"], layer, tabs,
                                                                   t_len, _row_tile(n, 512))
    l_real = t_len if past is None else past[0].shape[1] + t_len
    topk = min(TOPK_MAX, l_real // 4)
    tq = 2 * LANES if t_len % (2 * LANES) == 0 else LANES
    kb_rows = 256 if l_real % 256 == 0 else LANES
    l_pad = -(-l_real // (2 * kb_rows)) * (2 * kb_rows)
    t_pad = -(-t_len // tq) * tq
    if past is not None:
        pk, pv, pik = past
        one = jnp.ones(pv.shape[:2] + (A_KV_HEADS, 1), BF16)
        zero = jnp.zeros(pv.shape[:2] + (A_KV_HEADS, LANES - HEAD_DIM - 1), BF16)
        pv_aug = jnp.concatenate([pv.astype(BF16), one, zero], axis=-1)
        hm = lambda a: jnp.transpose(a, (2, 0, 1, 3))
        k_hm = jnp.concatenate([hm(pk.astype(BF16)), k_hm.reshape(A_KV_HEADS, bsz, t_len, HEAD_DIM)], axis=2)
        v_aug = jnp.concatenate([hm(pv_aug), v_aug.reshape(A_KV_HEADS, bsz, t_len, LANES)], axis=2)
        ik_b = jnp.concatenate([pik.astype(BF16), ik_b.reshape(bsz, t_len, HEAD_DIM)], axis=1)
    if l_pad != l_real or past is not None:
        padl = l_pad - l_real
        k_hm = jnp.pad(k_hm.reshape(A_KV_HEADS, bsz, l_real, HEAD_DIM), ((0, 0), (0, 0), (0, padl), (0, 0)))
        v_aug = jnp.pad(v_aug.reshape(A_KV_HEADS, bsz, l_real, LANES), ((0, 0), (0, 0), (0, padl), (0, 0)))
        ik_b = jnp.pad(ik_b.reshape(bsz, l_real, HEAD_DIM), ((0, 0), (0, padl), (0, 0)))
        k_hm = k_hm.reshape(A_KV_HEADS, bsz * l_pad, HEAD_DIM)
        v_aug = v_aug.reshape(A_KV_HEADS, bsz * l_pad, LANES)
        ik_b = ik_b.reshape(bsz * l_pad, HEAD_DIM)
    q_in, iq_in, iw_in = qa, iq, iw_t
    if t_pad != t_len:
        padq = lambda a: jnp.pad(a.reshape(bsz, t_len, -1),
                                 ((0, 0), (0, t_pad - t_len), (0, 0))).reshape(bsz * t_pad, -1)
        q_in, iq_in = padq(qa), padq(iq)
        iw_in = jnp.pad(iw_t.reshape(8, bsz, t_len), ((0, 0), (0, 0), (0, t_pad - t_len))).reshape(8, bsz * t_pad)
    oa = _dsa(q_in, iq_in, iw_in, k_hm, v_aug, ik_b, bsz=bsz, l_real=l_real, pos0=pos0, kb_rows=kb_rows,
              topk=topk, tq=tq)
    if t_pad != t_len:
        oa = oa.reshape(bsz, t_pad, A_WIDTH)[:, :t_len].reshape(n, A_WIDTH)
    ob, s_tiles = _hgrn(hb, w["hgrn_lb"], w["gnorm"], _state_to_tiles(s0), layer=layer, bsz=bsz, t_len=t_len,
                        tb=_row_tile(t_len, 256))
    x1 = _out_proj(oa, ob, w["w_out_a"], w["w_out_b"], x, _row_tile(n, 512))
    return x1, k_t, v_t, ik_t, _tiles_to_state(s_tiles)


def _from_time_minor(a, bsz, t_len, heads):
    if a.ndim == 2:
        a = a.reshape((-1, HEAD_DIM, bsz, t_len))
        a = jnp.transpose(a, (2, 3, 0, 1))
    else:
        a = jnp.moveaxis(a, -1, 1)
        a = a.reshape(bsz, t_len, -1, HEAD_DIM)
    return a if heads is not None else a.reshape(bsz, t_len, HEAD_DIM)


def kernel(x_prompt, x_sample, cache_k, cache_v, cache_idx_k, state_hgrn, cache_mem_k, cache_mem_v, mem_prompt,
           norm_mix, w_in, hgrn_lb, hgrn_gnorm, w_out, norm_mem, norm_memkv, w_mq, w_mk, w_mv, w_mo,
           norm_ffn, ffn_w1, ffn_w3, ffn_w2, moe_router, moe_w1, moe_w3, moe_w2, norm_final):
    depth = w_in.shape[0]
    bp, tp, _ = x_prompt.shape
    bs, ts, _ = x_sample.shape
    past = cache_k.shape[2]
    n_mem = mem_prompt.shape[1]
    tabs_p = _rope_tables(jnp.arange(tp, dtype=I32))
    tabs_s = tuple(jnp.tile(a, (bs, 1)) for a in _rope_tables(past + jnp.arange(ts, dtype=I32)))
    xp = x_prompt.reshape(bp * tp, D_MODEL)
    xs = x_sample.reshape(bs * ts, D_MODEL)
    mem = mem_prompt.reshape(bp * n_mem, D_MODEL)
    s0_p = jnp.zeros((bp, B_HEADS, HEAD_DIM, HEAD_DIM), F32)
    gfin = norm_final.reshape(1, D_MODEL)
    row2 = lambda a: a.reshape(1, -1)
    outs = {k: [] for k in ("kp", "vp", "ikp", "sp", "mkp", "mvp", "ks", "vs", "iks", "ss")}
    for l in range(depth):
        w = {"norm_mix": row2(norm_mix[l]), "w_in": w_in, "hgrn_lb": hgrn_lb.astype(F32),
             "gnorm": row2(jnp.tile(hgrn_gnorm[l], B_HEADS)),
             "w_out_a": w_out[l, :A_WIDTH].astype(BF16), "w_out_b": w_out[l, A_WIDTH:].astype(BF16)}
        xp, kp, vp, ikp, sp = _mixer(xp, tabs_p, w, bsz=bp, t_len=tp, pos0=0, past=None, s0=s0_p, layer=l)
        xs, ks, vs, iks, ss = _mixer(xs, tabs_s, w, bsz=bs, t_len=ts, pos0=past,
                                     past=(cache_k[l], cache_v[l], cache_idx_k[l]), s0=state_hgrn[l], layer=l)
        wkv = jnp.concatenate([w_mk[l], w_mv[l]], axis=1).astype(BF16)
        mkv = _norm_matmul(mem, row2(norm_memkv[l]), wkv, _row_tile(bp * n_mem, 256))
        mk_p = mkv[:, :D_MODEL].reshape(bp, n_mem, D_MODEL)
        mv_p = mkv[:, D_MODEL:].reshape(bp, n_mem, D_MODEL)
        wq, wo, gm = w_mq[l].astype(BF16), w_mo[l].astype(BF16), row2(norm_mem[l])
        xp = _mem_attn(xp, gm, wq, mk_p.astype(BF16), mv_p.astype(BF16), wo, tp, _row_tile(tp, 512))
        xs = _mem_attn(xs, gm, wq, cache_mem_k[l].reshape(bs, n_mem, D_MODEL).astype(BF16),
                       cache_mem_v[l].reshape(bs, n_mem, D_MODEL).astype(BF16), wo, ts, _row_tile(ts, 256))
        final = l == depth - 1
        gf = row2(norm_ffn[l])
        if l % 2 == 0:
            w1, w3, w2 = ffn_w1[l // 2].astype(BF16), ffn_w3[l // 2].astype(BF16), ffn_w2[l // 2].astype(BF16)
            tf = _ff_tile(w1.shape[1])
            xp = _ffn(xp, gf, w1, w3, w2, gfin, final=final, tm=_row_tile(bp * tp, 512), tf=tf)
            xs = _ffn(xs, gf, w1, w3, w2, gfin, final=final, tm=_row_tile(bs * ts, 512), tf=tf)
        else:
            w1, w3, w2 = moe_w1[l // 2].astype(BF16), moe_w3[l // 2].astype(BF16), moe_w2[l // 2].astype(BF16)
            rt = moe_router[l // 2].T
            tf = _ff_tile(w1.shape[2])
            comb_p, sel_p = _router(xp, gf, rt, _row_tile(bp * tp, 512))
            comb_s, sel_s = _router(xs, gf, rt, _row_tile(bs * ts, 512))
            xp = _moe(xp, gf, sel_p, sel_p.T, comb_p.T, w1, w3, w2, gfin, final=final,
                      nb=_row_tile(bp * tp, 1024), tf=tf)
            xs = _moe(xs, gf, sel_s, sel_s.T, comb_s.T, w1, w3, w2, gfin, final=final,
                      nb=_row_tile(bs * ts, 1024), tf=tf)
        outs["kp"].append(_from_time_minor(kp, bp, tp, A_KV_HEADS))
        outs["vp"].append(_from_time_minor(vp, bp, tp, A_KV_HEADS))
        outs["ikp"].append(_from_time_minor(ikp, bp, tp, None)); outs["sp"].append(sp)
        outs["mkp"].append(mk_p.reshape(bp, n_mem, MEM_HEADS, MEM_HEAD_DIM))
        outs["mvp"].append(mv_p.reshape(bp, n_mem, MEM_HEADS, MEM_HEAD_DIM))
        outs["ks"].append(_from_time_minor(ks, bs, ts, A_KV_HEADS))
        outs["vs"].append(_from_time_minor(vs, bs, ts, A_KV_HEADS))
        outs["iks"].append(_from_time_minor(iks, bs, ts, None)); outs["ss"].append(ss)
    y_prompt = xp.reshape(bp, tp, D_MODEL)
    y_sample = xs.reshape(bs, ts, D_MODEL)
    st = lambda k: jnp.stack(outs[k])
    return (y_prompt, y_sample, st("kp"), st("vp"), st("ikp"), st("sp"), st("mkp"), st("mvp"),
            st("ks"), st("vs"), st("iks"), st("ss"))
```

```python
import functools

import numpy as np
import jax
import jax.numpy as jnp
from jax import lax
from jax.experimental import pallas as pl
from jax.experimental.pallas import tpu as pltpu

F32 = jnp.float32
BF16 = jnp.bfloat16
I32 = jnp.int32
I16 = jnp.int16

D_MODEL = 1024
CHUNK = 64
EPS = 1e-6
ROPE_THETA = 500000.0
HEAD_DIM = 64
ROT_DIM = HEAD_DIM // 4
A_HEADS = 8
A_KV_HEADS = 2
A_GROUP = A_HEADS // A_KV_HEADS
A_WIDTH = A_HEADS * HEAD_DIM
KV_WIDTH = A_KV_HEADS * HEAD_DIM
IDX_HEADS = 4
IDX_WIDTH = IDX_HEADS * HEAD_DIM
TOPK_MAX = 256
B_HEADS = 8
B_WIDTH = B_HEADS * HEAD_DIM
B_CHUNK = 16
MEM_HEADS = 4
MEM_HEAD_DIM = D_MODEL // MEM_HEADS
N_EXPERTS = 8
MOE_TILE_ROWS = 256
LANES = 128
HALF = 256
IN_SPLITS = (A_WIDTH, KV_WIDTH, KV_WIDTH, IDX_WIDTH, HEAD_DIM, IDX_HEADS, B_WIDTH, B_WIDTH, B_WIDTH, B_WIDTH)
IN_WIDTH = sum(IN_SPLITS)
IN_WIDTH_PAD = 3200
VMEM_LIMIT = 56 * 1024 * 1024
INT_MIN = -2147483648
NEG_BIG = -1e30
LOG2E = 1.4426950408889634

_NT = (((1,), (1,)), ((), ()))
_TN = (((0,), (0,)), ((), ()))


def _cparams(*sem):
    return pltpu.CompilerParams(dimension_semantics=sem, vmem_limit_bytes=VMEM_LIMIT)


def _rms(x, g):
    return x * lax.rsqrt(jnp.mean(x * x, axis=-1, keepdims=True) + EPS) * g


def _sigmoid(x):
    return 1.0 / (1.0 + jnp.exp(-x))


def _dot(a, b):
    return jnp.dot(a, b, preferred_element_type=F32)


def _split2(x):
    hi = x.astype(BF16)
    lo = (x - hi.astype(F32)).astype(BF16)
    return hi, lo


def _split3(x):
    hi = x.astype(BF16)
    r = x - hi.astype(F32)
    mid = r.astype(BF16)
    lo = (r - mid.astype(F32)).astype(BF16)
    return hi, mid, lo


def _rope128(xb, c, s1, s2):
    return xb * c + pltpu.roll(xb, LANES - ROT_DIM // 2, 1) * s1 + pltpu.roll(xb, ROT_DIM // 2, 1) * s2


def _inproj_kernel(x_ref, g_ref, wraw_ref, c_ref, s1_ref, s2_ref,
                   qa_ref, hb_ref, iq_ref, kt_ref, vt_ref, ikt_ref, khm_ref, vaug_ref, ikb_ref, iwt_ref, w_ref):
    @pl.when(pl.program_id(0) == 0)
    def _():
        src = np.cumsum([0] + list(IN_SPLITS))
        dst = 0
        for i in (0, 6, 7, 8, 9, 1, 2, 3):
            width = int(src[i + 1] - src[i])
            w_ref[:, dst:dst + width] = wraw_ref[0, :, int(src[i]):int(src[i + 1])].astype(BF16)
            dst += width
        tail = jnp.concatenate([wraw_ref[0, :, int(src[4]):int(src[6])],
                                jnp.zeros((D_MODEL, LANES - HEAD_DIM - IDX_HEADS), F32)], axis=1)
        w_ref[:, dst:dst + LANES] = tail.astype(BF16)

    xn = _rms(x_ref[...], g_ref[...]).astype(BF16)
    c, s1, s2 = c_ref[...], s1_ref[...], s2_ref[...]
    for j in range(A_WIDTH // HALF):
        h = _dot(xn, w_ref[:, j * HALF:(j + 1) * HALF])
        for i in range(HALF // LANES):
            lo = j * HALF + i * LANES
            qa_ref[:, lo:lo + LANES] = _rope128(h[:, i * LANES:(i + 1) * LANES], c, s1, s2)
    for j in range(4):
        lo = A_WIDTH + j * B_WIDTH
        hb_ref[:, j * B_WIDTH:(j + 1) * B_WIDTH] = _dot(xn, w_ref[:, lo:lo + B_WIDTH])
    base = A_WIDTH + 4 * B_WIDTH
    kv = _dot(xn, w_ref[:, base:base + HALF])
    k = _rope128(kv[:, :LANES], c, s1, s2)
    v = kv[:, LANES:]
    iq = _dot(xn, w_ref[:, base + HALF:base + 2 * HALF])
    iq_ref[:, 0:128] = _rope128(iq[:, :LANES], c, s1, s2)
    iq_ref[:, 128:256] = _rope128(iq[:, LANES:], c, s1, s2)
    lane = lax.broadcasted_iota(I32, c.shape, 1)
    first = lane < HEAD_DIM
    ikw = _rope128(_dot(xn, w_ref[:, base + 512:base + 640]),
                   jnp.where(first, c, 1.0), jnp.where(first, s1, 0.0), jnp.where(first, s2, 0.0))
    kt_ref[...] = k.T.reshape(kt_ref.shape)
    vt_ref[...] = v.T.reshape(vt_ref.shape)
    ikw_t = ikw.T
    ikt_ref[...] = ikw_t[:HEAD_DIM].reshape(ikt_ref.shape)
    iwt_ref[...] = ikw_t[HEAD_DIM:HEAD_DIM + 8]
    ikb_ref[...] = ikw[:, :HEAD_DIM].astype(BF16)
    ones_col = jnp.where(lane == HEAD_DIM, 1.0, 0.0)
    for j in range(A_KV_HEADS):
        khm_ref[j] = k[:, j * HEAD_DIM:(j + 1) * HEAD_DIM].astype(BF16)
        vj = v if j == 0 else pltpu.roll(v, LANES - j * HEAD_DIM, 1)
        vaug_ref[j] = jnp.where(first, vj, ones_col).astype(BF16)


def _in_proj(x, g, w_all, layer, tabs, t_len, tm):
    n = x.shape[0]
    nt = tabs[0].shape[0] // tm
    per_batch = t_len % tm == 0
    ntb = t_len // tm if per_batch else 1
    row = lambda i: (i, 0)
    tab = lambda i: (i % nt, 0)
    whole = lambda i: (0, 0)
    tmin = lambda i: (0, i)
    hmaj = lambda i: (0, i, 0)
    if per_batch:
        kv_spec = pl.BlockSpec((1, A_KV_HEADS, HEAD_DIM, tm), lambda i: (i // ntb, 0, 0, i % ntb))
        ik_spec = pl.BlockSpec((1, HEAD_DIM, tm), lambda i: (i // ntb, 0, i % ntb))
        kv_shape = jax.ShapeDtypeStruct((n // t_len, A_KV_HEADS, HEAD_DIM, t_len), F32)
        ik_shape = jax.ShapeDtypeStruct((n // t_len, HEAD_DIM, t_len), F32)
    else:
        kv_spec, ik_spec = pl.BlockSpec((KV_WIDTH, tm), tmin), pl.BlockSpec((HEAD_DIM, tm), tmin)
        kv_shape, ik_shape = jax.ShapeDtypeStruct((KV_WIDTH, n), F32), jax.ShapeDtypeStruct((HEAD_DIM, n), F32)
    return pl.pallas_call(
        _inproj_kernel,
        grid=(n // tm,),
        in_specs=[pl.BlockSpec((tm, D_MODEL), row), pl.BlockSpec((1, D_MODEL), whole),
                  pl.BlockSpec((1, D_MODEL, IN_WIDTH), lambda i: (layer, 0, 0), pipeline_mode=pl.Buffered(1)),
                  pl.BlockSpec((tm, LANES), tab), pl.BlockSpec((tm, LANES), tab), pl.BlockSpec((tm, LANES), tab)],
        out_specs=[pl.BlockSpec((tm, A_WIDTH), row), pl.BlockSpec((tm, 4 * B_WIDTH), row),
                   pl.BlockSpec((tm, IDX_WIDTH), row),
                   kv_spec, kv_spec, ik_spec,
                   pl.BlockSpec((A_KV_HEADS, tm, HEAD_DIM), hmaj), pl.BlockSpec((A_KV_HEADS, tm, LANES), hmaj),
                   pl.BlockSpec((tm, HEAD_DIM), row), pl.BlockSpec((8, tm), tmin)],
        out_shape=[jax.ShapeDtypeStruct((n, A_WIDTH), F32), jax.ShapeDtypeStruct((n, 4 * B_WIDTH), F32),
                   jax.ShapeDtypeStruct((n, IDX_WIDTH), F32),
                   kv_shape, kv_shape, ik_shape,
                   jax.ShapeDtypeStruct((A_KV_HEADS, n, HEAD_DIM), BF16),
                   jax.ShapeDtypeStruct((A_KV_HEADS, n, LANES), BF16),
                   jax.ShapeDtypeStruct((n, HEAD_DIM), BF16), jax.ShapeDtypeStruct((8, n), F32)],
        scratch_shapes=[pltpu.VMEM((D_MODEL, IN_WIDTH_PAD), BF16)],
        compiler_params=_cparams("arbitrary"),
        name="in_proj",
    )(x, g, w_all, *tabs)


def _rope_tables(pos):
    half = ROT_DIM // 2
    inv_freq = 1.0 / (ROPE_THETA ** (jnp.arange(half, dtype=F32) * (2.0 / ROT_DIM)))
    ang = pos.astype(F32)[:, None] * inv_freq[None, :]
    cos, sin = jnp.cos(ang), jnp.sin(ang)
    t = pos.shape[0]
    pad = HEAD_DIM - ROT_DIM
    c = jnp.concatenate([cos, cos, jnp.ones((t, pad), F32)], axis=1)
    s1 = jnp.concatenate([-sin, jnp.zeros((t, half + pad), F32)], axis=1)
    s2 = jnp.concatenate([jnp.zeros((t, half), F32), sin, jnp.zeros((t, pad), F32)], axis=1)
    rep = LANES // HEAD_DIM
    return tuple(jnp.tile(a, (1, rep)) for a in (c, s1, s2))


def _dsa_kernel(q_ref, iq_ref, iw_ref, k_ref, v_ref, ik_ref, o_ref, key_ref, hi_ref, lo_ref, acc_ref, m_ref,
                s0_ref, s1_ref, alpha_ref, p_ref, bias_ref, mt0_ref, mt1_ref, *, l_real, pos0, kb_rows, topk, tq):
    i = pl.program_id(1)
    qpos0 = pos0 + i * tq
    kend = jnp.minimum(((qpos0 + tq - 1) // CHUNK + 1) * CHUNK, l_real)
    cb_rows = 2 * kb_rows
    nkc = (kend + cb_rows - 1) // cb_rows
    nkb = 2 * nkc
    qchunk = (qpos0 + lax.broadcasted_iota(I32, (1, tq), 1)) // CHUNK
    row_iota = lax.broadcasted_iota(I32, (kb_rows, tq), 0)

    iq = iq_ref[...].astype(BF16)
    iq4 = jnp.concatenate([iq[:, h * HEAD_DIM:(h + 1) * HEAD_DIM] for h in range(IDX_HEADS)], axis=0)
    iw = iw_ref[...] * (IDX_HEADS ** -0.5 * HEAD_DIM ** -0.5)

    def score_body(kb, carry):
        r0 = pl.multiple_of(kb * kb_rows, kb_rows)
        ikb = ik_ref[pl.ds(r0, kb_rows), :]
        lg4 = lax.dot_general(ikb, iq4, _NT, preferred_element_type=F32)
        acc = jnp.zeros((kb_rows, tq), F32)
        for h in range(IDX_HEADS):
            acc = acc + iw[h:h + 1, :] * jnp.maximum(lg4[:, h * tq:(h + 1) * tq], 0.0)
        bits = pltpu.bitcast(acc + 0.0, I32)
        key = bits ^ ((bits >> 31) & 0x7FFFFFFF)
        kpos = r0 + row_iota
        adm = (kpos // CHUNK <= qchunk) & (kpos < l_real)
        key = jnp.where(adm, key, INT_MIN)
        key_ref[pl.ds(r0, kb_rows), :] = key
        hi_ref[pl.ds(r0, kb_rows), :] = (key >> 16).astype(I16)
        return carry

    lax.fori_loop(0, nkb, score_body, 0)

    def count(pred):
        def body(kc, c):
            r0 = pl.multiple_of(kc * cb_rows, cb_rows)
            hit = pred(key_ref[pl.ds(r0, cb_rows), :], r0)
            return c + jnp.where(hit, 1, 0).reshape(cb_rows // 8, 8, tq).sum(axis=0)
        c = lax.fori_loop(0, nkc, body, jnp.zeros((8, tq), I32))
        return c.sum(axis=0, keepdims=True)

    def count16(ref, pred):
        one, zero = jnp.ones((), BF16), jnp.zeros((), BF16)

        def body(kc, c):
            r0 = pl.multiple_of(kc * cb_rows, cb_rows)
            w = jnp.where(pred(ref[pl.ds(r0, cb_rows), :]), one, zero).reshape(cb_rows // 16, 16, tq)
            parts = [w[r] for r in range(cb_rows // 16)]
            while len(parts) > 1:
                parts = [a + b for a, b in zip(parts[::2], parts[1::2])] + parts[len(parts) & ~1:]
            return c + parts[0]
        c = lax.fori_loop(0, nkc, body, jnp.zeros((16, tq), BF16))
        return c.astype(F32).sum(axis=0, keepdims=True).astype(I32)

    def bisect16(ref, kth):
        c0 = count16(ref, lambda blk: blk >= jnp.zeros((), I16))
        start = jnp.where(c0 >= kth, 0, -32768).astype(I32)

        def bit_body(it, prefix):
            cand = prefix | lax.shift_left(jnp.int32(1), 14 - it)
            c = count16(ref, lambda blk: blk >= cand.astype(I16))
            return jnp.where(c >= kth, cand, prefix)

        return lax.fori_loop(0, 15, bit_body, start)

    tau_hi = bisect16(hi_ref, topk)
    tau_hi16 = tau_hi.astype(I16)
    kth_lo = topk - count16(hi_ref, lambda blk: blk > tau_hi16)

    def lo_body(kc, carry):
        r0 = pl.multiple_of(kc * cb_rows, cb_rows)
        lo = ((key_ref[pl.ds(r0, cb_rows), :] & 0xFFFF) - 32768).astype(I16)
        lo_ref[pl.ds(r0, cb_rows), :] = jnp.where(hi_ref[pl.ds(r0, cb_rows), :] == tau_hi16, lo,
                                                 jnp.full((), -32768, I16))
        return carry

    lax.fori_loop(0, nkc, lo_body, 0)
    tau_lo = bisect16(lo_ref, kth_lo)
    tau = lax.shift_left(tau_hi, 16) | (tau_lo + 32768)
    n_ge = count(lambda blk, r0: blk >= tau)

    @pl.when(jnp.max(jnp.where((n_ge > topk) & (tau != INT_MIN), 1, 0)) > 0)
    def _():
        need = (topk - count(lambda blk, r0: blk > tau)).astype(F32)
        ri = lax.broadcasted_iota(I32, (kb_rows, kb_rows), 0)
        ci = lax.broadcasted_iota(I32, (kb_rows, kb_rows), 1)
        upto = jnp.where(ci <= ri, 1.0, 0.0).astype(BF16)

        def strike(kc, seen):
            for half in range(2):
                rows = pl.ds(pl.multiple_of(kc * cb_rows + half * kb_rows, kb_rows), kb_rows)
                key = key_ref[rows, :]
                tie = key == tau
                rank = _dot(upto, jnp.where(tie, 1.0, 0.0).astype(BF16)) + seen
                key_ref[rows, :] = jnp.where(tie & (rank > need), INT_MIN, key)
                seen = rank[kb_rows - 1:kb_rows, :]
            return seen

        lax.fori_loop(0, nkc, strike, jnp.zeros((1, tq), F32))

    lane_rep = kb_rows // LANES
    tau_c = jnp.tile(jnp.broadcast_to(tau, (LANES, tq)).T, (1, lane_rep))
    acc_ref[...] = jnp.zeros(acc_ref.shape, F32)
    m_ref[...] = jnp.full(m_ref.shape, NEG_BIG, F32)
    qs = (q_ref[...] * (HEAD_DIM ** -0.5 * LOG2E)).astype(BF16)
    q4 = [jnp.concatenate([qs[:, (j * A_GROUP + g) * HEAD_DIM:(j * A_GROUP + g + 1) * HEAD_DIM]
                           for g in range(A_GROUP)], axis=0) for j in range(A_KV_HEADS)]

    def qk_scores(kb, s_slot, mt_slot):
        r0 = pl.multiple_of(jnp.minimum(kb, nkb - 1) * kb_rows, kb_rows)
        key_t = key_ref[pl.ds(r0, kb_rows), :].T
        bias_ref[...] = jnp.where((key_t >= tau_c) & (key_t != INT_MIN), 0.0, NEG_BIG)
        for j in range(A_KV_HEADS):
            s = lax.dot_general(q4[j], k_ref[j, pl.ds(r0, kb_rows), :], _NT, preferred_element_type=F32)
            for g in range(A_GROUP):
                rows = slice(g * tq, (g + 1) * tq)
                sm = s[rows, :] + bias_ref[...]
                s_slot[j, rows, :] = sm
                mt_slot[j, g] = jnp.broadcast_to(jnp.max(sm, axis=1, keepdims=True), (tq, LANES))

    def softmax_pv(kb, s_slot, mt_slot):
        r0 = pl.multiple_of(kb * kb_rows, kb_rows)
        for j in range(A_KV_HEADS):
            for g in range(A_GROUP):
                rows = slice(g * tq, (g + 1) * tq)
                m_old = m_ref[j, g]
                m_new = jnp.maximum(m_old, mt_slot[j, g])
                alpha_ref[j, g] = jnp.exp2(m_old - m_new)
                m_ref[j, g] = m_new
                p_ref[j, rows, :] = jnp.exp2(s_slot[j, rows, :] - jnp.tile(m_new, (1, lane_rep))).astype(BF16)
            pv = _dot(p_ref[j], v_ref[j, pl.ds(r0, kb_rows), :])
            acc_ref[j] = alpha_ref[j] * acc_ref[j] + pv.reshape(A_GROUP, tq, LANES)

    qk_scores(0, s0_ref, mt0_ref)

    def pair_body(i, carry):
        kb = 2 * i
        qk_scores(kb + 1, s1_ref, mt1_ref)
        softmax_pv(kb, s0_ref, mt0_ref)
        qk_scores(kb + 2, s0_ref, mt0_ref)
        softmax_pv(kb + 1, s1_ref, mt1_ref)
        return carry

    lax.fori_loop(0, nkc, pair_body, 0)

    outs = []
    for j in range(A_KV_HEADS):
        acc = acc_ref[j]
        for g in range(A_GROUP):
            outs.append(acc[g, :, :HEAD_DIM] * (1.0 / acc[g, :, HEAD_DIM:HEAD_DIM + 1]))
    o_ref[...] = jnp.concatenate(outs, axis=1)


def _dsa(q, iq, iw_t, k_hm, v_aug, ik, *, bsz, l_real, pos0, kb_rows, topk, tq):
    t_len = q.shape[0] // bsz
    l_pad = ik.shape[0] // bsz
    nq = t_len // tq
    assert l_pad // 16 <= 256, "packed hit counts are summed in bf16, exact only up to 256 per slot"
    kern = functools.partial(_dsa_kernel, l_real=l_real, pos0=pos0, kb_rows=kb_rows, topk=topk, tq=tq)
    qrow = lambda b, i: (b * nq + i, 0)
    return pl.pallas_call(
        kern,
        grid=(bsz, nq),
        in_specs=[pl.BlockSpec((tq, A_WIDTH), qrow), pl.BlockSpec((tq, IDX_WIDTH), qrow),
                  pl.BlockSpec((8, tq), lambda b, i: (0, b * nq + i)),
                  pl.BlockSpec((A_KV_HEADS, l_pad, HEAD_DIM), lambda b, i: (0, b, 0)),
                  pl.BlockSpec((A_KV_HEADS, l_pad, LANES), lambda b, i: (0, b, 0)),
                  pl.BlockSpec((l_pad, HEAD_DIM), lambda b, i: (b, 0))],
        out_specs=pl.BlockSpec((tq, A_WIDTH), qrow),
        out_shape=jax.ShapeDtypeStruct((bsz * t_len, A_WIDTH), F32),
        scratch_shapes=[pltpu.VMEM((l_pad, tq), I32), pltpu.VMEM((l_pad, tq), I16), pltpu.VMEM((l_pad, tq), I16),
                        pltpu.VMEM((A_KV_HEADS, A_GROUP, tq, LANES), F32),
                        pltpu.VMEM((A_KV_HEADS, A_GROUP, tq, LANES), F32),
                        pltpu.VMEM((A_KV_HEADS, A_GROUP * tq, kb_rows), F32),
                        pltpu.VMEM((A_KV_HEADS, A_GROUP * tq, kb_rows), F32),
                        pltpu.VMEM((A_KV_HEADS, A_GROUP, tq, LANES), F32),
                        pltpu.VMEM((A_KV_HEADS, A_GROUP * tq, kb_rows), BF16),
                        pltpu.VMEM((tq, kb_rows), F32),
                        pltpu.VMEM((A_KV_HEADS, A_GROUP, tq, LANES), F32),
                        pltpu.VMEM((A_KV_HEADS, A_GROUP, tq, LANES), F32)],
        compiler_params=_cparams("parallel", "arbitrary"),
        name="dsa",
    )(q, iq, iw_t, k_hm, v_aug, ik)


def _hgrn_kernel(bq_ref, bf_ref, bi_ref, bg_ref, lb_ref, gn_ref, s0_ref, o_ref, sfin_ref,
                 st_ref, q_s, kk_s, b_s, qe_s, kd_s, dec_s, o_s, e_s, *, layer, tb, nseq):
    tstep = pl.program_id(1)
    nchunk = tb // B_CHUNK

    @pl.when(tstep == 0)
    def _():
        st_ref[...] = s0_ref[...]

    lbr = lb_ref[...]
    lbe = jnp.exp(lbr - jnp.max(lbr, axis=0, keepdims=True))
    lbs = lbe / jnp.sum(lbe, axis=0, keepdims=True)
    cum = lbs[0:1, :]
    first = cum
    for r in range(1, layer + 1):
        cum = cum + lbs[r:r + 1, :]
    lb = cum - first

    ri = lax.broadcasted_iota(I32, (tb, tb), 0)
    ci = lax.broadcasted_iota(I32, (tb, tb), 1)
    same = (ri // B_CHUNK) == (ci // B_CHUNK)
    tri = jnp.where(same & (ci <= ri), 1.0, 0.0).astype(BF16)
    ones = jnp.where(same, 1.0, 0.0).astype(BF16)
    for u in range(nseq):
        f = lb + (1.0 - lb) * _sigmoid(bf_ref[u])
        bq = bq_ref[u]
        q = bq * _sigmoid(bq) * (HEAD_DIM ** -0.5)
        g3 = _split3(jnp.log(f) * LOG2E)
        b = _dot(tri, g3[0]) + _dot(tri, g3[1]) + _dot(tri, g3[2])
        blast = _dot(ones, g3[0]) + _dot(ones, g3[1]) + _dot(ones, g3[2])
        q_s[u] = q
        kk_s[u] = 1.0 - f
        b_s[u] = b
        qe_s[u] = q * jnp.exp2(b)
        kd_s[u] = (1.0 - f) * jnp.exp2(blast - b)
        dec_s[u] = jnp.exp2(blast)

    hi = lax.broadcasted_iota(I32, (HALF, HALF), 0) // HEAD_DIM
    hj = lax.broadcasted_iota(I32, (HALF, HALF), 1) // HEAD_DIM
    bd_mask = hi == hj
    bd = jnp.where(bd_mask, 1.0, 0.0).astype(BF16)
    t_iota = lax.broadcasted_iota(I32, (B_CHUNK, B_WIDTH), 0)

    def seg_sum(x, two_pass):
        out = []
        for half in range(2):
            xs = x[:, half * HALF:(half + 1) * HALF]
            if two_pass:
                xh, xl = _split2(xs)
                out.append(_dot(xh, bd) + _dot(xl, bd))
            else:
                out.append(_dot(xs.astype(BF16), bd))
        return jnp.concatenate(out, axis=1)

    def chunk_body(c, carry):
        r0 = pl.multiple_of(c * B_CHUNK, B_CHUNK)
        rows = pl.ds(r0, B_CHUNK)
        for u in range(nseq):
            qc, kc, bc, vc = q_s[u, rows, :], kk_s[u, rows, :], b_s[u, rows, :], bi_ref[u, rows, :]
            qe, kd = qe_s[u, rows, :].astype(BF16), kd_s[u, rows, :].astype(BF16)
            dec = dec_s[u, pl.ds(r0, 1), :]
            for s in range(B_CHUNK):
                ex = jnp.exp2(jnp.where(t_iota >= s, bc - bc[s:s + 1, :], NEG_BIG))
                e_s[u, s * B_CHUNK:(s + 1) * B_CHUNK, :] = qc * kc[s:s + 1, :] * ex
            aexp = seg_sum(e_s[u], False)
            o = jnp.zeros((B_CHUNK, B_WIDTH), F32)
            for s in range(B_CHUNK):
                o = o + aexp[s * B_CHUNK:(s + 1) * B_CHUNK, :] * vc[s:s + 1, :]
            inter = []
            vb = vc.astype(BF16)
            for half in range(2):
                cols = slice(half * HALF, (half + 1) * HALF)
                st = st_ref[u, half]
                inter.append(lax.dot_general(qe[:, cols], st.astype(BF16), _NT, preferred_element_type=F32))
                upd = lax.dot_general(vb[:, cols], kd[:, cols], _TN, preferred_element_type=F32)
                st_ref[u, half] = st * dec[:, cols] + jnp.where(bd_mask, upd, 0.0)
            o_s[u, rows, :] = o + jnp.concatenate(inter, axis=1)
        return carry

    lax.fori_loop(0, nchunk, chunk_body, 0)

    for u in range(nseq):
        o = o_s[u]
        ms = seg_sum(o * o, True) * (1.0 / HEAD_DIM)
        bg = bg_ref[u]
        o_ref[u] = o * lax.rsqrt(ms + EPS) * gn_ref[...] * (bg * _sigmoid(bg))

    @pl.when(tstep == pl.num_programs(1) - 1)
    def _():
        sfin_ref[...] = st_ref[...]


def _hgrn(hb, lb_raw, gn_tiled, s0_bd, *, layer, bsz, t_len, tb):
    nt = t_len // tb
    nseq = 2 if bsz % 2 == 0 else 1
    hb3 = hb.reshape(bsz, t_len, 4 * B_WIDTH)
    col = lambda j: (lambda b, t: (b, t, j))
    scr = lambda r: pltpu.VMEM((nseq, r, B_WIDTH), F32)
    kern = functools.partial(_hgrn_kernel, layer=layer, tb=tb, nseq=nseq)
    blk = (nseq, tb, B_WIDTH)
    out, s_fin = pl.pallas_call(
        kern,
        grid=(bsz // nseq, nt),
        in_specs=[pl.BlockSpec(blk, col(0)), pl.BlockSpec(blk, col(1)), pl.BlockSpec(blk, col(2)),
                  pl.BlockSpec(blk, col(3)),
                  pl.BlockSpec(lb_raw.shape, lambda b, t: (0, 0)),
                  pl.BlockSpec((1, B_WIDTH), lambda b, t: (0, 0)),
                  pl.BlockSpec((nseq, 2, HALF, HALF), lambda b, t: (b, 0, 0, 0))],
        out_specs=[pl.BlockSpec(blk, lambda b, t: (b, t, 0)),
                   pl.BlockSpec((nseq, 2, HALF, HALF), lambda b, t: (b, 0, 0, 0))],
        out_shape=[jax.ShapeDtypeStruct((bsz, t_len, B_WIDTH), F32),
                   jax.ShapeDtypeStruct((bsz, 2, HALF, HALF), F32)],
        scratch_shapes=[pltpu.VMEM((nseq, 2, HALF, HALF), F32), scr(tb), scr(tb), scr(tb), scr(tb), scr(tb),
                        scr(tb), scr(tb), scr(B_CHUNK * B_CHUNK)],
        compiler_params=_cparams("parallel", "arbitrary"),
        name="hgrn2",
    )(hb3, hb3, hb3, hb3, lb_raw, gn_tiled, s0_bd)
    return out.reshape(bsz * t_len, B_WIDTH), s_fin


def _state_to_tiles(s):
    bsz = s.shape[0]
    st = jnp.swapaxes(s, -1, -2).reshape(bsz, 2, B_HEADS // 2, HEAD_DIM, HEAD_DIM)
    eye = jnp.eye(B_HEADS // 2, dtype=s.dtype)
    t = st[:, :, :, :, None, :] * eye[None, None, :, None, :, None]
    return t.reshape(bsz, 2, HALF, HALF)


def _tiles_to_state(t):
    bsz = t.shape[0]
    t6 = t.reshape(bsz, 2, B_HEADS // 2, HEAD_DIM, B_HEADS // 2, HEAD_DIM)
    diag = jnp.stack([t6[:, :, h, :, h, :] for h in range(B_HEADS // 2)], axis=2)
    return jnp.swapaxes(diag.reshape(bsz, B_HEADS, HEAD_DIM, HEAD_DIM), -1, -2)


def _outproj_kernel(a_ref, b_ref, wa_ref, wb_ref, x_ref, o_ref):
    o_ref[...] = (x_ref[...] + _dot(a_ref[...].astype(BF16), wa_ref[...])
                  + _dot(b_ref[...].astype(BF16), wb_ref[...]))


def _out_proj(a, b, wa, wb, x, tm):
    n = x.shape[0]
    row = lambda i: (i, 0)
    whole = lambda i: (0, 0)
    return pl.pallas_call(
        _outproj_kernel,
        grid=(n // tm,),
        in_specs=[pl.BlockSpec((tm, A_WIDTH), row), pl.BlockSpec((tm, B_WIDTH), row),
                  pl.BlockSpec((A_WIDTH, D_MODEL), whole), pl.BlockSpec((B_WIDTH, D_MODEL), whole),
                  pl.BlockSpec((tm, D_MODEL), row)],
        out_specs=pl.BlockSpec((tm, D_MODEL), row),
        out_shape=jax.ShapeDtypeStruct((n, D_MODEL), F32),
        compiler_params=_cparams("parallel"),
        name="out_proj",
    )(a, b, wa, wb, x)


def _normmm_kernel(x_ref, g_ref, w_ref, o_ref):
    o_ref[...] = _dot(_rms(x_ref[...], g_ref[...]).astype(BF16), w_ref[...])


def _norm_matmul(x, g, w, tm):
    n, m = x.shape[0], w.shape[1]
    return pl.pallas_call(
        _normmm_kernel,
        grid=(n // tm,),
        in_specs=[pl.BlockSpec((tm, D_MODEL), lambda i: (i, 0)), pl.BlockSpec((1, D_MODEL), lambda i: (0, 0)),
                  pl.BlockSpec((D_MODEL, m), lambda i: (0, 0))],
        out_specs=pl.BlockSpec((tm, m), lambda i: (i, 0)),
        out_shape=jax.ShapeDtypeStruct((n, m), F32),
        compiler_params=_cparams("parallel"),
        name="norm_matmul",
    )(x, g, w)


def _memattn_kernel(x_ref, g_ref, wq_ref, mk_ref, mv_ref, wo_ref, o_ref):
    x = x_ref[...]
    q = _dot(_rms(x, g_ref[...]).astype(BF16), wq_ref[...])
    heads = []
    for h in range(MEM_HEADS):
        cols = slice(h * MEM_HEAD_DIM, (h + 1) * MEM_HEAD_DIM)
        s = lax.dot_general(q[:, cols].astype(BF16), mk_ref[0, :, cols], _NT,
                            preferred_element_type=F32) * (MEM_HEAD_DIM ** -0.5)
        p = jnp.exp(s - jnp.max(s, axis=-1, keepdims=True))
        p = p * (1.0 / jnp.sum(p, axis=-1, keepdims=True))
        heads.append(_dot(p.astype(BF16), mv_ref[0, :, cols]).astype(BF16))
    o_ref[...] = x + _dot(jnp.concatenate(heads, axis=1), wo_ref[...])


def _mem_attn(x, g, wq, mk, mv, wo, rows_per_batch, tm):
    n = x.shape[0]
    n_mem = mk.shape[1]
    per = rows_per_batch // tm
    row = lambda i: (i, 0)
    whole = lambda i: (0, 0)
    mem = lambda i: (i // per, 0, 0)
    return pl.pallas_call(
        _memattn_kernel,
        grid=(n // tm,),
        in_specs=[pl.BlockSpec((tm, D_MODEL), row), pl.BlockSpec((1, D_MODEL), whole),
                  pl.BlockSpec((D_MODEL, D_MODEL), whole),
                  pl.BlockSpec((1, n_mem, D_MODEL), mem), pl.BlockSpec((1, n_mem, D_MODEL), mem),
                  pl.BlockSpec((D_MODEL, D_MODEL), whole)],
        out_specs=pl.BlockSpec((tm, D_MODEL), row),
        out_shape=jax.ShapeDtypeStruct((n, D_MODEL), F32),
        compiler_params=_cparams("parallel"),
        name="mem_attn",
    )(x, g, wq, mk, mv, wo)


def _swiglu_mid(xn, w1, w3):
    a = _dot(xn, w1)
    return (a * _sigmoid(a) * _dot(xn, w3)).astype(BF16)


def _ffn_kernel(x_ref, g_ref, w1_ref, w3_ref, w2_ref, gf_ref, o_ref, *, final, chunks):
    x = x_ref[...]
    xn = _rms(x, g_ref[...]).astype(BF16)
    y = x
    lo = 0
    for width in chunks:
        cols = slice(lo, lo + width)
        y = y + _dot(_swiglu_mid(xn, w1_ref[:, cols], w3_ref[:, cols]), w2_ref[cols, :])
        lo += width
    o_ref[...] = _rms(y, gf_ref[...]) if final else y


def _ffn(x, g, w1, w3, w2, gf, *, final, tm):
    n = x.shape[0]
    d_ff = w1.shape[1]
    step = 4 * HALF
    chunks = tuple(min(step, d_ff - lo) for lo in range(0, d_ff, step))
    row = lambda i: (i, 0)
    whole = lambda i: (0, 0)
    once = pl.Buffered(1)
    return pl.pallas_call(
        functools.partial(_ffn_kernel, final=final, chunks=chunks),
        grid=(n // tm,),
        in_specs=[pl.BlockSpec((tm, D_MODEL), row), pl.BlockSpec((1, D_MODEL), whole),
                  pl.BlockSpec((D_MODEL, d_ff), whole, pipeline_mode=once),
                  pl.BlockSpec((D_MODEL, d_ff), whole, pipeline_mode=once),
                  pl.BlockSpec((d_ff, D_MODEL), whole, pipeline_mode=once), pl.BlockSpec((1, D_MODEL), whole)],
        out_specs=pl.BlockSpec((tm, D_MODEL), row),
        out_shape=jax.ShapeDtypeStruct((n, D_MODEL), F32),
        compiler_params=_cparams("parallel"),
        name="ffn",
    )(x, g, w1, w3, w2, gf)


def _router_kernel(x_ref, g_ref, rt_ref, comb_ref, sel_ref):
    xn = _rms(x_ref[...], g_ref[...])
    logits = lax.dot_general(rt_ref[...], xn, _NT, preferred_element_type=F32,
                             precision=lax.Precision.HIGHEST)
    e_idx = lax.broadcasted_iota(I32, logits.shape, 0)
    m1 = jnp.max(logits, axis=0, keepdims=True)
    i1 = jnp.min(jnp.where(logits == m1, e_idx, N_EXPERTS), axis=0, keepdims=True)
    rest = jnp.where(e_idx == i1, -jnp.inf, logits)
    m2 = jnp.max(rest, axis=0, keepdims=True)
    i2 = jnp.min(jnp.where(rest == m2, e_idx, N_EXPERTS), axis=0, keepdims=True)
    e2 = jnp.exp(m2 - m1)
    g1 = 1.0 / (1.0 + e2)
    comb_ref[...] = jnp.where(e_idx == i1, g1, 0.0) + jnp.where(e_idx == i2, e2 * g1, 0.0)
    sel_ref[...] = jnp.where((e_idx == i1) | (e_idx == i2), 1.0, 0.0)


def _router(x, g, rt, tm):
    n = x.shape[0]
    return pl.pallas_call(
        _router_kernel,
        grid=(n // tm,),
        in_specs=[pl.BlockSpec((tm, D_MODEL), lambda i: (i, 0)), pl.BlockSpec((1, D_MODEL), lambda i: (0, 0)),
                  pl.BlockSpec((N_EXPERTS, D_MODEL), lambda i: (0, 0))],
        out_specs=[pl.BlockSpec((N_EXPERTS, tm), lambda i: (0, i)), pl.BlockSpec((N_EXPERTS, tm), lambda i: (0, i))],
        out_shape=[jax.ShapeDtypeStruct((N_EXPERTS, n), F32), jax.ShapeDtypeStruct((N_EXPERTS, n), F32)],
        compiler_params=_cparams("parallel"),
        name="moe_router",
    )(x, g, rt)


def _moe_kernel(x_ref, g_ref, selt_ref, sel_ref, comb_ref, w1_ref, w3_ref, w2_ref, gf_ref, o_ref,
                xn_ref, posr_ref, posc_ref, cnt_ref, xg_ref, yg_ref, *, final, nb, sub):
    e = pl.program_id(1)
    j = pl.program_id(2)
    last_j = pl.num_programs(2) - 1

    @pl.when((e == 0) & (j == 0))
    def _():
        x = x_ref[...]
        xn_ref[...] = _rms(x, g_ref[...]).astype(BF16)
        o_ref[...] = x
        cc = min(nb, 256)
        ri = lax.broadcasted_iota(I32, (cc, cc), 0)
        ci = lax.broadcasted_iota(I32, (cc, cc), 1)
        before_r = jnp.where(ri < ci, 1.0, 0.0).astype(BF16)
        before_c = jnp.where(ci < ri, 1.0, 0.0).astype(BF16)
        run_r = jnp.zeros((N_EXPERTS, 1), F32)
        run_c = jnp.zeros((1, N_EXPERTS), F32)
        for c in range(nb // cc):
            cols = slice(c * cc, (c + 1) * cc)
            mr = selt_ref[:, cols]
            posr_ref[:, cols] = jnp.where(mr > 0.0, _dot(mr.astype(BF16), before_r) + run_r, -1.0)
            run_r = run_r + jnp.sum(mr, axis=1, keepdims=True)
            mc = sel_ref[cols, :]
            posc_ref[cols, :] = jnp.where(mc > 0.0, _dot(before_c, mc.astype(BF16)) + run_c, -1.0)
            run_c = run_c + jnp.sum(mc, axis=0, keepdims=True)
        for k in range(N_EXPERTS):
            cnt_ref[k] = run_c[0, k].astype(I32)

    cnt = cnt_ref[e]
    nfull = cnt // sub
    rem = cnt - nfull * sub
    tail_sizes = [m for m in (64, 128) if m < sub] + [sub]

    def for_tiles(fn):
        def body(s, carry):
            fn(pl.multiple_of(s * sub, sub), sub)
            return carry
        lax.fori_loop(0, nfull, body, 0)
        r0 = pl.multiple_of(nfull * sub, sub)
        lo = 0
        for m in tail_sizes:
            @pl.when((rem > lo) & (rem <= m))
            def _():
                fn(r0, m)
            lo = m

    def gather_matrix(r0, m):
        slot_r = posr_ref[pl.ds(e, 1), :]
        want = (r0 + lax.broadcasted_iota(I32, (m, nb), 0)).astype(F32)
        return jnp.where(slot_r == want, 1.0, 0.0).astype(BF16)

    def gather(r0, m):
        xg_ref[pl.ds(r0, m), :] = _dot(gather_matrix(r0, m), xn_ref[...]).astype(BF16)

    def ffn(r0, m):
        rows = pl.ds(r0, m)
        part = _dot(_swiglu_mid(xg_ref[rows, :], w1_ref[0], w3_ref[0]), w2_ref[0])

        @pl.when(j == 0)
        def _():
            yg_ref[rows, :] = part

        @pl.when(j > 0)
        def _():
            yg_ref[rows, :] += part

    @pl.when(j == 0)
    def _():
        for_tiles(gather)

    for_tiles(ffn)

    @pl.when(j == last_j)
    def _():
        lane8 = lax.broadcasted_iota(I32, (nb, N_EXPERTS), 1)
        slot_c = jnp.sum(jnp.where(lane8 == e, posc_ref[...], 0.0), axis=1, keepdims=True)
        gate_c = jnp.sum(jnp.where(lane8 == e, comb_ref[...], 0.0), axis=1, keepdims=True)
        gate3 = jnp.concatenate(_split3(gate_c), axis=1)

        def scatter(r0, m):
            gate = jnp.sum(_dot(gather_matrix(r0, m), gate3), axis=1, keepdims=True)
            y = (yg_ref[pl.ds(r0, m), :] * gate).astype(BF16)
            want = (r0 + lax.broadcasted_iota(I32, (nb, m), 1)).astype(F32)
            o_ref[...] += _dot(jnp.where(slot_c == want, 1.0, 0.0).astype(BF16), y)

        for_tiles(scatter)

        if final:
            @pl.when(e == pl.num_programs(1) - 1)
            def _():
                o_ref[...] = _rms(o_ref[...], gf_ref[...])


def _moe(x, g, sel_t, sel, comb, w1, w3, w2, gf, *, final, nb, tf):
    n = x.shape[0]
    d_ff = w1.shape[2]
    sub = min(nb, MOE_TILE_ROWS)
    row = lambda i, e, j: (i, 0)
    whole = lambda i, e, j: (0, 0)
    return pl.pallas_call(
        functools.partial(_moe_kernel, final=final, nb=nb, sub=sub),
        grid=(n // nb, N_EXPERTS, d_ff // tf),
        in_specs=[pl.BlockSpec((nb, D_MODEL), row), pl.BlockSpec((1, D_MODEL), whole),
                  pl.BlockSpec((N_EXPERTS, nb), lambda i, e, j: (0, i)),
                  pl.BlockSpec((nb, N_EXPERTS), row), pl.BlockSpec((nb, N_EXPERTS), row),
                  pl.BlockSpec((1, D_MODEL, tf), lambda i, e, j: (e, 0, j)),
                  pl.BlockSpec((1, D_MODEL, tf), lambda i, e, j: (e, 0, j)),
                  pl.BlockSpec((1, tf, D_MODEL), lambda i, e, j: (e, j, 0)), pl.BlockSpec((1, D_MODEL), whole)],
        out_specs=pl.BlockSpec((nb, D_MODEL), row),
        out_shape=jax.ShapeDtypeStruct((n, D_MODEL), F32),
        scratch_shapes=[pltpu.VMEM((nb, D_MODEL), BF16), pltpu.VMEM((N_EXPERTS, nb), F32),
                        pltpu.VMEM((nb, N_EXPERTS), F32), pltpu.SMEM((N_EXPERTS,), I32),
                        pltpu.VMEM((nb + sub, D_MODEL), BF16), pltpu.VMEM((nb + sub, D_MODEL), F32)],
        compiler_params=_cparams("parallel", "arbitrary", "arbitrary"),
        name="moe",
    )(x, g, sel_t, sel, comb, w1, w3, w2, gf)


def _row_tile(n, cap):
    t = min(n, cap)
    assert n % t == 0, (n, cap)
    return t


def _ff_tile(d_ff):
    for parts in (2, 4, 11, 22):
        if d_ff % parts == 0 and (d_ff // parts) % LANES == 0:
            return d_ff // parts
    return d_ff


def _mixer(x, tabs, w, *, bsz, t_len, pos0, past, s0, layer):
    n = bsz * t_len
    qa, hb, iq, k_t, v_t, ik_t, k_hm, v_aug, ik_b, iw_t = _in_proj(x, w["norm_mix"], w["w_in"], layer, tabs,
                                                                   t_len, _row_tile(n, 512))
    l_real = t_len if past is None else past[0].shape[1] + t_len
    topk = min(TOPK_MAX, l_real // 4)
    tq = 2 * LANES if t_len % (2 * LANES) == 0 else LANES
    kb_rows = 256 if l_real % 256 == 0 else LANES
    l_pad = -(-l_real // (2 * kb_rows)) * (2 * kb_rows)
    t_pad = -(-t_len // tq) * tq
    if past is not None:
        pk, pv, pik = past
        one = jnp.ones(pv.shape[:2] + (A_KV_HEADS, 1), BF16)
        zero = jnp.zeros(pv.shape[:2] + (A_KV_HEADS, LANES - HEAD_DIM - 1), BF16)
        pv_aug = jnp.concatenate([pv.astype(BF16), one, zero], axis=-1)
        hm = lambda a: jnp.transpose(a, (2, 0, 1, 3))
        k_hm = jnp.concatenate([hm(pk.astype(BF16)), k_hm.reshape(A_KV_HEADS, bsz, t_len, HEAD_DIM)], axis=2)
        v_aug = jnp.concatenate([hm(pv_aug), v_aug.reshape(A_KV_HEADS, bsz, t_len, LANES)], axis=2)
        ik_b = jnp.concatenate([pik.astype(BF16), ik_b.reshape(bsz, t_len, HEAD_DIM)], axis=1)
    if l_pad != l_real or past is not None:
        padl = l_pad - l_real
        k_hm = jnp.pad(k_hm.reshape(A_KV_HEADS, bsz, l_real, HEAD_DIM), ((0, 0), (0, 0), (0, padl), (0, 0)))
        v_aug = jnp.pad(v_aug.reshape(A_KV_HEADS, bsz, l_real, LANES), ((0, 0), (0, 0), (0, padl), (0, 0)))
        ik_b = jnp.pad(ik_b.reshape(bsz, l_real, HEAD_DIM), ((0, 0), (0, padl), (0, 0)))
        k_hm = k_hm.reshape(A_KV_HEADS, bsz * l_pad, HEAD_DIM)
        v_aug = v_aug.reshape(A_KV_HEADS, bsz * l_pad, LANES)
        ik_b = ik_b.reshape(bsz * l_pad, HEAD_DIM)
    q_in, iq_in, iw_in = qa, iq, iw_t
    if t_pad != t_len:
        padq = lambda a: jnp.pad(a.reshape(bsz, t_len, -1),
                                 ((0, 0), (0, t_pad - t_len), (0, 0))).reshape(bsz * t_pad, -1)
        q_in, iq_in = padq(qa), padq(iq)
        iw_in = jnp.pad(iw_t.reshape(8, bsz, t_len), ((0, 0), (0, 0), (0, t_pad - t_len))).reshape(8, bsz * t_pad)
    oa = _dsa(q_in, iq_in, iw_in, k_hm, v_aug, ik_b, bsz=bsz, l_real=l_real, pos0=pos0, kb_rows=kb_rows,
              topk=topk, tq=tq)
    if t_pad != t_len:
        oa = oa.reshape(bsz, t_pad, A_WIDTH)[:, :t_len].reshape(n, A_WIDTH)
    ob, s_tiles = _hgrn(hb, w["hgrn_lb"], w["gnorm"], _state_to_tiles(s0), layer=layer, bsz=bsz, t_len=t_len,
                        tb=_row_tile(t_len, 256))
    x1 = _out_proj(oa, ob, w["w_out_a"], w["w_out_b"], x, _row_tile(n, 512))
    return x1, k_t, v_t, ik_t, _tiles_to_state(s_tiles)


def _from_time_minor(a, bsz, t_len, heads):
    if a.ndim == 2:
        a = a.reshape((-1, HEAD_DIM, bsz, t_len))
        a = jnp.transpose(a, (2, 3, 0, 1))
    else:
        a = jnp.moveaxis(a, -1, 1)
        a = a.reshape(bsz, t_len, -1, HEAD_DIM)
    return a if heads is not None else a.reshape(bsz, t_len, HEAD_DIM)


def kernel(x_prompt, x_sample, cache_k, cache_v, cache_idx_k, state_hgrn, cache_mem_k, cache_mem_v, mem_prompt,
           norm_mix, w_in, hgrn_lb, hgrn_gnorm, w_out, norm_mem, norm_memkv, w_mq, w_mk, w_mv, w_mo,
           norm_ffn, ffn_w1, ffn_w3, ffn_w2, moe_router, moe_w1, moe_w3, moe_w2, norm_final):
    depth = w_in.shape[0]
    bp, tp, _ = x_prompt.shape
    bs, ts, _ = x_sample.shape
    past = cache_k.shape[2]
    n_mem = mem_prompt.shape[1]
    tabs_p = _rope_tables(jnp.arange(tp, dtype=I32))
    tabs_s = tuple(jnp.tile(a, (bs, 1)) for a in _rope_tables(past + jnp.arange(ts, dtype=I32)))
    xp = x_prompt.reshape(bp * tp, D_MODEL)
    xs = x_sample.reshape(bs * ts, D_MODEL)
    mem = mem_prompt.reshape(bp * n_mem, D_MODEL)
    s0_p = jnp.zeros((bp, B_HEADS, HEAD_DIM, HEAD_DIM), F32)
    gfin = norm_final.reshape(1, D_MODEL)
    row2 = lambda a: a.reshape(1, -1)
    outs = {k: [] for k in ("kp", "vp", "ikp", "sp", "mkp", "mvp", "ks", "vs", "iks", "ss")}
    for l in range(depth):
        w = {"norm_mix": row2(norm_mix[l]), "w_in": w_in, "hgrn_lb": hgrn_lb.astype(F32),
             "gnorm": row2(jnp.tile(hgrn_gnorm[l], B_HEADS)),
             "w_out_a": w_out[l, :A_WIDTH].astype(BF16), "w_out_b": w_out[l, A_WIDTH:].astype(BF16)}
        xp, kp, vp, ikp, sp = _mixer(xp, tabs_p, w, bsz=bp, t_len=tp, pos0=0, past=None, s0=s0_p, layer=l)
        xs, ks, vs, iks, ss = _mixer(xs, tabs_s, w, bsz=bs, t_len=ts, pos0=past,
                                     past=(cache_k[l], cache_v[l], cache_idx_k[l]), s0=state_hgrn[l], layer=l)
        wkv = jnp.concatenate([w_mk[l], w_mv[l]], axis=1).astype(BF16)
        mkv = _norm_matmul(mem, row2(norm_memkv[l]), wkv, _row_tile(bp * n_mem, 256))
        mk_p = mkv[:, :D_MODEL].reshape(bp, n_mem, D_MODEL)
        mv_p = mkv[:, D_MODEL:].reshape(bp, n_mem, D_MODEL)
        wq, wo, gm = w_mq[l].astype(BF16), w_mo[l].astype(BF16), row2(norm_mem[l])
        xp = _mem_attn(xp, gm, wq, mk_p.astype(BF16), mv_p.astype(BF16), wo, tp, _row_tile(tp, 512))
        xs = _mem_attn(xs, gm, wq, cache_mem_k[l].reshape(bs, n_mem, D_MODEL).astype(BF16),
                       cache_mem_v[l].reshape(bs, n_mem, D_MODEL).astype(BF16), wo, ts, _row_tile(ts, 256))
        final = l == depth - 1
        gf = row2(norm_ffn[l])
        if l % 2 == 0:
            w1, w3, w2 = ffn_w1[l // 2].astype(BF16), ffn_w3[l // 2].astype(BF16), ffn_w2[l // 2].astype(BF16)
            xp = _ffn(xp, gf, w1, w3, w2, gfin, final=final, tm=_row_tile(bp * tp, 512))
            xs = _ffn(xs, gf, w1, w3, w2, gfin, final=final, tm=_row_tile(bs * ts, 512))
        else:
            w1, w3, w2 = moe_w1[l // 2].astype(BF16), moe_w3[l // 2].astype(BF16), moe_w2[l // 2].astype(BF16)
            rt = moe_router[l // 2].T
            tf = _ff_tile(w1.shape[2])
            comb_p, sel_p = _router(xp, gf, rt, _row_tile(bp * tp, 512))
            comb_s, sel_s = _router(xs, gf, rt, _row_tile(bs * ts, 512))
            xp = _moe(xp, gf, sel_p, sel_p.T, comb_p.T, w1, w3, w2, gfin, final=final,
                      nb=_row_tile(bp * tp, 1024), tf=tf)
            xs = _moe(xs, gf, sel_s, sel_s.T, comb_s.T, w1, w3, w2, gfin, final=final,
                      nb=_row_tile(bs * ts, 1024), tf=tf)
        outs["kp"].append(_from_time_minor(kp, bp, tp, A_KV_HEADS))
        outs["vp"].append(_from_time_minor(vp, bp, tp, A_KV_HEADS))
        outs["ikp"].append(_from_time_minor(ikp, bp, tp, None)); outs["sp"].append(sp)
        outs["mkp"].append(mk_p.reshape(bp, n_mem, MEM_HEADS, MEM_HEAD_DIM))
        outs["mvp"].append(mv_p.reshape(bp, n_mem, MEM_HEADS, MEM_HEAD_DIM))
        outs["ks"].append(_from_time_minor(ks, bs, ts, A_KV_HEADS))
        outs["vs"].append(_from_time_minor(vs, bs, ts, A_KV_HEADS))
        outs["iks"].append(_from_time_minor(iks, bs, ts, None)); outs["ss"].append(ss)
    y_prompt = xp.reshape(bp, tp, D_MODEL)
    y_sample = xs.reshape(bs, ts, D_MODEL)
    st = lambda k: jnp.stack(outs[k])
    return (y_prompt, y_sample, st("kp"), st("vp"), st("ikp"), st("sp"), st("mkp"), st("mvp"),
            st("ks"), st("vs"), st("iks"), st("ss"))
```

```python
import functools

import numpy as np
import jax
import jax.numpy as jnp
from jax import lax
from jax.experimental import pallas as pl
from jax.experimental.pallas import tpu as pltpu

F32 = jnp.float32
BF16 = jnp.bfloat16
I32 = jnp.int32
I16 = jnp.int16

D_MODEL = 1024
CHUNK = 64
EPS = 1e-6
ROPE_THETA = 500000.0
HEAD_DIM = 64
ROT_DIM = HEAD_DIM // 4
A_HEADS = 8
A_KV_HEADS = 2
A_GROUP = A_HEADS // A_KV_HEADS
A_WIDTH = A_HEADS * HEAD_DIM
KV_WIDTH = A_KV_HEADS * HEAD_DIM
IDX_HEADS = 4
IDX_WIDTH = IDX_HEADS * HEAD_DIM
TOPK_MAX = 256
B_HEADS = 8
B_WIDTH = B_HEADS * HEAD_DIM
B_CHUNK = 16
MEM_HEADS = 4
MEM_HEAD_DIM = D_MODEL // MEM_HEADS
N_EXPERTS = 8
MOE_TILE_ROWS = 256
LANES = 128
HALF = 256
IN_SPLITS = (A_WIDTH, KV_WIDTH, KV_WIDTH, IDX_WIDTH, HEAD_DIM, IDX_HEADS, B_WIDTH, B_WIDTH, B_WIDTH, B_WIDTH)
IN_WIDTH = sum(IN_SPLITS)
IN_WIDTH_PAD = 3200
VMEM_LIMIT = 56 * 1024 * 1024
INT_MIN = -2147483648
NEG_BIG = -1e30
LOG2E = 1.4426950408889634

_NT = (((1,), (1,)), ((), ()))
_TN = (((0,), (0,)), ((), ()))


def _cparams(*sem):
    return pltpu.CompilerParams(dimension_semantics=sem, vmem_limit_bytes=VMEM_LIMIT)


def _rms(x, g):
    return x * lax.rsqrt(jnp.mean(x * x, axis=-1, keepdims=True) + EPS) * g


def _sigmoid(x):
    return 1.0 / (1.0 + jnp.exp(-x))


def _dot(a, b):
    return jnp.dot(a, b, preferred_element_type=F32)


def _split2(x):
    hi = x.astype(BF16)
    lo = (x - hi.astype(F32)).astype(BF16)
    return hi, lo


def _split3(x):
    hi = x.astype(BF16)
    r = x - hi.astype(F32)
    mid = r.astype(BF16)
    lo = (r - mid.astype(F32)).astype(BF16)
    return hi, mid, lo


def _rope128(xb, c, s1, s2):
    return xb * c + pltpu.roll(xb, LANES - ROT_DIM // 2, 1) * s1 + pltpu.roll(xb, ROT_DIM // 2, 1) * s2


def _inproj_kernel(x_ref, g_ref, wraw_ref, c_ref, s1_ref, s2_ref,
                   qa_ref, hb_ref, iq_ref, kt_ref, vt_ref, ikt_ref, khm_ref, vaug_ref, ikb_ref, iwt_ref, w_ref):
    @pl.when(pl.program_id(0) == 0)
    def _():
        src = np.cumsum([0] + list(IN_SPLITS))
        dst = 0
        for i in (0, 6, 7, 8, 9, 1, 2, 3):
            width = int(src[i + 1] - src[i])
            w_ref[:, dst:dst + width] = wraw_ref[0, :, int(src[i]):int(src[i + 1])].astype(BF16)
            dst += width
        tail = jnp.concatenate([wraw_ref[0, :, int(src[4]):int(src[6])],
                                jnp.zeros((D_MODEL, LANES - HEAD_DIM - IDX_HEADS), F32)], axis=1)
        w_ref[:, dst:dst + LANES] = tail.astype(BF16)

    xn = _rms(x_ref[...], g_ref[...]).astype(BF16)
    c, s1, s2 = c_ref[...], s1_ref[...], s2_ref[...]
    for j in range(A_WIDTH // HALF):
        h = _dot(xn, w_ref[:, j * HALF:(j + 1) * HALF])
        for i in range(HALF // LANES):
            lo = j * HALF + i * LANES
            qa_ref[:, lo:lo + LANES] = _rope128(h[:, i * LANES:(i + 1) * LANES], c, s1, s2)
    for j in range(4):
        lo = A_WIDTH + j * B_WIDTH
        hb_ref[:, j * B_WIDTH:(j + 1) * B_WIDTH] = _dot(xn, w_ref[:, lo:lo + B_WIDTH])
    base = A_WIDTH + 4 * B_WIDTH
    kv = _dot(xn, w_ref[:, base:base + HALF])
    k = _rope128(kv[:, :LANES], c, s1, s2)
    v = kv[:, LANES:]
    iq = _dot(xn, w_ref[:, base + HALF:base + 2 * HALF])
    iq_ref[:, 0:128] = _rope128(iq[:, :LANES], c, s1, s2)
    iq_ref[:, 128:256] = _rope128(iq[:, LANES:], c, s1, s2)
    lane = lax.broadcasted_iota(I32, c.shape, 1)
    first = lane < HEAD_DIM
    ikw = _rope128(_dot(xn, w_ref[:, base + 512:base + 640]),
                   jnp.where(first, c, 1.0), jnp.where(first, s1, 0.0), jnp.where(first, s2, 0.0))
    kt_ref[...] = k.T.reshape(kt_ref.shape)
    vt_ref[...] = v.T.reshape(vt_ref.shape)
    ikw_t = ikw.T
    ikt_ref[...] = ikw_t[:HEAD_DIM].reshape(ikt_ref.shape)
    iwt_ref[...] = ikw_t[HEAD_DIM:HEAD_DIM + 8]
    ikb_ref[...] = ikw[:, :HEAD_DIM].astype(BF16)
    ones_col = jnp.where(lane == HEAD_DIM, 1.0, 0.0)
    for j in range(A_KV_HEADS):
        khm_ref[j] = k[:, j * HEAD_DIM:(j + 1) * HEAD_DIM].astype(BF16)
        vj = v if j == 0 else pltpu.roll(v, LANES - j * HEAD_DIM, 1)
        vaug_ref[j] = jnp.where(first, vj, ones_col).astype(BF16)


def _in_proj(x, g, w_all, layer, tabs, t_len, tm):
    n = x.shape[0]
    nt = tabs[0].shape[0] // tm
    per_batch = t_len % tm == 0
    ntb = t_len // tm if per_batch else 1
    row = lambda i: (i, 0)
    tab = lambda i: (i % nt, 0)
    whole = lambda i: (0, 0)
    tmin = lambda i: (0, i)
    hmaj = lambda i: (0, i, 0)
    if per_batch:
        kv_spec = pl.BlockSpec((1, A_KV_HEADS, HEAD_DIM, tm), lambda i: (i // ntb, 0, 0, i % ntb))
        ik_spec = pl.BlockSpec((1, HEAD_DIM, tm), lambda i: (i // ntb, 0, i % ntb))
        kv_shape = jax.ShapeDtypeStruct((n // t_len, A_KV_HEADS, HEAD_DIM, t_len), F32)
        ik_shape = jax.ShapeDtypeStruct((n // t_len, HEAD_DIM, t_len), F32)
    else:
        kv_spec, ik_spec = pl.BlockSpec((KV_WIDTH, tm), tmin), pl.BlockSpec((HEAD_DIM, tm), tmin)
        kv_shape, ik_shape = jax.ShapeDtypeStruct((KV_WIDTH, n), F32), jax.ShapeDtypeStruct((HEAD_DIM, n), F32)
    return pl.pallas_call(
        _inproj_kernel,
        grid=(n // tm,),
        in_specs=[pl.BlockSpec((tm, D_MODEL), row), pl.BlockSpec((1, D_MODEL), whole),
                  pl.BlockSpec((1, D_MODEL, IN_WIDTH), lambda i: (layer, 0, 0), pipeline_mode=pl.Buffered(1)),
                  pl.BlockSpec((tm, LANES), tab), pl.BlockSpec((tm, LANES), tab), pl.BlockSpec((tm, LANES), tab)],
        out_specs=[pl.BlockSpec((tm, A_WIDTH), row), pl.BlockSpec((tm, 4 * B_WIDTH), row),
                   pl.BlockSpec((tm, IDX_WIDTH), row),
                   kv_spec, kv_spec, ik_spec,
                   pl.BlockSpec((A_KV_HEADS, tm, HEAD_DIM), hmaj), pl.BlockSpec((A_KV_HEADS, tm, LANES), hmaj),
                   pl.BlockSpec((tm, HEAD_DIM), row), pl.BlockSpec((8, tm), tmin)],
        out_shape=[jax.ShapeDtypeStruct((n, A_WIDTH), F32), jax.ShapeDtypeStruct((n, 4 * B_WIDTH), F32),
                   jax.ShapeDtypeStruct((n, IDX_WIDTH), F32),
                   kv_shape, kv_shape, ik_shape,
                   jax.ShapeDtypeStruct((A_KV_HEADS, n, HEAD_DIM), BF16),
                   jax.ShapeDtypeStruct((A_KV_HEADS, n, LANES), BF16),
                   jax.ShapeDtypeStruct((n, HEAD_DIM), BF16), jax.ShapeDtypeStruct((8, n), F32)],
        scratch_shapes=[pltpu.VMEM((D_MODEL, IN_WIDTH_PAD), BF16)],
        compiler_params=_cparams("arbitrary"),
        name="in_proj",
    )(x, g, w_all, *tabs)


def _rope_tables(pos):
    half = ROT_DIM // 2
    inv_freq = 1.0 / (ROPE_THETA ** (jnp.arange(half, dtype=F32) * (2.0 / ROT_DIM)))
    ang = pos.astype(F32)[:, None] * inv_freq[None, :]
    cos, sin = jnp.cos(ang), jnp.sin(ang)
    t = pos.shape[0]
    pad = HEAD_DIM - ROT_DIM
    c = jnp.concatenate([cos, cos, jnp.ones((t, pad), F32)], axis=1)
    s1 = jnp.concatenate([-sin, jnp.zeros((t, half + pad), F32)], axis=1)
    s2 = jnp.concatenate([jnp.zeros((t, half), F32), sin, jnp.zeros((t, pad), F32)], axis=1)
    rep = LANES // HEAD_DIM
    return tuple(jnp.tile(a, (1, rep)) for a in (c, s1, s2))


def _dsa_kernel(q_ref, iq_ref, iw_ref, k_ref, v_ref, ik_ref, o_ref, key_ref, hi_ref, lo_ref, acc_ref, m_ref,
                s0_ref, s1_ref, alpha_ref, p_ref, bias_ref, mt0_ref, mt1_ref, *, l_real, pos0, kb_rows, topk, tq):
    i = pl.program_id(1)
    qpos0 = pos0 + i * tq
    kend = jnp.minimum(((qpos0 + tq - 1) // CHUNK + 1) * CHUNK, l_real)
    cb_rows = 2 * kb_rows
    nkc = (kend + cb_rows - 1) // cb_rows
    nkb = 2 * nkc
    qchunk = (qpos0 + lax.broadcasted_iota(I32, (1, tq), 1)) // CHUNK
    row_iota = lax.broadcasted_iota(I32, (kb_rows, tq), 0)

    iq = iq_ref[...].astype(BF16)
    iq4 = jnp.concatenate([iq[:, h * HEAD_DIM:(h + 1) * HEAD_DIM] for h in range(IDX_HEADS)], axis=0)
    iw = iw_ref[...] * (IDX_HEADS ** -0.5 * HEAD_DIM ** -0.5)

    def score_body(kb, carry):
        r0 = pl.multiple_of(kb * kb_rows, kb_rows)
        ikb = ik_ref[pl.ds(r0, kb_rows), :]
        lg4 = lax.dot_general(ikb, iq4, _NT, preferred_element_type=F32)
        acc = jnp.zeros((kb_rows, tq), F32)
        for h in range(IDX_HEADS):
            acc = acc + iw[h:h + 1, :] * jnp.maximum(lg4[:, h * tq:(h + 1) * tq], 0.0)
        bits = pltpu.bitcast(acc + 0.0, I32)
        key = bits ^ ((bits >> 31) & 0x7FFFFFFF)
        kpos = r0 + row_iota
        adm = (kpos // CHUNK <= qchunk) & (kpos < l_real)
        key = jnp.where(adm, key, INT_MIN)
        key_ref[pl.ds(r0, kb_rows), :] = key
        hi_ref[pl.ds(r0, kb_rows), :] = (key >> 16).astype(I16)
        return carry

    lax.fori_loop(0, nkb, score_body, 0)

    def count(pred):
        def body(kc, c):
            r0 = pl.multiple_of(kc * cb_rows, cb_rows)
            hit = pred(key_ref[pl.ds(r0, cb_rows), :], r0)
            return c + jnp.where(hit, 1, 0).reshape(cb_rows // 8, 8, tq).sum(axis=0)
        c = lax.fori_loop(0, nkc, body, jnp.zeros((8, tq), I32))
        return c.sum(axis=0, keepdims=True)

    def count16(ref, pred):
        one, zero = jnp.ones((), BF16), jnp.zeros((), BF16)

        def body(kc, c):
            r0 = pl.multiple_of(kc * cb_rows, cb_rows)
            w = jnp.where(pred(ref[pl.ds(r0, cb_rows), :]), one, zero).reshape(cb_rows // 16, 16, tq)
            parts = [w[r] for r in range(cb_rows // 16)]
            while len(parts) > 1:
                parts = [a + b for a, b in zip(parts[::2], parts[1::2])] + parts[len(parts) & ~1:]
            return c + parts[0]
        c = lax.fori_loop(0, nkc, body, jnp.zeros((16, tq), BF16))
        return c.astype(F32).sum(axis=0, keepdims=True).astype(I32)

    def bisect16(ref, kth):
        c0 = count16(ref, lambda blk: blk >= jnp.zeros((), I16))
        start = jnp.where(c0 >= kth, 0, -32768).astype(I32)

        def bit_body(it, prefix):
            cand = prefix | lax.shift_left(jnp.int32(1), 14 - it)
            c = count16(ref, lambda blk: blk >= cand.astype(I16))
            return jnp.where(c >= kth, cand, prefix)

        return lax.fori_loop(0, 15, bit_body, start)

    tau_hi = bisect16(hi_ref, topk)
    tau_hi16 = tau_hi.astype(I16)
    kth_lo = topk - count16(hi_ref, lambda blk: blk > tau_hi16)

    def lo_body(kc, carry):
        r0 = pl.multiple_of(kc * cb_rows, cb_rows)
        lo = ((key_ref[pl.ds(r0, cb_rows), :] & 0xFFFF) - 32768).astype(I16)
        lo_ref[pl.ds(r0, cb_rows), :] = jnp.where(hi_ref[pl.ds(r0, cb_rows), :] == tau_hi16, lo,
                                                 jnp.full((), -32768, I16))
        return carry

    lax.fori_loop(0, nkc, lo_body, 0)
    tau_lo = bisect16(lo_ref, kth_lo)
    tau = lax.shift_left(tau_hi, 16) | (tau_lo + 32768)
    n_ge = count(lambda blk, r0: blk >= tau)

    @pl.when(jnp.max(jnp.where((n_ge > topk) & (tau != INT_MIN), 1, 0)) > 0)
    def _():
        need = (topk - count(lambda blk, r0: blk > tau)).astype(F32)
        ri = lax.broadcasted_iota(I32, (kb_rows, kb_rows), 0)
        ci = lax.broadcasted_iota(I32, (kb_rows, kb_rows), 1)
        upto = jnp.where(ci <= ri, 1.0, 0.0).astype(BF16)

        def strike(kc, seen):
            for half in range(2):
                rows = pl.ds(pl.multiple_of(kc * cb_rows + half * kb_rows, kb_rows), kb_rows)
                key = key_ref[rows, :]
                tie = key == tau
                rank = _dot(upto, jnp.where(tie, 1.0, 0.0).astype(BF16)) + seen
                key_ref[rows, :] = jnp.where(tie & (rank > need), INT_MIN, key)
                seen = rank[kb_rows - 1:kb_rows, :]
            return seen

        lax.fori_loop(0, nkc, strike, jnp.zeros((1, tq), F32))

    lane_rep = kb_rows // LANES
    tau_c = jnp.tile(jnp.broadcast_to(tau, (LANES, tq)).T, (1, lane_rep))
    acc_ref[...] = jnp.zeros(acc_ref.shape, F32)
    m_ref[...] = jnp.full(m_ref.shape, NEG_BIG, F32)
    qs = (q_ref[...] * (HEAD_DIM ** -0.5 * LOG2E)).astype(BF16)
    q4 = [jnp.concatenate([qs[:, (j * A_GROUP + g) * HEAD_DIM:(j * A_GROUP + g + 1) * HEAD_DIM]
                           for g in range(A_GROUP)], axis=0) for j in range(A_KV_HEADS)]

    def qk_scores(kb, s_slot, mt_slot):
        r0 = pl.multiple_of(jnp.minimum(kb, nkb - 1) * kb_rows, kb_rows)
        key_t = key_ref[pl.ds(r0, kb_rows), :].T
        bias_ref[...] = jnp.where((key_t >= tau_c) & (key_t != INT_MIN), 0.0, NEG_BIG)
        for j in range(A_KV_HEADS):
            s = lax.dot_general(q4[j], k_ref[j, pl.ds(r0, kb_rows), :], _NT, preferred_element_type=F32)
            for g in range(A_GROUP):
                rows = slice(g * tq, (g + 1) * tq)
                sm = s[rows, :] + bias_ref[...]
                s_slot[j, rows, :] = sm
                mt_slot[j, g] = jnp.broadcast_to(jnp.max(sm, axis=1, keepdims=True), (tq, LANES))

    def softmax_pv(kb, s_slot, mt_slot):
        r0 = pl.multiple_of(kb * kb_rows, kb_rows)
        for j in range(A_KV_HEADS):
            for g in range(A_GROUP):
                rows = slice(g * tq, (g + 1) * tq)
                m_old = m_ref[j, g]
                m_new = jnp.maximum(m_old, mt_slot[j, g])
                alpha_ref[j, g] = jnp.exp2(m_old - m_new)
                m_ref[j, g] = m_new
                p_ref[j, rows, :] = jnp.exp2(s_slot[j, rows, :] - jnp.tile(m_new, (1, lane_rep))).astype(BF16)
            pv = _dot(p_ref[j], v_ref[j, pl.ds(r0, kb_rows), :])
            acc_ref[j] = alpha_ref[j] * acc_ref[j] + pv.reshape(A_GROUP, tq, LANES)

    qk_scores(0, s0_ref, mt0_ref)

    def pair_body(i, carry):
        kb = 2 * i
        qk_scores(kb + 1, s1_ref, mt1_ref)
        softmax_pv(kb, s0_ref, mt0_ref)
        qk_scores(kb + 2, s0_ref, mt0_ref)
        softmax_pv(kb + 1, s1_ref, mt1_ref)
        return carry

    lax.fori_loop(0, nkc, pair_body, 0)

    outs = []
    for j in range(A_KV_HEADS):
        acc = acc_ref[j]
        for g in range(A_GROUP):
            outs.append(acc[g, :, :HEAD_DIM] * (1.0 / acc[g, :, HEAD_DIM:HEAD_DIM + 1]))
    o_ref[...] = jnp.concatenate(outs, axis=1)


def _dsa(q, iq, iw_t, k_hm, v_aug, ik, *, bsz, l_real, pos0, kb_rows, topk, tq):
    t_len = q.shape[0] // bsz
    l_pad = ik.shape[0] // bsz
    nq = t_len // tq
    assert l_pad // 16 <= 256, "packed hit counts are summed in bf16, exact only up to 256 per slot"
    kern = functools.partial(_dsa_kernel, l_real=l_real, pos0=pos0, kb_rows=kb_rows, topk=topk, tq=tq)
    qrow = lambda b, i: (b * nq + i, 0)
    return pl.pallas_call(
        kern,
        grid=(bsz, nq),
        in_specs=[pl.BlockSpec((tq, A_WIDTH), qrow), pl.BlockSpec((tq, IDX_WIDTH), qrow),
                  pl.BlockSpec((8, tq), lambda b, i: (0, b * nq + i)),
                  pl.BlockSpec((A_KV_HEADS, l_pad, HEAD_DIM), lambda b, i: (0, b, 0)),
                  pl.BlockSpec((A_KV_HEADS, l_pad, LANES), lambda b, i: (0, b, 0)),
                  pl.BlockSpec((l_pad, HEAD_DIM), lambda b, i: (b, 0))],
        out_specs=pl.BlockSpec((tq, A_WIDTH), qrow),
        out_shape=jax.ShapeDtypeStruct((bsz * t_len, A_WIDTH), F32),
        scratch_shapes=[pltpu.VMEM((l_pad, tq), I32), pltpu.VMEM((l_pad, tq), I16), pltpu.VMEM((l_pad, tq), I16),
                        pltpu.VMEM((A_KV_HEADS, A_GROUP, tq, LANES), F32),
                        pltpu.VMEM((A_KV_HEADS, A_GROUP, tq, LANES), F32),
                        pltpu.VMEM((A_KV_HEADS, A_GROUP * tq, kb_rows), F32),
                        pltpu.VMEM((A_KV_HEADS, A_GROUP * tq, kb_rows), F32),
                        pltpu.VMEM((A_KV_HEADS, A_GROUP, tq, LANES), F32),
                        pltpu.VMEM((A_KV_HEADS, A_GROUP * tq, kb_rows), BF16),
                        pltpu.VMEM((tq, kb_rows), F32),
                        pltpu.VMEM((A_KV_HEADS, A_GROUP, tq, LANES), F32),
                        pltpu.VMEM((A_KV_HEADS, A_GROUP, tq, LANES), F32)],
        compiler_params=_cparams("parallel", "arbitrary"),
        name="dsa",
    )(q, iq, iw_t, k_hm, v_aug, ik)


def _hgrn_kernel(bq_ref, bf_ref, bi_ref, bg_ref, lb_ref, gn_ref, s0_ref, o_ref, sfin_ref,
                 st_ref, q_s, kk_s, b_s, qe_s, kd_s, dec_s, o_s, e_s, *, layer, tb, nseq):
    tstep = pl.program_id(1)
    nchunk = tb // B_CHUNK

    @pl.when(tstep == 0)
    def _():
        st_ref[...] = s0_ref[...]

    lbr = lb_ref[...]
    lbe = jnp.exp(lbr - jnp.max(lbr, axis=0, keepdims=True))
    lbs = lbe / jnp.sum(lbe, axis=0, keepdims=True)
    cum = lbs[0:1, :]
    first = cum
    for r in range(1, layer + 1):
        cum = cum + lbs[r:r + 1, :]
    lb = cum - first

    ri = lax.broadcasted_iota(I32, (tb, tb), 0)
    ci = lax.broadcasted_iota(I32, (tb, tb), 1)
    same = (ri // B_CHUNK) == (ci // B_CHUNK)
    tri = jnp.where(same & (ci <= ri), 1.0, 0.0).astype(BF16)
    ones = jnp.where(same, 1.0, 0.0).astype(BF16)
    for u in range(nseq):
        f = lb + (1.0 - lb) * _sigmoid(bf_ref[u])
        bq = bq_ref[u]
        q = bq * _sigmoid(bq) * (HEAD_DIM ** -0.5)
        g3 = _split3(jnp.log(f) * LOG2E)
        b = _dot(tri, g3[0]) + _dot(tri, g3[1]) + _dot(tri, g3[2])
        blast = _dot(ones, g3[0]) + _dot(ones, g3[1]) + _dot(ones, g3[2])
        q_s[u] = q
        kk_s[u] = 1.0 - f
        b_s[u] = b
        qe_s[u] = q * jnp.exp2(b)
        kd_s[u] = (1.0 - f) * jnp.exp2(blast - b)
        dec_s[u] = jnp.exp2(blast)

    hi = lax.broadcasted_iota(I32, (HALF, HALF), 0) // HEAD_DIM
    hj = lax.broadcasted_iota(I32, (HALF, HALF), 1) // HEAD_DIM
    bd_mask = hi == hj
    bd = jnp.where(bd_mask, 1.0, 0.0).astype(BF16)
    t_iota = lax.broadcasted_iota(I32, (B_CHUNK, B_WIDTH), 0)

    def seg_sum(x, two_pass):
        out = []
        for half in range(2):
            xs = x[:, half * HALF:(half + 1) * HALF]
            if two_pass:
                xh, xl = _split2(xs)
                out.append(_dot(xh, bd) + _dot(xl, bd))
            else:
                out.append(_dot(xs.astype(BF16), bd))
        return jnp.concatenate(out, axis=1)

    def chunk_body(c, carry):
        r0 = pl.multiple_of(c * B_CHUNK, B_CHUNK)
        rows = pl.ds(r0, B_CHUNK)
        for u in range(nseq):
            qc, kc, bc, vc = q_s[u, rows, :], kk_s[u, rows, :], b_s[u, rows, :], bi_ref[u, rows, :]
            qe, kd = qe_s[u, rows, :].astype(BF16), kd_s[u, rows, :].astype(BF16)
            dec = dec_s[u, pl.ds(r0, 1), :]
            for s in range(B_CHUNK):
                ex = jnp.exp2(jnp.where(t_iota >= s, bc - bc[s:s + 1, :], NEG_BIG))
                e_s[u, s * B_CHUNK:(s + 1) * B_CHUNK, :] = qc * kc[s:s + 1, :] * ex
            aexp = seg_sum(e_s[u], False)
            o = jnp.zeros((B_CHUNK, B_WIDTH), F32)
            for s in range(B_CHUNK):
                o = o + aexp[s * B_CHUNK:(s + 1) * B_CHUNK, :] * vc[s:s + 1, :]
            inter = []
            vb = vc.astype(BF16)
            for half in range(2):
                cols = slice(half * HALF, (half + 1) * HALF)
                st = st_ref[u, half]
                inter.append(lax.dot_general(qe[:, cols], st.astype(BF16), _NT, preferred_element_type=F32))
                upd = lax.dot_general(vb[:, cols], kd[:, cols], _TN, preferred_element_type=F32)
                st_ref[u, half] = st * dec[:, cols] + jnp.where(bd_mask, upd, 0.0)
            o_s[u, rows, :] = o + jnp.concatenate(inter, axis=1)
        return carry

    lax.fori_loop(0, nchunk, chunk_body, 0)

    for u in range(nseq):
        o = o_s[u]
        ms = seg_sum(o * o, True) * (1.0 / HEAD_DIM)
        bg = bg_ref[u]
        o_ref[u] = o * lax.rsqrt(ms + EPS) * gn_ref[...] * (bg * _sigmoid(bg))

    @pl.when(tstep == pl.num_programs(1) - 1)
    def _():
        sfin_ref[...] = st_ref[...]


def _hgrn(hb, lb_raw, gn_tiled, s0_bd, *, layer, bsz, t_len, tb):
    nt = t_len // tb
    nseq = 2 if bsz % 2 == 0 else 1
    hb3 = hb.reshape(bsz, t_len, 4 * B_WIDTH)
    col = lambda j: (lambda b, t: (b, t, j))
    scr = lambda r: pltpu.VMEM((nseq, r, B_WIDTH), F32)
    kern = functools.partial(_hgrn_kernel, layer=layer, tb=tb, nseq=nseq)
    blk = (nseq, tb, B_WIDTH)
    out, s_fin = pl.pallas_call(
        kern,
        grid=(bsz // nseq, nt),
        in_specs=[pl.BlockSpec(blk, col(0)), pl.BlockSpec(blk, col(1)), pl.BlockSpec(blk, col(2)),
                  pl.BlockSpec(blk, col(3)),
                  pl.BlockSpec(lb_raw.shape, lambda b, t: (0, 0)),
                  pl.BlockSpec((1, B_WIDTH), lambda b, t: (0, 0)),
                  pl.BlockSpec((nseq, 2, HALF, HALF), lambda b, t: (b, 0, 0, 0))],
        out_specs=[pl.BlockSpec(blk, lambda b, t: (b, t, 0)),
                   pl.BlockSpec((nseq, 2, HALF, HALF), lambda b, t: (b, 0, 0, 0))],
        out_shape=[jax.ShapeDtypeStruct((bsz, t_len, B_WIDTH), F32),
                   jax.ShapeDtypeStruct((bsz, 2, HALF, HALF), F32)],
        scratch_shapes=[pltpu.VMEM((nseq, 2, HALF, HALF), F32), scr(tb), scr(tb), scr(tb), scr(tb), scr(tb),
                        scr(tb), scr(tb), scr(B_CHUNK * B_CHUNK)],
        compiler_params=_cparams("parallel", "arbitrary"),
        name="hgrn2",
    )(hb3, hb3, hb3, hb3, lb_raw, gn_tiled, s0_bd)
    return out.reshape(bsz * t_len, B_WIDTH), s_fin


def _state_to_tiles(s):
    bsz = s.shape[0]
    st = jnp.swapaxes(s, -1, -2).reshape(bsz, 2, B_HEADS // 2, HEAD_DIM, HEAD_DIM)
    eye = jnp.eye(B_HEADS // 2, dtype=s.dtype)
    t = st[:, :, :, :, None, :] * eye[None, None, :, None, :, None]
    return t.reshape(bsz, 2, HALF, HALF)


def _tiles_to_state(t):
    bsz = t.shape[0]
    t6 = t.reshape(bsz, 2, B_HEADS // 2, HEAD_DIM, B_HEADS // 2, HEAD_DIM)
    diag = jnp.stack([t6[:, :, h, :, h, :] for h in range(B_HEADS // 2)], axis=2)
    return jnp.swapaxes(diag.reshape(bsz, B_HEADS, HEAD_DIM, HEAD_DIM), -1, -2)


def _normmm_kernel(x_ref, g_ref, w_ref, o_ref):
    o_ref[...] = _dot(_rms(x_ref[...], g_ref[...]).astype(BF16), w_ref[...])


def _norm_matmul(x, g, w, tm):
    n, m = x.shape[0], w.shape[1]
    return pl.pallas_call(
        _normmm_kernel,
        grid=(n // tm,),
        in_specs=[pl.BlockSpec((tm, D_MODEL), lambda i: (i, 0)), pl.BlockSpec((1, D_MODEL), lambda i: (0, 0)),
                  pl.BlockSpec((D_MODEL, m), lambda i: (0, 0))],
        out_specs=pl.BlockSpec((tm, m), lambda i: (i, 0)),
        out_shape=jax.ShapeDtypeStruct((n, m), F32),
        compiler_params=_cparams("parallel"),
        name="norm_matmul",
    )(x, g, w)


def _swiglu_mid(xn, w1, w3):
    a = _dot(xn, w1)
    return (a * _sigmoid(a) * _dot(xn, w3)).astype(BF16)


def _postmix_kernel(oa_ref, ob_ref, x_ref, wa_ref, wb_ref, gm_ref, wq_ref, mk_ref, mv_ref, wo_ref, *rest,
                    with_ffn, final, chunks):
    o_ref = rest[-1]
    x = (x_ref[...] + _dot(oa_ref[...].astype(BF16), wa_ref[...]) + _dot(ob_ref[...].astype(BF16), wb_ref[...]))
    q = _dot(_rms(x, gm_ref[...]).astype(BF16), wq_ref[...])
    heads = []
    for h in range(MEM_HEADS):
        cols = slice(h * MEM_HEAD_DIM, (h + 1) * MEM_HEAD_DIM)
        s = lax.dot_general(q[:, cols].astype(BF16), mk_ref[0, :, cols], _NT,
                            preferred_element_type=F32) * (MEM_HEAD_DIM ** -0.5)
        p = jnp.exp(s - jnp.max(s, axis=-1, keepdims=True))
        p = p * (1.0 / jnp.sum(p, axis=-1, keepdims=True))
        heads.append(_dot(p.astype(BF16), mv_ref[0, :, cols]).astype(BF16))
    y = x + _dot(jnp.concatenate(heads, axis=1), wo_ref[...])
    if with_ffn:
        gffn_ref, w1_ref, w3_ref, w2_ref, gf_ref = rest[:5]
        yn = _rms(y, gffn_ref[...]).astype(BF16)
        lo = 0
        for width in chunks:
            cols = slice(lo, lo + width)
            y = y + _dot(_swiglu_mid(yn, w1_ref[:, cols], w3_ref[:, cols]), w2_ref[cols, :])
            lo += width
        if final:
            y = _rms(y, gf_ref[...])
    o_ref[...] = y


def _postmix(oa, ob, x, wa, wb, gm, wq, mk, mv, wo, ffn=None, *, rows_per_batch, tm, final=False):
    n = x.shape[0]
    n_mem = mk.shape[1]
    per = rows_per_batch // tm
    row = lambda i: (i, 0)
    whole = lambda i: (0, 0)
    mem = lambda i: (i // per, 0, 0)
    const = lambda shape: pl.BlockSpec(shape, whole, pipeline_mode=pl.Buffered(1))
    in_specs = [pl.BlockSpec((tm, A_WIDTH), row), pl.BlockSpec((tm, B_WIDTH), row), pl.BlockSpec((tm, D_MODEL), row),
                const((A_WIDTH, D_MODEL)), const((B_WIDTH, D_MODEL)), const((1, D_MODEL)),
                const((D_MODEL, D_MODEL)),
                pl.BlockSpec((1, n_mem, D_MODEL), mem), pl.BlockSpec((1, n_mem, D_MODEL), mem),
                const((D_MODEL, D_MODEL))]
    args = [oa, ob, x, wa, wb, gm, wq, mk, mv, wo]
    chunks = ()
    if ffn is not None:
        d_ff = ffn[1].shape[1]
        step = 4 * HALF
        chunks = tuple(min(step, d_ff - lo) for lo in range(0, d_ff, step))
        in_specs += [const((1, D_MODEL)), const((D_MODEL, d_ff)), const((D_MODEL, d_ff)), const((d_ff, D_MODEL)),
                     const((1, D_MODEL))]
        args += list(ffn)
    return pl.pallas_call(
        functools.partial(_postmix_kernel, with_ffn=ffn is not None, final=final, chunks=chunks),
        grid=(n // tm,),
        in_specs=in_specs,
        out_specs=pl.BlockSpec((tm, D_MODEL), row),
        out_shape=jax.ShapeDtypeStruct((n, D_MODEL), F32),
        compiler_params=_cparams("parallel"),
        name="post_mix",
    )(*args)


def _router_kernel(x_ref, g_ref, rt_ref, comb_ref, sel_ref):
    xn = _rms(x_ref[...], g_ref[...])
    logits = lax.dot_general(rt_ref[...], xn, _NT, preferred_element_type=F32,
                             precision=lax.Precision.HIGHEST)
    e_idx = lax.broadcasted_iota(I32, logits.shape, 0)
    m1 = jnp.max(logits, axis=0, keepdims=True)
    i1 = jnp.min(jnp.where(logits == m1, e_idx, N_EXPERTS), axis=0, keepdims=True)
    rest = jnp.where(e_idx == i1, -jnp.inf, logits)
    m2 = jnp.max(rest, axis=0, keepdims=True)
    i2 = jnp.min(jnp.where(rest == m2, e_idx, N_EXPERTS), axis=0, keepdims=True)
    e2 = jnp.exp(m2 - m1)
    g1 = 1.0 / (1.0 + e2)
    comb_ref[...] = jnp.where(e_idx == i1, g1, 0.0) + jnp.where(e_idx == i2, e2 * g1, 0.0)
    sel_ref[...] = jnp.where((e_idx == i1) | (e_idx == i2), 1.0, 0.0)


def _router(x, g, rt, tm):
    n = x.shape[0]
    return pl.pallas_call(
        _router_kernel,
        grid=(n // tm,),
        in_specs=[pl.BlockSpec((tm, D_MODEL), lambda i: (i, 0)), pl.BlockSpec((1, D_MODEL), lambda i: (0, 0)),
                  pl.BlockSpec((N_EXPERTS, D_MODEL), lambda i: (0, 0))],
        out_specs=[pl.BlockSpec((N_EXPERTS, tm), lambda i: (0, i)), pl.BlockSpec((N_EXPERTS, tm), lambda i: (0, i))],
        out_shape=[jax.ShapeDtypeStruct((N_EXPERTS, n), F32), jax.ShapeDtypeStruct((N_EXPERTS, n), F32)],
        compiler_params=_cparams("parallel"),
        name="moe_router",
    )(x, g, rt)


def _moe_kernel(x_ref, g_ref, selt_ref, sel_ref, comb_ref, w1_ref, w3_ref, w2_ref, gf_ref, o_ref,
                xn_ref, posr_ref, posc_ref, cnt_ref, xg_ref, yg_ref, *, final, nb, sub):
    e = pl.program_id(1)
    j = pl.program_id(2)
    last_j = pl.num_programs(2) - 1

    @pl.when((e == 0) & (j == 0))
    def _():
        x = x_ref[...]
        xn_ref[...] = _rms(x, g_ref[...]).astype(BF16)
        o_ref[...] = x
        cc = min(nb, 256)
        ri = lax.broadcasted_iota(I32, (cc, cc), 0)
        ci = lax.broadcasted_iota(I32, (cc, cc), 1)
        before_r = jnp.where(ri < ci, 1.0, 0.0).astype(BF16)
        before_c = jnp.where(ci < ri, 1.0, 0.0).astype(BF16)
        run_r = jnp.zeros((N_EXPERTS, 1), F32)
        run_c = jnp.zeros((1, N_EXPERTS), F32)
        for c in range(nb // cc):
            cols = slice(c * cc, (c + 1) * cc)
            mr = selt_ref[:, cols]
            posr_ref[:, cols] = jnp.where(mr > 0.0, _dot(mr.astype(BF16), before_r) + run_r, -1.0)
            run_r = run_r + jnp.sum(mr, axis=1, keepdims=True)
            mc = sel_ref[cols, :]
            posc_ref[cols, :] = jnp.where(mc > 0.0, _dot(before_c, mc.astype(BF16)) + run_c, -1.0)
            run_c = run_c + jnp.sum(mc, axis=0, keepdims=True)
        for k in range(N_EXPERTS):
            cnt_ref[k] = run_c[0, k].astype(I32)

    cnt = cnt_ref[e]
    nfull = cnt // sub
    rem = cnt - nfull * sub
    tail_sizes = [m for m in (64, 128) if m < sub] + [sub]

    def for_tiles(fn):
        def body(s, carry):
            fn(pl.multiple_of(s * sub, sub), sub)
            return carry
        lax.fori_loop(0, nfull, body, 0)
        r0 = pl.multiple_of(nfull * sub, sub)
        lo = 0
        for m in tail_sizes:
            @pl.when((rem > lo) & (rem <= m))
            def _():
                fn(r0, m)
            lo = m

    def gather_matrix(r0, m):
        slot_r = posr_ref[pl.ds(e, 1), :]
        want = (r0 + lax.broadcasted_iota(I32, (m, nb), 0)).astype(F32)
        return jnp.where(slot_r == want, 1.0, 0.0).astype(BF16)

    def gather(r0, m):
        xg_ref[pl.ds(r0, m), :] = _dot(gather_matrix(r0, m), xn_ref[...]).astype(BF16)

    def ffn(r0, m):
        rows = pl.ds(r0, m)
        part = _dot(_swiglu_mid(xg_ref[rows, :], w1_ref[0], w3_ref[0]), w2_ref[0])

        @pl.when(j == 0)
        def _():
            yg_ref[rows, :] = part

        @pl.when(j > 0)
        def _():
            yg_ref[rows, :] += part

    @pl.when(j == 0)
    def _():
        for_tiles(gather)

    for_tiles(ffn)

    @pl.when(j == last_j)
    def _():
        lane8 = lax.broadcasted_iota(I32, (nb, N_EXPERTS), 1)
        slot_c = jnp.sum(jnp.where(lane8 == e, posc_ref[...], 0.0), axis=1, keepdims=True)
        gate_c = jnp.sum(jnp.where(lane8 == e, comb_ref[...], 0.0), axis=1, keepdims=True)
        gate3 = jnp.concatenate(_split3(gate_c), axis=1)

        def scatter(r0, m):
            gate = jnp.sum(_dot(gather_matrix(r0, m), gate3), axis=1, keepdims=True)
            y = (yg_ref[pl.ds(r0, m), :] * gate).astype(BF16)
            want = (r0 + lax.broadcasted_iota(I32, (nb, m), 1)).astype(F32)
            o_ref[...] += _dot(jnp.where(slot_c == want, 1.0, 0.0).astype(BF16), y)

        for_tiles(scatter)

        if final:
            @pl.when(e == pl.num_programs(1) - 1)
            def _():
                o_ref[...] = _rms(o_ref[...], gf_ref[...])


def _moe(x, g, sel_t, sel, comb, w1, w3, w2, gf, *, final, nb, tf):
    n = x.shape[0]
    d_ff = w1.shape[2]
    sub = min(nb, MOE_TILE_ROWS)
    row = lambda i, e, j: (i, 0)
    whole = lambda i, e, j: (0, 0)
    return pl.pallas_call(
        functools.partial(_moe_kernel, final=final, nb=nb, sub=sub),
        grid=(n // nb, N_EXPERTS, d_ff // tf),
        in_specs=[pl.BlockSpec((nb, D_MODEL), row), pl.BlockSpec((1, D_MODEL), whole),
                  pl.BlockSpec((N_EXPERTS, nb), lambda i, e, j: (0, i)),
                  pl.BlockSpec((nb, N_EXPERTS), row), pl.BlockSpec((nb, N_EXPERTS), row),
                  pl.BlockSpec((1, D_MODEL, tf), lambda i, e, j: (e, 0, j)),
                  pl.BlockSpec((1, D_MODEL, tf), lambda i, e, j: (e, 0, j)),
                  pl.BlockSpec((1, tf, D_MODEL), lambda i, e, j: (e, j, 0)), pl.BlockSpec((1, D_MODEL), whole)],
        out_specs=pl.BlockSpec((nb, D_MODEL), row),
        out_shape=jax.ShapeDtypeStruct((n, D_MODEL), F32),
        scratch_shapes=[pltpu.VMEM((nb, D_MODEL), BF16), pltpu.VMEM((N_EXPERTS, nb), F32),
                        pltpu.VMEM((nb, N_EXPERTS), F32), pltpu.SMEM((N_EXPERTS,), I32),
                        pltpu.VMEM((nb + sub, D_MODEL), BF16), pltpu.VMEM((nb + sub, D_MODEL), F32)],
        compiler_params=_cparams("parallel", "arbitrary", "arbitrary"),
        name="moe",
    )(x, g, sel_t, sel, comb, w1, w3, w2, gf)


def _row_tile(n, cap):
    t = min(n, cap)
    assert n % t == 0, (n, cap)
    return t


def _ff_tile(d_ff):
    for parts in (2, 4, 11, 22):
        if d_ff % parts == 0 and (d_ff // parts) % LANES == 0:
            return d_ff // parts
    return d_ff


def _mixer(x, tabs, w, *, bsz, t_len, pos0, past, s0, layer):
    n = bsz * t_len
    qa, hb, iq, k_t, v_t, ik_t, k_hm, v_aug, ik_b, iw_t = _in_proj(x, w["norm_mix"], w["w_in"], layer, tabs,
                                                                   t_len, _row_tile(n, 512))
    l_real = t_len if past is None else past[0].shape[1] + t_len
    topk = min(TOPK_MAX, l_real // 4)
    tq = 2 * LANES if t_len % (2 * LANES) == 0 else LANES
    kb_rows = 256 if l_real % 256 == 0 else LANES
    l_pad = -(-l_real // (2 * kb_rows)) * (2 * kb_rows)
    t_pad = -(-t_len // tq) * tq
    if past is not None:
        pk, pv, pik = past
        one = jnp.ones(pv.shape[:2] + (A_KV_HEADS, 1), BF16)
        zero = jnp.zeros(pv.shape[:2] + (A_KV_HEADS, LANES - HEAD_DIM - 1), BF16)
        pv_aug = jnp.concatenate([pv.astype(BF16), one, zero], axis=-1)
        hm = lambda a: jnp.transpose(a, (2, 0, 1, 3))
        k_hm = jnp.concatenate([hm(pk.astype(BF16)), k_hm.reshape(A_KV_HEADS, bsz, t_len, HEAD_DIM)], axis=2)
        v_aug = jnp.concatenate([hm(pv_aug), v_aug.reshape(A_KV_HEADS, bsz, t_len, LANES)], axis=2)
        ik_b = jnp.concatenate([pik.astype(BF16), ik_b.reshape(bsz, t_len, HEAD_DIM)], axis=1)
    if l_pad != l_real or past is not None:
        padl = l_pad - l_real
        k_hm = jnp.pad(k_hm.reshape(A_KV_HEADS, bsz, l_real, HEAD_DIM), ((0, 0), (0, 0), (0, padl), (0, 0)))
        v_aug = jnp.pad(v_aug.reshape(A_KV_HEADS, bsz, l_real, LANES), ((0, 0), (0, 0), (0, padl), (0, 0)))
        ik_b = jnp.pad(ik_b.reshape(bsz, l_real, HEAD_DIM), ((0, 0), (0, padl), (0, 0)))
        k_hm = k_hm.reshape(A_KV_HEADS, bsz * l_pad, HEAD_DIM)
        v_aug = v_aug.reshape(A_KV_HEADS, bsz * l_pad, LANES)
        ik_b = ik_b.reshape(bsz * l_pad, HEAD_DIM)
    q_in, iq_in, iw_in = qa, iq, iw_t
    if t_pad != t_len:
        padq = lambda a: jnp.pad(a.reshape(bsz, t_len, -1),
                                 ((0, 0), (0, t_pad - t_len), (0, 0))).reshape(bsz * t_pad, -1)
        q_in, iq_in = padq(qa), padq(iq)
        iw_in = jnp.pad(iw_t.reshape(8, bsz, t_len), ((0, 0), (0, 0), (0, t_pad - t_len))).reshape(8, bsz * t_pad)
    oa = _dsa(q_in, iq_in, iw_in, k_hm, v_aug, ik_b, bsz=bsz, l_real=l_real, pos0=pos0, kb_rows=kb_rows,
              topk=topk, tq=tq)
    if t_pad != t_len:
        oa = oa.reshape(bsz, t_pad, A_WIDTH)[:, :t_len].reshape(n, A_WIDTH)
    ob, s_tiles = _hgrn(hb, w["hgrn_lb"], w["gnorm"], _state_to_tiles(s0), layer=layer, bsz=bsz, t_len=t_len,
                        tb=_row_tile(t_len, 256))
    return oa, ob, k_t, v_t, ik_t, _tiles_to_state(s_tiles)


def _from_time_minor(a, bsz, t_len, heads):
    if a.ndim == 2:
        a = a.reshape((-1, HEAD_DIM, bsz, t_len))
        a = jnp.transpose(a, (2, 3, 0, 1))
    else:
        a = jnp.moveaxis(a, -1, 1)
        a = a.reshape(bsz, t_len, -1, HEAD_DIM)
    return a if heads is not None else a.reshape(bsz, t_len, HEAD_DIM)


def kernel(x_prompt, x_sample, cache_k, cache_v, cache_idx_k, state_hgrn, cache_mem_k, cache_mem_v, mem_prompt,
           norm_mix, w_in, hgrn_lb, hgrn_gnorm, w_out, norm_mem, norm_memkv, w_mq, w_mk, w_mv, w_mo,
           norm_ffn, ffn_w1, ffn_w3, ffn_w2, moe_router, moe_w1, moe_w3, moe_w2, norm_final):
    depth = w_in.shape[0]
    bp, tp, _ = x_prompt.shape
    bs, ts, _ = x_sample.shape
    past = cache_k.shape[2]
    n_mem = mem_prompt.shape[1]
    tabs_p = _rope_tables(jnp.arange(tp, dtype=I32))
    tabs_s = tuple(jnp.tile(a, (bs, 1)) for a in _rope_tables(past + jnp.arange(ts, dtype=I32)))
    xp = x_prompt.reshape(bp * tp, D_MODEL)
    xs = x_sample.reshape(bs * ts, D_MODEL)
    mem = mem_prompt.reshape(bp * n_mem, D_MODEL)
    s0_p = jnp.zeros((bp, B_HEADS, HEAD_DIM, HEAD_DIM), F32)
    gfin = norm_final.reshape(1, D_MODEL)
    row2 = lambda a: a.reshape(1, -1)
    outs = {k: [] for k in ("kp", "vp", "ikp", "sp", "mkp", "mvp", "ks", "vs", "iks", "ss")}
    for l in range(depth):
        w = {"norm_mix": row2(norm_mix[l]), "w_in": w_in, "hgrn_lb": hgrn_lb.astype(F32),
             "gnorm": row2(jnp.tile(hgrn_gnorm[l], B_HEADS)),
             "w_out_a": w_out[l, :A_WIDTH].astype(BF16), "w_out_b": w_out[l, A_WIDTH:].astype(BF16)}
        oa_p, ob_p, kp, vp, ikp, sp = _mixer(xp, tabs_p, w, bsz=bp, t_len=tp, pos0=0, past=None, s0=s0_p, layer=l)
        oa_s, ob_s, ks, vs, iks, ss = _mixer(xs, tabs_s, w, bsz=bs, t_len=ts, pos0=past,
                                             past=(cache_k[l], cache_v[l], cache_idx_k[l]), s0=state_hgrn[l],
                                             layer=l)
        wkv = jnp.concatenate([w_mk[l], w_mv[l]], axis=1).astype(BF16)
        mkv = _norm_matmul(mem, row2(norm_memkv[l]), wkv, _row_tile(bp * n_mem, 256))
        mk_p = mkv[:, :D_MODEL].reshape(bp, n_mem, D_MODEL)
        mv_p = mkv[:, D_MODEL:].reshape(bp, n_mem, D_MODEL)
        mk_s = cache_mem_k[l].reshape(bs, n_mem, D_MODEL).astype(BF16)
        mv_s = cache_mem_v[l].reshape(bs, n_mem, D_MODEL).astype(BF16)
        wq, wo, gm = w_mq[l].astype(BF16), w_mo[l].astype(BF16), row2(norm_mem[l])
        final = l == depth - 1
        gf = row2(norm_ffn[l])
        tail = (w["w_out_a"], w["w_out_b"], gm, wq)
        if l % 2 == 0:
            dense = (gf, ffn_w1[l // 2].astype(BF16), ffn_w3[l // 2].astype(BF16), ffn_w2[l // 2].astype(BF16), gfin)
            xp = _postmix(oa_p, ob_p, xp, *tail, mk_p.astype(BF16), mv_p.astype(BF16), wo, dense,
                          rows_per_batch=tp, tm=_row_tile(tp, 512), final=final)
            xs = _postmix(oa_s, ob_s, xs, *tail, mk_s, mv_s, wo, dense,
                          rows_per_batch=ts, tm=_row_tile(ts, 512), final=final)
        else:
            xp = _postmix(oa_p, ob_p, xp, *tail, mk_p.astype(BF16), mv_p.astype(BF16), wo,
                          rows_per_batch=tp, tm=_row_tile(tp, 512))
            xs = _postmix(oa_s, ob_s, xs, *tail, mk_s, mv_s, wo, rows_per_batch=ts, tm=_row_tile(ts, 512))
            w1, w3, w2 = moe_w1[l // 2].astype(BF16), moe_w3[l // 2].astype(BF16), moe_w2[l // 2].astype(BF16)
            rt = moe_router[l // 2].T
            tf = _ff_tile(w1.shape[2])
            comb_p, sel_p = _router(xp, gf, rt, _row_tile(bp * tp, 512))
            comb_s, sel_s = _router(xs, gf, rt, _row_tile(bs * ts, 512))
            xp = _moe(xp, gf, sel_p, sel_p.T, comb_p.T, w1, w3, w2, gfin, final=final,
                      nb=_row_tile(bp * tp, 1024), tf=tf)
            xs = _moe(xs, gf, sel_s, sel_s.T, comb_s.T, w1, w3, w2, gfin, final=final,
                      nb=_row_tile(bs * ts, 1024), tf=tf)
        outs["kp"].append(_from_time_minor(kp, bp, tp, A_KV_HEADS))
        outs["vp"].append(_from_time_minor(vp, bp, tp, A_KV_HEADS))
        outs["ikp"].append(_from_time_minor(ikp, bp, tp, None)); outs["sp"].append(sp)
        outs["mkp"].append(mk_p.reshape(bp, n_mem, MEM_HEADS, MEM_HEAD_DIM))
        outs["mvp"].append(mv_p.reshape(bp, n_mem, MEM_HEADS, MEM_HEAD_DIM))
        outs["ks"].append(_from_time_minor(ks, bs, ts, A_KV_HEADS))
        outs["vs"].append(_from_time_minor(vs, bs, ts, A_KV_HEADS))
        outs["iks"].append(_from_time_minor(iks, bs, ts, None)); outs["ss"].append(ss)
    y_prompt = xp.reshape(bp, tp, D_MODEL)
    y_sample = xs.reshape(bs, ts, D_MODEL)
    st = lambda k: jnp.stack(outs[k])
    return (y_prompt, y_sample, st("kp"), st("vp"), st("ikp"), st("sp"), st("mkp"), st("mvp"),
            st("ks"), st("vs"), st("iks"), st("ss"))
```

```python
import functools

import numpy as np
import jax
import jax.numpy as jnp
from jax import lax
from jax.experimental import pallas as pl
from jax.experimental.pallas import tpu as pltpu

F32 = jnp.float32
BF16 = jnp.bfloat16
I32 = jnp.int32
I16 = jnp.int16

D_MODEL = 1024
CHUNK = 64
EPS = 1e-6
ROPE_THETA = 500000.0
HEAD_DIM = 64
ROT_DIM = HEAD_DIM // 4
A_HEADS = 8
A_KV_HEADS = 2
A_GROUP = A_HEADS // A_KV_HEADS
A_WIDTH = A_HEADS * HEAD_DIM
KV_WIDTH = A_KV_HEADS * HEAD_DIM
IDX_HEADS = 4
IDX_WIDTH = IDX_HEADS * HEAD_DIM
TOPK_MAX = 256
B_HEADS = 8
B_WIDTH = B_HEADS * HEAD_DIM
B_CHUNK = 16
MEM_HEADS = 4
MEM_HEAD_DIM = D_MODEL // MEM_HEADS
N_EXPERTS = 8
MOE_TILE_ROWS = 256
LANES = 128
HALF = 256
IN_SPLITS = (A_WIDTH, KV_WIDTH, KV_WIDTH, IDX_WIDTH, HEAD_DIM, IDX_HEADS, B_WIDTH, B_WIDTH, B_WIDTH, B_WIDTH)
IN_WIDTH = sum(IN_SPLITS)
IN_WIDTH_PAD = 3200
VMEM_LIMIT = 56 * 1024 * 1024
INT_MIN = -2147483648
NEG_BIG = -1e30
LOG2E = 1.4426950408889634

_NT = (((1,), (1,)), ((), ()))
_TN = (((0,), (0,)), ((), ()))


def _cparams(*sem):
    return pltpu.CompilerParams(dimension_semantics=sem, vmem_limit_bytes=VMEM_LIMIT)


def _rms(x, g):
    return x * lax.rsqrt(jnp.mean(x * x, axis=-1, keepdims=True) + EPS) * g


def _sigmoid(x):
    return 1.0 / (1.0 + jnp.exp(-x))


def _dot(a, b):
    return jnp.dot(a, b, preferred_element_type=F32)


def _split2(x):
    hi = x.astype(BF16)
    lo = (x - hi.astype(F32)).astype(BF16)
    return hi, lo


def _split3(x):
    hi = x.astype(BF16)
    r = x - hi.astype(F32)
    mid = r.astype(BF16)
    lo = (r - mid.astype(F32)).astype(BF16)
    return hi, mid, lo


def _rope128(xb, c, s1, s2):
    return xb * c + pltpu.roll(xb, LANES - ROT_DIM // 2, 1) * s1 + pltpu.roll(xb, ROT_DIM // 2, 1) * s2


def _inproj_kernel(x_ref, g_ref, wraw_ref, c_ref, s1_ref, s2_ref,
                   qa_ref, hb_ref, iq_ref, kt_ref, vt_ref, ikt_ref, khm_ref, vaug_ref, ikb_ref, iwt_ref, w_ref):
    @pl.when(pl.program_id(0) == 0)
    def _():
        src = np.cumsum([0] + list(IN_SPLITS))
        dst = 0
        for i in (0, 6, 7, 8, 9, 1, 2, 3):
            width = int(src[i + 1] - src[i])
            w_ref[:, dst:dst + width] = wraw_ref[0, :, int(src[i]):int(src[i + 1])].astype(BF16)
            dst += width
        tail = jnp.concatenate([wraw_ref[0, :, int(src[4]):int(src[6])],
                                jnp.zeros((D_MODEL, LANES - HEAD_DIM - IDX_HEADS), F32)], axis=1)
        w_ref[:, dst:dst + LANES] = tail.astype(BF16)

    xn = _rms(x_ref[...], g_ref[...]).astype(BF16)
    c, s1, s2 = c_ref[...], s1_ref[...], s2_ref[...]
    for j in range(A_WIDTH // HALF):
        h = _dot(xn, w_ref[:, j * HALF:(j + 1) * HALF])
        for i in range(HALF // LANES):
            lo = j * HALF + i * LANES
            qa_ref[:, lo:lo + LANES] = _rope128(h[:, i * LANES:(i + 1) * LANES], c, s1, s2)
    for j in range(4):
        lo = A_WIDTH + j * B_WIDTH
        hb_ref[:, j * B_WIDTH:(j + 1) * B_WIDTH] = _dot(xn, w_ref[:, lo:lo + B_WIDTH])
    base = A_WIDTH + 4 * B_WIDTH
    kv = _dot(xn, w_ref[:, base:base + HALF])
    k = _rope128(kv[:, :LANES], c, s1, s2)
    v = kv[:, LANES:]
    iq = _dot(xn, w_ref[:, base + HALF:base + 2 * HALF])
    iq_ref[:, 0:128] = _rope128(iq[:, :LANES], c, s1, s2)
    iq_ref[:, 128:256] = _rope128(iq[:, LANES:], c, s1, s2)
    lane = lax.broadcasted_iota(I32, c.shape, 1)
    first = lane < HEAD_DIM
    ikw = _rope128(_dot(xn, w_ref[:, base + 512:base + 640]),
                   jnp.where(first, c, 1.0), jnp.where(first, s1, 0.0), jnp.where(first, s2, 0.0))
    kt_ref[...] = k.T.reshape(kt_ref.shape)
    vt_ref[...] = v.T.reshape(vt_ref.shape)
    ikw_t = ikw.T
    ikt_ref[...] = ikw_t[:HEAD_DIM].reshape(ikt_ref.shape)
    iwt_ref[...] = ikw_t[HEAD_DIM:HEAD_DIM + 8]
    ikb_ref[...] = ikw[:, :HEAD_DIM].astype(BF16)
    ones_col = jnp.where(lane == HEAD_DIM, 1.0, 0.0)
    for j in range(A_KV_HEADS):
        khm_ref[j] = k[:, j * HEAD_DIM:(j + 1) * HEAD_DIM].astype(BF16)
        vj = v if j == 0 else pltpu.roll(v, LANES - j * HEAD_DIM, 1)
        vaug_ref[j] = jnp.where(first, vj, ones_col).astype(BF16)


def _in_proj(x, g, w_all, layer, tabs, t_len, tm):
    n = x.shape[0]
    nt = tabs[0].shape[0] // tm
    per_batch = t_len % tm == 0
    ntb = t_len // tm if per_batch else 1
    row = lambda i: (i, 0)
    tab = lambda i: (i % nt, 0)
    whole = lambda i: (0, 0)
    tmin = lambda i: (0, i)
    hmaj = lambda i: (0, i, 0)
    if per_batch:
        kv_spec = pl.BlockSpec((1, A_KV_HEADS, HEAD_DIM, tm), lambda i: (i // ntb, 0, 0, i % ntb))
        ik_spec = pl.BlockSpec((1, HEAD_DIM, tm), lambda i: (i // ntb, 0, i % ntb))
        kv_shape = jax.ShapeDtypeStruct((n // t_len, A_KV_HEADS, HEAD_DIM, t_len), F32)
        ik_shape = jax.ShapeDtypeStruct((n // t_len, HEAD_DIM, t_len), F32)
    else:
        kv_spec, ik_spec = pl.BlockSpec((KV_WIDTH, tm), tmin), pl.BlockSpec((HEAD_DIM, tm), tmin)
        kv_shape, ik_shape = jax.ShapeDtypeStruct((KV_WIDTH, n), F32), jax.ShapeDtypeStruct((HEAD_DIM, n), F32)
    return pl.pallas_call(
        _inproj_kernel,
        grid=(n // tm,),
        in_specs=[pl.BlockSpec((tm, D_MODEL), row), pl.BlockSpec((1, D_MODEL), whole),
                  pl.BlockSpec((1, D_MODEL, IN_WIDTH), lambda i: (layer, 0, 0), pipeline_mode=pl.Buffered(1)),
                  pl.BlockSpec((tm, LANES), tab), pl.BlockSpec((tm, LANES), tab), pl.BlockSpec((tm, LANES), tab)],
        out_specs=[pl.BlockSpec((tm, A_WIDTH), row), pl.BlockSpec((tm, 4 * B_WIDTH), row),
                   pl.BlockSpec((tm, IDX_WIDTH), row),
                   kv_spec, kv_spec, ik_spec,
                   pl.BlockSpec((A_KV_HEADS, tm, HEAD_DIM), hmaj), pl.BlockSpec((A_KV_HEADS, tm, LANES), hmaj),
                   pl.BlockSpec((tm, HEAD_DIM), row), pl.BlockSpec((8, tm), tmin)],
        out_shape=[jax.ShapeDtypeStruct((n, A_WIDTH), F32), jax.ShapeDtypeStruct((n, 4 * B_WIDTH), F32),
                   jax.ShapeDtypeStruct((n, IDX_WIDTH), F32),
                   kv_shape, kv_shape, ik_shape,
                   jax.ShapeDtypeStruct((A_KV_HEADS, n, HEAD_DIM), BF16),
                   jax.ShapeDtypeStruct((A_KV_HEADS, n, LANES), BF16),
                   jax.ShapeDtypeStruct((n, HEAD_DIM), BF16), jax.ShapeDtypeStruct((8, n), F32)],
        scratch_shapes=[pltpu.VMEM((D_MODEL, IN_WIDTH_PAD), BF16)],
        compiler_params=_cparams("arbitrary"),
        name="in_proj",
    )(x, g, w_all, *tabs)


def _rope_tables(pos):
    half = ROT_DIM // 2
    inv_freq = 1.0 / (ROPE_THETA ** (jnp.arange(half, dtype=F32) * (2.0 / ROT_DIM)))
    ang = pos.astype(F32)[:, None] * inv_freq[None, :]
    cos, sin = jnp.cos(ang), jnp.sin(ang)
    t = pos.shape[0]
    pad = HEAD_DIM - ROT_DIM
    c = jnp.concatenate([cos, cos, jnp.ones((t, pad), F32)], axis=1)
    s1 = jnp.concatenate([-sin, jnp.zeros((t, half + pad), F32)], axis=1)
    s2 = jnp.concatenate([jnp.zeros((t, half), F32), sin, jnp.zeros((t, pad), F32)], axis=1)
    rep = LANES // HEAD_DIM
    return tuple(jnp.tile(a, (1, rep)) for a in (c, s1, s2))


def _dsa_kernel(q_ref, iq_ref, iw_ref, k_ref, v_ref, ik_ref, o_ref, key_ref, hi_ref, lo_ref, acc_ref, m_ref,
                s0_ref, s1_ref, alpha_ref, p_ref, bias_ref, mt0_ref, mt1_ref, *, l_real, pos0, kb_rows, topk, tq):
    i = pl.program_id(1)
    qpos0 = pos0 + i * tq
    kend = jnp.minimum(((qpos0 + tq - 1) // CHUNK + 1) * CHUNK, l_real)
    cb_rows = 2 * kb_rows
    nkc = (kend + cb_rows - 1) // cb_rows
    nkb = 2 * nkc
    qchunk = (qpos0 + lax.broadcasted_iota(I32, (1, tq), 1)) // CHUNK
    row_iota = lax.broadcasted_iota(I32, (kb_rows, tq), 0)

    iq = iq_ref[...].astype(BF16)
    iq4 = jnp.concatenate([iq[:, h * HEAD_DIM:(h + 1) * HEAD_DIM] for h in range(IDX_HEADS)], axis=0)
    iw = iw_ref[...] * (IDX_HEADS ** -0.5 * HEAD_DIM ** -0.5)

    def score_body(kb, carry):
        r0 = pl.multiple_of(kb * kb_rows, kb_rows)
        ikb = ik_ref[pl.ds(r0, kb_rows), :]
        lg4 = lax.dot_general(ikb, iq4, _NT, preferred_element_type=F32)
        acc = jnp.zeros((kb_rows, tq), F32)
        for h in range(IDX_HEADS):
            acc = acc + iw[h:h + 1, :] * jnp.maximum(lg4[:, h * tq:(h + 1) * tq], 0.0)
        bits = pltpu.bitcast(acc + 0.0, I32)
        key = bits ^ ((bits >> 31) & 0x7FFFFFFF)
        kpos = r0 + row_iota
        adm = (kpos // CHUNK <= qchunk) & (kpos < l_real)
        key = jnp.where(adm, key, INT_MIN)
        key_ref[pl.ds(r0, kb_rows), :] = key
        hi_ref[pl.ds(r0, kb_rows), :] = (key >> 16).astype(I16)
        return carry

    lax.fori_loop(0, nkb, score_body, 0)

    def count(pred):
        def body(kc, c):
            r0 = pl.multiple_of(kc * cb_rows, cb_rows)
            hit = pred(key_ref[pl.ds(r0, cb_rows), :], r0)
            return c + jnp.where(hit, 1, 0).reshape(cb_rows // 8, 8, tq).sum(axis=0)
        c = lax.fori_loop(0, nkc, body, jnp.zeros((8, tq), I32))
        return c.sum(axis=0, keepdims=True)

    def count16(ref, pred):
        one, zero = jnp.ones((), BF16), jnp.zeros((), BF16)

        def body(kc, c):
            r0 = pl.multiple_of(kc * cb_rows, cb_rows)
            w = jnp.where(pred(ref[pl.ds(r0, cb_rows), :]), one, zero).reshape(cb_rows // 16, 16, tq)
            parts = [w[r] for r in range(cb_rows // 16)]
            while len(parts) > 1:
                parts = [a + b for a, b in zip(parts[::2], parts[1::2])] + parts[len(parts) & ~1:]
            return c + parts[0]
        c = lax.fori_loop(0, nkc, body, jnp.zeros((16, tq), BF16))
        return c.astype(F32).sum(axis=0, keepdims=True).astype(I32)

    def bisect16(ref, kth):
        c0 = count16(ref, lambda blk: blk >= jnp.zeros((), I16))
        start = jnp.where(c0 >= kth, 0, -32768).astype(I32)

        def bit_body(it, prefix):
            cand = prefix | lax.shift_left(jnp.int32(1), 14 - it)
            c = count16(ref, lambda blk: blk >= cand.astype(I16))
            return jnp.where(c >= kth, cand, prefix)

        return lax.fori_loop(0, 15, bit_body, start)

    tau_hi = bisect16(hi_ref, topk)
    tau_hi16 = tau_hi.astype(I16)
    kth_lo = topk - count16(hi_ref, lambda blk: blk > tau_hi16)

    def lo_body(kc, carry):
        r0 = pl.multiple_of(kc * cb_rows, cb_rows)
        lo = ((key_ref[pl.ds(r0, cb_rows), :] & 0xFFFF) - 32768).astype(I16)
        lo_ref[pl.ds(r0, cb_rows), :] = jnp.where(hi_ref[pl.ds(r0, cb_rows), :] == tau_hi16, lo,
                                                 jnp.full((), -32768, I16))
        return carry

    lax.fori_loop(0, nkc, lo_body, 0)
    tau_lo = bisect16(lo_ref, kth_lo)
    tau = lax.shift_left(tau_hi, 16) | (tau_lo + 32768)
    n_ge = count(lambda blk, r0: blk >= tau)

    @pl.when(jnp.max(jnp.where((n_ge > topk) & (tau != INT_MIN), 1, 0)) > 0)
    def _():
        need = (topk - count(lambda blk, r0: blk > tau)).astype(F32)
        ri = lax.broadcasted_iota(I32, (kb_rows, kb_rows), 0)
        ci = lax.broadcasted_iota(I32, (kb_rows, kb_rows), 1)
        upto = jnp.where(ci <= ri, 1.0, 0.0).astype(BF16)

        def strike(kc, seen):
            for half in range(2):
                rows = pl.ds(pl.multiple_of(kc * cb_rows + half * kb_rows, kb_rows), kb_rows)
                key = key_ref[rows, :]
                tie = key == tau
                rank = _dot(upto, jnp.where(tie, 1.0, 0.0).astype(BF16)) + seen
                key_ref[rows, :] = jnp.where(tie & (rank > need), INT_MIN, key)
                seen = rank[kb_rows - 1:kb_rows, :]
            return seen

        lax.fori_loop(0, nkc, strike, jnp.zeros((1, tq), F32))

    lane_rep = kb_rows // LANES
    tau_c = jnp.tile(jnp.broadcast_to(tau, (LANES, tq)).T, (1, lane_rep))
    acc_ref[...] = jnp.zeros(acc_ref.shape, F32)
    m_ref[...] = jnp.full(m_ref.shape, NEG_BIG, F32)
    qs = (q_ref[...] * (HEAD_DIM ** -0.5 * LOG2E)).astype(BF16)
    q4 = [jnp.concatenate([qs[:, (j * A_GROUP + g) * HEAD_DIM:(j * A_GROUP + g + 1) * HEAD_DIM]
                           for g in range(A_GROUP)], axis=0) for j in range(A_KV_HEADS)]

    def qk_scores(kb, s_slot, mt_slot):
        r0 = pl.multiple_of(jnp.minimum(kb, nkb - 1) * kb_rows, kb_rows)
        key_t = key_ref[pl.ds(r0, kb_rows), :].T
        bias_ref[...] = jnp.where((key_t >= tau_c) & (key_t != INT_MIN), 0.0, NEG_BIG)
        for j in range(A_KV_HEADS):
            s = lax.dot_general(q4[j], k_ref[j, pl.ds(r0, kb_rows), :], _NT, preferred_element_type=F32)
            for g in range(A_GROUP):
                rows = slice(g * tq, (g + 1) * tq)
                sm = s[rows, :] + bias_ref[...]
                s_slot[j, rows, :] = sm
                mt_slot[j, g] = jnp.broadcast_to(jnp.max(sm, axis=1, keepdims=True), (tq, LANES))

    def softmax_pv(kb, s_slot, mt_slot):
        r0 = pl.multiple_of(kb * kb_rows, kb_rows)
        for j in range(A_KV_HEADS):
            for g in range(A_GROUP):
                rows = slice(g * tq, (g + 1) * tq)
                m_old = m_ref[j, g]
                m_new = jnp.maximum(m_old, mt_slot[j, g])
                alpha_ref[j, g] = jnp.exp2(m_old - m_new)
                m_ref[j, g] = m_new
                p_ref[j, rows, :] = jnp.exp2(s_slot[j, rows, :] - jnp.tile(m_new, (1, lane_rep))).astype(BF16)
            pv = _dot(p_ref[j], v_ref[j, pl.ds(r0, kb_rows), :])
            acc_ref[j] = alpha_ref[j] * acc_ref[j] + pv.reshape(A_GROUP, tq, LANES)

    qk_scores(0, s0_ref, mt0_ref)

    def pair_body(i, carry):
        kb = 2 * i
        qk_scores(kb + 1, s1_ref, mt1_ref)
        softmax_pv(kb, s0_ref, mt0_ref)
        qk_scores(kb + 2, s0_ref, mt0_ref)
        softmax_pv(kb + 1, s1_ref, mt1_ref)
        return carry

    lax.fori_loop(0, nkc, pair_body, 0)

    outs = []
    for j in range(A_KV_HEADS):
        acc = acc_ref[j]
        for g in range(A_GROUP):
            outs.append(acc[g, :, :HEAD_DIM] * (1.0 / acc[g, :, HEAD_DIM:HEAD_DIM + 1]))
    o_ref[...] = jnp.concatenate(outs, axis=1)


def _dsa(q, iq, iw_t, k_hm, v_aug, ik, *, bsz, l_real, pos0, kb_rows, topk, tq):
    t_len = q.shape[0] // bsz
    l_pad = ik.shape[0] // bsz
    nq = t_len // tq
    assert l_pad // 16 <= 256, "packed hit counts are summed in bf16, exact only up to 256 per slot"
    kern = functools.partial(_dsa_kernel, l_real=l_real, pos0=pos0, kb_rows=kb_rows, topk=topk, tq=tq)
    qrow = lambda b, i: (b * nq + i, 0)
    return pl.pallas_call(
        kern,
        grid=(bsz, nq),
        in_specs=[pl.BlockSpec((tq, A_WIDTH), qrow), pl.BlockSpec((tq, IDX_WIDTH), qrow),
                  pl.BlockSpec((8, tq), lambda b, i: (0, b * nq + i)),
                  pl.BlockSpec((A_KV_HEADS, l_pad, HEAD_DIM), lambda b, i: (0, b, 0)),
                  pl.BlockSpec((A_KV_HEADS, l_pad, LANES), lambda b, i: (0, b, 0)),
                  pl.BlockSpec((l_pad, HEAD_DIM), lambda b, i: (b, 0))],
        out_specs=pl.BlockSpec((tq, A_WIDTH), qrow),
        out_shape=jax.ShapeDtypeStruct((bsz * t_len, A_WIDTH), F32),
        scratch_shapes=[pltpu.VMEM((l_pad, tq), I32), pltpu.VMEM((l_pad, tq), I16), pltpu.VMEM((l_pad, tq), I16),
                        pltpu.VMEM((A_KV_HEADS, A_GROUP, tq, LANES), F32),
                        pltpu.VMEM((A_KV_HEADS, A_GROUP, tq, LANES), F32),
                        pltpu.VMEM((A_KV_HEADS, A_GROUP * tq, kb_rows), F32),
                        pltpu.VMEM((A_KV_HEADS, A_GROUP * tq, kb_rows), F32),
                        pltpu.VMEM((A_KV_HEADS, A_GROUP, tq, LANES), F32),
                        pltpu.VMEM((A_KV_HEADS, A_GROUP * tq, kb_rows), BF16),
                        pltpu.VMEM((tq, kb_rows), F32),
                        pltpu.VMEM((A_KV_HEADS, A_GROUP, tq, LANES), F32),
                        pltpu.VMEM((A_KV_HEADS, A_GROUP, tq, LANES), F32)],
        compiler_params=_cparams("parallel", "arbitrary"),
        name="dsa",
    )(q, iq, iw_t, k_hm, v_aug, ik)


def _hgrn_kernel(bq_ref, bf_ref, bi_ref, bg_ref, lb_ref, gn_ref, s0_ref, o_ref, sfin_ref,
                 st_ref, q_s, kk_s, b_s, qe_s, kd_s, dec_s, o_s, e_s, *, layer, tb, nseq):
    tstep = pl.program_id(1)
    nchunk = tb // B_CHUNK

    @pl.when(tstep == 0)
    def _():
        st_ref[...] = s0_ref[...]

    lbr = lb_ref[...]
    lbe = jnp.exp(lbr - jnp.max(lbr, axis=0, keepdims=True))
    lbs = lbe / jnp.sum(lbe, axis=0, keepdims=True)
    cum = lbs[0:1, :]
    first = cum
    for r in range(1, layer + 1):
        cum = cum + lbs[r:r + 1, :]
    lb = cum - first

    ri = lax.broadcasted_iota(I32, (tb, tb), 0)
    ci = lax.broadcasted_iota(I32, (tb, tb), 1)
    same = (ri // B_CHUNK) == (ci // B_CHUNK)
    tri = jnp.where(same & (ci <= ri), 1.0, 0.0).astype(BF16)
    ones = jnp.where(same, 1.0, 0.0).astype(BF16)
    for u in range(nseq):
        f = lb + (1.0 - lb) * _sigmoid(bf_ref[u])
        bq = bq_ref[u]
        q = bq * _sigmoid(bq) * (HEAD_DIM ** -0.5)
        g3 = _split3(jnp.log(f) * LOG2E)
        b = _dot(tri, g3[0]) + _dot(tri, g3[1]) + _dot(tri, g3[2])
        blast = _dot(ones, g3[0]) + _dot(ones, g3[1]) + _dot(ones, g3[2])
        q_s[u] = q
        kk_s[u] = 1.0 - f
        b_s[u] = b
        qe_s[u] = q * jnp.exp2(b)
        kd_s[u] = (1.0 - f) * jnp.exp2(blast - b)
        dec_s[u] = jnp.exp2(blast)

    hi = lax.broadcasted_iota(I32, (HALF, HALF), 0) // HEAD_DIM
    hj = lax.broadcasted_iota(I32, (HALF, HALF), 1) // HEAD_DIM
    bd_mask = hi == hj
    bd = jnp.where(bd_mask, 1.0, 0.0).astype(BF16)
    t_iota = lax.broadcasted_iota(I32, (B_CHUNK, B_WIDTH), 0)

    def seg_sum(x, two_pass):
        out = []
        for half in range(2):
            xs = x[:, half * HALF:(half + 1) * HALF]
            if two_pass:
                xh, xl = _split2(xs)
                out.append(_dot(xh, bd) + _dot(xl, bd))
            else:
                out.append(_dot(xs.astype(BF16), bd))
        return jnp.concatenate(out, axis=1)

    def chunk_body(c, carry):
        r0 = pl.multiple_of(c * B_CHUNK, B_CHUNK)
        rows = pl.ds(r0, B_CHUNK)
        for u in range(nseq):
            qc, kc, bc, vc = q_s[u, rows, :], kk_s[u, rows, :], b_s[u, rows, :], bi_ref[u, rows, :]
            qe, kd = qe_s[u, rows, :].astype(BF16), kd_s[u, rows, :].astype(BF16)
            dec = dec_s[u, pl.ds(r0, 1), :]
            for s in range(B_CHUNK):
                ex = jnp.exp2(jnp.where(t_iota >= s, bc - bc[s:s + 1, :], NEG_BIG))
                e_s[u, s * B_CHUNK:(s + 1) * B_CHUNK, :] = qc * kc[s:s + 1, :] * ex
            aexp = seg_sum(e_s[u], False)
            o = jnp.zeros((B_CHUNK, B_WIDTH), F32)
            for s in range(B_CHUNK):
                o = o + aexp[s * B_CHUNK:(s + 1) * B_CHUNK, :] * vc[s:s + 1, :]
            inter = []
            vb = vc.astype(BF16)
            for half in range(2):
                cols = slice(half * HALF, (half + 1) * HALF)
                st = st_ref[u, half]
                inter.append(lax.dot_general(qe[:, cols], st.astype(BF16), _NT, preferred_element_type=F32))
                upd = lax.dot_general(vb[:, cols], kd[:, cols], _TN, preferred_element_type=F32)
                st_ref[u, half] = st * dec[:, cols] + jnp.where(bd_mask, upd, 0.0)
            o_s[u, rows, :] = o + jnp.concatenate(inter, axis=1)
        return carry

    lax.fori_loop(0, nchunk, chunk_body, 0)

    for u in range(nseq):
        o = o_s[u]
        ms = seg_sum(o * o, True) * (1.0 / HEAD_DIM)
        bg = bg_ref[u]
        o_ref[u] = o * lax.rsqrt(ms + EPS) * gn_ref[...] * (bg * _sigmoid(bg))

    @pl.when(tstep == pl.num_programs(1) - 1)
    def _():
        sfin_ref[...] = st_ref[...]


def _hgrn(hb, lb_raw, gn_tiled, s0_bd, *, layer, bsz, t_len, tb):
    nt = t_len // tb
    nseq = 2 if bsz % 2 == 0 else 1
    hb3 = hb.reshape(bsz, t_len, 4 * B_WIDTH)
    col = lambda j: (lambda b, t: (b, t, j))
    scr = lambda r: pltpu.VMEM((nseq, r, B_WIDTH), F32)
    kern = functools.partial(_hgrn_kernel, layer=layer, tb=tb, nseq=nseq)
    blk = (nseq, tb, B_WIDTH)
    out, s_fin = pl.pallas_call(
        kern,
        grid=(bsz // nseq, nt),
        in_specs=[pl.BlockSpec(blk, col(0)), pl.BlockSpec(blk, col(1)), pl.BlockSpec(blk, col(2)),
                  pl.BlockSpec(blk, col(3)),
                  pl.BlockSpec(lb_raw.shape, lambda b, t: (0, 0)),
                  pl.BlockSpec((1, B_WIDTH), lambda b, t: (0, 0)),
                  pl.BlockSpec((nseq, 2, HALF, HALF), lambda b, t: (b, 0, 0, 0))],
        out_specs=[pl.BlockSpec(blk, lambda b, t: (b, t, 0)),
                   pl.BlockSpec((nseq, 2, HALF, HALF), lambda b, t: (b, 0, 0, 0))],
        out_shape=[jax.ShapeDtypeStruct((bsz, t_len, B_WIDTH), F32),
                   jax.ShapeDtypeStruct((bsz, 2, HALF, HALF), F32)],
        scratch_shapes=[pltpu.VMEM((nseq, 2, HALF, HALF), F32), scr(tb), scr(tb), scr(tb), scr(tb), scr(tb),
                        scr(tb), scr(tb), scr(B_CHUNK * B_CHUNK)],
        compiler_params=_cparams("parallel", "arbitrary"),
        name="hgrn2",
    )(hb3, hb3, hb3, hb3, lb_raw, gn_tiled, s0_bd)
    return out.reshape(bsz * t_len, B_WIDTH), s_fin


def _state_to_tiles(s):
    bsz = s.shape[0]
    st = jnp.swapaxes(s, -1, -2).reshape(bsz, 2, B_HEADS // 2, HEAD_DIM, HEAD_DIM)
    eye = jnp.eye(B_HEADS // 2, dtype=s.dtype)
    t = st[:, :, :, :, None, :] * eye[None, None, :, None, :, None]
    return t.reshape(bsz, 2, HALF, HALF)


def _tiles_to_state(t):
    bsz = t.shape[0]
    t6 = t.reshape(bsz, 2, B_HEADS // 2, HEAD_DIM, B_HEADS // 2, HEAD_DIM)
    diag = jnp.stack([t6[:, :, h, :, h, :] for h in range(B_HEADS // 2)], axis=2)
    return jnp.swapaxes(diag.reshape(bsz, B_HEADS, HEAD_DIM, HEAD_DIM), -1, -2)


def _normmm_kernel(x_ref, g_ref, w_ref, o_ref):
    o_ref[...] = _dot(_rms(x_ref[...], g_ref[...]).astype(BF16), w_ref[...])


def _norm_matmul(x, g, w, tm):
    n, m = x.shape[0], w.shape[1]
    return pl.pallas_call(
        _normmm_kernel,
        grid=(n // tm,),
        in_specs=[pl.BlockSpec((tm, D_MODEL), lambda i: (i, 0)), pl.BlockSpec((1, D_MODEL), lambda i: (0, 0)),
                  pl.BlockSpec((D_MODEL, m), lambda i: (0, 0))],
        out_specs=pl.BlockSpec((tm, m), lambda i: (i, 0)),
        out_shape=jax.ShapeDtypeStruct((n, m), F32),
        compiler_params=_cparams("parallel"),
        name="norm_matmul",
    )(x, g, w)


def _swiglu_mid(xn, w1, w3):
    a = _dot(xn, w1)
    return (a * _sigmoid(a) * _dot(xn, w3)).astype(BF16)


def _postmix_kernel(oa_ref, ob_ref, x_ref, wa_ref, wb_ref, gm_ref, wq_ref, mk_ref, mv_ref, wo_ref, *rest,
                    with_ffn, final, chunks):
    o_ref = rest[-1]
    x = (x_ref[...] + _dot(oa_ref[...].astype(BF16), wa_ref[...]) + _dot(ob_ref[...].astype(BF16), wb_ref[...]))
    q = _dot(_rms(x, gm_ref[...]).astype(BF16), wq_ref[...]).astype(BF16)
    nbt = mk_ref.shape[0]
    rpb = x.shape[0] // nbt
    att = []
    for bb in range(nbt):
        rows = slice(bb * rpb, (bb + 1) * rpb)
        heads = []
        for h in range(MEM_HEADS):
            cols = slice(h * MEM_HEAD_DIM, (h + 1) * MEM_HEAD_DIM)
            s = lax.dot_general(q[rows, cols], mk_ref[bb, :, cols], _NT,
                                preferred_element_type=F32) * (MEM_HEAD_DIM ** -0.5)
            p = jnp.exp(s - jnp.max(s, axis=-1, keepdims=True))
            p = p * (1.0 / jnp.sum(p, axis=-1, keepdims=True))
            heads.append(_dot(p.astype(BF16), mv_ref[bb, :, cols]).astype(BF16))
        att.append(jnp.concatenate(heads, axis=1))
    y = x + _dot(att[0] if nbt == 1 else jnp.concatenate(att, axis=0), wo_ref[...])
    if with_ffn:
        gffn_ref, w1_ref, w3_ref, w2_ref, gf_ref = rest[:5]
        yn = _rms(y, gffn_ref[...]).astype(BF16)
        lo = 0
        for width in chunks:
            cols = slice(lo, lo + width)
            y = y + _dot(_swiglu_mid(yn, w1_ref[:, cols], w3_ref[:, cols]), w2_ref[cols, :])
            lo += width
        if final:
            y = _rms(y, gf_ref[...])
    o_ref[...] = y


def _postmix(oa, ob, x, wa, wb, gm, wq, mk, mv, wo, ffn=None, *, rows_per_batch, tm, final=False):
    n = x.shape[0]
    n_mem = mk.shape[1]
    row = lambda i: (i, 0)
    whole = lambda i: (0, 0)
    if tm <= rows_per_batch:
        per = rows_per_batch // tm
        mem_spec = pl.BlockSpec((1, n_mem, D_MODEL), lambda i: (i // per, 0, 0))
    else:
        mem_spec = pl.BlockSpec((tm // rows_per_batch, n_mem, D_MODEL), lambda i: (i, 0, 0))
    const = lambda shape: pl.BlockSpec(shape, whole, pipeline_mode=pl.Buffered(1))
    in_specs = [pl.BlockSpec((tm, A_WIDTH), row), pl.BlockSpec((tm, B_WIDTH), row), pl.BlockSpec((tm, D_MODEL), row),
                const((A_WIDTH, D_MODEL)), const((B_WIDTH, D_MODEL)), const((1, D_MODEL)),
                const((D_MODEL, D_MODEL)),
                mem_spec, mem_spec, const((D_MODEL, D_MODEL))]
    args = [oa, ob, x, wa, wb, gm, wq, mk, mv, wo]
    chunks = ()
    if ffn is not None:
        d_ff = ffn[1].shape[1]
        step = 4 * HALF
        chunks = tuple(min(step, d_ff - lo) for lo in range(0, d_ff, step))
        in_specs += [const((1, D_MODEL)), const((D_MODEL, d_ff)), const((D_MODEL, d_ff)), const((d_ff, D_MODEL)),
                     const((1, D_MODEL))]
        args += list(ffn)
    return pl.pallas_call(
        functools.partial(_postmix_kernel, with_ffn=ffn is not None, final=final, chunks=chunks),
        grid=(n // tm,),
        in_specs=in_specs,
        out_specs=pl.BlockSpec((tm, D_MODEL), row),
        out_shape=jax.ShapeDtypeStruct((n, D_MODEL), F32),
        compiler_params=_cparams("parallel"),
        name="post_mix",
    )(*args)


def _router_kernel(x_ref, g_ref, rt_ref, comb_ref, sel_ref):
    xn = _rms(x_ref[...], g_ref[...])
    logits = lax.dot_general(rt_ref[...], xn, _NT, preferred_element_type=F32,
                             precision=lax.Precision.HIGHEST)
    e_idx = lax.broadcasted_iota(I32, logits.shape, 0)
    m1 = jnp.max(logits, axis=0, keepdims=True)
    i1 = jnp.min(jnp.where(logits == m1, e_idx, N_EXPERTS), axis=0, keepdims=True)
    rest = jnp.where(e_idx == i1, -jnp.inf, logits)
    m2 = jnp.max(rest, axis=0, keepdims=True)
    i2 = jnp.min(jnp.where(rest == m2, e_idx, N_EXPERTS), axis=0, keepdims=True)
    e2 = jnp.exp(m2 - m1)
    g1 = 1.0 / (1.0 + e2)
    comb_ref[...] = jnp.where(e_idx == i1, g1, 0.0) + jnp.where(e_idx == i2, e2 * g1, 0.0)
    sel_ref[...] = jnp.where((e_idx == i1) | (e_idx == i2), 1.0, 0.0)


def _router(x, g, rt, tm):
    n = x.shape[0]
    return pl.pallas_call(
        _router_kernel,
        grid=(n // tm,),
        in_specs=[pl.BlockSpec((tm, D_MODEL), lambda i: (i, 0)), pl.BlockSpec((1, D_MODEL), lambda i: (0, 0)),
                  pl.BlockSpec((N_EXPERTS, D_MODEL), lambda i: (0, 0))],
        out_specs=[pl.BlockSpec((N_EXPERTS, tm), lambda i: (0, i)), pl.BlockSpec((N_EXPERTS, tm), lambda i: (0, i))],
        out_shape=[jax.ShapeDtypeStruct((N_EXPERTS, n), F32), jax.ShapeDtypeStruct((N_EXPERTS, n), F32)],
        compiler_params=_cparams("parallel"),
        name="moe_router",
    )(x, g, rt)


def _moe_kernel(x_ref, g_ref, selt_ref, sel_ref, comb_ref, w1_ref, w3_ref, w2_ref, gf_ref, o_ref,
                xn_ref, posr_ref, posc_ref, cnt_ref, xg_ref, yg_ref, *, final, nb, sub):
    e = pl.program_id(1)
    j = pl.program_id(2)
    last_j = pl.num_programs(2) - 1

    @pl.when((e == 0) & (j == 0))
    def _():
        x = x_ref[...]
        xn_ref[...] = _rms(x, g_ref[...]).astype(BF16)
        o_ref[...] = x
        cc = min(nb, 256)
        ri = lax.broadcasted_iota(I32, (cc, cc), 0)
        ci = lax.broadcasted_iota(I32, (cc, cc), 1)
        before_r = jnp.where(ri < ci, 1.0, 0.0).astype(BF16)
        before_c = jnp.where(ci < ri, 1.0, 0.0).astype(BF16)
        run_r = jnp.zeros((N_EXPERTS, 1), F32)
        run_c = jnp.zeros((1, N_EXPERTS), F32)
        for c in range(nb // cc):
            cols = slice(c * cc, (c + 1) * cc)
            mr = selt_ref[:, cols]
            posr_ref[:, cols] = jnp.where(mr > 0.0, _dot(mr.astype(BF16), before_r) + run_r, -1.0)
            run_r = run_r + jnp.sum(mr, axis=1, keepdims=True)
            mc = sel_ref[cols, :]
            posc_ref[cols, :] = jnp.where(mc > 0.0, _dot(before_c, mc.astype(BF16)) + run_c, -1.0)
            run_c = run_c + jnp.sum(mc, axis=0, keepdims=True)
        for k in range(N_EXPERTS):
            cnt_ref[k] = run_c[0, k].astype(I32)

    cnt = cnt_ref[e]
    nfull = cnt // sub
    rem = cnt - nfull * sub
    tail_sizes = [m for m in (64, 128) if m < sub] + [sub]

    def for_tiles(fn):
        def body(s, carry):
            fn(pl.multiple_of(s * sub, sub), sub)
            return carry
        lax.fori_loop(0, nfull, body, 0)
        r0 = pl.multiple_of(nfull * sub, sub)
        lo = 0
        for m in tail_sizes:
            @pl.when((rem > lo) & (rem <= m))
            def _():
                fn(r0, m)
            lo = m

    def gather_matrix(r0, m):
        slot_r = posr_ref[pl.ds(e, 1), :]
        want = (r0 + lax.broadcasted_iota(I32, (m, nb), 0)).astype(F32)
        return jnp.where(slot_r == want, 1.0, 0.0).astype(BF16)

    def gather(r0, m):
        xg_ref[pl.ds(r0, m), :] = _dot(gather_matrix(r0, m), xn_ref[...]).astype(BF16)

    def ffn(r0, m):
        rows = pl.ds(r0, m)
        part = _dot(_swiglu_mid(xg_ref[rows, :], w1_ref[0], w3_ref[0]), w2_ref[0])

        @pl.when(j == 0)
        def _():
            yg_ref[rows, :] = part

        @pl.when(j > 0)
        def _():
            yg_ref[rows, :] += part

    @pl.when(j == 0)
    def _():
        for_tiles(gather)

    for_tiles(ffn)

    @pl.when(j == last_j)
    def _():
        lane8 = lax.broadcasted_iota(I32, (nb, N_EXPERTS), 1)
        slot_c = jnp.sum(jnp.where(lane8 == e, posc_ref[...], 0.0), axis=1, keepdims=True)
        gate_c = jnp.sum(jnp.where(lane8 == e, comb_ref[...], 0.0), axis=1, keepdims=True)
        gate3 = jnp.concatenate(_split3(gate_c), axis=1)

        def scatter(r0, m):
            gate = jnp.sum(_dot(gather_matrix(r0, m), gate3), axis=1, keepdims=True)
            y = (yg_ref[pl.ds(r0, m), :] * gate).astype(BF16)
            want = (r0 + lax.broadcasted_iota(I32, (nb, m), 1)).astype(F32)
            o_ref[...] += _dot(jnp.where(slot_c == want, 1.0, 0.0).astype(BF16), y)

        for_tiles(scatter)

        if final:
            @pl.when(e == pl.num_programs(1) - 1)
            def _():
                o_ref[...] = _rms(o_ref[...], gf_ref[...])


def _moe(x, g, sel_t, sel, comb, w1, w3, w2, gf, *, final, nb, tf):
    n = x.shape[0]
    d_ff = w1.shape[2]
    sub = min(nb, MOE_TILE_ROWS)
    row = lambda i, e, j: (i, 0)
    whole = lambda i, e, j: (0, 0)
    return pl.pallas_call(
        functools.partial(_moe_kernel, final=final, nb=nb, sub=sub),
        grid=(n // nb, N_EXPERTS, d_ff // tf),
        in_specs=[pl.BlockSpec((nb, D_MODEL), row), pl.BlockSpec((1, D_MODEL), whole),
                  pl.BlockSpec((N_EXPERTS, nb), lambda i, e, j: (0, i)),
                  pl.BlockSpec((nb, N_EXPERTS), row), pl.BlockSpec((nb, N_EXPERTS), row),
                  pl.BlockSpec((1, D_MODEL, tf), lambda i, e, j: (e, 0, j)),
                  pl.BlockSpec((1, D_MODEL, tf), lambda i, e, j: (e, 0, j)),
                  pl.BlockSpec((1, tf, D_MODEL), lambda i, e, j: (e, j, 0)), pl.BlockSpec((1, D_MODEL), whole)],
        out_specs=pl.BlockSpec((nb, D_MODEL), row),
        out_shape=jax.ShapeDtypeStruct((n, D_MODEL), F32),
        scratch_shapes=[pltpu.VMEM((nb, D_MODEL), BF16), pltpu.VMEM((N_EXPERTS, nb), F32),
                        pltpu.VMEM((nb, N_EXPERTS), F32), pltpu.SMEM((N_EXPERTS,), I32),
                        pltpu.VMEM((nb + sub, D_MODEL), BF16), pltpu.VMEM((nb + sub, D_MODEL), F32)],
        compiler_params=_cparams("parallel", "arbitrary", "arbitrary"),
        name="moe",
    )(x, g, sel_t, sel, comb, w1, w3, w2, gf)


def _row_tile(n, cap):
    t = min(n, cap)
    assert n % t == 0, (n, cap)
    return t


def _ff_tile(d_ff):
    for parts in (2, 4, 11, 22):
        if d_ff % parts == 0 and (d_ff // parts) % LANES == 0:
            return d_ff // parts
    return d_ff


def _mixer(x, tabs, w, *, bsz, t_len, pos0, past, s0, layer):
    n = bsz * t_len
    qa, hb, iq, k_t, v_t, ik_t, k_hm, v_aug, ik_b, iw_t = _in_proj(x, w["norm_mix"], w["w_in"], layer, tabs,
                                                                   t_len, _row_tile(n, 512))
    l_real = t_len if past is None else past[0].shape[1] + t_len
    topk = min(TOPK_MAX, l_real // 4)
    tq = 2 * LANES if t_len % (2 * LANES) == 0 else LANES
    kb_rows = 256 if l_real % 256 == 0 else LANES
    l_pad = -(-l_real // (2 * kb_rows)) * (2 * kb_rows)
    t_pad = -(-t_len // tq) * tq
    if past is not None:
        pk, pv, pik = past
        one = jnp.ones(pv.shape[:2] + (A_KV_HEADS, 1), BF16)
        zero = jnp.zeros(pv.shape[:2] + (A_KV_HEADS, LANES - HEAD_DIM - 1), BF16)
        pv_aug = jnp.concatenate([pv.astype(BF16), one, zero], axis=-1)
        hm = lambda a: jnp.transpose(a, (2, 0, 1, 3))
        k_hm = jnp.concatenate([hm(pk.astype(BF16)), k_hm.reshape(A_KV_HEADS, bsz, t_len, HEAD_DIM)], axis=2)
        v_aug = jnp.concatenate([hm(pv_aug), v_aug.reshape(A_KV_HEADS, bsz, t_len, LANES)], axis=2)
        ik_b = jnp.concatenate([pik.astype(BF16), ik_b.reshape(bsz, t_len, HEAD_DIM)], axis=1)
    if l_pad != l_real or past is not None:
        padl = l_pad - l_real
        k_hm = jnp.pad(k_hm.reshape(A_KV_HEADS, bsz, l_real, HEAD_DIM), ((0, 0), (0, 0), (0, padl), (0, 0)))
        v_aug = jnp.pad(v_aug.reshape(A_KV_HEADS, bsz, l_real, LANES), ((0, 0), (0, 0), (0, padl), (0, 0)))
        ik_b = jnp.pad(ik_b.reshape(bsz, l_real, HEAD_DIM), ((0, 0), (0, padl), (0, 0)))
        k_hm = k_hm.reshape(A_KV_HEADS, bsz * l_pad, HEAD_DIM)
        v_aug = v_aug.reshape(A_KV_HEADS, bsz * l_pad, LANES)
        ik_b = ik_b.reshape(bsz * l_pad, HEAD_DIM)
    q_in, iq_in, iw_in = qa, iq, iw_t
    if t_pad != t_len:
        padq = lambda a: jnp.pad(a.reshape(bsz, t_len, -1),
                                 ((0, 0), (0, t_pad - t_len), (0, 0))).reshape(bsz * t_pad, -1)
        q_in, iq_in = padq(qa), padq(iq)
        iw_in = jnp.pad(iw_t.reshape(8, bsz, t_len), ((0, 0), (0, 0), (0, t_pad - t_len))).reshape(8, bsz * t_pad)
    oa = _dsa(q_in, iq_in, iw_in, k_hm, v_aug, ik_b, bsz=bsz, l_real=l_real, pos0=pos0, kb_rows=kb_rows,
              topk=topk, tq=tq)
    if t_pad != t_len:
        oa = oa.reshape(bsz, t_pad, A_WIDTH)[:, :t_len].reshape(n, A_WIDTH)
    ob, s_tiles = _hgrn(hb, w["hgrn_lb"], w["gnorm"], _state_to_tiles(s0), layer=layer, bsz=bsz, t_len=t_len,
                        tb=_row_tile(t_len, 256))
    return oa, ob, k_t, v_t, ik_t, _tiles_to_state(s_tiles)


def _from_time_minor(a, bsz, t_len, heads):
    if a.ndim == 2:
        a = a.reshape((-1, HEAD_DIM, bsz, t_len))
        a = jnp.transpose(a, (2, 3, 0, 1))
    else:
        a = jnp.moveaxis(a, -1, 1)
        a = a.reshape(bsz, t_len, -1, HEAD_DIM)
    return a if heads is not None else a.reshape(bsz, t_len, HEAD_DIM)


def kernel(x_prompt, x_sample, cache_k, cache_v, cache_idx_k, state_hgrn, cache_mem_k, cache_mem_v, mem_prompt,
           norm_mix, w_in, hgrn_lb, hgrn_gnorm, w_out, norm_mem, norm_memkv, w_mq, w_mk, w_mv, w_mo,
           norm_ffn, ffn_w1, ffn_w3, ffn_w2, moe_router, moe_w1, moe_w3, moe_w2, norm_final):
    depth = w_in.shape[0]
    bp, tp, _ = x_prompt.shape
    bs, ts, _ = x_sample.shape
    past = cache_k.shape[2]
    n_mem = mem_prompt.shape[1]
    tabs_p = _rope_tables(jnp.arange(tp, dtype=I32))
    tabs_s = tuple(jnp.tile(a, (bs, 1)) for a in _rope_tables(past + jnp.arange(ts, dtype=I32)))
    xp = x_prompt.reshape(bp * tp, D_MODEL)
    xs = x_sample.reshape(bs * ts, D_MODEL)
    mem = mem_prompt.reshape(bp * n_mem, D_MODEL)
    s0_p = jnp.zeros((bp, B_HEADS, HEAD_DIM, HEAD_DIM), F32)
    gfin = norm_final.reshape(1, D_MODEL)
    row2 = lambda a: a.reshape(1, -1)
    outs = {k: [] for k in ("kp", "vp", "ikp", "sp", "mkp", "mvp", "ks", "vs", "iks", "ss")}
    for l in range(depth):
        w = {"norm_mix": row2(norm_mix[l]), "w_in": w_in, "hgrn_lb": hgrn_lb.astype(F32),
             "gnorm": row2(jnp.tile(hgrn_gnorm[l], B_HEADS)),
             "w_out_a": w_out[l, :A_WIDTH].astype(BF16), "w_out_b": w_out[l, A_WIDTH:].astype(BF16)}
        oa_p, ob_p, kp, vp, ikp, sp = _mixer(xp, tabs_p, w, bsz=bp, t_len=tp, pos0=0, past=None, s0=s0_p, layer=l)
        oa_s, ob_s, ks, vs, iks, ss = _mixer(xs, tabs_s, w, bsz=bs, t_len=ts, pos0=past,
                                             past=(cache_k[l], cache_v[l], cache_idx_k[l]), s0=state_hgrn[l],
                                             layer=l)
        wkv = jnp.concatenate([w_mk[l], w_mv[l]], axis=1).astype(BF16)
        mkv = _norm_matmul(mem, row2(norm_memkv[l]), wkv, _row_tile(bp * n_mem, 256))
        mk_p = mkv[:, :D_MODEL].reshape(bp, n_mem, D_MODEL)
        mv_p = mkv[:, D_MODEL:].reshape(bp, n_mem, D_MODEL)
        mk_s = cache_mem_k[l].reshape(bs, n_mem, D_MODEL).astype(BF16)
        mv_s = cache_mem_v[l].reshape(bs, n_mem, D_MODEL).astype(BF16)
        wq, wo, gm = w_mq[l].astype(BF16), w_mo[l].astype(BF16), row2(norm_mem[l])
        final = l == depth - 1
        gf = row2(norm_ffn[l])
        tail = (w["w_out_a"], w["w_out_b"], gm, wq)
        if l % 2 == 0:
            dense = (gf, ffn_w1[l // 2].astype(BF16), ffn_w3[l // 2].astype(BF16), ffn_w2[l // 2].astype(BF16), gfin)
            xp = _postmix(oa_p, ob_p, xp, *tail, mk_p.astype(BF16), mv_p.astype(BF16), wo, dense,
                          rows_per_batch=tp, tm=_row_tile(tp, 512), final=final)
            xs = _postmix(oa_s, ob_s, xs, *tail, mk_s, mv_s, wo, dense,
                          rows_per_batch=ts, tm=_row_tile(bs * ts, 512), final=final)
        else:
            xp = _postmix(oa_p, ob_p, xp, *tail, mk_p.astype(BF16), mv_p.astype(BF16), wo,
                          rows_per_batch=tp, tm=_row_tile(tp, 512))
            xs = _postmix(oa_s, ob_s, xs, *tail, mk_s, mv_s, wo, rows_per_batch=ts, tm=_row_tile(bs * ts, 512))
            w1, w3, w2 = moe_w1[l // 2].astype(BF16), moe_w3[l // 2].astype(BF16), moe_w2[l // 2].astype(BF16)
            rt = moe_router[l // 2].T
            tf = _ff_tile(w1.shape[2])
            comb_p, sel_p = _router(xp, gf, rt, _row_tile(bp * tp, 512))
            comb_s, sel_s = _router(xs, gf, rt, _row_tile(bs * ts, 512))
            xp = _moe(xp, gf, sel_p, sel_p.T, comb_p.T, w1, w3, w2, gfin, final=final,
                      nb=_row_tile(bp * tp, 1024), tf=tf)
            xs = _moe(xs, gf, sel_s, sel_s.T, comb_s.T, w1, w3, w2, gfin, final=final,
                      nb=_row_tile(bs * ts, 1024), tf=tf)
        outs["kp"].append(_from_time_minor(kp, bp, tp, A_KV_HEADS))
        outs["vp"].append(_from_time_minor(vp, bp, tp, A_KV_HEADS))
        outs["ikp"].append(_from_time_minor(ikp, bp, tp, None)); outs["sp"].append(sp)
        outs["mkp"].append(mk_p.reshape(bp, n_mem, MEM_HEADS, MEM_HEAD_DIM))
        outs["mvp"].append(mv_p.reshape(bp, n_mem, MEM_HEADS, MEM_HEAD_DIM))
        outs["ks"].append(_from_time_minor(ks, bs, ts, A_KV_HEADS))
        outs["vs"].append(_from_time_minor(vs, bs, ts, A_KV_HEADS))
        outs["iks"].append(_from_time_minor(iks, bs, ts, None)); outs["ss"].append(ss)
    y_prompt = xp.reshape(bp, tp, D_MODEL)
    y_sample = xs.reshape(bs, ts, D_MODEL)
    st = lambda k: jnp.stack(outs[k])
    return (y_prompt, y_sample, st("kp"), st("vp"), st("ikp"), st("sp"), st("mkp"), st("mvp"),
            st("ks"), st("vs"), st("iks"), st("ss"))
```

```python
import functools

import numpy as np
import jax
import jax.numpy as jnp
from jax import lax
from jax.experimental import pallas as pl
from jax.experimental.pallas import tpu as pltpu

F32 = jnp.float32
BF16 = jnp.bfloat16
I32 = jnp.int32
I16 = jnp.int16

D_MODEL = 1024
CHUNK = 64
EPS = 1e-6
ROPE_THETA = 500000.0
HEAD_DIM = 64
ROT_DIM = HEAD_DIM // 4
A_HEADS = 8
A_KV_HEADS = 2
A_GROUP = A_HEADS // A_KV_HEADS
A_WIDTH = A_HEADS * HEAD_DIM
KV_WIDTH = A_KV_HEADS * HEAD_DIM
IDX_HEADS = 4
IDX_WIDTH = IDX_HEADS * HEAD_DIM
TOPK_MAX = 256
B_HEADS = 8
B_WIDTH = B_HEADS * HEAD_DIM
B_CHUNK = 16
MEM_HEADS = 4
MEM_HEAD_DIM = D_MODEL // MEM_HEADS
N_EXPERTS = 8
MOE_TILE_ROWS = 256
LANES = 128
HALF = 256
IN_SPLITS = (A_WIDTH, KV_WIDTH, KV_WIDTH, IDX_WIDTH, HEAD_DIM, IDX_HEADS, B_WIDTH, B_WIDTH, B_WIDTH, B_WIDTH)
IN_WIDTH = sum(IN_SPLITS)
IN_WIDTH_PAD = 3200
VMEM_LIMIT = 56 * 1024 * 1024
INT_MIN = -2147483648
NEG_BIG = -1e30
LOG2E = 1.4426950408889634

_NT = (((1,), (1,)), ((), ()))
_TN = (((0,), (0,)), ((), ()))


def _cparams(*sem):
    return pltpu.CompilerParams(dimension_semantics=sem, vmem_limit_bytes=VMEM_LIMIT)


def _rms(x, g):
    return x * lax.rsqrt(jnp.mean(x * x, axis=-1, keepdims=True) + EPS) * g


def _sigmoid(x):
    return 1.0 / (1.0 + jnp.exp(-x))


def _dot(a, b):
    return jnp.dot(a, b, preferred_element_type=F32)


def _split2(x):
    hi = x.astype(BF16)
    lo = (x - hi.astype(F32)).astype(BF16)
    return hi, lo


def _split3(x):
    hi = x.astype(BF16)
    r = x - hi.astype(F32)
    mid = r.astype(BF16)
    lo = (r - mid.astype(F32)).astype(BF16)
    return hi, mid, lo


def _rope128(xb, c, s1, s2):
    return xb * c + pltpu.roll(xb, LANES - ROT_DIM // 2, 1) * s1 + pltpu.roll(xb, ROT_DIM // 2, 1) * s2


def _inproj_kernel(x_ref, g_ref, wraw_ref, c_ref, s1_ref, s2_ref,
                   qa_ref, hb_ref, iq_ref, kt_ref, vt_ref, ikt_ref, khm_ref, vaug_ref, ikb_ref, iwt_ref, w_ref):
    @pl.when(pl.program_id(0) == 0)
    def _():
        src = np.cumsum([0] + list(IN_SPLITS))
        dst = 0
        for i in (0, 6, 7, 8, 9, 1, 2, 3):
            width = int(src[i + 1] - src[i])
            w_ref[:, dst:dst + width] = wraw_ref[0, :, int(src[i]):int(src[i + 1])].astype(BF16)
            dst += width
        tail = jnp.concatenate([wraw_ref[0, :, int(src[4]):int(src[6])],
                                jnp.zeros((D_MODEL, LANES - HEAD_DIM - IDX_HEADS), F32)], axis=1)
        w_ref[:, dst:dst + LANES] = tail.astype(BF16)

    xn = _rms(x_ref[...], g_ref[...]).astype(BF16)
    c, s1, s2 = c_ref[...], s1_ref[...], s2_ref[...]
    for j in range(A_WIDTH // HALF):
        h = _dot(xn, w_ref[:, j * HALF:(j + 1) * HALF])
        for i in range(HALF // LANES):
            lo = j * HALF + i * LANES
            qa_ref[:, lo:lo + LANES] = _rope128(h[:, i * LANES:(i + 1) * LANES], c, s1, s2)
    for j in range(4):
        lo = A_WIDTH + j * B_WIDTH
        hb_ref[:, j * B_WIDTH:(j + 1) * B_WIDTH] = _dot(xn, w_ref[:, lo:lo + B_WIDTH])
    base = A_WIDTH + 4 * B_WIDTH
    kv = _dot(xn, w_ref[:, base:base + HALF])
    k = _rope128(kv[:, :LANES], c, s1, s2)
    v = kv[:, LANES:]
    iq = _dot(xn, w_ref[:, base + HALF:base + 2 * HALF])
    iq_ref[:, 0:128] = _rope128(iq[:, :LANES], c, s1, s2)
    iq_ref[:, 128:256] = _rope128(iq[:, LANES:], c, s1, s2)
    lane = lax.broadcasted_iota(I32, c.shape, 1)
    first = lane < HEAD_DIM
    ikw = _rope128(_dot(xn, w_ref[:, base + 512:base + 640]),
                   jnp.where(first, c, 1.0), jnp.where(first, s1, 0.0), jnp.where(first, s2, 0.0))
    kt_ref[...] = k.T.reshape(kt_ref.shape)
    vt_ref[...] = v.T.reshape(vt_ref.shape)
    ikw_t = ikw.T
    ikt_ref[...] = ikw_t[:HEAD_DIM].reshape(ikt_ref.shape)
    iwt_ref[...] = ikw_t[HEAD_DIM:HEAD_DIM + 8]
    ikb_ref[...] = ikw[:, :HEAD_DIM].astype(BF16)
    ones_col = jnp.where(lane == HEAD_DIM, 1.0, 0.0)
    for j in range(A_KV_HEADS):
        khm_ref[j] = k[:, j * HEAD_DIM:(j + 1) * HEAD_DIM].astype(BF16)
        vj = v if j == 0 else pltpu.roll(v, LANES - j * HEAD_DIM, 1)
        vaug_ref[j] = jnp.where(first, vj, ones_col).astype(BF16)


def _in_proj(x, g, w_all, layer, tabs, t_len, tm):
    n = x.shape[0]
    nt = tabs[0].shape[0] // tm
    per_batch = t_len % tm == 0
    ntb = t_len // tm if per_batch else 1
    row = lambda i: (i, 0)
    tab = lambda i: (i % nt, 0)
    whole = lambda i: (0, 0)
    tmin = lambda i: (0, i)
    hmaj = lambda i: (0, i, 0)
    if per_batch:
        kv_spec = pl.BlockSpec((1, A_KV_HEADS, HEAD_DIM, tm), lambda i: (i // ntb, 0, 0, i % ntb))
        ik_spec = pl.BlockSpec((1, HEAD_DIM, tm), lambda i: (i // ntb, 0, i % ntb))
        kv_shape = jax.ShapeDtypeStruct((n // t_len, A_KV_HEADS, HEAD_DIM, t_len), F32)
        ik_shape = jax.ShapeDtypeStruct((n // t_len, HEAD_DIM, t_len), F32)
    else:
        kv_spec, ik_spec = pl.BlockSpec((KV_WIDTH, tm), tmin), pl.BlockSpec((HEAD_DIM, tm), tmin)
        kv_shape, ik_shape = jax.ShapeDtypeStruct((KV_WIDTH, n), F32), jax.ShapeDtypeStruct((HEAD_DIM, n), F32)
    return pl.pallas_call(
        _inproj_kernel,
        grid=(n // tm,),
        in_specs=[pl.BlockSpec((tm, D_MODEL), row), pl.BlockSpec((1, D_MODEL), whole),
                  pl.BlockSpec((1, D_MODEL, IN_WIDTH), lambda i: (layer, 0, 0), pipeline_mode=pl.Buffered(1)),
                  pl.BlockSpec((tm, LANES), tab), pl.BlockSpec((tm, LANES), tab), pl.BlockSpec((tm, LANES), tab)],
        out_specs=[pl.BlockSpec((tm, A_WIDTH), row), pl.BlockSpec((tm, 4 * B_WIDTH), row),
                   pl.BlockSpec((tm, IDX_WIDTH), row),
                   kv_spec, kv_spec, ik_spec,
                   pl.BlockSpec((A_KV_HEADS, tm, HEAD_DIM), hmaj), pl.BlockSpec((A_KV_HEADS, tm, LANES), hmaj),
                   pl.BlockSpec((tm, HEAD_DIM), row), pl.BlockSpec((8, tm), tmin)],
        out_shape=[jax.ShapeDtypeStruct((n, A_WIDTH), F32), jax.ShapeDtypeStruct((n, 4 * B_WIDTH), F32),
                   jax.ShapeDtypeStruct((n, IDX_WIDTH), F32),
                   kv_shape, kv_shape, ik_shape,
                   jax.ShapeDtypeStruct((A_KV_HEADS, n, HEAD_DIM), BF16),
                   jax.ShapeDtypeStruct((A_KV_HEADS, n, LANES), BF16),
                   jax.ShapeDtypeStruct((n, HEAD_DIM), BF16), jax.ShapeDtypeStruct((8, n), F32)],
        scratch_shapes=[pltpu.VMEM((D_MODEL, IN_WIDTH_PAD), BF16)],
        compiler_params=_cparams("arbitrary"),
        name="in_proj",
    )(x, g, w_all, *tabs)


def _rope_tables(pos):
    half = ROT_DIM // 2
    inv_freq = 1.0 / (ROPE_THETA ** (jnp.arange(half, dtype=F32) * (2.0 / ROT_DIM)))
    ang = pos.astype(F32)[:, None] * inv_freq[None, :]
    cos, sin = jnp.cos(ang), jnp.sin(ang)
    t = pos.shape[0]
    pad = HEAD_DIM - ROT_DIM
    c = jnp.concatenate([cos, cos, jnp.ones((t, pad), F32)], axis=1)
    s1 = jnp.concatenate([-sin, jnp.zeros((t, half + pad), F32)], axis=1)
    s2 = jnp.concatenate([jnp.zeros((t, half), F32), sin, jnp.zeros((t, pad), F32)], axis=1)
    rep = LANES // HEAD_DIM
    return tuple(jnp.tile(a, (1, rep)) for a in (c, s1, s2))


def _dsa_kernel(q_ref, iq_ref, iw_ref, k_ref, v_ref, ik_ref, o_ref, key_ref, hi_ref, lo_ref, acc_ref, m_ref,
                s0_ref, s1_ref, alpha_ref, p_ref, bias_ref, mt0_ref, mt1_ref, *, l_real, pos0, kb_rows, topk, tq):
    i = pl.program_id(1)
    qpos0 = pos0 + i * tq
    kend = jnp.minimum(((qpos0 + tq - 1) // CHUNK + 1) * CHUNK, l_real)
    cb_rows = 2 * kb_rows
    nkc = (kend + cb_rows - 1) // cb_rows
    nkb = 2 * nkc
    qchunk = (qpos0 + lax.broadcasted_iota(I32, (1, tq), 1)) // CHUNK
    row_iota = lax.broadcasted_iota(I32, (kb_rows, tq), 0)

    iq = iq_ref[...].astype(BF16)
    iq4 = jnp.concatenate([iq[:, h * HEAD_DIM:(h + 1) * HEAD_DIM] for h in range(IDX_HEADS)], axis=0)
    iw = iw_ref[...] * (IDX_HEADS ** -0.5 * HEAD_DIM ** -0.5)

    def score_body(kb, carry):
        r0 = pl.multiple_of(kb * kb_rows, kb_rows)
        ikb = ik_ref[pl.ds(r0, kb_rows), :]
        lg4 = lax.dot_general(ikb, iq4, _NT, preferred_element_type=F32)
        acc = jnp.zeros((kb_rows, tq), F32)
        for h in range(IDX_HEADS):
            acc = acc + iw[h:h + 1, :] * jnp.maximum(lg4[:, h * tq:(h + 1) * tq], 0.0)
        bits = pltpu.bitcast(acc + 0.0, I32)
        key = bits ^ ((bits >> 31) & 0x7FFFFFFF)
        kpos = r0 + row_iota
        adm = (kpos // CHUNK <= qchunk) & (kpos < l_real)
        key = jnp.where(adm, key, INT_MIN)
        key_ref[pl.ds(r0, kb_rows), :] = key
        hi_ref[pl.ds(r0, kb_rows), :] = (key >> 16).astype(I16)
        return carry

    lax.fori_loop(0, nkb, score_body, 0)

    def count(pred):
        def body(kc, c):
            r0 = pl.multiple_of(kc * cb_rows, cb_rows)
            hit = pred(key_ref[pl.ds(r0, cb_rows), :], r0)
            return c + jnp.where(hit, 1, 0).reshape(cb_rows // 8, 8, tq).sum(axis=0)
        c = lax.fori_loop(0, nkc, body, jnp.zeros((8, tq), I32))
        return c.sum(axis=0, keepdims=True)

    def count16(ref, pred):
        one, zero = jnp.ones((), BF16), jnp.zeros((), BF16)

        def body(kc, c):
            r0 = pl.multiple_of(kc * cb_rows, cb_rows)
            w = jnp.where(pred(ref[pl.ds(r0, cb_rows), :]), one, zero).reshape(cb_rows // 16, 16, tq)
            parts = [w[r] for r in range(cb_rows // 16)]
            while len(parts) > 1:
                parts = [a + b for a, b in zip(parts[::2], parts[1::2])] + parts[len(parts) & ~1:]
            return c + parts[0]
        c = lax.fori_loop(0, nkc, body, jnp.zeros((16, tq), BF16))
        return c.astype(F32).sum(axis=0, keepdims=True).astype(I32)

    def bisect16(ref, kth):
        c0 = count16(ref, lambda blk: blk >= jnp.zeros((), I16))
        start = jnp.where(c0 >= kth, 0, -32768).astype(I32)

        def bit_body(it, prefix):
            cand = prefix | lax.shift_left(jnp.int32(1), 14 - it)
            c = count16(ref, lambda blk: blk >= cand.astype(I16))
            return jnp.where(c >= kth, cand, prefix)

        return lax.fori_loop(0, 15, bit_body, start)

    tau_hi = bisect16(hi_ref, topk)
    tau_hi16 = tau_hi.astype(I16)
    kth_lo = topk - count16(hi_ref, lambda blk: blk > tau_hi16)

    def lo_body(kc, carry):
        r0 = pl.multiple_of(kc * cb_rows, cb_rows)
        lo = ((key_ref[pl.ds(r0, cb_rows), :] & 0xFFFF) - 32768).astype(I16)
        lo_ref[pl.ds(r0, cb_rows), :] = jnp.where(hi_ref[pl.ds(r0, cb_rows), :] == tau_hi16, lo,
                                                 jnp.full((), -32768, I16))
        return carry

    lax.fori_loop(0, nkc, lo_body, 0)
    tau_lo = bisect16(lo_ref, kth_lo)
    tau = lax.shift_left(tau_hi, 16) | (tau_lo + 32768)
    n_ge = count(lambda blk, r0: blk >= tau)

    @pl.when(jnp.max(jnp.where((n_ge > topk) & (tau != INT_MIN), 1, 0)) > 0)
    def _():
        need = (topk - count(lambda blk, r0: blk > tau)).astype(F32)
        ri = lax.broadcasted_iota(I32, (kb_rows, kb_rows), 0)
        ci = lax.broadcasted_iota(I32, (kb_rows, kb_rows), 1)
        upto = jnp.where(ci <= ri, 1.0, 0.0).astype(BF16)

        def strike(kc, seen):
            for half in range(2):
                rows = pl.ds(pl.multiple_of(kc * cb_rows + half * kb_rows, kb_rows), kb_rows)
                key = key_ref[rows, :]
                tie = key == tau
                rank = _dot(upto, jnp.where(tie, 1.0, 0.0).astype(BF16)) + seen
                key_ref[rows, :] = jnp.where(tie & (rank > need), INT_MIN, key)
                seen = rank[kb_rows - 1:kb_rows, :]
            return seen

        lax.fori_loop(0, nkc, strike, jnp.zeros((1, tq), F32))

    lane_rep = kb_rows // LANES
    tau_c = jnp.tile(jnp.broadcast_to(tau, (LANES, tq)).T, (1, lane_rep))
    acc_ref[...] = jnp.zeros(acc_ref.shape, F32)
    m_ref[...] = jnp.full(m_ref.shape, NEG_BIG, F32)
    qs = (q_ref[...] * (HEAD_DIM ** -0.5 * LOG2E)).astype(BF16)
    q4 = [jnp.concatenate([qs[:, (j * A_GROUP + g) * HEAD_DIM:(j * A_GROUP + g + 1) * HEAD_DIM]
                           for g in range(A_GROUP)], axis=0) for j in range(A_KV_HEADS)]

    def qk_scores(kb, s_slot, mt_slot):
        r0 = pl.multiple_of(jnp.minimum(kb, nkb - 1) * kb_rows, kb_rows)
        key_t = key_ref[pl.ds(r0, kb_rows), :].T
        bias_ref[...] = jnp.where((key_t >= tau_c) & (key_t != INT_MIN), 0.0, NEG_BIG)
        for j in range(A_KV_HEADS):
            s = lax.dot_general(q4[j], k_ref[j, pl.ds(r0, kb_rows), :], _NT, preferred_element_type=F32)
            for g in range(A_GROUP):
                rows = slice(g * tq, (g + 1) * tq)
                sm = s[rows, :] + bias_ref[...]
                s_slot[j, rows, :] = sm
                mt_slot[j, g] = jnp.broadcast_to(jnp.max(sm, axis=1, keepdims=True), (tq, LANES))

    def softmax_pv(kb, s_slot, mt_slot):
        r0 = pl.multiple_of(kb * kb_rows, kb_rows)
        for j in range(A_KV_HEADS):
            for g in range(A_GROUP):
                rows = slice(g * tq, (g + 1) * tq)
                m_old = m_ref[j, g]
                m_new = jnp.maximum(m_old, mt_slot[j, g])
                alpha_ref[j, g] = jnp.exp2(m_old - m_new)
                m_ref[j, g] = m_new
                p_ref[j, rows, :] = jnp.exp2(s_slot[j, rows, :] - jnp.tile(m_new, (1, lane_rep))).astype(BF16)
            pv = _dot(p_ref[j], v_ref[j, pl.ds(r0, kb_rows), :])
            acc_ref[j] = alpha_ref[j] * acc_ref[j] + pv.reshape(A_GROUP, tq, LANES)

    qk_scores(0, s0_ref, mt0_ref)

    def pair_body(i, carry):
        kb = 2 * i
        qk_scores(kb + 1, s1_ref, mt1_ref)
        softmax_pv(kb, s0_ref, mt0_ref)
        qk_scores(kb + 2, s0_ref, mt0_ref)
        softmax_pv(kb + 1, s1_ref, mt1_ref)
        return carry

    lax.fori_loop(0, nkc, pair_body, 0)

    outs = []
    for j in range(A_KV_HEADS):
        acc = acc_ref[j]
        for g in range(A_GROUP):
            outs.append(acc[g, :, :HEAD_DIM] * (1.0 / acc[g, :, HEAD_DIM:HEAD_DIM + 1]))
    o_ref[...] = jnp.concatenate(outs, axis=1)


def _dsa(q, iq, iw_t, k_hm, v_aug, ik, *, bsz, l_real, pos0, kb_rows, topk, tq):
    t_len = q.shape[0] // bsz
    l_pad = ik.shape[0] // bsz
    nq = t_len // tq
    assert l_pad // 16 <= 256, "packed hit counts are summed in bf16, exact only up to 256 per slot"
    kern = functools.partial(_dsa_kernel, l_real=l_real, pos0=pos0, kb_rows=kb_rows, topk=topk, tq=tq)
    qrow = lambda b, i: (b * nq + i, 0)
    return pl.pallas_call(
        kern,
        grid=(bsz, nq),
        in_specs=[pl.BlockSpec((tq, A_WIDTH), qrow), pl.BlockSpec((tq, IDX_WIDTH), qrow),
                  pl.BlockSpec((8, tq), lambda b, i: (0, b * nq + i)),
                  pl.BlockSpec((A_KV_HEADS, l_pad, HEAD_DIM), lambda b, i: (0, b, 0)),
                  pl.BlockSpec((A_KV_HEADS, l_pad, LANES), lambda b, i: (0, b, 0)),
                  pl.BlockSpec((l_pad, HEAD_DIM), lambda b, i: (b, 0))],
        out_specs=pl.BlockSpec((tq, A_WIDTH), qrow),
        out_shape=jax.ShapeDtypeStruct((bsz * t_len, A_WIDTH), F32),
        scratch_shapes=[pltpu.VMEM((l_pad, tq), I32), pltpu.VMEM((l_pad, tq), I16), pltpu.VMEM((l_pad, tq), I16),
                        pltpu.VMEM((A_KV_HEADS, A_GROUP, tq, LANES), F32),
                        pltpu.VMEM((A_KV_HEADS, A_GROUP, tq, LANES), F32),
                        pltpu.VMEM((A_KV_HEADS, A_GROUP * tq, kb_rows), F32),
                        pltpu.VMEM((A_KV_HEADS, A_GROUP * tq, kb_rows), F32),
                        pltpu.VMEM((A_KV_HEADS, A_GROUP, tq, LANES), F32),
                        pltpu.VMEM((A_KV_HEADS, A_GROUP * tq, kb_rows), BF16),
                        pltpu.VMEM((tq, kb_rows), F32),
                        pltpu.VMEM((A_KV_HEADS, A_GROUP, tq, LANES), F32),
                        pltpu.VMEM((A_KV_HEADS, A_GROUP, tq, LANES), F32)],
        compiler_params=_cparams("parallel", "arbitrary"),
        name="dsa",
    )(q, iq, iw_t, k_hm, v_aug, ik)


def _hgrn_kernel(bq_ref, bf_ref, bi_ref, bg_ref, lb_ref, gn_ref, s0_ref, o_ref, sfin_ref,
                 st_ref, q_s, kk_s, b_s, qe_s, kd_s, dec_s, o_s, e_s, *, layer, tb, nseq):
    tstep = pl.program_id(1)
    nchunk = tb // B_CHUNK

    @pl.when(tstep == 0)
    def _():
        st_ref[...] = s0_ref[...]

    lbr = lb_ref[...]
    lbe = jnp.exp(lbr - jnp.max(lbr, axis=0, keepdims=True))
    lbs = lbe / jnp.sum(lbe, axis=0, keepdims=True)
    cum = lbs[0:1, :]
    first = cum
    for r in range(1, layer + 1):
        cum = cum + lbs[r:r + 1, :]
    lb = cum - first

    ri = lax.broadcasted_iota(I32, (tb, tb), 0)
    ci = lax.broadcasted_iota(I32, (tb, tb), 1)
    same = (ri // B_CHUNK) == (ci // B_CHUNK)
    tri = jnp.where(same & (ci <= ri), 1.0, 0.0).astype(BF16)
    ones = jnp.where(same, 1.0, 0.0).astype(BF16)
    for u in range(nseq):
        f = lb + (1.0 - lb) * _sigmoid(bf_ref[u])
        bq = bq_ref[u]
        q = bq * _sigmoid(bq) * (HEAD_DIM ** -0.5)
        g3 = _split3(jnp.log(f) * LOG2E)
        b = _dot(tri, g3[0]) + _dot(tri, g3[1]) + _dot(tri, g3[2])
        blast = _dot(ones, g3[0]) + _dot(ones, g3[1]) + _dot(ones, g3[2])
        q_s[u] = q
        kk_s[u] = 1.0 - f
        b_s[u] = b
        qe_s[u] = q * jnp.exp2(b)
        kd_s[u] = (1.0 - f) * jnp.exp2(blast - b)
        dec_s[u] = jnp.exp2(blast)

    hi = lax.broadcasted_iota(I32, (HALF, HALF), 0) // HEAD_DIM
    hj = lax.broadcasted_iota(I32, (HALF, HALF), 1) // HEAD_DIM
    bd_mask = hi == hj
    bd = jnp.where(bd_mask, 1.0, 0.0).astype(BF16)
    t_iota = lax.broadcasted_iota(I32, (B_CHUNK, B_WIDTH), 0)

    def seg_sum(x, two_pass):
        out = []
        for half in range(2):
            xs = x[:, half * HALF:(half + 1) * HALF]
            if two_pass:
                xh, xl = _split2(xs)
                out.append(_dot(xh, bd) + _dot(xl, bd))
            else:
                out.append(_dot(xs.astype(BF16), bd))
        return jnp.concatenate(out, axis=1)

    def chunk_body(c, carry):
        r0 = pl.multiple_of(c * B_CHUNK, B_CHUNK)
        rows = pl.ds(r0, B_CHUNK)
        for u in range(nseq):
            qc, kc, bc, vc = q_s[u, rows, :], kk_s[u, rows, :], b_s[u, rows, :], bi_ref[u, rows, :]
            qe, kd = qe_s[u, rows, :].astype(BF16), kd_s[u, rows, :].astype(BF16)
            dec = dec_s[u, pl.ds(r0, 1), :]
            for s in range(B_CHUNK):
                ex = jnp.exp2(jnp.where(t_iota >= s, bc - bc[s:s + 1, :], NEG_BIG))
                e_s[u, s * B_CHUNK:(s + 1) * B_CHUNK, :] = qc * kc[s:s + 1, :] * ex
            aexp = seg_sum(e_s[u], False)
            o = jnp.zeros((B_CHUNK, B_WIDTH), F32)
            for s in range(B_CHUNK):
                o = o + aexp[s * B_CHUNK:(s + 1) * B_CHUNK, :] * vc[s:s + 1, :]
            inter = []
            vb = vc.astype(BF16)
            for half in range(2):
                cols = slice(half * HALF, (half + 1) * HALF)
                st = st_ref[u, half]
                inter.append(lax.dot_general(qe[:, cols], st.astype(BF16), _NT, preferred_element_type=F32))
                upd = lax.dot_general(vb[:, cols], kd[:, cols], _TN, preferred_element_type=F32)
                st_ref[u, half] = st * dec[:, cols] + jnp.where(bd_mask, upd, 0.0)
            o_s[u, rows, :] = o + jnp.concatenate(inter, axis=1)
        return carry

    lax.fori_loop(0, nchunk, chunk_body, 0)

    for u in range(nseq):
        o = o_s[u]
        ms = seg_sum(o * o, True) * (1.0 / HEAD_DIM)
        bg = bg_ref[u]
        o_ref[u] = o * lax.rsqrt(ms + EPS) * gn_ref[...] * (bg * _sigmoid(bg))

    @pl.when(tstep == pl.num_programs(1) - 1)
    def _():
        sfin_ref[...] = st_ref[...]


def _hgrn(hb, lb_raw, gn_tiled, s0_bd, *, layer, bsz, t_len, tb):
    nt = t_len // tb
    nseq = 2 if bsz % 2 == 0 else 1
    hb3 = hb.reshape(bsz, t_len, 4 * B_WIDTH)
    col = lambda j: (lambda b, t: (b, t, j))
    scr = lambda r: pltpu.VMEM((nseq, r, B_WIDTH), F32)
    kern = functools.partial(_hgrn_kernel, layer=layer, tb=tb, nseq=nseq)
    blk = (nseq, tb, B_WIDTH)
    out, s_fin = pl.pallas_call(
        kern,
        grid=(bsz // nseq, nt),
        in_specs=[pl.BlockSpec(blk, col(0)), pl.BlockSpec(blk, col(1)), pl.BlockSpec(blk, col(2)),
                  pl.BlockSpec(blk, col(3)),
                  pl.BlockSpec(lb_raw.shape, lambda b, t: (0, 0)),
                  pl.BlockSpec((1, B_WIDTH), lambda b, t: (0, 0)),
                  pl.BlockSpec((nseq, 2, HALF, HALF), lambda b, t: (b, 0, 0, 0))],
        out_specs=[pl.BlockSpec(blk, lambda b, t: (b, t, 0)),
                   pl.BlockSpec((nseq, 2, HALF, HALF), lambda b, t: (b, 0, 0, 0))],
        out_shape=[jax.ShapeDtypeStruct((bsz, t_len, B_WIDTH), F32),
                   jax.ShapeDtypeStruct((bsz, 2, HALF, HALF), F32)],
        scratch_shapes=[pltpu.VMEM((nseq, 2, HALF, HALF), F32), scr(tb), scr(tb), scr(tb), scr(tb), scr(tb),
                        scr(tb), scr(tb), scr(B_CHUNK * B_CHUNK)],
        compiler_params=_cparams("parallel", "arbitrary"),
        name="hgrn2",
    )(hb3, hb3, hb3, hb3, lb_raw, gn_tiled, s0_bd)
    return out.reshape(bsz * t_len, B_WIDTH), s_fin


def _state_to_tiles(s):
    bsz = s.shape[0]
    st = jnp.swapaxes(s, -1, -2).reshape(bsz, 2, B_HEADS // 2, HEAD_DIM, HEAD_DIM)
    eye = jnp.eye(B_HEADS // 2, dtype=s.dtype)
    t = st[:, :, :, :, None, :] * eye[None, None, :, None, :, None]
    return t.reshape(bsz, 2, HALF, HALF)


def _tiles_to_state(t):
    bsz = t.shape[0]
    t6 = t.reshape(bsz, 2, B_HEADS // 2, HEAD_DIM, B_HEADS // 2, HEAD_DIM)
    diag = jnp.stack([t6[:, :, h, :, h, :] for h in range(B_HEADS // 2)], axis=2)
    return jnp.swapaxes(diag.reshape(bsz, B_HEADS, HEAD_DIM, HEAD_DIM), -1, -2)


def _normmm_kernel(x_ref, g_ref, w_ref, o_ref):
    o_ref[...] = _dot(_rms(x_ref[...], g_ref[...]).astype(BF16), w_ref[...])


def _norm_matmul(x, g, w, tm):
    n, m = x.shape[0], w.shape[1]
    return pl.pallas_call(
        _normmm_kernel,
        grid=(n // tm,),
        in_specs=[pl.BlockSpec((tm, D_MODEL), lambda i: (i, 0)), pl.BlockSpec((1, D_MODEL), lambda i: (0, 0)),
                  pl.BlockSpec((D_MODEL, m), lambda i: (0, 0))],
        out_specs=pl.BlockSpec((tm, m), lambda i: (i, 0)),
        out_shape=jax.ShapeDtypeStruct((n, m), F32),
        compiler_params=_cparams("parallel"),
        name="norm_matmul",
    )(x, g, w)


def _swiglu_mid(xn, w1, w3):
    a = _dot(xn, w1)
    return (a * _sigmoid(a) * _dot(xn, w3)).astype(BF16)


def _postmix_kernel(oa_ref, ob_ref, x_ref, wa_ref, wb_ref, gm_ref, wq_ref, mk_ref, mv_ref, wo_ref, *rest,
                    with_ffn, final, chunks):
    o_ref = rest[-1]
    x = (x_ref[...] + _dot(oa_ref[...].astype(BF16), wa_ref[...]) + _dot(ob_ref[...].astype(BF16), wb_ref[...]))
    q = _dot(_rms(x, gm_ref[...]).astype(BF16), wq_ref[...]).astype(BF16)
    nbt = mk_ref.shape[0]
    rpb = x.shape[0] // nbt
    att = []
    for bb in range(nbt):
        rows = slice(bb * rpb, (bb + 1) * rpb)
        heads = []
        for h in range(MEM_HEADS):
            cols = slice(h * MEM_HEAD_DIM, (h + 1) * MEM_HEAD_DIM)
            s = lax.dot_general(q[rows, cols], mk_ref[bb, :, cols], _NT,
                                preferred_element_type=F32) * (MEM_HEAD_DIM ** -0.5)
            p = jnp.exp(s - jnp.max(s, axis=-1, keepdims=True))
            p = p * (1.0 / jnp.sum(p, axis=-1, keepdims=True))
            heads.append(_dot(p.astype(BF16), mv_ref[bb, :, cols]).astype(BF16))
        att.append(jnp.concatenate(heads, axis=1))
    y = x + _dot(att[0] if nbt == 1 else jnp.concatenate(att, axis=0), wo_ref[...])
    if with_ffn:
        gffn_ref, w1_ref, w3_ref, w2_ref, gf_ref = rest[:5]
        yn = _rms(y, gffn_ref[...]).astype(BF16)
        lo = 0
        for width in chunks:
            cols = slice(lo, lo + width)
            y = y + _dot(_swiglu_mid(yn, w1_ref[:, cols], w3_ref[:, cols]), w2_ref[cols, :])
            lo += width
        if final:
            y = _rms(y, gf_ref[...])
    o_ref[...] = y


def _postmix(oa, ob, x, wa, wb, gm, wq, mk, mv, wo, ffn=None, *, rows_per_batch, tm, final=False):
    n = x.shape[0]
    n_mem = mk.shape[1]
    row = lambda i: (i, 0)
    whole = lambda i: (0, 0)
    if tm <= rows_per_batch:
        per = rows_per_batch // tm
        mem_spec = pl.BlockSpec((1, n_mem, D_MODEL), lambda i: (i // per, 0, 0))
    else:
        mem_spec = pl.BlockSpec((tm // rows_per_batch, n_mem, D_MODEL), lambda i: (i, 0, 0))
    const = lambda shape: pl.BlockSpec(shape, whole, pipeline_mode=pl.Buffered(1))
    in_specs = [pl.BlockSpec((tm, A_WIDTH), row), pl.BlockSpec((tm, B_WIDTH), row), pl.BlockSpec((tm, D_MODEL), row),
                const((A_WIDTH, D_MODEL)), const((B_WIDTH, D_MODEL)), const((1, D_MODEL)),
                const((D_MODEL, D_MODEL)),
                mem_spec, mem_spec, const((D_MODEL, D_MODEL))]
    args = [oa, ob, x, wa, wb, gm, wq, mk, mv, wo]
    chunks = ()
    if ffn is not None:
        d_ff = ffn[1].shape[1]
        step = 4 * HALF
        chunks = tuple(min(step, d_ff - lo) for lo in range(0, d_ff, step))
        in_specs += [const((1, D_MODEL)), const((D_MODEL, d_ff)), const((D_MODEL, d_ff)), const((d_ff, D_MODEL)),
                     const((1, D_MODEL))]
        args += list(ffn)
    return pl.pallas_call(
        functools.partial(_postmix_kernel, with_ffn=ffn is not None, final=final, chunks=chunks),
        grid=(n // tm,),
        in_specs=in_specs,
        out_specs=pl.BlockSpec((tm, D_MODEL), row),
        out_shape=jax.ShapeDtypeStruct((n, D_MODEL), F32),
        compiler_params=_cparams("parallel"),
        name="post_mix",
    )(*args)


def _router_kernel(x_ref, g_ref, rt_ref, comb_ref, sel_ref):
    xh, xl = _split2(_rms(x_ref[...], g_ref[...]))
    rh, rl = _split2(rt_ref[...])
    nt_dot = lambda a, b: lax.dot_general(a, b, _NT, preferred_element_type=F32)
    logits = nt_dot(rh, xh) + (nt_dot(rh, xl) + nt_dot(rl, xh))
    e_idx = lax.broadcasted_iota(I32, logits.shape, 0)
    m1 = jnp.max(logits, axis=0, keepdims=True)
    i1 = jnp.min(jnp.where(logits == m1, e_idx, N_EXPERTS), axis=0, keepdims=True)
    rest = jnp.where(e_idx == i1, -jnp.inf, logits)
    m2 = jnp.max(rest, axis=0, keepdims=True)
    i2 = jnp.min(jnp.where(rest == m2, e_idx, N_EXPERTS), axis=0, keepdims=True)
    e2 = jnp.exp(m2 - m1)
    g1 = 1.0 / (1.0 + e2)
    comb_ref[...] = jnp.where(e_idx == i1, g1, 0.0) + jnp.where(e_idx == i2, e2 * g1, 0.0)
    sel_ref[...] = jnp.where((e_idx == i1) | (e_idx == i2), 1.0, 0.0)


def _router(x, g, rt, tm):
    n = x.shape[0]
    return pl.pallas_call(
        _router_kernel,
        grid=(n // tm,),
        in_specs=[pl.BlockSpec((tm, D_MODEL), lambda i: (i, 0)), pl.BlockSpec((1, D_MODEL), lambda i: (0, 0)),
                  pl.BlockSpec((N_EXPERTS, D_MODEL), lambda i: (0, 0))],
        out_specs=[pl.BlockSpec((N_EXPERTS, tm), lambda i: (0, i)), pl.BlockSpec((N_EXPERTS, tm), lambda i: (0, i))],
        out_shape=[jax.ShapeDtypeStruct((N_EXPERTS, n), F32), jax.ShapeDtypeStruct((N_EXPERTS, n), F32)],
        compiler_params=_cparams("parallel"),
        name="moe_router",
    )(x, g, rt)


def _moe_kernel(x_ref, g_ref, selt_ref, sel_ref, comb_ref, w1_ref, w3_ref, w2_ref, gf_ref, o_ref,
                xn_ref, posr_ref, posc_ref, cnt_ref, xg_ref, yg_ref, *, final, nb, sub):
    e = pl.program_id(1)
    j = pl.program_id(2)
    last_j = pl.num_programs(2) - 1

    @pl.when((e == 0) & (j == 0))
    def _():
        x = x_ref[...]
        xn_ref[...] = _rms(x, g_ref[...]).astype(BF16)
        o_ref[...] = x
        cc = min(nb, 256)
        ri = lax.broadcasted_iota(I32, (cc, cc), 0)
        ci = lax.broadcasted_iota(I32, (cc, cc), 1)
        before_r = jnp.where(ri < ci, 1.0, 0.0).astype(BF16)
        before_c = jnp.where(ci < ri, 1.0, 0.0).astype(BF16)
        run_r = jnp.zeros((N_EXPERTS, 1), F32)
        run_c = jnp.zeros((1, N_EXPERTS), F32)
        for c in range(nb // cc):
            cols = slice(c * cc, (c + 1) * cc)
            mr = selt_ref[:, cols]
            posr_ref[:, cols] = jnp.where(mr > 0.0, _dot(mr.astype(BF16), before_r) + run_r, -1.0)
            run_r = run_r + jnp.sum(mr, axis=1, keepdims=True)
            mc = sel_ref[cols, :]
            posc_ref[cols, :] = jnp.where(mc > 0.0, _dot(before_c, mc.astype(BF16)) + run_c, -1.0)
            run_c = run_c + jnp.sum(mc, axis=0, keepdims=True)
        for k in range(N_EXPERTS):
            cnt_ref[k] = run_c[0, k].astype(I32)

    cnt = cnt_ref[e]
    nfull = cnt // sub
    rem = cnt - nfull * sub
    tail_sizes = [m for m in (64, 128) if m < sub] + [sub]

    def for_tiles(fn):
        def body(s, carry):
            fn(pl.multiple_of(s * sub, sub), sub)
            return carry
        lax.fori_loop(0, nfull, body, 0)
        r0 = pl.multiple_of(nfull * sub, sub)
        lo = 0
        for m in tail_sizes:
            @pl.when((rem > lo) & (rem <= m))
            def _():
                fn(r0, m)
            lo = m

    def gather_matrix(r0, m):
        slot_r = posr_ref[pl.ds(e, 1), :]
        want = (r0 + lax.broadcasted_iota(I32, (m, nb), 0)).astype(F32)
        return jnp.where(slot_r == want, 1.0, 0.0).astype(BF16)

    def gather(r0, m):
        xg_ref[pl.ds(r0, m), :] = _dot(gather_matrix(r0, m), xn_ref[...]).astype(BF16)

    def ffn(r0, m):
        rows = pl.ds(r0, m)
        part = _dot(_swiglu_mid(xg_ref[rows, :], w1_ref[0], w3_ref[0]), w2_ref[0])

        @pl.when(j == 0)
        def _():
            yg_ref[rows, :] = part

        @pl.when(j > 0)
        def _():
            yg_ref[rows, :] += part

    @pl.when(j == 0)
    def _():
        for_tiles(gather)

    for_tiles(ffn)

    @pl.when(j == last_j)
    def _():
        lane8 = lax.broadcasted_iota(I32, (nb, N_EXPERTS), 1)
        slot_c = jnp.sum(jnp.where(lane8 == e, posc_ref[...], 0.0), axis=1, keepdims=True)
        gate_c = jnp.sum(jnp.where(lane8 == e, comb_ref[...], 0.0), axis=1, keepdims=True)
        gate3 = jnp.concatenate(_split3(gate_c), axis=1)

        def scatter(r0, m):
            gate = jnp.sum(_dot(gather_matrix(r0, m), gate3), axis=1, keepdims=True)
            y = (yg_ref[pl.ds(r0, m), :] * gate).astype(BF16)
            want = (r0 + lax.broadcasted_iota(I32, (nb, m), 1)).astype(F32)
            o_ref[...] += _dot(jnp.where(slot_c == want, 1.0, 0.0).astype(BF16), y)

        for_tiles(scatter)

        if final:
            @pl.when(e == pl.num_programs(1) - 1)
            def _():
                o_ref[...] = _rms(o_ref[...], gf_ref[...])


def _moe(x, g, sel_t, sel, comb, w1, w3, w2, gf, *, final, nb, tf):
    n = x.shape[0]
    d_ff = w1.shape[2]
    sub = min(nb, MOE_TILE_ROWS)
    row = lambda i, e, j: (i, 0)
    whole = lambda i, e, j: (0, 0)
    return pl.pallas_call(
        functools.partial(_moe_kernel, final=final, nb=nb, sub=sub),
        grid=(n // nb, N_EXPERTS, d_ff // tf),
        in_specs=[pl.BlockSpec((nb, D_MODEL), row), pl.BlockSpec((1, D_MODEL), whole),
                  pl.BlockSpec((N_EXPERTS, nb), lambda i, e, j: (0, i)),
                  pl.BlockSpec((nb, N_EXPERTS), row), pl.BlockSpec((nb, N_EXPERTS), row),
                  pl.BlockSpec((1, D_MODEL, tf), lambda i, e, j: (e, 0, j)),
                  pl.BlockSpec((1, D_MODEL, tf), lambda i, e, j: (e, 0, j)),
                  pl.BlockSpec((1, tf, D_MODEL), lambda i, e, j: (e, j, 0)), pl.BlockSpec((1, D_MODEL), whole)],
        out_specs=pl.BlockSpec((nb, D_MODEL), row),
        out_shape=jax.ShapeDtypeStruct((n, D_MODEL), F32),
        scratch_shapes=[pltpu.VMEM((nb, D_MODEL), BF16), pltpu.VMEM((N_EXPERTS, nb), F32),
                        pltpu.VMEM((nb, N_EXPERTS), F32), pltpu.SMEM((N_EXPERTS,), I32),
                        pltpu.VMEM((nb + sub, D_MODEL), BF16), pltpu.VMEM((nb + sub, D_MODEL), F32)],
        compiler_params=_cparams("parallel", "arbitrary", "arbitrary"),
        name="moe",
    )(x, g, sel_t, sel, comb, w1, w3, w2, gf)


def _row_tile(n, cap):
    t = min(n, cap)
    assert n % t == 0, (n, cap)
    return t


def _ff_tile(d_ff):
    for parts in (2, 4, 11, 22):
        if d_ff % parts == 0 and (d_ff // parts) % LANES == 0:
            return d_ff // parts
    return d_ff


def _mixer(x, tabs, w, *, bsz, t_len, pos0, past, s0, layer):
    n = bsz * t_len
    qa, hb, iq, k_t, v_t, ik_t, k_hm, v_aug, ik_b, iw_t = _in_proj(x, w["norm_mix"], w["w_in"], layer, tabs,
                                                                   t_len, _row_tile(n, 512))
    l_real = t_len if past is None else past[0].shape[1] + t_len
    topk = min(TOPK_MAX, l_real // 4)
    tq = 2 * LANES if t_len % (2 * LANES) == 0 else LANES
    kb_rows = 256 if l_real % 256 == 0 else LANES
    l_pad = -(-l_real // (2 * kb_rows)) * (2 * kb_rows)
    t_pad = -(-t_len // tq) * tq
    if past is not None:
        pk, pv, pik = past
        one = jnp.ones(pv.shape[:2] + (A_KV_HEADS, 1), BF16)
        zero = jnp.zeros(pv.shape[:2] + (A_KV_HEADS, LANES - HEAD_DIM - 1), BF16)
        pv_aug = jnp.concatenate([pv.astype(BF16), one, zero], axis=-1)
        hm = lambda a: jnp.transpose(a, (2, 0, 1, 3))
        k_hm = jnp.concatenate([hm(pk.astype(BF16)), k_hm.reshape(A_KV_HEADS, bsz, t_len, HEAD_DIM)], axis=2)
        v_aug = jnp.concatenate([hm(pv_aug), v_aug.reshape(A_KV_HEADS, bsz, t_len, LANES)], axis=2)
        ik_b = jnp.concatenate([pik.astype(BF16), ik_b.reshape(bsz, t_len, HEAD_DIM)], axis=1)
    if l_pad != l_real or past is not None:
        padl = l_pad - l_real
        k_hm = jnp.pad(k_hm.reshape(A_KV_HEADS, bsz, l_real, HEAD_DIM), ((0, 0), (0, 0), (0, padl), (0, 0)))
        v_aug = jnp.pad(v_aug.reshape(A_KV_HEADS, bsz, l_real, LANES), ((0, 0), (0, 0), (0, padl), (0, 0)))
        ik_b = jnp.pad(ik_b.reshape(bsz, l_real, HEAD_DIM), ((0, 0), (0, padl), (0, 0)))
        k_hm = k_hm.reshape(A_KV_HEADS, bsz * l_pad, HEAD_DIM)
        v_aug = v_aug.reshape(A_KV_HEADS, bsz * l_pad, LANES)
        ik_b = ik_b.reshape(bsz * l_pad, HEAD_DIM)
    q_in, iq_in, iw_in = qa, iq, iw_t
    if t_pad != t_len:
        padq = lambda a: jnp.pad(a.reshape(bsz, t_len, -1),
                                 ((0, 0), (0, t_pad - t_len), (0, 0))).reshape(bsz * t_pad, -1)
        q_in, iq_in = padq(qa), padq(iq)
        iw_in = jnp.pad(iw_t.reshape(8, bsz, t_len), ((0, 0), (0, 0), (0, t_pad - t_len))).reshape(8, bsz * t_pad)
    oa = _dsa(q_in, iq_in, iw_in, k_hm, v_aug, ik_b, bsz=bsz, l_real=l_real, pos0=pos0, kb_rows=kb_rows,
              topk=topk, tq=tq)
    if t_pad != t_len:
        oa = oa.reshape(bsz, t_pad, A_WIDTH)[:, :t_len].reshape(n, A_WIDTH)
    ob, s_tiles = _hgrn(hb, w["hgrn_lb"], w["gnorm"], _state_to_tiles(s0), layer=layer, bsz=bsz, t_len=t_len,
                        tb=_row_tile(t_len, 256))
    return oa, ob, k_t, v_t, ik_t, _tiles_to_state(s_tiles)


def _from_time_minor(a, bsz, t_len, heads):
    if a.ndim == 2:
        a = a.reshape((-1, HEAD_DIM, bsz, t_len))
        a = jnp.transpose(a, (2, 3, 0, 1))
    else:
        a = jnp.moveaxis(a, -1, 1)
        a = a.reshape(bsz, t_len, -1, HEAD_DIM)
    return a if heads is not None else a.reshape(bsz, t_len, HEAD_DIM)


def kernel(x_prompt, x_sample, cache_k, cache_v, cache_idx_k, state_hgrn, cache_mem_k, cache_mem_v, mem_prompt,
           norm_mix, w_in, hgrn_lb, hgrn_gnorm, w_out, norm_mem, norm_memkv, w_mq, w_mk, w_mv, w_mo,
           norm_ffn, ffn_w1, ffn_w3, ffn_w2, moe_router, moe_w1, moe_w3, moe_w2, norm_final):
    depth = w_in.shape[0]
    bp, tp, _ = x_prompt.shape
    bs, ts, _ = x_sample.shape
    past = cache_k.shape[2]
    n_mem = mem_prompt.shape[1]
    tabs_p = _rope_tables(jnp.arange(tp, dtype=I32))
    tabs_s = tuple(jnp.tile(a, (bs, 1)) for a in _rope_tables(past + jnp.arange(ts, dtype=I32)))
    xp = x_prompt.reshape(bp * tp, D_MODEL)
    xs = x_sample.reshape(bs * ts, D_MODEL)
    mem = mem_prompt.reshape(bp * n_mem, D_MODEL)
    s0_p = jnp.zeros((bp, B_HEADS, HEAD_DIM, HEAD_DIM), F32)
    gfin = norm_final.reshape(1, D_MODEL)
    row2 = lambda a: a.reshape(1, -1)
    outs = {k: [] for k in ("kp", "vp", "ikp", "sp", "mkp", "mvp", "ks", "vs", "iks", "ss")}
    for l in range(depth):
        w = {"norm_mix": row2(norm_mix[l]), "w_in": w_in, "hgrn_lb": hgrn_lb.astype(F32),
             "gnorm": row2(jnp.tile(hgrn_gnorm[l], B_HEADS)),
             "w_out_a": w_out[l, :A_WIDTH].astype(BF16), "w_out_b": w_out[l, A_WIDTH:].astype(BF16)}
        oa_p, ob_p, kp, vp, ikp, sp = _mixer(xp, tabs_p, w, bsz=bp, t_len=tp, pos0=0, past=None, s0=s0_p, layer=l)
        oa_s, ob_s, ks, vs, iks, ss = _mixer(xs, tabs_s, w, bsz=bs, t_len=ts, pos0=past,
                                             past=(cache_k[l], cache_v[l], cache_idx_k[l]), s0=state_hgrn[l],
                                             layer=l)
        wkv = jnp.concatenate([w_mk[l], w_mv[l]], axis=1).astype(BF16)
        mkv = _norm_matmul(mem, row2(norm_memkv[l]), wkv, _row_tile(bp * n_mem, 256))
        mk_p = mkv[:, :D_MODEL].reshape(bp, n_mem, D_MODEL)
        mv_p = mkv[:, D_MODEL:].reshape(bp, n_mem, D_MODEL)
        mk_s = cache_mem_k[l].reshape(bs, n_mem, D_MODEL).astype(BF16)
        mv_s = cache_mem_v[l].reshape(bs, n_mem, D_MODEL).astype(BF16)
        wq, wo, gm = w_mq[l].astype(BF16), w_mo[l].astype(BF16), row2(norm_mem[l])
        final = l == depth - 1
        gf = row2(norm_ffn[l])
        tail = (w["w_out_a"], w["w_out_b"], gm, wq)
        if l % 2 == 0:
            dense = (gf, ffn_w1[l // 2].astype(BF16), ffn_w3[l // 2].astype(BF16), ffn_w2[l // 2].astype(BF16), gfin)
            xp = _postmix(oa_p, ob_p, xp, *tail, mk_p.astype(BF16), mv_p.astype(BF16), wo, dense,
                          rows_per_batch=tp, tm=_row_tile(tp, 512), final=final)
            xs = _postmix(oa_s, ob_s, xs, *tail, mk_s, mv_s, wo, dense,
                          rows_per_batch=ts, tm=_row_tile(bs * ts, 512), final=final)
        else:
            xp = _postmix(oa_p, ob_p, xp, *tail, mk_p.astype(BF16), mv_p.astype(BF16), wo,
                          rows_per_batch=tp, tm=_row_tile(tp, 512))
            xs = _postmix(oa_s, ob_s, xs, *tail, mk_s, mv_s, wo, rows_per_batch=ts, tm=_row_tile(bs * ts, 512))
            w1, w3, w2 = moe_w1[l // 2].astype(BF16), moe_w3[l // 2].astype(BF16), moe_w2[l // 2].astype(BF16)
            rt = moe_router[l // 2].T
            tf = _ff_tile(w1.shape[2])
            comb_p, sel_p = _router(xp, gf, rt, _row_tile(bp * tp, 512))
            comb_s, sel_s = _router(xs, gf, rt, _row_tile(bs * ts, 512))
            xp = _moe(xp, gf, sel_p, sel_p.T, comb_p.T, w1, w3, w2, gfin, final=final,
                      nb=_row_tile(bp * tp, 1024), tf=tf)
            xs = _moe(xs, gf, sel_s, sel_s.T, comb_s.T, w1, w3, w2, gfin, final=final,
                      nb=_row_tile(bs * ts, 1024), tf=tf)
        outs["kp"].append(_from_time_minor(kp, bp, tp, A_KV_HEADS))
        outs["vp"].append(_from_time_minor(vp, bp, tp, A_KV_HEADS))
        outs["ikp"].append(_from_time_minor(ikp, bp, tp, None)); outs["sp"].append(sp)
        outs["mkp"].append(mk_p.reshape(bp, n_mem, MEM_HEADS, MEM_HEAD_DIM))
        outs["mvp"].append(mv_p.reshape(bp, n_mem, MEM_HEADS, MEM_HEAD_DIM))
        outs["ks"].append(_from_time_minor(ks, bs, ts, A_KV_HEADS))
        outs["vs"].append(_from_time_minor(vs, bs, ts, A_KV_HEADS))
        outs["iks"].append(_from_time_minor(iks, bs, ts, None)); outs["ss"].append(ss)
    y_prompt = xp.reshape(bp, tp, D_MODEL)
    y_sample = xs.reshape(bs, ts, D_MODEL)
    st = lambda k: jnp.stack(outs[k])
    return (y_prompt, y_sample, st("kp"), st("vp"), st("ikp"), st("sp"), st("mkp"), st("mvp"),
            st("ks"), st("vs"), st("iks"), st("ss"))
```

```python
import functools

import numpy as np
import jax
import jax.numpy as jnp
from jax import lax
from jax.experimental import pallas as pl
from jax.experimental.pallas import tpu as pltpu

F32 = jnp.float32
BF16 = jnp.bfloat16
I32 = jnp.int32
I16 = jnp.int16

D_MODEL = 1024
CHUNK = 64
EPS = 1e-6
ROPE_THETA = 500000.0
HEAD_DIM = 64
ROT_DIM = HEAD_DIM // 4
A_HEADS = 8
A_KV_HEADS = 2
A_GROUP = A_HEADS // A_KV_HEADS
A_WIDTH = A_HEADS * HEAD_DIM
KV_WIDTH = A_KV_HEADS * HEAD_DIM
IDX_HEADS = 4
IDX_WIDTH = IDX_HEADS * HEAD_DIM
TOPK_MAX = 256
B_HEADS = 8
B_WIDTH = B_HEADS * HEAD_DIM
B_CHUNK = 16
MEM_HEADS = 4
MEM_HEAD_DIM = D_MODEL // MEM_HEADS
N_EXPERTS = 8
MOE_TILE_ROWS = 256
LANES = 128
HALF = 256
IN_SPLITS = (A_WIDTH, KV_WIDTH, KV_WIDTH, IDX_WIDTH, HEAD_DIM, IDX_HEADS, B_WIDTH, B_WIDTH, B_WIDTH, B_WIDTH)
IN_WIDTH = sum(IN_SPLITS)
IN_WIDTH_PAD = 3200
VMEM_LIMIT = 56 * 1024 * 1024
INT_MIN = -2147483648
NEG_BIG = -1e30
LOG2E = 1.4426950408889634

_NT = (((1,), (1,)), ((), ()))
_TN = (((0,), (0,)), ((), ()))


def _cparams(*sem):
    return pltpu.CompilerParams(dimension_semantics=sem, vmem_limit_bytes=VMEM_LIMIT)


def _rms(x, g):
    return x * lax.rsqrt(jnp.mean(x * x, axis=-1, keepdims=True) + EPS) * g


def _sigmoid(x):
    return 1.0 / (1.0 + jnp.exp(-x))


def _dot(a, b):
    return jnp.dot(a, b, preferred_element_type=F32)


def _split2(x):
    hi = x.astype(BF16)
    lo = (x - hi.astype(F32)).astype(BF16)
    return hi, lo


def _split3(x):
    hi = x.astype(BF16)
    r = x - hi.astype(F32)
    mid = r.astype(BF16)
    lo = (r - mid.astype(F32)).astype(BF16)
    return hi, mid, lo


def _rope128(xb, c, s1, s2):
    return xb * c + pltpu.roll(xb, LANES - ROT_DIM // 2, 1) * s1 + pltpu.roll(xb, ROT_DIM // 2, 1) * s2


def _inproj_kernel(x_ref, g_ref, wraw_ref, c_ref, s1_ref, s2_ref,
                   qa_ref, hb_ref, iq_ref, kt_ref, vt_ref, ikt_ref, khm_ref, vaug_ref, ikb_ref, iwt_ref, w_ref):
    @pl.when(pl.program_id(0) == 0)
    def _():
        src = np.cumsum([0] + list(IN_SPLITS))
        dst = 0
        for i in (0, 6, 7, 8, 9, 1, 2, 3):
            width = int(src[i + 1] - src[i])
            w_ref[:, dst:dst + width] = wraw_ref[0, :, int(src[i]):int(src[i + 1])].astype(BF16)
            dst += width
        tail = jnp.concatenate([wraw_ref[0, :, int(src[4]):int(src[6])],
                                jnp.zeros((D_MODEL, LANES - HEAD_DIM - IDX_HEADS), F32)], axis=1)
        w_ref[:, dst:dst + LANES] = tail.astype(BF16)

    xn = _rms(x_ref[...], g_ref[...]).astype(BF16)
    c, s1, s2 = c_ref[...], s1_ref[...], s2_ref[...]
    for j in range(A_WIDTH // HALF):
        h = _dot(xn, w_ref[:, j * HALF:(j + 1) * HALF])
        for i in range(HALF // LANES):
            lo = j * HALF + i * LANES
            qa_ref[:, lo:lo + LANES] = (_rope128(h[:, i * LANES:(i + 1) * LANES], c, s1, s2)
                                        * (HEAD_DIM ** -0.5 * LOG2E)).astype(BF16)
    for j in range(4):
        lo = A_WIDTH + j * B_WIDTH
        hb_ref[:, j * B_WIDTH:(j + 1) * B_WIDTH] = _dot(xn, w_ref[:, lo:lo + B_WIDTH])
    base = A_WIDTH + 4 * B_WIDTH
    kv = _dot(xn, w_ref[:, base:base + HALF])
    k = _rope128(kv[:, :LANES], c, s1, s2)
    v = kv[:, LANES:]
    iq = _dot(xn, w_ref[:, base + HALF:base + 2 * HALF])
    iq_ref[:, 0:128] = _rope128(iq[:, :LANES], c, s1, s2).astype(BF16)
    iq_ref[:, 128:256] = _rope128(iq[:, LANES:], c, s1, s2).astype(BF16)
    lane = lax.broadcasted_iota(I32, c.shape, 1)
    first = lane < HEAD_DIM
    ikw = _rope128(_dot(xn, w_ref[:, base + 512:base + 640]),
                   jnp.where(first, c, 1.0), jnp.where(first, s1, 0.0), jnp.where(first, s2, 0.0))
    kt_ref[...] = k.T.reshape(kt_ref.shape)
    vt_ref[...] = v.T.reshape(vt_ref.shape)
    ikw_t = ikw.T
    ikt_ref[...] = ikw_t[:HEAD_DIM].reshape(ikt_ref.shape)
    iwt_ref[...] = ikw_t[HEAD_DIM:HEAD_DIM + 8]
    ikb_ref[...] = ikw[:, :HEAD_DIM].astype(BF16)
    ones_col = jnp.where(lane == HEAD_DIM, 1.0, 0.0)
    for j in range(A_KV_HEADS):
        khm_ref[j] = k[:, j * HEAD_DIM:(j + 1) * HEAD_DIM].astype(BF16)
        vj = v if j == 0 else pltpu.roll(v, LANES - j * HEAD_DIM, 1)
        vaug_ref[j] = jnp.where(first, vj, ones_col).astype(BF16)


def _in_proj(x, g, w_all, layer, tabs, t_len, tm):
    n = x.shape[0]
    nt = tabs[0].shape[0] // tm
    per_batch = t_len % tm == 0
    ntb = t_len // tm if per_batch else 1
    row = lambda i: (i, 0)
    tab = lambda i: (i % nt, 0)
    whole = lambda i: (0, 0)
    tmin = lambda i: (0, i)
    hmaj = lambda i: (0, i, 0)
    if per_batch:
        kv_spec = pl.BlockSpec((1, A_KV_HEADS, HEAD_DIM, tm), lambda i: (i // ntb, 0, 0, i % ntb))
        ik_spec = pl.BlockSpec((1, HEAD_DIM, tm), lambda i: (i // ntb, 0, i % ntb))
        kv_shape = jax.ShapeDtypeStruct((n // t_len, A_KV_HEADS, HEAD_DIM, t_len), F32)
        ik_shape = jax.ShapeDtypeStruct((n // t_len, HEAD_DIM, t_len), F32)
    else:
        kv_spec, ik_spec = pl.BlockSpec((KV_WIDTH, tm), tmin), pl.BlockSpec((HEAD_DIM, tm), tmin)
        kv_shape, ik_shape = jax.ShapeDtypeStruct((KV_WIDTH, n), F32), jax.ShapeDtypeStruct((HEAD_DIM, n), F32)
    return pl.pallas_call(
        _inproj_kernel,
        grid=(n // tm,),
        in_specs=[pl.BlockSpec((tm, D_MODEL), row), pl.BlockSpec((1, D_MODEL), whole),
                  pl.BlockSpec((1, D_MODEL, IN_WIDTH), lambda i: (layer, 0, 0), pipeline_mode=pl.Buffered(1)),
                  pl.BlockSpec((tm, LANES), tab), pl.BlockSpec((tm, LANES), tab), pl.BlockSpec((tm, LANES), tab)],
        out_specs=[pl.BlockSpec((tm, A_WIDTH), row), pl.BlockSpec((tm, 4 * B_WIDTH), row),
                   pl.BlockSpec((tm, IDX_WIDTH), row),
                   kv_spec, kv_spec, ik_spec,
                   pl.BlockSpec((A_KV_HEADS, tm, HEAD_DIM), hmaj), pl.BlockSpec((A_KV_HEADS, tm, LANES), hmaj),
                   pl.BlockSpec((tm, HEAD_DIM), row), pl.BlockSpec((8, tm), tmin)],
        out_shape=[jax.ShapeDtypeStruct((n, A_WIDTH), BF16), jax.ShapeDtypeStruct((n, 4 * B_WIDTH), F32),
                   jax.ShapeDtypeStruct((n, IDX_WIDTH), BF16),
                   kv_shape, kv_shape, ik_shape,
                   jax.ShapeDtypeStruct((A_KV_HEADS, n, HEAD_DIM), BF16),
                   jax.ShapeDtypeStruct((A_KV_HEADS, n, LANES), BF16),
                   jax.ShapeDtypeStruct((n, HEAD_DIM), BF16), jax.ShapeDtypeStruct((8, n), F32)],
        scratch_shapes=[pltpu.VMEM((D_MODEL, IN_WIDTH_PAD), BF16)],
        compiler_params=_cparams("arbitrary"),
        name="in_proj",
    )(x, g, w_all, *tabs)


def _rope_tables(pos):
    half = ROT_DIM // 2
    inv_freq = 1.0 / (ROPE_THETA ** (jnp.arange(half, dtype=F32) * (2.0 / ROT_DIM)))
    ang = pos.astype(F32)[:, None] * inv_freq[None, :]
    cos, sin = jnp.cos(ang), jnp.sin(ang)
    t = pos.shape[0]
    pad = HEAD_DIM - ROT_DIM
    c = jnp.concatenate([cos, cos, jnp.ones((t, pad), F32)], axis=1)
    s1 = jnp.concatenate([-sin, jnp.zeros((t, half + pad), F32)], axis=1)
    s2 = jnp.concatenate([jnp.zeros((t, half), F32), sin, jnp.zeros((t, pad), F32)], axis=1)
    rep = LANES // HEAD_DIM
    return tuple(jnp.tile(a, (1, rep)) for a in (c, s1, s2))


def _dsa_kernel(q_ref, iq_ref, iw_ref, k_ref, v_ref, ik_ref, o_ref, key_ref, hi_ref, lo_ref, acc_ref, m_ref,
                s0_ref, s1_ref, alpha_ref, p_ref, bias_ref, mt0_ref, mt1_ref, *, l_real, pos0, kb_rows, topk, tq):
    i = pl.program_id(1)
    qpos0 = pos0 + i * tq
    kend = jnp.minimum(((qpos0 + tq - 1) // CHUNK + 1) * CHUNK, l_real)
    cb_rows = 2 * kb_rows
    nkc = (kend + cb_rows - 1) // cb_rows
    nkb = 2 * nkc
    qchunk = (qpos0 + lax.broadcasted_iota(I32, (1, tq), 1)) // CHUNK
    row_iota = lax.broadcasted_iota(I32, (kb_rows, tq), 0)

    iq = iq_ref[...]
    iq4 = jnp.concatenate([iq[:, h * HEAD_DIM:(h + 1) * HEAD_DIM] for h in range(IDX_HEADS)], axis=0)
    iw = iw_ref[...] * (IDX_HEADS ** -0.5 * HEAD_DIM ** -0.5)

    def score_body(kb, carry):
        r0 = pl.multiple_of(kb * kb_rows, kb_rows)
        ikb = ik_ref[pl.ds(r0, kb_rows), :]
        lg4 = lax.dot_general(ikb, iq4, _NT, preferred_element_type=F32)
        acc = jnp.zeros((kb_rows, tq), F32)
        for h in range(IDX_HEADS):
            acc = acc + iw[h:h + 1, :] * jnp.maximum(lg4[:, h * tq:(h + 1) * tq], 0.0)
        bits = pltpu.bitcast(acc + 0.0, I32)
        key = bits ^ ((bits >> 31) & 0x7FFFFFFF)
        kpos = r0 + row_iota
        adm = (kpos // CHUNK <= qchunk) & (kpos < l_real)
        key = jnp.where(adm, key, INT_MIN)
        key_ref[pl.ds(r0, kb_rows), :] = key
        hi_ref[pl.ds(r0, kb_rows), :] = (key >> 16).astype(I16)
        return carry

    lax.fori_loop(0, nkb, score_body, 0)

    def count(pred):
        def body(kc, c):
            r0 = pl.multiple_of(kc * cb_rows, cb_rows)
            hit = pred(key_ref[pl.ds(r0, cb_rows), :], r0)
            return c + jnp.where(hit, 1, 0).reshape(cb_rows // 8, 8, tq).sum(axis=0)
        c = lax.fori_loop(0, nkc, body, jnp.zeros((8, tq), I32))
        return c.sum(axis=0, keepdims=True)

    def count16(ref, pred):
        one, zero = jnp.ones((), BF16), jnp.zeros((), BF16)

        def body(kc, c):
            r0 = pl.multiple_of(kc * cb_rows, cb_rows)
            w = jnp.where(pred(ref[pl.ds(r0, cb_rows), :]), one, zero).reshape(cb_rows // 16, 16, tq)
            parts = [w[r] for r in range(cb_rows // 16)]
            while len(parts) > 1:
                parts = [a + b for a, b in zip(parts[::2], parts[1::2])] + parts[len(parts) & ~1:]
            return c + parts[0]
        c = lax.fori_loop(0, nkc, body, jnp.zeros((16, tq), BF16))
        return c.astype(F32).sum(axis=0, keepdims=True).astype(I32)

    def bisect16(ref, kth):
        c0 = count16(ref, lambda blk: blk >= jnp.zeros((), I16))
        start = jnp.where(c0 >= kth, 0, -32768).astype(I32)

        def bit_body(it, prefix):
            cand = prefix | lax.shift_left(jnp.int32(1), 14 - it)
            c = count16(ref, lambda blk: blk >= cand.astype(I16))
            return jnp.where(c >= kth, cand, prefix)

        return lax.fori_loop(0, 15, bit_body, start)

    tau_hi = bisect16(hi_ref, topk)
    tau_hi16 = tau_hi.astype(I16)
    kth_lo = topk - count16(hi_ref, lambda blk: blk > tau_hi16)

    def lo_body(kc, carry):
        r0 = pl.multiple_of(kc * cb_rows, cb_rows)
        lo = ((key_ref[pl.ds(r0, cb_rows), :] & 0xFFFF) - 32768).astype(I16)
        lo_ref[pl.ds(r0, cb_rows), :] = jnp.where(hi_ref[pl.ds(r0, cb_rows), :] == tau_hi16, lo,
                                                 jnp.full((), -32768, I16))
        return carry

    lax.fori_loop(0, nkc, lo_body, 0)
    tau_lo = bisect16(lo_ref, kth_lo)
    tau = lax.shift_left(tau_hi, 16) | (tau_lo + 32768)
    n_ge = count(lambda blk, r0: blk >= tau)

    @pl.when(jnp.max(jnp.where((n_ge > topk) & (tau != INT_MIN), 1, 0)) > 0)
    def _():
        need = (topk - count(lambda blk, r0: blk > tau)).astype(F32)
        ri = lax.broadcasted_iota(I32, (kb_rows, kb_rows), 0)
        ci = lax.broadcasted_iota(I32, (kb_rows, kb_rows), 1)
        upto = jnp.where(ci <= ri, 1.0, 0.0).astype(BF16)

        def strike(kc, seen):
            for half in range(2):
                rows = pl.ds(pl.multiple_of(kc * cb_rows + half * kb_rows, kb_rows), kb_rows)
                key = key_ref[rows, :]
                tie = key == tau
                rank = _dot(upto, jnp.where(tie, 1.0, 0.0).astype(BF16)) + seen
                key_ref[rows, :] = jnp.where(tie & (rank > need), INT_MIN, key)
                seen = rank[kb_rows - 1:kb_rows, :]
            return seen

        lax.fori_loop(0, nkc, strike, jnp.zeros((1, tq), F32))

    lane_rep = kb_rows // LANES
    tau_c = jnp.tile(jnp.broadcast_to(tau, (LANES, tq)).T, (1, lane_rep))
    acc_ref[...] = jnp.zeros(acc_ref.shape, F32)
    m_ref[...] = jnp.full(m_ref.shape, NEG_BIG, F32)
    qs = q_ref[...]
    q4 = [jnp.concatenate([qs[:, (j * A_GROUP + g) * HEAD_DIM:(j * A_GROUP + g + 1) * HEAD_DIM]
                           for g in range(A_GROUP)], axis=0) for j in range(A_KV_HEADS)]

    def qk_scores(kb, s_slot, mt_slot):
        r0 = pl.multiple_of(jnp.minimum(kb, nkb - 1) * kb_rows, kb_rows)
        key_t = key_ref[pl.ds(r0, kb_rows), :].T
        bias_ref[...] = jnp.where((key_t >= tau_c) & (key_t != INT_MIN), 0.0, NEG_BIG)
        for j in range(A_KV_HEADS):
            s = lax.dot_general(q4[j], k_ref[j, pl.ds(r0, kb_rows), :], _NT, preferred_element_type=F32)
            for g in range(A_GROUP):
                rows = slice(g * tq, (g + 1) * tq)
                sm = s[rows, :] + bias_ref[...]
                s_slot[j, rows, :] = sm
                mt_slot[j, g] = jnp.broadcast_to(jnp.max(sm, axis=1, keepdims=True), (tq, LANES))

    def softmax_pv(kb, s_slot, mt_slot):
        r0 = pl.multiple_of(kb * kb_rows, kb_rows)
        for j in range(A_KV_HEADS):
            for g in range(A_GROUP):
                rows = slice(g * tq, (g + 1) * tq)
                m_old = m_ref[j, g]
                m_new = jnp.maximum(m_old, mt_slot[j, g])
                alpha_ref[j, g] = jnp.exp2(m_old - m_new)
                m_ref[j, g] = m_new
                p_ref[j, rows, :] = jnp.exp2(s_slot[j, rows, :] - jnp.tile(m_new, (1, lane_rep))).astype(BF16)
            pv = _dot(p_ref[j], v_ref[j, pl.ds(r0, kb_rows), :])
            acc_ref[j] = alpha_ref[j] * acc_ref[j] + pv.reshape(A_GROUP, tq, LANES)

    qk_scores(0, s0_ref, mt0_ref)

    def pair_body(i, carry):
        kb = 2 * i
        qk_scores(kb + 1, s1_ref, mt1_ref)
        softmax_pv(kb, s0_ref, mt0_ref)
        qk_scores(kb + 2, s0_ref, mt0_ref)
        softmax_pv(kb + 1, s1_ref, mt1_ref)
        return carry

    lax.fori_loop(0, nkc, pair_body, 0)

    outs = []
    for j in range(A_KV_HEADS):
        acc = acc_ref[j]
        for g in range(A_GROUP):
            outs.append(acc[g, :, :HEAD_DIM] * (1.0 / acc[g, :, HEAD_DIM:HEAD_DIM + 1]))
    o_ref[...] = jnp.concatenate(outs, axis=1)


def _dsa(q, iq, iw_t, k_hm, v_aug, ik, *, bsz, l_real, pos0, kb_rows, topk, tq):
    t_len = q.shape[0] // bsz
    l_pad = ik.shape[0] // bsz
    nq = t_len // tq
    assert l_pad // 16 <= 256, "packed hit counts are summed in bf16, exact only up to 256 per slot"
    kern = functools.partial(_dsa_kernel, l_real=l_real, pos0=pos0, kb_rows=kb_rows, topk=topk, tq=tq)
    qrow = lambda b, i: (b * nq + i, 0)
    return pl.pallas_call(
        kern,
        grid=(bsz, nq),
        in_specs=[pl.BlockSpec((tq, A_WIDTH), qrow), pl.BlockSpec((tq, IDX_WIDTH), qrow),
                  pl.BlockSpec((8, tq), lambda b, i: (0, b * nq + i)),
                  pl.BlockSpec((A_KV_HEADS, l_pad, HEAD_DIM), lambda b, i: (0, b, 0)),
                  pl.BlockSpec((A_KV_HEADS, l_pad, LANES), lambda b, i: (0, b, 0)),
                  pl.BlockSpec((l_pad, HEAD_DIM), lambda b, i: (b, 0))],
        out_specs=pl.BlockSpec((tq, A_WIDTH), qrow),
        out_shape=jax.ShapeDtypeStruct((bsz * t_len, A_WIDTH), F32),
        scratch_shapes=[pltpu.VMEM((l_pad, tq), I32), pltpu.VMEM((l_pad, tq), I16), pltpu.VMEM((l_pad, tq), I16),
                        pltpu.VMEM((A_KV_HEADS, A_GROUP, tq, LANES), F32),
                        pltpu.VMEM((A_KV_HEADS, A_GROUP, tq, LANES), F32),
                        pltpu.VMEM((A_KV_HEADS, A_GROUP * tq, kb_rows), F32),
                        pltpu.VMEM((A_KV_HEADS, A_GROUP * tq, kb_rows), F32),
                        pltpu.VMEM((A_KV_HEADS, A_GROUP, tq, LANES), F32),
                        pltpu.VMEM((A_KV_HEADS, A_GROUP * tq, kb_rows), BF16),
                        pltpu.VMEM((tq, kb_rows), F32),
                        pltpu.VMEM((A_KV_HEADS, A_GROUP, tq, LANES), F32),
                        pltpu.VMEM((A_KV_HEADS, A_GROUP, tq, LANES), F32)],
        compiler_params=_cparams("parallel", "arbitrary"),
        name="dsa",
    )(q, iq, iw_t, k_hm, v_aug, ik)


def _hgrn_kernel(bq_ref, bf_ref, bi_ref, bg_ref, lb_ref, gn_ref, s0_ref, o_ref, sfin_ref,
                 st_ref, q_s, kk_s, b_s, qe_s, kd_s, dec_s, o_s, e_s, *, layer, tb, nseq):
    tstep = pl.program_id(1)
    nchunk = tb // B_CHUNK

    @pl.when(tstep == 0)
    def _():
        st_ref[...] = s0_ref[...]

    lbr = lb_ref[...]
    lbe = jnp.exp(lbr - jnp.max(lbr, axis=0, keepdims=True))
    lbs = lbe / jnp.sum(lbe, axis=0, keepdims=True)
    cum = lbs[0:1, :]
    first = cum
    for r in range(1, layer + 1):
        cum = cum + lbs[r:r + 1, :]
    lb = cum - first

    ri = lax.broadcasted_iota(I32, (tb, tb), 0)
    ci = lax.broadcasted_iota(I32, (tb, tb), 1)
    same = (ri // B_CHUNK) == (ci // B_CHUNK)
    tri = jnp.where(same & (ci <= ri), 1.0, 0.0).astype(BF16)
    ones = jnp.where(same, 1.0, 0.0).astype(BF16)
    for u in range(nseq):
        f = lb + (1.0 - lb) * _sigmoid(bf_ref[u])
        bq = bq_ref[u]
        q = bq * _sigmoid(bq) * (HEAD_DIM ** -0.5)
        g3 = _split3(jnp.log(f) * LOG2E)
        b = _dot(tri, g3[0]) + _dot(tri, g3[1]) + _dot(tri, g3[2])
        blast = _dot(ones, g3[0]) + _dot(ones, g3[1]) + _dot(ones, g3[2])
        q_s[u] = q
        kk_s[u] = 1.0 - f
        b_s[u] = b
        qe_s[u] = q * jnp.exp2(b)
        kd_s[u] = (1.0 - f) * jnp.exp2(blast - b)
        dec_s[u] = jnp.exp2(blast)

    hi = lax.broadcasted_iota(I32, (HALF, HALF), 0) // HEAD_DIM
    hj = lax.broadcasted_iota(I32, (HALF, HALF), 1) // HEAD_DIM
    bd_mask = hi == hj
    bd = jnp.where(bd_mask, 1.0, 0.0).astype(BF16)
    t_iota = lax.broadcasted_iota(I32, (B_CHUNK, B_WIDTH), 0)

    def seg_sum(x, two_pass):
        out = []
        for half in range(2):
            xs = x[:, half * HALF:(half + 1) * HALF]
            if two_pass:
                xh, xl = _split2(xs)
                out.append(_dot(xh, bd) + _dot(xl, bd))
            else:
                out.append(_dot(xs.astype(BF16), bd))
        return jnp.concatenate(out, axis=1)

    def chunk_body(c, carry):
        r0 = pl.multiple_of(c * B_CHUNK, B_CHUNK)
        rows = pl.ds(r0, B_CHUNK)
        for u in range(nseq):
            qc, kc, bc, vc = q_s[u, rows, :], kk_s[u, rows, :], b_s[u, rows, :], bi_ref[u, rows, :]
            qe, kd = qe_s[u, rows, :].astype(BF16), kd_s[u, rows, :].astype(BF16)
            dec = dec_s[u, pl.ds(r0, 1), :]
            for s in range(B_CHUNK):
                ex = jnp.exp2(jnp.where(t_iota >= s, bc - bc[s:s + 1, :], NEG_BIG))
                e_s[u, s * B_CHUNK:(s + 1) * B_CHUNK, :] = qc * kc[s:s + 1, :] * ex
            aexp = seg_sum(e_s[u], False)
            o = jnp.zeros((B_CHUNK, B_WIDTH), F32)
            for s in range(B_CHUNK):
                o = o + aexp[s * B_CHUNK:(s + 1) * B_CHUNK, :] * vc[s:s + 1, :]
            inter = []
            vb = vc.astype(BF16)
            for half in range(2):
                cols = slice(half * HALF, (half + 1) * HALF)
                st = st_ref[u, half]
                inter.append(lax.dot_general(qe[:, cols], st.astype(BF16), _NT, preferred_element_type=F32))
                upd = lax.dot_general(vb[:, cols], kd[:, cols], _TN, preferred_element_type=F32)
                st_ref[u, half] = st * dec[:, cols] + jnp.where(bd_mask, upd, 0.0)
            o_s[u, rows, :] = o + jnp.concatenate(inter, axis=1)
        return carry

    lax.fori_loop(0, nchunk, chunk_body, 0)

    for u in range(nseq):
        o = o_s[u]
        ms = seg_sum(o * o, True) * (1.0 / HEAD_DIM)
        bg = bg_ref[u]
        o_ref[u] = o * lax.rsqrt(ms + EPS) * gn_ref[...] * (bg * _sigmoid(bg))

    @pl.when(tstep == pl.num_programs(1) - 1)
    def _():
        sfin_ref[...] = st_ref[...]


def _hgrn(hb, lb_raw, gn_tiled, s0_bd, *, layer, bsz, t_len, tb):
    nt = t_len // tb
    nseq = 2 if bsz % 2 == 0 else 1
    hb3 = hb.reshape(bsz, t_len, 4 * B_WIDTH)
    col = lambda j: (lambda b, t: (b, t, j))
    scr = lambda r: pltpu.VMEM((nseq, r, B_WIDTH), F32)
    kern = functools.partial(_hgrn_kernel, layer=layer, tb=tb, nseq=nseq)
    blk = (nseq, tb, B_WIDTH)
    out, s_fin = pl.pallas_call(
        kern,
        grid=(bsz // nseq, nt),
        in_specs=[pl.BlockSpec(blk, col(0)), pl.BlockSpec(blk, col(1)), pl.BlockSpec(blk, col(2)),
                  pl.BlockSpec(blk, col(3)),
                  pl.BlockSpec(lb_raw.shape, lambda b, t: (0, 0)),
                  pl.BlockSpec((1, B_WIDTH), lambda b, t: (0, 0)),
                  pl.BlockSpec((nseq, 2, HALF, HALF), lambda b, t: (b, 0, 0, 0))],
        out_specs=[pl.BlockSpec(blk, lambda b, t: (b, t, 0)),
                   pl.BlockSpec((nseq, 2, HALF, HALF), lambda b, t: (b, 0, 0, 0))],
        out_shape=[jax.ShapeDtypeStruct((bsz, t_len, B_WIDTH), F32),
                   jax.ShapeDtypeStruct((bsz, 2, HALF, HALF), F32)],
        scratch_shapes=[pltpu.VMEM((nseq, 2, HALF, HALF), F32), scr(tb), scr(tb), scr(tb), scr(tb), scr(tb),
                        scr(tb), scr(tb), scr(B_CHUNK * B_CHUNK)],
        compiler_params=_cparams("parallel", "arbitrary"),
        name="hgrn2",
    )(hb3, hb3, hb3, hb3, lb_raw, gn_tiled, s0_bd)
    return out.reshape(bsz * t_len, B_WIDTH), s_fin


def _state_to_tiles(s):
    bsz = s.shape[0]
    st = jnp.swapaxes(s, -1, -2).reshape(bsz, 2, B_HEADS // 2, HEAD_DIM, HEAD_DIM)
    eye = jnp.eye(B_HEADS // 2, dtype=s.dtype)
    t = st[:, :, :, :, None, :] * eye[None, None, :, None, :, None]
    return t.reshape(bsz, 2, HALF, HALF)


def _tiles_to_state(t):
    bsz = t.shape[0]
    t6 = t.reshape(bsz, 2, B_HEADS // 2, HEAD_DIM, B_HEADS // 2, HEAD_DIM)
    diag = jnp.stack([t6[:, :, h, :, h, :] for h in range(B_HEADS // 2)], axis=2)
    return jnp.swapaxes(diag.reshape(bsz, B_HEADS, HEAD_DIM, HEAD_DIM), -1, -2)


def _normmm_kernel(x_ref, g_ref, w_ref, o_ref):
    o_ref[...] = _dot(_rms(x_ref[...], g_ref[...]).astype(BF16), w_ref[...])


def _norm_matmul(x, g, w, tm):
    n, m = x.shape[0], w.shape[1]
    return pl.pallas_call(
        _normmm_kernel,
        grid=(n // tm,),
        in_specs=[pl.BlockSpec((tm, D_MODEL), lambda i: (i, 0)), pl.BlockSpec((1, D_MODEL), lambda i: (0, 0)),
                  pl.BlockSpec((D_MODEL, m), lambda i: (0, 0))],
        out_specs=pl.BlockSpec((tm, m), lambda i: (i, 0)),
        out_shape=jax.ShapeDtypeStruct((n, m), F32),
        compiler_params=_cparams("parallel"),
        name="norm_matmul",
    )(x, g, w)


def _swiglu_mid(xn, w1, w3):
    a = _dot(xn, w1)
    return (a * _sigmoid(a) * _dot(xn, w3)).astype(BF16)


def _postmix_kernel(oa_ref, ob_ref, x_ref, wa_ref, wb_ref, gm_ref, wq_ref, mk_ref, mv_ref, wo_ref, *rest,
                    with_ffn, final, chunks):
    o_ref = rest[-1]
    x = (x_ref[...] + _dot(oa_ref[...].astype(BF16), wa_ref[...]) + _dot(ob_ref[...].astype(BF16), wb_ref[...]))
    q = _dot(_rms(x, gm_ref[...]).astype(BF16), wq_ref[...]).astype(BF16)
    nbt = mk_ref.shape[0]
    rpb = x.shape[0] // nbt
    att = []
    for bb in range(nbt):
        rows = slice(bb * rpb, (bb + 1) * rpb)
        heads = []
        for h in range(MEM_HEADS):
            cols = slice(h * MEM_HEAD_DIM, (h + 1) * MEM_HEAD_DIM)
            s = lax.dot_general(q[rows, cols], mk_ref[bb, :, cols], _NT,
                                preferred_element_type=F32) * (MEM_HEAD_DIM ** -0.5)
            p = jnp.exp(s - jnp.max(s, axis=-1, keepdims=True))
            p = p * (1.0 / jnp.sum(p, axis=-1, keepdims=True))
            heads.append(_dot(p.astype(BF16), mv_ref[bb, :, cols]).astype(BF16))
        att.append(jnp.concatenate(heads, axis=1))
    y = x + _dot(att[0] if nbt == 1 else jnp.concatenate(att, axis=0), wo_ref[...])
    if with_ffn:
        gffn_ref, w1_ref, w3_ref, w2_ref, gf_ref = rest[:5]
        yn = _rms(y, gffn_ref[...]).astype(BF16)
        lo = 0
        for width in chunks:
            cols = slice(lo, lo + width)
            y = y + _dot(_swiglu_mid(yn, w1_ref[:, cols], w3_ref[:, cols]), w2_ref[cols, :])
            lo += width
        if final:
            y = _rms(y, gf_ref[...])
    o_ref[...] = y


def _postmix(oa, ob, x, wa, wb, gm, wq, mk, mv, wo, ffn=None, *, rows_per_batch, tm, final=False):
    n = x.shape[0]
    n_mem = mk.shape[1]
    row = lambda i: (i, 0)
    whole = lambda i: (0, 0)
    if tm <= rows_per_batch:
        per = rows_per_batch // tm
        mem_spec = pl.BlockSpec((1, n_mem, D_MODEL), lambda i: (i // per, 0, 0))
    else:
        mem_spec = pl.BlockSpec((tm // rows_per_batch, n_mem, D_MODEL), lambda i: (i, 0, 0))
    const = lambda shape: pl.BlockSpec(shape, whole, pipeline_mode=pl.Buffered(1))
    in_specs = [pl.BlockSpec((tm, A_WIDTH), row), pl.BlockSpec((tm, B_WIDTH), row), pl.BlockSpec((tm, D_MODEL), row),
                const((A_WIDTH, D_MODEL)), const((B_WIDTH, D_MODEL)), const((1, D_MODEL)),
                const((D_MODEL, D_MODEL)),
                mem_spec, mem_spec, const((D_MODEL, D_MODEL))]
    args = [oa, ob, x, wa, wb, gm, wq, mk, mv, wo]
    chunks = ()
    if ffn is not None:
        d_ff = ffn[1].shape[1]
        step = 4 * HALF
        chunks = tuple(min(step, d_ff - lo) for lo in range(0, d_ff, step))
        in_specs += [const((1, D_MODEL)), const((D_MODEL, d_ff)), const((D_MODEL, d_ff)), const((d_ff, D_MODEL)),
                     const((1, D_MODEL))]
        args += list(ffn)
    return pl.pallas_call(
        functools.partial(_postmix_kernel, with_ffn=ffn is not None, final=final, chunks=chunks),
        grid=(n // tm,),
        in_specs=in_specs,
        out_specs=pl.BlockSpec((tm, D_MODEL), row),
        out_shape=jax.ShapeDtypeStruct((n, D_MODEL), F32),
        compiler_params=_cparams("parallel"),
        name="post_mix",
    )(*args)


def _router_kernel(x_ref, g_ref, rt_ref, comb_ref, sel_ref):
    xh, xl = _split2(_rms(x_ref[...], g_ref[...]))
    rh, rl = _split2(rt_ref[...])
    nt_dot = lambda a, b: lax.dot_general(a, b, _NT, preferred_element_type=F32)
    logits = nt_dot(rh, xh) + (nt_dot(rh, xl) + nt_dot(rl, xh))
    e_idx = lax.broadcasted_iota(I32, logits.shape, 0)
    m1 = jnp.max(logits, axis=0, keepdims=True)
    i1 = jnp.min(jnp.where(logits == m1, e_idx, N_EXPERTS), axis=0, keepdims=True)
    rest = jnp.where(e_idx == i1, -jnp.inf, logits)
    m2 = jnp.max(rest, axis=0, keepdims=True)
    i2 = jnp.min(jnp.where(rest == m2, e_idx, N_EXPERTS), axis=0, keepdims=True)
    e2 = jnp.exp(m2 - m1)
    g1 = 1.0 / (1.0 + e2)
    comb_ref[...] = jnp.where(e_idx == i1, g1, 0.0) + jnp.where(e_idx == i2, e2 * g1, 0.0)
    sel_ref[...] = jnp.where((e_idx == i1) | (e_idx == i2), 1.0, 0.0)


def _router(x, g, rt, tm):
    n = x.shape[0]
    return pl.pallas_call(
        _router_kernel,
        grid=(n // tm,),
        in_specs=[pl.BlockSpec((tm, D_MODEL), lambda i: (i, 0)), pl.BlockSpec((1, D_MODEL), lambda i: (0, 0)),
                  pl.BlockSpec((N_EXPERTS, D_MODEL), lambda i: (0, 0))],
        out_specs=[pl.BlockSpec((N_EXPERTS, tm), lambda i: (0, i)), pl.BlockSpec((N_EXPERTS, tm), lambda i: (0, i))],
        out_shape=[jax.ShapeDtypeStruct((N_EXPERTS, n), F32), jax.ShapeDtypeStruct((N_EXPERTS, n), F32)],
        compiler_params=_cparams("parallel"),
        name="moe_router",
    )(x, g, rt)


def _moe_kernel(x_ref, g_ref, selt_ref, sel_ref, comb_ref, w1_ref, w3_ref, w2_ref, gf_ref, o_ref,
                xn_ref, posr_ref, posc_ref, cnt_ref, xg_ref, yg_ref, *, final, nb, sub):
    e = pl.program_id(1)
    j = pl.program_id(2)
    last_j = pl.num_programs(2) - 1

    @pl.when((e == 0) & (j == 0))
    def _():
        x = x_ref[...]
        xn_ref[...] = _rms(x, g_ref[...]).astype(BF16)
        o_ref[...] = x
        cc = min(nb, 256)
        ri = lax.broadcasted_iota(I32, (cc, cc), 0)
        ci = lax.broadcasted_iota(I32, (cc, cc), 1)
        before_r = jnp.where(ri < ci, 1.0, 0.0).astype(BF16)
        before_c = jnp.where(ci < ri, 1.0, 0.0).astype(BF16)
        run_r = jnp.zeros((N_EXPERTS, 1), F32)
        run_c = jnp.zeros((1, N_EXPERTS), F32)
        for c in range(nb // cc):
            cols = slice(c * cc, (c + 1) * cc)
            mr = selt_ref[:, cols]
            posr_ref[:, cols] = jnp.where(mr > 0.0, _dot(mr.astype(BF16), before_r) + run_r, -1.0)
            run_r = run_r + jnp.sum(mr, axis=1, keepdims=True)
            mc = sel_ref[cols, :]
            posc_ref[cols, :] = jnp.where(mc > 0.0, _dot(before_c, mc.astype(BF16)) + run_c, -1.0)
            run_c = run_c + jnp.sum(mc, axis=0, keepdims=True)
        for k in range(N_EXPERTS):
            cnt_ref[k] = run_c[0, k].astype(I32)

    cnt = cnt_ref[e]
    nfull = cnt // sub
    rem = cnt - nfull * sub
    tail_sizes = [m for m in (64, 128) if m < sub] + [sub]

    def for_tiles(fn):
        def body(s, carry):
            fn(pl.multiple_of(s * sub, sub), sub)
            return carry
        lax.fori_loop(0, nfull, body, 0)
        r0 = pl.multiple_of(nfull * sub, sub)
        lo = 0
        for m in tail_sizes:
            @pl.when((rem > lo) & (rem <= m))
            def _():
                fn(r0, m)
            lo = m

    def gather_matrix(r0, m):
        slot_r = posr_ref[pl.ds(e, 1), :]
        want = (r0 + lax.broadcasted_iota(I32, (m, nb), 0)).astype(F32)
        return jnp.where(slot_r == want, 1.0, 0.0).astype(BF16)

    def gather(r0, m):
        xg_ref[pl.ds(r0, m), :] = _dot(gather_matrix(r0, m), xn_ref[...]).astype(BF16)

    def ffn(r0, m):
        rows = pl.ds(r0, m)
        part = _dot(_swiglu_mid(xg_ref[rows, :], w1_ref[0], w3_ref[0]), w2_ref[0])

        @pl.when(j == 0)
        def _():
            yg_ref[rows, :] = part

        @pl.when(j > 0)
        def _():
            yg_ref[rows, :] += part

    @pl.when(j == 0)
    def _():
        for_tiles(gather)

    for_tiles(ffn)

    @pl.when(j == last_j)
    def _():
        lane8 = lax.broadcasted_iota(I32, (nb, N_EXPERTS), 1)
        slot_c = jnp.sum(jnp.where(lane8 == e, posc_ref[...], 0.0), axis=1, keepdims=True)
        gate_c = jnp.sum(jnp.where(lane8 == e, comb_ref[...], 0.0), axis=1, keepdims=True)
        gate3 = jnp.concatenate(_split3(gate_c), axis=1)

        def scatter(r0, m):
            gate = jnp.sum(_dot(gather_matrix(r0, m), gate3), axis=1, keepdims=True)
            y = (yg_ref[pl.ds(r0, m), :] * gate).astype(BF16)
            want = (r0 + lax.broadcasted_iota(I32, (nb, m), 1)).astype(F32)
            o_ref[...] += _dot(jnp.where(slot_c == want, 1.0, 0.0).astype(BF16), y)

        for_tiles(scatter)

        if final:
            @pl.when(e == pl.num_programs(1) - 1)
            def _():
                o_ref[...] = _rms(o_ref[...], gf_ref[...])


def _moe(x, g, sel_t, sel, comb, w1, w3, w2, gf, *, final, nb, tf):
    n = x.shape[0]
    d_ff = w1.shape[2]
    sub = min(nb, MOE_TILE_ROWS)
    row = lambda i, e, j: (i, 0)
    whole = lambda i, e, j: (0, 0)
    return pl.pallas_call(
        functools.partial(_moe_kernel, final=final, nb=nb, sub=sub),
        grid=(n // nb, N_EXPERTS, d_ff // tf),
        in_specs=[pl.BlockSpec((nb, D_MODEL), row), pl.BlockSpec((1, D_MODEL), whole),
                  pl.BlockSpec((N_EXPERTS, nb), lambda i, e, j: (0, i)),
                  pl.BlockSpec((nb, N_EXPERTS), row), pl.BlockSpec((nb, N_EXPERTS), row),
                  pl.BlockSpec((1, D_MODEL, tf), lambda i, e, j: (e, 0, j)),
                  pl.BlockSpec((1, D_MODEL, tf), lambda i, e, j: (e, 0, j)),
                  pl.BlockSpec((1, tf, D_MODEL), lambda i, e, j: (e, j, 0)), pl.BlockSpec((1, D_MODEL), whole)],
        out_specs=pl.BlockSpec((nb, D_MODEL), row),
        out_shape=jax.ShapeDtypeStruct((n, D_MODEL), F32),
        scratch_shapes=[pltpu.VMEM((nb, D_MODEL), BF16), pltpu.VMEM((N_EXPERTS, nb), F32),
                        pltpu.VMEM((nb, N_EXPERTS), F32), pltpu.SMEM((N_EXPERTS,), I32),
                        pltpu.VMEM((nb + sub, D_MODEL), BF16), pltpu.VMEM((nb + sub, D_MODEL), F32)],
        compiler_params=_cparams("parallel", "arbitrary", "arbitrary"),
        name="moe",
    )(x, g, sel_t, sel, comb, w1, w3, w2, gf)


def _row_tile(n, cap):
    t = min(n, cap)
    assert n % t == 0, (n, cap)
    return t


def _ff_tile(d_ff):
    for parts in (2, 4, 11, 22):
        if d_ff % parts == 0 and (d_ff // parts) % LANES == 0:
            return d_ff // parts
    return d_ff


def _mixer(x, tabs, w, *, bsz, t_len, pos0, past, s0, layer):
    n = bsz * t_len
    qa, hb, iq, k_t, v_t, ik_t, k_hm, v_aug, ik_b, iw_t = _in_proj(x, w["norm_mix"], w["w_in"], layer, tabs,
                                                                   t_len, _row_tile(n, 512))
    l_real = t_len if past is None else past[0].shape[1] + t_len
    topk = min(TOPK_MAX, l_real // 4)
    tq = 2 * LANES if t_len % (2 * LANES) == 0 else LANES
    kb_rows = 256 if l_real % 256 == 0 else LANES
    l_pad = -(-l_real // (2 * kb_rows)) * (2 * kb_rows)
    t_pad = -(-t_len // tq) * tq
    if past is not None:
        pk, pv, pik = past
        one = jnp.ones(pv.shape[:2] + (A_KV_HEADS, 1), BF16)
        zero = jnp.zeros(pv.shape[:2] + (A_KV_HEADS, LANES - HEAD_DIM - 1), BF16)
        pv_aug = jnp.concatenate([pv.astype(BF16), one, zero], axis=-1)
        hm = lambda a: jnp.transpose(a, (2, 0, 1, 3))
        k_hm = jnp.concatenate([hm(pk.astype(BF16)), k_hm.reshape(A_KV_HEADS, bsz, t_len, HEAD_DIM)], axis=2)
        v_aug = jnp.concatenate([hm(pv_aug), v_aug.reshape(A_KV_HEADS, bsz, t_len, LANES)], axis=2)
        ik_b = jnp.concatenate([pik.astype(BF16), ik_b.reshape(bsz, t_len, HEAD_DIM)], axis=1)
    if l_pad != l_real or past is not None:
        padl = l_pad - l_real
        k_hm = jnp.pad(k_hm.reshape(A_KV_HEADS, bsz, l_real, HEAD_DIM), ((0, 0), (0, 0), (0, padl), (0, 0)))
        v_aug = jnp.pad(v_aug.reshape(A_KV_HEADS, bsz, l_real, LANES), ((0, 0), (0, 0), (0, padl), (0, 0)))
        ik_b = jnp.pad(ik_b.reshape(bsz, l_real, HEAD_DIM), ((0, 0), (0, padl), (0, 0)))
        k_hm = k_hm.reshape(A_KV_HEADS, bsz * l_pad, HEAD_DIM)
        v_aug = v_aug.reshape(A_KV_HEADS, bsz * l_pad, LANES)
        ik_b = ik_b.reshape(bsz * l_pad, HEAD_DIM)
    q_in, iq_in, iw_in = qa, iq, iw_t
    if t_pad != t_len:
        padq = lambda a: jnp.pad(a.reshape(bsz, t_len, -1),
                                 ((0, 0), (0, t_pad - t_len), (0, 0))).reshape(bsz * t_pad, -1)
        q_in, iq_in = padq(qa), padq(iq)
        iw_in = jnp.pad(iw_t.reshape(8, bsz, t_len), ((0, 0), (0, 0), (0, t_pad - t_len))).reshape(8, bsz * t_pad)
    oa = _dsa(q_in, iq_in, iw_in, k_hm, v_aug, ik_b, bsz=bsz, l_real=l_real, pos0=pos0, kb_rows=kb_rows,
              topk=topk, tq=tq)
    if t_pad != t_len:
        oa = oa.reshape(bsz, t_pad, A_WIDTH)[:, :t_len].reshape(n, A_WIDTH)
    ob, s_tiles = _hgrn(hb, w["hgrn_lb"], w["gnorm"], _state_to_tiles(s0), layer=layer, bsz=bsz, t_len=t_len,
                        tb=_row_tile(t_len, 256))
    return oa, ob, k_t, v_t, ik_t, _tiles_to_state(s_tiles)


def _from_time_minor(a, bsz, t_len, heads):
    if a.ndim == 2:
        a = a.reshape((-1, HEAD_DIM, bsz, t_len))
        a = jnp.transpose(a, (2, 3, 0, 1))
    else:
        a = jnp.moveaxis(a, -1, 1)
        a = a.reshape(bsz, t_len, -1, HEAD_DIM)
    return a if heads is not None else a.reshape(bsz, t_len, HEAD_DIM)


def kernel(x_prompt, x_sample, cache_k, cache_v, cache_idx_k, state_hgrn, cache_mem_k, cache_mem_v, mem_prompt,
           norm_mix, w_in, hgrn_lb, hgrn_gnorm, w_out, norm_mem, norm_memkv, w_mq, w_mk, w_mv, w_mo,
           norm_ffn, ffn_w1, ffn_w3, ffn_w2, moe_router, moe_w1, moe_w3, moe_w2, norm_final):
    depth = w_in.shape[0]
    bp, tp, _ = x_prompt.shape
    bs, ts, _ = x_sample.shape
    past = cache_k.shape[2]
    n_mem = mem_prompt.shape[1]
    tabs_p = _rope_tables(jnp.arange(tp, dtype=I32))
    tabs_s = tuple(jnp.tile(a, (bs, 1)) for a in _rope_tables(past + jnp.arange(ts, dtype=I32)))
    xp = x_prompt.reshape(bp * tp, D_MODEL)
    xs = x_sample.reshape(bs * ts, D_MODEL)
    mem = mem_prompt.reshape(bp * n_mem, D_MODEL)
    s0_p = jnp.zeros((bp, B_HEADS, HEAD_DIM, HEAD_DIM), F32)
    gfin = norm_final.reshape(1, D_MODEL)
    row2 = lambda a: a.reshape(1, -1)
    outs = {k: [] for k in ("kp", "vp", "ikp", "sp", "mkp", "mvp", "ks", "vs", "iks", "ss")}
    for l in range(depth):
        w = {"norm_mix": row2(norm_mix[l]), "w_in": w_in, "hgrn_lb": hgrn_lb.astype(F32),
             "gnorm": row2(jnp.tile(hgrn_gnorm[l], B_HEADS)),
             "w_out_a": w_out[l, :A_WIDTH].astype(BF16), "w_out_b": w_out[l, A_WIDTH:].astype(BF16)}
        oa_p, ob_p, kp, vp, ikp, sp = _mixer(xp, tabs_p, w, bsz=bp, t_len=tp, pos0=0, past=None, s0=s0_p, layer=l)
        oa_s, ob_s, ks, vs, iks, ss = _mixer(xs, tabs_s, w, bsz=bs, t_len=ts, pos0=past,
                                             past=(cache_k[l], cache_v[l], cache_idx_k[l]), s0=state_hgrn[l],
                                             layer=l)
        wkv = jnp.concatenate([w_mk[l], w_mv[l]], axis=1).astype(BF16)
        mkv = _norm_matmul(mem, row2(norm_memkv[l]), wkv, _row_tile(bp * n_mem, 256))
        mk_p = mkv[:, :D_MODEL].reshape(bp, n_mem, D_MODEL)
        mv_p = mkv[:, D_MODEL:].reshape(bp, n_mem, D_MODEL)
        mk_s = cache_mem_k[l].reshape(bs, n_mem, D_MODEL).astype(BF16)
        mv_s = cache_mem_v[l].reshape(bs, n_mem, D_MODEL).astype(BF16)
        wq, wo, gm = w_mq[l].astype(BF16), w_mo[l].astype(BF16), row2(norm_mem[l])
        final = l == depth - 1
        gf = row2(norm_ffn[l])
        tail = (w["w_out_a"], w["w_out_b"], gm, wq)
        if l % 2 == 0:
            dense = (gf, ffn_w1[l // 2].astype(BF16), ffn_w3[l // 2].astype(BF16), ffn_w2[l // 2].astype(BF16), gfin)
            xp = _postmix(oa_p, ob_p, xp, *tail, mk_p.astype(BF16), mv_p.astype(BF16), wo, dense,
                          rows_per_batch=tp, tm=_row_tile(tp, 512), final=final)
            xs = _postmix(oa_s, ob_s, xs, *tail, mk_s, mv_s, wo, dense,
                          rows_per_batch=ts, tm=_row_tile(bs * ts, 512), final=final)
        else:
            xp = _postmix(oa_p, ob_p, xp, *tail, mk_p.astype(BF16), mv_p.astype(BF16), wo,
                          rows_per_batch=tp, tm=_row_tile(tp, 512))
            xs = _postmix(oa_s, ob_s, xs, *tail, mk_s, mv_s, wo, rows_per_batch=ts, tm=_row_tile(bs * ts, 512))
            w1, w3, w2 = moe_w1[l // 2].astype(BF16), moe_w3[l // 2].astype(BF16), moe_w2[l // 2].astype(BF16)
            rt = moe_router[l // 2].T
            tf = _ff_tile(w1.shape[2])
            comb_p, sel_p = _router(xp, gf, rt, _row_tile(bp * tp, 512))
            comb_s, sel_s = _router(xs, gf, rt, _row_tile(bs * ts, 512))
            xp = _moe(xp, gf, sel_p, sel_p.T, comb_p.T, w1, w3, w2, gfin, final=final,
                      nb=_row_tile(bp * tp, 1024), tf=tf)
            xs = _moe(xs, gf, sel_s, sel_s.T, comb_s.T, w1, w3, w2, gfin, final=final,
                      nb=_row_tile(bs * ts, 1024), tf=tf)
        outs["kp"].append(_from_time_minor(kp, bp, tp, A_KV_HEADS))
        outs["vp"].append(_from_time_minor(vp, bp, tp, A_KV_HEADS))
        outs["ikp"].append(_from_time_minor(ikp, bp, tp, None)); outs["sp"].append(sp)
        outs["mkp"].append(mk_p.reshape(bp, n_mem, MEM_HEADS, MEM_HEAD_DIM))
        outs["mvp"].append(mv_p.reshape(bp, n_mem, MEM_HEADS, MEM_HEAD_DIM))
        outs["ks"].append(_from_time_minor(ks, bs, ts, A_KV_HEADS))
        outs["vs"].append(_from_time_minor(vs, bs, ts, A_KV_HEADS))
        outs["iks"].append(_from_time_minor(iks, bs, ts, None)); outs["ss"].append(ss)
    y_prompt = xp.reshape(bp, tp, D_MODEL)
    y_sample = xs.reshape(bs, ts, D_MODEL)
    st = lambda k: jnp.stack(outs[k])
    return (y_prompt, y_sample, st("kp"), st("vp"), st("ikp"), st("sp"), st("mkp"), st("mvp"),
            st("ks"), st("vs"), st("iks"), st("ss"))
```

```python
import functools

import numpy as np
import jax
import jax.numpy as jnp
from jax import lax
from jax.experimental import pallas as pl
from jax.experimental.pallas import tpu as pltpu

F32 = jnp.float32
BF16 = jnp.bfloat16
I32 = jnp.int32
I16 = jnp.int16

D_MODEL = 1024
CHUNK = 64
EPS = 1e-6
ROPE_THETA = 500000.0
HEAD_DIM = 64
ROT_DIM = HEAD_DIM // 4
A_HEADS = 8
A_KV_HEADS = 2
A_GROUP = A_HEADS // A_KV_HEADS
A_WIDTH = A_HEADS * HEAD_DIM
KV_WIDTH = A_KV_HEADS * HEAD_DIM
IDX_HEADS = 4
IDX_WIDTH = IDX_HEADS * HEAD_DIM
TOPK_MAX = 256
B_HEADS = 8
B_WIDTH = B_HEADS * HEAD_DIM
B_CHUNK = 16
MEM_HEADS = 4
MEM_HEAD_DIM = D_MODEL // MEM_HEADS
N_EXPERTS = 8
MOE_TILE_ROWS = 256
LANES = 128
HALF = 256
IN_SPLITS = (A_WIDTH, KV_WIDTH, KV_WIDTH, IDX_WIDTH, HEAD_DIM, IDX_HEADS, B_WIDTH, B_WIDTH, B_WIDTH, B_WIDTH)
IN_WIDTH = sum(IN_SPLITS)
IN_WIDTH_PAD = 3200
VMEM_LIMIT = 56 * 1024 * 1024
INT_MIN = -2147483648
NEG_BIG = -1e30
LOG2E = 1.4426950408889634

_NT = (((1,), (1,)), ((), ()))
_TN = (((0,), (0,)), ((), ()))


def _cparams(*sem):
    return pltpu.CompilerParams(dimension_semantics=sem, vmem_limit_bytes=VMEM_LIMIT)


def _rms(x, g):
    return x * lax.rsqrt(jnp.mean(x * x, axis=-1, keepdims=True) + EPS) * g


def _sigmoid(x):
    return 1.0 / (1.0 + jnp.exp(-x))


def _dot(a, b):
    return jnp.dot(a, b, preferred_element_type=F32)


def _split2(x):
    hi = x.astype(BF16)
    lo = (x - hi.astype(F32)).astype(BF16)
    return hi, lo


def _split3(x):
    hi = x.astype(BF16)
    r = x - hi.astype(F32)
    mid = r.astype(BF16)
    lo = (r - mid.astype(F32)).astype(BF16)
    return hi, mid, lo


def _rope128(xb, c, s1, s2):
    return xb * c + pltpu.roll(xb, LANES - ROT_DIM // 2, 1) * s1 + pltpu.roll(xb, ROT_DIM // 2, 1) * s2


def _inproj_kernel(x_ref, g_ref, wraw_ref, c_ref, s1_ref, s2_ref,
                   qa_ref, hb_ref, iq_ref, kt_ref, vt_ref, ikt_ref, khm_ref, vaug_ref, ikb_ref, iwt_ref, w_ref):
    @pl.when(pl.program_id(0) == 0)
    def _():
        src = np.cumsum([0] + list(IN_SPLITS))
        dst = 0
        for i in (0, 6, 7, 8, 9, 1, 2, 3):
            width = int(src[i + 1] - src[i])
            w_ref[:, dst:dst + width] = wraw_ref[0, :, int(src[i]):int(src[i + 1])].astype(BF16)
            dst += width
        tail = jnp.concatenate([wraw_ref[0, :, int(src[4]):int(src[6])],
                                jnp.zeros((D_MODEL, LANES - HEAD_DIM - IDX_HEADS), F32)], axis=1)
        w_ref[:, dst:dst + LANES] = tail.astype(BF16)

    xn = _rms(x_ref[...], g_ref[...]).astype(BF16)
    c, s1, s2 = c_ref[...], s1_ref[...], s2_ref[...]
    for j in range(A_WIDTH // HALF):
        h = _dot(xn, w_ref[:, j * HALF:(j + 1) * HALF])
        for i in range(HALF // LANES):
            lo = j * HALF + i * LANES
            qa_ref[:, lo:lo + LANES] = _rope128(h[:, i * LANES:(i + 1) * LANES], c, s1, s2)
    for j in range(4):
        lo = A_WIDTH + j * B_WIDTH
        hb_ref[:, j * B_WIDTH:(j + 1) * B_WIDTH] = _dot(xn, w_ref[:, lo:lo + B_WIDTH])
    base = A_WIDTH + 4 * B_WIDTH
    kv = _dot(xn, w_ref[:, base:base + HALF])
    k = _rope128(kv[:, :LANES], c, s1, s2)
    v = kv[:, LANES:]
    iq = _dot(xn, w_ref[:, base + HALF:base + 2 * HALF])
    iq_ref[:, 0:128] = _rope128(iq[:, :LANES], c, s1, s2)
    iq_ref[:, 128:256] = _rope128(iq[:, LANES:], c, s1, s2)
    lane = lax.broadcasted_iota(I32, c.shape, 1)
    first = lane < HEAD_DIM
    ikw = _rope128(_dot(xn, w_ref[:, base + 512:base + 640]),
                   jnp.where(first, c, 1.0), jnp.where(first, s1, 0.0), jnp.where(first, s2, 0.0))
    kt_ref[...] = k.T.reshape(kt_ref.shape)
    vt_ref[...] = v.T.reshape(vt_ref.shape)
    ikw_t = ikw.T
    ikt_ref[...] = ikw_t[:HEAD_DIM].reshape(ikt_ref.shape)
    iwt_ref[...] = ikw_t[HEAD_DIM:HEAD_DIM + 8]
    ikb_ref[...] = ikw[:, :HEAD_DIM].astype(BF16)
    ones_col = jnp.where(lane == HEAD_DIM, 1.0, 0.0)
    for j in range(A_KV_HEADS):
        khm_ref[j] = k[:, j * HEAD_DIM:(j + 1) * HEAD_DIM].astype(BF16)
        vj = v if j == 0 else pltpu.roll(v, LANES - j * HEAD_DIM, 1)
        vaug_ref[j] = jnp.where(first, vj, ones_col).astype(BF16)


def _in_proj(x, g, w_all, layer, tabs, t_len, tm):
    n = x.shape[0]
    assert tabs[0].shape[0] % tm == 0, (tabs[0].shape, tm)
    nt = tabs[0].shape[0] // tm
    per_batch = t_len % tm == 0
    ntb = t_len // tm if per_batch else 1
    row = lambda i: (i, 0)
    tab = lambda i: (i % nt, 0)
    whole = lambda i: (0, 0)
    tmin = lambda i: (0, i)
    hmaj = lambda i: (0, i, 0)
    if per_batch:
        kv_spec = pl.BlockSpec((1, A_KV_HEADS, HEAD_DIM, tm), lambda i: (i // ntb, 0, 0, i % ntb))
        ik_spec = pl.BlockSpec((1, HEAD_DIM, tm), lambda i: (i // ntb, 0, i % ntb))
        kv_shape = jax.ShapeDtypeStruct((n // t_len, A_KV_HEADS, HEAD_DIM, t_len), F32)
        ik_shape = jax.ShapeDtypeStruct((n // t_len, HEAD_DIM, t_len), F32)
    else:
        kv_spec, ik_spec = pl.BlockSpec((KV_WIDTH, tm), tmin), pl.BlockSpec((HEAD_DIM, tm), tmin)
        kv_shape, ik_shape = jax.ShapeDtypeStruct((KV_WIDTH, n), F32), jax.ShapeDtypeStruct((HEAD_DIM, n), F32)
    return pl.pallas_call(
        _inproj_kernel,
        grid=(n // tm,),
        in_specs=[pl.BlockSpec((tm, D_MODEL), row), pl.BlockSpec((1, D_MODEL), whole),
                  pl.BlockSpec((1, D_MODEL, IN_WIDTH), lambda i: (layer, 0, 0), pipeline_mode=pl.Buffered(1)),
                  pl.BlockSpec((tm, LANES), tab), pl.BlockSpec((tm, LANES), tab), pl.BlockSpec((tm, LANES), tab)],
        out_specs=[pl.BlockSpec((tm, A_WIDTH), row), pl.BlockSpec((tm, 4 * B_WIDTH), row),
                   pl.BlockSpec((tm, IDX_WIDTH), row),
                   kv_spec, kv_spec, ik_spec,
                   pl.BlockSpec((A_KV_HEADS, tm, HEAD_DIM), hmaj), pl.BlockSpec((A_KV_HEADS, tm, LANES), hmaj),
                   pl.BlockSpec((tm, HEAD_DIM), row), pl.BlockSpec((8, tm), tmin)],
        out_shape=[jax.ShapeDtypeStruct((n, A_WIDTH), F32), jax.ShapeDtypeStruct((n, 4 * B_WIDTH), F32),
                   jax.ShapeDtypeStruct((n, IDX_WIDTH), F32),
                   kv_shape, kv_shape, ik_shape,
                   jax.ShapeDtypeStruct((A_KV_HEADS, n, HEAD_DIM), BF16),
                   jax.ShapeDtypeStruct((A_KV_HEADS, n, LANES), BF16),
                   jax.ShapeDtypeStruct((n, HEAD_DIM), BF16), jax.ShapeDtypeStruct((8, n), F32)],
        scratch_shapes=[pltpu.VMEM((D_MODEL, IN_WIDTH_PAD), BF16)],
        compiler_params=_cparams("arbitrary"),
        name="in_proj",
    )(x, g, w_all, *tabs)


def _rope_tables(pos):
    half = ROT_DIM // 2
    inv_freq = 1.0 / (ROPE_THETA ** (jnp.arange(half, dtype=F32) * (2.0 / ROT_DIM)))
    ang = pos.astype(F32)[:, None] * inv_freq[None, :]
    cos, sin = jnp.cos(ang), jnp.sin(ang)
    t = pos.shape[0]
    pad = HEAD_DIM - ROT_DIM
    c = jnp.concatenate([cos, cos, jnp.ones((t, pad), F32)], axis=1)
    s1 = jnp.concatenate([-sin, jnp.zeros((t, half + pad), F32)], axis=1)
    s2 = jnp.concatenate([jnp.zeros((t, half), F32), sin, jnp.zeros((t, pad), F32)], axis=1)
    rep = LANES // HEAD_DIM
    return tuple(jnp.tile(a, (1, rep)) for a in (c, s1, s2))


def _dsa_kernel(q_ref, iq_ref, iw_ref, k_ref, v_ref, ik_ref, o_ref, key_ref, hi_ref, lo_ref, acc_ref, m_ref,
                s0_ref, s1_ref, alpha_ref, p_ref, bias_ref, mt0_ref, mt1_ref, *, l_real, pos0, kb_rows, topk, tq):
    i = pl.program_id(1)
    qpos0 = pos0 + i * tq
    kend = jnp.minimum(((qpos0 + tq - 1) // CHUNK + 1) * CHUNK, l_real)
    cb_rows = 2 * kb_rows
    nkc = (kend + cb_rows - 1) // cb_rows
    nkb = 2 * nkc
    qchunk = (qpos0 + lax.broadcasted_iota(I32, (1, tq), 1)) // CHUNK
    row_iota = lax.broadcasted_iota(I32, (kb_rows, tq), 0)

    iq = iq_ref[...].astype(BF16)
    iq4 = jnp.concatenate([iq[:, h * HEAD_DIM:(h + 1) * HEAD_DIM] for h in range(IDX_HEADS)], axis=0)
    iw = iw_ref[...] * (IDX_HEADS ** -0.5 * HEAD_DIM ** -0.5)

    def score_body(kb, carry):
        r0 = pl.multiple_of(kb * kb_rows, kb_rows)
        ikb = ik_ref[pl.ds(r0, kb_rows), :]
        lg4 = lax.dot_general(ikb, iq4, _NT, preferred_element_type=F32)
        acc = jnp.zeros((kb_rows, tq), F32)
        for h in range(IDX_HEADS):
            acc = acc + iw[h:h + 1, :] * jnp.maximum(lg4[:, h * tq:(h + 1) * tq], 0.0)
        bits = pltpu.bitcast(acc + 0.0, I32)
        key = bits ^ ((bits >> 31) & 0x7FFFFFFF)
        kpos = r0 + row_iota
        adm = (kpos // CHUNK <= qchunk) & (kpos < l_real)
        key = jnp.where(adm, key, INT_MIN)
        key_ref[pl.ds(r0, kb_rows), :] = key
        hi_ref[pl.ds(r0, kb_rows), :] = (key >> 16).astype(I16)
        return carry

    lax.fori_loop(0, nkb, score_body, 0)

    def count(pred):
        def body(kc, c):
            r0 = pl.multiple_of(kc * cb_rows, cb_rows)
            hit = pred(key_ref[pl.ds(r0, cb_rows), :], r0)
            return c + jnp.where(hit, 1, 0).reshape(cb_rows // 8, 8, tq).sum(axis=0)
        c = lax.fori_loop(0, nkc, body, jnp.zeros((8, tq), I32))
        return c.sum(axis=0, keepdims=True)

    def count16(ref, pred):
        one, zero = jnp.ones((), BF16), jnp.zeros((), BF16)

        def body(kc, c):
            r0 = pl.multiple_of(kc * cb_rows, cb_rows)
            w = jnp.where(pred(ref[pl.ds(r0, cb_rows), :]), one, zero).reshape(cb_rows // 16, 16, tq)
            parts = [w[r] for r in range(cb_rows // 16)]
            while len(parts) > 1:
                parts = [a + b for a, b in zip(parts[::2], parts[1::2])] + parts[len(parts) & ~1:]
            return c + parts[0]
        c = lax.fori_loop(0, nkc, body, jnp.zeros((16, tq), BF16))
        return c.astype(F32).sum(axis=0, keepdims=True).astype(I32)

    def bisect16(ref, kth):
        c0 = count16(ref, lambda blk: blk >= jnp.zeros((), I16))
        start = jnp.where(c0 >= kth, 0, -32768).astype(I32)

        def bit_body(it, prefix):
            cand = prefix | lax.shift_left(jnp.int32(1), 14 - it)
            c = count16(ref, lambda blk: blk >= cand.astype(I16))
            return jnp.where(c >= kth, cand, prefix)

        return lax.fori_loop(0, 15, bit_body, start)

    tau_hi = bisect16(hi_ref, topk)
    tau_hi16 = tau_hi.astype(I16)
    kth_lo = topk - count16(hi_ref, lambda blk: blk > tau_hi16)

    def lo_body(kc, carry):
        r0 = pl.multiple_of(kc * cb_rows, cb_rows)
        lo = ((key_ref[pl.ds(r0, cb_rows), :] & 0xFFFF) - 32768).astype(I16)
        lo_ref[pl.ds(r0, cb_rows), :] = jnp.where(hi_ref[pl.ds(r0, cb_rows), :] == tau_hi16, lo,
                                                 jnp.full((), -32768, I16))
        return carry

    lax.fori_loop(0, nkc, lo_body, 0)
    tau_lo = bisect16(lo_ref, kth_lo)
    tau = lax.shift_left(tau_hi, 16) | (tau_lo + 32768)
    n_ge = count(lambda blk, r0: blk >= tau)

    @pl.when(jnp.max(jnp.where((n_ge > topk) & (tau != INT_MIN), 1, 0)) > 0)
    def _():
        need = (topk - count(lambda blk, r0: blk > tau)).astype(F32)
        ri = lax.broadcasted_iota(I32, (kb_rows, kb_rows), 0)
        ci = lax.broadcasted_iota(I32, (kb_rows, kb_rows), 1)
        upto = jnp.where(ci <= ri, 1.0, 0.0).astype(BF16)

        def strike(kc, seen):
            for half in range(2):
                rows = pl.ds(pl.multiple_of(kc * cb_rows + half * kb_rows, kb_rows), kb_rows)
                key = key_ref[rows, :]
                tie = key == tau
                rank = _dot(upto, jnp.where(tie, 1.0, 0.0).astype(BF16)) + seen
                key_ref[rows, :] = jnp.where(tie & (rank > need), INT_MIN, key)
                seen = rank[kb_rows - 1:kb_rows, :]
            return seen

        lax.fori_loop(0, nkc, strike, jnp.zeros((1, tq), F32))

    lane_rep = kb_rows // LANES
    tau_c = jnp.tile(jnp.broadcast_to(tau, (LANES, tq)).T, (1, lane_rep))
    acc_ref[...] = jnp.zeros(acc_ref.shape, F32)
    m_ref[...] = jnp.full(m_ref.shape, NEG_BIG, F32)
    qs = (q_ref[...] * (HEAD_DIM ** -0.5 * LOG2E)).astype(BF16)
    q4 = [jnp.concatenate([qs[:, (j * A_GROUP + g) * HEAD_DIM:(j * A_GROUP + g + 1) * HEAD_DIM]
                           for g in range(A_GROUP)], axis=0) for j in range(A_KV_HEADS)]

    def qk_scores(kb, s_slot, mt_slot):
        r0 = pl.multiple_of(jnp.minimum(kb, nkb - 1) * kb_rows, kb_rows)
        key_t = key_ref[pl.ds(r0, kb_rows), :].T
        bias_ref[...] = jnp.where((key_t >= tau_c) & (key_t != INT_MIN), 0.0, NEG_BIG)
        for j in range(A_KV_HEADS):
            s = lax.dot_general(q4[j], k_ref[j, pl.ds(r0, kb_rows), :], _NT, preferred_element_type=F32)
            for g in range(A_GROUP):
                rows = slice(g * tq, (g + 1) * tq)
                sm = s[rows, :] + bias_ref[...]
                s_slot[j, rows, :] = sm
                mt_slot[j, g] = jnp.broadcast_to(jnp.max(sm, axis=1, keepdims=True), (tq, LANES))

    def softmax_pv(kb, s_slot, mt_slot):
        r0 = pl.multiple_of(kb * kb_rows, kb_rows)
        for j in range(A_KV_HEADS):
            for g in range(A_GROUP):
                rows = slice(g * tq, (g + 1) * tq)
                m_old = m_ref[j, g]
                m_new = jnp.maximum(m_old, mt_slot[j, g])
                alpha_ref[j, g] = jnp.exp2(m_old - m_new)
                m_ref[j, g] = m_new
                p_ref[j, rows, :] = jnp.exp2(s_slot[j, rows, :] - jnp.tile(m_new, (1, lane_rep))).astype(BF16)
            pv = _dot(p_ref[j], v_ref[j, pl.ds(r0, kb_rows), :])
            acc_ref[j] = alpha_ref[j] * acc_ref[j] + pv.reshape(A_GROUP, tq, LANES)

    qk_scores(0, s0_ref, mt0_ref)

    def pair_body(i, carry):
        kb = 2 * i
        qk_scores(kb + 1, s1_ref, mt1_ref)
        softmax_pv(kb, s0_ref, mt0_ref)
        qk_scores(kb + 2, s0_ref, mt0_ref)
        softmax_pv(kb + 1, s1_ref, mt1_ref)
        return carry

    lax.fori_loop(0, nkc, pair_body, 0)

    outs = []
    for j in range(A_KV_HEADS):
        acc = acc_ref[j]
        for g in range(A_GROUP):
            outs.append(acc[g, :, :HEAD_DIM] * (1.0 / acc[g, :, HEAD_DIM:HEAD_DIM + 1]))
    o_ref[...] = jnp.concatenate(outs, axis=1)


def _dsa(q, iq, iw_t, k_hm, v_aug, ik, *, bsz, l_real, pos0, kb_rows, topk, tq):
    t_len = q.shape[0] // bsz
    l_pad = ik.shape[0] // bsz
    nq = t_len // tq
    assert l_pad // 16 <= 256, "packed hit counts are summed in bf16, exact only up to 256 per slot"
    kern = functools.partial(_dsa_kernel, l_real=l_real, pos0=pos0, kb_rows=kb_rows, topk=topk, tq=tq)
    qrow = lambda b, i: (b * nq + i, 0)
    return pl.pallas_call(
        kern,
        grid=(bsz, nq),
        in_specs=[pl.BlockSpec((tq, A_WIDTH), qrow), pl.BlockSpec((tq, IDX_WIDTH), qrow),
                  pl.BlockSpec((8, tq), lambda b, i: (0, b * nq + i)),
                  pl.BlockSpec((A_KV_HEADS, l_pad, HEAD_DIM), lambda b, i: (0, b, 0)),
                  pl.BlockSpec((A_KV_HEADS, l_pad, LANES), lambda b, i: (0, b, 0)),
                  pl.BlockSpec((l_pad, HEAD_DIM), lambda b, i: (b, 0))],
        out_specs=pl.BlockSpec((tq, A_WIDTH), qrow),
        out_shape=jax.ShapeDtypeStruct((bsz * t_len, A_WIDTH), F32),
        scratch_shapes=[pltpu.VMEM((l_pad, tq), I32), pltpu.VMEM((l_pad, tq), I16), pltpu.VMEM((l_pad, tq), I16),
                        pltpu.VMEM((A_KV_HEADS, A_GROUP, tq, LANES), F32),
                        pltpu.VMEM((A_KV_HEADS, A_GROUP, tq, LANES), F32),
                        pltpu.VMEM((A_KV_HEADS, A_GROUP * tq, kb_rows), F32),
                        pltpu.VMEM((A_KV_HEADS, A_GROUP * tq, kb_rows), F32),
                        pltpu.VMEM((A_KV_HEADS, A_GROUP, tq, LANES), F32),
                        pltpu.VMEM((A_KV_HEADS, A_GROUP * tq, kb_rows), BF16),
                        pltpu.VMEM((tq, kb_rows), F32),
                        pltpu.VMEM((A_KV_HEADS, A_GROUP, tq, LANES), F32),
                        pltpu.VMEM((A_KV_HEADS, A_GROUP, tq, LANES), F32)],
        compiler_params=_cparams("parallel", "arbitrary"),
        name="dsa",
    )(q, iq, iw_t, k_hm, v_aug, ik)


def _hgrn_kernel(bq_ref, bf_ref, bi_ref, bg_ref, lb_ref, gn_ref, s0_ref, o_ref, sfin_ref,
                 st_ref, q_s, kk_s, b_s, qe_s, kd_s, dec_s, o_s, e_s, *, layer, tb, nseq):
    tstep = pl.program_id(1)
    nchunk = tb // B_CHUNK

    @pl.when(tstep == 0)
    def _():
        st_ref[...] = s0_ref[...]

    lbr = lb_ref[...]
    lbe = jnp.exp(lbr - jnp.max(lbr, axis=0, keepdims=True))
    lbs = lbe / jnp.sum(lbe, axis=0, keepdims=True)
    cum = lbs[0:1, :]
    first = cum
    for r in range(1, layer + 1):
        cum = cum + lbs[r:r + 1, :]
    lb = cum - first

    ri = lax.broadcasted_iota(I32, (tb, tb), 0)
    ci = lax.broadcasted_iota(I32, (tb, tb), 1)
    same = (ri // B_CHUNK) == (ci // B_CHUNK)
    tri = jnp.where(same & (ci <= ri), 1.0, 0.0).astype(BF16)
    ones = jnp.where(same, 1.0, 0.0).astype(BF16)
    for u in range(nseq):
        f = lb + (1.0 - lb) * _sigmoid(bf_ref[u])
        bq = bq_ref[u]
        q = bq * _sigmoid(bq) * (HEAD_DIM ** -0.5)
        g3 = _split3(jnp.log(f) * LOG2E)
        b = _dot(tri, g3[0]) + _dot(tri, g3[1]) + _dot(tri, g3[2])
        blast = _dot(ones, g3[0]) + _dot(ones, g3[1]) + _dot(ones, g3[2])
        q_s[u] = q
        kk_s[u] = 1.0 - f
        b_s[u] = b
        qe_s[u] = q * jnp.exp2(b)
        kd_s[u] = (1.0 - f) * jnp.exp2(blast - b)
        dec_s[u] = jnp.exp2(blast)

    hi = lax.broadcasted_iota(I32, (HALF, HALF), 0) // HEAD_DIM
    hj = lax.broadcasted_iota(I32, (HALF, HALF), 1) // HEAD_DIM
    bd_mask = hi == hj
    bd = jnp.where(bd_mask, 1.0, 0.0).astype(BF16)
    t_iota = lax.broadcasted_iota(I32, (B_CHUNK, B_WIDTH), 0)

    def seg_sum(x, two_pass):
        out = []
        for half in range(2):
            xs = x[:, half * HALF:(half + 1) * HALF]
            if two_pass:
                xh, xl = _split2(xs)
                out.append(_dot(xh, bd) + _dot(xl, bd))
            else:
                out.append(_dot(xs.astype(BF16), bd))
        return jnp.concatenate(out, axis=1)

    def chunk_body(c, carry):
        r0 = pl.multiple_of(c * B_CHUNK, B_CHUNK)
        rows = pl.ds(r0, B_CHUNK)
        for u in range(nseq):
            qc, kc, bc, vc = q_s[u, rows, :], kk_s[u, rows, :], b_s[u, rows, :], bi_ref[u, rows, :]
            qe, kd = qe_s[u, rows, :].astype(BF16), kd_s[u, rows, :].astype(BF16)
            dec = dec_s[u, pl.ds(r0, 1), :]
            for s in range(B_CHUNK):
                ex = jnp.exp2(jnp.where(t_iota >= s, bc - bc[s:s + 1, :], NEG_BIG))
                e_s[u, s * B_CHUNK:(s + 1) * B_CHUNK, :] = qc * kc[s:s + 1, :] * ex
            aexp = seg_sum(e_s[u], False)
            o = jnp.zeros((B_CHUNK, B_WIDTH), F32)
            for s in range(B_CHUNK):
                o = o + aexp[s * B_CHUNK:(s + 1) * B_CHUNK, :] * vc[s:s + 1, :]
            inter = []
            vb = vc.astype(BF16)
            for half in range(2):
                cols = slice(half * HALF, (half + 1) * HALF)
                st = st_ref[u, half]
                inter.append(lax.dot_general(qe[:, cols], st.astype(BF16), _NT, preferred_element_type=F32))
                upd = lax.dot_general(vb[:, cols], kd[:, cols], _TN, preferred_element_type=F32)
                st_ref[u, half] = st * dec[:, cols] + jnp.where(bd_mask, upd, 0.0)
            o_s[u, rows, :] = o + jnp.concatenate(inter, axis=1)
        return carry

    lax.fori_loop(0, nchunk, chunk_body, 0)

    for u in range(nseq):
        o = o_s[u]
        ms = seg_sum(o * o, True) * (1.0 / HEAD_DIM)
        bg = bg_ref[u]
        o_ref[u] = o * lax.rsqrt(ms + EPS) * gn_ref[...] * (bg * _sigmoid(bg))

    @pl.when(tstep == pl.num_programs(1) - 1)
    def _():
        sfin_ref[...] = st_ref[...]


def _hgrn(hb, lb_raw, gn_tiled, s0_bd, *, layer, bsz, t_len, tb):
    nt = t_len // tb
    nseq = 4 if bsz % 4 == 0 else (2 if bsz % 2 == 0 else 1)
    hb3 = hb.reshape(bsz, t_len, 4 * B_WIDTH)
    col = lambda j: (lambda b, t: (b, t, j))
    scr = lambda r: pltpu.VMEM((nseq, r, B_WIDTH), F32)
    kern = functools.partial(_hgrn_kernel, layer=layer, tb=tb, nseq=nseq)
    blk = (nseq, tb, B_WIDTH)
    out, s_fin = pl.pallas_call(
        kern,
        grid=(bsz // nseq, nt),
        in_specs=[pl.BlockSpec(blk, col(0)), pl.BlockSpec(blk, col(1)), pl.BlockSpec(blk, col(2)),
                  pl.BlockSpec(blk, col(3)),
                  pl.BlockSpec(lb_raw.shape, lambda b, t: (0, 0)),
                  pl.BlockSpec((1, B_WIDTH), lambda b, t: (0, 0)),
                  pl.BlockSpec((nseq, 2, HALF, HALF), lambda b, t: (b, 0, 0, 0))],
        out_specs=[pl.BlockSpec(blk, lambda b, t: (b, t, 0)),
                   pl.BlockSpec((nseq, 2, HALF, HALF), lambda b, t: (b, 0, 0, 0))],
        out_shape=[jax.ShapeDtypeStruct((bsz, t_len, B_WIDTH), F32),
                   jax.ShapeDtypeStruct((bsz, 2, HALF, HALF), F32)],
        scratch_shapes=[pltpu.VMEM((nseq, 2, HALF, HALF), F32), scr(tb), scr(tb), scr(tb), scr(tb), scr(tb),
                        scr(tb), scr(tb), scr(B_CHUNK * B_CHUNK)],
        compiler_params=_cparams("parallel", "arbitrary"),
        name="hgrn2",
    )(hb3, hb3, hb3, hb3, lb_raw, gn_tiled, s0_bd)
    return out.reshape(bsz * t_len, B_WIDTH), s_fin


def _state_to_tiles(s):
    bsz = s.shape[0]
    st = jnp.swapaxes(s, -1, -2).reshape(bsz, 2, B_HEADS // 2, HEAD_DIM, HEAD_DIM)
    eye = jnp.eye(B_HEADS // 2, dtype=s.dtype)
    t = st[:, :, :, :, None, :] * eye[None, None, :, None, :, None]
    return t.reshape(bsz, 2, HALF, HALF)


def _tiles_to_state(t):
    bsz = t.shape[0]
    t6 = t.reshape(bsz, 2, B_HEADS // 2, HEAD_DIM, B_HEADS // 2, HEAD_DIM)
    diag = jnp.stack([t6[:, :, h, :, h, :] for h in range(B_HEADS // 2)], axis=2)
    return jnp.swapaxes(diag.reshape(bsz, B_HEADS, HEAD_DIM, HEAD_DIM), -1, -2)


def _normmm_kernel(x_ref, g_ref, w_ref, o_ref):
    o_ref[...] = _dot(_rms(x_ref[...], g_ref[...]).astype(BF16), w_ref[...])


def _norm_matmul(x, g, w, tm):
    n, m = x.shape[0], w.shape[1]
    return pl.pallas_call(
        _normmm_kernel,
        grid=(n // tm,),
        in_specs=[pl.BlockSpec((tm, D_MODEL), lambda i: (i, 0)), pl.BlockSpec((1, D_MODEL), lambda i: (0, 0)),
                  pl.BlockSpec((D_MODEL, m), lambda i: (0, 0))],
        out_specs=pl.BlockSpec((tm, m), lambda i: (i, 0)),
        out_shape=jax.ShapeDtypeStruct((n, m), F32),
        compiler_params=_cparams("parallel"),
        name="norm_matmul",
    )(x, g, w)


def _swiglu_mid(xn, w1, w3):
    a = _dot(xn, w1)
    return (a * _sigmoid(a) * _dot(xn, w3)).astype(BF16)


def _postmix_kernel(oa_ref, ob_ref, x_ref, wa_ref, wb_ref, gm_ref, wq_ref, mk_ref, mv_ref, wo_ref, *rest,
                    with_ffn, final, chunks):
    o_ref = rest[-1]
    x = (x_ref[...] + _dot(oa_ref[...].astype(BF16), wa_ref[...]) + _dot(ob_ref[...].astype(BF16), wb_ref[...]))
    q = _dot(_rms(x, gm_ref[...]).astype(BF16), wq_ref[...]).astype(BF16)
    nbt = mk_ref.shape[0]
    rpb = x.shape[0] // nbt
    att = []
    for bb in range(nbt):
        rows = slice(bb * rpb, (bb + 1) * rpb)
        heads = []
        for h in range(MEM_HEADS):
            cols = slice(h * MEM_HEAD_DIM, (h + 1) * MEM_HEAD_DIM)
            s = lax.dot_general(q[rows, cols], mk_ref[bb, :, cols], _NT,
                                preferred_element_type=F32) * (MEM_HEAD_DIM ** -0.5)
            p = jnp.exp(s - jnp.max(s, axis=-1, keepdims=True))
            p = p * (1.0 / jnp.sum(p, axis=-1, keepdims=True))
            heads.append(_dot(p.astype(BF16), mv_ref[bb, :, cols]).astype(BF16))
        att.append(jnp.concatenate(heads, axis=1))
    y = x + _dot(att[0] if nbt == 1 else jnp.concatenate(att, axis=0), wo_ref[...])
    if with_ffn:
        gffn_ref, w1_ref, w3_ref, w2_ref, gf_ref = rest[:5]
        yn = _rms(y, gffn_ref[...]).astype(BF16)
        lo = 0
        for width in chunks:
            cols = slice(lo, lo + width)
            y = y + _dot(_swiglu_mid(yn, w1_ref[:, cols], w3_ref[:, cols]), w2_ref[cols, :])
            lo += width
        if final:
            y = _rms(y, gf_ref[...])
    o_ref[...] = y


def _postmix(oa, ob, x, wa, wb, gm, wq, mk, mv, wo, ffn=None, *, rows_per_batch, tm, final=False):
    n = x.shape[0]
    n_mem = mk.shape[1]
    row = lambda i: (i, 0)
    whole = lambda i: (0, 0)
    if tm <= rows_per_batch:
        per = rows_per_batch // tm
        mem_spec = pl.BlockSpec((1, n_mem, D_MODEL), lambda i: (i // per, 0, 0))
    else:
        mem_spec = pl.BlockSpec((tm // rows_per_batch, n_mem, D_MODEL), lambda i: (i, 0, 0))
    const = lambda shape: pl.BlockSpec(shape, whole, pipeline_mode=pl.Buffered(1))
    in_specs = [pl.BlockSpec((tm, A_WIDTH), row), pl.BlockSpec((tm, B_WIDTH), row), pl.BlockSpec((tm, D_MODEL), row),
                const((A_WIDTH, D_MODEL)), const((B_WIDTH, D_MODEL)), const((1, D_MODEL)),
                const((D_MODEL, D_MODEL)),
                mem_spec, mem_spec, const((D_MODEL, D_MODEL))]
    args = [oa, ob, x, wa, wb, gm, wq, mk, mv, wo]
    chunks = ()
    if ffn is not None:
        d_ff = ffn[1].shape[1]
        step = 4 * HALF
        chunks = tuple(min(step, d_ff - lo) for lo in range(0, d_ff, step))
        in_specs += [const((1, D_MODEL)), const((D_MODEL, d_ff)), const((D_MODEL, d_ff)), const((d_ff, D_MODEL)),
                     const((1, D_MODEL))]
        args += list(ffn)
    return pl.pallas_call(
        functools.partial(_postmix_kernel, with_ffn=ffn is not None, final=final, chunks=chunks),
        grid=(n // tm,),
        in_specs=in_specs,
        out_specs=pl.BlockSpec((tm, D_MODEL), row),
        out_shape=jax.ShapeDtypeStruct((n, D_MODEL), F32),
        compiler_params=_cparams("parallel"),
        name="post_mix",
    )(*args)


def _router_kernel(x_ref, g_ref, rt_ref, comb_ref, sel_ref):
    xh, xl = _split2(_rms(x_ref[...], g_ref[...]))
    rh, rl = _split2(rt_ref[...])
    nt_dot = lambda a, b: lax.dot_general(a, b, _NT, preferred_element_type=F32)
    logits = nt_dot(rh, xh) + (nt_dot(rh, xl) + nt_dot(rl, xh))
    e_idx = lax.broadcasted_iota(I32, logits.shape, 0)
    m1 = jnp.max(logits, axis=0, keepdims=True)
    i1 = jnp.min(jnp.where(logits == m1, e_idx, N_EXPERTS), axis=0, keepdims=True)
    rest = jnp.where(e_idx == i1, -jnp.inf, logits)
    m2 = jnp.max(rest, axis=0, keepdims=True)
    i2 = jnp.min(jnp.where(rest == m2, e_idx, N_EXPERTS), axis=0, keepdims=True)
    e2 = jnp.exp(m2 - m1)
    g1 = 1.0 / (1.0 + e2)
    comb_ref[...] = jnp.where(e_idx == i1, g1, 0.0) + jnp.where(e_idx == i2, e2 * g1, 0.0)
    sel_ref[...] = jnp.where((e_idx == i1) | (e_idx == i2), 1.0, 0.0)


def _router(x, g, rt, tm):
    n = x.shape[0]
    return pl.pallas_call(
        _router_kernel,
        grid=(n // tm,),
        in_specs=[pl.BlockSpec((tm, D_MODEL), lambda i: (i, 0)), pl.BlockSpec((1, D_MODEL), lambda i: (0, 0)),
                  pl.BlockSpec((N_EXPERTS, D_MODEL), lambda i: (0, 0))],
        out_specs=[pl.BlockSpec((N_EXPERTS, tm), lambda i: (0, i)), pl.BlockSpec((N_EXPERTS, tm), lambda i: (0, i))],
        out_shape=[jax.ShapeDtypeStruct((N_EXPERTS, n), F32), jax.ShapeDtypeStruct((N_EXPERTS, n), F32)],
        compiler_params=_cparams("parallel"),
        name="moe_router",
    )(x, g, rt)


def _moe_kernel(x_ref, g_ref, selt_ref, sel_ref, comb_ref, w1_ref, w3_ref, w2_ref, gf_ref, o_ref,
                xn_ref, posr_ref, posc_ref, cnt_ref, xg_ref, yg_ref, *, final, nb, sub):
    e = pl.program_id(1)
    j = pl.program_id(2)
    last_j = pl.num_programs(2) - 1

    @pl.when((e == 0) & (j == 0))
    def _():
        x = x_ref[...]
        xn_ref[...] = _rms(x, g_ref[...]).astype(BF16)
        o_ref[...] = x
        cc = min(nb, 256)
        ri = lax.broadcasted_iota(I32, (cc, cc), 0)
        ci = lax.broadcasted_iota(I32, (cc, cc), 1)
        before_r = jnp.where(ri < ci, 1.0, 0.0).astype(BF16)
        before_c = jnp.where(ci < ri, 1.0, 0.0).astype(BF16)
        run_r = jnp.zeros((N_EXPERTS, 1), F32)
        run_c = jnp.zeros((1, N_EXPERTS), F32)
        for c in range(nb // cc):
            cols = slice(c * cc, (c + 1) * cc)
            mr = selt_ref[:, cols]
            posr_ref[:, cols] = jnp.where(mr > 0.0, _dot(mr.astype(BF16), before_r) + run_r, -1.0)
            run_r = run_r + jnp.sum(mr, axis=1, keepdims=True)
            mc = sel_ref[cols, :]
            posc_ref[cols, :] = jnp.where(mc > 0.0, _dot(before_c, mc.astype(BF16)) + run_c, -1.0)
            run_c = run_c + jnp.sum(mc, axis=0, keepdims=True)
        for k in range(N_EXPERTS):
            cnt_ref[k] = run_c[0, k].astype(I32)

    cnt = cnt_ref[e]
    nfull = cnt // sub
    rem = cnt - nfull * sub
    tail_sizes = [m for m in (64, 128) if m < sub] + [sub]

    def for_tiles(fn):
        def body(s, carry):
            fn(pl.multiple_of(s * sub, sub), sub)
            return carry
        lax.fori_loop(0, nfull, body, 0)
        r0 = pl.multiple_of(nfull * sub, sub)
        lo = 0
        for m in tail_sizes:
            @pl.when((rem > lo) & (rem <= m))
            def _():
                fn(r0, m)
            lo = m

    def gather_matrix(r0, m):
        slot_r = posr_ref[pl.ds(e, 1), :]
        want = (r0 + lax.broadcasted_iota(I32, (m, nb), 0)).astype(F32)
        return jnp.where(slot_r == want, 1.0, 0.0).astype(BF16)

    def gather(r0, m):
        xg_ref[pl.ds(r0, m), :] = _dot(gather_matrix(r0, m), xn_ref[...]).astype(BF16)

    def ffn(r0, m):
        rows = pl.ds(r0, m)
        part = _dot(_swiglu_mid(xg_ref[rows, :], w1_ref[0], w3_ref[0]), w2_ref[0])

        @pl.when(j == 0)
        def _():
            yg_ref[rows, :] = part

        @pl.when(j > 0)
        def _():
            yg_ref[rows, :] += part

    @pl.when(j == 0)
    def _():
        for_tiles(gather)

    for_tiles(ffn)

    @pl.when(j == last_j)
    def _():
        lane8 = lax.broadcasted_iota(I32, (nb, N_EXPERTS), 1)
        slot_c = jnp.sum(jnp.where(lane8 == e, posc_ref[...], 0.0), axis=1, keepdims=True)
        gate_c = jnp.sum(jnp.where(lane8 == e, comb_ref[...], 0.0), axis=1, keepdims=True)
        gate3 = jnp.concatenate(_split3(gate_c), axis=1)

        def scatter(r0, m):
            gate = jnp.sum(_dot(gather_matrix(r0, m), gate3), axis=1, keepdims=True)
            y = (yg_ref[pl.ds(r0, m), :] * gate).astype(BF16)
            want = (r0 + lax.broadcasted_iota(I32, (nb, m), 1)).astype(F32)
            o_ref[...] += _dot(jnp.where(slot_c == want, 1.0, 0.0).astype(BF16), y)

        for_tiles(scatter)

        if final:
            @pl.when(e == pl.num_programs(1) - 1)
            def _():
                o_ref[...] = _rms(o_ref[...], gf_ref[...])


def _moe(x, g, sel_t, sel, comb, w1, w3, w2, gf, *, final, nb, tf):
    n = x.shape[0]
    d_ff = w1.shape[2]
    sub = min(nb, MOE_TILE_ROWS)
    row = lambda i, e, j: (i, 0)
    whole = lambda i, e, j: (0, 0)
    return pl.pallas_call(
        functools.partial(_moe_kernel, final=final, nb=nb, sub=sub),
        grid=(n // nb, N_EXPERTS, d_ff // tf),
        in_specs=[pl.BlockSpec((nb, D_MODEL), row), pl.BlockSpec((1, D_MODEL), whole),
                  pl.BlockSpec((N_EXPERTS, nb), lambda i, e, j: (0, i)),
                  pl.BlockSpec((nb, N_EXPERTS), row), pl.BlockSpec((nb, N_EXPERTS), row),
                  pl.BlockSpec((1, D_MODEL, tf), lambda i, e, j: (e, 0, j)),
                  pl.BlockSpec((1, D_MODEL, tf), lambda i, e, j: (e, 0, j)),
                  pl.BlockSpec((1, tf, D_MODEL), lambda i, e, j: (e, j, 0)), pl.BlockSpec((1, D_MODEL), whole)],
        out_specs=pl.BlockSpec((nb, D_MODEL), row),
        out_shape=jax.ShapeDtypeStruct((n, D_MODEL), F32),
        scratch_shapes=[pltpu.VMEM((nb, D_MODEL), BF16), pltpu.VMEM((N_EXPERTS, nb), F32),
                        pltpu.VMEM((nb, N_EXPERTS), F32), pltpu.SMEM((N_EXPERTS,), I32),
                        pltpu.VMEM((nb + sub, D_MODEL), BF16), pltpu.VMEM((nb + sub, D_MODEL), F32)],
        compiler_params=_cparams("parallel", "arbitrary", "arbitrary"),
        name="moe",
    )(x, g, sel_t, sel, comb, w1, w3, w2, gf)


def _row_tile(n, cap):
    t = min(n, cap)
    assert n % t == 0, (n, cap)
    return t


def _ff_tile(d_ff):
    for parts in (2, 4, 11, 22):
        if d_ff % parts == 0 and (d_ff // parts) % LANES == 0:
            return d_ff // parts
    return d_ff


def _mixer(x, tabs, w, *, bsz, t_len, pos0, past, s0, layer):
    n = bsz * t_len
    qa, hb, iq, k_t, v_t, ik_t, k_hm, v_aug, ik_b, iw_t = _in_proj(x, w["norm_mix"], w["w_in"], layer, tabs,
                                                                   t_len, _row_tile(n, 512))
    l_real = t_len if past is None else past[0].shape[1] + t_len
    topk = min(TOPK_MAX, l_real // 4)
    tq = 2 * LANES if t_len % (2 * LANES) == 0 else LANES
    kb_rows = 256 if l_real % 256 == 0 else LANES
    l_pad = -(-l_real // (2 * kb_rows)) * (2 * kb_rows)
    t_pad = -(-t_len // tq) * tq
    if past is not None:
        pk, pv, pik = past
        one = jnp.ones(pv.shape[:2] + (A_KV_HEADS, 1), BF16)
        zero = jnp.zeros(pv.shape[:2] + (A_KV_HEADS, LANES - HEAD_DIM - 1), BF16)
        pv_aug = jnp.concatenate([pv.astype(BF16), one, zero], axis=-1)
        hm = lambda a: jnp.transpose(a, (2, 0, 1, 3))
        k_hm = jnp.concatenate([hm(pk.astype(BF16)), k_hm.reshape(A_KV_HEADS, bsz, t_len, HEAD_DIM)], axis=2)
        v_aug = jnp.concatenate([hm(pv_aug), v_aug.reshape(A_KV_HEADS, bsz, t_len, LANES)], axis=2)
        ik_b = jnp.concatenate([pik.astype(BF16), ik_b.reshape(bsz, t_len, HEAD_DIM)], axis=1)
    if l_pad != l_real or past is not None:
        padl = l_pad - l_real
        k_hm = jnp.pad(k_hm.reshape(A_KV_HEADS, bsz, l_real, HEAD_DIM), ((0, 0), (0, 0), (0, padl), (0, 0)))
        v_aug = jnp.pad(v_aug.reshape(A_KV_HEADS, bsz, l_real, LANES), ((0, 0), (0, 0), (0, padl), (0, 0)))
        ik_b = jnp.pad(ik_b.reshape(bsz, l_real, HEAD_DIM), ((0, 0), (0, padl), (0, 0)))
        k_hm = k_hm.reshape(A_KV_HEADS, bsz * l_pad, HEAD_DIM)
        v_aug = v_aug.reshape(A_KV_HEADS, bsz * l_pad, LANES)
        ik_b = ik_b.reshape(bsz * l_pad, HEAD_DIM)
    q_in, iq_in, iw_in = qa, iq, iw_t
    if t_pad != t_len:
        padq = lambda a: jnp.pad(a.reshape(bsz, t_len, -1),
                                 ((0, 0), (0, t_pad - t_len), (0, 0))).reshape(bsz * t_pad, -1)
        q_in, iq_in = padq(qa), padq(iq)
        iw_in = jnp.pad(iw_t.reshape(8, bsz, t_len), ((0, 0), (0, 0), (0, t_pad - t_len))).reshape(8, bsz * t_pad)
    oa = _dsa(q_in, iq_in, iw_in, k_hm, v_aug, ik_b, bsz=bsz, l_real=l_real, pos0=pos0, kb_rows=kb_rows,
              topk=topk, tq=tq)
    if t_pad != t_len:
        oa = oa.reshape(bsz, t_pad, A_WIDTH)[:, :t_len].reshape(n, A_WIDTH)
    ob, s_tiles = _hgrn(hb, w["hgrn_lb"], w["gnorm"], _state_to_tiles(s0), layer=layer, bsz=bsz, t_len=t_len,
                        tb=_row_tile(t_len, 256))
    return oa, ob, k_t, v_t, ik_t, _tiles_to_state(s_tiles)


def _from_time_minor(a, bsz, t_len, heads):
    if a.ndim == 2:
        a = a.reshape((-1, HEAD_DIM, bsz, t_len))
        a = jnp.transpose(a, (2, 3, 0, 1))
    else:
        a = jnp.moveaxis(a, -1, 1)
        a = a.reshape(bsz, t_len, -1, HEAD_DIM)
    return a if heads is not None else a.reshape(bsz, t_len, HEAD_DIM)


def kernel(x_prompt, x_sample, cache_k, cache_v, cache_idx_k, state_hgrn, cache_mem_k, cache_mem_v, mem_prompt,
           norm_mix, w_in, hgrn_lb, hgrn_gnorm, w_out, norm_mem, norm_memkv, w_mq, w_mk, w_mv, w_mo,
           norm_ffn, ffn_w1, ffn_w3, ffn_w2, moe_router, moe_w1, moe_w3, moe_w2, norm_final):
    depth = w_in.shape[0]
    bp, tp, _ = x_prompt.shape
    bs, ts, _ = x_sample.shape
    past = cache_k.shape[2]
    n_mem = mem_prompt.shape[1]
    tabs_p = _rope_tables(jnp.arange(tp, dtype=I32))
    tabs_s = tuple(jnp.tile(a, (bs, 1)) for a in _rope_tables(past + jnp.arange(ts, dtype=I32)))
    xp = x_prompt.reshape(bp * tp, D_MODEL)
    xs = x_sample.reshape(bs * ts, D_MODEL)
    mem = mem_prompt.reshape(bp * n_mem, D_MODEL)
    s0_p = jnp.zeros((bp, B_HEADS, HEAD_DIM, HEAD_DIM), F32)
    gfin = norm_final.reshape(1, D_MODEL)
    row2 = lambda a: a.reshape(1, -1)
    outs = {k: [] for k in ("kp", "vp", "ikp", "sp", "mkp", "mvp", "ks", "vs", "iks", "ss")}
    for l in range(depth):
        w = {"norm_mix": row2(norm_mix[l]), "w_in": w_in, "hgrn_lb": hgrn_lb.astype(F32),
             "gnorm": row2(jnp.tile(hgrn_gnorm[l], B_HEADS)),
             "w_out_a": w_out[l, :A_WIDTH].astype(BF16), "w_out_b": w_out[l, A_WIDTH:].astype(BF16)}
        oa_p, ob_p, kp, vp, ikp, sp = _mixer(xp, tabs_p, w, bsz=bp, t_len=tp, pos0=0, past=None, s0=s0_p, layer=l)
        oa_s, ob_s, ks, vs, iks, ss = _mixer(xs, tabs_s, w, bsz=bs, t_len=ts, pos0=past,
                                             past=(cache_k[l], cache_v[l], cache_idx_k[l]), s0=state_hgrn[l],
                                             layer=l)
        wkv = jnp.concatenate([w_mk[l], w_mv[l]], axis=1).astype(BF16)
        mkv = _norm_matmul(mem, row2(norm_memkv[l]), wkv, _row_tile(bp * n_mem, 256))
        mk_p = mkv[:, :D_MODEL].reshape(bp, n_mem, D_MODEL)
        mv_p = mkv[:, D_MODEL:].reshape(bp, n_mem, D_MODEL)
        mk_s = cache_mem_k[l].reshape(bs, n_mem, D_MODEL).astype(BF16)
        mv_s = cache_mem_v[l].reshape(bs, n_mem, D_MODEL).astype(BF16)
        wq, wo, gm = w_mq[l].astype(BF16), w_mo[l].astype(BF16), row2(norm_mem[l])
        final = l == depth - 1
        gf = row2(norm_ffn[l])
        tail = (w["w_out_a"], w["w_out_b"], gm, wq)
        if l % 2 == 0:
            dense = (gf, ffn_w1[l // 2].astype(BF16), ffn_w3[l // 2].astype(BF16), ffn_w2[l // 2].astype(BF16), gfin)
            xp = _postmix(oa_p, ob_p, xp, *tail, mk_p.astype(BF16), mv_p.astype(BF16), wo, dense,
                          rows_per_batch=tp, tm=_row_tile(tp, 512), final=final)
            xs = _postmix(oa_s, ob_s, xs, *tail, mk_s, mv_s, wo, dense,
                          rows_per_batch=ts, tm=_row_tile(bs * ts, 512), final=final)
        else:
            xp = _postmix(oa_p, ob_p, xp, *tail, mk_p.astype(BF16), mv_p.astype(BF16), wo,
                          rows_per_batch=tp, tm=_row_tile(tp, 512))
            xs = _postmix(oa_s, ob_s, xs, *tail, mk_s, mv_s, wo, rows_per_batch=ts, tm=_row_tile(bs * ts, 512))
            w1, w3, w2 = moe_w1[l // 2].astype(BF16), moe_w3[l // 2].astype(BF16), moe_w2[l // 2].astype(BF16)
            rt = moe_router[l // 2].T
            tf = _ff_tile(w1.shape[2])
            comb_p, sel_p = _router(xp, gf, rt, _row_tile(bp * tp, 512))
            comb_s, sel_s = _router(xs, gf, rt, _row_tile(bs * ts, 512))
            xp = _moe(xp, gf, sel_p, sel_p.T, comb_p.T, w1, w3, w2, gfin, final=final,
                      nb=_row_tile(bp * tp, 1024), tf=tf)
            xs = _moe(xs, gf, sel_s, sel_s.T, comb_s.T, w1, w3, w2, gfin, final=final,
                      nb=_row_tile(bs * ts, 1024), tf=tf)
        outs["kp"].append(_from_time_minor(kp, bp, tp, A_KV_HEADS))
        outs["vp"].append(_from_time_minor(vp, bp, tp, A_KV_HEADS))
        outs["ikp"].append(_from_time_minor(ikp, bp, tp, None)); outs["sp"].append(sp)
        outs["mkp"].append(mk_p.reshape(bp, n_mem, MEM_HEADS, MEM_HEAD_DIM))
        outs["mvp"].append(mv_p.reshape(bp, n_mem, MEM_HEADS, MEM_HEAD_DIM))
        outs["ks"].append(_from_time_minor(ks, bs, ts, A_KV_HEADS))
        outs["vs"].append(_from_time_minor(vs, bs, ts, A_KV_HEADS))
        outs["iks"].append(_from_time_minor(iks, bs, ts, None)); outs["ss"].append(ss)
    y_prompt = xp.reshape(bp, tp, D_MODEL)
    y_sample = xs.reshape(bs, ts, D_MODEL)
    st = lambda k: jnp.stack(outs[k])
    return (y_prompt, y_sample, st("kp"), st("vp"), st("ikp"), st("sp"), st("mkp"), st("mvp"),
            st("ks"), st("vs"), st("iks"), st("ss"))
```

```python
import functools

import numpy as np
import jax
import jax.numpy as jnp
from jax import lax
from jax.experimental import pallas as pl
from jax.experimental.pallas import tpu as pltpu

F32 = jnp.float32
BF16 = jnp.bfloat16
I32 = jnp.int32
I16 = jnp.int16

D_MODEL = 1024
CHUNK = 64
EPS = 1e-6
ROPE_THETA = 500000.0
HEAD_DIM = 64
ROT_DIM = HEAD_DIM // 4
A_HEADS = 8
A_KV_HEADS = 2
A_GROUP = A_HEADS // A_KV_HEADS
A_WIDTH = A_HEADS * HEAD_DIM
KV_WIDTH = A_KV_HEADS * HEAD_DIM
IDX_HEADS = 4
IDX_WIDTH = IDX_HEADS * HEAD_DIM
TOPK_MAX = 256
B_HEADS = 8
B_WIDTH = B_HEADS * HEAD_DIM
B_CHUNK = 16
MEM_HEADS = 4
MEM_HEAD_DIM = D_MODEL // MEM_HEADS
N_EXPERTS = 8
MOE_TILE_ROWS = 256
LANES = 128
HALF = 256
IN_SPLITS = (A_WIDTH, KV_WIDTH, KV_WIDTH, IDX_WIDTH, HEAD_DIM, IDX_HEADS, B_WIDTH, B_WIDTH, B_WIDTH, B_WIDTH)
IN_WIDTH = sum(IN_SPLITS)
IN_WIDTH_PAD = 3200
VMEM_LIMIT = 56 * 1024 * 1024
INT_MIN = -2147483648
NEG_BIG = -1e30
LOG2E = 1.4426950408889634

_NT = (((1,), (1,)), ((), ()))
_TN = (((0,), (0,)), ((), ()))


def _cparams(*sem):
    return pltpu.CompilerParams(dimension_semantics=sem, vmem_limit_bytes=VMEM_LIMIT)


def _rms(x, g):
    return x * lax.rsqrt(jnp.mean(x * x, axis=-1, keepdims=True) + EPS) * g


def _sigmoid(x):
    return 1.0 / (1.0 + jnp.exp(-x))


def _dot(a, b):
    return jnp.dot(a, b, preferred_element_type=F32)


def _split2(x):
    hi = x.astype(BF16)
    lo = (x - hi.astype(F32)).astype(BF16)
    return hi, lo


def _split3(x):
    hi = x.astype(BF16)
    r = x - hi.astype(F32)
    mid = r.astype(BF16)
    lo = (r - mid.astype(F32)).astype(BF16)
    return hi, mid, lo


def _rope128(xb, c, s1, s2):
    return xb * c + pltpu.roll(xb, LANES - ROT_DIM // 2, 1) * s1 + pltpu.roll(xb, ROT_DIM // 2, 1) * s2


def _inproj_kernel(x_ref, g_ref, wraw_ref, c_ref, s1_ref, s2_ref,
                   qa_ref, hb_ref, iq_ref, kt_ref, vt_ref, ikt_ref, khm_ref, vaug_ref, ikb_ref, iwt_ref, w_ref):
    @pl.when(pl.program_id(0) == 0)
    def _():
        src = np.cumsum([0] + list(IN_SPLITS))
        dst = 0
        for i in (0, 6, 7, 8, 9, 1, 2, 3):
            width = int(src[i + 1] - src[i])
            w_ref[:, dst:dst + width] = wraw_ref[0, :, int(src[i]):int(src[i + 1])].astype(BF16)
            dst += width
        tail = jnp.concatenate([wraw_ref[0, :, int(src[4]):int(src[6])],
                                jnp.zeros((D_MODEL, LANES - HEAD_DIM - IDX_HEADS), F32)], axis=1)
        w_ref[:, dst:dst + LANES] = tail.astype(BF16)

    xn = _rms(x_ref[...], g_ref[...]).astype(BF16)
    c, s1, s2 = c_ref[...], s1_ref[...], s2_ref[...]
    for j in range(A_WIDTH // HALF):
        h = _dot(xn, w_ref[:, j * HALF:(j + 1) * HALF])
        for i in range(HALF // LANES):
            lo = j * HALF + i * LANES
            qa_ref[:, lo:lo + LANES] = _rope128(h[:, i * LANES:(i + 1) * LANES], c, s1, s2)
    for j in range(4):
        lo = A_WIDTH + j * B_WIDTH
        hb_ref[:, j * B_WIDTH:(j + 1) * B_WIDTH] = _dot(xn, w_ref[:, lo:lo + B_WIDTH])
    base = A_WIDTH + 4 * B_WIDTH
    kv = _dot(xn, w_ref[:, base:base + HALF])
    k = _rope128(kv[:, :LANES], c, s1, s2)
    v = kv[:, LANES:]
    iq = _dot(xn, w_ref[:, base + HALF:base + 2 * HALF])
    iq_ref[:, 0:128] = _rope128(iq[:, :LANES], c, s1, s2)
    iq_ref[:, 128:256] = _rope128(iq[:, LANES:], c, s1, s2)
    lane = lax.broadcasted_iota(I32, c.shape, 1)
    first = lane < HEAD_DIM
    ikw = _rope128(_dot(xn, w_ref[:, base + 512:base + 640]),
                   jnp.where(first, c, 1.0), jnp.where(first, s1, 0.0), jnp.where(first, s2, 0.0))
    kt_ref[...] = k.T.reshape(kt_ref.shape)
    vt_ref[...] = v.T.reshape(vt_ref.shape)
    ikw_t = ikw.T
    ikt_ref[...] = ikw_t[:HEAD_DIM].reshape(ikt_ref.shape)
    iwt_ref[...] = ikw_t[HEAD_DIM:HEAD_DIM + 8]
    ikb_ref[...] = ikw[:, :HEAD_DIM].astype(BF16)
    ones_col = jnp.where(lane == HEAD_DIM, 1.0, 0.0)
    for j in range(A_KV_HEADS):
        khm_ref[j] = k[:, j * HEAD_DIM:(j + 1) * HEAD_DIM].astype(BF16)
        vj = v if j == 0 else pltpu.roll(v, LANES - j * HEAD_DIM, 1)
        vaug_ref[j] = jnp.where(first, vj, ones_col).astype(BF16)


def _in_proj(x, g, w_all, layer, tabs, t_len, tm):
    n = x.shape[0]
    assert tabs[0].shape[0] % tm == 0, (tabs[0].shape, tm)
    nt = tabs[0].shape[0] // tm
    per_batch = t_len % tm == 0
    ntb = t_len // tm if per_batch else 1
    row = lambda i: (i, 0)
    tab = lambda i: (i % nt, 0)
    whole = lambda i: (0, 0)
    tmin = lambda i: (0, i)
    hmaj = lambda i: (0, i, 0)
    if per_batch:
        kv_spec = pl.BlockSpec((1, A_KV_HEADS, HEAD_DIM, tm), lambda i: (i // ntb, 0, 0, i % ntb))
        ik_spec = pl.BlockSpec((1, HEAD_DIM, tm), lambda i: (i // ntb, 0, i % ntb))
        kv_shape = jax.ShapeDtypeStruct((n // t_len, A_KV_HEADS, HEAD_DIM, t_len), F32)
        ik_shape = jax.ShapeDtypeStruct((n // t_len, HEAD_DIM, t_len), F32)
    else:
        kv_spec, ik_spec = pl.BlockSpec((KV_WIDTH, tm), tmin), pl.BlockSpec((HEAD_DIM, tm), tmin)
        kv_shape, ik_shape = jax.ShapeDtypeStruct((KV_WIDTH, n), F32), jax.ShapeDtypeStruct((HEAD_DIM, n), F32)
    return pl.pallas_call(
        _inproj_kernel,
        grid=(n // tm,),
        in_specs=[pl.BlockSpec((tm, D_MODEL), row), pl.BlockSpec((1, D_MODEL), whole),
                  pl.BlockSpec((1, D_MODEL, IN_WIDTH), lambda i: (layer, 0, 0), pipeline_mode=pl.Buffered(1)),
                  pl.BlockSpec((tm, LANES), tab), pl.BlockSpec((tm, LANES), tab), pl.BlockSpec((tm, LANES), tab)],
        out_specs=[pl.BlockSpec((tm, A_WIDTH), row), pl.BlockSpec((tm, 4 * B_WIDTH), row),
                   pl.BlockSpec((tm, IDX_WIDTH), row),
                   kv_spec, kv_spec, ik_spec,
                   pl.BlockSpec((A_KV_HEADS, tm, HEAD_DIM), hmaj), pl.BlockSpec((A_KV_HEADS, tm, LANES), hmaj),
                   pl.BlockSpec((tm, HEAD_DIM), row), pl.BlockSpec((8, tm), tmin)],
        out_shape=[jax.ShapeDtypeStruct((n, A_WIDTH), F32), jax.ShapeDtypeStruct((n, 4 * B_WIDTH), F32),
                   jax.ShapeDtypeStruct((n, IDX_WIDTH), F32),
                   kv_shape, kv_shape, ik_shape,
                   jax.ShapeDtypeStruct((A_KV_HEADS, n, HEAD_DIM), BF16),
                   jax.ShapeDtypeStruct((A_KV_HEADS, n, LANES), BF16),
                   jax.ShapeDtypeStruct((n, HEAD_DIM), BF16), jax.ShapeDtypeStruct((8, n), F32)],
        scratch_shapes=[pltpu.VMEM((D_MODEL, IN_WIDTH_PAD), BF16)],
        compiler_params=_cparams("arbitrary"),
        name="in_proj",
    )(x, g, w_all, *tabs)


def _rope_tables(pos):
    half = ROT_DIM // 2
    inv_freq = 1.0 / (ROPE_THETA ** (jnp.arange(half, dtype=F32) * (2.0 / ROT_DIM)))
    ang = pos.astype(F32)[:, None] * inv_freq[None, :]
    cos, sin = jnp.cos(ang), jnp.sin(ang)
    t = pos.shape[0]
    pad = HEAD_DIM - ROT_DIM
    c = jnp.concatenate([cos, cos, jnp.ones((t, pad), F32)], axis=1)
    s1 = jnp.concatenate([-sin, jnp.zeros((t, half + pad), F32)], axis=1)
    s2 = jnp.concatenate([jnp.zeros((t, half), F32), sin, jnp.zeros((t, pad), F32)], axis=1)
    rep = LANES // HEAD_DIM
    return tuple(jnp.tile(a, (1, rep)) for a in (c, s1, s2))


def _dsa_kernel(q_ref, iq_ref, iw_ref, k_ref, v_ref, ik_ref, o_ref, key_ref, hi_ref, lo_ref, acc_ref, m_ref,
                s0_ref, s1_ref, alpha_ref, p_ref, bias_ref, mt0_ref, mt1_ref, *, l_real, pos0, kb_rows, topk, tq):
    i = pl.program_id(1)
    qpos0 = pos0 + i * tq
    kend = jnp.minimum(((qpos0 + tq - 1) // CHUNK + 1) * CHUNK, l_real)
    cb_rows = 2 * kb_rows
    nkc = (kend + cb_rows - 1) // cb_rows
    nkb = 2 * nkc
    qchunk = (qpos0 + lax.broadcasted_iota(I32, (1, tq), 1)) // CHUNK
    row_iota = lax.broadcasted_iota(I32, (kb_rows, tq), 0)

    iq = iq_ref[...].astype(BF16)
    iq4 = jnp.concatenate([iq[:, h * HEAD_DIM:(h + 1) * HEAD_DIM] for h in range(IDX_HEADS)], axis=0)
    iw = iw_ref[...] * (IDX_HEADS ** -0.5 * HEAD_DIM ** -0.5)

    def score_body(kb, carry):
        r0 = pl.multiple_of(kb * kb_rows, kb_rows)
        ikb = ik_ref[pl.ds(r0, kb_rows), :]
        lg4 = lax.dot_general(ikb, iq4, _NT, preferred_element_type=F32)
        acc = jnp.zeros((kb_rows, tq), F32)
        for h in range(IDX_HEADS):
            acc = acc + iw[h:h + 1, :] * jnp.maximum(lg4[:, h * tq:(h + 1) * tq], 0.0)
        bits = pltpu.bitcast(acc + 0.0, I32)
        key = bits ^ ((bits >> 31) & 0x7FFFFFFF)
        kpos = r0 + row_iota
        adm = (kpos // CHUNK <= qchunk) & (kpos < l_real)
        key = jnp.where(adm, key, INT_MIN)
        key_ref[pl.ds(r0, kb_rows), :] = key
        hi_ref[pl.ds(r0, kb_rows), :] = (key >> 16).astype(I16)
        return carry

    lax.fori_loop(0, nkb, score_body, 0)

    def count(pred):
        def body(kc, c):
            r0 = pl.multiple_of(kc * cb_rows, cb_rows)
            hit = pred(key_ref[pl.ds(r0, cb_rows), :], r0)
            return c + jnp.where(hit, 1, 0).reshape(cb_rows // 8, 8, tq).sum(axis=0)
        c = lax.fori_loop(0, nkc, body, jnp.zeros((8, tq), I32))
        return c.sum(axis=0, keepdims=True)

    def count16(ref, pred):
        one, zero = jnp.ones((), BF16), jnp.zeros((), BF16)

        def body(kc, c):
            r0 = pl.multiple_of(kc * cb_rows, cb_rows)
            w = jnp.where(pred(ref[pl.ds(r0, cb_rows), :]), one, zero).reshape(cb_rows // 16, 16, tq)
            parts = [w[r] for r in range(cb_rows // 16)]
            while len(parts) > 1:
                parts = [a + b for a, b in zip(parts[::2], parts[1::2])] + parts[len(parts) & ~1:]
            return c + parts[0]
        c = lax.fori_loop(0, nkc, body, jnp.zeros((16, tq), BF16))
        return c.astype(F32).sum(axis=0, keepdims=True).astype(I32)

    def bisect16(ref, kth):
        c0 = count16(ref, lambda blk: blk >= jnp.zeros((), I16))
        start = jnp.where(c0 >= kth, 0, -32768).astype(I32)

        def bit_body(it, prefix):
            cand = prefix | lax.shift_left(jnp.int32(1), 14 - it)
            c = count16(ref, lambda blk: blk >= cand.astype(I16))
            return jnp.where(c >= kth, cand, prefix)

        return lax.fori_loop(0, 15, bit_body, start)

    tau_hi = bisect16(hi_ref, topk)
    tau_hi16 = tau_hi.astype(I16)
    kth_lo = topk - count16(hi_ref, lambda blk: blk > tau_hi16)

    def lo_body(kc, carry):
        r0 = pl.multiple_of(kc * cb_rows, cb_rows)
        lo = ((key_ref[pl.ds(r0, cb_rows), :] & 0xFFFF) - 32768).astype(I16)
        lo_ref[pl.ds(r0, cb_rows), :] = jnp.where(hi_ref[pl.ds(r0, cb_rows), :] == tau_hi16, lo,
                                                 jnp.full((), -32768, I16))
        return carry

    lax.fori_loop(0, nkc, lo_body, 0)
    tau_lo = bisect16(lo_ref, kth_lo)
    tau = lax.shift_left(tau_hi, 16) | (tau_lo + 32768)
    n_ge = count(lambda blk, r0: blk >= tau)

    @pl.when(jnp.max(jnp.where((n_ge > topk) & (tau != INT_MIN), 1, 0)) > 0)
    def _():
        need = (topk - count(lambda blk, r0: blk > tau)).astype(F32)
        ri = lax.broadcasted_iota(I32, (kb_rows, kb_rows), 0)
        ci = lax.broadcasted_iota(I32, (kb_rows, kb_rows), 1)
        upto = jnp.where(ci <= ri, 1.0, 0.0).astype(BF16)

        def strike(kc, seen):
            for half in range(2):
                rows = pl.ds(pl.multiple_of(kc * cb_rows + half * kb_rows, kb_rows), kb_rows)
                key = key_ref[rows, :]
                tie = key == tau
                rank = _dot(upto, jnp.where(tie, 1.0, 0.0).astype(BF16)) + seen
                key_ref[rows, :] = jnp.where(tie & (rank > need), INT_MIN, key)
                seen = rank[kb_rows - 1:kb_rows, :]
            return seen

        lax.fori_loop(0, nkc, strike, jnp.zeros((1, tq), F32))

    lane_rep = kb_rows // LANES
    tau_c = jnp.tile(jnp.broadcast_to(tau, (LANES, tq)).T, (1, lane_rep))
    acc_ref[...] = jnp.zeros(acc_ref.shape, F32)
    m_ref[...] = jnp.full(m_ref.shape, NEG_BIG, F32)
    qs = (q_ref[...] * (HEAD_DIM ** -0.5 * LOG2E)).astype(BF16)
    q4 = [jnp.concatenate([qs[:, (j * A_GROUP + g) * HEAD_DIM:(j * A_GROUP + g + 1) * HEAD_DIM]
                           for g in range(A_GROUP)], axis=0) for j in range(A_KV_HEADS)]

    def qk_scores(kb, s_slot, mt_slot):
        r0 = pl.multiple_of(jnp.minimum(kb, nkb - 1) * kb_rows, kb_rows)
        key_t = key_ref[pl.ds(r0, kb_rows), :].T
        bias_ref[...] = jnp.where((key_t >= tau_c) & (key_t != INT_MIN), 0.0, NEG_BIG)
        for j in range(A_KV_HEADS):
            s = lax.dot_general(q4[j], k_ref[j, pl.ds(r0, kb_rows), :], _NT, preferred_element_type=F32)
            for g in range(A_GROUP):
                rows = slice(g * tq, (g + 1) * tq)
                sm = s[rows, :] + bias_ref[...]
                s_slot[j, rows, :] = sm
                mt_slot[j, g] = jnp.broadcast_to(jnp.max(sm, axis=1, keepdims=True), (tq, LANES))

    def softmax_pv(kb, s_slot, mt_slot):
        r0 = pl.multiple_of(kb * kb_rows, kb_rows)
        for j in range(A_KV_HEADS):
            for g in range(A_GROUP):
                rows = slice(g * tq, (g + 1) * tq)
                m_old = m_ref[j, g]
                m_new = jnp.maximum(m_old, mt_slot[j, g])
                alpha_ref[j, g] = jnp.exp2(m_old - m_new)
                m_ref[j, g] = m_new
                p_ref[j, rows, :] = jnp.exp2(s_slot[j, rows, :] - jnp.tile(m_new, (1, lane_rep))).astype(BF16)
            pv = _dot(p_ref[j], v_ref[j, pl.ds(r0, kb_rows), :])
            acc_ref[j] = alpha_ref[j] * acc_ref[j] + pv.reshape(A_GROUP, tq, LANES)

    qk_scores(0, s0_ref, mt0_ref)

    def pair_body(i, carry):
        kb = 2 * i
        qk_scores(kb + 1, s1_ref, mt1_ref)
        softmax_pv(kb, s0_ref, mt0_ref)
        qk_scores(kb + 2, s0_ref, mt0_ref)
        softmax_pv(kb + 1, s1_ref, mt1_ref)
        return carry

    lax.fori_loop(0, nkc, pair_body, 0)

    outs = []
    for j in range(A_KV_HEADS):
        acc = acc_ref[j]
        for g in range(A_GROUP):
            outs.append(acc[g, :, :HEAD_DIM] * (1.0 / acc[g, :, HEAD_DIM:HEAD_DIM + 1]))
    o_ref[...] = jnp.concatenate(outs, axis=1)


def _dsa(q, iq, iw_t, k_hm, v_aug, ik, *, bsz, l_real, pos0, kb_rows, topk, tq):
    t_len = q.shape[0] // bsz
    l_pad = ik.shape[0] // bsz
    nq = t_len // tq
    assert l_pad // 16 <= 256, "packed hit counts are summed in bf16, exact only up to 256 per slot"
    kern = functools.partial(_dsa_kernel, l_real=l_real, pos0=pos0, kb_rows=kb_rows, topk=topk, tq=tq)
    qrow = lambda b, i: (b * nq + i, 0)
    return pl.pallas_call(
        kern,
        grid=(bsz, nq),
        in_specs=[pl.BlockSpec((tq, A_WIDTH), qrow), pl.BlockSpec((tq, IDX_WIDTH), qrow),
                  pl.BlockSpec((8, tq), lambda b, i: (0, b * nq + i)),
                  pl.BlockSpec((A_KV_HEADS, l_pad, HEAD_DIM), lambda b, i: (0, b, 0)),
                  pl.BlockSpec((A_KV_HEADS, l_pad, LANES), lambda b, i: (0, b, 0)),
                  pl.BlockSpec((l_pad, HEAD_DIM), lambda b, i: (b, 0))],
        out_specs=pl.BlockSpec((tq, A_WIDTH), qrow),
        out_shape=jax.ShapeDtypeStruct((bsz * t_len, A_WIDTH), F32),
        scratch_shapes=[pltpu.VMEM((l_pad, tq), I32), pltpu.VMEM((l_pad, tq), I16), pltpu.VMEM((l_pad, tq), I16),
                        pltpu.VMEM((A_KV_HEADS, A_GROUP, tq, LANES), F32),
                        pltpu.VMEM((A_KV_HEADS, A_GROUP, tq, LANES), F32),
                        pltpu.VMEM((A_KV_HEADS, A_GROUP * tq, kb_rows), F32),
                        pltpu.VMEM((A_KV_HEADS, A_GROUP * tq, kb_rows), F32),
                        pltpu.VMEM((A_KV_HEADS, A_GROUP, tq, LANES), F32),
                        pltpu.VMEM((A_KV_HEADS, A_GROUP * tq, kb_rows), BF16),
                        pltpu.VMEM((tq, kb_rows), F32),
                        pltpu.VMEM((A_KV_HEADS, A_GROUP, tq, LANES), F32),
                        pltpu.VMEM((A_KV_HEADS, A_GROUP, tq, LANES), F32)],
        compiler_params=_cparams("parallel", "arbitrary"),
        name="dsa",
    )(q, iq, iw_t, k_hm, v_aug, ik)


def _hgrn_kernel(bq_ref, bf_ref, bi_ref, bg_ref, lb_ref, gn_ref, s0_ref, o_ref, sfin_ref,
                 st_ref, q_s, kk_s, b_s, qe_s, kd_s, dec_s, o_s, e_s, *, layer, tb, nseq):
    tstep = pl.program_id(1)
    nchunk = tb // B_CHUNK

    @pl.when(tstep == 0)
    def _():
        st_ref[...] = s0_ref[...]

    lbr = lb_ref[...]
    lbe = jnp.exp(lbr - jnp.max(lbr, axis=0, keepdims=True))
    lbs = lbe / jnp.sum(lbe, axis=0, keepdims=True)
    cum = lbs[0:1, :]
    first = cum
    for r in range(1, layer + 1):
        cum = cum + lbs[r:r + 1, :]
    lb = cum - first

    ri = lax.broadcasted_iota(I32, (tb, tb), 0)
    ci = lax.broadcasted_iota(I32, (tb, tb), 1)
    same = (ri // B_CHUNK) == (ci // B_CHUNK)
    tri = jnp.where(same & (ci <= ri), 1.0, 0.0).astype(BF16)
    ones = jnp.where(same, 1.0, 0.0).astype(BF16)
    for u in range(nseq):
        f = lb + (1.0 - lb) * _sigmoid(bf_ref[u])
        bq = bq_ref[u]
        q = bq * _sigmoid(bq) * (HEAD_DIM ** -0.5)
        g3 = _split3(jnp.log(f) * LOG2E)
        b = _dot(tri, g3[0]) + _dot(tri, g3[1]) + _dot(tri, g3[2])
        blast = _dot(ones, g3[0]) + _dot(ones, g3[1]) + _dot(ones, g3[2])
        q_s[u] = q
        kk_s[u] = 1.0 - f
        b_s[u] = b
        qe_s[u] = q * jnp.exp2(b)
        kd_s[u] = (1.0 - f) * jnp.exp2(blast - b)
        dec_s[u] = jnp.exp2(blast)

    hi = lax.broadcasted_iota(I32, (HALF, HALF), 0) // HEAD_DIM
    hj = lax.broadcasted_iota(I32, (HALF, HALF), 1) // HEAD_DIM
    bd_mask = hi == hj
    bd = jnp.where(bd_mask, 1.0, 0.0).astype(BF16)
    t_iota = lax.broadcasted_iota(I32, (B_CHUNK, B_WIDTH), 0)

    def seg_sum(x, two_pass):
        out = []
        for half in range(2):
            xs = x[:, half * HALF:(half + 1) * HALF]
            if two_pass:
                xh, xl = _split2(xs)
                out.append(_dot(xh, bd) + _dot(xl, bd))
            else:
                out.append(_dot(xs.astype(BF16), bd))
        return jnp.concatenate(out, axis=1)

    def chunk_body(c, carry):
        r0 = pl.multiple_of(c * B_CHUNK, B_CHUNK)
        rows = pl.ds(r0, B_CHUNK)
        for u in range(nseq):
            qc, kc, bc, vc = q_s[u, rows, :], kk_s[u, rows, :], b_s[u, rows, :], bi_ref[u, rows, :]
            qe, kd = qe_s[u, rows, :].astype(BF16), kd_s[u, rows, :].astype(BF16)
            dec = dec_s[u, pl.ds(r0, 1), :]
            for s in range(B_CHUNK):
                ex = jnp.exp2(jnp.where(t_iota >= s, bc - bc[s:s + 1, :], NEG_BIG))
                e_s[u, s * B_CHUNK:(s + 1) * B_CHUNK, :] = qc * kc[s:s + 1, :] * ex
            aexp = seg_sum(e_s[u], False)
            o = jnp.zeros((B_CHUNK, B_WIDTH), F32)
            for s in range(B_CHUNK):
                o = o + aexp[s * B_CHUNK:(s + 1) * B_CHUNK, :] * vc[s:s + 1, :]
            inter = []
            vb = vc.astype(BF16)
            for half in range(2):
                cols = slice(half * HALF, (half + 1) * HALF)
                st = st_ref[u, half]
                inter.append(lax.dot_general(qe[:, cols], st.astype(BF16), _NT, preferred_element_type=F32))
                upd = lax.dot_general(vb[:, cols], kd[:, cols], _TN, preferred_element_type=F32)
                st_ref[u, half] = st * dec[:, cols] + jnp.where(bd_mask, upd, 0.0)
            o_s[u, rows, :] = o + jnp.concatenate(inter, axis=1)
        return carry

    lax.fori_loop(0, nchunk, chunk_body, 0)

    for u in range(nseq):
        o = o_s[u]
        ms = seg_sum(o * o, True) * (1.0 / HEAD_DIM)
        bg = bg_ref[u]
        o_ref[u] = o * lax.rsqrt(ms + EPS) * gn_ref[...] * (bg * _sigmoid(bg))

    @pl.when(tstep == pl.num_programs(1) - 1)
    def _():
        sfin_ref[...] = st_ref[...]


def _hgrn(hb, lb_raw, gn_tiled, s0_bd, *, layer, bsz, t_len, tb):
    nt = t_len // tb
    nseq = 4 if bsz % 4 == 0 else (2 if bsz % 2 == 0 else 1)
    hb3 = hb.reshape(bsz, t_len, 4 * B_WIDTH)
    col = lambda j: (lambda b, t: (b, t, j))
    scr = lambda r: pltpu.VMEM((nseq, r, B_WIDTH), F32)
    kern = functools.partial(_hgrn_kernel, layer=layer, tb=tb, nseq=nseq)
    blk = (nseq, tb, B_WIDTH)
    out, s_fin = pl.pallas_call(
        kern,
        grid=(bsz // nseq, nt),
        in_specs=[pl.BlockSpec(blk, col(0)), pl.BlockSpec(blk, col(1)), pl.BlockSpec(blk, col(2)),
                  pl.BlockSpec(blk, col(3)),
                  pl.BlockSpec(lb_raw.shape, lambda b, t: (0, 0)),
                  pl.BlockSpec((1, B_WIDTH), lambda b, t: (0, 0)),
                  pl.BlockSpec((nseq, 2, HALF, HALF), lambda b, t: (b, 0, 0, 0))],
        out_specs=[pl.BlockSpec(blk, lambda b, t: (b, t, 0)),
                   pl.BlockSpec((nseq, 2, HALF, HALF), lambda b, t: (b, 0, 0, 0))],
        out_shape=[jax.ShapeDtypeStruct((bsz, t_len, B_WIDTH), F32),
                   jax.ShapeDtypeStruct((bsz, 2, HALF, HALF), F32)],
        scratch_shapes=[pltpu.VMEM((nseq, 2, HALF, HALF), F32), scr(tb), scr(tb), scr(tb), scr(tb), scr(tb),
                        scr(tb), scr(tb), scr(B_CHUNK * B_CHUNK)],
        compiler_params=_cparams("parallel", "arbitrary"),
        name="hgrn2",
    )(hb3, hb3, hb3, hb3, lb_raw, gn_tiled, s0_bd)
    return out.reshape(bsz * t_len, B_WIDTH), s_fin


def _state_to_tiles(s):
    bsz = s.shape[0]
    st = jnp.swapaxes(s, -1, -2).reshape(bsz, 2, B_HEADS // 2, HEAD_DIM, HEAD_DIM)
    eye = jnp.eye(B_HEADS // 2, dtype=s.dtype)
    t = st[:, :, :, :, None, :] * eye[None, None, :, None, :, None]
    return t.reshape(bsz, 2, HALF, HALF)


def _tiles_to_state(t):
    bsz = t.shape[0]
    t6 = t.reshape(bsz, 2, B_HEADS // 2, HEAD_DIM, B_HEADS // 2, HEAD_DIM)
    diag = jnp.stack([t6[:, :, h, :, h, :] for h in range(B_HEADS // 2)], axis=2)
    return jnp.swapaxes(diag.reshape(bsz, B_HEADS, HEAD_DIM, HEAD_DIM), -1, -2)


def _normmm_kernel(x_ref, g_ref, w_ref, o_ref):
    o_ref[...] = _dot(_rms(x_ref[...], g_ref[...]).astype(BF16), w_ref[...])


def _norm_matmul(x, g, w, tm):
    n, m = x.shape[0], w.shape[1]
    return pl.pallas_call(
        _normmm_kernel,
        grid=(n // tm,),
        in_specs=[pl.BlockSpec((tm, D_MODEL), lambda i: (i, 0)), pl.BlockSpec((1, D_MODEL), lambda i: (0, 0)),
                  pl.BlockSpec((D_MODEL, m), lambda i: (0, 0))],
        out_specs=pl.BlockSpec((tm, m), lambda i: (i, 0)),
        out_shape=jax.ShapeDtypeStruct((n, m), F32),
        compiler_params=_cparams("parallel"),
        name="norm_matmul",
    )(x, g, w)


def _swiglu_mid(xn, w1, w3):
    a = _dot(xn, w1)
    return (a * _sigmoid(a) * _dot(xn, w3)).astype(BF16)


def _postmix_kernel(oa_ref, ob_ref, x_ref, wa_ref, wb_ref, gm_ref, wq_ref, mk_ref, mv_ref, wo_ref, *rest,
                    with_ffn, final, chunks):
    o_ref = rest[-1]
    x = (x_ref[...] + _dot(oa_ref[...].astype(BF16), wa_ref[...]) + _dot(ob_ref[...].astype(BF16), wb_ref[...]))
    q = _dot(_rms(x, gm_ref[...]).astype(BF16), wq_ref[...]).astype(BF16)
    nbt = mk_ref.shape[0]
    rpb = x.shape[0] // nbt
    att = []
    for bb in range(nbt):
        rows = slice(bb * rpb, (bb + 1) * rpb)
        heads = []
        for h in range(MEM_HEADS):
            cols = slice(h * MEM_HEAD_DIM, (h + 1) * MEM_HEAD_DIM)
            s = lax.dot_general(q[rows, cols], mk_ref[bb, :, cols], _NT,
                                preferred_element_type=F32) * (MEM_HEAD_DIM ** -0.5)
            p = jnp.exp(s - jnp.max(s, axis=-1, keepdims=True))
            p = p * (1.0 / jnp.sum(p, axis=-1, keepdims=True))
            heads.append(_dot(p.astype(BF16), mv_ref[bb, :, cols]).astype(BF16))
        att.append(jnp.concatenate(heads, axis=1))
    y = x + _dot(att[0] if nbt == 1 else jnp.concatenate(att, axis=0), wo_ref[...])
    if with_ffn:
        gffn_ref, w1_ref, w3_ref, w2_ref, gf_ref = rest[:5]
        yn = _rms(y, gffn_ref[...]).astype(BF16)
        lo = 0
        for width in chunks:
            cols = slice(lo, lo + width)
            y = y + _dot(_swiglu_mid(yn, w1_ref[:, cols], w3_ref[:, cols]), w2_ref[cols, :])
            lo += width
        if final:
            y = _rms(y, gf_ref[...])
    o_ref[...] = y


def _postmix(oa, ob, x, wa, wb, gm, wq, mk, mv, wo, ffn=None, *, rows_per_batch, tm, final=False):
    n = x.shape[0]
    n_mem = mk.shape[1]
    row = lambda i: (i, 0)
    whole = lambda i: (0, 0)
    if tm <= rows_per_batch:
        per = rows_per_batch // tm
        mem_spec = pl.BlockSpec((1, n_mem, D_MODEL), lambda i: (i // per, 0, 0))
    else:
        mem_spec = pl.BlockSpec((tm // rows_per_batch, n_mem, D_MODEL), lambda i: (i, 0, 0))
    const = lambda shape: pl.BlockSpec(shape, whole, pipeline_mode=pl.Buffered(1))
    in_specs = [pl.BlockSpec((tm, A_WIDTH), row), pl.BlockSpec((tm, B_WIDTH), row), pl.BlockSpec((tm, D_MODEL), row),
                const((A_WIDTH, D_MODEL)), const((B_WIDTH, D_MODEL)), const((1, D_MODEL)),
                const((D_MODEL, D_MODEL)),
                mem_spec, mem_spec, const((D_MODEL, D_MODEL))]
    args = [oa, ob, x, wa, wb, gm, wq, mk, mv, wo]
    fuse = [False, False, False, True, True, False, True, True, True, True]
    chunks = ()
    if ffn is not None:
        d_ff = ffn[1].shape[1]
        step = 4 * HALF
        chunks = tuple(min(step, d_ff - lo) for lo in range(0, d_ff, step))
        in_specs += [const((1, D_MODEL)), const((D_MODEL, d_ff)), const((D_MODEL, d_ff)), const((d_ff, D_MODEL)),
                     const((1, D_MODEL))]
        args += list(ffn)
        fuse += [False, True, True, True, False]
    return pl.pallas_call(
        functools.partial(_postmix_kernel, with_ffn=ffn is not None, final=final, chunks=chunks),
        grid=(n // tm,),
        in_specs=in_specs,
        out_specs=pl.BlockSpec((tm, D_MODEL), row),
        out_shape=jax.ShapeDtypeStruct((n, D_MODEL), F32),
        compiler_params=pltpu.CompilerParams(dimension_semantics=("parallel",), vmem_limit_bytes=VMEM_LIMIT,
                                             allow_input_fusion=fuse),
        name="post_mix",
    )(*args)


def _router_kernel(x_ref, g_ref, rt_ref, comb_ref, sel_ref):
    xh, xl = _split2(_rms(x_ref[...], g_ref[...]))
    rh, rl = _split2(rt_ref[...])
    nt_dot = lambda a, b: lax.dot_general(a, b, _NT, preferred_element_type=F32)
    logits = nt_dot(rh, xh) + (nt_dot(rh, xl) + nt_dot(rl, xh))
    e_idx = lax.broadcasted_iota(I32, logits.shape, 0)
    m1 = jnp.max(logits, axis=0, keepdims=True)
    i1 = jnp.min(jnp.where(logits == m1, e_idx, N_EXPERTS), axis=0, keepdims=True)
    rest = jnp.where(e_idx == i1, -jnp.inf, logits)
    m2 = jnp.max(rest, axis=0, keepdims=True)
    i2 = jnp.min(jnp.where(rest == m2, e_idx, N_EXPERTS), axis=0, keepdims=True)
    e2 = jnp.exp(m2 - m1)
    g1 = 1.0 / (1.0 + e2)
    comb_ref[...] = jnp.where(e_idx == i1, g1, 0.0) + jnp.where(e_idx == i2, e2 * g1, 0.0)
    sel_ref[...] = jnp.where((e_idx == i1) | (e_idx == i2), 1.0, 0.0)


def _router(x, g, rt, tm):
    n = x.shape[0]
    return pl.pallas_call(
        _router_kernel,
        grid=(n // tm,),
        in_specs=[pl.BlockSpec((tm, D_MODEL), lambda i: (i, 0)), pl.BlockSpec((1, D_MODEL), lambda i: (0, 0)),
                  pl.BlockSpec((N_EXPERTS, D_MODEL), lambda i: (0, 0))],
        out_specs=[pl.BlockSpec((N_EXPERTS, tm), lambda i: (0, i)), pl.BlockSpec((N_EXPERTS, tm), lambda i: (0, i))],
        out_shape=[jax.ShapeDtypeStruct((N_EXPERTS, n), F32), jax.ShapeDtypeStruct((N_EXPERTS, n), F32)],
        compiler_params=_cparams("parallel"),
        name="moe_router",
    )(x, g, rt)


def _moe_kernel(x_ref, g_ref, selt_ref, sel_ref, comb_ref, w1_ref, w3_ref, w2_ref, gf_ref, o_ref,
                xn_ref, posr_ref, posc_ref, cnt_ref, xg_ref, yg_ref, *, final, nb, sub):
    e = pl.program_id(1)
    j = pl.program_id(2)
    last_j = pl.num_programs(2) - 1

    @pl.when((e == 0) & (j == 0))
    def _():
        x = x_ref[...]
        xn_ref[...] = _rms(x, g_ref[...]).astype(BF16)
        o_ref[...] = x
        cc = min(nb, 256)
        ri = lax.broadcasted_iota(I32, (cc, cc), 0)
        ci = lax.broadcasted_iota(I32, (cc, cc), 1)
        before_r = jnp.where(ri < ci, 1.0, 0.0).astype(BF16)
        before_c = jnp.where(ci < ri, 1.0, 0.0).astype(BF16)
        run_r = jnp.zeros((N_EXPERTS, 1), F32)
        run_c = jnp.zeros((1, N_EXPERTS), F32)
        for c in range(nb // cc):
            cols = slice(c * cc, (c + 1) * cc)
            mr = selt_ref[:, cols]
            posr_ref[:, cols] = jnp.where(mr > 0.0, _dot(mr.astype(BF16), before_r) + run_r, -1.0)
            run_r = run_r + jnp.sum(mr, axis=1, keepdims=True)
            mc = sel_ref[cols, :]
            posc_ref[cols, :] = jnp.where(mc > 0.0, _dot(before_c, mc.astype(BF16)) + run_c, -1.0)
            run_c = run_c + jnp.sum(mc, axis=0, keepdims=True)
        for k in range(N_EXPERTS):
            cnt_ref[k] = run_c[0, k].astype(I32)

    cnt = cnt_ref[e]
    nfull = cnt // sub
    rem = cnt - nfull * sub
    tail_sizes = [m for m in (64, 128) if m < sub] + [sub]

    def for_tiles(fn):
        def body(s, carry):
            fn(pl.multiple_of(s * sub, sub), sub)
            return carry
        lax.fori_loop(0, nfull, body, 0)
        r0 = pl.multiple_of(nfull * sub, sub)
        lo = 0
        for m in tail_sizes:
            @pl.when((rem > lo) & (rem <= m))
            def _():
                fn(r0, m)
            lo = m

    def gather_matrix(r0, m):
        slot_r = posr_ref[pl.ds(e, 1), :]
        want = (r0 + lax.broadcasted_iota(I32, (m, nb), 0)).astype(F32)
        return jnp.where(slot_r == want, 1.0, 0.0).astype(BF16)

    def gather(r0, m):
        xg_ref[pl.ds(r0, m), :] = _dot(gather_matrix(r0, m), xn_ref[...]).astype(BF16)

    def ffn(r0, m):
        rows = pl.ds(r0, m)
        part = _dot(_swiglu_mid(xg_ref[rows, :], w1_ref[0], w3_ref[0]), w2_ref[0])

        @pl.when(j == 0)
        def _():
            yg_ref[rows, :] = part

        @pl.when(j > 0)
        def _():
            yg_ref[rows, :] += part

    @pl.when(j == 0)
    def _():
        for_tiles(gather)

    for_tiles(ffn)

    @pl.when(j == last_j)
    def _():
        lane8 = lax.broadcasted_iota(I32, (nb, N_EXPERTS), 1)
        slot_c = jnp.sum(jnp.where(lane8 == e, posc_ref[...], 0.0), axis=1, keepdims=True)
        gate_c = jnp.sum(jnp.where(lane8 == e, comb_ref[...], 0.0), axis=1, keepdims=True)
        gate3 = jnp.concatenate(_split3(gate_c), axis=1)

        def scatter(r0, m):
            gate = jnp.sum(_dot(gather_matrix(r0, m), gate3), axis=1, keepdims=True)
            y = (yg_ref[pl.ds(r0, m), :] * gate).astype(BF16)
            want = (r0 + lax.broadcasted_iota(I32, (nb, m), 1)).astype(F32)
            o_ref[...] += _dot(jnp.where(slot_c == want, 1.0, 0.0).astype(BF16), y)

        for_tiles(scatter)

        if final:
            @pl.when(e == pl.num_programs(1) - 1)
            def _():
                o_ref[...] = _rms(o_ref[...], gf_ref[...])


def _moe(x, g, sel_t, sel, comb, w1, w3, w2, gf, *, final, nb, tf):
    n = x.shape[0]
    d_ff = w1.shape[2]
    sub = min(nb, MOE_TILE_ROWS)
    row = lambda i, e, j: (i, 0)
    whole = lambda i, e, j: (0, 0)
    return pl.pallas_call(
        functools.partial(_moe_kernel, final=final, nb=nb, sub=sub),
        grid=(n // nb, N_EXPERTS, d_ff // tf),
        in_specs=[pl.BlockSpec((nb, D_MODEL), row), pl.BlockSpec((1, D_MODEL), whole),
                  pl.BlockSpec((N_EXPERTS, nb), lambda i, e, j: (0, i)),
                  pl.BlockSpec((nb, N_EXPERTS), row), pl.BlockSpec((nb, N_EXPERTS), row),
                  pl.BlockSpec((1, D_MODEL, tf), lambda i, e, j: (e, 0, j)),
                  pl.BlockSpec((1, D_MODEL, tf), lambda i, e, j: (e, 0, j)),
                  pl.BlockSpec((1, tf, D_MODEL), lambda i, e, j: (e, j, 0)), pl.BlockSpec((1, D_MODEL), whole)],
        out_specs=pl.BlockSpec((nb, D_MODEL), row),
        out_shape=jax.ShapeDtypeStruct((n, D_MODEL), F32),
        scratch_shapes=[pltpu.VMEM((nb, D_MODEL), BF16), pltpu.VMEM((N_EXPERTS, nb), F32),
                        pltpu.VMEM((nb, N_EXPERTS), F32), pltpu.SMEM((N_EXPERTS,), I32),
                        pltpu.VMEM((nb + sub, D_MODEL), BF16), pltpu.VMEM((nb + sub, D_MODEL), F32)],
        compiler_params=_cparams("parallel", "arbitrary", "arbitrary"),
        name="moe",
    )(x, g, sel_t, sel, comb, w1, w3, w2, gf)


def _row_tile(n, cap):
    t = min(n, cap)
    assert n % t == 0, (n, cap)
    return t


def _ff_tile(d_ff):
    for parts in (2, 4, 11, 22):
        if d_ff % parts == 0 and (d_ff // parts) % LANES == 0:
            return d_ff // parts
    return d_ff


def _mixer(x, tabs, w, *, bsz, t_len, pos0, past, s0, layer):
    n = bsz * t_len
    qa, hb, iq, k_t, v_t, ik_t, k_hm, v_aug, ik_b, iw_t = _in_proj(x, w["norm_mix"], w["w_in"], layer, tabs,
                                                                   t_len, _row_tile(n, 512))
    l_real = t_len if past is None else past[0].shape[1] + t_len
    topk = min(TOPK_MAX, l_real // 4)
    tq = 2 * LANES if t_len % (2 * LANES) == 0 else LANES
    kb_rows = 256 if l_real % 256 == 0 else LANES
    l_pad = -(-l_real // (2 * kb_rows)) * (2 * kb_rows)
    t_pad = -(-t_len // tq) * tq
    if past is not None:
        pk, pv, pik = past
        one = jnp.ones(pv.shape[:2] + (A_KV_HEADS, 1), BF16)
        zero = jnp.zeros(pv.shape[:2] + (A_KV_HEADS, LANES - HEAD_DIM - 1), BF16)
        pv_aug = jnp.concatenate([pv.astype(BF16), one, zero], axis=-1)
        hm = lambda a: jnp.transpose(a, (2, 0, 1, 3))
        k_hm = jnp.concatenate([hm(pk.astype(BF16)), k_hm.reshape(A_KV_HEADS, bsz, t_len, HEAD_DIM)], axis=2)
        v_aug = jnp.concatenate([hm(pv_aug), v_aug.reshape(A_KV_HEADS, bsz, t_len, LANES)], axis=2)
        ik_b = jnp.concatenate([pik.astype(BF16), ik_b.reshape(bsz, t_len, HEAD_DIM)], axis=1)
    if l_pad != l_real or past is not None:
        padl = l_pad - l_real
        k_hm = jnp.pad(k_hm.reshape(A_KV_HEADS, bsz, l_real, HEAD_DIM), ((0, 0), (0, 0), (0, padl), (0, 0)))
        v_aug = jnp.pad(v_aug.reshape(A_KV_HEADS, bsz, l_real, LANES), ((0, 0), (0, 0), (0, padl), (0, 0)))
        ik_b = jnp.pad(ik_b.reshape(bsz, l_real, HEAD_DIM), ((0, 0), (0, padl), (0, 0)))
        k_hm = k_hm.reshape(A_KV_HEADS, bsz * l_pad, HEAD_DIM)
        v_aug = v_aug.reshape(A_KV_HEADS, bsz * l_pad, LANES)
        ik_b = ik_b.reshape(bsz * l_pad, HEAD_DIM)
    q_in, iq_in, iw_in = qa, iq, iw_t
    if t_pad != t_len:
        padq = lambda a: jnp.pad(a.reshape(bsz, t_len, -1),
                                 ((0, 0), (0, t_pad - t_len), (0, 0))).reshape(bsz * t_pad, -1)
        q_in, iq_in = padq(qa), padq(iq)
        iw_in = jnp.pad(iw_t.reshape(8, bsz, t_len), ((0, 0), (0, 0), (0, t_pad - t_len))).reshape(8, bsz * t_pad)
    oa = _dsa(q_in, iq_in, iw_in, k_hm, v_aug, ik_b, bsz=bsz, l_real=l_real, pos0=pos0, kb_rows=kb_rows,
              topk=topk, tq=tq)
    if t_pad != t_len:
        oa = oa.reshape(bsz, t_pad, A_WIDTH)[:, :t_len].reshape(n, A_WIDTH)
    ob, s_tiles = _hgrn(hb, w["hgrn_lb"], w["gnorm"], _state_to_tiles(s0), layer=layer, bsz=bsz, t_len=t_len,
                        tb=_row_tile(t_len, 256))
    return oa, ob, k_t, v_t, ik_t, _tiles_to_state(s_tiles)


def _from_time_minor(a, bsz, t_len, heads):
    if a.ndim == 2:
        a = a.reshape((-1, HEAD_DIM, bsz, t_len))
        a = jnp.transpose(a, (2, 3, 0, 1))
    else:
        a = jnp.moveaxis(a, -1, 1)
        a = a.reshape(bsz, t_len, -1, HEAD_DIM)
    return a if heads is not None else a.reshape(bsz, t_len, HEAD_DIM)


def kernel(x_prompt, x_sample, cache_k, cache_v, cache_idx_k, state_hgrn, cache_mem_k, cache_mem_v, mem_prompt,
           norm_mix, w_in, hgrn_lb, hgrn_gnorm, w_out, norm_mem, norm_memkv, w_mq, w_mk, w_mv, w_mo,
           norm_ffn, ffn_w1, ffn_w3, ffn_w2, moe_router, moe_w1, moe_w3, moe_w2, norm_final):
    depth = w_in.shape[0]
    bp, tp, _ = x_prompt.shape
    bs, ts, _ = x_sample.shape
    past = cache_k.shape[2]
    n_mem = mem_prompt.shape[1]
    tabs_p = _rope_tables(jnp.arange(tp, dtype=I32))
    tabs_s = tuple(jnp.tile(a, (bs, 1)) for a in _rope_tables(past + jnp.arange(ts, dtype=I32)))
    xp = x_prompt.reshape(bp * tp, D_MODEL)
    xs = x_sample.reshape(bs * ts, D_MODEL)
    mem = mem_prompt.reshape(bp * n_mem, D_MODEL)
    s0_p = jnp.zeros((bp, B_HEADS, HEAD_DIM, HEAD_DIM), F32)
    gfin = norm_final.reshape(1, D_MODEL)
    row2 = lambda a: a.reshape(1, -1)
    outs = {k: [] for k in ("kp", "vp", "ikp", "sp", "mkp", "mvp", "ks", "vs", "iks", "ss")}
    for l in range(depth):
        w = {"norm_mix": row2(norm_mix[l]), "w_in": w_in, "hgrn_lb": hgrn_lb.astype(F32),
             "gnorm": row2(jnp.tile(hgrn_gnorm[l], B_HEADS)),
             "w_out_a": w_out[l, :A_WIDTH].astype(BF16), "w_out_b": w_out[l, A_WIDTH:].astype(BF16)}
        oa_p, ob_p, kp, vp, ikp, sp = _mixer(xp, tabs_p, w, bsz=bp, t_len=tp, pos0=0, past=None, s0=s0_p, layer=l)
        oa_s, ob_s, ks, vs, iks, ss = _mixer(xs, tabs_s, w, bsz=bs, t_len=ts, pos0=past,
                                             past=(cache_k[l], cache_v[l], cache_idx_k[l]), s0=state_hgrn[l],
                                             layer=l)
        wkv = jnp.concatenate([w_mk[l], w_mv[l]], axis=1).astype(BF16)
        mkv = _norm_matmul(mem, row2(norm_memkv[l]), wkv, _row_tile(bp * n_mem, 256))
        mk_p = mkv[:, :D_MODEL].reshape(bp, n_mem, D_MODEL)
        mv_p = mkv[:, D_MODEL:].reshape(bp, n_mem, D_MODEL)
        mk_s = cache_mem_k[l].reshape(bs, n_mem, D_MODEL).astype(BF16)
        mv_s = cache_mem_v[l].reshape(bs, n_mem, D_MODEL).astype(BF16)
        wq, wo, gm = w_mq[l].astype(BF16), w_mo[l].astype(BF16), row2(norm_mem[l])
        final = l == depth - 1
        gf = row2(norm_ffn[l])
        tail = (w["w_out_a"], w["w_out_b"], gm, wq)
        if l % 2 == 0:
            dense = (gf, ffn_w1[l // 2].astype(BF16), ffn_w3[l // 2].astype(BF16), ffn_w2[l // 2].astype(BF16), gfin)
            xp = _postmix(oa_p, ob_p, xp, *tail, mk_p.astype(BF16), mv_p.astype(BF16), wo, dense,
                          rows_per_batch=tp, tm=_row_tile(tp, 512), final=final)
            xs = _postmix(oa_s, ob_s, xs, *tail, mk_s, mv_s, wo, dense,
                          rows_per_batch=ts, tm=_row_tile(bs * ts, 512), final=final)
        else:
            xp = _postmix(oa_p, ob_p, xp, *tail, mk_p.astype(BF16), mv_p.astype(BF16), wo,
                          rows_per_batch=tp, tm=_row_tile(tp, 512))
            xs = _postmix(oa_s, ob_s, xs, *tail, mk_s, mv_s, wo, rows_per_batch=ts, tm=_row_tile(bs * ts, 512))
            w1, w3, w2 = moe_w1[l // 2].astype(BF16), moe_w3[l // 2].astype(BF16), moe_w2[l // 2].astype(BF16)
            rt = moe_router[l // 2].T
            tf = _ff_tile(w1.shape[2])
            comb_p, sel_p = _router(xp, gf, rt, _row_tile(bp * tp, 512))
            comb_s, sel_s = _router(xs, gf, rt, _row_tile(bs * ts, 512))
            xp = _moe(xp, gf, sel_p, sel_p.T, comb_p.T, w1, w3, w2, gfin, final=final,
                      nb=_row_tile(bp * tp, 1024), tf=tf)
            xs = _moe(xs, gf, sel_s, sel_s.T, comb_s.T, w1, w3, w2, gfin, final=final,
                      nb=_row_tile(bs * ts, 1024), tf=tf)
        outs["kp"].append(_from_time_minor(kp, bp, tp, A_KV_HEADS))
        outs["vp"].append(_from_time_minor(vp, bp, tp, A_KV_HEADS))
        outs["ikp"].append(_from_time_minor(ikp, bp, tp, None)); outs["sp"].append(sp)
        outs["mkp"].append(mk_p.reshape(bp, n_mem, MEM_HEADS, MEM_HEAD_DIM))
        outs["mvp"].append(mv_p.reshape(bp, n_mem, MEM_HEADS, MEM_HEAD_DIM))
        outs["ks"].append(_from_time_minor(ks, bs, ts, A_KV_HEADS))
        outs["vs"].append(_from_time_minor(vs, bs, ts, A_KV_HEADS))
        outs["iks"].append(_from_time_minor(iks, bs, ts, None)); outs["ss"].append(ss)
    y_prompt = xp.reshape(bp, tp, D_MODEL)
    y_sample = xs.reshape(bs, ts, D_MODEL)
    st = lambda k: jnp.stack(outs[k])
    return (y_prompt, y_sample, st("kp"), st("vp"), st("ikp"), st("sp"), st("mkp"), st("mvp"),
            st("ks"), st("vs"), st("iks"), st("ss"))
```
